```python
import math
import jax, jax.numpy as jnp
from jax import lax
import numpy as np

D_MODEL = 1024
BATCH = 8
SEQ = 2048
DEPTH = 1

D_CONV = D_MODEL
CONV_K = 3
HEAD_DIM = 64
N_HEADS = 16
N_KV_HEADS = 4
GROUP = N_HEADS // N_KV_HEADS
D_ATTN = N_HEADS * HEAD_DIM
D_KV = N_KV_HEADS * HEAD_DIM
WINDOW = 128
BLOCK = 128
ROT_DIM = HEAD_DIM // 4
ROPE_THETA = 500000.0
ATTN_SCALE = 1.0 / math.sqrt(HEAD_DIM)
NEG_INF = -1e30
D_FF = ((8 * D_MODEL // 3 + 255) // 256) * 256
EPS = 1e-5

IN_WIDTHS = (D_CONV, D_CONV, D_CONV, D_ATTN, D_KV, D_KV, D_MODEL, D_MODEL)
N_IN = sum(IN_WIDTHS)
SPLIT_POINTS = tuple(int(v) for v in np.cumsum(IN_WIDTHS)[:-1])

kernel_name = "hybrid_gated_conv_swa_sink_block"


def rms_norm(x, g):
    xf = x.astype(jnp.float32)
    y = xf * lax.rsqrt(jnp.mean(xf * xf, axis=-1, keepdims=True) + EPS)
    return (y * g.astype(jnp.float32)).astype(x.dtype)


def rotary_tables(seq, dtype):
    inv_freq = ROPE_THETA ** (-jnp.arange(0, ROT_DIM, 2, dtype=jnp.float32) / ROT_DIM)
    ang = jnp.arange(seq, dtype=jnp.float32)[:, None] * inv_freq[None, :]
    return jnp.cos(ang).astype(dtype), jnp.sin(ang).astype(dtype)


def partial_rotary(t, cos, sin):
    rot, rest = t[..., :ROT_DIM], t[..., ROT_DIM:]
    r1, r2 = rot[..., :ROT_DIM // 2], rot[..., ROT_DIM // 2:]
    c = cos[None, :, None, :]
    s = sin[None, :, None, :]
    rot = jnp.concatenate([r1 * c - r2 * s, r2 * c + r1 * s], axis=-1)
    return jnp.concatenate([rot, rest], axis=-1)


def causal_depthwise_conv(u, w):
    rhs = w[:, None, :].astype(u.dtype)
    return lax.conv_general_dilated(
        u, rhs, window_strides=(1,), padding=[(CONV_K - 1, 0)],
        dimension_numbers=('NWC', 'WIO', 'NWC'), feature_group_count=u.shape[-1])


def sliding_window_sink_attention(q, k, v, sinks):
    b, s = q.shape[0], q.shape[1]
    nb = s // BLOCK
    qb = q.reshape(b, nb, BLOCK, N_KV_HEADS, GROUP, HEAD_DIM)

    def band(t):
        tp = jnp.pad(t, ((0, 0), (BLOCK, 0), (0, 0), (0, 0)))
        tp = tp.reshape(b, nb + 1, BLOCK, N_KV_HEADS, HEAD_DIM)
        return jnp.concatenate([tp[:, :-1], tp[:, 1:]], axis=2)

    kb, vb = band(k), band(v)
    scores = jnp.einsum('bnqhgd,bnkhd->bnhgqk', qb, kb,
                        preferred_element_type=jnp.float32) * ATTN_SCALE
    qi = jnp.arange(BLOCK)[:, None]
    kj = jnp.arange(2 * BLOCK)[None, :]
    rel = qi + BLOCK - kj
    in_window = (rel >= 0) & (rel < WINDOW)
    key_pos = jnp.arange(nb)[:, None] * BLOCK - BLOCK + jnp.arange(2 * BLOCK)[None, :]
    mask = in_window[None] & (key_pos >= 0)[:, None, :]
    scores = jnp.where(mask[None, :, None, None], scores, NEG_INF)
    sink = jnp.broadcast_to(sinks.astype(jnp.float32).reshape(1, 1, N_KV_HEADS, GROUP, 1, 1),
                            scores.shape[:-1] + (1,))
    probs = jax.nn.softmax(jnp.concatenate([scores, sink], axis=-1), axis=-1)[..., :-1]
    out = jnp.einsum('bnhgqk,bnkhd->bnqhgd', probs.astype(v.dtype), vb)
    return out.reshape(b, s, D_ATTN)


def _fwd_setup_inputs(seed: int = 0) -> dict:
    key = jax.random.key(seed)
    ks = jax.random.split(key, 13)
    f32 = jnp.float32

    def w(k, shape, fan_in):
        return jax.random.normal(k, shape, f32) * (fan_in ** -0.5)

    def gain(k, shape):
        return 1.0 + 0.05 * jax.random.normal(k, shape, f32)

    return {
        "x": jax.random.normal(ks[0], (BATCH, SEQ, D_MODEL), f32),
        "g_mix": gain(ks[1], (DEPTH, D_MODEL)),
        "w_in": w(ks[2], (DEPTH, D_MODEL, N_IN), D_MODEL),
        "conv_w": w(ks[3], (DEPTH, CONV_K, D_CONV), CONV_K),
        "attn_sinks": 0.5 * jax.random.normal(ks[4], (DEPTH, N_HEADS), f32),
        "w_conv_out": w(ks[5], (DEPTH, D_CONV, D_MODEL), D_CONV),
        "w_attn_out": w(ks[6], (DEPTH, D_ATTN, D_MODEL), D_ATTN),
        "w_o": w(ks[7], (DEPTH, D_MODEL, D_MODEL), D_MODEL),
        "g_ffn": gain(ks[8], (DEPTH, D_MODEL)),
        "w_gate_up": w(ks[9], (DEPTH, D_MODEL, 2 * D_FF), D_MODEL),
        "w_down": w(ks[10], (DEPTH, D_FF, D_MODEL), D_FF),
        "g_final": gain(ks[11], (D_MODEL,)),
    }


def _fwd_reference(x, g_mix, w_in, conv_w, attn_sinks, w_conv_out, w_attn_out, w_o,
              g_ffn, w_gate_up, w_down, g_final):
    b, s, _ = x.shape
    cos, sin = rotary_tables(s, x.dtype)
    for l in range(DEPTH):
        h = rms_norm(x, g_mix[l])
        proj = jnp.einsum('bsd,dn->bsn', h, w_in[l])
        cb, cc, cx, q, k, v, gate_c, gate_a = jnp.split(proj, SPLIT_POINTS, axis=-1)

        conv_y = cb * causal_depthwise_conv(cc * cx, conv_w[l])
        conv_out = jnp.einsum('bsc,cd->bsd', conv_y, w_conv_out[l])

        q = partial_rotary(q.reshape(b, s, N_HEADS, HEAD_DIM), cos, sin)
        k = partial_rotary(k.reshape(b, s, N_KV_HEADS, HEAD_DIM), cos, sin)
        v = v.reshape(b, s, N_KV_HEADS, HEAD_DIM)
        attn = sliding_window_sink_attention(q, k, v, attn_sinks[l])
        attn_out = jnp.einsum('bsc,cd->bsd', attn, w_attn_out[l])

        merged = jax.nn.sigmoid(gate_c) * conv_out + jax.nn.sigmoid(gate_a) * attn_out
        x = x + jnp.einsum('bsd,de->bse', merged, w_o[l])

        h = rms_norm(x, g_ffn[l])
        gu = jnp.einsum('bsd,df->bsf', h, w_gate_up[l])
        g_act, up = gu[..., :D_FF], gu[..., D_FF:]
        x = x + jnp.einsum('bsf,fd->bsd', jax.nn.silu(g_act) * up, w_down[l])
    return rms_norm(x, g_final)


import jax as _jax
import jax.numpy as _jnp

TWIN_FORMAT = 'train_step'
FWD_PARAMS = ['x', 'g_mix', 'w_in', 'conv_w', 'attn_sinks', 'w_conv_out', 'w_attn_out', 'w_o', 'g_ffn', 'w_gate_up', 'w_down', 'g_final']
TWIN_WEIGHTS = ['g_mix', 'w_in', 'conv_w', 'attn_sinks', 'w_conv_out', 'w_attn_out', 'w_o', 'g_ffn', 'w_gate_up', 'w_down', 'g_final']
TWIN_DIFF_INPUT = 'x'
TWIN_INPUTS = ['x', 'g_mix', 'w_in', 'conv_w', 'attn_sinks', 'w_conv_out', 'w_attn_out', 'w_o', 'g_ffn', 'w_gate_up', 'w_down', 'g_final', 'loss_target', 'm_g_mix', 'm_w_in', 'm_conv_w', 'm_attn_sinks', 'm_w_conv_out', 'm_w_attn_out', 'm_w_o', 'm_g_ffn', 'm_w_gate_up', 'm_w_down', 'm_g_final', 'v_g_mix', 'v_w_in', 'v_conv_w', 'v_attn_sinks', 'v_w_conv_out', 'v_w_attn_out', 'v_w_o', 'v_g_ffn', 'v_w_gate_up', 'v_w_down', 'v_g_final']
TWIN_OUTPUTS = ['loss', 'grad_x', 'grad_g_mix', 'grad_w_in', 'grad_conv_w', 'grad_attn_sinks', 'grad_w_conv_out', 'grad_w_attn_out', 'grad_w_o', 'grad_g_ffn', 'grad_w_gate_up', 'grad_w_down', 'grad_g_final', 'delta_g_mix', 'delta_w_in', 'delta_conv_w', 'delta_attn_sinks', 'delta_w_conv_out', 'delta_w_attn_out', 'delta_w_o', 'delta_g_ffn', 'delta_w_gate_up', 'delta_w_down', 'delta_g_final', 'new_m_g_mix', 'new_m_w_in', 'new_m_conv_w', 'new_m_attn_sinks', 'new_m_w_conv_out', 'new_m_w_attn_out', 'new_m_w_o', 'new_m_g_ffn', 'new_m_w_gate_up', 'new_m_w_down', 'new_m_g_final', 'new_v_g_mix', 'new_v_w_in', 'new_v_conv_w', 'new_v_attn_sinks', 'new_v_w_conv_out', 'new_v_w_attn_out', 'new_v_w_o', 'new_v_g_ffn', 'new_v_w_gate_up', 'new_v_w_down', 'new_v_g_final']
TWIN_LEAF_KINDS = {'loss': 'loss', 'grad_x': 'grad_x', 'grad_g_mix': 'grad_w', 'grad_w_in': 'grad_w', 'grad_conv_w': 'grad_w', 'grad_attn_sinks': 'grad_w', 'grad_w_conv_out': 'grad_w', 'grad_w_attn_out': 'grad_w', 'grad_w_o': 'grad_w', 'grad_g_ffn': 'grad_w', 'grad_w_gate_up': 'grad_w', 'grad_w_down': 'grad_w', 'grad_g_final': 'grad_w', 'delta_g_mix': 'delta_w', 'delta_w_in': 'delta_w', 'delta_conv_w': 'delta_w', 'delta_attn_sinks': 'delta_w', 'delta_w_conv_out': 'delta_w', 'delta_w_attn_out': 'delta_w', 'delta_w_o': 'delta_w', 'delta_g_ffn': 'delta_w', 'delta_w_gate_up': 'delta_w', 'delta_w_down': 'delta_w', 'delta_g_final': 'delta_w', 'new_m_g_mix': 'new_m', 'new_m_w_in': 'new_m', 'new_m_conv_w': 'new_m', 'new_m_attn_sinks': 'new_m', 'new_m_w_conv_out': 'new_m', 'new_m_w_attn_out': 'new_m', 'new_m_w_o': 'new_m', 'new_m_g_ffn': 'new_m', 'new_m_w_gate_up': 'new_m', 'new_m_w_down': 'new_m', 'new_m_g_final': 'new_m', 'new_v_g_mix': 'new_v', 'new_v_w_in': 'new_v', 'new_v_conv_w': 'new_v', 'new_v_attn_sinks': 'new_v', 'new_v_w_conv_out': 'new_v', 'new_v_w_attn_out': 'new_v', 'new_v_w_o': 'new_v', 'new_v_g_ffn': 'new_v', 'new_v_w_gate_up': 'new_v', 'new_v_w_down': 'new_v', 'new_v_g_final': 'new_v'}


def _forward(args):
    return _fwd_reference(*[args[k] for k in FWD_PARAMS])


def _output_shape():
    out = _jax.eval_shape(lambda: _forward(_fwd_setup_inputs(0)))
    return out.shape, out.dtype

N_MICROBATCH = 1
ADAM_LR = 0.001
ADAM_B1 = 0.9
ADAM_B2 = 0.999
ADAM_EPS = 1e-08
ADAM_WD = 0.01
ADAM_STEP = 10
PER_EXAMPLE_BATCH_AXIS = {'x': 0, 'loss_target': 0}
SHARED_INPUTS = []
_WEIGHT_DTYPES = {'g_mix': _jnp.float32, 'w_in': _jnp.float32, 'conv_w': _jnp.float32, 'attn_sinks': _jnp.float32, 'w_conv_out': _jnp.float32, 'w_attn_out': _jnp.float32, 'w_o': _jnp.float32, 'g_ffn': _jnp.float32, 'w_gate_up': _jnp.float32, 'w_down': _jnp.float32, 'g_final': _jnp.float32}
MOMENT_SCALE = {'g_mix': 1.196628e-01, 'w_in': 4.653832e-02, 'conv_w': 6.616066e-02, 'attn_sinks': 8.726469e-03, 'w_conv_out': 6.581954e-02, 'w_attn_out': 1.364345e-02, 'w_o': 6.676451e-02, 'g_ffn': 9.021972e-02, 'w_gate_up': 3.598939e-02, 'w_down': 5.912822e-02, 'g_final': 1.604199e+01}


def _to_microbatches(a, axis):
    t = _jnp.moveaxis(a, axis, 0)
    t = t.reshape((N_MICROBATCH, t.shape[0] // N_MICROBATCH) + t.shape[1:])
    return _jnp.moveaxis(t, 1, axis + 1)


def setup_inputs(seed: int = 0) -> dict:
    inp = _fwd_setup_inputs(seed)
    key = _jax.random.fold_in(_jax.random.key(seed), 7919)
    shape, _ = _output_shape()
    out = dict(inp)
    out["loss_target"] = _jax.random.normal(_jax.random.fold_in(key, 0), shape, _jnp.float32)
    for i, name in enumerate(TWIN_WEIGHTS):
        w = inp[name].astype(_jnp.float32)
        if MOMENT_SCALE is None:
            s = _jnp.sqrt(_jnp.mean(_jnp.square(w)) + 1e-30)
        else:
            s = MOMENT_SCALE[name]
        km, kv = _jax.random.split(_jax.random.fold_in(key, i + 1))
        out[name] = w
        out["m_" + name] = s * _jax.random.normal(km, w.shape, _jnp.float32)
        out["v_" + name] = (s * s) * _jax.random.uniform(kv, w.shape, _jnp.float32, 0.5, 1.5)
    if N_MICROBATCH > 1:
        for name, axis in PER_EXAMPLE_BATCH_AXIS.items():
            out[name] = _to_microbatches(out[name], axis)
    return {'x': out['x'], 'g_mix': out['g_mix'], 'w_in': out['w_in'], 'conv_w': out['conv_w'], 'attn_sinks': out['attn_sinks'], 'w_conv_out': out['w_conv_out'], 'w_attn_out': out['w_attn_out'], 'w_o': out['w_o'], 'g_ffn': out['g_ffn'], 'w_gate_up': out['w_gate_up'], 'w_down': out['w_down'], 'g_final': out['g_final'], 'loss_target': out['loss_target'], 'm_g_mix': out['m_g_mix'], 'm_w_in': out['m_w_in'], 'm_conv_w': out['m_conv_w'], 'm_attn_sinks': out['m_attn_sinks'], 'm_w_conv_out': out['m_w_conv_out'], 'm_w_attn_out': out['m_w_attn_out'], 'm_w_o': out['m_w_o'], 'm_g_ffn': out['m_g_ffn'], 'm_w_gate_up': out['m_w_gate_up'], 'm_w_down': out['m_w_down'], 'm_g_final': out['m_g_final'], 'v_g_mix': out['v_g_mix'], 'v_w_in': out['v_w_in'], 'v_conv_w': out['v_conv_w'], 'v_attn_sinks': out['v_attn_sinks'], 'v_w_conv_out': out['v_w_conv_out'], 'v_w_attn_out': out['v_w_attn_out'], 'v_w_o': out['v_w_o'], 'v_g_ffn': out['v_g_ffn'], 'v_w_gate_up': out['v_w_gate_up'], 'v_w_down': out['v_w_down'], 'v_g_final': out['v_g_final']}


def _loss(weights, diff, rest, loss_target):
    with _jax.named_scope("forward"):
        args = {**rest, TWIN_DIFF_INPUT: diff, **{k: w.astype(_WEIGHT_DTYPES[k]) for k, w in weights.items()}}
        y = _forward(args)
    with _jax.named_scope("loss_head"):
        err = _jnp.square(y.astype(_jnp.float32) - loss_target)
        return 0.5 * _jnp.sum(_jnp.mean(err, axis=-1)) if err.ndim else 0.5 * err


def _adamw(w, g, m, v):
    m = ADAM_B1 * m + (1.0 - ADAM_B1) * g
    v = ADAM_B2 * v + (1.0 - ADAM_B2) * _jnp.square(g)
    m_hat = m / (1.0 - ADAM_B1 ** ADAM_STEP)
    v_hat = v / (1.0 - ADAM_B2 ** ADAM_STEP)
    delta = -ADAM_LR * (m_hat / (_jnp.sqrt(v_hat) + ADAM_EPS) + ADAM_WD * w)
    return delta, m, v


def reference(x, g_mix, w_in, conv_w, attn_sinks, w_conv_out, w_attn_out, w_o, g_ffn, w_gate_up, w_down, g_final, loss_target, m_g_mix, m_w_in, m_conv_w, m_attn_sinks, m_w_conv_out, m_w_attn_out, m_w_o, m_g_ffn, m_w_gate_up, m_w_down, m_g_final, v_g_mix, v_w_in, v_conv_w, v_attn_sinks, v_w_conv_out, v_w_attn_out, v_w_o, v_g_ffn, v_w_gate_up, v_w_down, v_g_final):
    given = dict(x=x, g_mix=g_mix, w_in=w_in, conv_w=conv_w, attn_sinks=attn_sinks, w_conv_out=w_conv_out, w_attn_out=w_attn_out, w_o=w_o, g_ffn=g_ffn, w_gate_up=w_gate_up, w_down=w_down, g_final=g_final, loss_target=loss_target, m_g_mix=m_g_mix, m_w_in=m_w_in, m_conv_w=m_conv_w, m_attn_sinks=m_attn_sinks, m_w_conv_out=m_w_conv_out, m_w_attn_out=m_w_attn_out, m_w_o=m_w_o, m_g_ffn=m_g_ffn, m_w_gate_up=m_w_gate_up, m_w_down=m_w_down, m_g_final=m_g_final, v_g_mix=v_g_mix, v_w_in=v_w_in, v_conv_w=v_conv_w, v_attn_sinks=v_attn_sinks, v_w_conv_out=v_w_conv_out, v_w_attn_out=v_w_attn_out, v_w_o=v_w_o, v_g_ffn=v_g_ffn, v_w_gate_up=v_w_gate_up, v_w_down=v_w_down, v_g_final=v_g_final)
    weights = {n: given[n] for n in TWIN_WEIGHTS}
    shared = {n: given[n] for n in SHARED_INPUTS}
    per_example = {n: given[n] for n in ['x']}
    grad_fn = _jax.value_and_grad(_loss, argnums=(0, 1))

    def one_microbatch(ex, loss_target):
        ex = dict(ex)
        diff = ex.pop(TWIN_DIFF_INPUT)
        return grad_fn(weights, diff, {**shared, **ex}, loss_target)

    if N_MICROBATCH == 1:
        loss, (grad_w, grad_x) = one_microbatch(per_example, given["loss_target"])
    else:
        def body(carry, xs):
            loss_sum, grad_sum = carry
            l_k, (gw_k, gx_k) = one_microbatch(xs[0], xs[1])
            with _jax.named_scope("update"):
                return (loss_sum + l_k, _jax.tree.map(_jnp.add, grad_sum, gw_k)), gx_k

        init = (_jnp.zeros((), _jnp.float32), _jax.tree.map(_jnp.zeros_like, weights))
        (loss, grad_w), grad_x = _jax.lax.scan(body, init, (per_example, given["loss_target"]))
    with _jax.named_scope("update"):
        delta_w, new_m, new_v = {}, {}, {}
        for n in TWIN_WEIGHTS:
            delta_w[n], new_m[n], new_v[n] = _adamw(weights[n], grad_w[n], given["m_" + n], given["v_" + n])
    return (loss, grad_x, *[grad_w[n] for n in TWIN_WEIGHTS], *[delta_w[n] for n in TWIN_WEIGHTS],
            *[new_m[n] for n in TWIN_WEIGHTS], *[new_v[n] for n in TWIN_WEIGHTS])
```

```python
import functools
import math

import jax
import jax.numpy as jnp
from jax import lax
from jax.experimental import pallas as pl
from jax.experimental.pallas import tpu as pltpu

F32 = jnp.float32
BF16 = jnp.bfloat16

D_MODEL = 1024
SEQ = 2048
HEAD_DIM = 64
N_HEADS = 16
N_KV_HEADS = 4
GROUP = N_HEADS // N_KV_HEADS
D_ATTN = N_HEADS * HEAD_DIM
D_KV = N_KV_HEADS * HEAD_DIM
BLOCK = 128
ROT_DIM = HEAD_DIM // 4
ROPE_THETA = 500000.0
ATTN_SCALE = 1.0 / math.sqrt(HEAD_DIM)
NEG_INF = -1e30
D_FF = 2816
EPS = 1e-5
N_IN = 3 * D_MODEL + D_ATTN + 2 * D_KV + 2 * D_MODEL
COL_Q = 3 * D_MODEL
COL_K = COL_Q + D_ATTN
COL_V = COL_K + D_KV
COL_GC = COL_V + D_KV
COL_GA = COL_GC + D_MODEL

ADAM_LR = 0.001
ADAM_B1 = 0.9
ADAM_B2 = 0.999
ADAM_EPS = 1e-08
ADAM_WD = 0.01
ADAM_STEP = 10

N_CHIPS = 4
N_DEV = 8

V7X_VMEM_BYTES = 64 * 1024 * 1024
VMEM_LIMIT = (V7X_VMEM_BYTES * 3) // 4
LANES = 128
MESH = pl.DeviceIdType.MESH


def _params(semantics=None):
    return pltpu.CompilerParams(dimension_semantics=semantics, vmem_limit_bytes=VMEM_LIMIT)


def _sds(shape, dtype):
    return jax.ShapeDtypeStruct(shape, dtype)


NN = ((1,), (0,))
NT = ((1,), (1,))
TN = ((0,), (0,))


def _matmul(name, a, b, dims, grid, a_spec, b_spec, o_spec, o_shape, o_dtype, res=None, res_spec=None):
    nk = grid[2]

    def body(*refs):
        if res is None:
            a_ref, b_ref, o_ref = refs[:3]
            r_ref = None
            scratch = refs[3:]
        else:
            a_ref, b_ref, r_ref, o_ref = refs[:4]
            scratch = refs[4:]
        p = lax.dot_general(a_ref[...], b_ref[...], (dims, ((), ())), preferred_element_type=F32)

        def finish(acc):
            if r_ref is not None:
                acc = r_ref[...] + acc
            o_ref[...] = acc.astype(o_dtype)

        if nk == 1:
            finish(p)
        else:
            acc_ref = scratch[0]
            k = pl.program_id(2)

            @pl.when(k == 0)
            def _():
                acc_ref[...] = p

            @pl.when(k > 0)
            def _():
                acc_ref[...] += p

            @pl.when(k == nk - 1)
            def _():
                finish(acc_ref[...])

    operands = [a, b] if res is None else [a, b, res]
    in_specs = [a_spec, b_spec] if res is None else [a_spec, b_spec, res_spec]
    scratch = [pltpu.VMEM(o_spec.block_shape, F32)] if nk > 1 else []
    return pl.pallas_call(
        body,
        name=name,
        grid=grid,
        in_specs=in_specs,
        out_specs=o_spec,
        out_shape=_sds(o_shape, o_dtype),
        scratch_shapes=scratch,
        compiler_params=_params(("parallel", "parallel", "arbitrary")),
    )(*operands)


def _mm_nn(name, a, b, bm, bn, o_dtype, res=None):
    m, k = a.shape
    n = b.shape[1]
    return _matmul(
        name, a, b, NN, (m // bm, n // bn, 1),
        pl.BlockSpec((bm, k), lambda i, j, kk: (i, 0)),
        pl.BlockSpec((k, bn), lambda i, j, kk: (0, j)),
        pl.BlockSpec((bm, bn), lambda i, j, kk: (i, j)),
        (m, n), o_dtype, res,
        None if res is None else pl.BlockSpec((bm, bn), lambda i, j, kk: (i, j)),
    )


def _mm_nt(name, a, b, bm, bn, bk, o_dtype):
    m, k = a.shape
    n = b.shape[0]
    return _matmul(
        name, a, b, NT, (m // bm, n // bn, k // bk),
        pl.BlockSpec((bm, bk), lambda i, j, kk: (i, kk)),
        pl.BlockSpec((bn, bk), lambda i, j, kk: (j, kk)),
        pl.BlockSpec((bm, bn), lambda i, j, kk: (i, j)),
        (m, n), o_dtype,
    )


def _mm_tn(name, a, b, bm, bn, o_dtype):
    k, m = a.shape
    n = b.shape[1]
    return _matmul(
        name, a, b, TN, (m // bm, n // bn, 1),
        pl.BlockSpec((k, bm), lambda i, j, kk: (0, i)),
        pl.BlockSpec((k, bn), lambda i, j, kk: (0, j)),
        pl.BlockSpec((bm, bn), lambda i, j, kk: (i, j)),
        (m, n), o_dtype,
    )


ROWS = 256


def _row_spec(width, col=0):
    return pl.BlockSpec((ROWS, width), lambda i: (i, col))


def _full_spec(shape):
    return pl.BlockSpec(shape, lambda *_: (0,) * len(shape))


def _rms_norm(name, x, g):
    def body(x_ref, g_ref, h_ref):
        xf = x_ref[...]
        r = lax.rsqrt(jnp.mean(xf * xf, axis=-1, keepdims=True) + EPS)
        h_ref[...] = ((xf * r) * g_ref[...]).astype(BF16)

    return pl.pallas_call(
        body, name=name, grid=(SEQ // ROWS,),
        in_specs=[_row_spec(D_MODEL), _full_spec((1, D_MODEL))],
        out_specs=_row_spec(D_MODEL),
        out_shape=_sds((SEQ, D_MODEL), BF16),
        compiler_params=_params(("parallel",)),
    )(x, g)


CONV_COLS = 256


def _shift_rows(u, k):
    rows = lax.broadcasted_iota(jnp.int32, u.shape, 0)
    return jnp.where(rows >= k, pltpu.roll(u, k, axis=0), 0.0)


def _conv_fwd(proj, conv_w):
    nblk = D_MODEL // CONV_COLS

    def body(cb_ref, cc_ref, cx_ref, w_ref, y_ref):
        u = cc_ref[...] * cx_ref[...]
        w = w_ref[...]
        cv = w[0:1, :] * _shift_rows(u, 2) + w[1:2, :] * _shift_rows(u, 1) + w[2:3, :] * u
        y_ref[...] = (cb_ref[...] * cv).astype(BF16)

    def col(part):
        return pl.BlockSpec((SEQ, CONV_COLS), lambda j: (0, part * nblk + j))

    return pl.pallas_call(
        body, name="conv_fwd", grid=(nblk,),
        in_specs=[col(0), col(1), col(2), pl.BlockSpec((3, CONV_COLS), lambda j: (0, j))],
        out_specs=pl.BlockSpec((SEQ, CONV_COLS), lambda j: (0, j)),
        out_shape=_sds((SEQ, D_MODEL), BF16),
        compiler_params=_params(("parallel",)),
    )(proj, proj, proj, conv_w)


ROPE_COLS = 256


def _rope_tables():
    inv_freq = ROPE_THETA ** (-jnp.arange(0, ROT_DIM, 2, dtype=F32) / ROT_DIM)
    ang = jnp.arange(SEQ, dtype=F32)[:, None] * inv_freq[None, :]
    cos, sin = jnp.cos(ang), jnp.sin(ang)
    half = ROT_DIM // 2
    ones = jnp.ones((SEQ, HEAD_DIM - ROT_DIM), F32)
    zeros = jnp.zeros((SEQ, HEAD_DIM - ROT_DIM), F32)
    zh = jnp.zeros((SEQ, half), F32)
    c = jnp.concatenate([cos, cos, ones], axis=1)
    s_up = jnp.concatenate([-sin, zh, zeros], axis=1)
    s_dn = jnp.concatenate([zh, sin, zeros], axis=1)
    reps = ROPE_COLS // HEAD_DIM
    return tuple(jnp.tile(t, (1, reps)) for t in (c, s_up, s_dn))


def _rotate(t, c, s_up, s_dn):
    width = t.shape[1]
    half = ROT_DIM // 2
    return t * c + pltpu.roll(t, width - half, axis=1) * s_up + pltpu.roll(t, half, axis=1) * s_dn


def _rope_fwd(proj, tables):
    ntile = (D_ATTN + 2 * D_KV) // ROPE_COLS
    first = COL_Q // ROPE_COLS

    def body(t_ref, c_ref, su_ref, sd_ref, o_ref):
        j = pl.program_id(1)
        t = t_ref[...]
        rot = _rotate(t, c_ref[...], su_ref[...], sd_ref[...])
        o_ref[...] = jnp.where(j < ntile - 1, rot, t).astype(BF16)

    tab = pl.BlockSpec((ROWS, ROPE_COLS), lambda i, j: (i, 0))
    return pl.pallas_call(
        body, name="rope_fwd", grid=(SEQ // ROWS, ntile),
        in_specs=[pl.BlockSpec((ROWS, ROPE_COLS), lambda i, j: (i, first + j)), tab, tab, tab],
        out_specs=pl.BlockSpec((ROWS, ROPE_COLS), lambda i, j: (i, j)),
        out_shape=_sds((SEQ, D_ATTN + 2 * D_KV), BF16),
        compiler_params=_params(("parallel", "parallel")),
    )(proj, *tables)


N_QBLK = SEQ // BLOCK
KV_TILE = D_ATTN // D_KV


def _attn_specs():
    q = pl.BlockSpec((BLOCK, D_ATTN), lambda n: (n, 0))
    k_prev = pl.BlockSpec((BLOCK, D_KV), lambda n: (jnp.maximum(n - 1, 0), KV_TILE))
    k_cur = pl.BlockSpec((BLOCK, D_KV), lambda n: (n, KV_TILE))
    v_prev = pl.BlockSpec((BLOCK, D_KV), lambda n: (jnp.maximum(n - 1, 0), KV_TILE + 1))
    v_cur = pl.BlockSpec((BLOCK, D_KV), lambda n: (n, KV_TILE + 1))
    return [q, k_prev, k_cur, v_prev, v_cur]


def _band_mask(n):
    qi = lax.broadcasted_iota(jnp.int32, (GROUP * BLOCK, 2 * BLOCK), 0) % BLOCK
    kj = lax.broadcasted_iota(jnp.int32, (GROUP * BLOCK, 2 * BLOCK), 1)
    rel = qi + BLOCK - kj
    return (rel >= 0) & (rel < BLOCK) & ((kj >= BLOCK) | (n > 0))


def _head_cols(ref_or_val, h, width=HEAD_DIM):
    return ref_or_val[:, h * width:(h + 1) * width]


def _group_rows(x, h):
    return jnp.concatenate([_head_cols(x, h * GROUP + g) for g in range(GROUP)], axis=0)


def _sink_col(sink_ref, h):
    return jnp.concatenate(
        [jnp.full((BLOCK, 1), sink_ref[0, h * GROUP + g], F32) for g in range(GROUP)], axis=0)


def _softmax_with_sink(s, sink):
    m = jnp.maximum(jnp.max(s, axis=-1, keepdims=True), sink)
    e = jnp.exp(s - m)
    es = jnp.exp(sink - m)
    z = jnp.sum(e, axis=-1, keepdims=True) + es
    return e / z, es / z


def _attn_fwd(qkv, sinks):
    def body(sink_ref, q_ref, kp_ref, kc_ref, vp_ref, vc_ref, o_ref):
        n = pl.program_id(0)
        mask = _band_mask(n)
        q = q_ref[...]
        k = jnp.concatenate([kp_ref[...], kc_ref[...]], axis=0)
        v = jnp.concatenate([vp_ref[...], vc_ref[...]], axis=0)
        outs = []
        for h in range(N_KV_HEADS):
            s = lax.dot_general(_group_rows(q, h), _head_cols(k, h), (NT, ((), ())),
                                preferred_element_type=F32) * ATTN_SCALE
            s = jnp.where(mask, s, NEG_INF)
            p, _ = _softmax_with_sink(s, _sink_col(sink_ref, h))
            o = jnp.dot(p.astype(BF16), _head_cols(v, h), preferred_element_type=F32)
            outs += [o[g * BLOCK:(g + 1) * BLOCK, :] for g in range(GROUP)]
        o_ref[...] = jnp.concatenate(outs, axis=1).astype(BF16)

    return pl.pallas_call(
        body, name="attn_fwd", grid=(N_QBLK,),
        in_specs=[pl.BlockSpec(memory_space=pltpu.SMEM)] + _attn_specs(),
        out_specs=pl.BlockSpec((BLOCK, D_ATTN), lambda n: (n, 0)),
        out_shape=_sds((SEQ, D_ATTN), BF16),
        compiler_params=_params(("parallel",)),
    )(sinks, qkv, qkv, qkv, qkv, qkv)


def _branch_merge(conv_y, attn, w_co, w_ao, proj):
    bm, bn = 1024, 512

    def body(cy_ref, at_ref, wc_ref, wa_ref, gc_ref, ga_ref, co_ref, ao_ref, mg_ref):
        co = jnp.dot(cy_ref[...], wc_ref[...], preferred_element_type=F32)
        ao = jnp.dot(at_ref[...], wa_ref[...], preferred_element_type=F32)
        co_ref[...] = co
        ao_ref[...] = ao
        mg_ref[...] = (jax.nn.sigmoid(gc_ref[...]) * co + jax.nn.sigmoid(ga_ref[...]) * ao).astype(BF16)

    act = pl.BlockSpec((bm, D_MODEL), lambda i, j: (i, 0))
    wgt = pl.BlockSpec((D_MODEL, bn), lambda i, j: (0, j))
    out = pl.BlockSpec((bm, bn), lambda i, j: (i, j))
    return pl.pallas_call(
        body, name="branch_merge", grid=(SEQ // bm, D_MODEL // bn),
        in_specs=[act, act, wgt, wgt,
                  pl.BlockSpec((bm, bn), lambda i, j: (i, COL_GC // bn + j)),
                  pl.BlockSpec((bm, bn), lambda i, j: (i, COL_GA // bn + j))],
        out_specs=[out, out, out],
        out_shape=[_sds((SEQ, D_MODEL), F32), _sds((SEQ, D_MODEL), F32), _sds((SEQ, D_MODEL), BF16)],
        compiler_params=_params(("parallel", "parallel")),
    )(conv_y, attn, w_co, w_ao, proj, proj)


FF_ROWS = 128


def _swiglu_fwd(gu):
    def body(gu_ref, act_ref):
        g = gu_ref[:, :D_FF]
        act_ref[...] = (jax.nn.silu(g) * gu_ref[:, D_FF:]).astype(BF16)

    return pl.pallas_call(
        body, name="swiglu_fwd", grid=(SEQ // FF_ROWS,),
        in_specs=[pl.BlockSpec((FF_ROWS, 2 * D_FF), lambda i: (i, 0))],
        out_specs=pl.BlockSpec((FF_ROWS, D_FF), lambda i: (i, 0)),
        out_shape=_sds((SEQ, D_FF), BF16),
        compiler_params=_params(("parallel",)),
    )(gu)


def _loss_head(x3, g, target):
    def body(x_ref, g_ref, t_ref, dx_ref, dxb_ref, dg_ref, loss_ref):
        i = pl.program_id(0)
        xf = x_ref[...]
        r = lax.rsqrt(jnp.mean(xf * xf, axis=-1, keepdims=True) + EPS)
        xn = xf * r
        gg = g_ref[...]
        err = xn * gg - t_ref[...]
        part = 0.5 * jnp.sum(jnp.mean(err * err, axis=-1, keepdims=True), axis=0, keepdims=True)
        dy = err * (1.0 / D_MODEL)
        dxn = dy * gg
        dx = r * (dxn - xn * jnp.mean(dxn * xn, axis=-1, keepdims=True))
        dx_ref[...] = dx
        dxb_ref[...] = dx.astype(BF16)
        dg = jnp.sum(dy * xn, axis=0, keepdims=True)
        lane0 = lax.broadcasted_iota(jnp.int32, (1, LANES), 1) == 0
        lpart = jnp.where(lane0, part, 0.0)

        @pl.when(i == 0)
        def _():
            dg_ref[...] = dg
            loss_ref[...] = lpart

        @pl.when(i > 0)
        def _():
            dg_ref[...] += dg
            loss_ref[...] += lpart

    return pl.pallas_call(
        body, name="loss_head", grid=(SEQ // ROWS,),
        in_specs=[_row_spec(D_MODEL), _full_spec((1, D_MODEL)), _row_spec(D_MODEL)],
        out_specs=[_row_spec(D_MODEL), _row_spec(D_MODEL), _full_spec((1, D_MODEL)), _full_spec((1, LANES))],
        out_shape=[_sds((SEQ, D_MODEL), F32), _sds((SEQ, D_MODEL), BF16),
                   _sds((1, D_MODEL), F32), _sds((1, LANES), F32)],
        compiler_params=_params(("arbitrary",)),
    )(x3, g, target)


def _swiglu_bwd(dact, gu):
    def body(da_ref, gu_ref, o_ref):
        g = gu_ref[:, :D_FF]
        up = gu_ref[:, D_FF:]
        da = da_ref[...]
        sg = jax.nn.sigmoid(g)
        o_ref[:, :D_FF] = (da * up * (sg * (1.0 + g * (1.0 - sg)))).astype(BF16)
        o_ref[:, D_FF:] = (da * (g * sg)).astype(BF16)

    return pl.pallas_call(
        body, name="swiglu_bwd", grid=(SEQ // FF_ROWS,),
        in_specs=[pl.BlockSpec((FF_ROWS, D_FF), lambda i: (i, 0)),
                  pl.BlockSpec((FF_ROWS, 2 * D_FF), lambda i: (i, 0))],
        out_specs=pl.BlockSpec((FF_ROWS, 2 * D_FF), lambda i: (i, 0)),
        out_shape=_sds((SEQ, 2 * D_FF), BF16),
        compiler_params=_params(("parallel",)),
    )(dact, gu)


def _rms_norm_bwd(name, dh, x, g, dres, with_bf16):
    def body(dh_ref, x_ref, g_ref, dr_ref, *outs):
        i = pl.program_id(0)
        dx_ref = outs[0]
        dg_ref = outs[-1]
        xf = x_ref[...]
        r = lax.rsqrt(jnp.mean(xf * xf, axis=-1, keepdims=True) + EPS)
        xn = xf * r
        dh = dh_ref[...]
        dxn = dh * g_ref[...]
        dx = dr_ref[...] + r * (dxn - xn * jnp.mean(dxn * xn, axis=-1, keepdims=True))
        dx_ref[...] = dx
        if with_bf16:
            outs[1][...] = dx.astype(BF16)
        dg = jnp.sum(dh * xn, axis=0, keepdims=True)

        @pl.when(i == 0)
        def _():
            dg_ref[...] = dg

        @pl.when(i > 0)
        def _():
            dg_ref[...] += dg

    row = _row_spec(D_MODEL)
    out_specs = [row] + ([row] if with_bf16 else []) + [_full_spec((1, D_MODEL))]
    out_shape = ([_sds((SEQ, D_MODEL), F32)] + ([_sds((SEQ, D_MODEL), BF16)] if with_bf16 else [])
                 + [_sds((1, D_MODEL), F32)])
    return pl.pallas_call(
        body, name=name, grid=(SEQ // ROWS,),
        in_specs=[row, row, _full_spec((1, D_MODEL)), row],
        out_specs=out_specs, out_shape=out_shape,
        compiler_params=_params(("arbitrary",)),
    )(dh, x, g, dres)


def _merge_bwd(dmerged, conv_out, attn_out, proj):
    def body(dm_ref, co_ref, ao_ref, gc_ref, ga_ref, dco_ref, dao_ref, dgc_ref, dga_ref):
        dm = dm_ref[...]
        sc = jax.nn.sigmoid(gc_ref[...])
        sa = jax.nn.sigmoid(ga_ref[...])
        dco_ref[...] = (dm * sc).astype(BF16)
        dao_ref[...] = (dm * sa).astype(BF16)
        dgc_ref[...] = (dm * co_ref[...] * (sc * (1.0 - sc))).astype(BF16)
        dga_ref[...] = (dm * ao_ref[...] * (sa * (1.0 - sa))).astype(BF16)

    half = D_MODEL // 2
    own = pl.BlockSpec((ROWS, half), lambda i, j: (i, j))
    sd = _sds((SEQ, D_MODEL), BF16)
    return pl.pallas_call(
        body, name="merge_bwd", grid=(SEQ // ROWS, 2),
        in_specs=[own, own, own,
                  pl.BlockSpec((ROWS, half), lambda i, j: (i, COL_GC // half + j)),
                  pl.BlockSpec((ROWS, half), lambda i, j: (i, COL_GA // half + j))],
        out_specs=[own, own, own, own], out_shape=[sd, sd, sd, sd],
        compiler_params=_params(("parallel", "parallel")),
    )(dmerged, conv_out, attn_out, proj, proj)


def _conv_bwd(dconv_y, proj, conv_w):
    nblk = D_MODEL // CONV_COLS

    def body(dy_ref, cb_ref, cc_ref, cx_ref, w_ref, dcb_ref, dcc_ref, dcx_ref, dw_ref):
        cc = cc_ref[...]
        cx = cx_ref[...]
        u = cc * cx
        w = w_ref[...]
        u1 = _shift_rows(u, 1)
        u2 = _shift_rows(u, 2)
        cv = w[0:1, :] * u2 + w[1:2, :] * u1 + w[2:3, :] * u
        dy = dy_ref[...]
        dcb_ref[...] = (dy * cv).astype(BF16)
        dcv = dy * cb_ref[...]
        rows = lax.broadcasted_iota(jnp.int32, dcv.shape, 0)
        up1 = jnp.where(rows < SEQ - 1, pltpu.roll(dcv, SEQ - 1, axis=0), 0.0)
        up2 = jnp.where(rows < SEQ - 2, pltpu.roll(dcv, SEQ - 2, axis=0), 0.0)
        du = w[2:3, :] * dcv + w[1:2, :] * up1 + w[0:1, :] * up2
        dcc_ref[...] = (du * cx).astype(BF16)
        dcx_ref[...] = (du * cc).astype(BF16)
        dw_ref[...] = jnp.concatenate(
            [jnp.sum(dcv * u2, axis=0, keepdims=True),
             jnp.sum(dcv * u1, axis=0, keepdims=True),
             jnp.sum(dcv * u, axis=0, keepdims=True)], axis=0)

    def col(part):
        return pl.BlockSpec((SEQ, CONV_COLS), lambda j: (0, part * nblk + j))

    own = pl.BlockSpec((SEQ, CONV_COLS), lambda j: (0, j))
    wsp = pl.BlockSpec((3, CONV_COLS), lambda j: (0, j))
    sd = _sds((SEQ, D_MODEL), BF16)
    return pl.pallas_call(
        body, name="conv_bwd", grid=(nblk,),
        in_specs=[own, col(0), col(1), col(2), wsp],
        out_specs=[own, own, own, wsp],
        out_shape=[sd, sd, sd, _sds((3, D_MODEL), F32)],
        compiler_params=_params(("parallel",)),
    )(dconv_y, proj, proj, proj, conv_w)


def _attn_bwd(qkv, dattn, sinks, tables):
    c_t, su_t, sd_t = tables

    def body(sink_ref, q_ref, kp_ref, kc_ref, vp_ref, vc_ref, do_ref, c_ref, su_ref, sd_ref,
             dq_ref, dkp_ref, dkc_ref, dvp_ref, dvc_ref, ds_ref):
        n = pl.program_id(0)
        mask = _band_mask(n)
        q = q_ref[...]
        do = do_ref[...]
        k = jnp.concatenate([kp_ref[...], kc_ref[...]], axis=0)
        v = jnp.concatenate([vp_ref[...], vc_ref[...]], axis=0)
        lane = lax.broadcasted_iota(jnp.int32, (1, LANES), 1)
        dsink = jnp.zeros((1, LANES), F32)
        dqs, dks, dvs = [], [], []
        for h in range(N_KV_HEADS):
            qg = _group_rows(q, h)
            dog = _group_rows(do, h)
            kh = _head_cols(k, h)
            vh = _head_cols(v, h)
            s = lax.dot_general(qg, kh, (NT, ((), ())), preferred_element_type=F32) * ATTN_SCALE
            s = jnp.where(mask, s, NEG_INF)
            p, p_sink = _softmax_with_sink(s, _sink_col(sink_ref, h))
            dp = lax.dot_general(dog, vh, (NT, ((), ())), preferred_element_type=F32)
            delta = jnp.sum(p * dp, axis=-1, keepdims=True)
            ds = (p * (dp - delta) * ATTN_SCALE).astype(BF16)
            dqg = jnp.dot(ds, kh, preferred_element_type=F32)
            dqs += [dqg[g * BLOCK:(g + 1) * BLOCK, :] for g in range(GROUP)]
            dks.append(lax.dot_general(ds, qg, (TN, ((), ())), preferred_element_type=F32))
            dvs.append(lax.dot_general(p.astype(BF16), dog, (TN, ((), ())), preferred_element_type=F32))
            dsk = -(p_sink * delta)
            for g in range(GROUP):
                val = jnp.sum(dsk[g * BLOCK:(g + 1) * BLOCK, :], axis=0, keepdims=True)
                dsink = dsink + jnp.where(lane == h * GROUP + g, val, 0.0)
        c, su, sd = c_ref[...], su_ref[...], sd_ref[...]
        for t in range(D_ATTN // ROPE_COLS):
            dq_t = jnp.concatenate(dqs[t * GROUP:(t + 1) * GROUP], axis=1)
            dq_ref[:, t * ROPE_COLS:(t + 1) * ROPE_COLS] = _rotate(dq_t, c, -su, -sd).astype(BF16)
        dk = jnp.concatenate(dks, axis=1)
        dv = jnp.concatenate(dvs, axis=1)
        dkp_ref[...] = dk[:BLOCK, :]
        dkc_ref[...] = dk[BLOCK:, :]
        dvp_ref[...] = dv[:BLOCK, :]
        dvc_ref[...] = dv[BLOCK:, :]

        @pl.when(n == 0)
        def _():
            ds_ref[...] = dsink

        @pl.when(n > 0)
        def _():
            ds_ref[...] += dsink

    blk = pl.BlockSpec((BLOCK, D_KV), lambda n: (n, 0))
    tab = pl.BlockSpec((BLOCK, ROPE_COLS), lambda n: (n, 0))
    kv = _sds((SEQ, D_KV), F32)
    return pl.pallas_call(
        body, name="attn_bwd", grid=(N_QBLK,),
        in_specs=[pl.BlockSpec(memory_space=pltpu.SMEM)] + _attn_specs()
        + [pl.BlockSpec((BLOCK, D_ATTN), lambda n: (n, 0)), tab, tab, tab],
        out_specs=[pl.BlockSpec((BLOCK, D_ATTN), lambda n: (n, 0)), blk, blk, blk, blk,
                   _full_spec((1, LANES))],
        out_shape=[_sds((SEQ, D_ATTN), BF16), kv, kv, kv, kv, _sds((1, LANES), F32)],
        compiler_params=_params(("arbitrary",)),
    )(sinks, qkv, qkv, qkv, qkv, qkv, dattn, c_t, su_t, sd_t)


def _kv_grad_combine(dk_prev, dk_cur, dv_prev, dv_cur, tables):
    def body(kp_ref, kc_ref, vp_ref, vc_ref, c_ref, su_ref, sd_ref, o_ref):
        m = pl.program_id(0)
        has_next = m < N_QBLK - 1
        dk = kc_ref[...] + jnp.where(has_next, kp_ref[...], 0.0)
        dv = vc_ref[...] + jnp.where(has_next, vp_ref[...], 0.0)
        o_ref[:, :D_KV] = _rotate(dk, c_ref[...], -su_ref[...], -sd_ref[...]).astype(BF16)
        o_ref[:, D_KV:] = dv.astype(BF16)

    cur = pl.BlockSpec((BLOCK, D_KV), lambda m: (m, 0))
    nxt = pl.BlockSpec((BLOCK, D_KV), lambda m: (jnp.minimum(m + 1, N_QBLK - 1), 0))
    return pl.pallas_call(
        body, name="kv_grad_combine", grid=(N_QBLK,),
        in_specs=[nxt, cur, nxt, cur, cur, cur, cur],
        out_specs=pl.BlockSpec((BLOCK, 2 * D_KV), lambda m: (m, 0)),
        out_shape=_sds((SEQ, 2 * D_KV), BF16),
        compiler_params=_params(("parallel",)),
    )(dk_prev, dk_cur, dv_prev, dv_cur, *tables)


def _local_step(x, target, g_mix, g_ffn, g_final, sinks, conv_w, w_in, w_co, w_ao, w_o, w_gu, w_down):
    tables = _rope_tables()
    h1 = _rms_norm("norm_mix", x, g_mix)
    proj = _mm_nn("mm_in", h1, w_in, 1024, 1664, F32)
    conv_y = _conv_fwd(proj, conv_w)
    qkv = _rope_fwd(proj, tables)
    attn = _attn_fwd(qkv, sinks)
    conv_out, attn_out, merged = _branch_merge(conv_y, attn, w_co, w_ao, proj)
    x2 = _mm_nn("mm_o", merged, w_o, 1024, 1024, F32, res=x)
    h2 = _rms_norm("norm_ffn", x2, g_ffn)
    gu = _mm_nn("mm_gate_up", h2, w_gu, 1024, 1408, F32)
    act = _swiglu_fwd(gu)
    x3 = _mm_nn("mm_down", act, w_down, 1024, 512, F32, res=x2)
    dx3, dx3b, dg_final, loss_row = _loss_head(x3, g_final, target)
    dact = _mm_nt("mm_dact", dx3b, w_down, 1024, 1408, D_MODEL, F32)
    dw_down = _mm_tn("mm_dw_down", act, dx3b, 1408, 1024, BF16)
    dgu = _swiglu_bwd(dact, gu)
    dh2 = _mm_nt("mm_dh2", dgu, w_gu, 1024, 1024, 1408, F32)
    dw_gu = _mm_tn("mm_dw_gate_up", h2, dgu, 1024, 1408, BF16)
    dx2, dx2b, dg_ffn = _rms_norm_bwd("norm_ffn_bwd", dh2, x2, g_ffn, dx3, True)
    dmerged = _mm_nt("mm_dmerged", dx2b, w_o, 1024, 1024, D_MODEL, F32)
    dw_o = _mm_tn("mm_dw_o", merged, dx2b, 1024, 1024, BF16)
    dco, dao, dgc, dga = _merge_bwd(dmerged, conv_out, attn_out, proj)
    dconv_y = _mm_nt("mm_dconv_y", dco, w_co, 1024, 1024, D_MODEL, F32)
    dw_co = _mm_tn("mm_dw_conv_out", conv_y, dco, 1024, 1024, BF16)
    dattn = _mm_nt("mm_dattn", dao, w_ao, 1024, 1024, D_MODEL, BF16)
    dw_ao = _mm_tn("mm_dw_attn_out", attn, dao, 1024, 1024, BF16)
    dcb, dcc, dcx, dconv_w = _conv_bwd(dconv_y, proj, conv_w)
    dq, dk_prev, dk_cur, dv_prev, dv_cur, dsinks = _attn_bwd(qkv, dattn, sinks, tables)
    dkv = _kv_grad_combine(dk_prev, dk_cur, dv_prev, dv_cur, tables)
    dproj = jnp.concatenate([dcb, dcc, dcx, dq, dkv, dgc, dga], axis=1)
    dh1 = _mm_nt("mm_dh1", dproj, w_in, 1024, 1024, 1664, F32)
    dw_in = _mm_tn("mm_dw_in", h1, dproj, 1024, 1664, BF16)
    grad_x, dg_mix = _rms_norm_bwd("norm_mix_bwd", dh1, x, g_mix, dx2, False)
    small = (dg_mix, dg_ffn, dg_final, dconv_w, dsinks, loss_row)
    return grad_x, small, (dw_in, dw_co, dw_ao, dw_o, dw_gu, dw_down)


MATRICES = (
    (D_MODEL, N_IN // N_CHIPS, "col"),
    (D_MODEL // N_CHIPS, D_MODEL, "row"),
    (D_MODEL // N_CHIPS, D_MODEL, "row"),
    (D_MODEL // N_CHIPS, D_MODEL, "row"),
    (D_MODEL, 2 * D_FF // N_CHIPS, "col"),
    (D_FF // N_CHIPS, D_MODEL, "row"),
)
N_MAT = len(MATRICES)
BF16_ROW_TILE = 16
CONV_W_COLS = D_MODEL // N_CHIPS
SMALL_ROWS = 8
HBM_SPEC = pl.BlockSpec(memory_space=pl.ANY)


def _whole_shape(spec):
    rows, cols, kind = spec
    return (rows, cols * N_CHIPS) if kind == "col" else (rows * N_CHIPS, cols)


def _half_shape(spec):
    return (spec[0] // 2, spec[1])


def _aligned(start, multiple):
    return start if isinstance(start, int) else pl.multiple_of(start, multiple)


def _region(ref, spec, shard, half):
    rows, cols, kind = spec
    hr = rows // 2
    if kind == "col":
        return ref.at[pl.ds(_aligned(half * hr, BF16_ROW_TILE), hr),
                      pl.ds(_aligned(shard * cols, LANES), cols)]
    return ref.at[pl.ds(_aligned(shard * rows + half * hr, BF16_ROW_TILE), hr), :]


def _position():
    x, y, c = lax.axis_index("x"), lax.axis_index("y"), lax.axis_index("c")
    chips = [(1 - x, y), (x, 1 - y), (1 - x, 1 - y)]
    return x, y, c, chips


def _shard_of(chip):
    return 2 * chip[0] + chip[1]


def _remote(src, dst, send_sem, recv_sem, to):
    return pltpu.make_async_remote_copy(src_ref=src, dst_ref=dst, send_sem=send_sem, recv_sem=recv_sem,
                                        device_id=to, device_id_type=MESH)


def _to_bf16(name, w, rows):
    def body(w_ref, o_ref):
        o_ref[...] = w_ref[...].astype(BF16)

    spec = pl.BlockSpec((rows, w.shape[1]), lambda i: (i, 0))
    return pl.pallas_call(
        body, name=name, grid=(w.shape[0] // rows,), in_specs=[spec], out_specs=spec,
        out_shape=_sds(w.shape, BF16), compiler_params=_params(("parallel",)),
    )(w)


def _all_gather_weights(shards, conv_w):
    n_ici = N_MAT * 3

    def body(*refs):
        ins, outs, sems = refs[:N_MAT + 1], refs[N_MAT + 1:2 * N_MAT + 2], refs[2 * N_MAT + 2:]
        send_a, recv_a, send_b, recv_b, local_sems, send_w, recv_w = sems
        x, y, c, chips = _position()
        me = _shard_of((x, y))
        sibling = (x, y, 1 - c)
        cw_in, cw_out = ins[N_MAT], outs[N_MAT]

        def cw_cols(shard):
            return cw_out.at[:, pl.ds(_aligned(shard * CONV_W_COLS, LANES), CONV_W_COLS)]

        local = []
        for i, spec in enumerate(MATRICES):
            for h in range(2):
                src = ins[i].at[pl.ds(h * (spec[0] // 2), spec[0] // 2), :]
                local.append(pltpu.make_async_copy(src, _region(outs[i], spec, me, h), local_sems.at[2 * i + h]))
        local.append(pltpu.make_async_copy(cw_in, cw_cols(me), local_sems.at[2 * N_MAT]))
        for cp in local:
            cp.start()

        sent = []
        for j, chip in enumerate(chips):
            cp = _remote(cw_in, cw_cols(me), send_w.at[j], recv_w.at[j], (*chip, c))
            cp.start()
            sent.append(cp)
        for i, spec in enumerate(MATRICES):
            hr = spec[0] // 2
            mine = ins[i].at[pl.ds(_aligned(c * hr, BF16_ROW_TILE), hr), :]
            for j, chip in enumerate(chips):
                k = 3 * i + j
                cp = _remote(mine, _region(outs[i], spec, me, c), send_a.at[k], recv_a.at[k], (*chip, c))
                cp.start()
                sent.append(cp)
        for i, spec in enumerate(MATRICES):
            for j, chip in enumerate(chips):
                k = 3 * i + j
                landed = _region(outs[i], spec, _shard_of(chip), c)
                _remote(landed, landed, send_a.at[k], recv_a.at[k], (*chip, c)).wait_recv()
                cp = _remote(landed, landed, send_b.at[k], recv_b.at[k], sibling)
                cp.start()
                sent.append(cp)
        for i, spec in enumerate(MATRICES):
            for j, chip in enumerate(chips):
                k = 3 * i + j
                other = _region(outs[i], spec, _shard_of(chip), 1 - c)
                _remote(other, other, send_b.at[k], recv_b.at[k], sibling).wait_recv()
        for j, chip in enumerate(chips):
            got = cw_cols(_shard_of(chip))
            _remote(got, got, send_w.at[j], recv_w.at[j], (*chip, c)).wait_recv()
        for cp in sent:
            cp.wait_send()
        for cp in local:
            cp.wait()

    out_shape = [_sds(_whole_shape(s), BF16) for s in MATRICES] + [_sds((3, D_MODEL), F32)]
    return pl.pallas_call(
        body, name="all_gather_weights",
        in_specs=[HBM_SPEC] * (N_MAT + 1), out_specs=[HBM_SPEC] * (N_MAT + 1), out_shape=out_shape,
        scratch_shapes=[pltpu.SemaphoreType.DMA((n_ici,)), pltpu.SemaphoreType.DMA((n_ici,)),
                        pltpu.SemaphoreType.DMA((n_ici,)), pltpu.SemaphoreType.DMA((n_ici,)),
                        pltpu.SemaphoreType.DMA((2 * N_MAT + 1,)),
                        pltpu.SemaphoreType.DMA((3,)), pltpu.SemaphoreType.DMA((3,))],
    )(*shards, conv_w)


def _pack_small(dg_mix, dg_ffn, dg_final, dconv_w, dsinks, loss_row):
    def body(a_ref, b_ref, c_ref, w_ref, s_ref, l_ref, o_ref):
        pad = jnp.zeros((1, D_MODEL - LANES), F32)
        o_ref[0:1, :] = a_ref[...]
        o_ref[1:2, :] = b_ref[...]
        o_ref[2:3, :] = c_ref[...]
        o_ref[3:6, :] = w_ref[...]
        o_ref[6:7, :] = jnp.concatenate([s_ref[...], pad], axis=1)
        o_ref[7:8, :] = jnp.concatenate([l_ref[...], pad], axis=1)

    return pl.pallas_call(
        body, name="pack_small", out_shape=_sds((SMALL_ROWS, D_MODEL), F32),
        compiler_params=_params(),
    )(dg_mix, dg_ffn, dg_final, dconv_w, dsinks, loss_row)


def _pair_exchange(dws, small):
    n_big = N_MAT * N_CHIPS

    def body(*refs):
        ins, outs, sems = refs[:N_MAT + 1], refs[N_MAT + 1:2 * N_MAT + 2], refs[2 * N_MAT + 2:]
        send_big, recv_big, send_small, recv_small, local_sem = sems
        x, y, c, _ = _position()
        sibling = (x, y, 1 - c)
        me = 4 * x + 2 * y + c
        small_in, small_out = ins[N_MAT], outs[N_MAT]
        own = pltpu.make_async_copy(small_in, small_out.at[me], local_sem)
        own.start()
        sent = []
        for r in range(1, N_DEV):
            flip = ((r >> 2) & 1, (r >> 1) & 1, r & 1)
            peer = tuple(1 - p if f else p for p, f in zip((x, y, c), flip))
            cp = _remote(small_in, small_out.at[me], send_small.at[r - 1], recv_small.at[r - 1], peer)
            cp.start()
            sent.append(cp)
        for i, spec in enumerate(MATRICES):
            for t in range(N_CHIPS):
                k = N_CHIPS * i + t
                cp = _remote(_region(ins[i], spec, t, 1 - c), outs[i].at[t], send_big.at[k], recv_big.at[k], sibling)
                cp.start()
                sent.append(cp)
        for i, spec in enumerate(MATRICES):
            for t in range(N_CHIPS):
                k = N_CHIPS * i + t
                _remote(outs[i].at[t], outs[i].at[t], send_big.at[k], recv_big.at[k], sibling).wait_recv()
        for r in range(1, N_DEV):
            flip = ((r >> 2) & 1, (r >> 1) & 1, r & 1)
            peer = tuple(1 - p if f else p for p, f in zip((x, y, c), flip))
            slot = small_out.at[4 * peer[0] + 2 * peer[1] + peer[2]]
            _remote(slot, slot, send_small.at[r - 1], recv_small.at[r - 1], peer).wait_recv()
        for cp in sent:
            cp.wait_send()
        own.wait()

    out_shape = [_sds((N_CHIPS, *_half_shape(s)), BF16) for s in MATRICES]
    out_shape.append(_sds((N_DEV, SMALL_ROWS, D_MODEL), F32))
    return pl.pallas_call(
        body, name="pair_exchange",
        in_specs=[HBM_SPEC] * (N_MAT + 1), out_specs=[HBM_SPEC] * (N_MAT + 1), out_shape=out_shape,
        scratch_shapes=[pltpu.SemaphoreType.DMA((n_big,)), pltpu.SemaphoreType.DMA((n_big,)),
                        pltpu.SemaphoreType.DMA((N_DEV - 1,)), pltpu.SemaphoreType.DMA((N_DEV - 1,)),
                        pltpu.SemaphoreType.DMA],
    )(*dws, small)


def _pair_sum(dws, got, c):
    def body(c_ref, *refs):
        del c_ref
        mine, theirs, outs = refs[:N_MAT], refs[N_MAT:2 * N_MAT], refs[2 * N_MAT:]
        for a, b, o in zip(mine, theirs, outs):
            o[...] = (a[...].astype(F32) + b[...].astype(F32)).astype(BF16)

    def mine_spec(spec):
        hr, cols = _half_shape(spec)
        if spec[2] == "col":
            return pl.BlockSpec((hr, cols), lambda t, c_ref: (c_ref[0], t))
        return pl.BlockSpec((hr, cols), lambda t, c_ref: (2 * t + c_ref[0], 0))

    def slot_spec(spec):
        return pl.BlockSpec((None, *_half_shape(spec)), lambda t, c_ref: (t, 0, 0))

    grid_spec = pltpu.PrefetchScalarGridSpec(
        num_scalar_prefetch=1, grid=(N_CHIPS,),
        in_specs=[mine_spec(s) for s in MATRICES] + [slot_spec(s) for s in MATRICES],
        out_specs=[slot_spec(s) for s in MATRICES])
    return pl.pallas_call(
        body, name="pair_sum", grid_spec=grid_spec,
        out_shape=[_sds((N_CHIPS, *_half_shape(s)), BF16) for s in MATRICES],
        compiler_params=_params(("parallel",)),
    )(c, *dws, *got)


def _chip_exchange(sums):
    n_ici = N_MAT * 3

    def body(*refs):
        ins, outs, sems = refs[:N_MAT], refs[N_MAT:2 * N_MAT], refs[2 * N_MAT:]
        send_sems, recv_sems, local_sems = sems
        x, y, c, chips = _position()
        me = _shard_of((x, y))
        local = [pltpu.make_async_copy(ins[i].at[me], outs[i].at[me], local_sems.at[i]) for i in range(N_MAT)]
        for cp in local:
            cp.start()
        sent = []
        for i in range(N_MAT):
            for j, chip in enumerate(chips):
                k = 3 * i + j
                cp = _remote(ins[i].at[_shard_of(chip)], outs[i].at[me], send_sems.at[k], recv_sems.at[k], (*chip, c))
                cp.start()
                sent.append(cp)
        for i in range(N_MAT):
            for j, chip in enumerate(chips):
                k = 3 * i + j
                slot = outs[i].at[_shard_of(chip)]
                _remote(slot, slot, send_sems.at[k], recv_sems.at[k], (*chip, c)).wait_recv()
        for cp in sent:
            cp.wait_send()
        for cp in local:
            cp.wait()

    return pl.pallas_call(
        body, name="chip_exchange",
        in_specs=[HBM_SPEC] * N_MAT, out_specs=[HBM_SPEC] * N_MAT,
        out_shape=[_sds((N_CHIPS, *_half_shape(s)), BF16) for s in MATRICES],
        scratch_shapes=[pltpu.SemaphoreType.DMA((n_ici,)), pltpu.SemaphoreType.DMA((n_ici,)),
                        pltpu.SemaphoreType.DMA((N_MAT,))],
    )(*sums)


def _chip_sum(slots):
    steps = 2

    def body(*refs):
        ins, outs = refs[:N_MAT], refs[N_MAT:]
        for a, o in zip(ins, outs):
            acc = a[0].astype(F32)
            for t in range(1, N_CHIPS):
                acc = acc + a[t].astype(F32)
            o[...] = acc

    def in_spec(spec):
        hr, cols = _half_shape(spec)
        return pl.BlockSpec((N_CHIPS, hr // steps, cols), lambda i: (0, i, 0))

    def out_spec(spec):
        hr, cols = _half_shape(spec)
        return pl.BlockSpec((hr // steps, cols), lambda i: (i, 0))

    return pl.pallas_call(
        body, name="chip_sum", grid=(steps,),
        in_specs=[in_spec(s) for s in MATRICES], out_specs=[out_spec(s) for s in MATRICES],
        out_shape=[_sds(_half_shape(s), F32) for s in MATRICES],
        compiler_params=_params(("parallel",)),
    )(*slots)


def _half_exchange(halves):
    def body(*refs):
        ins, outs, sems = refs[:N_MAT], refs[N_MAT:2 * N_MAT], refs[2 * N_MAT:]
        send_sems, recv_sems, local_sems = sems
        x, y, c, _ = _position()
        sibling = (x, y, 1 - c)

        def rows(i, h):
            hr = MATRICES[i][0] // 2
            return outs[i].at[pl.ds(_aligned(h * hr, 8), hr), :]

        local = [pltpu.make_async_copy(ins[i], rows(i, c), local_sems.at[i]) for i in range(N_MAT)]
        sent = [_remote(ins[i], rows(i, c), send_sems.at[i], recv_sems.at[i], sibling) for i in range(N_MAT)]
        for cp in local + sent:
            cp.start()
        for i in range(N_MAT):
            _remote(rows(i, 1 - c), rows(i, 1 - c), send_sems.at[i], recv_sems.at[i], sibling).wait_recv()
        for cp in sent:
            cp.wait_send()
        for cp in local:
            cp.wait()

    return pl.pallas_call(
        body, name="half_exchange",
        in_specs=[HBM_SPEC] * N_MAT, out_specs=[HBM_SPEC] * N_MAT,
        out_shape=[_sds((s[0], s[1]), F32) for s in MATRICES],
        scratch_shapes=[pltpu.SemaphoreType.DMA((N_MAT,)), pltpu.SemaphoreType.DMA((N_MAT,)),
                        pltpu.SemaphoreType.DMA((N_MAT,))],
    )(*halves)


def _small_sum(blocks):
    def body(b_ref, o_ref):
        acc = b_ref[0]
        for d in range(1, N_DEV):
            acc = acc + b_ref[d]
        o_ref[...] = acc

    return pl.pallas_call(
        body, name="small_sum", out_shape=_sds((SMALL_ROWS, D_MODEL), F32), compiler_params=_params(),
    )(blocks)


def _adamw(name, w, g, m, v, rows):
    def body(w_ref, g_ref, m_ref, v_ref, d_ref, nm_ref, nv_ref):
        g = g_ref[...]
        m = ADAM_B1 * m_ref[...] + (1.0 - ADAM_B1) * g
        v = ADAM_B2 * v_ref[...] + (1.0 - ADAM_B2) * jnp.square(g)
        m_hat = m / (1.0 - ADAM_B1 ** ADAM_STEP)
        v_hat = v / (1.0 - ADAM_B2 ** ADAM_STEP)
        d_ref[...] = -ADAM_LR * (m_hat / (jnp.sqrt(v_hat) + ADAM_EPS) + ADAM_WD * w_ref[...])
        nm_ref[...] = m
        nv_ref[...] = v

    spec = pl.BlockSpec((rows, w.shape[1]), lambda i: (i, 0))
    sd = _sds(w.shape, F32)
    return pl.pallas_call(
        body, name=name, grid=(w.shape[0] // rows,),
        in_specs=[spec] * 4, out_specs=[spec] * 3, out_shape=[sd, sd, sd],
        compiler_params=_params(("parallel",)),
    )(w, g, m, v)


MATRIX_NAMES = ("w_in", "w_conv_out", "w_attn_out", "w_o", "w_gate_up", "w_down")
WEIGHT_ORDER = ("g_mix", "w_in", "conv_w", "attn_sinks", "w_conv_out", "w_attn_out", "w_o", "g_ffn",
                "w_gate_up", "w_down", "g_final")


def kernel(x, g_mix, w_in, conv_w, attn_sinks, w_conv_out, w_attn_out, w_o, g_ffn, w_gate_up, w_down, g_final, loss_target, m_g_mix, m_w_in, m_conv_w, m_attn_sinks, m_w_conv_out, m_w_attn_out, m_w_o, m_g_ffn, m_w_gate_up, m_w_down, m_g_final, v_g_mix, v_w_in, v_conv_w, v_attn_sinks, v_w_conv_out, v_w_attn_out, v_w_o, v_g_ffn, v_w_gate_up, v_w_down, v_g_final):
    w = dict(g_mix=g_mix, w_in=w_in[0], conv_w=conv_w[0], attn_sinks=attn_sinks, w_conv_out=w_conv_out[0],
             w_attn_out=w_attn_out[0], w_o=w_o[0], g_ffn=g_ffn, w_gate_up=w_gate_up[0], w_down=w_down[0],
             g_final=g_final[None, :])
    m = dict(g_mix=m_g_mix, w_in=m_w_in[0], conv_w=m_conv_w[0], attn_sinks=m_attn_sinks,
             w_conv_out=m_w_conv_out[0], w_attn_out=m_w_attn_out[0], w_o=m_w_o[0], g_ffn=m_g_ffn,
             w_gate_up=m_w_gate_up[0], w_down=m_w_down[0], g_final=m_g_final[None, :])
    v = dict(g_mix=v_g_mix, w_in=v_w_in[0], conv_w=v_conv_w[0], attn_sinks=v_attn_sinks,
             w_conv_out=v_w_conv_out[0], w_attn_out=v_w_attn_out[0], w_o=v_w_o[0], g_ffn=v_g_ffn,
             w_gate_up=v_w_gate_up[0], w_down=v_w_down[0], g_final=v_g_final[None, :])
    shard = 2 * lax.axis_index("x") + lax.axis_index("y")
    core = lax.axis_index("c").astype(jnp.int32).reshape((1,))

    cast_rows = {"w_down": D_FF // N_CHIPS // 2}
    shards = [_to_bf16("cast_" + n, w[n], cast_rows.get(n, 256)) for n in MATRIX_NAMES]
    *whole, conv_w_whole = _all_gather_weights(shards, w["conv_w"])

    grad_x, small, dws = _local_step(x[0], loss_target[0], w["g_mix"], w["g_ffn"], w["g_final"],
                                     w["attn_sinks"], conv_w_whole, *whole)

    *got, small_blocks = _pair_exchange(dws, _pack_small(*small))
    slots = _chip_exchange(_pair_sum(dws, got, core))
    mat_grads = _half_exchange(_chip_sum(slots))
    small_sum = _small_sum(small_blocks)

    g = dict(zip(MATRIX_NAMES, mat_grads))
    g["g_mix"] = small_sum[0:1, :]
    g["g_ffn"] = small_sum[1:2, :]
    g["g_final"] = small_sum[2:3, :]
    g["conv_w"] = lax.dynamic_slice(small_sum, (3, shard * CONV_W_COLS), (3, CONV_W_COLS))
    g["attn_sinks"] = small_sum[6:7, :N_HEADS]
    loss = small_sum[7, 0]

    adam_rows = {"w_in": 256, "w_gate_up": 256, "w_down": D_FF // N_CHIPS // 2}
    delta, new_m, new_v = {}, {}, {}
    for n in WEIGHT_ORDER:
        rows = adam_rows.get(n, w[n].shape[0])
        delta[n], new_m[n], new_v[n] = _adamw("adamw_" + n, w[n], g[n], m[n], v[n], rows)

    def shaped(vals):
        return [vals[n].reshape((D_MODEL,)) if n == "g_final" else
                (vals[n][None] if n in MATRIX_NAMES or n == "conv_w" else vals[n]) for n in WEIGHT_ORDER]

    return (loss, grad_x[None], *shaped(g), *shaped(delta), *shaped(new_m), *shaped(new_v))
```

```python
import functools
import math

import jax
import jax.numpy as jnp
from jax import lax
from jax.experimental import pallas as pl
from jax.experimental.pallas import tpu as pltpu

F32 = jnp.float32
BF16 = jnp.bfloat16

D_MODEL = 1024
SEQ = 2048
HEAD_DIM = 64
N_HEADS = 16
N_KV_HEADS = 4
GROUP = N_HEADS // N_KV_HEADS
D_ATTN = N_HEADS * HEAD_DIM
D_KV = N_KV_HEADS * HEAD_DIM
BLOCK = 128
ROT_DIM = HEAD_DIM // 4
ROPE_THETA = 500000.0
ATTN_SCALE = 1.0 / math.sqrt(HEAD_DIM)
NEG_INF = -1e30
D_FF = 2816
EPS = 1e-5
N_IN = 3 * D_MODEL + D_ATTN + 2 * D_KV + 2 * D_MODEL
COL_Q = 3 * D_MODEL
COL_K = COL_Q + D_ATTN
COL_V = COL_K + D_KV
COL_GC = COL_V + D_KV
COL_GA = COL_GC + D_MODEL

ADAM_LR = 0.001
ADAM_B1 = 0.9
ADAM_B2 = 0.999
ADAM_EPS = 1e-08
ADAM_WD = 0.01
ADAM_STEP = 10

N_CHIPS = 4
N_DEV = 8

V7X_VMEM_BYTES = 64 * 1024 * 1024
VMEM_LIMIT = (V7X_VMEM_BYTES * 3) // 4
LANES = 128
MESH = pl.DeviceIdType.MESH


def _params(semantics=None):
    return pltpu.CompilerParams(dimension_semantics=semantics, vmem_limit_bytes=VMEM_LIMIT)


def _sds(shape, dtype):
    return jax.ShapeDtypeStruct(shape, dtype)


NN = ((1,), (0,))
NT = ((1,), (1,))
TN = ((0,), (0,))


def _matmul(name, a, b, dims, grid, a_spec, b_spec, o_spec, o_shape, o_dtype, res=None, res_spec=None):
    nk = grid[2]

    def body(*refs):
        if res is None:
            a_ref, b_ref, o_ref = refs[:3]
            r_ref = None
            scratch = refs[3:]
        else:
            a_ref, b_ref, r_ref, o_ref = refs[:4]
            scratch = refs[4:]
        p = lax.dot_general(a_ref[...], b_ref[...], (dims, ((), ())), preferred_element_type=F32)

        def finish(acc):
            if r_ref is not None:
                acc = r_ref[...] + acc
            o_ref[...] = acc.astype(o_dtype)

        if nk == 1:
            finish(p)
        else:
            acc_ref = scratch[0]
            k = pl.program_id(2)

            @pl.when(k == 0)
            def _():
                acc_ref[...] = p

            @pl.when(k > 0)
            def _():
                acc_ref[...] += p

            @pl.when(k == nk - 1)
            def _():
                finish(acc_ref[...])

    operands = [a, b] if res is None else [a, b, res]
    in_specs = [a_spec, b_spec] if res is None else [a_spec, b_spec, res_spec]
    scratch = [pltpu.VMEM(o_spec.block_shape, F32)] if nk > 1 else []
    return pl.pallas_call(
        body,
        name=name,
        grid=grid,
        in_specs=in_specs,
        out_specs=o_spec,
        out_shape=_sds(o_shape, o_dtype),
        scratch_shapes=scratch,
        compiler_params=_params(("parallel", "parallel", "arbitrary")),
    )(*operands)


def _mm_nn(name, a, b, bm, bn, o_dtype, res=None):
    m, k = a.shape
    n = b.shape[1]
    return _matmul(
        name, a, b, NN, (m // bm, n // bn, 1),
        pl.BlockSpec((bm, k), lambda i, j, kk: (i, 0)),
        pl.BlockSpec((k, bn), lambda i, j, kk: (0, j)),
        pl.BlockSpec((bm, bn), lambda i, j, kk: (i, j)),
        (m, n), o_dtype, res,
        None if res is None else pl.BlockSpec((bm, bn), lambda i, j, kk: (i, j)),
    )


def _mm_nt(name, a, b, bm, bn, bk, o_dtype):
    m, k = a.shape
    n = b.shape[0]
    return _matmul(
        name, a, b, NT, (m // bm, n // bn, k // bk),
        pl.BlockSpec((bm, bk), lambda i, j, kk: (i, kk)),
        pl.BlockSpec((bn, bk), lambda i, j, kk: (j, kk)),
        pl.BlockSpec((bm, bn), lambda i, j, kk: (i, j)),
        (m, n), o_dtype,
    )


def _mm_tn(name, a, b, bm, bn, o_dtype):
    k, m = a.shape
    n = b.shape[1]
    return _matmul(
        name, a, b, TN, (m // bm, n // bn, 1),
        pl.BlockSpec((k, bm), lambda i, j, kk: (0, i)),
        pl.BlockSpec((k, bn), lambda i, j, kk: (0, j)),
        pl.BlockSpec((bm, bn), lambda i, j, kk: (i, j)),
        (m, n), o_dtype,
    )


ROWS = 256


def _row_spec(width, col=0):
    return pl.BlockSpec((ROWS, width), lambda i: (i, col))


def _full_spec(shape):
    return pl.BlockSpec(shape, lambda *_: (0,) * len(shape))


def _rms_norm(name, x, g):
    def body(x_ref, g_ref, h_ref):
        xf = x_ref[...]
        r = lax.rsqrt(jnp.mean(xf * xf, axis=-1, keepdims=True) + EPS)
        h_ref[...] = ((xf * r) * g_ref[...]).astype(BF16)

    return pl.pallas_call(
        body, name=name, grid=(SEQ // ROWS,),
        in_specs=[_row_spec(D_MODEL), _full_spec((1, D_MODEL))],
        out_specs=_row_spec(D_MODEL),
        out_shape=_sds((SEQ, D_MODEL), BF16),
        compiler_params=_params(("parallel",)),
    )(x, g)


CONV_COLS = 256


def _shift_rows(u, k):
    rows = lax.broadcasted_iota(jnp.int32, u.shape, 0)
    return jnp.where(rows >= k, pltpu.roll(u, k, axis=0), 0.0)


def _conv_fwd(proj, conv_w):
    nblk = D_MODEL // CONV_COLS

    def body(cb_ref, cc_ref, cx_ref, w_ref, y_ref):
        u = cc_ref[...] * cx_ref[...]
        w = w_ref[...]
        cv = w[0:1, :] * _shift_rows(u, 2) + w[1:2, :] * _shift_rows(u, 1) + w[2:3, :] * u
        y_ref[...] = (cb_ref[...] * cv).astype(BF16)

    def col(part):
        return pl.BlockSpec((SEQ, CONV_COLS), lambda j: (0, part * nblk + j))

    return pl.pallas_call(
        body, name="conv_fwd", grid=(nblk,),
        in_specs=[col(0), col(1), col(2), pl.BlockSpec((3, CONV_COLS), lambda j: (0, j))],
        out_specs=pl.BlockSpec((SEQ, CONV_COLS), lambda j: (0, j)),
        out_shape=_sds((SEQ, D_MODEL), BF16),
        compiler_params=_params(("parallel",)),
    )(proj, proj, proj, conv_w)


ROPE_COLS = 256


def _rope_tables():
    inv_freq = ROPE_THETA ** (-jnp.arange(0, ROT_DIM, 2, dtype=F32) / ROT_DIM)
    ang = jnp.arange(SEQ, dtype=F32)[:, None] * inv_freq[None, :]
    cos, sin = jnp.cos(ang), jnp.sin(ang)
    half = ROT_DIM // 2
    ones = jnp.ones((SEQ, HEAD_DIM - ROT_DIM), F32)
    zeros = jnp.zeros((SEQ, HEAD_DIM - ROT_DIM), F32)
    zh = jnp.zeros((SEQ, half), F32)
    c = jnp.concatenate([cos, cos, ones], axis=1)
    s_up = jnp.concatenate([-sin, zh, zeros], axis=1)
    s_dn = jnp.concatenate([zh, sin, zeros], axis=1)
    reps = ROPE_COLS // HEAD_DIM
    return tuple(jnp.tile(t, (1, reps)) for t in (c, s_up, s_dn))


def _rotate(t, c, s_up, s_dn):
    width = t.shape[1]
    half = ROT_DIM // 2
    return t * c + pltpu.roll(t, width - half, axis=1) * s_up + pltpu.roll(t, half, axis=1) * s_dn


def _rope_fwd(proj, tables):
    ntile = (D_ATTN + 2 * D_KV) // ROPE_COLS
    first = COL_Q // ROPE_COLS

    def body(t_ref, c_ref, su_ref, sd_ref, o_ref):
        j = pl.program_id(1)
        t = t_ref[...]
        rot = _rotate(t, c_ref[...], su_ref[...], sd_ref[...])
        o_ref[...] = jnp.where(j < ntile - 1, rot, t).astype(BF16)

    tab = pl.BlockSpec((ROWS, ROPE_COLS), lambda i, j: (i, 0))
    return pl.pallas_call(
        body, name="rope_fwd", grid=(SEQ // ROWS, ntile),
        in_specs=[pl.BlockSpec((ROWS, ROPE_COLS), lambda i, j: (i, first + j)), tab, tab, tab],
        out_specs=pl.BlockSpec((ROWS, ROPE_COLS), lambda i, j: (i, j)),
        out_shape=_sds((SEQ, D_ATTN + 2 * D_KV), BF16),
        compiler_params=_params(("parallel", "parallel")),
    )(proj, *tables)


N_QBLK = SEQ // BLOCK
KV_TILE = D_ATTN // D_KV


def _attn_specs():
    q = pl.BlockSpec((BLOCK, D_ATTN), lambda n: (n, 0))
    k_prev = pl.BlockSpec((BLOCK, D_KV), lambda n: (jnp.maximum(n - 1, 0), KV_TILE))
    k_cur = pl.BlockSpec((BLOCK, D_KV), lambda n: (n, KV_TILE))
    v_prev = pl.BlockSpec((BLOCK, D_KV), lambda n: (jnp.maximum(n - 1, 0), KV_TILE + 1))
    v_cur = pl.BlockSpec((BLOCK, D_KV), lambda n: (n, KV_TILE + 1))
    return [q, k_prev, k_cur, v_prev, v_cur]


def _band_mask(n):
    qi = lax.broadcasted_iota(jnp.int32, (GROUP * BLOCK, 2 * BLOCK), 0) % BLOCK
    kj = lax.broadcasted_iota(jnp.int32, (GROUP * BLOCK, 2 * BLOCK), 1)
    rel = qi + BLOCK - kj
    return (rel >= 0) & (rel < BLOCK) & ((kj >= BLOCK) | (n > 0))


def _head_cols(ref_or_val, h, width=HEAD_DIM):
    return ref_or_val[:, h * width:(h + 1) * width]


def _group_rows(x, h):
    return jnp.concatenate([_head_cols(x, h * GROUP + g) for g in range(GROUP)], axis=0)


def _sink_col(sink_ref, h):
    return jnp.concatenate(
        [jnp.full((BLOCK, 1), sink_ref[0, h * GROUP + g], F32) for g in range(GROUP)], axis=0)


def _softmax_with_sink(s, sink):
    m = jnp.maximum(jnp.max(s, axis=-1, keepdims=True), sink)
    e = jnp.exp(s - m)
    es = jnp.exp(sink - m)
    z = jnp.sum(e, axis=-1, keepdims=True) + es
    return e / z, es / z


def _attn_fwd(qkv, sinks):
    def body(sink_ref, q_ref, kp_ref, kc_ref, vp_ref, vc_ref, o_ref):
        n = pl.program_id(0)
        mask = _band_mask(n)
        q = q_ref[...]
        k = jnp.concatenate([kp_ref[...], kc_ref[...]], axis=0)
        v = jnp.concatenate([vp_ref[...], vc_ref[...]], axis=0)
        outs = []
        for h in range(N_KV_HEADS):
            s = lax.dot_general(_group_rows(q, h), _head_cols(k, h), (NT, ((), ())),
                                preferred_element_type=F32) * ATTN_SCALE
            s = jnp.where(mask, s, NEG_INF)
            p, _ = _softmax_with_sink(s, _sink_col(sink_ref, h))
            o = jnp.dot(p.astype(BF16), _head_cols(v, h), preferred_element_type=F32)
            outs += [o[g * BLOCK:(g + 1) * BLOCK, :] for g in range(GROUP)]
        o_ref[...] = jnp.concatenate(outs, axis=1).astype(BF16)

    return pl.pallas_call(
        body, name="attn_fwd", grid=(N_QBLK,),
        in_specs=[pl.BlockSpec(memory_space=pltpu.SMEM)] + _attn_specs(),
        out_specs=pl.BlockSpec((BLOCK, D_ATTN), lambda n: (n, 0)),
        out_shape=_sds((SEQ, D_ATTN), BF16),
        compiler_params=_params(("parallel",)),
    )(sinks, qkv, qkv, qkv, qkv, qkv)


def _branch_merge(conv_y, attn, w_co, w_ao, proj):
    bm, bn = 1024, 512

    def body(cy_ref, at_ref, wc_ref, wa_ref, gc_ref, ga_ref, co_ref, ao_ref, mg_ref):
        co = jnp.dot(cy_ref[...], wc_ref[...], preferred_element_type=F32)
        ao = jnp.dot(at_ref[...], wa_ref[...], preferred_element_type=F32)
        co_ref[...] = co
        ao_ref[...] = ao
        mg_ref[...] = (jax.nn.sigmoid(gc_ref[...]) * co + jax.nn.sigmoid(ga_ref[...]) * ao).astype(BF16)

    act = pl.BlockSpec((bm, D_MODEL), lambda i, j: (i, 0))
    wgt = pl.BlockSpec((D_MODEL, bn), lambda i, j: (0, j))
    out = pl.BlockSpec((bm, bn), lambda i, j: (i, j))
    return pl.pallas_call(
        body, name="branch_merge", grid=(SEQ // bm, D_MODEL // bn),
        in_specs=[act, act, wgt, wgt,
                  pl.BlockSpec((bm, bn), lambda i, j: (i, COL_GC // bn + j)),
                  pl.BlockSpec((bm, bn), lambda i, j: (i, COL_GA // bn + j))],
        out_specs=[out, out, out],
        out_shape=[_sds((SEQ, D_MODEL), F32), _sds((SEQ, D_MODEL), F32), _sds((SEQ, D_MODEL), BF16)],
        compiler_params=_params(("parallel", "parallel")),
    )(conv_y, attn, w_co, w_ao, proj, proj)


FF_ROWS = 128


def _swiglu_fwd(gu):
    def body(gu_ref, act_ref):
        g = gu_ref[:, :D_FF]
        act_ref[...] = (jax.nn.silu(g) * gu_ref[:, D_FF:]).astype(BF16)

    return pl.pallas_call(
        body, name="swiglu_fwd", grid=(SEQ // FF_ROWS,),
        in_specs=[pl.BlockSpec((FF_ROWS, 2 * D_FF), lambda i: (i, 0))],
        out_specs=pl.BlockSpec((FF_ROWS, D_FF), lambda i: (i, 0)),
        out_shape=_sds((SEQ, D_FF), BF16),
        compiler_params=_params(("parallel",)),
    )(gu)


def _loss_head(x3, g, target):
    def body(x_ref, g_ref, t_ref, dx_ref, dxb_ref, dg_ref, loss_ref):
        i = pl.program_id(0)
        xf = x_ref[...]
        r = lax.rsqrt(jnp.mean(xf * xf, axis=-1, keepdims=True) + EPS)
        xn = xf * r
        gg = g_ref[...]
        err = xn * gg - t_ref[...]
        part = 0.5 * jnp.sum(jnp.mean(err * err, axis=-1, keepdims=True), axis=0, keepdims=True)
        dy = err * (1.0 / D_MODEL)
        dxn = dy * gg
        dx = r * (dxn - xn * jnp.mean(dxn * xn, axis=-1, keepdims=True))
        dx_ref[...] = dx
        dxb_ref[...] = dx.astype(BF16)
        dg = jnp.sum(dy * xn, axis=0, keepdims=True)
        lane0 = lax.broadcasted_iota(jnp.int32, (1, LANES), 1) == 0
        lpart = jnp.where(lane0, part, 0.0)

        @pl.when(i == 0)
        def _():
            dg_ref[...] = dg
            loss_ref[...] = lpart

        @pl.when(i > 0)
        def _():
            dg_ref[...] += dg
            loss_ref[...] += lpart

    return pl.pallas_call(
        body, name="loss_head", grid=(SEQ // ROWS,),
        in_specs=[_row_spec(D_MODEL), _full_spec((1, D_MODEL)), _row_spec(D_MODEL)],
        out_specs=[_row_spec(D_MODEL), _row_spec(D_MODEL), _full_spec((1, D_MODEL)), _full_spec((1, LANES))],
        out_shape=[_sds((SEQ, D_MODEL), F32), _sds((SEQ, D_MODEL), BF16),
                   _sds((1, D_MODEL), F32), _sds((1, LANES), F32)],
        compiler_params=_params(("arbitrary",)),
    )(x3, g, target)


def _swiglu_bwd(dact, gu):
    def body(da_ref, gu_ref, o_ref):
        g = gu_ref[:, :D_FF]
        up = gu_ref[:, D_FF:]
        da = da_ref[...]
        sg = jax.nn.sigmoid(g)
        o_ref[:, :D_FF] = (da * up * (sg * (1.0 + g * (1.0 - sg)))).astype(BF16)
        o_ref[:, D_FF:] = (da * (g * sg)).astype(BF16)

    return pl.pallas_call(
        body, name="swiglu_bwd", grid=(SEQ // FF_ROWS,),
        in_specs=[pl.BlockSpec((FF_ROWS, D_FF), lambda i: (i, 0)),
                  pl.BlockSpec((FF_ROWS, 2 * D_FF), lambda i: (i, 0))],
        out_specs=pl.BlockSpec((FF_ROWS, 2 * D_FF), lambda i: (i, 0)),
        out_shape=_sds((SEQ, 2 * D_FF), BF16),
        compiler_params=_params(("parallel",)),
    )(dact, gu)


def _rms_norm_bwd(name, dh, x, g, dres, with_bf16):
    def body(dh_ref, x_ref, g_ref, dr_ref, *outs):
        i = pl.program_id(0)
        dx_ref = outs[0]
        dg_ref = outs[-1]
        xf = x_ref[...]
        r = lax.rsqrt(jnp.mean(xf * xf, axis=-1, keepdims=True) + EPS)
        xn = xf * r
        dh = dh_ref[...]
        dxn = dh * g_ref[...]
        dx = dr_ref[...] + r * (dxn - xn * jnp.mean(dxn * xn, axis=-1, keepdims=True))
        dx_ref[...] = dx
        if with_bf16:
            outs[1][...] = dx.astype(BF16)
        dg = jnp.sum(dh * xn, axis=0, keepdims=True)

        @pl.when(i == 0)
        def _():
            dg_ref[...] = dg

        @pl.when(i > 0)
        def _():
            dg_ref[...] += dg

    row = _row_spec(D_MODEL)
    out_specs = [row] + ([row] if with_bf16 else []) + [_full_spec((1, D_MODEL))]
    out_shape = ([_sds((SEQ, D_MODEL), F32)] + ([_sds((SEQ, D_MODEL), BF16)] if with_bf16 else [])
                 + [_sds((1, D_MODEL), F32)])
    return pl.pallas_call(
        body, name=name, grid=(SEQ // ROWS,),
        in_specs=[row, row, _full_spec((1, D_MODEL)), row],
        out_specs=out_specs, out_shape=out_shape,
        compiler_params=_params(("arbitrary",)),
    )(dh, x, g, dres)


def _merge_bwd(dmerged, conv_out, attn_out, proj):
    def body(dm_ref, co_ref, ao_ref, gc_ref, ga_ref, dco_ref, dao_ref, dgc_ref, dga_ref):
        dm = dm_ref[...]
        sc = jax.nn.sigmoid(gc_ref[...])
        sa = jax.nn.sigmoid(ga_ref[...])
        dco_ref[...] = (dm * sc).astype(BF16)
        dao_ref[...] = (dm * sa).astype(BF16)
        dgc_ref[...] = (dm * co_ref[...] * (sc * (1.0 - sc))).astype(BF16)
        dga_ref[...] = (dm * ao_ref[...] * (sa * (1.0 - sa))).astype(BF16)

    half = D_MODEL // 2
    own = pl.BlockSpec((ROWS, half), lambda i, j: (i, j))
    sd = _sds((SEQ, D_MODEL), BF16)
    return pl.pallas_call(
        body, name="merge_bwd", grid=(SEQ // ROWS, 2),
        in_specs=[own, own, own,
                  pl.BlockSpec((ROWS, half), lambda i, j: (i, COL_GC // half + j)),
                  pl.BlockSpec((ROWS, half), lambda i, j: (i, COL_GA // half + j))],
        out_specs=[own, own, own, own], out_shape=[sd, sd, sd, sd],
        compiler_params=_params(("parallel", "parallel")),
    )(dmerged, conv_out, attn_out, proj, proj)


def _conv_bwd(dconv_y, proj, conv_w):
    nblk = D_MODEL // CONV_COLS

    def body(dy_ref, cb_ref, cc_ref, cx_ref, w_ref, dcb_ref, dcc_ref, dcx_ref, dw_ref):
        cc = cc_ref[...]
        cx = cx_ref[...]
        u = cc * cx
        w = w_ref[...]
        u1 = _shift_rows(u, 1)
        u2 = _shift_rows(u, 2)
        cv = w[0:1, :] * u2 + w[1:2, :] * u1 + w[2:3, :] * u
        dy = dy_ref[...]
        dcb_ref[...] = (dy * cv).astype(BF16)
        dcv = dy * cb_ref[...]
        rows = lax.broadcasted_iota(jnp.int32, dcv.shape, 0)
        up1 = jnp.where(rows < SEQ - 1, pltpu.roll(dcv, SEQ - 1, axis=0), 0.0)
        up2 = jnp.where(rows < SEQ - 2, pltpu.roll(dcv, SEQ - 2, axis=0), 0.0)
        du = w[2:3, :] * dcv + w[1:2, :] * up1 + w[0:1, :] * up2
        dcc_ref[...] = (du * cx).astype(BF16)
        dcx_ref[...] = (du * cc).astype(BF16)
        dw_ref[...] = jnp.concatenate(
            [jnp.sum(dcv * u2, axis=0, keepdims=True),
             jnp.sum(dcv * u1, axis=0, keepdims=True),
             jnp.sum(dcv * u, axis=0, keepdims=True)], axis=0)

    def col(part):
        return pl.BlockSpec((SEQ, CONV_COLS), lambda j: (0, part * nblk + j))

    own = pl.BlockSpec((SEQ, CONV_COLS), lambda j: (0, j))
    wsp = pl.BlockSpec((3, CONV_COLS), lambda j: (0, j))
    sd = _sds((SEQ, D_MODEL), BF16)
    return pl.pallas_call(
        body, name="conv_bwd", grid=(nblk,),
        in_specs=[own, col(0), col(1), col(2), wsp],
        out_specs=[own, own, own, wsp],
        out_shape=[sd, sd, sd, _sds((3, D_MODEL), F32)],
        compiler_params=_params(("parallel",)),
    )(dconv_y, proj, proj, proj, conv_w)


def _attn_bwd(qkv, dattn, sinks, tables):
    c_t, su_t, sd_t = tables

    def body(sink_ref, q_ref, kp_ref, kc_ref, vp_ref, vc_ref, do_ref, c_ref, su_ref, sd_ref,
             dq_ref, dkp_ref, dkc_ref, dvp_ref, dvc_ref, ds_ref):
        n = pl.program_id(0)
        mask = _band_mask(n)
        q = q_ref[...]
        do = do_ref[...]
        k = jnp.concatenate([kp_ref[...], kc_ref[...]], axis=0)
        v = jnp.concatenate([vp_ref[...], vc_ref[...]], axis=0)
        lane = lax.broadcasted_iota(jnp.int32, (1, LANES), 1)
        dsink = jnp.zeros((1, LANES), F32)
        dqs, dks, dvs = [], [], []
        for h in range(N_KV_HEADS):
            qg = _group_rows(q, h)
            dog = _group_rows(do, h)
            kh = _head_cols(k, h)
            vh = _head_cols(v, h)
            s = lax.dot_general(qg, kh, (NT, ((), ())), preferred_element_type=F32) * ATTN_SCALE
            s = jnp.where(mask, s, NEG_INF)
            p, p_sink = _softmax_with_sink(s, _sink_col(sink_ref, h))
            dp = lax.dot_general(dog, vh, (NT, ((), ())), preferred_element_type=F32)
            delta = jnp.sum(p * dp, axis=-1, keepdims=True)
            ds = (p * (dp - delta) * ATTN_SCALE).astype(BF16)
            dqg = jnp.dot(ds, kh, preferred_element_type=F32)
            dqs += [dqg[g * BLOCK:(g + 1) * BLOCK, :] for g in range(GROUP)]
            dks.append(lax.dot_general(ds, qg, (TN, ((), ())), preferred_element_type=F32))
            dvs.append(lax.dot_general(p.astype(BF16), dog, (TN, ((), ())), preferred_element_type=F32))
            dsk = -(p_sink * delta)
            for g in range(GROUP):
                val = jnp.sum(dsk[g * BLOCK:(g + 1) * BLOCK, :], axis=0, keepdims=True)
                dsink = dsink + jnp.where(lane == h * GROUP + g, val, 0.0)
        c, su, sd = c_ref[...], su_ref[...], sd_ref[...]
        for t in range(D_ATTN // ROPE_COLS):
            dq_t = jnp.concatenate(dqs[t * GROUP:(t + 1) * GROUP], axis=1)
            dq_ref[:, t * ROPE_COLS:(t + 1) * ROPE_COLS] = _rotate(dq_t, c, -su, -sd).astype(BF16)
        dk = jnp.concatenate(dks, axis=1)
        dv = jnp.concatenate(dvs, axis=1)
        dkp_ref[...] = dk[:BLOCK, :]
        dkc_ref[...] = dk[BLOCK:, :]
        dvp_ref[...] = dv[:BLOCK, :]
        dvc_ref[...] = dv[BLOCK:, :]

        @pl.when(n == 0)
        def _():
            ds_ref[...] = dsink

        @pl.when(n > 0)
        def _():
            ds_ref[...] += dsink

    blk = pl.BlockSpec((BLOCK, D_KV), lambda n: (n, 0))
    tab = pl.BlockSpec((BLOCK, ROPE_COLS), lambda n: (n, 0))
    kv = _sds((SEQ, D_KV), F32)
    return pl.pallas_call(
        body, name="attn_bwd", grid=(N_QBLK,),
        in_specs=[pl.BlockSpec(memory_space=pltpu.SMEM)] + _attn_specs()
        + [pl.BlockSpec((BLOCK, D_ATTN), lambda n: (n, 0)), tab, tab, tab],
        out_specs=[pl.BlockSpec((BLOCK, D_ATTN), lambda n: (n, 0)), blk, blk, blk, blk,
                   _full_spec((1, LANES))],
        out_shape=[_sds((SEQ, D_ATTN), BF16), kv, kv, kv, kv, _sds((1, LANES), F32)],
        compiler_params=_params(("arbitrary",)),
    )(sinks, qkv, qkv, qkv, qkv, qkv, dattn, c_t, su_t, sd_t)


def _kv_grad_combine(dk_prev, dk_cur, dv_prev, dv_cur, tables):
    def body(kp_ref, kc_ref, vp_ref, vc_ref, c_ref, su_ref, sd_ref, o_ref):
        m = pl.program_id(0)
        has_next = m < N_QBLK - 1
        dk = kc_ref[...] + jnp.where(has_next, kp_ref[...], 0.0)
        dv = vc_ref[...] + jnp.where(has_next, vp_ref[...], 0.0)
        o_ref[:, :D_KV] = _rotate(dk, c_ref[...], -su_ref[...], -sd_ref[...]).astype(BF16)
        o_ref[:, D_KV:] = dv.astype(BF16)

    cur = pl.BlockSpec((BLOCK, D_KV), lambda m: (m, 0))
    nxt = pl.BlockSpec((BLOCK, D_KV), lambda m: (jnp.minimum(m + 1, N_QBLK - 1), 0))
    return pl.pallas_call(
        body, name="kv_grad_combine", grid=(N_QBLK,),
        in_specs=[nxt, cur, nxt, cur, cur, cur, cur],
        out_specs=pl.BlockSpec((BLOCK, 2 * D_KV), lambda m: (m, 0)),
        out_shape=_sds((SEQ, 2 * D_KV), BF16),
        compiler_params=_params(("parallel",)),
    )(dk_prev, dk_cur, dv_prev, dv_cur, *tables)


def _local_step(x, target, g_mix, g_ffn, g_final, sinks, conv_w, w_in, w_co, w_ao, w_o, w_gu, w_down):
    tables = _rope_tables()
    h1 = _rms_norm("norm_mix", x, g_mix)
    proj = _mm_nn("mm_in", h1, w_in, 1024, 1664, F32)
    conv_y = _conv_fwd(proj, conv_w)
    qkv = _rope_fwd(proj, tables)
    attn = _attn_fwd(qkv, sinks)
    conv_out, attn_out, merged = _branch_merge(conv_y, attn, w_co, w_ao, proj)
    x2 = _mm_nn("mm_o", merged, w_o, 1024, 1024, F32, res=x)
    h2 = _rms_norm("norm_ffn", x2, g_ffn)
    gu = _mm_nn("mm_gate_up", h2, w_gu, 1024, 1408, F32)
    act = _swiglu_fwd(gu)
    x3 = _mm_nn("mm_down", act, w_down, 1024, 512, F32, res=x2)
    dx3, dx3b, dg_final, loss_row = _loss_head(x3, g_final, target)
    dact = _mm_nt("mm_dact", dx3b, w_down, 1024, 1408, D_MODEL, F32)
    dw_down = _mm_tn("mm_dw_down", act, dx3b, 1408, 1024, BF16)
    dgu = _swiglu_bwd(dact, gu)
    dh2 = _mm_nt("mm_dh2", dgu, w_gu, 1024, 1024, 1408, F32)
    dw_gu = _mm_tn("mm_dw_gate_up", h2, dgu, 1024, 1408, BF16)
    dx2, dx2b, dg_ffn = _rms_norm_bwd("norm_ffn_bwd", dh2, x2, g_ffn, dx3, True)
    dmerged = _mm_nt("mm_dmerged", dx2b, w_o, 1024, 1024, D_MODEL, F32)
    dw_o = _mm_tn("mm_dw_o", merged, dx2b, 1024, 1024, BF16)
    dco, dao, dgc, dga = _merge_bwd(dmerged, conv_out, attn_out, proj)
    dconv_y = _mm_nt("mm_dconv_y", dco, w_co, 1024, 1024, D_MODEL, F32)
    dw_co = _mm_tn("mm_dw_conv_out", conv_y, dco, 1024, 1024, BF16)
    dattn = _mm_nt("mm_dattn", dao, w_ao, 1024, 1024, D_MODEL, BF16)
    dw_ao = _mm_tn("mm_dw_attn_out", attn, dao, 1024, 1024, BF16)
    dcb, dcc, dcx, dconv_w = _conv_bwd(dconv_y, proj, conv_w)
    dq, dk_prev, dk_cur, dv_prev, dv_cur, dsinks = _attn_bwd(qkv, dattn, sinks, tables)
    dkv = _kv_grad_combine(dk_prev, dk_cur, dv_prev, dv_cur, tables)
    dproj = jnp.concatenate([dcb, dcc, dcx, dq, dkv, dgc, dga], axis=1)
    dh1 = _mm_nt("mm_dh1", dproj, w_in, 1024, 1024, 1664, F32)
    dw_in = _mm_tn("mm_dw_in", h1, dproj, 1024, 1664, BF16)
    grad_x, dg_mix = _rms_norm_bwd("norm_mix_bwd", dh1, x, g_mix, dx2, False)
    small = (dg_mix, dg_ffn, dg_final, dconv_w, dsinks, loss_row)
    return grad_x, small, (dw_in, dw_co, dw_ao, dw_o, dw_gu, dw_down)


MATRICES = (
    (D_MODEL, N_IN // N_CHIPS, "col"),
    (D_MODEL // N_CHIPS, D_MODEL, "row"),
    (D_MODEL // N_CHIPS, D_MODEL, "row"),
    (D_MODEL // N_CHIPS, D_MODEL, "row"),
    (D_MODEL, 2 * D_FF // N_CHIPS, "col"),
    (D_FF // N_CHIPS, D_MODEL, "row"),
)
N_MAT = len(MATRICES)
BF16_ROW_TILE = 16
CONV_W_COLS = D_MODEL // N_CHIPS
SMALL_ROWS = 8
HBM_SPEC = pl.BlockSpec(memory_space=pl.ANY)


def _whole_shape(spec):
    rows, cols, kind = spec
    return (rows, cols * N_CHIPS) if kind == "col" else (rows * N_CHIPS, cols)


def _half_shape(spec):
    return (spec[0] // 2, spec[1])


def _aligned(start, multiple):
    return start if isinstance(start, int) else pl.multiple_of(start, multiple)


def _region(ref, spec, shard, half):
    rows, cols, kind = spec
    hr = rows // 2
    if kind == "col":
        return ref.at[pl.ds(_aligned(half * hr, BF16_ROW_TILE), hr),
                      pl.ds(_aligned(shard * cols, LANES), cols)]
    return ref.at[pl.ds(_aligned(shard * rows + half * hr, BF16_ROW_TILE), hr), :]


def _position():
    x, y, c = lax.axis_index("x"), lax.axis_index("y"), lax.axis_index("c")
    chips = [(1 - x, y), (x, 1 - y), (1 - x, 1 - y)]
    return x, y, c, chips


def _shard_of(chip):
    return 2 * chip[0] + chip[1]


def _remote(src, dst, send_sem, recv_sem, to):
    return pltpu.make_async_remote_copy(src_ref=src, dst_ref=dst, send_sem=send_sem, recv_sem=recv_sem,
                                        device_id=to, device_id_type=MESH)


def _to_bf16_in_whole(name, w, spec, shard, rows):
    steps = spec[0] // rows

    def body(s_ref, w_ref, o_ref):
        del s_ref
        o_ref[...] = w_ref[...].astype(BF16)

    if spec[2] == "col":
        out_spec = pl.BlockSpec((rows, spec[1]), lambda i, s_ref: (i, s_ref[0]))
    else:
        out_spec = pl.BlockSpec((rows, spec[1]), lambda i, s_ref: (s_ref[0] * steps + i, 0))
    grid_spec = pltpu.PrefetchScalarGridSpec(
        num_scalar_prefetch=1, grid=(steps,),
        in_specs=[pl.BlockSpec((rows, spec[1]), lambda i, s_ref: (i, 0))], out_specs=out_spec)
    return pl.pallas_call(
        body, name=name, grid_spec=grid_spec, out_shape=_sds(_whole_shape(spec), BF16),
        compiler_params=_params(("parallel",)),
    )(shard, w)


def _all_gather_weights(wholes, conv_w):
    n_ici = N_MAT * 3

    def body(*refs):
        ins, outs, sems = refs[:N_MAT + 1], refs[N_MAT + 1:2 * N_MAT + 2], refs[2 * N_MAT + 2:]
        send_a, recv_a, send_b, recv_b, local_sems, send_w, recv_w = sems
        x, y, c, chips = _position()
        me = _shard_of((x, y))
        sibling = (x, y, 1 - c)
        cw_in, cw_out = ins[N_MAT], outs[N_MAT]

        def cw_cols(shard):
            return cw_out.at[:, pl.ds(_aligned(shard * CONV_W_COLS, LANES), CONV_W_COLS)]

        local = [pltpu.make_async_copy(cw_in, cw_cols(me), local_sems.at[0])]
        for cp in local:
            cp.start()

        sent = []
        for j, chip in enumerate(chips):
            cp = _remote(cw_in, cw_cols(me), send_w.at[j], recv_w.at[j], (*chip, c))
            cp.start()
            sent.append(cp)
        for i, spec in enumerate(MATRICES):
            mine = _region(outs[i], spec, me, c)
            for j, chip in enumerate(chips):
                k = 3 * i + j
                cp = _remote(mine, mine, send_a.at[k], recv_a.at[k], (*chip, c))
                cp.start()
                sent.append(cp)
        for i, spec in enumerate(MATRICES):
            for j, chip in enumerate(chips):
                k = 3 * i + j
                landed = _region(outs[i], spec, _shard_of(chip), c)
                _remote(landed, landed, send_a.at[k], recv_a.at[k], (*chip, c)).wait_recv()
                cp = _remote(landed, landed, send_b.at[k], recv_b.at[k], sibling)
                cp.start()
                sent.append(cp)
        for i, spec in enumerate(MATRICES):
            for j, chip in enumerate(chips):
                k = 3 * i + j
                other = _region(outs[i], spec, _shard_of(chip), 1 - c)
                _remote(other, other, send_b.at[k], recv_b.at[k], sibling).wait_recv()
        for j, chip in enumerate(chips):
            got = cw_cols(_shard_of(chip))
            _remote(got, got, send_w.at[j], recv_w.at[j], (*chip, c)).wait_recv()
        for cp in sent:
            cp.wait_send()
        for cp in local:
            cp.wait()

    out_shape = [_sds(_whole_shape(s), BF16) for s in MATRICES] + [_sds((3, D_MODEL), F32)]
    return pl.pallas_call(
        body, name="all_gather_weights",
        in_specs=[HBM_SPEC] * (N_MAT + 1), out_specs=[HBM_SPEC] * (N_MAT + 1), out_shape=out_shape,
        input_output_aliases={i: i for i in range(N_MAT)},
        scratch_shapes=[pltpu.SemaphoreType.DMA((n_ici,)), pltpu.SemaphoreType.DMA((n_ici,)),
                        pltpu.SemaphoreType.DMA((n_ici,)), pltpu.SemaphoreType.DMA((n_ici,)),
                        pltpu.SemaphoreType.DMA((1,)),
                        pltpu.SemaphoreType.DMA((3,)), pltpu.SemaphoreType.DMA((3,))],
    )(*wholes, conv_w)


def _pack_small(dg_mix, dg_ffn, dg_final, dconv_w, dsinks, loss_row):
    def body(a_ref, b_ref, c_ref, w_ref, s_ref, l_ref, o_ref):
        pad = jnp.zeros((1, D_MODEL - LANES), F32)
        o_ref[0:1, :] = a_ref[...]
        o_ref[1:2, :] = b_ref[...]
        o_ref[2:3, :] = c_ref[...]
        o_ref[3:6, :] = w_ref[...]
        o_ref[6:7, :] = jnp.concatenate([s_ref[...], pad], axis=1)
        o_ref[7:8, :] = jnp.concatenate([l_ref[...], pad], axis=1)

    return pl.pallas_call(
        body, name="pack_small", out_shape=_sds((SMALL_ROWS, D_MODEL), F32),
        compiler_params=_params(),
    )(dg_mix, dg_ffn, dg_final, dconv_w, dsinks, loss_row)


def _pair_exchange(dws, small):
    n_big = N_MAT * N_CHIPS

    def body(*refs):
        ins, outs, sems = refs[:N_MAT + 1], refs[N_MAT + 1:2 * N_MAT + 2], refs[2 * N_MAT + 2:]
        send_big, recv_big, send_small, recv_small, local_sem = sems
        x, y, c, _ = _position()
        sibling = (x, y, 1 - c)
        me = 4 * x + 2 * y + c
        small_in, small_out = ins[N_MAT], outs[N_MAT]
        own = pltpu.make_async_copy(small_in, small_out.at[me], local_sem)
        own.start()
        sent = []
        for r in range(1, N_DEV):
            flip = ((r >> 2) & 1, (r >> 1) & 1, r & 1)
            peer = tuple(1 - p if f else p for p, f in zip((x, y, c), flip))
            cp = _remote(small_in, small_out.at[me], send_small.at[r - 1], recv_small.at[r - 1], peer)
            cp.start()
            sent.append(cp)
        for i, spec in enumerate(MATRICES):
            for t in range(N_CHIPS):
                k = N_CHIPS * i + t
                cp = _remote(_region(ins[i], spec, t, 1 - c), outs[i].at[t], send_big.at[k], recv_big.at[k], sibling)
                cp.start()
                sent.append(cp)
        for i, spec in enumerate(MATRICES):
            for t in range(N_CHIPS):
                k = N_CHIPS * i + t
                _remote(outs[i].at[t], outs[i].at[t], send_big.at[k], recv_big.at[k], sibling).wait_recv()
        for r in range(1, N_DEV):
            flip = ((r >> 2) & 1, (r >> 1) & 1, r & 1)
            peer = tuple(1 - p if f else p for p, f in zip((x, y, c), flip))
            slot = small_out.at[4 * peer[0] + 2 * peer[1] + peer[2]]
            _remote(slot, slot, send_small.at[r - 1], recv_small.at[r - 1], peer).wait_recv()
        for cp in sent:
            cp.wait_send()
        own.wait()

    out_shape = [_sds((N_CHIPS, *_half_shape(s)), BF16) for s in MATRICES]
    out_shape.append(_sds((N_DEV, SMALL_ROWS, D_MODEL), F32))
    return pl.pallas_call(
        body, name="pair_exchange",
        in_specs=[HBM_SPEC] * (N_MAT + 1), out_specs=[HBM_SPEC] * (N_MAT + 1), out_shape=out_shape,
        scratch_shapes=[pltpu.SemaphoreType.DMA((n_big,)), pltpu.SemaphoreType.DMA((n_big,)),
                        pltpu.SemaphoreType.DMA((N_DEV - 1,)), pltpu.SemaphoreType.DMA((N_DEV - 1,)),
                        pltpu.SemaphoreType.DMA],
    )(*dws, small)


def _pair_sum(dws, got, place):
    def body(p_ref, *refs):
        t = pl.program_id(0)
        mine, theirs = refs[:N_MAT], refs[N_MAT:2 * N_MAT]
        outs, owns = refs[2 * N_MAT:3 * N_MAT], refs[3 * N_MAT:]
        for a, b, o, own in zip(mine, theirs, outs, owns):
            s = (a[...].astype(F32) + b[...].astype(F32)).astype(BF16)
            o[...] = s

            @pl.when(t == p_ref[1])
            def _():
                own[...] = s

    def mine_spec(spec):
        hr, cols = _half_shape(spec)
        if spec[2] == "col":
            return pl.BlockSpec((hr, cols), lambda t, p_ref: (p_ref[0], t))
        return pl.BlockSpec((hr, cols), lambda t, p_ref: (2 * t + p_ref[0], 0))

    def slot_spec(spec):
        return pl.BlockSpec((None, *_half_shape(spec)), lambda t, p_ref: (t, 0, 0))

    def own_spec(spec):
        return pl.BlockSpec((None, *_half_shape(spec)), lambda t, p_ref: (p_ref[1], 0, 0))

    slots = [_sds((N_CHIPS, *_half_shape(s)), BF16) for s in MATRICES]
    grid_spec = pltpu.PrefetchScalarGridSpec(
        num_scalar_prefetch=1, grid=(N_CHIPS,),
        in_specs=[mine_spec(s) for s in MATRICES] + [slot_spec(s) for s in MATRICES],
        out_specs=[slot_spec(s) for s in MATRICES] + [own_spec(s) for s in MATRICES])
    res = pl.pallas_call(
        body, name="pair_sum", grid_spec=grid_spec, out_shape=slots + slots,
        compiler_params=_params(("arbitrary",)),
    )(place, *dws, *got)
    return res[:N_MAT], res[N_MAT:]


def _chip_exchange(sums, slots):
    n_ici = N_MAT * 3

    def body(*refs):
        ins, outs, sems = refs[:N_MAT], refs[2 * N_MAT:3 * N_MAT], refs[3 * N_MAT:]
        send_sems, recv_sems = sems
        x, y, c, chips = _position()
        me = _shard_of((x, y))
        sent = []
        for i in range(N_MAT):
            for j, chip in enumerate(chips):
                k = 3 * i + j
                cp = _remote(ins[i].at[_shard_of(chip)], outs[i].at[me], send_sems.at[k], recv_sems.at[k], (*chip, c))
                cp.start()
                sent.append(cp)
        for i in range(N_MAT):
            for j, chip in enumerate(chips):
                k = 3 * i + j
                slot = outs[i].at[_shard_of(chip)]
                _remote(slot, slot, send_sems.at[k], recv_sems.at[k], (*chip, c)).wait_recv()
        for cp in sent:
            cp.wait_send()

    return pl.pallas_call(
        body, name="chip_exchange",
        in_specs=[HBM_SPEC] * (2 * N_MAT), out_specs=[HBM_SPEC] * N_MAT,
        out_shape=[_sds((N_CHIPS, *_half_shape(s)), BF16) for s in MATRICES],
        input_output_aliases={N_MAT + i: i for i in range(N_MAT)},
        scratch_shapes=[pltpu.SemaphoreType.DMA((n_ici,)), pltpu.SemaphoreType.DMA((n_ici,))],
    )(*sums, *slots)


def _chip_sum(slots, core):
    steps = 2

    def body(c_ref, *refs):
        del c_ref
        ins, outs = refs[:N_MAT], refs[N_MAT:]
        for a, o in zip(ins, outs):
            acc = a[0].astype(F32)
            for t in range(1, N_CHIPS):
                acc = acc + a[t].astype(F32)
            o[...] = acc

    def in_spec(spec):
        hr, cols = _half_shape(spec)
        return pl.BlockSpec((N_CHIPS, hr // steps, cols), lambda i, c_ref: (0, i, 0))

    def out_spec(spec):
        hr, cols = _half_shape(spec)
        return pl.BlockSpec((hr // steps, cols), lambda i, c_ref: (c_ref[0] * steps + i, 0))

    grid_spec = pltpu.PrefetchScalarGridSpec(
        num_scalar_prefetch=1, grid=(steps,),
        in_specs=[in_spec(s) for s in MATRICES], out_specs=[out_spec(s) for s in MATRICES])
    return pl.pallas_call(
        body, name="chip_sum", grid_spec=grid_spec,
        out_shape=[_sds((s[0], s[1]), F32) for s in MATRICES],
        compiler_params=_params(("parallel",)),
    )(core, *slots)


def _half_exchange(grads):
    def body(*refs):
        outs, sems = refs[N_MAT:2 * N_MAT], refs[2 * N_MAT:]
        send_sems, recv_sems = sems
        x, y, c, _ = _position()
        sibling = (x, y, 1 - c)

        def rows(i, h):
            hr = MATRICES[i][0] // 2
            return outs[i].at[pl.ds(_aligned(h * hr, 8), hr), :]

        sent = [_remote(rows(i, c), rows(i, c), send_sems.at[i], recv_sems.at[i], sibling) for i in range(N_MAT)]
        for cp in sent:
            cp.start()
        for i in range(N_MAT):
            _remote(rows(i, 1 - c), rows(i, 1 - c), send_sems.at[i], recv_sems.at[i], sibling).wait_recv()
        for cp in sent:
            cp.wait_send()

    return pl.pallas_call(
        body, name="half_exchange",
        in_specs=[HBM_SPEC] * N_MAT, out_specs=[HBM_SPEC] * N_MAT,
        out_shape=[_sds((s[0], s[1]), F32) for s in MATRICES],
        input_output_aliases={i: i for i in range(N_MAT)},
        scratch_shapes=[pltpu.SemaphoreType.DMA((N_MAT,)), pltpu.SemaphoreType.DMA((N_MAT,))],
    )(*grads)


def _small_sum(blocks):
    def body(b_ref, o_ref):
        acc = b_ref[0]
        for d in range(1, N_DEV):
            acc = acc + b_ref[d]
        o_ref[...] = acc

    return pl.pallas_call(
        body, name="small_sum", out_shape=_sds((SMALL_ROWS, D_MODEL), F32), compiler_params=_params(),
    )(blocks)


def _adamw(name, w, g, m, v, rows):
    def body(w_ref, g_ref, m_ref, v_ref, d_ref, nm_ref, nv_ref):
        g = g_ref[...]
        m = ADAM_B1 * m_ref[...] + (1.0 - ADAM_B1) * g
        v = ADAM_B2 * v_ref[...] + (1.0 - ADAM_B2) * jnp.square(g)
        m_hat = m / (1.0 - ADAM_B1 ** ADAM_STEP)
        v_hat = v / (1.0 - ADAM_B2 ** ADAM_STEP)
        d_ref[...] = -ADAM_LR * (m_hat / (jnp.sqrt(v_hat) + ADAM_EPS) + ADAM_WD * w_ref[...])
        nm_ref[...] = m
        nv_ref[...] = v

    spec = pl.BlockSpec((rows, w.shape[1]), lambda i: (i, 0))
    sd = _sds(w.shape, F32)
    return pl.pallas_call(
        body, name=name, grid=(w.shape[0] // rows,),
        in_specs=[spec] * 4, out_specs=[spec] * 3, out_shape=[sd, sd, sd],
        compiler_params=_params(("parallel",)),
    )(w, g, m, v)


MATRIX_NAMES = ("w_in", "w_conv_out", "w_attn_out", "w_o", "w_gate_up", "w_down")
WEIGHT_ORDER = ("g_mix", "w_in", "conv_w", "attn_sinks", "w_conv_out", "w_attn_out", "w_o", "g_ffn",
                "w_gate_up", "w_down", "g_final")


def kernel(x, g_mix, w_in, conv_w, attn_sinks, w_conv_out, w_attn_out, w_o, g_ffn, w_gate_up, w_down, g_final, loss_target, m_g_mix, m_w_in, m_conv_w, m_attn_sinks, m_w_conv_out, m_w_attn_out, m_w_o, m_g_ffn, m_w_gate_up, m_w_down, m_g_final, v_g_mix, v_w_in, v_conv_w, v_attn_sinks, v_w_conv_out, v_w_attn_out, v_w_o, v_g_ffn, v_w_gate_up, v_w_down, v_g_final):
    w = dict(g_mix=g_mix, w_in=w_in[0], conv_w=conv_w[0], attn_sinks=attn_sinks, w_conv_out=w_conv_out[0],
             w_attn_out=w_attn_out[0], w_o=w_o[0], g_ffn=g_ffn, w_gate_up=w_gate_up[0], w_down=w_down[0],
             g_final=g_final[None, :])
    m = dict(g_mix=m_g_mix, w_in=m_w_in[0], conv_w=m_conv_w[0], attn_sinks=m_attn_sinks,
             w_conv_out=m_w_conv_out[0], w_attn_out=m_w_attn_out[0], w_o=m_w_o[0], g_ffn=m_g_ffn,
             w_gate_up=m_w_gate_up[0], w_down=m_w_down[0], g_final=m_g_final[None, :])
    v = dict(g_mix=v_g_mix, w_in=v_w_in[0], conv_w=v_conv_w[0], attn_sinks=v_attn_sinks,
             w_conv_out=v_w_conv_out[0], w_attn_out=v_w_attn_out[0], w_o=v_w_o[0], g_ffn=v_g_ffn,
             w_gate_up=v_w_gate_up[0], w_down=v_w_down[0], g_final=v_g_final[None, :])
    shard = (2 * lax.axis_index("x") + lax.axis_index("y")).astype(jnp.int32)
    core = lax.axis_index("c").astype(jnp.int32)
    shard1, core1, place = shard.reshape((1,)), core.reshape((1,)), jnp.stack([core, shard])

    cast_rows = {"w_down": D_FF // N_CHIPS // 2}
    wholes = [_to_bf16_in_whole("cast_" + n, w[n], spec, shard1, cast_rows.get(n, 256))
              for n, spec in zip(MATRIX_NAMES, MATRICES)]
    *whole, conv_w_whole = _all_gather_weights(wholes, w["conv_w"])

    grad_x, small, dws = _local_step(x[0], loss_target[0], w["g_mix"], w["g_ffn"], w["g_final"],
                                     w["attn_sinks"], conv_w_whole, *whole)

    *got, small_blocks = _pair_exchange(dws, _pack_small(*small))
    sums, own_slots = _pair_sum(dws, got, place)
    mat_grads = _half_exchange(_chip_sum(_chip_exchange(sums, own_slots), core1))
    small_sum = _small_sum(small_blocks)

    g = dict(zip(MATRIX_NAMES, mat_grads))
    g["g_mix"] = small_sum[0:1, :]
    g["g_ffn"] = small_sum[1:2, :]
    g["g_final"] = small_sum[2:3, :]
    g["conv_w"] = lax.dynamic_slice(small_sum, (3, shard * CONV_W_COLS), (3, CONV_W_COLS))
    g["attn_sinks"] = small_sum[6:7, :N_HEADS]
    loss = small_sum[7, 0]

    adam_rows = {"w_in": 256, "w_gate_up": 256, "w_down": D_FF // N_CHIPS // 2}
    delta, new_m, new_v = {}, {}, {}
    for n in WEIGHT_ORDER:
        rows = adam_rows.get(n, w[n].shape[0])
        delta[n], new_m[n], new_v[n] = _adamw("adamw_" + n, w[n], g[n], m[n], v[n], rows)

    def shaped(vals):
        return [vals[n].reshape((D_MODEL,)) if n == "g_final" else
                (vals[n][None] if n in MATRIX_NAMES or n == "conv_w" else vals[n]) for n in WEIGHT_ORDER]

    return (loss, grad_x[None], *shaped(g), *shaped(delta), *shaped(new_m), *shaped(new_v))
```

```python
import functools
import math

import jax
import jax.numpy as jnp
from jax import lax
from jax.experimental import pallas as pl
from jax.experimental.pallas import tpu as pltpu

F32 = jnp.float32
BF16 = jnp.bfloat16

D_MODEL = 1024
SEQ = 2048
HEAD_DIM = 64
N_HEADS = 16
N_KV_HEADS = 4
GROUP = N_HEADS // N_KV_HEADS
D_ATTN = N_HEADS * HEAD_DIM
D_KV = N_KV_HEADS * HEAD_DIM
BLOCK = 128
ROT_DIM = HEAD_DIM // 4
ROPE_THETA = 500000.0
ATTN_SCALE = 1.0 / math.sqrt(HEAD_DIM)
NEG_INF = -1e30
D_FF = 2816
EPS = 1e-5
N_IN = 3 * D_MODEL + D_ATTN + 2 * D_KV + 2 * D_MODEL
COL_Q = 3 * D_MODEL
COL_K = COL_Q + D_ATTN
COL_V = COL_K + D_KV
COL_GC = COL_V + D_KV
COL_GA = COL_GC + D_MODEL

ADAM_LR = 0.001
ADAM_B1 = 0.9
ADAM_B2 = 0.999
ADAM_EPS = 1e-08
ADAM_WD = 0.01
ADAM_STEP = 10

N_CHIPS = 4
N_DEV = 8

V7X_VMEM_BYTES = 64 * 1024 * 1024
VMEM_LIMIT = (V7X_VMEM_BYTES * 3) // 4
LANES = 128
MESH = pl.DeviceIdType.MESH


def _params(semantics=None):
    return pltpu.CompilerParams(dimension_semantics=semantics, vmem_limit_bytes=VMEM_LIMIT)


def _sds(shape, dtype):
    return jax.ShapeDtypeStruct(shape, dtype)


HBM_SPEC = pl.BlockSpec(memory_space=pl.ANY)


def _pcall(body, name, grid, in_specs, out_specs, out_shape, operands, scratch=(), semantics=None, comm=None):
    if comm is None:
        return pl.pallas_call(
            body, name=name, grid=grid, in_specs=in_specs, out_specs=out_specs, out_shape=out_shape,
            scratch_shapes=list(scratch), compiler_params=_params(semantics))(*operands)
    multi = isinstance(out_shape, (list, tuple))
    o_specs = list(out_specs) if multi else [out_specs]
    o_shape = list(out_shape) if multi else [out_shape]
    n_in, n_out, n_scr = len(operands), len(o_shape), len(scratch)
    n_cin, n_cout = len(comm.operands), len(comm.out_shape)

    def hosted(*refs):
        ins, cins = refs[:n_in], refs[n_in:n_in + n_cin]
        o0 = n_in + n_cin
        outs, couts = refs[o0:o0 + n_out], refs[o0 + n_out:o0 + n_out + n_cout]
        s0 = o0 + n_out + n_cout
        scr, sems = refs[s0:s0 + n_scr], refs[s0 + n_scr:]
        first = last = None
        for axis, size in enumerate(grid):
            i = pl.program_id(axis)
            first = (i == 0) if first is None else first & (i == 0)
            last = (i == size - 1) if last is None else last & (i == size - 1)

        @pl.when(first)
        def _():
            comm.start(cins, couts, sems)

        body(*ins, *outs, *scr)

        @pl.when(last)
        def _():
            comm.finish(cins, couts, sems)

    res = pl.pallas_call(
        hosted, name=name, grid=grid,
        in_specs=list(in_specs) + [HBM_SPEC] * n_cin, out_specs=o_specs + [HBM_SPEC] * n_cout,
        out_shape=o_shape + list(comm.out_shape), scratch_shapes=list(scratch) + list(comm.sems),
        input_output_aliases={n_in + a: n_out + b for a, b in comm.aliases.items()},
        compiler_params=_params(("arbitrary",) * len(grid)))(*operands, *comm.operands)
    outs = list(res[:n_out])
    return (outs if multi else outs[0]), list(res[n_out:])


def _comm_call(name, comm):
    def body(*refs):
        n_cin, n_cout = len(comm.operands), len(comm.out_shape)
        cins, couts, sems = refs[:n_cin], refs[n_cin:n_cin + n_cout], refs[n_cin + n_cout:]
        comm.start(cins, couts, sems)
        comm.finish(cins, couts, sems)

    return list(pl.pallas_call(
        body, name=name, in_specs=[HBM_SPEC] * len(comm.operands), out_specs=[HBM_SPEC] * len(comm.out_shape),
        out_shape=list(comm.out_shape), scratch_shapes=list(comm.sems),
        input_output_aliases=dict(comm.aliases))(*comm.operands))


NN = ((1,), (0,))
NT = ((1,), (1,))
TN = ((0,), (0,))


def _matmul(name, a, b, dims, grid, a_spec, b_spec, o_spec, o_shape, o_dtype, res=None, res_spec=None, comm=None):
    nk = grid[2]

    def body(*refs):
        if res is None:
            a_ref, b_ref, o_ref = refs[:3]
            r_ref = None
            scratch = refs[3:]
        else:
            a_ref, b_ref, r_ref, o_ref = refs[:4]
            scratch = refs[4:]
        p = lax.dot_general(a_ref[...], b_ref[...], (dims, ((), ())), preferred_element_type=F32)

        def finish(acc):
            if r_ref is not None:
                acc = r_ref[...] + acc
            o_ref[...] = acc.astype(o_dtype)

        if nk == 1:
            finish(p)
        else:
            acc_ref = scratch[0]
            k = pl.program_id(2)

            @pl.when(k == 0)
            def _():
                acc_ref[...] = p

            @pl.when(k > 0)
            def _():
                acc_ref[...] += p

            @pl.when(k == nk - 1)
            def _():
                finish(acc_ref[...])

    operands = [a, b] if res is None else [a, b, res]
    in_specs = [a_spec, b_spec] if res is None else [a_spec, b_spec, res_spec]
    scratch = [pltpu.VMEM(o_spec.block_shape, F32)] if nk > 1 else []
    return _pcall(body, name, grid, in_specs, o_spec, _sds(o_shape, o_dtype), operands, scratch,
                  ("parallel", "parallel", "arbitrary"), comm)


def _mm_nn(name, a, b, bm, bn, o_dtype, res=None, comm=None):
    m, k = a.shape
    n = b.shape[1]
    return _matmul(
        name, a, b, NN, (m // bm, n // bn, 1),
        pl.BlockSpec((bm, k), lambda i, j, kk: (i, 0)),
        pl.BlockSpec((k, bn), lambda i, j, kk: (0, j)),
        pl.BlockSpec((bm, bn), lambda i, j, kk: (i, j)),
        (m, n), o_dtype, res,
        None if res is None else pl.BlockSpec((bm, bn), lambda i, j, kk: (i, j)), comm,
    )


def _mm_nt(name, a, b, bm, bn, bk, o_dtype, comm=None):
    m, k = a.shape
    n = b.shape[0]
    return _matmul(
        name, a, b, NT, (m // bm, n // bn, k // bk),
        pl.BlockSpec((bm, bk), lambda i, j, kk: (i, kk)),
        pl.BlockSpec((bn, bk), lambda i, j, kk: (j, kk)),
        pl.BlockSpec((bm, bn), lambda i, j, kk: (i, j)),
        (m, n), o_dtype, comm=comm,
    )


def _mm_tn(name, a, b, bm, bn, o_dtype):
    k, m = a.shape
    n = b.shape[1]
    return _matmul(
        name, a, b, TN, (m // bm, n // bn, 1),
        pl.BlockSpec((k, bm), lambda i, j, kk: (0, i)),
        pl.BlockSpec((k, bn), lambda i, j, kk: (0, j)),
        pl.BlockSpec((bm, bn), lambda i, j, kk: (i, j)),
        (m, n), o_dtype,
    )


ROWS = 256


def _row_spec(width, col=0):
    return pl.BlockSpec((ROWS, width), lambda i: (i, col))


def _full_spec(shape):
    return pl.BlockSpec(shape, lambda *_: (0,) * len(shape))


def _rms_norm(name, x, g):
    def body(x_ref, g_ref, h_ref):
        xf = x_ref[...]
        r = lax.rsqrt(jnp.mean(xf * xf, axis=-1, keepdims=True) + EPS)
        h_ref[...] = ((xf * r) * g_ref[...]).astype(BF16)

    return pl.pallas_call(
        body, name=name, grid=(SEQ // ROWS,),
        in_specs=[_row_spec(D_MODEL), _full_spec((1, D_MODEL))],
        out_specs=_row_spec(D_MODEL),
        out_shape=_sds((SEQ, D_MODEL), BF16),
        compiler_params=_params(("parallel",)),
    )(x, g)


CONV_COLS = 256


def _shift_rows(u, k):
    rows = lax.broadcasted_iota(jnp.int32, u.shape, 0)
    return jnp.where(rows >= k, pltpu.roll(u, k, axis=0), 0.0)


def _conv_fwd(proj, conv_w):
    nblk = D_MODEL // CONV_COLS

    def body(cb_ref, cc_ref, cx_ref, w_ref, y_ref):
        u = cc_ref[...] * cx_ref[...]
        w = w_ref[...]
        cv = w[0:1, :] * _shift_rows(u, 2) + w[1:2, :] * _shift_rows(u, 1) + w[2:3, :] * u
        y_ref[...] = (cb_ref[...] * cv).astype(BF16)

    def col(part):
        return pl.BlockSpec((SEQ, CONV_COLS), lambda j: (0, part * nblk + j))

    return pl.pallas_call(
        body, name="conv_fwd", grid=(nblk,),
        in_specs=[col(0), col(1), col(2), pl.BlockSpec((3, CONV_COLS), lambda j: (0, j))],
        out_specs=pl.BlockSpec((SEQ, CONV_COLS), lambda j: (0, j)),
        out_shape=_sds((SEQ, D_MODEL), BF16),
        compiler_params=_params(("parallel",)),
    )(proj, proj, proj, conv_w)


ROPE_COLS = 256


def _rope_tables():
    inv_freq = ROPE_THETA ** (-jnp.arange(0, ROT_DIM, 2, dtype=F32) / ROT_DIM)
    ang = jnp.arange(SEQ, dtype=F32)[:, None] * inv_freq[None, :]
    cos, sin = jnp.cos(ang), jnp.sin(ang)
    half = ROT_DIM // 2
    ones = jnp.ones((SEQ, HEAD_DIM - ROT_DIM), F32)
    zeros = jnp.zeros((SEQ, HEAD_DIM - ROT_DIM), F32)
    zh = jnp.zeros((SEQ, half), F32)
    c = jnp.concatenate([cos, cos, ones], axis=1)
    s_up = jnp.concatenate([-sin, zh, zeros], axis=1)
    s_dn = jnp.concatenate([zh, sin, zeros], axis=1)
    reps = ROPE_COLS // HEAD_DIM
    return tuple(jnp.tile(t, (1, reps)) for t in (c, s_up, s_dn))


def _rotate(t, c, s_up, s_dn):
    width = t.shape[1]
    half = ROT_DIM // 2
    return t * c + pltpu.roll(t, width - half, axis=1) * s_up + pltpu.roll(t, half, axis=1) * s_dn


def _rope_fwd(proj, tables, comm=None):
    ntile = (D_ATTN + 2 * D_KV) // ROPE_COLS
    first = COL_Q // ROPE_COLS

    def body(t_ref, c_ref, su_ref, sd_ref, o_ref):
        j = pl.program_id(1)
        t = t_ref[...]
        rot = _rotate(t, c_ref[...], su_ref[...], sd_ref[...])
        o_ref[...] = jnp.where(j < ntile - 1, rot, t).astype(BF16)

    tab = pl.BlockSpec((ROWS, ROPE_COLS), lambda i, j: (i, 0))
    return _pcall(
        body, "rope_fwd", (SEQ // ROWS, ntile),
        [pl.BlockSpec((ROWS, ROPE_COLS), lambda i, j: (i, first + j)), tab, tab, tab],
        pl.BlockSpec((ROWS, ROPE_COLS), lambda i, j: (i, j)),
        _sds((SEQ, D_ATTN + 2 * D_KV), BF16), [proj, *tables], (), ("parallel", "parallel"), comm)


N_QBLK = SEQ // BLOCK
KV_TILE = D_ATTN // D_KV


def _attn_specs():
    q = pl.BlockSpec((BLOCK, D_ATTN), lambda n: (n, 0))
    k_prev = pl.BlockSpec((BLOCK, D_KV), lambda n: (jnp.maximum(n - 1, 0), KV_TILE))
    k_cur = pl.BlockSpec((BLOCK, D_KV), lambda n: (n, KV_TILE))
    v_prev = pl.BlockSpec((BLOCK, D_KV), lambda n: (jnp.maximum(n - 1, 0), KV_TILE + 1))
    v_cur = pl.BlockSpec((BLOCK, D_KV), lambda n: (n, KV_TILE + 1))
    return [q, k_prev, k_cur, v_prev, v_cur]


def _band_mask(n):
    qi = lax.broadcasted_iota(jnp.int32, (BLOCK, 2 * BLOCK), 0)
    kj = lax.broadcasted_iota(jnp.int32, (BLOCK, 2 * BLOCK), 1)
    rel = qi + BLOCK - kj
    return (rel >= 0) & (rel < BLOCK) & ((kj >= BLOCK) | (n > 0))


HEADS_PER_TILE = LANES // HEAD_DIM


def _lane_half(shape, par):
    lane = lax.broadcasted_iota(jnp.int32, shape, 1)
    return (lane < HEAD_DIM) if par == 0 else (lane >= HEAD_DIM)


def _head_tiles(kv, h):
    tile = kv[:, (h // HEADS_PER_TILE) * LANES:(h // HEADS_PER_TILE + 1) * LANES].astype(F32)
    own = jnp.where(_lane_half(tile.shape, h % HEADS_PER_TILE), tile, 0.0)
    other = pltpu.roll(own, HEAD_DIM, axis=1)
    lo, hi = (own, other) if h % HEADS_PER_TILE == 0 else (other, own)
    return lo.astype(BF16), hi.astype(BF16)


def _head_softmax(q_tile, k_half, sink, mask):
    s = lax.dot_general(q_tile, k_half, (NT, ((), ())), preferred_element_type=F32) * ATTN_SCALE
    s = jnp.where(mask, s, NEG_INF)
    m = jnp.maximum(jnp.max(s, axis=-1, keepdims=True), sink)
    e = jnp.exp(s - m)
    es = jnp.exp(sink - m)
    inv = 1.0 / (jnp.sum(e, axis=-1, keepdims=True) + es)
    return e * inv, es * inv


def _attn_fwd(qkv, sinks, comm=None):
    def body(sink_ref, q_ref, kp_ref, kc_ref, vp_ref, vc_ref, o_ref):
        n = pl.program_id(0)
        mask = _band_mask(n)
        k = jnp.concatenate([kp_ref[...], kc_ref[...]], axis=0)
        v = jnp.concatenate([vp_ref[...], vc_ref[...]], axis=0)
        for h in range(N_KV_HEADS):
            k_halves = _head_tiles(k, h)
            v_halves = _head_tiles(v, h)
            for t in range(GROUP // HEADS_PER_TILE):
                tile = h * (GROUP // HEADS_PER_TILE) + t
                q_tile = q_ref[:, tile * LANES:(tile + 1) * LANES]
                acc = None
                for par in range(HEADS_PER_TILE):
                    sink = sink_ref[0, tile * HEADS_PER_TILE + par]
                    p, _ = _head_softmax(q_tile, k_halves[par], sink, mask)
                    o = jnp.dot(p.astype(BF16), v_halves[par], preferred_element_type=F32)
                    acc = o if acc is None else acc + o
                o_ref[:, tile * LANES:(tile + 1) * LANES] = acc.astype(BF16)

    return _pcall(
        body, "attn_fwd", (N_QBLK,),
        [pl.BlockSpec(memory_space=pltpu.SMEM)] + _attn_specs(),
        pl.BlockSpec((BLOCK, D_ATTN), lambda n: (n, 0)),
        _sds((SEQ, D_ATTN), BF16), [sinks, qkv, qkv, qkv, qkv, qkv], (), ("parallel",), comm)


def _branch_merge(conv_y, attn, w_co, w_ao, proj):
    bm, bn = 1024, 512

    def body(cy_ref, at_ref, wc_ref, wa_ref, gc_ref, ga_ref, co_ref, ao_ref, mg_ref):
        co = jnp.dot(cy_ref[...], wc_ref[...], preferred_element_type=F32)
        ao = jnp.dot(at_ref[...], wa_ref[...], preferred_element_type=F32)
        co_ref[...] = co
        ao_ref[...] = ao
        mg_ref[...] = (jax.nn.sigmoid(gc_ref[...]) * co + jax.nn.sigmoid(ga_ref[...]) * ao).astype(BF16)

    act = pl.BlockSpec((bm, D_MODEL), lambda i, j: (i, 0))
    wgt = pl.BlockSpec((D_MODEL, bn), lambda i, j: (0, j))
    out = pl.BlockSpec((bm, bn), lambda i, j: (i, j))
    return pl.pallas_call(
        body, name="branch_merge", grid=(SEQ // bm, D_MODEL // bn),
        in_specs=[act, act, wgt, wgt,
                  pl.BlockSpec((bm, bn), lambda i, j: (i, COL_GC // bn + j)),
                  pl.BlockSpec((bm, bn), lambda i, j: (i, COL_GA // bn + j))],
        out_specs=[out, out, out],
        out_shape=[_sds((SEQ, D_MODEL), F32), _sds((SEQ, D_MODEL), F32), _sds((SEQ, D_MODEL), BF16)],
        compiler_params=_params(("parallel", "parallel")),
    )(conv_y, attn, w_co, w_ao, proj, proj)


FF_ROWS = 128


def _swiglu_fwd(gu):
    def body(gu_ref, act_ref):
        g = gu_ref[:, :D_FF]
        act_ref[...] = (jax.nn.silu(g) * gu_ref[:, D_FF:]).astype(BF16)

    return pl.pallas_call(
        body, name="swiglu_fwd", grid=(SEQ // FF_ROWS,),
        in_specs=[pl.BlockSpec((FF_ROWS, 2 * D_FF), lambda i: (i, 0))],
        out_specs=pl.BlockSpec((FF_ROWS, D_FF), lambda i: (i, 0)),
        out_shape=_sds((SEQ, D_FF), BF16),
        compiler_params=_params(("parallel",)),
    )(gu)


def _loss_head(x3, g, target):
    def body(x_ref, g_ref, t_ref, dx_ref, dxb_ref, dg_ref, loss_ref):
        i = pl.program_id(0)
        xf = x_ref[...]
        r = lax.rsqrt(jnp.mean(xf * xf, axis=-1, keepdims=True) + EPS)
        xn = xf * r
        gg = g_ref[...]
        err = xn * gg - t_ref[...]
        part = 0.5 * jnp.sum(jnp.mean(err * err, axis=-1, keepdims=True), axis=0, keepdims=True)
        dy = err * (1.0 / D_MODEL)
        dxn = dy * gg
        dx = r * (dxn - xn * jnp.mean(dxn * xn, axis=-1, keepdims=True))
        dx_ref[...] = dx
        dxb_ref[...] = dx.astype(BF16)
        dg = jnp.sum(dy * xn, axis=0, keepdims=True)
        lane0 = lax.broadcasted_iota(jnp.int32, (1, LANES), 1) == 0
        lpart = jnp.where(lane0, part, 0.0)

        @pl.when(i == 0)
        def _():
            dg_ref[...] = dg
            loss_ref[...] = lpart

        @pl.when(i > 0)
        def _():
            dg_ref[...] += dg
            loss_ref[...] += lpart

    return pl.pallas_call(
        body, name="loss_head", grid=(SEQ // ROWS,),
        in_specs=[_row_spec(D_MODEL), _full_spec((1, D_MODEL)), _row_spec(D_MODEL)],
        out_specs=[_row_spec(D_MODEL), _row_spec(D_MODEL), _full_spec((1, D_MODEL)), _full_spec((1, LANES))],
        out_shape=[_sds((SEQ, D_MODEL), F32), _sds((SEQ, D_MODEL), BF16),
                   _sds((1, D_MODEL), F32), _sds((1, LANES), F32)],
        compiler_params=_params(("arbitrary",)),
    )(x3, g, target)


def _swiglu_bwd(dact, gu, comm=None):
    def body(da_ref, gu_ref, o_ref):
        g = gu_ref[:, :D_FF]
        up = gu_ref[:, D_FF:]
        da = da_ref[...]
        sg = jax.nn.sigmoid(g)
        o_ref[:, :D_FF] = (da * up * (sg * (1.0 + g * (1.0 - sg)))).astype(BF16)
        o_ref[:, D_FF:] = (da * (g * sg)).astype(BF16)

    return _pcall(
        body, "swiglu_bwd", (SEQ // FF_ROWS,),
        [pl.BlockSpec((FF_ROWS, D_FF), lambda i: (i, 0)), pl.BlockSpec((FF_ROWS, 2 * D_FF), lambda i: (i, 0))],
        pl.BlockSpec((FF_ROWS, 2 * D_FF), lambda i: (i, 0)),
        _sds((SEQ, 2 * D_FF), BF16), [dact, gu], (), ("parallel",), comm)


def _rms_norm_bwd(name, dh, x, g, dres, with_bf16, comm=None):
    def body(dh_ref, x_ref, g_ref, dr_ref, *outs):
        i = pl.program_id(0)
        dx_ref = outs[0]
        dg_ref = outs[-1]
        xf = x_ref[...]
        r = lax.rsqrt(jnp.mean(xf * xf, axis=-1, keepdims=True) + EPS)
        xn = xf * r
        dh = dh_ref[...]
        dxn = dh * g_ref[...]
        dx = dr_ref[...] + r * (dxn - xn * jnp.mean(dxn * xn, axis=-1, keepdims=True))
        dx_ref[...] = dx
        if with_bf16:
            outs[1][...] = dx.astype(BF16)
        dg = jnp.sum(dh * xn, axis=0, keepdims=True)

        @pl.when(i == 0)
        def _():
            dg_ref[...] = dg

        @pl.when(i > 0)
        def _():
            dg_ref[...] += dg

    row = _row_spec(D_MODEL)
    out_specs = [row] + ([row] if with_bf16 else []) + [_full_spec((1, D_MODEL))]
    out_shape = ([_sds((SEQ, D_MODEL), F32)] + ([_sds((SEQ, D_MODEL), BF16)] if with_bf16 else [])
                 + [_sds((1, D_MODEL), F32)])
    return _pcall(body, name, (SEQ // ROWS,), [row, row, _full_spec((1, D_MODEL)), row], out_specs, out_shape,
                  [dh, x, g, dres], (), ("arbitrary",), comm)


def _merge_bwd(dmerged, conv_out, attn_out, proj):
    def body(dm_ref, co_ref, ao_ref, gc_ref, ga_ref, dco_ref, dao_ref, dgc_ref, dga_ref):
        dm = dm_ref[...]
        sc = jax.nn.sigmoid(gc_ref[...])
        sa = jax.nn.sigmoid(ga_ref[...])
        dco_ref[...] = (dm * sc).astype(BF16)
        dao_ref[...] = (dm * sa).astype(BF16)
        dgc_ref[...] = (dm * co_ref[...] * (sc * (1.0 - sc))).astype(BF16)
        dga_ref[...] = (dm * ao_ref[...] * (sa * (1.0 - sa))).astype(BF16)

    half = D_MODEL // 2
    own = pl.BlockSpec((ROWS, half), lambda i, j: (i, j))
    sd = _sds((SEQ, D_MODEL), BF16)
    return pl.pallas_call(
        body, name="merge_bwd", grid=(SEQ // ROWS, 2),
        in_specs=[own, own, own,
                  pl.BlockSpec((ROWS, half), lambda i, j: (i, COL_GC // half + j)),
                  pl.BlockSpec((ROWS, half), lambda i, j: (i, COL_GA // half + j))],
        out_specs=[own, own, own, own], out_shape=[sd, sd, sd, sd],
        compiler_params=_params(("parallel", "parallel")),
    )(dmerged, conv_out, attn_out, proj, proj)


def _conv_bwd(dconv_y, proj, conv_w, comm=None):
    nblk = D_MODEL // CONV_COLS

    def body(dy_ref, cb_ref, cc_ref, cx_ref, w_ref, dcb_ref, dcc_ref, dcx_ref, dw_ref):
        cc = cc_ref[...]
        cx = cx_ref[...]
        u = cc * cx
        w = w_ref[...]
        u1 = _shift_rows(u, 1)
        u2 = _shift_rows(u, 2)
        cv = w[0:1, :] * u2 + w[1:2, :] * u1 + w[2:3, :] * u
        dy = dy_ref[...]
        dcb_ref[...] = (dy * cv).astype(BF16)
        dcv = dy * cb_ref[...]
        rows = lax.broadcasted_iota(jnp.int32, dcv.shape, 0)
        up1 = jnp.where(rows < SEQ - 1, pltpu.roll(dcv, SEQ - 1, axis=0), 0.0)
        up2 = jnp.where(rows < SEQ - 2, pltpu.roll(dcv, SEQ - 2, axis=0), 0.0)
        du = w[2:3, :] * dcv + w[1:2, :] * up1 + w[0:1, :] * up2
        dcc_ref[...] = (du * cx).astype(BF16)
        dcx_ref[...] = (du * cc).astype(BF16)
        dw_ref[...] = jnp.concatenate(
            [jnp.sum(dcv * u2, axis=0, keepdims=True),
             jnp.sum(dcv * u1, axis=0, keepdims=True),
             jnp.sum(dcv * u, axis=0, keepdims=True)], axis=0)

    def col(part):
        return pl.BlockSpec((SEQ, CONV_COLS), lambda j: (0, part * nblk + j))

    own = pl.BlockSpec((SEQ, CONV_COLS), lambda j: (0, j))
    wsp = pl.BlockSpec((3, CONV_COLS), lambda j: (0, j))
    sd = _sds((SEQ, D_MODEL), BF16)
    return _pcall(
        body, "conv_bwd", (nblk,), [own, col(0), col(1), col(2), wsp], [own, own, own, wsp],
        [sd, sd, sd, _sds((3, D_MODEL), F32)], [dconv_y, proj, proj, proj, conv_w], (), ("parallel",), comm)


def _attn_bwd(qkv, dattn, sinks, tables, comm=None):
    c_t, su_t, sd_t = tables

    def body(sink_ref, q_ref, kp_ref, kc_ref, vp_ref, vc_ref, do_ref, c_ref, su_ref, sd_ref,
             dq_ref, dkp_ref, dkc_ref, dvp_ref, dvc_ref, ds_ref):
        n = pl.program_id(0)
        mask = _band_mask(n)
        k = jnp.concatenate([kp_ref[...], kc_ref[...]], axis=0)
        v = jnp.concatenate([vp_ref[...], vc_ref[...]], axis=0)
        lane = lax.broadcasted_iota(jnp.int32, (1, LANES), 1)
        dsink = jnp.zeros((1, LANES), F32)
        c, su, sd = c_ref[:, :LANES], su_ref[:, :LANES], sd_ref[:, :LANES]
        dk_tiles = [None] * (N_KV_HEADS // HEADS_PER_TILE)
        dv_tiles = [None] * (N_KV_HEADS // HEADS_PER_TILE)
        for h in range(N_KV_HEADS):
            k_halves = _head_tiles(k, h)
            v_halves = _head_tiles(v, h)
            dk_par = [None] * HEADS_PER_TILE
            dv_par = [None] * HEADS_PER_TILE
            for t in range(GROUP // HEADS_PER_TILE):
                tile = h * (GROUP // HEADS_PER_TILE) + t
                q_tile = q_ref[:, tile * LANES:(tile + 1) * LANES]
                do_tile = do_ref[:, tile * LANES:(tile + 1) * LANES]
                dq_tile = None
                for par in range(HEADS_PER_TILE):
                    head = tile * HEADS_PER_TILE + par
                    p, p_sink = _head_softmax(q_tile, k_halves[par], sink_ref[0, head], mask)
                    dp = lax.dot_general(do_tile, v_halves[par], (NT, ((), ())), preferred_element_type=F32)
                    delta = jnp.sum(p * dp, axis=-1, keepdims=True)
                    ds = (p * (dp - delta) * ATTN_SCALE).astype(BF16)
                    dq = jnp.dot(ds, k_halves[par], preferred_element_type=F32)
                    dq_tile = dq if dq_tile is None else dq_tile + dq
                    dk = lax.dot_general(ds, q_tile, (TN, ((), ())), preferred_element_type=F32)
                    dv = lax.dot_general(p.astype(BF16), do_tile, (TN, ((), ())), preferred_element_type=F32)
                    dk_par[par] = dk if dk_par[par] is None else dk_par[par] + dk
                    dv_par[par] = dv if dv_par[par] is None else dv_par[par] + dv
                    val = -jnp.sum(p_sink * delta, axis=0, keepdims=True)
                    dsink = dsink + jnp.where(lane == head, val, 0.0)
                dq_ref[:, tile * LANES:(tile + 1) * LANES] = _rotate(dq_tile, c, -su, -sd).astype(BF16)
            own = h % HEADS_PER_TILE
            for par_grads, tiles in ((dk_par, dk_tiles), (dv_par, dv_tiles)):
                shifted = pltpu.roll(par_grads[1 - own], HEAD_DIM, axis=1)
                total = jnp.where(_lane_half(shifted.shape, own), par_grads[own] + shifted, 0.0)
                i = h // HEADS_PER_TILE
                tiles[i] = total if tiles[i] is None else tiles[i] + total
        for i in range(N_KV_HEADS // HEADS_PER_TILE):
            cols = slice(i * LANES, (i + 1) * LANES)
            dkp_ref[:, cols] = dk_tiles[i][:BLOCK, :]
            dkc_ref[:, cols] = dk_tiles[i][BLOCK:, :]
            dvp_ref[:, cols] = dv_tiles[i][:BLOCK, :]
            dvc_ref[:, cols] = dv_tiles[i][BLOCK:, :]

        @pl.when(n == 0)
        def _():
            ds_ref[...] = dsink

        @pl.when(n > 0)
        def _():
            ds_ref[...] += dsink

    blk = pl.BlockSpec((BLOCK, D_KV), lambda n: (n, 0))
    tab = pl.BlockSpec((BLOCK, ROPE_COLS), lambda n: (n, 0))
    kv = _sds((SEQ, D_KV), F32)
    return _pcall(
        body, "attn_bwd", (N_QBLK,),
        [pl.BlockSpec(memory_space=pltpu.SMEM)] + _attn_specs()
        + [pl.BlockSpec((BLOCK, D_ATTN), lambda n: (n, 0)), tab, tab, tab],
        [pl.BlockSpec((BLOCK, D_ATTN), lambda n: (n, 0)), blk, blk, blk, blk, _full_spec((1, LANES))],
        [_sds((SEQ, D_ATTN), BF16), kv, kv, kv, kv, _sds((1, LANES), F32)],
        [sinks, qkv, qkv, qkv, qkv, qkv, dattn, c_t, su_t, sd_t], (), ("arbitrary",), comm)


def _kv_grad_combine(dk_prev, dk_cur, dv_prev, dv_cur, tables):
    def body(kp_ref, kc_ref, vp_ref, vc_ref, c_ref, su_ref, sd_ref, o_ref):
        m = pl.program_id(0)
        has_next = m < N_QBLK - 1
        dk = kc_ref[...] + jnp.where(has_next, kp_ref[...], 0.0)
        dv = vc_ref[...] + jnp.where(has_next, vp_ref[...], 0.0)
        o_ref[:, :D_KV] = _rotate(dk, c_ref[...], -su_ref[...], -sd_ref[...]).astype(BF16)
        o_ref[:, D_KV:] = dv.astype(BF16)

    cur = pl.BlockSpec((BLOCK, D_KV), lambda m: (m, 0))
    nxt = pl.BlockSpec((BLOCK, D_KV), lambda m: (jnp.minimum(m + 1, N_QBLK - 1), 0))
    return pl.pallas_call(
        body, name="kv_grad_combine", grid=(N_QBLK,),
        in_specs=[nxt, cur, nxt, cur, cur, cur, cur],
        out_specs=pl.BlockSpec((BLOCK, 2 * D_KV), lambda m: (m, 0)),
        out_shape=_sds((SEQ, 2 * D_KV), BF16),
        compiler_params=_params(("parallel",)),
    )(dk_prev, dk_cur, dv_prev, dv_cur, *tables)


MATRICES = {
    "w_in": (D_MODEL, N_IN // N_CHIPS, "col"),
    "w_conv_out": (D_MODEL // N_CHIPS, D_MODEL, "row"),
    "w_attn_out": (D_MODEL // N_CHIPS, D_MODEL, "row"),
    "w_o": (D_MODEL // N_CHIPS, D_MODEL, "row"),
    "w_gate_up": (D_MODEL, 2 * D_FF // N_CHIPS, "col"),
    "w_down": (D_FF // N_CHIPS, D_MODEL, "row"),
}
BF16_ROW_TILE = 16
CONV_W_COLS = D_MODEL // N_CHIPS
SMALL_ROWS = 8


def _whole_shape(spec):
    rows, cols, kind = spec
    return (rows, cols * N_CHIPS) if kind == "col" else (rows * N_CHIPS, cols)


def _half_shape(spec):
    return (spec[0] // 2, spec[1])


def _aligned(start, multiple):
    return start if isinstance(start, int) else pl.multiple_of(start, multiple)


def _region(ref, spec, shard, half, part=0, parts=1):
    rows, cols, kind = spec
    hr = rows // 2
    n = hr // parts
    if kind == "col":
        return ref.at[pl.ds(_aligned(half * hr + part * n, BF16_ROW_TILE), n),
                      pl.ds(_aligned(shard * cols, LANES), cols)]
    return ref.at[pl.ds(_aligned(shard * rows + half * hr + part * n, BF16_ROW_TILE), n), :]


def _position():
    x, y, c = lax.axis_index("x"), lax.axis_index("y"), lax.axis_index("c")
    chips = [(1 - x, y), (x, 1 - y), (1 - x, 1 - y)]
    return x, y, c, chips


def _shard_of(chip):
    return 2 * chip[0] + chip[1]


def _remote(src, dst, send_sem, recv_sem, to):
    return pltpu.make_async_remote_copy(src_ref=src, dst_ref=dst, send_sem=send_sem, recv_sem=recv_sem,
                                        device_id=to, device_id_type=MESH)


def _to_bf16_in_whole(name, w, spec, shard, rows):
    steps = spec[0] // rows

    def body(s_ref, w_ref, o_ref):
        del s_ref
        o_ref[...] = w_ref[...].astype(BF16)

    if spec[2] == "col":
        out_spec = pl.BlockSpec((rows, spec[1]), lambda i, s_ref: (i, s_ref[0]))
    else:
        out_spec = pl.BlockSpec((rows, spec[1]), lambda i, s_ref: (s_ref[0] * steps + i, 0))
    grid_spec = pltpu.PrefetchScalarGridSpec(
        num_scalar_prefetch=1, grid=(steps,),
        in_specs=[pl.BlockSpec((rows, spec[1]), lambda i, s_ref: (i, 0))], out_specs=out_spec)
    return pl.pallas_call(
        body, name=name, grid_spec=grid_spec, out_shape=_sds(_whole_shape(spec), BF16),
        compiler_params=_params(("parallel",)),
    )(shard, w)


class _Gather:
    def __init__(self, wholes, pieces, conv_w=None):
        self.pieces = pieces
        self.n = len(wholes)
        self.with_conv_w = conv_w is not None
        self.operands = list(wholes) + ([conv_w] if self.with_conv_w else [])
        self.out_shape = [_sds(w.shape, w.dtype) for w in wholes]
        if self.with_conv_w:
            self.out_shape.append(_sds((3, D_MODEL), F32))
        self.aliases = {i: i for i in range(self.n)}
        n_ici = 3 * len(pieces)
        self.sems = [pltpu.SemaphoreType.DMA((n_ici,))] * 4
        if self.with_conv_w:
            self.sems += [pltpu.SemaphoreType.DMA((1,)), pltpu.SemaphoreType.DMA((3,)), pltpu.SemaphoreType.DMA((3,))]

    def _conv_w(self, cins, couts, sems, with_recvs):
        cw_in, cw_out = cins[self.n], couts[self.n]
        x, y, c, chips = _position()

        def cols(shard):
            return cw_out.at[:, pl.ds(_aligned(shard * CONV_W_COLS, LANES), CONV_W_COLS)]

        me = _shard_of((x, y))
        local = pltpu.make_async_copy(cw_in, cols(me), sems[4].at[0])
        sends = [_remote(cw_in, cols(me), sems[5].at[j], sems[6].at[j], (*chip, c)) for j, chip in enumerate(chips)]
        if not with_recvs:
            return local, sends, []
        recvs = [_remote(cols(_shard_of(chip)), cols(_shard_of(chip)), sems[5].at[j], sems[6].at[j], (*chip, c))
                 for j, chip in enumerate(chips)]
        return local, sends, recvs

    def start(self, cins, couts, sems):
        x, y, c, chips = _position()
        me = _shard_of((x, y))
        if self.with_conv_w:
            local, sends, _ = self._conv_w(cins, couts, sems, False)
            local.start()
            for cp in sends:
                cp.start()
        for p, (i, spec, part, parts) in enumerate(self.pieces):
            mine = _region(couts[i], spec, me, c, part, parts)
            for j, chip in enumerate(chips):
                _remote(mine, mine, sems[0].at[3 * p + j], sems[1].at[3 * p + j], (*chip, c)).start()

    def finish(self, cins, couts, sems):
        x, y, c, chips = _position()
        me = _shard_of((x, y))
        sibling = (x, y, 1 - c)
        send_a, recv_a, send_b, recv_b = sems[:4]
        passed = []
        for p, (i, spec, part, parts) in enumerate(self.pieces):
            for j, chip in enumerate(chips):
                k = 3 * p + j
                landed = _region(couts[i], spec, _shard_of(chip), c, part, parts)
                _remote(landed, landed, send_a.at[k], recv_a.at[k], (*chip, c)).wait_recv()
                cp = _remote(landed, landed, send_b.at[k], recv_b.at[k], sibling)
                cp.start()
                passed.append(cp)
        for p, (i, spec, part, parts) in enumerate(self.pieces):
            mine = _region(couts[i], spec, me, c, part, parts)
            for j, chip in enumerate(chips):
                k = 3 * p + j
                other = _region(couts[i], spec, _shard_of(chip), 1 - c, part, parts)
                _remote(other, other, send_b.at[k], recv_b.at[k], sibling).wait_recv()
                _remote(mine, mine, send_a.at[k], recv_a.at[k], (*chip, c)).wait_send()
        for cp in passed:
            cp.wait_send()
        if self.with_conv_w:
            local, sends, recvs = self._conv_w(cins, couts, sems, True)
            for cp in recvs:
                cp.wait_recv()
            for cp in sends:
                cp.wait_send()
            local.wait()


def _pack_small(dg_mix, dg_ffn, dg_final, dconv_w, dsinks, loss_row):
    def body(a_ref, b_ref, c_ref, w_ref, s_ref, l_ref, o_ref):
        pad = jnp.zeros((1, D_MODEL - LANES), F32)
        o_ref[0:1, :] = a_ref[...]
        o_ref[1:2, :] = b_ref[...]
        o_ref[2:3, :] = c_ref[...]
        o_ref[3:6, :] = w_ref[...]
        o_ref[6:7, :] = jnp.concatenate([s_ref[...], pad], axis=1)
        o_ref[7:8, :] = jnp.concatenate([l_ref[...], pad], axis=1)

    return pl.pallas_call(
        body, name="pack_small", out_shape=_sds((SMALL_ROWS, D_MODEL), F32),
        compiler_params=_params(),
    )(dg_mix, dg_ffn, dg_final, dconv_w, dsinks, loss_row)


class _Pair:
    def __init__(self, dws, specs):
        self.specs = specs
        self.operands = list(dws)
        self.out_shape = [_sds((N_CHIPS, *_half_shape(s)), BF16) for s in specs]
        self.aliases = {}
        n = N_CHIPS * len(specs)
        self.sems = [pltpu.SemaphoreType.DMA((n,)), pltpu.SemaphoreType.DMA((n,))]

    def _copies(self, cins, couts, sems):
        x, y, c, _ = _position()
        sibling = (x, y, 1 - c)
        for i, spec in enumerate(self.specs):
            for t in range(N_CHIPS):
                k = N_CHIPS * i + t
                yield _remote(_region(cins[i], spec, t, 1 - c), couts[i].at[t], sems[0].at[k], sems[1].at[k], sibling)

    def start(self, cins, couts, sems):
        for cp in self._copies(cins, couts, sems):
            cp.start()

    def finish(self, cins, couts, sems):
        for cp in self._copies(cins, couts, sems):
            cp.wait()


class _SmallAllToAll:
    def __init__(self, small):
        self.operands = [small]
        self.out_shape = [_sds((N_DEV, SMALL_ROWS, D_MODEL), F32)]
        self.aliases = {}
        self.sems = [pltpu.SemaphoreType.DMA((N_DEV - 1,)), pltpu.SemaphoreType.DMA((N_DEV - 1,)),
                     pltpu.SemaphoreType.DMA((1,))]

    def _copies(self, cins, couts, sems):
        x, y, c, _ = _position()
        me = 4 * x + 2 * y + c
        out = []
        for r in range(1, N_DEV):
            flip = ((r >> 2) & 1, (r >> 1) & 1, r & 1)
            peer = tuple(1 - p if f else p for p, f in zip((x, y, c), flip))
            theirs = couts[0].at[4 * peer[0] + 2 * peer[1] + peer[2]]
            out.append((_remote(cins[0], couts[0].at[me], sems[0].at[r - 1], sems[1].at[r - 1], peer),
                        _remote(theirs, theirs, sems[0].at[r - 1], sems[1].at[r - 1], peer)))
        return pltpu.make_async_copy(cins[0], couts[0].at[me], sems[2].at[0]), out

    def start(self, cins, couts, sems):
        own, copies = self._copies(cins, couts, sems)
        own.start()
        for send, _ in copies:
            send.start()

    def finish(self, cins, couts, sems):
        own, copies = self._copies(cins, couts, sems)
        for send, recv in copies:
            recv.wait_recv()
            send.wait_send()
        own.wait()


class _Both:
    def __init__(self, a, b):
        self.a, self.b = a, b
        self.operands = list(a.operands) + list(b.operands)
        self.out_shape = list(a.out_shape) + list(b.out_shape)
        self.aliases = dict(a.aliases)
        self.aliases.update({len(a.operands) + k: len(a.out_shape) + v for k, v in b.aliases.items()})
        self.sems = list(a.sems) + list(b.sems)

    def _split(self, cins, couts, sems):
        na, ma, sa = len(self.a.operands), len(self.a.out_shape), len(self.a.sems)
        return (cins[:na], couts[:ma], sems[:sa]), (cins[na:], couts[ma:], sems[sa:])

    def start(self, cins, couts, sems):
        for plan, args in zip((self.a, self.b), self._split(cins, couts, sems)):
            plan.start(*args)

    def finish(self, cins, couts, sems):
        for plan, args in zip((self.a, self.b), self._split(cins, couts, sems)):
            plan.finish(*args)


def _pair_sum(name, specs, dws, got, place):
    n_mat = len(specs)

    def body(p_ref, *refs):
        t = pl.program_id(0)
        mine, theirs = refs[:n_mat], refs[n_mat:2 * n_mat]
        outs, owns = refs[2 * n_mat:3 * n_mat], refs[3 * n_mat:]
        for a, b, o, own in zip(mine, theirs, outs, owns):
            s = (a[...].astype(F32) + b[...].astype(F32)).astype(BF16)
            o[...] = s

            @pl.when(t == p_ref[1])
            def _():
                own[...] = s

    def mine_spec(spec):
        hr, cols = _half_shape(spec)
        if spec[2] == "col":
            return pl.BlockSpec((hr, cols), lambda t, p_ref: (p_ref[0], t))
        return pl.BlockSpec((hr, cols), lambda t, p_ref: (2 * t + p_ref[0], 0))

    def slot_spec(spec):
        return pl.BlockSpec((None, *_half_shape(spec)), lambda t, p_ref: (t, 0, 0))

    def own_spec(spec):
        return pl.BlockSpec((None, *_half_shape(spec)), lambda t, p_ref: (p_ref[1], 0, 0))

    slots = [_sds((N_CHIPS, *_half_shape(s)), BF16) for s in specs]
    grid_spec = pltpu.PrefetchScalarGridSpec(
        num_scalar_prefetch=1, grid=(N_CHIPS,),
        in_specs=[mine_spec(s) for s in specs] + [slot_spec(s) for s in specs],
        out_specs=[slot_spec(s) for s in specs] + [own_spec(s) for s in specs])
    res = pl.pallas_call(
        body, name=name, grid_spec=grid_spec, out_shape=slots + slots,
        compiler_params=_params(("arbitrary",)),
    )(place, *dws, *got)
    return list(res[:n_mat]), list(res[n_mat:])


class _ChipExchange:
    def __init__(self, sums, slots):
        self.n = len(sums)
        self.operands = list(sums) + list(slots)
        self.out_shape = [_sds(s.shape, s.dtype) for s in slots]
        self.aliases = {self.n + i: i for i in range(self.n)}
        self.sems = [pltpu.SemaphoreType.DMA((3 * self.n,)), pltpu.SemaphoreType.DMA((3 * self.n,))]

    def _copies(self, cins, couts, sems):
        x, y, c, chips = _position()
        me = _shard_of((x, y))
        for i in range(self.n):
            for j, chip in enumerate(chips):
                k = 3 * i + j
                theirs = couts[i].at[_shard_of(chip)]
                yield (_remote(cins[i].at[_shard_of(chip)], couts[i].at[me], sems[0].at[k], sems[1].at[k], (*chip, c)),
                       _remote(theirs, theirs, sems[0].at[k], sems[1].at[k], (*chip, c)))

    def start(self, cins, couts, sems):
        for send, _ in self._copies(cins, couts, sems):
            send.start()

    def finish(self, cins, couts, sems):
        for send, recv in self._copies(cins, couts, sems):
            recv.wait_recv()
            send.wait_send()


def _chip_sum(name, specs, slots, core):
    steps = 2
    n_mat = len(specs)

    def body(c_ref, *refs):
        del c_ref
        ins, outs = refs[:n_mat], refs[n_mat:]
        for a, o in zip(ins, outs):
            acc = a[0].astype(F32)
            for t in range(1, N_CHIPS):
                acc = acc + a[t].astype(F32)
            o[...] = acc

    def in_spec(spec):
        hr, cols = _half_shape(spec)
        return pl.BlockSpec((N_CHIPS, hr // steps, cols), lambda i, c_ref: (0, i, 0))

    def out_spec(spec):
        hr, cols = _half_shape(spec)
        return pl.BlockSpec((hr // steps, cols), lambda i, c_ref: (c_ref[0] * steps + i, 0))

    grid_spec = pltpu.PrefetchScalarGridSpec(
        num_scalar_prefetch=1, grid=(steps,),
        in_specs=[in_spec(s) for s in specs], out_specs=[out_spec(s) for s in specs])
    return list(pl.pallas_call(
        body, name=name, grid_spec=grid_spec,
        out_shape=[_sds((s[0], s[1]), F32) for s in specs],
        compiler_params=_params(("parallel",)),
    )(core, *slots))


class _HalfExchange:
    def __init__(self, grads, specs):
        self.specs = specs
        self.operands = list(grads)
        self.out_shape = [_sds(g.shape, g.dtype) for g in grads]
        self.aliases = {i: i for i in range(len(grads))}
        self.sems = [pltpu.SemaphoreType.DMA((len(grads),)), pltpu.SemaphoreType.DMA((len(grads),))]

    def _copies(self, couts, sems):
        x, y, c, _ = _position()
        sibling = (x, y, 1 - c)
        for i, spec in enumerate(self.specs):
            hr = spec[0] // 2
            mine = couts[i].at[pl.ds(_aligned(c * hr, 8), hr), :]
            theirs = couts[i].at[pl.ds(_aligned((1 - c) * hr, 8), hr), :]
            yield (_remote(mine, mine, sems[0].at[i], sems[1].at[i], sibling),
                   _remote(theirs, theirs, sems[0].at[i], sems[1].at[i], sibling))

    def start(self, cins, couts, sems):
        for send, _ in self._copies(couts, sems):
            send.start()

    def finish(self, cins, couts, sems):
        for send, recv in self._copies(couts, sems):
            recv.wait_recv()
            send.wait_send()


def _small_sum(blocks):
    def body(b_ref, o_ref):
        acc = b_ref[0]
        for d in range(1, N_DEV):
            acc = acc + b_ref[d]
        o_ref[...] = acc

    return pl.pallas_call(
        body, name="small_sum", out_shape=_sds((SMALL_ROWS, D_MODEL), F32), compiler_params=_params(),
    )(blocks)


def _adamw(name, params, steps, comm=None):
    n = len(params)

    def body(*refs):
        for p in range(n):
            w_ref, g_ref, m_ref, v_ref = refs[4 * p:4 * p + 4]
            d_ref, nm_ref, nv_ref = refs[4 * n + 3 * p:4 * n + 3 * p + 3]
            g = g_ref[...]
            m = ADAM_B1 * m_ref[...] + (1.0 - ADAM_B1) * g
            v = ADAM_B2 * v_ref[...] + (1.0 - ADAM_B2) * jnp.square(g)
            m_hat = m / (1.0 - ADAM_B1 ** ADAM_STEP)
            v_hat = v / (1.0 - ADAM_B2 ** ADAM_STEP)
            d_ref[...] = -ADAM_LR * (m_hat / (jnp.sqrt(v_hat) + ADAM_EPS) + ADAM_WD * w_ref[...])
            nm_ref[...] = m
            nv_ref[...] = v

    in_specs, out_specs, out_shape, operands = [], [], [], []
    for w, g, m, v in params:
        spec = pl.BlockSpec((w.shape[0] // steps, w.shape[1]), lambda i: (i, 0))
        in_specs += [spec] * 4
        out_specs += [spec] * 3
        out_shape += [_sds(w.shape, F32)] * 3
        operands += [w, g, m, v]
    res = _pcall(body, name, (steps,), in_specs, out_specs, out_shape, operands, (), ("parallel",), comm)
    outs, extra = res if comm is not None else (res, None)
    triples = [tuple(outs[3 * p:3 * p + 3]) for p in range(n)]
    return triples if comm is None else (triples, extra)


MATRIX_NAMES = tuple(MATRICES)
WEIGHT_ORDER = ("g_mix", "w_in", "conv_w", "attn_sinks", "w_conv_out", "w_attn_out", "w_o", "g_ffn",
                "w_gate_up", "w_down", "g_final")


def kernel(x, g_mix, w_in, conv_w, attn_sinks, w_conv_out, w_attn_out, w_o, g_ffn, w_gate_up, w_down, g_final, loss_target, m_g_mix, m_w_in, m_conv_w, m_attn_sinks, m_w_conv_out, m_w_attn_out, m_w_o, m_g_ffn, m_w_gate_up, m_w_down, m_g_final, v_g_mix, v_w_in, v_conv_w, v_attn_sinks, v_w_conv_out, v_w_attn_out, v_w_o, v_g_ffn, v_w_gate_up, v_w_down, v_g_final):
    w = dict(g_mix=g_mix, w_in=w_in[0], conv_w=conv_w[0], attn_sinks=attn_sinks, w_conv_out=w_conv_out[0],
             w_attn_out=w_attn_out[0], w_o=w_o[0], g_ffn=g_ffn, w_gate_up=w_gate_up[0], w_down=w_down[0],
             g_final=g_final[None, :])
    m = dict(g_mix=m_g_mix, w_in=m_w_in[0], conv_w=m_conv_w[0], attn_sinks=m_attn_sinks,
             w_conv_out=m_w_conv_out[0], w_attn_out=m_w_attn_out[0], w_o=m_w_o[0], g_ffn=m_g_ffn,
             w_gate_up=m_w_gate_up[0], w_down=m_w_down[0], g_final=m_g_final[None, :])
    v = dict(g_mix=v_g_mix, w_in=v_w_in[0], conv_w=v_conv_w[0], attn_sinks=v_attn_sinks,
             w_conv_out=v_w_conv_out[0], w_attn_out=v_w_attn_out[0], w_o=v_w_o[0], g_ffn=v_g_ffn,
             w_gate_up=v_w_gate_up[0], w_down=v_w_down[0], g_final=v_g_final[None, :])
    shard = (2 * lax.axis_index("x") + lax.axis_index("y")).astype(jnp.int32)
    core = lax.axis_index("c").astype(jnp.int32)
    shard1, core1, place = shard.reshape((1,)), core.reshape((1,)), jnp.stack([core, shard])
    spec = MATRICES
    xs, target, sinks = x[0], loss_target[0], w["attn_sinks"]
    tables = _rope_tables()

    def gather(names, part=0, parts=1):
        return _Gather([whole[n] for n in names], [(i, spec[n], part, parts) for i, n in enumerate(names)])

    def pair(names):
        return _Pair([dw[n] for n in names], [spec[n] for n in names])

    def pair_sum(tag, names, got):
        return _pair_sum("pair_sum_" + tag, [spec[n] for n in names], [dw[n] for n in names], got, place)

    cast_rows = {"w_down": D_FF // N_CHIPS // 2}
    whole = {n: _to_bf16_in_whole("cast_" + n, w[n], spec[n], shard1, cast_rows.get(n, 256)) for n in MATRIX_NAMES}
    whole["w_in"], conv_w_whole = _comm_call(
        "gather_w_in", _Gather([whole["w_in"]], [(0, spec["w_in"], 0, 1)], conv_w=w["conv_w"]))

    mixers = ("w_conv_out", "w_attn_out", "w_o")
    h1 = _rms_norm("norm_mix", xs, w["g_mix"])
    proj, got = _mm_nn("mm_in", h1, whole["w_in"], 1024, 1664, F32, comm=gather(mixers))
    whole.update(zip(mixers, got))
    conv_y = _conv_fwd(proj, conv_w_whole)
    qkv, (whole["w_gate_up"],) = _rope_fwd(proj, tables, comm=gather(("w_gate_up",), 0, 2))
    attn, (whole["w_gate_up"],) = _attn_fwd(qkv, sinks, comm=gather(("w_gate_up",), 1, 2))
    conv_out, attn_out, merged = _branch_merge(conv_y, attn, whole["w_conv_out"], whole["w_attn_out"], proj)
    x2 = _mm_nn("mm_o", merged, whole["w_o"], 1024, 1024, F32, res=xs)
    h2 = _rms_norm("norm_ffn", x2, w["g_ffn"])
    gu, (whole["w_down"],) = _mm_nn("mm_gate_up", h2, whole["w_gate_up"], 1024, 1408, F32, comm=gather(("w_down",)))
    act = _swiglu_fwd(gu)
    x3 = _mm_nn("mm_down", act, whole["w_down"], 1024, 512, F32, res=x2)
    dx3, dx3b, dg_final, loss_row = _loss_head(x3, w["g_final"], target)

    dw = {}
    dact = _mm_nt("mm_dact", dx3b, whole["w_down"], 1024, 1408, D_MODEL, F32)
    dw["w_down"] = _mm_tn("mm_dw_down", act, dx3b, 1408, 1024, BF16)
    dgu, got = _swiglu_bwd(dact, gu, comm=pair(("w_down",)))
    sums_a, own_a = pair_sum("down", ("w_down",), got)
    dh2, slots_a = _mm_nt("mm_dh2", dgu, whole["w_gate_up"], 1024, 1024, 1408, F32, comm=_ChipExchange(sums_a, own_a))
    dw["w_gate_up"] = _mm_tn("mm_dw_gate_up", h2, dgu, 1024, 1408, BF16)
    (dx2, dx2b, dg_ffn), got = _rms_norm_bwd("norm_ffn_bwd", dh2, x2, w["g_ffn"], dx3, True, comm=pair(("w_gate_up",)))
    sums_b, own_b = pair_sum("gate_up", ("w_gate_up",), got)
    dmerged = _mm_nt("mm_dmerged", dx2b, whole["w_o"], 1024, 1024, D_MODEL, F32)
    dw["w_o"] = _mm_tn("mm_dw_o", merged, dx2b, 1024, 1024, BF16)
    dco, dao, dgc, dga = _merge_bwd(dmerged, conv_out, attn_out, proj)
    dconv_y = _mm_nt("mm_dconv_y", dco, whole["w_conv_out"], 1024, 1024, D_MODEL, F32)
    dw["w_conv_out"] = _mm_tn("mm_dw_conv_out", conv_y, dco, 1024, 1024, BF16)
    dattn = _mm_nt("mm_dattn", dao, whole["w_attn_out"], 1024, 1024, D_MODEL, BF16)
    dw["w_attn_out"] = _mm_tn("mm_dw_attn_out", attn, dao, 1024, 1024, BF16)
    (dcb, dcc, dcx, dconv_w), got = _conv_bwd(dconv_y, proj, conv_w_whole, comm=pair(mixers))
    sums_c, own_c = pair_sum("mixers", mixers, got)
    (dq, dk_prev, dk_cur, dv_prev, dv_cur, dsinks), slots_bc = _attn_bwd(
        qkv, dattn, sinks, tables, comm=_ChipExchange(sums_b + sums_c, own_b + own_c))
    dkv = _kv_grad_combine(dk_prev, dk_cur, dv_prev, dv_cur, tables)
    early = ("w_down", "w_gate_up") + mixers
    halves = _chip_sum("chip_sum_early", [spec[n] for n in early], slots_a + slots_bc, core1)
    dproj = jnp.concatenate([dcb, dcc, dcx, dq, dkv, dgc, dga], axis=1)
    dh1, reduced = _mm_nt("mm_dh1", dproj, whole["w_in"], 1024, 1024, 1664, F32,
                          comm=_HalfExchange(halves, [spec[n] for n in early]))
    g = dict(zip(early, reduced))
    dw["w_in"] = _mm_tn("mm_dw_in", h1, dproj, 1024, 1664, BF16)
    (grad_x, dg_mix), got = _rms_norm_bwd("norm_mix_bwd", dh1, xs, w["g_mix"], dx2, False, comm=pair(("w_in",)))
    sums_d, own_d = pair_sum("in", ("w_in",), got)
    small = _pack_small(dg_mix, dg_ffn, dg_final, dconv_w, dsinks, loss_row)

    updates, (slots_d, small_blocks) = _adamw(
        "adamw_early", [(w[n], g[n], m[n], v[n]) for n in early], 8,
        comm=_Both(_ChipExchange(sums_d, own_d), _SmallAllToAll(small)))
    delta, new_m, new_v = {}, {}, {}

    def keep(names, triples):
        for n, (d, nm, nv) in zip(names, triples):
            delta[n], new_m[n], new_v[n] = d, nm, nv

    keep(early, updates)
    half_in = _chip_sum("chip_sum_in", [spec["w_in"]], [slots_d], core1)
    (g["w_in"],) = _comm_call("half_exchange_in", _HalfExchange(half_in, [spec["w_in"]]))
    small_sum = _small_sum(small_blocks)
    g["g_mix"] = small_sum[0:1, :]
    g["g_ffn"] = small_sum[1:2, :]
    g["g_final"] = small_sum[2:3, :]
    g["conv_w"] = lax.dynamic_slice(small_sum, (3, shard * CONV_W_COLS), (3, CONV_W_COLS))
    g["attn_sinks"] = small_sum[6:7, :N_HEADS]
    loss = small_sum[7, 0]
    keep(("w_in",), _adamw("adamw_w_in", [(w["w_in"], g["w_in"], m["w_in"], v["w_in"])], 4))
    rest = ("g_mix", "g_ffn", "g_final", "conv_w", "attn_sinks")
    keep(rest, _adamw("adamw_small", [(w[n], g[n], m[n], v[n]) for n in rest], 1))

    def shaped(vals):
        return [vals[n].reshape((D_MODEL,)) if n == "g_final" else
                (vals[n][None] if n in MATRIX_NAMES or n == "conv_w" else vals[n]) for n in WEIGHT_ORDER]

    return (loss, grad_x[None], *shaped(g), *shaped(delta), *shaped(new_m), *shaped(new_v))
```

```python
import functools
import math

import jax
import jax.numpy as jnp
from jax import lax
from jax.experimental import pallas as pl
from jax.experimental.pallas import tpu as pltpu

F32 = jnp.float32
BF16 = jnp.bfloat16

D_MODEL = 1024
SEQ = 2048
HEAD_DIM = 64
N_HEADS = 16
N_KV_HEADS = 4
GROUP = N_HEADS // N_KV_HEADS
D_ATTN = N_HEADS * HEAD_DIM
D_KV = N_KV_HEADS * HEAD_DIM
BLOCK = 128
ROT_DIM = HEAD_DIM // 4
ROPE_THETA = 500000.0
ATTN_SCALE = 1.0 / math.sqrt(HEAD_DIM)
NEG_INF = -1e30
D_FF = 2816
EPS = 1e-5
N_IN = 3 * D_MODEL + D_ATTN + 2 * D_KV + 2 * D_MODEL
COL_Q = 3 * D_MODEL
COL_K = COL_Q + D_ATTN
COL_V = COL_K + D_KV
COL_GC = COL_V + D_KV
COL_GA = COL_GC + D_MODEL

ADAM_LR = 0.001
ADAM_B1 = 0.9
ADAM_B2 = 0.999
ADAM_EPS = 1e-08
ADAM_WD = 0.01
ADAM_STEP = 10

N_CHIPS = 4
N_DEV = 8

V7X_VMEM_BYTES = 64 * 1024 * 1024
VMEM_LIMIT = (V7X_VMEM_BYTES * 3) // 4
LANES = 128
MESH = pl.DeviceIdType.MESH


def _params(semantics=None):
    return pltpu.CompilerParams(dimension_semantics=semantics, vmem_limit_bytes=VMEM_LIMIT)


def _sds(shape, dtype):
    return jax.ShapeDtypeStruct(shape, dtype)


HBM_SPEC = pl.BlockSpec(memory_space=pl.ANY)


def _pcall(body, name, grid, in_specs, out_specs, out_shape, operands, scratch=(), semantics=None, comm=None,
           aliases=None):
    aliases = dict(aliases or {})
    if comm is None:
        return pl.pallas_call(
            body, name=name, grid=grid, in_specs=in_specs, out_specs=out_specs, out_shape=out_shape,
            scratch_shapes=list(scratch), input_output_aliases=aliases,
            compiler_params=_params(semantics))(*operands)
    multi = isinstance(out_shape, (list, tuple))
    o_specs = list(out_specs) if multi else [out_specs]
    o_shape = list(out_shape) if multi else [out_shape]
    n_in, n_out, n_scr = len(operands), len(o_shape), len(scratch)
    n_cin, n_cout = len(comm.operands), len(comm.out_shape)

    def hosted(*refs):
        ins, cins = refs[:n_in], refs[n_in:n_in + n_cin]
        o0 = n_in + n_cin
        outs, couts = refs[o0:o0 + n_out], refs[o0 + n_out:o0 + n_out + n_cout]
        s0 = o0 + n_out + n_cout
        scr, sems = refs[s0:s0 + n_scr], refs[s0 + n_scr:]
        first = last = None
        for axis, size in enumerate(grid):
            i = pl.program_id(axis)
            first = (i == 0) if first is None else first & (i == 0)
            last = (i == size - 1) if last is None else last & (i == size - 1)

        @pl.when(first)
        def _():
            comm.start(cins, couts, sems)

        body(*ins, *outs, *scr)

        @pl.when(last)
        def _():
            comm.finish(cins, couts, sems)

    res = pl.pallas_call(
        hosted, name=name, grid=grid,
        in_specs=list(in_specs) + [HBM_SPEC] * n_cin, out_specs=o_specs + [HBM_SPEC] * n_cout,
        out_shape=o_shape + list(comm.out_shape), scratch_shapes=list(scratch) + list(comm.sems),
        input_output_aliases={**aliases, **{n_in + a: n_out + b for a, b in comm.aliases.items()}},
        compiler_params=_params(("arbitrary",) * len(grid)))(*operands, *comm.operands)
    outs = list(res[:n_out])
    return (outs if multi else outs[0]), list(res[n_out:])


def _comm_call(name, comm):
    def body(*refs):
        n_cin, n_cout = len(comm.operands), len(comm.out_shape)
        cins, couts, sems = refs[:n_cin], refs[n_cin:n_cin + n_cout], refs[n_cin + n_cout:]
        comm.start(cins, couts, sems)
        comm.finish(cins, couts, sems)

    return list(pl.pallas_call(
        body, name=name, in_specs=[HBM_SPEC] * len(comm.operands), out_specs=[HBM_SPEC] * len(comm.out_shape),
        out_shape=list(comm.out_shape), scratch_shapes=list(comm.sems),
        input_output_aliases=dict(comm.aliases))(*comm.operands))


NN = ((1,), (0,))
NT = ((1,), (1,))
TN = ((0,), (0,))


def _matmul(name, a, b, dims, grid, a_spec, b_spec, o_spec, o_shape, o_dtype, res=None, res_spec=None, comm=None):
    nk = grid[2]

    def body(*refs):
        if res is None:
            a_ref, b_ref, o_ref = refs[:3]
            r_ref = None
            scratch = refs[3:]
        else:
            a_ref, b_ref, r_ref, o_ref = refs[:4]
            scratch = refs[4:]
        p = lax.dot_general(a_ref[...], b_ref[...], (dims, ((), ())), preferred_element_type=F32)

        def finish(acc):
            if r_ref is not None:
                acc = r_ref[...] + acc
            o_ref[...] = acc.astype(o_dtype)

        if nk == 1:
            finish(p)
        else:
            acc_ref = scratch[0]
            k = pl.program_id(2)

            @pl.when(k == 0)
            def _():
                acc_ref[...] = p

            @pl.when(k > 0)
            def _():
                acc_ref[...] += p

            @pl.when(k == nk - 1)
            def _():
                finish(acc_ref[...])

    operands = [a, b] if res is None else [a, b, res]
    in_specs = [a_spec, b_spec] if res is None else [a_spec, b_spec, res_spec]
    scratch = [pltpu.VMEM(o_spec.block_shape, F32)] if nk > 1 else []
    return _pcall(body, name, grid, in_specs, o_spec, _sds(o_shape, o_dtype), operands, scratch,
                  ("parallel", "parallel", "arbitrary"), comm)


def _mm_nn(name, a, b, bm, bn, o_dtype, res=None, comm=None):
    m, k = a.shape
    n = b.shape[1]
    return _matmul(
        name, a, b, NN, (m // bm, n // bn, 1),
        pl.BlockSpec((bm, k), lambda i, j, kk: (i, 0)),
        pl.BlockSpec((k, bn), lambda i, j, kk: (0, j)),
        pl.BlockSpec((bm, bn), lambda i, j, kk: (i, j)),
        (m, n), o_dtype, res,
        None if res is None else pl.BlockSpec((bm, bn), lambda i, j, kk: (i, j)), comm,
    )


def _mm_nt(name, a, b, bm, bn, bk, o_dtype, comm=None):
    m, k = a.shape
    n = b.shape[0]
    return _matmul(
        name, a, b, NT, (m // bm, n // bn, k // bk),
        pl.BlockSpec((bm, bk), lambda i, j, kk: (i, kk)),
        pl.BlockSpec((bn, bk), lambda i, j, kk: (j, kk)),
        pl.BlockSpec((bm, bn), lambda i, j, kk: (i, j)),
        (m, n), o_dtype, comm=comm,
    )


def _mm_tn(name, a, b, bm, bn, o_dtype, comm=None):
    k, m = a.shape
    n = b.shape[1]
    return _matmul(
        name, a, b, TN, (m // bm, n // bn, 1),
        pl.BlockSpec((k, bm), lambda i, j, kk: (0, i)),
        pl.BlockSpec((k, bn), lambda i, j, kk: (0, j)),
        pl.BlockSpec((bm, bn), lambda i, j, kk: (i, j)),
        (m, n), o_dtype, comm=comm,
    )


ROWS = 256


def _row_spec(width, col=0):
    return pl.BlockSpec((ROWS, width), lambda i: (i, col))


def _full_spec(shape):
    return pl.BlockSpec(shape, lambda *_: (0,) * len(shape))


def _rms_norm(name, x, g):
    def body(x_ref, g_ref, h_ref):
        xf = x_ref[...]
        r = lax.rsqrt(jnp.mean(xf * xf, axis=-1, keepdims=True) + EPS)
        h_ref[...] = ((xf * r) * g_ref[...]).astype(BF16)

    return pl.pallas_call(
        body, name=name, grid=(SEQ // ROWS,),
        in_specs=[_row_spec(D_MODEL), _full_spec((1, D_MODEL))],
        out_specs=_row_spec(D_MODEL),
        out_shape=_sds((SEQ, D_MODEL), BF16),
        compiler_params=_params(("parallel",)),
    )(x, g)


CONV_COLS = 256


def _shift_rows(u, k):
    rows = lax.broadcasted_iota(jnp.int32, u.shape, 0)
    return jnp.where(rows >= k, pltpu.roll(u, k, axis=0), 0.0)


def _conv_fwd(proj, conv_w):
    nblk = D_MODEL // CONV_COLS

    def body(cb_ref, cc_ref, cx_ref, w_ref, y_ref):
        u = cc_ref[...] * cx_ref[...]
        w = w_ref[...]
        cv = w[0:1, :] * _shift_rows(u, 2) + w[1:2, :] * _shift_rows(u, 1) + w[2:3, :] * u
        y_ref[...] = (cb_ref[...] * cv).astype(BF16)

    def col(part):
        return pl.BlockSpec((SEQ, CONV_COLS), lambda j: (0, part * nblk + j))

    return pl.pallas_call(
        body, name="conv_fwd", grid=(nblk,),
        in_specs=[col(0), col(1), col(2), pl.BlockSpec((3, CONV_COLS), lambda j: (0, j))],
        out_specs=pl.BlockSpec((SEQ, CONV_COLS), lambda j: (0, j)),
        out_shape=_sds((SEQ, D_MODEL), BF16),
        compiler_params=_params(("parallel",)),
    )(proj, proj, proj, conv_w)


ROPE_COLS = 256


def _rope_tables():
    inv_freq = ROPE_THETA ** (-jnp.arange(0, ROT_DIM, 2, dtype=F32) / ROT_DIM)
    ang = jnp.arange(SEQ, dtype=F32)[:, None] * inv_freq[None, :]
    cos, sin = jnp.cos(ang), jnp.sin(ang)
    half = ROT_DIM // 2
    ones = jnp.ones((SEQ, HEAD_DIM - ROT_DIM), F32)
    zeros = jnp.zeros((SEQ, HEAD_DIM - ROT_DIM), F32)
    zh = jnp.zeros((SEQ, half), F32)
    c = jnp.concatenate([cos, cos, ones], axis=1)
    s_up = jnp.concatenate([-sin, zh, zeros], axis=1)
    s_dn = jnp.concatenate([zh, sin, zeros], axis=1)
    reps = ROPE_COLS // HEAD_DIM
    return tuple(jnp.tile(t, (1, reps)) for t in (c, s_up, s_dn))


def _rotate(t, c, s_up, s_dn):
    width = t.shape[1]
    half = ROT_DIM // 2
    return t * c + pltpu.roll(t, width - half, axis=1) * s_up + pltpu.roll(t, half, axis=1) * s_dn


def _rope_fwd(proj, tables, comm=None):
    ntile = (D_ATTN + 2 * D_KV) // ROPE_COLS
    first = COL_Q // ROPE_COLS

    def body(t_ref, c_ref, su_ref, sd_ref, o_ref):
        j = pl.program_id(1)
        t = t_ref[...]
        rot = _rotate(t, c_ref[...], su_ref[...], sd_ref[...])
        o_ref[...] = jnp.where(j < ntile - 1, rot, t).astype(BF16)

    tab = pl.BlockSpec((ROWS, ROPE_COLS), lambda i, j: (i, 0))
    return _pcall(
        body, "rope_fwd", (SEQ // ROWS, ntile),
        [pl.BlockSpec((ROWS, ROPE_COLS), lambda i, j: (i, first + j)), tab, tab, tab],
        pl.BlockSpec((ROWS, ROPE_COLS), lambda i, j: (i, j)),
        _sds((SEQ, D_ATTN + 2 * D_KV), BF16), [proj, *tables], (), ("parallel", "parallel"), comm)


N_QBLK = SEQ // BLOCK
KV_TILE = D_ATTN // D_KV


def _attn_specs():
    q = pl.BlockSpec((BLOCK, D_ATTN), lambda n: (n, 0))
    k_prev = pl.BlockSpec((BLOCK, D_KV), lambda n: (jnp.maximum(n - 1, 0), KV_TILE))
    k_cur = pl.BlockSpec((BLOCK, D_KV), lambda n: (n, KV_TILE))
    v_prev = pl.BlockSpec((BLOCK, D_KV), lambda n: (jnp.maximum(n - 1, 0), KV_TILE + 1))
    v_cur = pl.BlockSpec((BLOCK, D_KV), lambda n: (n, KV_TILE + 1))
    return [q, k_prev, k_cur, v_prev, v_cur]


def _band_mask(n):
    qi = lax.broadcasted_iota(jnp.int32, (BLOCK, 2 * BLOCK), 0)
    kj = lax.broadcasted_iota(jnp.int32, (BLOCK, 2 * BLOCK), 1)
    rel = qi + BLOCK - kj
    return (rel >= 0) & (rel < BLOCK) & ((kj >= BLOCK) | (n > 0))


HEADS_PER_TILE = LANES // HEAD_DIM


def _lane_half(shape, par):
    lane = lax.broadcasted_iota(jnp.int32, shape, 1)
    return (lane < HEAD_DIM) if par == 0 else (lane >= HEAD_DIM)


def _head_tiles(kv, h):
    tile = kv[:, (h // HEADS_PER_TILE) * LANES:(h // HEADS_PER_TILE + 1) * LANES].astype(F32)
    own = jnp.where(_lane_half(tile.shape, h % HEADS_PER_TILE), tile, 0.0)
    other = pltpu.roll(own, HEAD_DIM, axis=1)
    lo, hi = (own, other) if h % HEADS_PER_TILE == 0 else (other, own)
    return lo.astype(BF16), hi.astype(BF16)


def _head_softmax(q_tile, k_half, sink, mask):
    s = lax.dot_general(q_tile, k_half, (NT, ((), ())), preferred_element_type=F32) * ATTN_SCALE
    s = jnp.where(mask, s, NEG_INF)
    m = jnp.maximum(jnp.max(s, axis=-1, keepdims=True), sink)
    e = jnp.exp(s - m)
    es = jnp.exp(sink - m)
    inv = 1.0 / (jnp.sum(e, axis=-1, keepdims=True) + es)
    return e * inv, es * inv


def _attn_fwd(qkv, sinks, comm=None):
    def body(sink_ref, q_ref, kp_ref, kc_ref, vp_ref, vc_ref, o_ref):
        n = pl.program_id(0)
        mask = _band_mask(n)
        k = jnp.concatenate([kp_ref[...], kc_ref[...]], axis=0)
        v = jnp.concatenate([vp_ref[...], vc_ref[...]], axis=0)
        for h in range(N_KV_HEADS):
            k_halves = _head_tiles(k, h)
            v_halves = _head_tiles(v, h)
            for t in range(GROUP // HEADS_PER_TILE):
                tile = h * (GROUP // HEADS_PER_TILE) + t
                q_tile = q_ref[:, tile * LANES:(tile + 1) * LANES]
                acc = None
                for par in range(HEADS_PER_TILE):
                    sink = sink_ref[0, tile * HEADS_PER_TILE + par]
                    p, _ = _head_softmax(q_tile, k_halves[par], sink, mask)
                    o = jnp.dot(p.astype(BF16), v_halves[par], preferred_element_type=F32)
                    acc = o if acc is None else acc + o
                o_ref[:, tile * LANES:(tile + 1) * LANES] = acc.astype(BF16)

    return _pcall(
        body, "attn_fwd", (N_QBLK,),
        [pl.BlockSpec(memory_space=pltpu.SMEM)] + _attn_specs(),
        pl.BlockSpec((BLOCK, D_ATTN), lambda n: (n, 0)),
        _sds((SEQ, D_ATTN), BF16), [sinks, qkv, qkv, qkv, qkv, qkv], (), ("parallel",), comm)


def _branch_merge(conv_y, attn, w_co, w_ao, proj):
    bm, bn = 1024, 512

    def body(cy_ref, at_ref, wc_ref, wa_ref, gc_ref, ga_ref, co_ref, ao_ref, mg_ref):
        co = jnp.dot(cy_ref[...], wc_ref[...], preferred_element_type=F32)
        ao = jnp.dot(at_ref[...], wa_ref[...], preferred_element_type=F32)
        co_ref[...] = co
        ao_ref[...] = ao
        mg_ref[...] = (jax.nn.sigmoid(gc_ref[...]) * co + jax.nn.sigmoid(ga_ref[...]) * ao).astype(BF16)

    act = pl.BlockSpec((bm, D_MODEL), lambda i, j: (i, 0))
    wgt = pl.BlockSpec((D_MODEL, bn), lambda i, j: (0, j))
    out = pl.BlockSpec((bm, bn), lambda i, j: (i, j))
    return pl.pallas_call(
        body, name="branch_merge", grid=(SEQ // bm, D_MODEL // bn),
        in_specs=[act, act, wgt, wgt,
                  pl.BlockSpec((bm, bn), lambda i, j: (i, COL_GC // bn + j)),
                  pl.BlockSpec((bm, bn), lambda i, j: (i, COL_GA // bn + j))],
        out_specs=[out, out, out],
        out_shape=[_sds((SEQ, D_MODEL), F32), _sds((SEQ, D_MODEL), F32), _sds((SEQ, D_MODEL), BF16)],
        compiler_params=_params(("parallel", "parallel")),
    )(conv_y, attn, w_co, w_ao, proj, proj)


FF_ROWS = 128


def _swiglu_fwd(gu):
    def body(gu_ref, act_ref):
        g = gu_ref[:, :D_FF]
        act_ref[...] = (jax.nn.silu(g) * gu_ref[:, D_FF:]).astype(BF16)

    return pl.pallas_call(
        body, name="swiglu_fwd", grid=(SEQ // FF_ROWS,),
        in_specs=[pl.BlockSpec((FF_ROWS, 2 * D_FF), lambda i: (i, 0))],
        out_specs=pl.BlockSpec((FF_ROWS, D_FF), lambda i: (i, 0)),
        out_shape=_sds((SEQ, D_FF), BF16),
        compiler_params=_params(("parallel",)),
    )(gu)


def _loss_head(x3, g, target):
    def body(x_ref, g_ref, t_ref, dx_ref, dxb_ref, dg_ref, loss_ref):
        i = pl.program_id(0)
        xf = x_ref[...]
        r = lax.rsqrt(jnp.mean(xf * xf, axis=-1, keepdims=True) + EPS)
        xn = xf * r
        gg = g_ref[...]
        err = xn * gg - t_ref[...]
        part = 0.5 * jnp.sum(jnp.mean(err * err, axis=-1, keepdims=True), axis=0, keepdims=True)
        dy = err * (1.0 / D_MODEL)
        dxn = dy * gg
        dx = r * (dxn - xn * jnp.mean(dxn * xn, axis=-1, keepdims=True))
        dx_ref[...] = dx
        dxb_ref[...] = dx.astype(BF16)
        dg = jnp.sum(dy * xn, axis=0, keepdims=True)
        lane0 = lax.broadcasted_iota(jnp.int32, (1, LANES), 1) == 0
        lpart = jnp.where(lane0, part, 0.0)

        @pl.when(i == 0)
        def _():
            dg_ref[...] = dg
            loss_ref[...] = lpart

        @pl.when(i > 0)
        def _():
            dg_ref[...] += dg
            loss_ref[...] += lpart

    return pl.pallas_call(
        body, name="loss_head", grid=(SEQ // ROWS,),
        in_specs=[_row_spec(D_MODEL), _full_spec((1, D_MODEL)), _row_spec(D_MODEL)],
        out_specs=[_row_spec(D_MODEL), _row_spec(D_MODEL), _full_spec((1, D_MODEL)), _full_spec((1, LANES))],
        out_shape=[_sds((SEQ, D_MODEL), F32), _sds((SEQ, D_MODEL), BF16),
                   _sds((1, D_MODEL), F32), _sds((1, LANES), F32)],
        compiler_params=_params(("arbitrary",)),
    )(x3, g, target)


def _swiglu_bwd(dact, gu, comm=None):
    def body(da_ref, gu_ref, o_ref):
        g = gu_ref[:, :D_FF]
        up = gu_ref[:, D_FF:]
        da = da_ref[...]
        sg = jax.nn.sigmoid(g)
        o_ref[:, :D_FF] = (da * up * (sg * (1.0 + g * (1.0 - sg)))).astype(BF16)
        o_ref[:, D_FF:] = (da * (g * sg)).astype(BF16)

    return _pcall(
        body, "swiglu_bwd", (SEQ // FF_ROWS,),
        [pl.BlockSpec((FF_ROWS, D_FF), lambda i: (i, 0)), pl.BlockSpec((FF_ROWS, 2 * D_FF), lambda i: (i, 0))],
        pl.BlockSpec((FF_ROWS, 2 * D_FF), lambda i: (i, 0)),
        _sds((SEQ, 2 * D_FF), BF16), [dact, gu], (), ("parallel",), comm)


def _rms_norm_bwd(name, dh, x, g, dres, with_bf16, comm=None):
    def body(dh_ref, x_ref, g_ref, dr_ref, *outs):
        i = pl.program_id(0)
        dx_ref = outs[0]
        dg_ref = outs[-1]
        xf = x_ref[...]
        r = lax.rsqrt(jnp.mean(xf * xf, axis=-1, keepdims=True) + EPS)
        xn = xf * r
        dh = dh_ref[...]
        dxn = dh * g_ref[...]
        dx = dr_ref[...] + r * (dxn - xn * jnp.mean(dxn * xn, axis=-1, keepdims=True))
        dx_ref[...] = dx
        if with_bf16:
            outs[1][...] = dx.astype(BF16)
        dg = jnp.sum(dh * xn, axis=0, keepdims=True)

        @pl.when(i == 0)
        def _():
            dg_ref[...] = dg

        @pl.when(i > 0)
        def _():
            dg_ref[...] += dg

    row = _row_spec(D_MODEL)
    out_specs = [row] + ([row] if with_bf16 else []) + [_full_spec((1, D_MODEL))]
    out_shape = ([_sds((SEQ, D_MODEL), F32)] + ([_sds((SEQ, D_MODEL), BF16)] if with_bf16 else [])
                 + [_sds((1, D_MODEL), F32)])
    return _pcall(body, name, (SEQ // ROWS,), [row, row, _full_spec((1, D_MODEL)), row], out_specs, out_shape,
                  [dh, x, g, dres], (), ("arbitrary",), comm)


def _merge_bwd(dmerged, conv_out, attn_out, proj):
    def body(dm_ref, co_ref, ao_ref, gc_ref, ga_ref, dco_ref, dao_ref, dgc_ref, dga_ref):
        dm = dm_ref[...]
        sc = jax.nn.sigmoid(gc_ref[...])
        sa = jax.nn.sigmoid(ga_ref[...])
        dco_ref[...] = (dm * sc).astype(BF16)
        dao_ref[...] = (dm * sa).astype(BF16)
        dgc_ref[...] = (dm * co_ref[...] * (sc * (1.0 - sc))).astype(BF16)
        dga_ref[...] = (dm * ao_ref[...] * (sa * (1.0 - sa))).astype(BF16)

    half = D_MODEL // 2
    own = pl.BlockSpec((ROWS, half), lambda i, j: (i, j))
    sd = _sds((SEQ, D_MODEL), BF16)
    return pl.pallas_call(
        body, name="merge_bwd", grid=(SEQ // ROWS, 2),
        in_specs=[own, own, own,
                  pl.BlockSpec((ROWS, half), lambda i, j: (i, COL_GC // half + j)),
                  pl.BlockSpec((ROWS, half), lambda i, j: (i, COL_GA // half + j))],
        out_specs=[own, own, own, own], out_shape=[sd, sd, sd, sd],
        compiler_params=_params(("parallel", "parallel")),
    )(dmerged, conv_out, attn_out, proj, proj)


def _conv_bwd(dconv_y, proj, conv_w, comm=None):
    nblk = D_MODEL // CONV_COLS

    def body(dy_ref, cb_ref, cc_ref, cx_ref, w_ref, dcb_ref, dcc_ref, dcx_ref, dw_ref):
        cc = cc_ref[...]
        cx = cx_ref[...]
        u = cc * cx
        w = w_ref[...]
        u1 = _shift_rows(u, 1)
        u2 = _shift_rows(u, 2)
        cv = w[0:1, :] * u2 + w[1:2, :] * u1 + w[2:3, :] * u
        dy = dy_ref[...]
        dcb_ref[...] = (dy * cv).astype(BF16)
        dcv = dy * cb_ref[...]
        rows = lax.broadcasted_iota(jnp.int32, dcv.shape, 0)
        up1 = jnp.where(rows < SEQ - 1, pltpu.roll(dcv, SEQ - 1, axis=0), 0.0)
        up2 = jnp.where(rows < SEQ - 2, pltpu.roll(dcv, SEQ - 2, axis=0), 0.0)
        du = w[2:3, :] * dcv + w[1:2, :] * up1 + w[0:1, :] * up2
        dcc_ref[...] = (du * cx).astype(BF16)
        dcx_ref[...] = (du * cc).astype(BF16)
        dw_ref[...] = jnp.concatenate(
            [jnp.sum(dcv * u2, axis=0, keepdims=True),
             jnp.sum(dcv * u1, axis=0, keepdims=True),
             jnp.sum(dcv * u, axis=0, keepdims=True)], axis=0)

    def col(part):
        return pl.BlockSpec((SEQ, CONV_COLS), lambda j: (0, part * nblk + j))

    own = pl.BlockSpec((SEQ, CONV_COLS), lambda j: (0, j))
    wsp = pl.BlockSpec((3, CONV_COLS), lambda j: (0, j))
    sd = _sds((SEQ, D_MODEL), BF16)
    return _pcall(
        body, "conv_bwd", (nblk,), [own, col(0), col(1), col(2), wsp], [own, own, own, wsp],
        [sd, sd, sd, _sds((3, D_MODEL), F32)], [dconv_y, proj, proj, proj, conv_w], (), ("parallel",), comm)


def _attn_bwd(qkv, dattn, sinks, tables, comm=None):
    c_t, su_t, sd_t = tables

    def body(sink_ref, q_ref, kp_ref, kc_ref, vp_ref, vc_ref, do_ref, c_ref, su_ref, sd_ref,
             dq_ref, dkp_ref, dkc_ref, dvp_ref, dvc_ref, ds_ref):
        n = pl.program_id(0)
        mask = _band_mask(n)
        k = jnp.concatenate([kp_ref[...], kc_ref[...]], axis=0)
        v = jnp.concatenate([vp_ref[...], vc_ref[...]], axis=0)
        lane = lax.broadcasted_iota(jnp.int32, (1, LANES), 1)
        dsink = jnp.zeros((1, LANES), F32)
        c, su, sd = c_ref[:, :LANES], su_ref[:, :LANES], sd_ref[:, :LANES]
        dk_tiles = [None] * (N_KV_HEADS // HEADS_PER_TILE)
        dv_tiles = [None] * (N_KV_HEADS // HEADS_PER_TILE)
        for h in range(N_KV_HEADS):
            k_halves = _head_tiles(k, h)
            v_halves = _head_tiles(v, h)
            dk_par = [None] * HEADS_PER_TILE
            dv_par = [None] * HEADS_PER_TILE
            for t in range(GROUP // HEADS_PER_TILE):
                tile = h * (GROUP // HEADS_PER_TILE) + t
                q_tile = q_ref[:, tile * LANES:(tile + 1) * LANES]
                do_tile = do_ref[:, tile * LANES:(tile + 1) * LANES]
                dq_tile = None
                for par in range(HEADS_PER_TILE):
                    head = tile * HEADS_PER_TILE + par
                    p, p_sink = _head_softmax(q_tile, k_halves[par], sink_ref[0, head], mask)
                    dp = lax.dot_general(do_tile, v_halves[par], (NT, ((), ())), preferred_element_type=F32)
                    delta = jnp.sum(p * dp, axis=-1, keepdims=True)
                    ds = (p * (dp - delta) * ATTN_SCALE).astype(BF16)
                    dq = jnp.dot(ds, k_halves[par], preferred_element_type=F32)
                    dq_tile = dq if dq_tile is None else dq_tile + dq
                    dk = lax.dot_general(ds, q_tile, (TN, ((), ())), preferred_element_type=F32)
                    dv = lax.dot_general(p.astype(BF16), do_tile, (TN, ((), ())), preferred_element_type=F32)
                    dk_par[par] = dk if dk_par[par] is None else dk_par[par] + dk
                    dv_par[par] = dv if dv_par[par] is None else dv_par[par] + dv
                    val = -jnp.sum(p_sink * delta, axis=0, keepdims=True)
                    dsink = dsink + jnp.where(lane == head, val, 0.0)
                dq_ref[:, tile * LANES:(tile + 1) * LANES] = _rotate(dq_tile, c, -su, -sd).astype(BF16)
            own = h % HEADS_PER_TILE
            for par_grads, tiles in ((dk_par, dk_tiles), (dv_par, dv_tiles)):
                shifted = pltpu.roll(par_grads[1 - own], HEAD_DIM, axis=1)
                total = jnp.where(_lane_half(shifted.shape, own), par_grads[own] + shifted, 0.0)
                i = h // HEADS_PER_TILE
                tiles[i] = total if tiles[i] is None else tiles[i] + total
        for i in range(N_KV_HEADS // HEADS_PER_TILE):
            cols = slice(i * LANES, (i + 1) * LANES)
            dkp_ref[:, cols] = dk_tiles[i][:BLOCK, :]
            dkc_ref[:, cols] = dk_tiles[i][BLOCK:, :]
            dvp_ref[:, cols] = dv_tiles[i][:BLOCK, :]
            dvc_ref[:, cols] = dv_tiles[i][BLOCK:, :]

        @pl.when(n == 0)
        def _():
            ds_ref[...] = dsink

        @pl.when(n > 0)
        def _():
            ds_ref[...] += dsink

    blk = pl.BlockSpec((BLOCK, D_KV), lambda n: (n, 0))
    tab = pl.BlockSpec((BLOCK, ROPE_COLS), lambda n: (n, 0))
    kv = _sds((SEQ, D_KV), F32)
    return _pcall(
        body, "attn_bwd", (N_QBLK,),
        [pl.BlockSpec(memory_space=pltpu.SMEM)] + _attn_specs()
        + [pl.BlockSpec((BLOCK, D_ATTN), lambda n: (n, 0)), tab, tab, tab],
        [pl.BlockSpec((BLOCK, D_ATTN), lambda n: (n, 0)), blk, blk, blk, blk, _full_spec((1, LANES))],
        [_sds((SEQ, D_ATTN), BF16), kv, kv, kv, kv, _sds((1, LANES), F32)],
        [sinks, qkv, qkv, qkv, qkv, qkv, dattn, c_t, su_t, sd_t], (), ("arbitrary",), comm)


def _kv_grad_combine(dk_prev, dk_cur, dv_prev, dv_cur, tables):
    def body(kp_ref, kc_ref, vp_ref, vc_ref, c_ref, su_ref, sd_ref, o_ref):
        m = pl.program_id(0)
        has_next = m < N_QBLK - 1
        dk = kc_ref[...] + jnp.where(has_next, kp_ref[...], 0.0)
        dv = vc_ref[...] + jnp.where(has_next, vp_ref[...], 0.0)
        o_ref[:, :D_KV] = _rotate(dk, c_ref[...], -su_ref[...], -sd_ref[...]).astype(BF16)
        o_ref[:, D_KV:] = dv.astype(BF16)

    cur = pl.BlockSpec((BLOCK, D_KV), lambda m: (m, 0))
    nxt = pl.BlockSpec((BLOCK, D_KV), lambda m: (jnp.minimum(m + 1, N_QBLK - 1), 0))
    return pl.pallas_call(
        body, name="kv_grad_combine", grid=(N_QBLK,),
        in_specs=[nxt, cur, nxt, cur, cur, cur, cur],
        out_specs=pl.BlockSpec((BLOCK, 2 * D_KV), lambda m: (m, 0)),
        out_shape=_sds((SEQ, 2 * D_KV), BF16),
        compiler_params=_params(("parallel",)),
    )(dk_prev, dk_cur, dv_prev, dv_cur, *tables)


MATRICES = {
    "w_in": (D_MODEL, N_IN // N_CHIPS, "col"),
    "w_conv_out": (D_MODEL // N_CHIPS, D_MODEL, "row"),
    "w_attn_out": (D_MODEL // N_CHIPS, D_MODEL, "row"),
    "w_o": (D_MODEL // N_CHIPS, D_MODEL, "row"),
    "w_gate_up": (D_MODEL, 2 * D_FF // N_CHIPS, "col"),
    "w_down": (D_FF // N_CHIPS, D_MODEL, "row"),
}
BF16_ROW_TILE = 16
CONV_W_COLS = D_MODEL // N_CHIPS
SMALL_ROWS = 8


def _whole_shape(spec):
    rows, cols, kind = spec
    return (rows, cols * N_CHIPS) if kind == "col" else (rows * N_CHIPS, cols)


def _half_shape(spec):
    return (spec[0] // 2, spec[1])


def _aligned(start, multiple):
    return start if isinstance(start, int) else pl.multiple_of(start, multiple)


def _region(ref, spec, shard, half, part=0, parts=1):
    rows, cols, kind = spec
    hr = rows // 2
    n = hr // parts
    if kind == "col":
        return ref.at[pl.ds(_aligned(half * hr + part * n, BF16_ROW_TILE), n),
                      pl.ds(_aligned(shard * cols, LANES), cols)]
    return ref.at[pl.ds(_aligned(shard * rows + half * hr + part * n, BF16_ROW_TILE), n), :]


def _position():
    x, y, c = lax.axis_index("x"), lax.axis_index("y"), lax.axis_index("c")
    chips = [(1 - x, y), (x, 1 - y), (1 - x, 1 - y)]
    return x, y, c, chips


def _shard_of(chip):
    return 2 * chip[0] + chip[1]


def _remote(src, dst, send_sem, recv_sem, to):
    return pltpu.make_async_remote_copy(src_ref=src, dst_ref=dst, send_sem=send_sem, recv_sem=recv_sem,
                                        device_id=to, device_id_type=MESH)


def _to_bf16_in_whole(name, w, spec, shard, rows):
    steps = spec[0] // rows

    def body(s_ref, w_ref, o_ref):
        del s_ref
        o_ref[...] = w_ref[...].astype(BF16)

    if spec[2] == "col":
        out_spec = pl.BlockSpec((rows, spec[1]), lambda i, s_ref: (i, s_ref[0]))
    else:
        out_spec = pl.BlockSpec((rows, spec[1]), lambda i, s_ref: (s_ref[0] * steps + i, 0))
    grid_spec = pltpu.PrefetchScalarGridSpec(
        num_scalar_prefetch=1, grid=(steps,),
        in_specs=[pl.BlockSpec((rows, spec[1]), lambda i, s_ref: (i, 0))], out_specs=out_spec)
    return pl.pallas_call(
        body, name=name, grid_spec=grid_spec, out_shape=_sds(_whole_shape(spec), BF16),
        compiler_params=_params(("parallel",)),
    )(shard, w)


class _Gather:
    def __init__(self, wholes, pieces, conv_w=None):
        self.pieces = pieces
        self.n = len(wholes)
        self.with_conv_w = conv_w is not None
        self.operands = list(wholes) + ([conv_w] if self.with_conv_w else [])
        self.out_shape = [_sds(w.shape, w.dtype) for w in wholes]
        if self.with_conv_w:
            self.out_shape.append(_sds((3, D_MODEL), F32))
        self.aliases = {i: i for i in range(self.n)}
        n_ici = 3 * len(pieces)
        self.sems = [pltpu.SemaphoreType.DMA((n_ici,))] * 4
        if self.with_conv_w:
            self.sems += [pltpu.SemaphoreType.DMA((1,)), pltpu.SemaphoreType.DMA((3,)), pltpu.SemaphoreType.DMA((3,))]

    def _conv_w(self, cins, couts, sems, with_recvs):
        cw_in, cw_out = cins[self.n], couts[self.n]
        x, y, c, chips = _position()

        def cols(shard):
            return cw_out.at[:, pl.ds(_aligned(shard * CONV_W_COLS, LANES), CONV_W_COLS)]

        me = _shard_of((x, y))
        local = pltpu.make_async_copy(cw_in, cols(me), sems[4].at[0])
        sends = [_remote(cw_in, cols(me), sems[5].at[j], sems[6].at[j], (*chip, c)) for j, chip in enumerate(chips)]
        if not with_recvs:
            return local, sends, []
        recvs = [_remote(cols(_shard_of(chip)), cols(_shard_of(chip)), sems[5].at[j], sems[6].at[j], (*chip, c))
                 for j, chip in enumerate(chips)]
        return local, sends, recvs

    def start(self, cins, couts, sems):
        x, y, c, chips = _position()
        me = _shard_of((x, y))
        if self.with_conv_w:
            local, sends, _ = self._conv_w(cins, couts, sems, False)
            local.start()
            for cp in sends:
                cp.start()
        for p, (i, spec, part, parts) in enumerate(self.pieces):
            mine = _region(couts[i], spec, me, c, part, parts)
            for j, chip in enumerate(chips):
                _remote(mine, mine, sems[0].at[3 * p + j], sems[1].at[3 * p + j], (*chip, c)).start()

    def finish(self, cins, couts, sems):
        x, y, c, chips = _position()
        me = _shard_of((x, y))
        sibling = (x, y, 1 - c)
        send_a, recv_a, send_b, recv_b = sems[:4]
        passed = []
        for p, (i, spec, part, parts) in enumerate(self.pieces):
            for j, chip in enumerate(chips):
                k = 3 * p + j
                landed = _region(couts[i], spec, _shard_of(chip), c, part, parts)
                _remote(landed, landed, send_a.at[k], recv_a.at[k], (*chip, c)).wait_recv()
                cp = _remote(landed, landed, send_b.at[k], recv_b.at[k], sibling)
                cp.start()
                passed.append(cp)
        for p, (i, spec, part, parts) in enumerate(self.pieces):
            mine = _region(couts[i], spec, me, c, part, parts)
            for j, chip in enumerate(chips):
                k = 3 * p + j
                other = _region(couts[i], spec, _shard_of(chip), 1 - c, part, parts)
                _remote(other, other, send_b.at[k], recv_b.at[k], sibling).wait_recv()
                _remote(mine, mine, send_a.at[k], recv_a.at[k], (*chip, c)).wait_send()
        for cp in passed:
            cp.wait_send()
        if self.with_conv_w:
            local, sends, recvs = self._conv_w(cins, couts, sems, True)
            for cp in recvs:
                cp.wait_recv()
            for cp in sends:
                cp.wait_send()
            local.wait()


def _mm_in_gather(h1, w_whole, comm):
    spec = MATRICES["w_in"]
    cols = spec[1]
    bm = SEQ // 2

    def body(h_ref, w_in_ref, proj_ref, w_ref, wbuf, obuf, send_a, recv_a, send_b, recv_b, load_sem, store_sems):
        del w_in_ref
        s, mi = pl.program_id(0), pl.program_id(1)
        x, y, c, chips = _position()
        me = _shard_of((x, y))
        sibling = (x, y, 1 - c)
        mine = _region(w_ref, spec, me, c)

        @pl.when((s == 0) & (mi == 0))
        def _():
            for j, chip in enumerate(chips):
                _remote(mine, mine, send_a.at[j], recv_a.at[j], (*chip, c)).start()

        shard = me
        for j, chip in enumerate(chips):
            shard = jnp.where(s == j + 1, _shard_of(chip), shard)

            @pl.when((s == j + 1) & (mi == 0))
            def _():
                landed = _region(w_ref, spec, _shard_of(chip), c)
                _remote(landed, landed, send_a.at[j], recv_a.at[j], (*chip, c)).wait_recv()
                _remote(landed, landed, send_b.at[j], recv_b.at[j], sibling).start()
                other = _region(w_ref, spec, _shard_of(chip), 1 - c)
                _remote(other, other, send_b.at[j], recv_b.at[j], sibling).wait_recv()

        col0 = pl.multiple_of(shard * cols, LANES)

        @pl.when(mi == 0)
        def _():
            load = pltpu.make_async_copy(w_ref.at[:, pl.ds(col0, cols)], wbuf, load_sem.at[0])
            load.start()
            load.wait()

        def store():
            rows = pl.ds(pl.multiple_of(mi * bm, bm), bm)
            return pltpu.make_async_copy(obuf.at[mi], proj_ref.at[rows, pl.ds(col0, cols)], store_sems.at[mi])

        @pl.when(s > 0)
        def _():
            store().wait()

        obuf[mi] = jnp.dot(h_ref[...], wbuf[...], preferred_element_type=F32)
        store().start()

        @pl.when(s == N_CHIPS - 1)
        def _():
            store().wait()

        @pl.when((s == N_CHIPS - 1) & (mi == 1))
        def _():
            for j, chip in enumerate(chips):
                landed = _region(w_ref, spec, _shard_of(chip), c)
                _remote(mine, mine, send_a.at[j], recv_a.at[j], (*chip, c)).wait_send()
                _remote(landed, landed, send_b.at[j], recv_b.at[j], sibling).wait_send()

    sem3 = pltpu.SemaphoreType.DMA((3,))
    (proj, whole), extra = _pcall(
        body, "mm_in", (N_CHIPS, SEQ // bm),
        [pl.BlockSpec((bm, D_MODEL), lambda s, m: (m, 0)), HBM_SPEC], [HBM_SPEC, HBM_SPEC],
        [_sds((SEQ, N_IN), F32), _sds(w_whole.shape, w_whole.dtype)], [h1, w_whole],
        [pltpu.VMEM((D_MODEL, cols), BF16), pltpu.VMEM((SEQ // bm, bm, cols), F32), sem3, sem3, sem3, sem3,
         pltpu.SemaphoreType.DMA((1,)), pltpu.SemaphoreType.DMA((SEQ // bm,))],
        None, comm, aliases={1: 1})
    return proj, whole, extra


def _pack_small(dg_mix, dg_ffn, dg_final, dconv_w, dsinks, loss_row):
    def body(a_ref, b_ref, c_ref, w_ref, s_ref, l_ref, o_ref):
        pad = jnp.zeros((1, D_MODEL - LANES), F32)
        o_ref[0:1, :] = a_ref[...]
        o_ref[1:2, :] = b_ref[...]
        o_ref[2:3, :] = c_ref[...]
        o_ref[3:6, :] = w_ref[...]
        o_ref[6:7, :] = jnp.concatenate([s_ref[...], pad], axis=1)
        o_ref[7:8, :] = jnp.concatenate([l_ref[...], pad], axis=1)

    return pl.pallas_call(
        body, name="pack_small", out_shape=_sds((SMALL_ROWS, D_MODEL), F32),
        compiler_params=_params(),
    )(dg_mix, dg_ffn, dg_final, dconv_w, dsinks, loss_row)


class _Pair:
    def __init__(self, dws, specs):
        self.specs = specs
        self.operands = list(dws)
        self.out_shape = [_sds((N_CHIPS, *_half_shape(s)), BF16) for s in specs]
        self.aliases = {}
        n = N_CHIPS * len(specs)
        self.sems = [pltpu.SemaphoreType.DMA((n,)), pltpu.SemaphoreType.DMA((n,))]

    def _copies(self, cins, couts, sems):
        x, y, c, _ = _position()
        sibling = (x, y, 1 - c)
        for i, spec in enumerate(self.specs):
            for t in range(N_CHIPS):
                k = N_CHIPS * i + t
                yield _remote(_region(cins[i], spec, t, 1 - c), couts[i].at[t], sems[0].at[k], sems[1].at[k], sibling)

    def start(self, cins, couts, sems):
        for cp in self._copies(cins, couts, sems):
            cp.start()

    def finish(self, cins, couts, sems):
        for cp in self._copies(cins, couts, sems):
            cp.wait()


class _SmallAllToAll:
    def __init__(self, small):
        self.operands = [small]
        self.out_shape = [_sds((N_DEV, SMALL_ROWS, D_MODEL), F32)]
        self.aliases = {}
        self.sems = [pltpu.SemaphoreType.DMA((N_DEV - 1,)), pltpu.SemaphoreType.DMA((N_DEV - 1,)),
                     pltpu.SemaphoreType.DMA((1,))]

    def _copies(self, cins, couts, sems):
        x, y, c, _ = _position()
        me = 4 * x + 2 * y + c
        out = []
        for r in range(1, N_DEV):
            flip = ((r >> 2) & 1, (r >> 1) & 1, r & 1)
            peer = tuple(1 - p if f else p for p, f in zip((x, y, c), flip))
            theirs = couts[0].at[4 * peer[0] + 2 * peer[1] + peer[2]]
            out.append((_remote(cins[0], couts[0].at[me], sems[0].at[r - 1], sems[1].at[r - 1], peer),
                        _remote(theirs, theirs, sems[0].at[r - 1], sems[1].at[r - 1], peer)))
        return pltpu.make_async_copy(cins[0], couts[0].at[me], sems[2].at[0]), out

    def start(self, cins, couts, sems):
        own, copies = self._copies(cins, couts, sems)
        own.start()
        for send, _ in copies:
            send.start()

    def finish(self, cins, couts, sems):
        own, copies = self._copies(cins, couts, sems)
        for send, recv in copies:
            recv.wait_recv()
            send.wait_send()
        own.wait()


class _Both:
    def __init__(self, a, b):
        self.a, self.b = a, b
        self.operands = list(a.operands) + list(b.operands)
        self.out_shape = list(a.out_shape) + list(b.out_shape)
        self.aliases = dict(a.aliases)
        self.aliases.update({len(a.operands) + k: len(a.out_shape) + v for k, v in b.aliases.items()})
        self.sems = list(a.sems) + list(b.sems)

    def _split(self, cins, couts, sems):
        na, ma, sa = len(self.a.operands), len(self.a.out_shape), len(self.a.sems)
        return (cins[:na], couts[:ma], sems[:sa]), (cins[na:], couts[ma:], sems[sa:])

    def start(self, cins, couts, sems):
        for plan, args in zip((self.a, self.b), self._split(cins, couts, sems)):
            plan.start(*args)

    def finish(self, cins, couts, sems):
        for plan, args in zip((self.a, self.b), self._split(cins, couts, sems)):
            plan.finish(*args)


def _pair_sum(name, specs, dws, got, place):
    n_mat = len(specs)

    def body(p_ref, *refs):
        t = pl.program_id(0)
        mine, theirs = refs[:n_mat], refs[n_mat:2 * n_mat]
        outs, owns = refs[2 * n_mat:3 * n_mat], refs[3 * n_mat:]
        for a, b, o, own in zip(mine, theirs, outs, owns):
            s = (a[...].astype(F32) + b[...].astype(F32)).astype(BF16)
            o[...] = s

            @pl.when(t == p_ref[1])
            def _():
                own[...] = s

    def mine_spec(spec):
        hr, cols = _half_shape(spec)
        if spec[2] == "col":
            return pl.BlockSpec((hr, cols), lambda t, p_ref: (p_ref[0], t))
        return pl.BlockSpec((hr, cols), lambda t, p_ref: (2 * t + p_ref[0], 0))

    def slot_spec(spec):
        return pl.BlockSpec((None, *_half_shape(spec)), lambda t, p_ref: (t, 0, 0))

    def own_spec(spec):
        return pl.BlockSpec((None, *_half_shape(spec)), lambda t, p_ref: (p_ref[1], 0, 0))

    slots = [_sds((N_CHIPS, *_half_shape(s)), BF16) for s in specs]
    grid_spec = pltpu.PrefetchScalarGridSpec(
        num_scalar_prefetch=1, grid=(N_CHIPS,),
        in_specs=[mine_spec(s) for s in specs] + [slot_spec(s) for s in specs],
        out_specs=[slot_spec(s) for s in specs] + [own_spec(s) for s in specs])
    res = pl.pallas_call(
        body, name=name, grid_spec=grid_spec, out_shape=slots + slots,
        compiler_params=_params(("arbitrary",)),
    )(place, *dws, *got)
    return list(res[:n_mat]), list(res[n_mat:])


class _ChipExchange:
    def __init__(self, sums, slots, part=0, parts=1):
        self.n = len(sums)
        self.part, self.parts = part, parts
        self.operands = list(sums) + list(slots)
        self.out_shape = [_sds(s.shape, s.dtype) for s in slots]
        self.aliases = {self.n + i: i for i in range(self.n)}
        self.sems = [pltpu.SemaphoreType.DMA((3 * self.n,)), pltpu.SemaphoreType.DMA((3 * self.n,))]

    def _rows(self, ref, slot):
        n = ref.shape[1] // self.parts
        return ref.at[slot, pl.ds(self.part * n, n), :]

    def _copies(self, cins, couts, sems):
        x, y, c, chips = _position()
        me = _shard_of((x, y))
        for i in range(self.n):
            for j, chip in enumerate(chips):
                k = 3 * i + j
                theirs = self._rows(couts[i], _shard_of(chip))
                yield (_remote(self._rows(cins[i], _shard_of(chip)), self._rows(couts[i], me),
                               sems[0].at[k], sems[1].at[k], (*chip, c)),
                       _remote(theirs, theirs, sems[0].at[k], sems[1].at[k], (*chip, c)))

    def start(self, cins, couts, sems):
        for send, _ in self._copies(cins, couts, sems):
            send.start()

    def finish(self, cins, couts, sems):
        for send, recv in self._copies(cins, couts, sems):
            recv.wait_recv()
            send.wait_send()


def _chip_sum(name, specs, slots, core):
    steps = 2
    n_mat = len(specs)

    def body(c_ref, *refs):
        del c_ref
        ins, outs = refs[:n_mat], refs[n_mat:]
        for a, o in zip(ins, outs):
            acc = a[0].astype(F32)
            for t in range(1, N_CHIPS):
                acc = acc + a[t].astype(F32)
            o[...] = acc

    def in_spec(spec):
        hr, cols = _half_shape(spec)
        return pl.BlockSpec((N_CHIPS, hr // steps, cols), lambda i, c_ref: (0, i, 0))

    def out_spec(spec):
        hr, cols = _half_shape(spec)
        return pl.BlockSpec((hr // steps, cols), lambda i, c_ref: (c_ref[0] * steps + i, 0))

    grid_spec = pltpu.PrefetchScalarGridSpec(
        num_scalar_prefetch=1, grid=(steps,),
        in_specs=[in_spec(s) for s in specs], out_specs=[out_spec(s) for s in specs])
    return list(pl.pallas_call(
        body, name=name, grid_spec=grid_spec,
        out_shape=[_sds((s[0], s[1]), F32) for s in specs],
        compiler_params=_params(("parallel",)),
    )(core, *slots))


class _HalfExchange:
    def __init__(self, grads, specs):
        self.specs = specs
        self.operands = list(grads)
        self.out_shape = [_sds(g.shape, g.dtype) for g in grads]
        self.aliases = {i: i for i in range(len(grads))}
        self.sems = [pltpu.SemaphoreType.DMA((len(grads),)), pltpu.SemaphoreType.DMA((len(grads),))]

    def _copies(self, couts, sems):
        x, y, c, _ = _position()
        sibling = (x, y, 1 - c)
        for i, spec in enumerate(self.specs):
            hr = spec[0] // 2
            mine = couts[i].at[pl.ds(_aligned(c * hr, 8), hr), :]
            theirs = couts[i].at[pl.ds(_aligned((1 - c) * hr, 8), hr), :]
            yield (_remote(mine, mine, sems[0].at[i], sems[1].at[i], sibling),
                   _remote(theirs, theirs, sems[0].at[i], sems[1].at[i], sibling))

    def start(self, cins, couts, sems):
        for send, _ in self._copies(couts, sems):
            send.start()

    def finish(self, cins, couts, sems):
        for send, recv in self._copies(couts, sems):
            recv.wait_recv()
            send.wait_send()


def _small_sum(blocks):
    def body(b_ref, o_ref):
        acc = b_ref[0]
        for d in range(1, N_DEV):
            acc = acc + b_ref[d]
        o_ref[...] = acc

    return pl.pallas_call(
        body, name="small_sum", out_shape=_sds((SMALL_ROWS, D_MODEL), F32), compiler_params=_params(),
    )(blocks)


def _adamw(name, params, steps, comm=None):
    n = len(params)

    def body(*refs):
        for p in range(n):
            w_ref, g_ref, m_ref, v_ref = refs[4 * p:4 * p + 4]
            d_ref, nm_ref, nv_ref = refs[4 * n + 3 * p:4 * n + 3 * p + 3]
            g = g_ref[...]
            m = ADAM_B1 * m_ref[...] + (1.0 - ADAM_B1) * g
            v = ADAM_B2 * v_ref[...] + (1.0 - ADAM_B2) * jnp.square(g)
            m_hat = m / (1.0 - ADAM_B1 ** ADAM_STEP)
            v_hat = v / (1.0 - ADAM_B2 ** ADAM_STEP)
            d_ref[...] = -ADAM_LR * (m_hat / (jnp.sqrt(v_hat) + ADAM_EPS) + ADAM_WD * w_ref[...])
            nm_ref[...] = m
            nv_ref[...] = v

    in_specs, out_specs, out_shape, operands = [], [], [], []
    for w, g, m, v in params:
        spec = pl.BlockSpec((w.shape[0] // steps, w.shape[1]), lambda i: (i, 0))
        in_specs += [spec] * 4
        out_specs += [spec] * 3
        out_shape += [_sds(w.shape, F32)] * 3
        operands += [w, g, m, v]
    res = _pcall(body, name, (steps,), in_specs, out_specs, out_shape, operands, (), ("parallel",), comm)
    outs, extra = res if comm is not None else (res, None)
    triples = [tuple(outs[3 * p:3 * p + 3]) for p in range(n)]
    return triples if comm is None else (triples, extra)


MATRIX_NAMES = tuple(MATRICES)
WEIGHT_ORDER = ("g_mix", "w_in", "conv_w", "attn_sinks", "w_conv_out", "w_attn_out", "w_o", "g_ffn",
                "w_gate_up", "w_down", "g_final")


def kernel(x, g_mix, w_in, conv_w, attn_sinks, w_conv_out, w_attn_out, w_o, g_ffn, w_gate_up, w_down, g_final, loss_target, m_g_mix, m_w_in, m_conv_w, m_attn_sinks, m_w_conv_out, m_w_attn_out, m_w_o, m_g_ffn, m_w_gate_up, m_w_down, m_g_final, v_g_mix, v_w_in, v_conv_w, v_attn_sinks, v_w_conv_out, v_w_attn_out, v_w_o, v_g_ffn, v_w_gate_up, v_w_down, v_g_final):
    w = dict(g_mix=g_mix, w_in=w_in[0], conv_w=conv_w[0], attn_sinks=attn_sinks, w_conv_out=w_conv_out[0],
             w_attn_out=w_attn_out[0], w_o=w_o[0], g_ffn=g_ffn, w_gate_up=w_gate_up[0], w_down=w_down[0],
             g_final=g_final[None, :])
    m = dict(g_mix=m_g_mix, w_in=m_w_in[0], conv_w=m_conv_w[0], attn_sinks=m_attn_sinks,
             w_conv_out=m_w_conv_out[0], w_attn_out=m_w_attn_out[0], w_o=m_w_o[0], g_ffn=m_g_ffn,
             w_gate_up=m_w_gate_up[0], w_down=m_w_down[0], g_final=m_g_final[None, :])
    v = dict(g_mix=v_g_mix, w_in=v_w_in[0], conv_w=v_conv_w[0], attn_sinks=v_attn_sinks,
             w_conv_out=v_w_conv_out[0], w_attn_out=v_w_attn_out[0], w_o=v_w_o[0], g_ffn=v_g_ffn,
             w_gate_up=v_w_gate_up[0], w_down=v_w_down[0], g_final=v_g_final[None, :])
    shard = (2 * lax.axis_index("x") + lax.axis_index("y")).astype(jnp.int32)
    core = lax.axis_index("c").astype(jnp.int32)
    shard1, core1, place = shard.reshape((1,)), core.reshape((1,)), jnp.stack([core, shard])
    spec = MATRICES
    xs, target, sinks = x[0], loss_target[0], w["attn_sinks"]
    tables = _rope_tables()

    def gather(names, part=0, parts=1):
        return _Gather([whole[n] for n in names], [(i, spec[n], part, parts) for i, n in enumerate(names)])

    def pair(names):
        return _Pair([dw[n] for n in names], [spec[n] for n in names])

    def pair_sum(tag, names, got):
        return _pair_sum("pair_sum_" + tag, [spec[n] for n in names], [dw[n] for n in names], got, place)

    cast_rows = {"w_down": D_FF // N_CHIPS // 2}
    whole = {n: _to_bf16_in_whole("cast_" + n, w[n], spec[n], shard1, cast_rows.get(n, 256)) for n in MATRIX_NAMES}

    mixers = ("w_conv_out", "w_attn_out", "w_o")
    h1 = _rms_norm("norm_mix", xs, w["g_mix"])
    proj, whole["w_in"], (*got, conv_w_whole) = _mm_in_gather(
        h1, whole["w_in"], _Gather([whole[n] for n in mixers], [(i, spec[n], 0, 1) for i, n in enumerate(mixers)],
                                   conv_w=w["conv_w"]))
    whole.update(zip(mixers, got))
    conv_y = _conv_fwd(proj, conv_w_whole)
    qkv, (whole["w_gate_up"],) = _rope_fwd(proj, tables, comm=gather(("w_gate_up",), 0, 2))
    attn, (whole["w_gate_up"],) = _attn_fwd(qkv, sinks, comm=gather(("w_gate_up",), 1, 2))
    conv_out, attn_out, merged = _branch_merge(conv_y, attn, whole["w_conv_out"], whole["w_attn_out"], proj)
    x2 = _mm_nn("mm_o", merged, whole["w_o"], 1024, 1024, F32, res=xs)
    h2 = _rms_norm("norm_ffn", x2, w["g_ffn"])
    gu, (whole["w_down"],) = _mm_nn("mm_gate_up", h2, whole["w_gate_up"], 1024, 1408, F32, comm=gather(("w_down",)))
    act = _swiglu_fwd(gu)
    x3 = _mm_nn("mm_down", act, whole["w_down"], 1024, 512, F32, res=x2)
    dx3, dx3b, dg_final, loss_row = _loss_head(x3, w["g_final"], target)

    dw = {}
    dact = _mm_nt("mm_dact", dx3b, whole["w_down"], 1024, 1408, D_MODEL, F32)
    dw["w_down"] = _mm_tn("mm_dw_down", act, dx3b, 1408, 1024, BF16)
    dgu, got = _swiglu_bwd(dact, gu, comm=pair(("w_down",)))
    sums_a, own_a = pair_sum("down", ("w_down",), got)
    dh2, slots_a = _mm_nt("mm_dh2", dgu, whole["w_gate_up"], 1024, 1024, 1408, F32, comm=_ChipExchange(sums_a, own_a))
    dw["w_gate_up"] = _mm_tn("mm_dw_gate_up", h2, dgu, 1024, 1408, BF16)
    (dx2, dx2b, dg_ffn), got = _rms_norm_bwd("norm_ffn_bwd", dh2, x2, w["g_ffn"], dx3, True, comm=pair(("w_gate_up",)))
    sums_b, own_b = pair_sum("gate_up", ("w_gate_up",), got)
    dmerged = _mm_nt("mm_dmerged", dx2b, whole["w_o"], 1024, 1024, D_MODEL, F32)
    dw["w_o"] = _mm_tn("mm_dw_o", merged, dx2b, 1024, 1024, BF16)
    dco, dao, dgc, dga = _merge_bwd(dmerged, conv_out, attn_out, proj)
    dconv_y = _mm_nt("mm_dconv_y", dco, whole["w_conv_out"], 1024, 1024, D_MODEL, F32)
    dw["w_conv_out"] = _mm_tn("mm_dw_conv_out", conv_y, dco, 1024, 1024, BF16)
    dattn = _mm_nt("mm_dattn", dao, whole["w_attn_out"], 1024, 1024, D_MODEL, BF16)
    dw["w_attn_out"] = _mm_tn("mm_dw_attn_out", attn, dao, 1024, 1024, BF16)
    (dcb, dcc, dcx, dconv_w), got = _conv_bwd(dconv_y, proj, conv_w_whole, comm=pair(mixers))
    sums_c, own_c = pair_sum("mixers", mixers, got)
    (dq, dk_prev, dk_cur, dv_prev, dv_cur, dsinks), slots_bc = _attn_bwd(
        qkv, dattn, sinks, tables, comm=_ChipExchange(sums_b + sums_c, own_b + own_c))
    dkv = _kv_grad_combine(dk_prev, dk_cur, dv_prev, dv_cur, tables)
    early = ("w_down", "w_gate_up") + mixers
    halves = _chip_sum("chip_sum_early", [spec[n] for n in early], slots_a + slots_bc, core1)
    dproj = jnp.concatenate([dcb, dcc, dcx, dq, dkv, dgc, dga], axis=1)
    dw["w_in"], reduced = _mm_tn("mm_dw_in", h1, dproj, 1024, 1664, BF16,
                                 comm=_HalfExchange(halves, [spec[n] for n in early]))
    g = dict(zip(early, reduced))
    dh1, got = _mm_nt("mm_dh1", dproj, whole["w_in"], 1024, 1024, 1664, F32, comm=pair(("w_in",)))
    sums_d, own_d = pair_sum("in", ("w_in",), got)
    (grad_x, dg_mix), own_d = _rms_norm_bwd("norm_mix_bwd", dh1, xs, w["g_mix"], dx2, False,
                                            comm=_ChipExchange(sums_d, own_d, 0, 2))
    small = _pack_small(dg_mix, dg_ffn, dg_final, dconv_w, dsinks, loss_row)
    updates, (slots_d, small_blocks) = _adamw(
        "adamw_early", [(w[n], g[n], m[n], v[n]) for n in early], 8,
        comm=_Both(_ChipExchange(sums_d, own_d, 1, 2), _SmallAllToAll(small)))
    delta, new_m, new_v = {}, {}, {}

    def keep(names, triples):
        for n, (d, nm, nv) in zip(names, triples):
            delta[n], new_m[n], new_v[n] = d, nm, nv

    keep(early, updates)
    half_in = _chip_sum("chip_sum_in", [spec["w_in"]], [slots_d], core1)
    (g["w_in"],) = _comm_call("half_exchange_in", _HalfExchange(half_in, [spec["w_in"]]))
    small_sum = _small_sum(small_blocks)
    g["g_mix"] = small_sum[0:1, :]
    g["g_ffn"] = small_sum[1:2, :]
    g["g_final"] = small_sum[2:3, :]
    g["conv_w"] = lax.dynamic_slice(small_sum, (3, shard * CONV_W_COLS), (3, CONV_W_COLS))
    g["attn_sinks"] = small_sum[6:7, :N_HEADS]
    loss = small_sum[7, 0]
    keep(("w_in",), _adamw("adamw_w_in", [(w["w_in"], g["w_in"], m["w_in"], v["w_in"])], 4))
    rest = ("g_mix", "g_ffn", "g_final", "conv_w", "attn_sinks")
    keep(rest, _adamw("adamw_small", [(w[n], g[n], m[n], v[n]) for n in rest], 1))

    def shaped(vals):
        return [vals[n].reshape((D_MODEL,)) if n == "g_final" else
                (vals[n][None] if n in MATRIX_NAMES or n == "conv_w" else vals[n]) for n in WEIGHT_ORDER]

    return (loss, grad_x[None], *shaped(g), *shaped(delta), *shaped(new_m), *shaped(new_v))
```

```python
import functools
import math

import jax
import jax.numpy as jnp
from jax import lax
from jax.experimental import pallas as pl
from jax.experimental.pallas import tpu as pltpu

F32 = jnp.float32
BF16 = jnp.bfloat16

D_MODEL = 1024
SEQ = 2048
HEAD_DIM = 64
N_HEADS = 16
N_KV_HEADS = 4
GROUP = N_HEADS // N_KV_HEADS
D_ATTN = N_HEADS * HEAD_DIM
D_KV = N_KV_HEADS * HEAD_DIM
BLOCK = 128
ROT_DIM = HEAD_DIM // 4
ROPE_THETA = 500000.0
ATTN_SCALE = 1.0 / math.sqrt(HEAD_DIM)
NEG_INF = -1e30
D_FF = 2816
EPS = 1e-5
N_IN = 3 * D_MODEL + D_ATTN + 2 * D_KV + 2 * D_MODEL
COL_Q = 3 * D_MODEL
COL_K = COL_Q + D_ATTN
COL_V = COL_K + D_KV
COL_GC = COL_V + D_KV
COL_GA = COL_GC + D_MODEL

ADAM_LR = 0.001
ADAM_B1 = 0.9
ADAM_B2 = 0.999
ADAM_EPS = 1e-08
ADAM_WD = 0.01
ADAM_STEP = 10

N_CHIPS = 4
N_DEV = 8

V7X_VMEM_BYTES = 64 * 1024 * 1024
VMEM_LIMIT = (V7X_VMEM_BYTES * 3) // 4
LANES = 128
MESH = pl.DeviceIdType.MESH


def _params(semantics=None):
    return pltpu.CompilerParams(dimension_semantics=semantics, vmem_limit_bytes=VMEM_LIMIT)


def _sds(shape, dtype):
    return jax.ShapeDtypeStruct(shape, dtype)


HBM_SPEC = pl.BlockSpec(memory_space=pl.ANY)


def _pcall(body, name, grid, in_specs, out_specs, out_shape, operands, scratch=(), semantics=None, comm=None,
           aliases=None):
    aliases = dict(aliases or {})
    if comm is None:
        return pl.pallas_call(
            body, name=name, grid=grid, in_specs=in_specs, out_specs=out_specs, out_shape=out_shape,
            scratch_shapes=list(scratch), input_output_aliases=aliases,
            compiler_params=_params(semantics))(*operands)
    multi = isinstance(out_shape, (list, tuple))
    o_specs = list(out_specs) if multi else [out_specs]
    o_shape = list(out_shape) if multi else [out_shape]
    n_in, n_out, n_scr = len(operands), len(o_shape), len(scratch)
    n_cin, n_cout = len(comm.operands), len(comm.out_shape)

    def hosted(*refs):
        ins, cins = refs[:n_in], refs[n_in:n_in + n_cin]
        o0 = n_in + n_cin
        outs, couts = refs[o0:o0 + n_out], refs[o0 + n_out:o0 + n_out + n_cout]
        s0 = o0 + n_out + n_cout
        scr, sems = refs[s0:s0 + n_scr], refs[s0 + n_scr:]
        first = last = None
        for axis, size in enumerate(grid):
            i = pl.program_id(axis)
            first = (i == 0) if first is None else first & (i == 0)
            last = (i == size - 1) if last is None else last & (i == size - 1)

        body(*ins, *outs, *scr)

        @pl.when(first)
        def _():
            comm.start(cins, couts, sems)

        @pl.when(last)
        def _():
            comm.finish(cins, couts, sems)

    res = pl.pallas_call(
        hosted, name=name, grid=grid,
        in_specs=list(in_specs) + [HBM_SPEC] * n_cin, out_specs=o_specs + [HBM_SPEC] * n_cout,
        out_shape=o_shape + list(comm.out_shape), scratch_shapes=list(scratch) + list(comm.sems),
        input_output_aliases={**aliases, **{n_in + a: n_out + b for a, b in comm.aliases.items()}},
        compiler_params=_params(("arbitrary",) * len(grid)))(*operands, *comm.operands)
    outs = list(res[:n_out])
    return (outs if multi else outs[0]), list(res[n_out:])


def _comm_call(name, comm):
    def body(*refs):
        n_cin, n_cout = len(comm.operands), len(comm.out_shape)
        cins, couts, sems = refs[:n_cin], refs[n_cin:n_cin + n_cout], refs[n_cin + n_cout:]
        comm.start(cins, couts, sems)
        comm.finish(cins, couts, sems)

    return list(pl.pallas_call(
        body, name=name, in_specs=[HBM_SPEC] * len(comm.operands), out_specs=[HBM_SPEC] * len(comm.out_shape),
        out_shape=list(comm.out_shape), scratch_shapes=list(comm.sems),
        input_output_aliases=dict(comm.aliases))(*comm.operands))


NN = ((1,), (0,))
NT = ((1,), (1,))
TN = ((0,), (0,))


def _matmul(name, a, b, dims, grid, a_spec, b_spec, o_spec, o_shape, o_dtype, res=None, res_spec=None, comm=None):
    nk = grid[2]

    def body(*refs):
        if res is None:
            a_ref, b_ref, o_ref = refs[:3]
            r_ref = None
            scratch = refs[3:]
        else:
            a_ref, b_ref, r_ref, o_ref = refs[:4]
            scratch = refs[4:]
        p = lax.dot_general(a_ref[...], b_ref[...], (dims, ((), ())), preferred_element_type=F32)

        def finish(acc):
            if r_ref is not None:
                acc = r_ref[...] + acc
            o_ref[...] = acc.astype(o_dtype)

        if nk == 1:
            finish(p)
        else:
            acc_ref = scratch[0]
            k = pl.program_id(2)

            @pl.when(k == 0)
            def _():
                acc_ref[...] = p

            @pl.when(k > 0)
            def _():
                acc_ref[...] += p

            @pl.when(k == nk - 1)
            def _():
                finish(acc_ref[...])

    operands = [a, b] if res is None else [a, b, res]
    in_specs = [a_spec, b_spec] if res is None else [a_spec, b_spec, res_spec]
    scratch = [pltpu.VMEM(o_spec.block_shape, F32)] if nk > 1 else []
    return _pcall(body, name, grid, in_specs, o_spec, _sds(o_shape, o_dtype), operands, scratch,
                  ("parallel", "parallel", "arbitrary"), comm)


def _mm_nn(name, a, b, bm, bn, o_dtype, res=None, comm=None):
    m, k = a.shape
    n = b.shape[1]
    return _matmul(
        name, a, b, NN, (m // bm, n // bn, 1),
        pl.BlockSpec((bm, k), lambda i, j, kk: (i, 0)),
        pl.BlockSpec((k, bn), lambda i, j, kk: (0, j)),
        pl.BlockSpec((bm, bn), lambda i, j, kk: (i, j)),
        (m, n), o_dtype, res,
        None if res is None else pl.BlockSpec((bm, bn), lambda i, j, kk: (i, j)), comm,
    )


def _mm_nt(name, a, b, bm, bn, bk, o_dtype, comm=None):
    m, k = a.shape
    n = b.shape[0]
    return _matmul(
        name, a, b, NT, (m // bm, n // bn, k // bk),
        pl.BlockSpec((bm, bk), lambda i, j, kk: (i, kk)),
        pl.BlockSpec((bn, bk), lambda i, j, kk: (j, kk)),
        pl.BlockSpec((bm, bn), lambda i, j, kk: (i, j)),
        (m, n), o_dtype, comm=comm,
    )


def _mm_tn(name, a, b, bm, bn, o_dtype, comm=None):
    k, m = a.shape
    n = b.shape[1]
    return _matmul(
        name, a, b, TN, (m // bm, n // bn, 1),
        pl.BlockSpec((k, bm), lambda i, j, kk: (0, i)),
        pl.BlockSpec((k, bn), lambda i, j, kk: (0, j)),
        pl.BlockSpec((bm, bn), lambda i, j, kk: (i, j)),
        (m, n), o_dtype, comm=comm,
    )


ROWS = 256


def _row_spec(width, col=0):
    return pl.BlockSpec((ROWS, width), lambda i: (i, col))


def _full_spec(shape):
    return pl.BlockSpec(shape, lambda *_: (0,) * len(shape))


def _rms_norm(name, x, g):
    def body(x_ref, g_ref, h_ref):
        xf = x_ref[...]
        r = lax.rsqrt(jnp.mean(xf * xf, axis=-1, keepdims=True) + EPS)
        h_ref[...] = ((xf * r) * g_ref[...]).astype(BF16)

    return pl.pallas_call(
        body, name=name, grid=(SEQ // ROWS,),
        in_specs=[_row_spec(D_MODEL), _full_spec((1, D_MODEL))],
        out_specs=_row_spec(D_MODEL),
        out_shape=_sds((SEQ, D_MODEL), BF16),
        compiler_params=_params(("parallel",)),
    )(x, g)


CONV_COLS = 256


def _shift_rows(u, k):
    rows = lax.broadcasted_iota(jnp.int32, u.shape, 0)
    return jnp.where(rows >= k, pltpu.roll(u, k, axis=0), 0.0)


def _conv_fwd(proj, conv_w):
    nblk = D_MODEL // CONV_COLS

    def body(cb_ref, cc_ref, cx_ref, w_ref, y_ref):
        u = cc_ref[...] * cx_ref[...]
        w = w_ref[...]
        cv = w[0:1, :] * _shift_rows(u, 2) + w[1:2, :] * _shift_rows(u, 1) + w[2:3, :] * u
        y_ref[...] = (cb_ref[...] * cv).astype(BF16)

    def col(part):
        return pl.BlockSpec((SEQ, CONV_COLS), lambda j: (0, part * nblk + j))

    return pl.pallas_call(
        body, name="conv_fwd", grid=(nblk,),
        in_specs=[col(0), col(1), col(2), pl.BlockSpec((3, CONV_COLS), lambda j: (0, j))],
        out_specs=pl.BlockSpec((SEQ, CONV_COLS), lambda j: (0, j)),
        out_shape=_sds((SEQ, D_MODEL), BF16),
        compiler_params=_params(("parallel",)),
    )(proj, proj, proj, conv_w)


ROPE_COLS = 256


def _rope_tables():
    inv_freq = ROPE_THETA ** (-jnp.arange(0, ROT_DIM, 2, dtype=F32) / ROT_DIM)
    ang = jnp.arange(SEQ, dtype=F32)[:, None] * inv_freq[None, :]
    cos, sin = jnp.cos(ang), jnp.sin(ang)
    half = ROT_DIM // 2
    ones = jnp.ones((SEQ, HEAD_DIM - ROT_DIM), F32)
    zeros = jnp.zeros((SEQ, HEAD_DIM - ROT_DIM), F32)
    zh = jnp.zeros((SEQ, half), F32)
    c = jnp.concatenate([cos, cos, ones], axis=1)
    s_up = jnp.concatenate([-sin, zh, zeros], axis=1)
    s_dn = jnp.concatenate([zh, sin, zeros], axis=1)
    reps = ROPE_COLS // HEAD_DIM
    return tuple(jnp.tile(t, (1, reps)) for t in (c, s_up, s_dn))


def _rotate(t, c, s_up, s_dn):
    width = t.shape[1]
    half = ROT_DIM // 2
    return t * c + pltpu.roll(t, width - half, axis=1) * s_up + pltpu.roll(t, half, axis=1) * s_dn


N_QBLK = SEQ // BLOCK


def _attn_specs():
    prev = lambda n: jnp.maximum(n - 1, 0)
    q = pl.BlockSpec((BLOCK, D_ATTN), lambda n: (n, COL_Q // D_ATTN))
    k_prev = pl.BlockSpec((BLOCK, D_KV), lambda n: (prev(n), COL_K // D_KV))
    k_cur = pl.BlockSpec((BLOCK, D_KV), lambda n: (n, COL_K // D_KV))
    v_prev = pl.BlockSpec((BLOCK, D_KV), lambda n: (prev(n), COL_V // D_KV))
    v_cur = pl.BlockSpec((BLOCK, D_KV), lambda n: (n, COL_V // D_KV))
    tab_cur = pl.BlockSpec((BLOCK, ROPE_COLS), lambda n: (n, 0))
    tab_prev = pl.BlockSpec((BLOCK, ROPE_COLS), lambda n: (prev(n), 0))
    return [q, k_prev, k_cur, v_prev, v_cur] + [tab_cur] * 3 + [tab_prev] * 3


def _band_kv(kp_ref, kc_ref, vp_ref, vc_ref, tabs_cur, tabs_prev):
    k = jnp.concatenate([_rotate(kp_ref[...], *(t[...] for t in tabs_prev)),
                         _rotate(kc_ref[...], *(t[...] for t in tabs_cur))], axis=0)
    v = jnp.concatenate([vp_ref[...], vc_ref[...]], axis=0)
    return k, v


def _query_tile(q_ref, tile, tabs_cur):
    c, su, sd = (t[:, :LANES] for t in tabs_cur)
    return _rotate(q_ref[:, tile * LANES:(tile + 1) * LANES], c, su, sd).astype(BF16)


def _band_mask(n):
    qi = lax.broadcasted_iota(jnp.int32, (BLOCK, 2 * BLOCK), 0)
    kj = lax.broadcasted_iota(jnp.int32, (BLOCK, 2 * BLOCK), 1)
    rel = qi + BLOCK - kj
    return (rel >= 0) & (rel < BLOCK) & ((kj >= BLOCK) | (n > 0))


HEADS_PER_TILE = LANES // HEAD_DIM


def _lane_half(shape, par):
    lane = lax.broadcasted_iota(jnp.int32, shape, 1)
    return (lane < HEAD_DIM) if par == 0 else (lane >= HEAD_DIM)


def _head_tiles(kv, h):
    tile = kv[:, (h // HEADS_PER_TILE) * LANES:(h // HEADS_PER_TILE + 1) * LANES].astype(F32)
    own = jnp.where(_lane_half(tile.shape, h % HEADS_PER_TILE), tile, 0.0)
    other = pltpu.roll(own, HEAD_DIM, axis=1)
    lo, hi = (own, other) if h % HEADS_PER_TILE == 0 else (other, own)
    return lo.astype(BF16), hi.astype(BF16)


def _head_softmax(q_tile, k_half, sink, mask):
    s = lax.dot_general(q_tile, k_half, (NT, ((), ())), preferred_element_type=F32) * ATTN_SCALE
    s = jnp.where(mask, s, NEG_INF)
    m = jnp.maximum(jnp.max(s, axis=-1, keepdims=True), sink)
    e = jnp.exp(s - m)
    es = jnp.exp(sink - m)
    inv = 1.0 / (jnp.sum(e, axis=-1, keepdims=True) + es)
    return e * inv, es * inv


def _attn_fwd(proj, tables, sinks, comm=None):
    def body(sink_ref, q_ref, kp_ref, kc_ref, vp_ref, vc_ref, c_ref, su_ref, sd_ref, cp_ref, sup_ref, sdp_ref, o_ref):
        n = pl.program_id(0)
        mask = _band_mask(n)
        tabs_cur = (c_ref, su_ref, sd_ref)
        k, v = _band_kv(kp_ref, kc_ref, vp_ref, vc_ref, tabs_cur, (cp_ref, sup_ref, sdp_ref))
        for h in range(N_KV_HEADS):
            k_halves = _head_tiles(k, h)
            v_halves = _head_tiles(v, h)
            for t in range(GROUP // HEADS_PER_TILE):
                tile = h * (GROUP // HEADS_PER_TILE) + t
                q_tile = _query_tile(q_ref, tile, tabs_cur)
                acc = None
                for par in range(HEADS_PER_TILE):
                    sink = sink_ref[0, tile * HEADS_PER_TILE + par]
                    p, _ = _head_softmax(q_tile, k_halves[par], sink, mask)
                    o = jnp.dot(p.astype(BF16), v_halves[par], preferred_element_type=F32)
                    acc = o if acc is None else acc + o
                o_ref[:, tile * LANES:(tile + 1) * LANES] = acc.astype(BF16)

    return _pcall(
        body, "attn_fwd", (N_QBLK,),
        [pl.BlockSpec(memory_space=pltpu.SMEM)] + _attn_specs(),
        pl.BlockSpec((BLOCK, D_ATTN), lambda n: (n, 0)),
        _sds((SEQ, D_ATTN), BF16), [sinks] + [proj] * 5 + list(tables) * 2, (), ("parallel",), comm)


def _branch_merge(conv_y, attn, w_co, w_ao, proj):
    bm, bn = 1024, 512

    def body(cy_ref, at_ref, wc_ref, wa_ref, gc_ref, ga_ref, co_ref, ao_ref, mg_ref):
        co = jnp.dot(cy_ref[...], wc_ref[...], preferred_element_type=F32)
        ao = jnp.dot(at_ref[...], wa_ref[...], preferred_element_type=F32)
        co_ref[...] = co
        ao_ref[...] = ao
        mg_ref[...] = (jax.nn.sigmoid(gc_ref[...]) * co + jax.nn.sigmoid(ga_ref[...]) * ao).astype(BF16)

    act = pl.BlockSpec((bm, D_MODEL), lambda i, j: (i, 0))
    wgt = pl.BlockSpec((D_MODEL, bn), lambda i, j: (0, j))
    out = pl.BlockSpec((bm, bn), lambda i, j: (i, j))
    return pl.pallas_call(
        body, name="branch_merge", grid=(SEQ // bm, D_MODEL // bn),
        in_specs=[act, act, wgt, wgt,
                  pl.BlockSpec((bm, bn), lambda i, j: (i, COL_GC // bn + j)),
                  pl.BlockSpec((bm, bn), lambda i, j: (i, COL_GA // bn + j))],
        out_specs=[out, out, out],
        out_shape=[_sds((SEQ, D_MODEL), F32), _sds((SEQ, D_MODEL), F32), _sds((SEQ, D_MODEL), BF16)],
        compiler_params=_params(("parallel", "parallel")),
    )(conv_y, attn, w_co, w_ao, proj, proj)


FF_ROWS = 128


def _swiglu_fwd(gu):
    def body(gu_ref, act_ref):
        g = gu_ref[:, :D_FF]
        act_ref[...] = (jax.nn.silu(g) * gu_ref[:, D_FF:]).astype(BF16)

    return pl.pallas_call(
        body, name="swiglu_fwd", grid=(SEQ // FF_ROWS,),
        in_specs=[pl.BlockSpec((FF_ROWS, 2 * D_FF), lambda i: (i, 0))],
        out_specs=pl.BlockSpec((FF_ROWS, D_FF), lambda i: (i, 0)),
        out_shape=_sds((SEQ, D_FF), BF16),
        compiler_params=_params(("parallel",)),
    )(gu)


def _loss_head(x3, g, target):
    def body(x_ref, g_ref, t_ref, dx_ref, dxb_ref, dg_ref, loss_ref):
        i = pl.program_id(0)
        xf = x_ref[...]
        r = lax.rsqrt(jnp.mean(xf * xf, axis=-1, keepdims=True) + EPS)
        xn = xf * r
        gg = g_ref[...]
        err = xn * gg - t_ref[...]
        part = 0.5 * jnp.sum(jnp.mean(err * err, axis=-1, keepdims=True), axis=0, keepdims=True)
        dy = err * (1.0 / D_MODEL)
        dxn = dy * gg
        dx = r * (dxn - xn * jnp.mean(dxn * xn, axis=-1, keepdims=True))
        dx_ref[...] = dx
        dxb_ref[...] = dx.astype(BF16)
        dg = jnp.sum(dy * xn, axis=0, keepdims=True)
        lane0 = lax.broadcasted_iota(jnp.int32, (1, LANES), 1) == 0
        lpart = jnp.where(lane0, part, 0.0)

        @pl.when(i == 0)
        def _():
            dg_ref[...] = dg
            loss_ref[...] = lpart

        @pl.when(i > 0)
        def _():
            dg_ref[...] += dg
            loss_ref[...] += lpart

    return pl.pallas_call(
        body, name="loss_head", grid=(SEQ // ROWS,),
        in_specs=[_row_spec(D_MODEL), _full_spec((1, D_MODEL)), _row_spec(D_MODEL)],
        out_specs=[_row_spec(D_MODEL), _row_spec(D_MODEL), _full_spec((1, D_MODEL)), _full_spec((1, LANES))],
        out_shape=[_sds((SEQ, D_MODEL), F32), _sds((SEQ, D_MODEL), BF16),
                   _sds((1, D_MODEL), F32), _sds((1, LANES), F32)],
        compiler_params=_params(("arbitrary",)),
    )(x3, g, target)


def _swiglu_bwd(dact, gu, comm=None):
    def body(da_ref, gu_ref, o_ref):
        g = gu_ref[:, :D_FF]
        up = gu_ref[:, D_FF:]
        da = da_ref[...]
        sg = jax.nn.sigmoid(g)
        o_ref[:, :D_FF] = (da * up * (sg * (1.0 + g * (1.0 - sg)))).astype(BF16)
        o_ref[:, D_FF:] = (da * (g * sg)).astype(BF16)

    return _pcall(
        body, "swiglu_bwd", (SEQ // FF_ROWS,),
        [pl.BlockSpec((FF_ROWS, D_FF), lambda i: (i, 0)), pl.BlockSpec((FF_ROWS, 2 * D_FF), lambda i: (i, 0))],
        pl.BlockSpec((FF_ROWS, 2 * D_FF), lambda i: (i, 0)),
        _sds((SEQ, 2 * D_FF), BF16), [dact, gu], (), ("parallel",), comm)


def _rms_norm_bwd(name, dh, x, g, dres, with_bf16, comm=None):
    def body(dh_ref, x_ref, g_ref, dr_ref, *outs):
        i = pl.program_id(0)
        dx_ref = outs[0]
        dg_ref = outs[-1]
        xf = x_ref[...]
        r = lax.rsqrt(jnp.mean(xf * xf, axis=-1, keepdims=True) + EPS)
        xn = xf * r
        dh = dh_ref[...]
        dxn = dh * g_ref[...]
        dx = dr_ref[...] + r * (dxn - xn * jnp.mean(dxn * xn, axis=-1, keepdims=True))
        dx_ref[...] = dx
        if with_bf16:
            outs[1][...] = dx.astype(BF16)
        dg = jnp.sum(dh * xn, axis=0, keepdims=True)

        @pl.when(i == 0)
        def _():
            dg_ref[...] = dg

        @pl.when(i > 0)
        def _():
            dg_ref[...] += dg

    row = _row_spec(D_MODEL)
    out_specs = [row] + ([row] if with_bf16 else []) + [_full_spec((1, D_MODEL))]
    out_shape = ([_sds((SEQ, D_MODEL), F32)] + ([_sds((SEQ, D_MODEL), BF16)] if with_bf16 else [])
                 + [_sds((1, D_MODEL), F32)])
    return _pcall(body, name, (SEQ // ROWS,), [row, row, _full_spec((1, D_MODEL)), row], out_specs, out_shape,
                  [dh, x, g, dres], (), ("arbitrary",), comm)


def _merge_bwd(dmerged, conv_out, attn_out, proj):
    def body(dm_ref, co_ref, ao_ref, gc_ref, ga_ref, dco_ref, dao_ref, dgc_ref, dga_ref):
        dm = dm_ref[...]
        sc = jax.nn.sigmoid(gc_ref[...])
        sa = jax.nn.sigmoid(ga_ref[...])
        dco_ref[...] = (dm * sc).astype(BF16)
        dao_ref[...] = (dm * sa).astype(BF16)
        dgc_ref[...] = (dm * co_ref[...] * (sc * (1.0 - sc))).astype(BF16)
        dga_ref[...] = (dm * ao_ref[...] * (sa * (1.0 - sa))).astype(BF16)

    half = D_MODEL // 2
    own = pl.BlockSpec((ROWS, half), lambda i, j: (i, j))
    sd = _sds((SEQ, D_MODEL), BF16)
    return pl.pallas_call(
        body, name="merge_bwd", grid=(SEQ // ROWS, 2),
        in_specs=[own, own, own,
                  pl.BlockSpec((ROWS, half), lambda i, j: (i, COL_GC // half + j)),
                  pl.BlockSpec((ROWS, half), lambda i, j: (i, COL_GA // half + j))],
        out_specs=[own, own, own, own], out_shape=[sd, sd, sd, sd],
        compiler_params=_params(("parallel", "parallel")),
    )(dmerged, conv_out, attn_out, proj, proj)


def _conv_bwd(dconv_y, proj, conv_w, comm=None):
    nblk = D_MODEL // CONV_COLS

    def body(dy_ref, cb_ref, cc_ref, cx_ref, w_ref, dcb_ref, dcc_ref, dcx_ref, dw_ref):
        cc = cc_ref[...]
        cx = cx_ref[...]
        u = cc * cx
        w = w_ref[...]
        u1 = _shift_rows(u, 1)
        u2 = _shift_rows(u, 2)
        cv = w[0:1, :] * u2 + w[1:2, :] * u1 + w[2:3, :] * u
        dy = dy_ref[...]
        dcb_ref[...] = (dy * cv).astype(BF16)
        dcv = dy * cb_ref[...]
        rows = lax.broadcasted_iota(jnp.int32, dcv.shape, 0)
        up1 = jnp.where(rows < SEQ - 1, pltpu.roll(dcv, SEQ - 1, axis=0), 0.0)
        up2 = jnp.where(rows < SEQ - 2, pltpu.roll(dcv, SEQ - 2, axis=0), 0.0)
        du = w[2:3, :] * dcv + w[1:2, :] * up1 + w[0:1, :] * up2
        dcc_ref[...] = (du * cx).astype(BF16)
        dcx_ref[...] = (du * cc).astype(BF16)
        dw_ref[...] = jnp.concatenate(
            [jnp.sum(dcv * u2, axis=0, keepdims=True),
             jnp.sum(dcv * u1, axis=0, keepdims=True),
             jnp.sum(dcv * u, axis=0, keepdims=True)], axis=0)

    def col(part):
        return pl.BlockSpec((SEQ, CONV_COLS), lambda j: (0, part * nblk + j))

    own = pl.BlockSpec((SEQ, CONV_COLS), lambda j: (0, j))
    wsp = pl.BlockSpec((3, CONV_COLS), lambda j: (0, j))
    sd = _sds((SEQ, D_MODEL), BF16)
    return _pcall(
        body, "conv_bwd", (nblk,), [own, col(0), col(1), col(2), wsp], [own, own, own, wsp],
        [sd, sd, sd, _sds((3, D_MODEL), F32)], [dconv_y, proj, proj, proj, conv_w], (), ("parallel",), comm)


def _attn_bwd(proj, dattn, sinks, tables, comm=None):
    def body(sink_ref, q_ref, kp_ref, kc_ref, vp_ref, vc_ref, c_ref, su_ref, sd_ref, cp_ref, sup_ref, sdp_ref,
             do_ref, dq_ref, dkp_ref, dkc_ref, dvp_ref, dvc_ref, ds_ref):
        n = pl.program_id(0)
        mask = _band_mask(n)
        tabs_cur = (c_ref, su_ref, sd_ref)
        k, v = _band_kv(kp_ref, kc_ref, vp_ref, vc_ref, tabs_cur, (cp_ref, sup_ref, sdp_ref))
        lane = lax.broadcasted_iota(jnp.int32, (1, LANES), 1)
        dsink = jnp.zeros((1, LANES), F32)
        c, su, sd = c_ref[:, :LANES], su_ref[:, :LANES], sd_ref[:, :LANES]
        dk_tiles = [None] * (N_KV_HEADS // HEADS_PER_TILE)
        dv_tiles = [None] * (N_KV_HEADS // HEADS_PER_TILE)
        for h in range(N_KV_HEADS):
            k_halves = _head_tiles(k, h)
            v_halves = _head_tiles(v, h)
            dk_par = [None] * HEADS_PER_TILE
            dv_par = [None] * HEADS_PER_TILE
            for t in range(GROUP // HEADS_PER_TILE):
                tile = h * (GROUP // HEADS_PER_TILE) + t
                q_tile = _query_tile(q_ref, tile, tabs_cur)
                do_tile = do_ref[:, tile * LANES:(tile + 1) * LANES]
                dq_tile = None
                for par in range(HEADS_PER_TILE):
                    head = tile * HEADS_PER_TILE + par
                    p, p_sink = _head_softmax(q_tile, k_halves[par], sink_ref[0, head], mask)
                    dp = lax.dot_general(do_tile, v_halves[par], (NT, ((), ())), preferred_element_type=F32)
                    delta = jnp.sum(p * dp, axis=-1, keepdims=True)
                    ds = (p * (dp - delta) * ATTN_SCALE).astype(BF16)
                    dq = jnp.dot(ds, k_halves[par], preferred_element_type=F32)
                    dq_tile = dq if dq_tile is None else dq_tile + dq
                    dk = lax.dot_general(ds, q_tile, (TN, ((), ())), preferred_element_type=F32)
                    dv = lax.dot_general(p.astype(BF16), do_tile, (TN, ((), ())), preferred_element_type=F32)
                    dk_par[par] = dk if dk_par[par] is None else dk_par[par] + dk
                    dv_par[par] = dv if dv_par[par] is None else dv_par[par] + dv
                    val = -jnp.sum(p_sink * delta, axis=0, keepdims=True)
                    dsink = dsink + jnp.where(lane == head, val, 0.0)
                dq_ref[:, tile * LANES:(tile + 1) * LANES] = _rotate(dq_tile, c, -su, -sd).astype(BF16)
            own = h % HEADS_PER_TILE
            for par_grads, tiles in ((dk_par, dk_tiles), (dv_par, dv_tiles)):
                shifted = pltpu.roll(par_grads[1 - own], HEAD_DIM, axis=1)
                total = jnp.where(_lane_half(shifted.shape, own), par_grads[own] + shifted, 0.0)
                i = h // HEADS_PER_TILE
                tiles[i] = total if tiles[i] is None else tiles[i] + total
        for i in range(N_KV_HEADS // HEADS_PER_TILE):
            cols = slice(i * LANES, (i + 1) * LANES)
            dkp_ref[:, cols] = dk_tiles[i][:BLOCK, :]
            dkc_ref[:, cols] = dk_tiles[i][BLOCK:, :]
            dvp_ref[:, cols] = dv_tiles[i][:BLOCK, :]
            dvc_ref[:, cols] = dv_tiles[i][BLOCK:, :]

        @pl.when(n == 0)
        def _():
            ds_ref[...] = dsink

        @pl.when(n > 0)
        def _():
            ds_ref[...] += dsink

    blk = pl.BlockSpec((BLOCK, D_KV), lambda n: (n, 0))
    kv = _sds((SEQ, D_KV), F32)
    return _pcall(
        body, "attn_bwd", (N_QBLK,),
        [pl.BlockSpec(memory_space=pltpu.SMEM)] + _attn_specs() + [pl.BlockSpec((BLOCK, D_ATTN), lambda n: (n, 0))],
        [pl.BlockSpec((BLOCK, D_ATTN), lambda n: (n, 0)), blk, blk, blk, blk, _full_spec((1, LANES))],
        [_sds((SEQ, D_ATTN), BF16), kv, kv, kv, kv, _sds((1, LANES), F32)],
        [sinks] + [proj] * 5 + list(tables) * 2 + [dattn], (), ("arbitrary",), comm)


def _kv_grad_combine(dk_prev, dk_cur, dv_prev, dv_cur, tables):
    def body(kp_ref, kc_ref, vp_ref, vc_ref, c_ref, su_ref, sd_ref, o_ref):
        m = pl.program_id(0)
        has_next = m < N_QBLK - 1
        dk = kc_ref[...] + jnp.where(has_next, kp_ref[...], 0.0)
        dv = vc_ref[...] + jnp.where(has_next, vp_ref[...], 0.0)
        o_ref[:, :D_KV] = _rotate(dk, c_ref[...], -su_ref[...], -sd_ref[...]).astype(BF16)
        o_ref[:, D_KV:] = dv.astype(BF16)

    cur = pl.BlockSpec((BLOCK, D_KV), lambda m: (m, 0))
    nxt = pl.BlockSpec((BLOCK, D_KV), lambda m: (jnp.minimum(m + 1, N_QBLK - 1), 0))
    return pl.pallas_call(
        body, name="kv_grad_combine", grid=(N_QBLK,),
        in_specs=[nxt, cur, nxt, cur, cur, cur, cur],
        out_specs=pl.BlockSpec((BLOCK, 2 * D_KV), lambda m: (m, 0)),
        out_shape=_sds((SEQ, 2 * D_KV), BF16),
        compiler_params=_params(("parallel",)),
    )(dk_prev, dk_cur, dv_prev, dv_cur, *tables)


MATRICES = {
    "w_in": (D_MODEL, N_IN // N_CHIPS, "col"),
    "w_conv_out": (D_MODEL // N_CHIPS, D_MODEL, "row"),
    "w_attn_out": (D_MODEL // N_CHIPS, D_MODEL, "row"),
    "w_o": (D_MODEL // N_CHIPS, D_MODEL, "row"),
    "w_gate_up": (D_MODEL, 2 * D_FF // N_CHIPS, "col"),
    "w_down": (D_FF // N_CHIPS, D_MODEL, "row"),
}
BF16_ROW_TILE = 16
CONV_W_COLS = D_MODEL // N_CHIPS
SMALL_ROWS = 8


def _whole_shape(spec):
    rows, cols, kind = spec
    return (rows, cols * N_CHIPS) if kind == "col" else (rows * N_CHIPS, cols)


def _half_shape(spec):
    return (spec[0] // 2, spec[1])


def _aligned(start, multiple):
    return start if isinstance(start, int) else pl.multiple_of(start, multiple)


def _region(ref, spec, shard, half, part=0, parts=1):
    rows, cols, kind = spec
    hr = rows // 2
    n = hr // parts
    if kind == "col":
        return ref.at[pl.ds(_aligned(half * hr + part * n, BF16_ROW_TILE), n),
                      pl.ds(_aligned(shard * cols, LANES), cols)]
    return ref.at[pl.ds(_aligned(shard * rows + half * hr + part * n, BF16_ROW_TILE), n), :]


def _position():
    x, y, c = lax.axis_index("x"), lax.axis_index("y"), lax.axis_index("c")
    chips = [(1 - x, y), (x, 1 - y), (1 - x, 1 - y)]
    return x, y, c, chips


def _shard_of(chip):
    return 2 * chip[0] + chip[1]


def _remote(src, dst, send_sem, recv_sem, to):
    return pltpu.make_async_remote_copy(src_ref=src, dst_ref=dst, send_sem=send_sem, recv_sem=recv_sem,
                                        device_id=to, device_id_type=MESH)


def _to_bf16_in_whole(name, w, spec, shard, rows):
    steps = spec[0] // rows

    def body(s_ref, w_ref, o_ref):
        del s_ref
        o_ref[...] = w_ref[...].astype(BF16)

    if spec[2] == "col":
        out_spec = pl.BlockSpec((rows, spec[1]), lambda i, s_ref: (i, s_ref[0]))
    else:
        out_spec = pl.BlockSpec((rows, spec[1]), lambda i, s_ref: (s_ref[0] * steps + i, 0))
    grid_spec = pltpu.PrefetchScalarGridSpec(
        num_scalar_prefetch=1, grid=(steps,),
        in_specs=[pl.BlockSpec((rows, spec[1]), lambda i, s_ref: (i, 0))], out_specs=out_spec)
    return pl.pallas_call(
        body, name=name, grid_spec=grid_spec, out_shape=_sds(_whole_shape(spec), BF16),
        compiler_params=_params(("parallel",)),
    )(shard, w)


class _Gather:
    def __init__(self, wholes, pieces, conv_w=None):
        self.pieces = pieces
        self.n = len(wholes)
        self.with_conv_w = conv_w is not None
        self.operands = list(wholes) + ([conv_w] if self.with_conv_w else [])
        self.out_shape = [_sds(w.shape, w.dtype) for w in wholes]
        if self.with_conv_w:
            self.out_shape.append(_sds((3, D_MODEL), F32))
        self.aliases = {i: i for i in range(self.n)}
        n_ici = 3 * len(pieces)
        self.sems = [pltpu.SemaphoreType.DMA((n_ici,))] * 4
        if self.with_conv_w:
            self.sems += [pltpu.SemaphoreType.DMA((1,)), pltpu.SemaphoreType.DMA((3,)), pltpu.SemaphoreType.DMA((3,))]

    def _conv_w(self, cins, couts, sems, with_recvs):
        cw_in, cw_out = cins[self.n], couts[self.n]
        x, y, c, chips = _position()

        def cols(shard):
            return cw_out.at[:, pl.ds(_aligned(shard * CONV_W_COLS, LANES), CONV_W_COLS)]

        me = _shard_of((x, y))
        local = pltpu.make_async_copy(cw_in, cols(me), sems[4].at[0])
        sends = [_remote(cw_in, cols(me), sems[5].at[j], sems[6].at[j], (*chip, c)) for j, chip in enumerate(chips)]
        if not with_recvs:
            return local, sends, []
        recvs = [_remote(cols(_shard_of(chip)), cols(_shard_of(chip)), sems[5].at[j], sems[6].at[j], (*chip, c))
                 for j, chip in enumerate(chips)]
        return local, sends, recvs

    def start(self, cins, couts, sems):
        x, y, c, chips = _position()
        me = _shard_of((x, y))
        if self.with_conv_w:
            local, sends, _ = self._conv_w(cins, couts, sems, False)
            local.start()
            for cp in sends:
                cp.start()
        for p, (i, spec, part, parts) in enumerate(self.pieces):
            mine = _region(couts[i], spec, me, c, part, parts)
            for j, chip in enumerate(chips):
                _remote(mine, mine, sems[0].at[3 * p + j], sems[1].at[3 * p + j], (*chip, c)).start()

    def finish(self, cins, couts, sems):
        x, y, c, chips = _position()
        me = _shard_of((x, y))
        sibling = (x, y, 1 - c)
        send_a, recv_a, send_b, recv_b = sems[:4]
        passed = []
        for p, (i, spec, part, parts) in enumerate(self.pieces):
            for j, chip in enumerate(chips):
                k = 3 * p + j
                landed = _region(couts[i], spec, _shard_of(chip), c, part, parts)
                _remote(landed, landed, send_a.at[k], recv_a.at[k], (*chip, c)).wait_recv()
                cp = _remote(landed, landed, send_b.at[k], recv_b.at[k], sibling)
                cp.start()
                passed.append(cp)
        for p, (i, spec, part, parts) in enumerate(self.pieces):
            mine = _region(couts[i], spec, me, c, part, parts)
            for j, chip in enumerate(chips):
                k = 3 * p + j
                other = _region(couts[i], spec, _shard_of(chip), 1 - c, part, parts)
                _remote(other, other, send_b.at[k], recv_b.at[k], sibling).wait_recv()
                _remote(mine, mine, send_a.at[k], recv_a.at[k], (*chip, c)).wait_send()
        for cp in passed:
            cp.wait_send()
        if self.with_conv_w:
            local, sends, recvs = self._conv_w(cins, couts, sems, True)
            for cp in recvs:
                cp.wait_recv()
            for cp in sends:
                cp.wait_send()
            local.wait()


def _mm_in_gather(h1, w_whole, comm):
    spec = MATRICES["w_in"]
    cols = spec[1]
    bm = SEQ // 2

    def body(h_ref, w_in_ref, proj_ref, w_ref, wbuf, obuf, send_a, recv_a, send_b, recv_b, load_sem, store_sems):
        del w_in_ref
        s, mi = pl.program_id(0), pl.program_id(1)
        x, y, c, chips = _position()
        me = _shard_of((x, y))
        sibling = (x, y, 1 - c)
        mine = _region(w_ref, spec, me, c)

        @pl.when((s == 0) & (mi == 0))
        def _():
            for j, chip in enumerate(chips):
                _remote(mine, mine, send_a.at[j], recv_a.at[j], (*chip, c)).start()

        shard = me
        for j, chip in enumerate(chips):
            shard = jnp.where(s == j + 1, _shard_of(chip), shard)

            @pl.when((s == j + 1) & (mi == 0))
            def _():
                landed = _region(w_ref, spec, _shard_of(chip), c)
                _remote(landed, landed, send_a.at[j], recv_a.at[j], (*chip, c)).wait_recv()
                _remote(landed, landed, send_b.at[j], recv_b.at[j], sibling).start()
                other = _region(w_ref, spec, _shard_of(chip), 1 - c)
                _remote(other, other, send_b.at[j], recv_b.at[j], sibling).wait_recv()

        col0 = pl.multiple_of(shard * cols, LANES)

        @pl.when(mi == 0)
        def _():
            load = pltpu.make_async_copy(w_ref.at[:, pl.ds(col0, cols)], wbuf, load_sem.at[0])
            load.start()
            load.wait()

        def store():
            rows = pl.ds(pl.multiple_of(mi * bm, bm), bm)
            return pltpu.make_async_copy(obuf.at[mi], proj_ref.at[rows, pl.ds(col0, cols)], store_sems.at[mi])

        @pl.when(s > 0)
        def _():
            store().wait()

        obuf[mi] = jnp.dot(h_ref[...], wbuf[...], preferred_element_type=F32)
        store().start()

        @pl.when(s == N_CHIPS - 1)
        def _():
            store().wait()

        @pl.when((s == N_CHIPS - 1) & (mi == 1))
        def _():
            for j, chip in enumerate(chips):
                landed = _region(w_ref, spec, _shard_of(chip), c)
                _remote(mine, mine, send_a.at[j], recv_a.at[j], (*chip, c)).wait_send()
                _remote(landed, landed, send_b.at[j], recv_b.at[j], sibling).wait_send()

    sem3 = pltpu.SemaphoreType.DMA((3,))
    (proj, whole), extra = _pcall(
        body, "mm_in", (N_CHIPS, SEQ // bm),
        [pl.BlockSpec((bm, D_MODEL), lambda s, m: (m, 0)), HBM_SPEC], [HBM_SPEC, HBM_SPEC],
        [_sds((SEQ, N_IN), F32), _sds(w_whole.shape, w_whole.dtype)], [h1, w_whole],
        [pltpu.VMEM((D_MODEL, cols), BF16), pltpu.VMEM((SEQ // bm, bm, cols), F32), sem3, sem3, sem3, sem3,
         pltpu.SemaphoreType.DMA((1,)), pltpu.SemaphoreType.DMA((SEQ // bm,))],
        None, comm, aliases={1: 1})
    return proj, whole, extra


def _pack_small(dg_mix, dg_ffn, dg_final, dconv_w, dsinks, loss_row):
    def body(a_ref, b_ref, c_ref, w_ref, s_ref, l_ref, o_ref):
        pad = jnp.zeros((1, D_MODEL - LANES), F32)
        o_ref[0:1, :] = a_ref[...]
        o_ref[1:2, :] = b_ref[...]
        o_ref[2:3, :] = c_ref[...]
        o_ref[3:6, :] = w_ref[...]
        o_ref[6:7, :] = jnp.concatenate([s_ref[...], pad], axis=1)
        o_ref[7:8, :] = jnp.concatenate([l_ref[...], pad], axis=1)

    return pl.pallas_call(
        body, name="pack_small", out_shape=_sds((SMALL_ROWS, D_MODEL), F32),
        compiler_params=_params(),
    )(dg_mix, dg_ffn, dg_final, dconv_w, dsinks, loss_row)


class _Pair:
    def __init__(self, dws, specs):
        self.specs = specs
        self.operands = list(dws)
        self.out_shape = [_sds((N_CHIPS, *_half_shape(s)), BF16) for s in specs]
        self.aliases = {}
        n = N_CHIPS * len(specs)
        self.sems = [pltpu.SemaphoreType.DMA((n,)), pltpu.SemaphoreType.DMA((n,))]

    def _copies(self, cins, couts, sems):
        x, y, c, _ = _position()
        sibling = (x, y, 1 - c)
        for i, spec in enumerate(self.specs):
            for t in range(N_CHIPS):
                k = N_CHIPS * i + t
                yield _remote(_region(cins[i], spec, t, 1 - c), couts[i].at[t], sems[0].at[k], sems[1].at[k], sibling)

    def start(self, cins, couts, sems):
        for cp in self._copies(cins, couts, sems):
            cp.start()

    def finish(self, cins, couts, sems):
        for cp in self._copies(cins, couts, sems):
            cp.wait()


class _SmallAllToAll:
    def __init__(self, small):
        self.operands = [small]
        self.out_shape = [_sds((N_DEV, SMALL_ROWS, D_MODEL), F32)]
        self.aliases = {}
        self.sems = [pltpu.SemaphoreType.DMA((N_DEV - 1,)), pltpu.SemaphoreType.DMA((N_DEV - 1,)),
                     pltpu.SemaphoreType.DMA((1,))]

    def _copies(self, cins, couts, sems):
        x, y, c, _ = _position()
        me = 4 * x + 2 * y + c
        out = []
        for r in range(1, N_DEV):
            flip = ((r >> 2) & 1, (r >> 1) & 1, r & 1)
            peer = tuple(1 - p if f else p for p, f in zip((x, y, c), flip))
            theirs = couts[0].at[4 * peer[0] + 2 * peer[1] + peer[2]]
            out.append((_remote(cins[0], couts[0].at[me], sems[0].at[r - 1], sems[1].at[r - 1], peer),
                        _remote(theirs, theirs, sems[0].at[r - 1], sems[1].at[r - 1], peer)))
        return pltpu.make_async_copy(cins[0], couts[0].at[me], sems[2].at[0]), out

    def start(self, cins, couts, sems):
        own, copies = self._copies(cins, couts, sems)
        own.start()
        for send, _ in copies:
            send.start()

    def finish(self, cins, couts, sems):
        own, copies = self._copies(cins, couts, sems)
        for send, recv in copies:
            recv.wait_recv()
            send.wait_send()
        own.wait()


class _Both:
    def __init__(self, a, b):
        self.a, self.b = a, b
        self.operands = list(a.operands) + list(b.operands)
        self.out_shape = list(a.out_shape) + list(b.out_shape)
        self.aliases = dict(a.aliases)
        self.aliases.update({len(a.operands) + k: len(a.out_shape) + v for k, v in b.aliases.items()})
        self.sems = list(a.sems) + list(b.sems)

    def _split(self, cins, couts, sems):
        na, ma, sa = len(self.a.operands), len(self.a.out_shape), len(self.a.sems)
        return (cins[:na], couts[:ma], sems[:sa]), (cins[na:], couts[ma:], sems[sa:])

    def start(self, cins, couts, sems):
        for plan, args in zip((self.a, self.b), self._split(cins, couts, sems)):
            plan.start(*args)

    def finish(self, cins, couts, sems):
        for plan, args in zip((self.a, self.b), self._split(cins, couts, sems)):
            plan.finish(*args)


def _pair_sum(name, specs, dws, got, place):
    n_mat = len(specs)

    def body(p_ref, *refs):
        t = pl.program_id(0)
        mine, theirs = refs[:n_mat], refs[n_mat:2 * n_mat]
        outs, owns = refs[2 * n_mat:3 * n_mat], refs[3 * n_mat:]
        for a, b, o, own in zip(mine, theirs, outs, owns):
            s = (a[...].astype(F32) + b[...].astype(F32)).astype(BF16)
            o[...] = s

            @pl.when(t == p_ref[1])
            def _():
                own[...] = s

    def mine_spec(spec):
        hr, cols = _half_shape(spec)
        if spec[2] == "col":
            return pl.BlockSpec((hr, cols), lambda t, p_ref: (p_ref[0], t))
        return pl.BlockSpec((hr, cols), lambda t, p_ref: (2 * t + p_ref[0], 0))

    def slot_spec(spec):
        return pl.BlockSpec((None, *_half_shape(spec)), lambda t, p_ref: (t, 0, 0))

    def own_spec(spec):
        return pl.BlockSpec((None, *_half_shape(spec)), lambda t, p_ref: (p_ref[1], 0, 0))

    slots = [_sds((N_CHIPS, *_half_shape(s)), BF16) for s in specs]
    grid_spec = pltpu.PrefetchScalarGridSpec(
        num_scalar_prefetch=1, grid=(N_CHIPS,),
        in_specs=[mine_spec(s) for s in specs] + [slot_spec(s) for s in specs],
        out_specs=[slot_spec(s) for s in specs] + [own_spec(s) for s in specs])
    res = pl.pallas_call(
        body, name=name, grid_spec=grid_spec, out_shape=slots + slots,
        compiler_params=_params(("arbitrary",)),
    )(place, *dws, *got)
    return list(res[:n_mat]), list(res[n_mat:])


class _ChipExchange:
    def __init__(self, sums, slots, part=0, parts=1):
        self.n = len(sums)
        self.part, self.parts = part, parts
        self.operands = list(sums) + list(slots)
        self.out_shape = [_sds(s.shape, s.dtype) for s in slots]
        self.aliases = {self.n + i: i for i in range(self.n)}
        self.sems = [pltpu.SemaphoreType.DMA((3 * self.n,)), pltpu.SemaphoreType.DMA((3 * self.n,))]

    def _rows(self, ref, slot):
        n = ref.shape[1] // self.parts
        return ref.at[slot, pl.ds(self.part * n, n), :]

    def _copies(self, cins, couts, sems):
        x, y, c, chips = _position()
        me = _shard_of((x, y))
        for i in range(self.n):
            for j, chip in enumerate(chips):
                k = 3 * i + j
                theirs = self._rows(couts[i], _shard_of(chip))
                yield (_remote(self._rows(cins[i], _shard_of(chip)), self._rows(couts[i], me),
                               sems[0].at[k], sems[1].at[k], (*chip, c)),
                       _remote(theirs, theirs, sems[0].at[k], sems[1].at[k], (*chip, c)))

    def start(self, cins, couts, sems):
        for send, _ in self._copies(cins, couts, sems):
            send.start()

    def finish(self, cins, couts, sems):
        for send, recv in self._copies(cins, couts, sems):
            recv.wait_recv()
            send.wait_send()


def _chip_sum(name, specs, slots, core):
    steps = 2
    n_mat = len(specs)

    def body(c_ref, *refs):
        del c_ref
        ins, outs = refs[:n_mat], refs[n_mat:]
        for a, o in zip(ins, outs):
            acc = a[0].astype(F32)
            for t in range(1, N_CHIPS):
                acc = acc + a[t].astype(F32)
            o[...] = acc

    def in_spec(spec):
        hr, cols = _half_shape(spec)
        return pl.BlockSpec((N_CHIPS, hr // steps, cols), lambda i, c_ref: (0, i, 0))

    def out_spec(spec):
        hr, cols = _half_shape(spec)
        return pl.BlockSpec((hr // steps, cols), lambda i, c_ref: (c_ref[0] * steps + i, 0))

    grid_spec = pltpu.PrefetchScalarGridSpec(
        num_scalar_prefetch=1, grid=(steps,),
        in_specs=[in_spec(s) for s in specs], out_specs=[out_spec(s) for s in specs])
    return list(pl.pallas_call(
        body, name=name, grid_spec=grid_spec,
        out_shape=[_sds((s[0], s[1]), F32) for s in specs],
        compiler_params=_params(("parallel",)),
    )(core, *slots))


class _HalfExchange:
    def __init__(self, grads, specs):
        self.specs = specs
        self.operands = list(grads)
        self.out_shape = [_sds(g.shape, g.dtype) for g in grads]
        self.aliases = {i: i for i in range(len(grads))}
        self.sems = [pltpu.SemaphoreType.DMA((len(grads),)), pltpu.SemaphoreType.DMA((len(grads),))]

    def _copies(self, couts, sems):
        x, y, c, _ = _position()
        sibling = (x, y, 1 - c)
        for i, spec in enumerate(self.specs):
            hr = spec[0] // 2
            mine = couts[i].at[pl.ds(_aligned(c * hr, 8), hr), :]
            theirs = couts[i].at[pl.ds(_aligned((1 - c) * hr, 8), hr), :]
            yield (_remote(mine, mine, sems[0].at[i], sems[1].at[i], sibling),
                   _remote(theirs, theirs, sems[0].at[i], sems[1].at[i], sibling))

    def start(self, cins, couts, sems):
        for send, _ in self._copies(couts, sems):
            send.start()

    def finish(self, cins, couts, sems):
        for send, recv in self._copies(couts, sems):
            recv.wait_recv()
            send.wait_send()


def _small_sum(blocks):
    def body(b_ref, o_ref):
        acc = b_ref[0]
        for d in range(1, N_DEV):
            acc = acc + b_ref[d]
        o_ref[...] = acc

    return pl.pallas_call(
        body, name="small_sum", out_shape=_sds((SMALL_ROWS, D_MODEL), F32), compiler_params=_params(),
    )(blocks)


def _adamw(name, params, steps, comm=None):
    n = len(params)

    def body(*refs):
        for p in range(n):
            w_ref, g_ref, m_ref, v_ref = refs[4 * p:4 * p + 4]
            d_ref, nm_ref, nv_ref = refs[4 * n + 3 * p:4 * n + 3 * p + 3]
            g = g_ref[...]
            m = ADAM_B1 * m_ref[...] + (1.0 - ADAM_B1) * g
            v = ADAM_B2 * v_ref[...] + (1.0 - ADAM_B2) * jnp.square(g)
            m_hat = m / (1.0 - ADAM_B1 ** ADAM_STEP)
            v_hat = v / (1.0 - ADAM_B2 ** ADAM_STEP)
            d_ref[...] = -ADAM_LR * (m_hat / (jnp.sqrt(v_hat) + ADAM_EPS) + ADAM_WD * w_ref[...])
            nm_ref[...] = m
            nv_ref[...] = v

    in_specs, out_specs, out_shape, operands = [], [], [], []
    for w, g, m, v in params:
        spec = pl.BlockSpec((w.shape[0] // steps, w.shape[1]), lambda i: (i, 0))
        in_specs += [spec] * 4
        out_specs += [spec] * 3
        out_shape += [_sds(w.shape, F32)] * 3
        operands += [w, g, m, v]
    res = _pcall(body, name, (steps,), in_specs, out_specs, out_shape, operands, (), ("parallel",), comm)
    outs, extra = res if comm is not None else (res, None)
    triples = [tuple(outs[3 * p:3 * p + 3]) for p in range(n)]
    return triples if comm is None else (triples, extra)


MATRIX_NAMES = tuple(MATRICES)
WEIGHT_ORDER = ("g_mix", "w_in", "conv_w", "attn_sinks", "w_conv_out", "w_attn_out", "w_o", "g_ffn",
                "w_gate_up", "w_down", "g_final")


def kernel(x, g_mix, w_in, conv_w, attn_sinks, w_conv_out, w_attn_out, w_o, g_ffn, w_gate_up, w_down, g_final, loss_target, m_g_mix, m_w_in, m_conv_w, m_attn_sinks, m_w_conv_out, m_w_attn_out, m_w_o, m_g_ffn, m_w_gate_up, m_w_down, m_g_final, v_g_mix, v_w_in, v_conv_w, v_attn_sinks, v_w_conv_out, v_w_attn_out, v_w_o, v_g_ffn, v_w_gate_up, v_w_down, v_g_final):
    w = dict(g_mix=g_mix, w_in=w_in[0], conv_w=conv_w[0], attn_sinks=attn_sinks, w_conv_out=w_conv_out[0],
             w_attn_out=w_attn_out[0], w_o=w_o[0], g_ffn=g_ffn, w_gate_up=w_gate_up[0], w_down=w_down[0],
             g_final=g_final[None, :])
    m = dict(g_mix=m_g_mix, w_in=m_w_in[0], conv_w=m_conv_w[0], attn_sinks=m_attn_sinks,
             w_conv_out=m_w_conv_out[0], w_attn_out=m_w_attn_out[0], w_o=m_w_o[0], g_ffn=m_g_ffn,
             w_gate_up=m_w_gate_up[0], w_down=m_w_down[0], g_final=m_g_final[None, :])
    v = dict(g_mix=v_g_mix, w_in=v_w_in[0], conv_w=v_conv_w[0], attn_sinks=v_attn_sinks,
             w_conv_out=v_w_conv_out[0], w_attn_out=v_w_attn_out[0], w_o=v_w_o[0], g_ffn=v_g_ffn,
             w_gate_up=v_w_gate_up[0], w_down=v_w_down[0], g_final=v_g_final[None, :])
    shard = (2 * lax.axis_index("x") + lax.axis_index("y")).astype(jnp.int32)
    core = lax.axis_index("c").astype(jnp.int32)
    shard1, core1, place = shard.reshape((1,)), core.reshape((1,)), jnp.stack([core, shard])
    spec = MATRICES
    xs, target, sinks = x[0], loss_target[0], w["attn_sinks"]
    tables = _rope_tables()

    def gather(names, part=0, parts=1):
        return _Gather([whole[n] for n in names], [(i, spec[n], part, parts) for i, n in enumerate(names)])

    def pair(names):
        return _Pair([dw[n] for n in names], [spec[n] for n in names])

    def pair_sum(tag, names, got):
        return _pair_sum("pair_sum_" + tag, [spec[n] for n in names], [dw[n] for n in names], got, place)

    cast_rows = {"w_down": D_FF // N_CHIPS // 2}
    whole = {n: _to_bf16_in_whole("cast_" + n, w[n], spec[n], shard1, cast_rows.get(n, 256)) for n in MATRIX_NAMES}

    mixers = ("w_conv_out", "w_attn_out", "w_o")
    h1 = _rms_norm("norm_mix", xs, w["g_mix"])
    proj, whole["w_in"], (*got, conv_w_whole) = _mm_in_gather(
        h1, whole["w_in"], _Gather([whole[n] for n in mixers], [(i, spec[n], 0, 1) for i, n in enumerate(mixers)],
                                   conv_w=w["conv_w"]))
    whole.update(zip(mixers, got))
    conv_y = _conv_fwd(proj, conv_w_whole)
    attn, (whole["w_gate_up"],) = _attn_fwd(proj, tables, sinks, comm=gather(("w_gate_up",)))
    conv_out, attn_out, merged = _branch_merge(conv_y, attn, whole["w_conv_out"], whole["w_attn_out"], proj)
    x2 = _mm_nn("mm_o", merged, whole["w_o"], 1024, 1024, F32, res=xs)
    h2 = _rms_norm("norm_ffn", x2, w["g_ffn"])
    gu, (whole["w_down"],) = _mm_nn("mm_gate_up", h2, whole["w_gate_up"], 1024, 1408, F32, comm=gather(("w_down",)))
    act = _swiglu_fwd(gu)
    x3 = _mm_nn("mm_down", act, whole["w_down"], 1024, 512, F32, res=x2)
    dx3, dx3b, dg_final, loss_row = _loss_head(x3, w["g_final"], target)

    dw = {}
    dact = _mm_nt("mm_dact", dx3b, whole["w_down"], 1024, 1408, D_MODEL, F32)
    dw["w_down"] = _mm_tn("mm_dw_down", act, dx3b, 1408, 1024, BF16)
    dgu, got = _swiglu_bwd(dact, gu, comm=pair(("w_down",)))
    sums_a, own_a = pair_sum("down", ("w_down",), got)
    dh2, slots_a = _mm_nt("mm_dh2", dgu, whole["w_gate_up"], 1024, 1024, 1408, F32, comm=_ChipExchange(sums_a, own_a))
    dw["w_gate_up"] = _mm_tn("mm_dw_gate_up", h2, dgu, 1024, 1408, BF16)
    (dx2, dx2b, dg_ffn), got = _rms_norm_bwd("norm_ffn_bwd", dh2, x2, w["g_ffn"], dx3, True, comm=pair(("w_gate_up",)))
    sums_b, own_b = pair_sum("gate_up", ("w_gate_up",), got)
    dmerged = _mm_nt("mm_dmerged", dx2b, whole["w_o"], 1024, 1024, D_MODEL, F32)
    dw["w_o"] = _mm_tn("mm_dw_o", merged, dx2b, 1024, 1024, BF16)
    dco, dao, dgc, dga = _merge_bwd(dmerged, conv_out, attn_out, proj)
    dconv_y = _mm_nt("mm_dconv_y", dco, whole["w_conv_out"], 1024, 1024, D_MODEL, F32)
    dw["w_conv_out"] = _mm_tn("mm_dw_conv_out", conv_y, dco, 1024, 1024, BF16)
    dattn = _mm_nt("mm_dattn", dao, whole["w_attn_out"], 1024, 1024, D_MODEL, BF16)
    dw["w_attn_out"] = _mm_tn("mm_dw_attn_out", attn, dao, 1024, 1024, BF16)
    (dcb, dcc, dcx, dconv_w), got = _conv_bwd(dconv_y, proj, conv_w_whole, comm=pair(mixers))
    sums_c, own_c = pair_sum("mixers", mixers, got)
    (dq, dk_prev, dk_cur, dv_prev, dv_cur, dsinks), slots_b = _attn_bwd(
        proj, dattn, sinks, tables, comm=_ChipExchange(sums_b, own_b))
    dkv = _kv_grad_combine(dk_prev, dk_cur, dv_prev, dv_cur, tables)
    dproj = jnp.concatenate([dcb, dcc, dcx, dq, dkv, dgc, dga], axis=1)
    dw["w_in"], slots_c = _mm_tn("mm_dw_in", h1, dproj, 1024, 1664, BF16, comm=_ChipExchange(sums_c, own_c))
    sums_d, own_d = pair_sum("in", ("w_in",), _comm_call("pair_exchange_in", pair(("w_in",))))
    early = ("w_down", "w_gate_up") + mixers
    halves = _chip_sum("chip_sum_early", [spec[n] for n in early], slots_a + slots_b + slots_c, core1)
    dh1, (own_d, *reduced) = _mm_nt(
        "mm_dh1", dproj, whole["w_in"], 1024, 1024, 1664, F32,
        comm=_Both(_ChipExchange(sums_d, own_d, 0, 2), _HalfExchange(halves, [spec[n] for n in early])))
    g = dict(zip(early, reduced))
    (grad_x, dg_mix), slots_d = _rms_norm_bwd("norm_mix_bwd", dh1, xs, w["g_mix"], dx2, False,
                                              comm=_ChipExchange(sums_d, [own_d], 1, 2))
    small = _pack_small(dg_mix, dg_ffn, dg_final, dconv_w, dsinks, loss_row)
    half_in = _chip_sum("chip_sum_in", [spec["w_in"]], slots_d, core1)
    g["w_in"], small_blocks = _comm_call(
        "half_exchange_in", _Both(_HalfExchange(half_in, [spec["w_in"]]), _SmallAllToAll(small)))
    delta, new_m, new_v = {}, {}, {}

    def keep(names, triples):
        for n, (d, nm, nv) in zip(names, triples):
            delta[n], new_m[n], new_v[n] = d, nm, nv

    keep(early, _adamw("adamw_early", [(w[n], g[n], m[n], v[n]) for n in early], 8))
    small_sum = _small_sum(small_blocks)
    g["g_mix"] = small_sum[0:1, :]
    g["g_ffn"] = small_sum[1:2, :]
    g["g_final"] = small_sum[2:3, :]
    g["conv_w"] = lax.dynamic_slice(small_sum, (3, shard * CONV_W_COLS), (3, CONV_W_COLS))
    g["attn_sinks"] = small_sum[6:7, :N_HEADS]
    loss = small_sum[7, 0]
    keep(("w_in",), _adamw("adamw_w_in", [(w["w_in"], g["w_in"], m["w_in"], v["w_in"])], 4))
    rest = ("g_mix", "g_ffn", "g_final", "conv_w", "attn_sinks")
    keep(rest, _adamw("adamw_small", [(w[n], g[n], m[n], v[n]) for n in rest], 1))

    def shaped(vals):
        return [vals[n].reshape((D_MODEL,)) if n == "g_final" else
                (vals[n][None] if n in MATRIX_NAMES or n == "conv_w" else vals[n]) for n in WEIGHT_ORDER]

    return (loss, grad_x[None], *shaped(g), *shaped(delta), *shaped(new_m), *shaped(new_v))
```

```python
import functools
import math

import jax
import jax.numpy as jnp
from jax import lax
from jax.experimental import pallas as pl
from jax.experimental.pallas import tpu as pltpu

F32 = jnp.float32
BF16 = jnp.bfloat16

D_MODEL = 1024
SEQ = 2048
HEAD_DIM = 64
N_HEADS = 16
N_KV_HEADS = 4
GROUP = N_HEADS // N_KV_HEADS
D_ATTN = N_HEADS * HEAD_DIM
D_KV = N_KV_HEADS * HEAD_DIM
BLOCK = 128
ROT_DIM = HEAD_DIM // 4
ROPE_THETA = 500000.0
ATTN_SCALE = 1.0 / math.sqrt(HEAD_DIM)
NEG_INF = -1e30
D_FF = 2816
EPS = 1e-5
N_IN = 3 * D_MODEL + D_ATTN + 2 * D_KV + 2 * D_MODEL
COL_Q = 3 * D_MODEL
COL_K = COL_Q + D_ATTN
COL_V = COL_K + D_KV
COL_GC = COL_V + D_KV
COL_GA = COL_GC + D_MODEL

ADAM_LR = 0.001
ADAM_B1 = 0.9
ADAM_B2 = 0.999
ADAM_EPS = 1e-08
ADAM_WD = 0.01
ADAM_STEP = 10

N_CHIPS = 4
N_DEV = 8

V7X_VMEM_BYTES = 64 * 1024 * 1024
VMEM_LIMIT = (V7X_VMEM_BYTES * 3) // 4
LANES = 128
MESH = pl.DeviceIdType.MESH


def _params(semantics=None):
    return pltpu.CompilerParams(dimension_semantics=semantics, vmem_limit_bytes=VMEM_LIMIT)


def _sds(shape, dtype):
    return jax.ShapeDtypeStruct(shape, dtype)


HBM_SPEC = pl.BlockSpec(memory_space=pl.ANY)


def _pcall(body, name, grid, in_specs, out_specs, out_shape, operands, scratch=(), semantics=None, comm=None,
           aliases=None):
    aliases = dict(aliases or {})
    if comm is None:
        return pl.pallas_call(
            body, name=name, grid=grid, in_specs=in_specs, out_specs=out_specs, out_shape=out_shape,
            scratch_shapes=list(scratch), input_output_aliases=aliases,
            compiler_params=_params(semantics))(*operands)
    multi = isinstance(out_shape, (list, tuple))
    o_specs = list(out_specs) if multi else [out_specs]
    o_shape = list(out_shape) if multi else [out_shape]
    n_in, n_out, n_scr = len(operands), len(o_shape), len(scratch)
    n_cin, n_cout = len(comm.operands), len(comm.out_shape)

    def hosted(*refs):
        ins, cins = refs[:n_in], refs[n_in:n_in + n_cin]
        o0 = n_in + n_cin
        outs, couts = refs[o0:o0 + n_out], refs[o0 + n_out:o0 + n_out + n_cout]
        s0 = o0 + n_out + n_cout
        scr, sems = refs[s0:s0 + n_scr], refs[s0 + n_scr:]
        first = last = None
        for axis, size in enumerate(grid):
            i = pl.program_id(axis)
            first = (i == 0) if first is None else first & (i == 0)
            last = (i == size - 1) if last is None else last & (i == size - 1)

        body(*ins, *outs, *scr)

        @pl.when(first)
        def _():
            comm.start(cins, couts, sems)

        @pl.when(last)
        def _():
            comm.finish(cins, couts, sems)

    res = pl.pallas_call(
        hosted, name=name, grid=grid,
        in_specs=list(in_specs) + [HBM_SPEC] * n_cin, out_specs=o_specs + [HBM_SPEC] * n_cout,
        out_shape=o_shape + list(comm.out_shape), scratch_shapes=list(scratch) + list(comm.sems),
        input_output_aliases={**aliases, **{n_in + a: n_out + b for a, b in comm.aliases.items()}},
        compiler_params=_params(("arbitrary",) * len(grid)))(*operands, *comm.operands)
    outs = list(res[:n_out])
    return (outs if multi else outs[0]), list(res[n_out:])


def _comm_call(name, comm):
    def body(*refs):
        n_cin, n_cout = len(comm.operands), len(comm.out_shape)
        cins, couts, sems = refs[:n_cin], refs[n_cin:n_cin + n_cout], refs[n_cin + n_cout:]
        comm.start(cins, couts, sems)
        comm.finish(cins, couts, sems)

    return list(pl.pallas_call(
        body, name=name, in_specs=[HBM_SPEC] * len(comm.operands), out_specs=[HBM_SPEC] * len(comm.out_shape),
        out_shape=list(comm.out_shape), scratch_shapes=list(comm.sems),
        input_output_aliases=dict(comm.aliases))(*comm.operands))


NN = ((1,), (0,))
NT = ((1,), (1,))
TN = ((0,), (0,))


def _matmul(name, a, b, dims, grid, a_spec, b_spec, o_spec, o_shape, o_dtype, res=None, res_spec=None, comm=None):
    nk = grid[2]

    def body(*refs):
        if res is None:
            a_ref, b_ref, o_ref = refs[:3]
            r_ref = None
            scratch = refs[3:]
        else:
            a_ref, b_ref, r_ref, o_ref = refs[:4]
            scratch = refs[4:]
        p = lax.dot_general(a_ref[...], b_ref[...], (dims, ((), ())), preferred_element_type=F32)

        def finish(acc):
            if r_ref is not None:
                acc = r_ref[...] + acc
            o_ref[...] = acc.astype(o_dtype)

        if nk == 1:
            finish(p)
        else:
            acc_ref = scratch[0]
            k = pl.program_id(2)

            @pl.when(k == 0)
            def _():
                acc_ref[...] = p

            @pl.when(k > 0)
            def _():
                acc_ref[...] += p

            @pl.when(k == nk - 1)
            def _():
                finish(acc_ref[...])

    operands = [a, b] if res is None else [a, b, res]
    in_specs = [a_spec, b_spec] if res is None else [a_spec, b_spec, res_spec]
    scratch = [pltpu.VMEM(o_spec.block_shape, F32)] if nk > 1 else []
    return _pcall(body, name, grid, in_specs, o_spec, _sds(o_shape, o_dtype), operands, scratch,
                  ("parallel", "parallel", "arbitrary"), comm)


def _mm_nn(name, a, b, bm, bn, o_dtype, res=None, comm=None):
    m, k = a.shape
    n = b.shape[1]
    return _matmul(
        name, a, b, NN, (m // bm, n // bn, 1),
        pl.BlockSpec((bm, k), lambda i, j, kk: (i, 0)),
        pl.BlockSpec((k, bn), lambda i, j, kk: (0, j)),
        pl.BlockSpec((bm, bn), lambda i, j, kk: (i, j)),
        (m, n), o_dtype, res,
        None if res is None else pl.BlockSpec((bm, bn), lambda i, j, kk: (i, j)), comm,
    )


def _mm_nt(name, a, b, bm, bn, bk, o_dtype, comm=None):
    m, k = a.shape
    n = b.shape[0]
    return _matmul(
        name, a, b, NT, (m // bm, n // bn, k // bk),
        pl.BlockSpec((bm, bk), lambda i, j, kk: (i, kk)),
        pl.BlockSpec((bn, bk), lambda i, j, kk: (j, kk)),
        pl.BlockSpec((bm, bn), lambda i, j, kk: (i, j)),
        (m, n), o_dtype, comm=comm,
    )


def _mm_tn(name, a, b, bm, bn, o_dtype, comm=None):
    k, m = a.shape
    n = b.shape[1]
    return _matmul(
        name, a, b, TN, (m // bm, n // bn, 1),
        pl.BlockSpec((k, bm), lambda i, j, kk: (0, i)),
        pl.BlockSpec((k, bn), lambda i, j, kk: (0, j)),
        pl.BlockSpec((bm, bn), lambda i, j, kk: (i, j)),
        (m, n), o_dtype, comm=comm,
    )


ROWS = 256


def _row_spec(width, col=0):
    return pl.BlockSpec((ROWS, width), lambda i: (i, col))


def _full_spec(shape):
    return pl.BlockSpec(shape, lambda *_: (0,) * len(shape))


def _rms_norm(name, x, g):
    def body(x_ref, g_ref, h_ref):
        xf = x_ref[...]
        r = lax.rsqrt(jnp.mean(xf * xf, axis=-1, keepdims=True) + EPS)
        h_ref[...] = ((xf * r) * g_ref[...]).astype(BF16)

    return pl.pallas_call(
        body, name=name, grid=(SEQ // ROWS,),
        in_specs=[_row_spec(D_MODEL), _full_spec((1, D_MODEL))],
        out_specs=_row_spec(D_MODEL),
        out_shape=_sds((SEQ, D_MODEL), BF16),
        compiler_params=_params(("parallel",)),
    )(x, g)


CONV_COLS = 256


def _shift_rows(u, k):
    rows = lax.broadcasted_iota(jnp.int32, u.shape, 0)
    return jnp.where(rows >= k, pltpu.roll(u, k, axis=0), 0.0)


def _conv_fwd(proj, conv_w):
    nblk = D_MODEL // CONV_COLS

    def body(cb_ref, cc_ref, cx_ref, w_ref, y_ref):
        u = cc_ref[...] * cx_ref[...]
        w = w_ref[...]
        cv = w[0:1, :] * _shift_rows(u, 2) + w[1:2, :] * _shift_rows(u, 1) + w[2:3, :] * u
        y_ref[...] = (cb_ref[...] * cv).astype(BF16)

    def col(part):
        return pl.BlockSpec((SEQ, CONV_COLS), lambda j: (0, part * nblk + j))

    return pl.pallas_call(
        body, name="conv_fwd", grid=(nblk,),
        in_specs=[col(0), col(1), col(2), pl.BlockSpec((3, CONV_COLS), lambda j: (0, j))],
        out_specs=pl.BlockSpec((SEQ, CONV_COLS), lambda j: (0, j)),
        out_shape=_sds((SEQ, D_MODEL), BF16),
        compiler_params=_params(("parallel",)),
    )(proj, proj, proj, conv_w)


ROPE_COLS = 256


def _rope_tables():
    inv_freq = ROPE_THETA ** (-jnp.arange(0, ROT_DIM, 2, dtype=F32) / ROT_DIM)
    ang = jnp.arange(SEQ, dtype=F32)[:, None] * inv_freq[None, :]
    cos, sin = jnp.cos(ang), jnp.sin(ang)
    half = ROT_DIM // 2
    ones = jnp.ones((SEQ, HEAD_DIM - ROT_DIM), F32)
    zeros = jnp.zeros((SEQ, HEAD_DIM - ROT_DIM), F32)
    zh = jnp.zeros((SEQ, half), F32)
    c = jnp.concatenate([cos, cos, ones], axis=1)
    s_up = jnp.concatenate([-sin, zh, zeros], axis=1)
    s_dn = jnp.concatenate([zh, sin, zeros], axis=1)
    reps = ROPE_COLS // HEAD_DIM
    return tuple(jnp.tile(t, (1, reps)) for t in (c, s_up, s_dn))


def _rotate(t, c, s_up, s_dn):
    width = t.shape[1]
    half = ROT_DIM // 2
    return t * c + pltpu.roll(t, width - half, axis=1) * s_up + pltpu.roll(t, half, axis=1) * s_dn


N_QBLK = SEQ // BLOCK


def _attn_specs():
    prev = lambda n: jnp.maximum(n - 1, 0)
    q = pl.BlockSpec((BLOCK, D_ATTN), lambda n: (n, COL_Q // D_ATTN))
    k_prev = pl.BlockSpec((BLOCK, D_KV), lambda n: (prev(n), COL_K // D_KV))
    k_cur = pl.BlockSpec((BLOCK, D_KV), lambda n: (n, COL_K // D_KV))
    v_prev = pl.BlockSpec((BLOCK, D_KV), lambda n: (prev(n), COL_V // D_KV))
    v_cur = pl.BlockSpec((BLOCK, D_KV), lambda n: (n, COL_V // D_KV))
    tab_cur = pl.BlockSpec((BLOCK, ROPE_COLS), lambda n: (n, 0))
    tab_prev = pl.BlockSpec((BLOCK, ROPE_COLS), lambda n: (prev(n), 0))
    return [q, k_prev, k_cur, v_prev, v_cur] + [tab_cur] * 3 + [tab_prev] * 3


def _band_kv(kp_ref, kc_ref, vp_ref, vc_ref, tabs_cur, tabs_prev):
    k = jnp.concatenate([_rotate(kp_ref[...], *(t[...] for t in tabs_prev)),
                         _rotate(kc_ref[...], *(t[...] for t in tabs_cur))], axis=0)
    v = jnp.concatenate([vp_ref[...], vc_ref[...]], axis=0)
    return k, v


def _query_tile(q_ref, tile, tabs_cur):
    c, su, sd = (t[:, :LANES] for t in tabs_cur)
    return _rotate(q_ref[:, tile * LANES:(tile + 1) * LANES], c, su, sd).astype(BF16)


def _band_mask(n):
    qi = lax.broadcasted_iota(jnp.int32, (BLOCK, 2 * BLOCK), 0)
    kj = lax.broadcasted_iota(jnp.int32, (BLOCK, 2 * BLOCK), 1)
    rel = qi + BLOCK - kj
    return (rel >= 0) & (rel < BLOCK) & ((kj >= BLOCK) | (n > 0))


HEADS_PER_TILE = LANES // HEAD_DIM


def _lane_half(shape, par):
    lane = lax.broadcasted_iota(jnp.int32, shape, 1)
    return (lane < HEAD_DIM) if par == 0 else (lane >= HEAD_DIM)


def _head_tiles(kv, h):
    tile = kv[:, (h // HEADS_PER_TILE) * LANES:(h // HEADS_PER_TILE + 1) * LANES].astype(F32)
    own = jnp.where(_lane_half(tile.shape, h % HEADS_PER_TILE), tile, 0.0)
    other = pltpu.roll(own, HEAD_DIM, axis=1)
    lo, hi = (own, other) if h % HEADS_PER_TILE == 0 else (other, own)
    return lo.astype(BF16), hi.astype(BF16)


def _head_softmax(q_tile, k_half, sink, mask):
    s = lax.dot_general(q_tile, k_half, (NT, ((), ())), preferred_element_type=F32) * ATTN_SCALE
    s = jnp.where(mask, s, NEG_INF)
    m = jnp.maximum(jnp.max(s, axis=-1, keepdims=True), sink)
    e = jnp.exp(s - m)
    es = jnp.exp(sink - m)
    inv = 1.0 / (jnp.sum(e, axis=-1, keepdims=True) + es)
    return e * inv, es * inv


def _attn_fwd(proj, tables, sinks, comm=None):
    def body(sink_ref, q_ref, kp_ref, kc_ref, vp_ref, vc_ref, c_ref, su_ref, sd_ref, cp_ref, sup_ref, sdp_ref, o_ref):
        n = pl.program_id(0)
        mask = _band_mask(n)
        tabs_cur = (c_ref, su_ref, sd_ref)
        k, v = _band_kv(kp_ref, kc_ref, vp_ref, vc_ref, tabs_cur, (cp_ref, sup_ref, sdp_ref))
        for h in range(N_KV_HEADS):
            k_halves = _head_tiles(k, h)
            v_halves = _head_tiles(v, h)
            for t in range(GROUP // HEADS_PER_TILE):
                tile = h * (GROUP // HEADS_PER_TILE) + t
                q_tile = _query_tile(q_ref, tile, tabs_cur)
                acc = None
                for par in range(HEADS_PER_TILE):
                    sink = sink_ref[0, tile * HEADS_PER_TILE + par]
                    p, _ = _head_softmax(q_tile, k_halves[par], sink, mask)
                    o = jnp.dot(p.astype(BF16), v_halves[par], preferred_element_type=F32)
                    acc = o if acc is None else acc + o
                o_ref[:, tile * LANES:(tile + 1) * LANES] = acc.astype(BF16)

    return _pcall(
        body, "attn_fwd", (N_QBLK,),
        [pl.BlockSpec(memory_space=pltpu.SMEM)] + _attn_specs(),
        pl.BlockSpec((BLOCK, D_ATTN), lambda n: (n, 0)),
        _sds((SEQ, D_ATTN), BF16), [sinks] + [proj] * 5 + list(tables) * 2, (), ("parallel",), comm)


def _branch_merge(conv_y, attn, w_co, w_ao, proj):
    bm, bn = 1024, 512

    def body(cy_ref, at_ref, wc_ref, wa_ref, gc_ref, ga_ref, co_ref, ao_ref, mg_ref):
        co = jnp.dot(cy_ref[...], wc_ref[...], preferred_element_type=F32)
        ao = jnp.dot(at_ref[...], wa_ref[...], preferred_element_type=F32)
        co_ref[...] = co
        ao_ref[...] = ao
        mg_ref[...] = (jax.nn.sigmoid(gc_ref[...]) * co + jax.nn.sigmoid(ga_ref[...]) * ao).astype(BF16)

    act = pl.BlockSpec((bm, D_MODEL), lambda i, j: (i, 0))
    wgt = pl.BlockSpec((D_MODEL, bn), lambda i, j: (0, j))
    out = pl.BlockSpec((bm, bn), lambda i, j: (i, j))
    return pl.pallas_call(
        body, name="branch_merge", grid=(SEQ // bm, D_MODEL // bn),
        in_specs=[act, act, wgt, wgt,
                  pl.BlockSpec((bm, bn), lambda i, j: (i, COL_GC // bn + j)),
                  pl.BlockSpec((bm, bn), lambda i, j: (i, COL_GA // bn + j))],
        out_specs=[out, out, out],
        out_shape=[_sds((SEQ, D_MODEL), F32), _sds((SEQ, D_MODEL), F32), _sds((SEQ, D_MODEL), BF16)],
        compiler_params=_params(("parallel", "parallel")),
    )(conv_y, attn, w_co, w_ao, proj, proj)


FF_BM, FF_BN = 512, 1408
FF_NB = D_FF // FF_BN


def _gate_up_fwd(h2, w_gu, comm=None):
    def body(h_ref, wg_ref, wu_ref, g_ref, u_ref, a_ref):
        h = h_ref[...]
        g = jnp.dot(h, wg_ref[...], preferred_element_type=F32)
        u = jnp.dot(h, wu_ref[...], preferred_element_type=F32)
        g_ref[...] = g
        u_ref[...] = u
        a_ref[...] = (jax.nn.silu(g) * u).astype(BF16)

    out = pl.BlockSpec((FF_BM, FF_BN), lambda i, j: (i, j))
    f32, b16 = _sds((SEQ, D_FF), F32), _sds((SEQ, D_FF), BF16)
    return _pcall(
        body, "mm_gate_up", (SEQ // FF_BM, FF_NB),
        [pl.BlockSpec((FF_BM, D_MODEL), lambda i, j: (i, 0)),
         pl.BlockSpec((D_MODEL, FF_BN), lambda i, j: (0, j)),
         pl.BlockSpec((D_MODEL, FF_BN), lambda i, j: (0, FF_NB + j))],
        [out, out, out], [f32, f32, b16], [h2, w_gu, w_gu], (), ("parallel", "parallel"), comm)


def _dact_swiglu(dx3b, w_down, g, u, comm=None):
    def body(dx_ref, w_ref, g_ref, u_ref, dg_ref, du_ref):
        da = lax.dot_general(dx_ref[...], w_ref[...], (NT, ((), ())), preferred_element_type=F32)
        g = g_ref[...]
        sg = jax.nn.sigmoid(g)
        dg_ref[...] = (da * u_ref[...] * (sg * (1.0 + g * (1.0 - sg)))).astype(BF16)
        du_ref[...] = (da * (g * sg)).astype(BF16)

    blk = pl.BlockSpec((FF_BM, FF_BN), lambda i, j: (i, j))
    b16 = _sds((SEQ, D_FF), BF16)
    return _pcall(
        body, "mm_dact", (SEQ // FF_BM, FF_NB),
        [pl.BlockSpec((FF_BM, D_MODEL), lambda i, j: (i, 0)), pl.BlockSpec((FF_BN, D_MODEL), lambda i, j: (j, 0)),
         blk, blk],
        [blk, blk], [b16, b16], [dx3b, w_down, g, u], (), ("parallel", "parallel"), comm)


def _mm_dh2(dg, du, w_gu, comm=None):
    bm = 1024
    nk = 2 * FF_NB

    def body(dg_ref, du_ref, w_ref, o_ref, acc_ref):
        k = pl.program_id(1)

        def part(a_ref):
            return lax.dot_general(a_ref[...], w_ref[...], (NT, ((), ())), preferred_element_type=F32)

        @pl.when(k == 0)
        def _():
            acc_ref[...] = part(dg_ref)

        @pl.when((k > 0) & (k < FF_NB))
        def _():
            acc_ref[...] += part(dg_ref)

        @pl.when(k >= FF_NB)
        def _():
            acc_ref[...] += part(du_ref)

        @pl.when(k == nk - 1)
        def _():
            o_ref[...] = acc_ref[...]

    return _pcall(
        body, "mm_dh2", (SEQ // bm, nk),
        [pl.BlockSpec((bm, FF_BN), lambda i, k: (i, jnp.minimum(k, FF_NB - 1))),
         pl.BlockSpec((bm, FF_BN), lambda i, k: (i, jnp.maximum(k - FF_NB, 0))),
         pl.BlockSpec((D_MODEL, FF_BN), lambda i, k: (0, k))],
        pl.BlockSpec((bm, D_MODEL), lambda i, k: (i, 0)), _sds((SEQ, D_MODEL), F32),
        [dg, du, w_gu], [pltpu.VMEM((bm, D_MODEL), F32)], ("parallel", "arbitrary"), comm)


def _mm_dw_gate_up(h2, dg, du):
    def body(h_ref, dg_ref, du_ref, o_ref):
        j = pl.program_id(0)

        def part(b_ref):
            return lax.dot_general(h_ref[...], b_ref[...], (TN, ((), ())), preferred_element_type=F32).astype(BF16)

        @pl.when(j < FF_NB)
        def _():
            o_ref[...] = part(dg_ref)

        @pl.when(j >= FF_NB)
        def _():
            o_ref[...] = part(du_ref)

    return pl.pallas_call(
        body, name="mm_dw_gate_up", grid=(2 * FF_NB,),
        in_specs=[_full_spec((SEQ, D_MODEL)),
                  pl.BlockSpec((SEQ, FF_BN), lambda j: (0, jnp.minimum(j, FF_NB - 1))),
                  pl.BlockSpec((SEQ, FF_BN), lambda j: (0, jnp.maximum(j - FF_NB, 0)))],
        out_specs=pl.BlockSpec((D_MODEL, FF_BN), lambda j: (0, j)),
        out_shape=_sds((D_MODEL, 2 * D_FF), BF16),
        compiler_params=_params(("arbitrary",)),
    )(h2, dg, du)


def _loss_head(x3, g, target):
    def body(x_ref, g_ref, t_ref, dx_ref, dxb_ref, dg_ref, loss_ref):
        i = pl.program_id(0)
        xf = x_ref[...]
        r = lax.rsqrt(jnp.mean(xf * xf, axis=-1, keepdims=True) + EPS)
        xn = xf * r
        gg = g_ref[...]
        err = xn * gg - t_ref[...]
        part = 0.5 * jnp.sum(jnp.mean(err * err, axis=-1, keepdims=True), axis=0, keepdims=True)
        dy = err * (1.0 / D_MODEL)
        dxn = dy * gg
        dx = r * (dxn - xn * jnp.mean(dxn * xn, axis=-1, keepdims=True))
        dx_ref[...] = dx
        dxb_ref[...] = dx.astype(BF16)
        dg = jnp.sum(dy * xn, axis=0, keepdims=True)
        lane0 = lax.broadcasted_iota(jnp.int32, (1, LANES), 1) == 0
        lpart = jnp.where(lane0, part, 0.0)

        @pl.when(i == 0)
        def _():
            dg_ref[...] = dg
            loss_ref[...] = lpart

        @pl.when(i > 0)
        def _():
            dg_ref[...] += dg
            loss_ref[...] += lpart

    return pl.pallas_call(
        body, name="loss_head", grid=(SEQ // ROWS,),
        in_specs=[_row_spec(D_MODEL), _full_spec((1, D_MODEL)), _row_spec(D_MODEL)],
        out_specs=[_row_spec(D_MODEL), _row_spec(D_MODEL), _full_spec((1, D_MODEL)), _full_spec((1, LANES))],
        out_shape=[_sds((SEQ, D_MODEL), F32), _sds((SEQ, D_MODEL), BF16),
                   _sds((1, D_MODEL), F32), _sds((1, LANES), F32)],
        compiler_params=_params(("arbitrary",)),
    )(x3, g, target)


def _rms_norm_bwd(name, dh, x, g, dres, with_bf16, comm=None):
    def body(dh_ref, x_ref, g_ref, dr_ref, *outs):
        i = pl.program_id(0)
        dx_ref = outs[0]
        dg_ref = outs[-1]
        xf = x_ref[...]
        r = lax.rsqrt(jnp.mean(xf * xf, axis=-1, keepdims=True) + EPS)
        xn = xf * r
        dh = dh_ref[...]
        dxn = dh * g_ref[...]
        dx = dr_ref[...] + r * (dxn - xn * jnp.mean(dxn * xn, axis=-1, keepdims=True))
        dx_ref[...] = dx
        if with_bf16:
            outs[1][...] = dx.astype(BF16)
        dg = jnp.sum(dh * xn, axis=0, keepdims=True)

        @pl.when(i == 0)
        def _():
            dg_ref[...] = dg

        @pl.when(i > 0)
        def _():
            dg_ref[...] += dg

    row = _row_spec(D_MODEL)
    out_specs = [row] + ([row] if with_bf16 else []) + [_full_spec((1, D_MODEL))]
    out_shape = ([_sds((SEQ, D_MODEL), F32)] + ([_sds((SEQ, D_MODEL), BF16)] if with_bf16 else [])
                 + [_sds((1, D_MODEL), F32)])
    return _pcall(body, name, (SEQ // ROWS,), [row, row, _full_spec((1, D_MODEL)), row], out_specs, out_shape,
                  [dh, x, g, dres], (), ("arbitrary",), comm)


def _merge_bwd(dx2b, w_o, conv_out, attn_out, proj):
    bm, bn = 1024, D_MODEL // 2

    def body(dx_ref, w_ref, co_ref, ao_ref, gc_ref, ga_ref, dco_ref, dao_ref, dgc_ref, dga_ref):
        dm = lax.dot_general(dx_ref[...], w_ref[...], (NT, ((), ())), preferred_element_type=F32)
        sc = jax.nn.sigmoid(gc_ref[...])
        sa = jax.nn.sigmoid(ga_ref[...])
        dco_ref[...] = (dm * sc).astype(BF16)
        dao_ref[...] = (dm * sa).astype(BF16)
        dgc_ref[...] = (dm * co_ref[...] * (sc * (1.0 - sc))).astype(BF16)
        dga_ref[...] = (dm * ao_ref[...] * (sa * (1.0 - sa))).astype(BF16)

    own = pl.BlockSpec((bm, bn), lambda i, j: (i, j))
    sd = _sds((SEQ, D_MODEL), BF16)
    return pl.pallas_call(
        body, name="mm_dmerged", grid=(SEQ // bm, D_MODEL // bn),
        in_specs=[pl.BlockSpec((bm, D_MODEL), lambda i, j: (i, 0)), pl.BlockSpec((bn, D_MODEL), lambda i, j: (j, 0)),
                  own, own,
                  pl.BlockSpec((bm, bn), lambda i, j: (i, COL_GC // bn + j)),
                  pl.BlockSpec((bm, bn), lambda i, j: (i, COL_GA // bn + j))],
        out_specs=[own, own, own, own], out_shape=[sd, sd, sd, sd],
        compiler_params=_params(("parallel", "parallel")),
    )(dx2b, w_o, conv_out, attn_out, proj, proj)


def _conv_bwd(dconv_y, proj, conv_w, comm=None):
    nblk = D_MODEL // CONV_COLS

    def body(dy_ref, cb_ref, cc_ref, cx_ref, w_ref, dcb_ref, dcc_ref, dcx_ref, dw_ref):
        cc = cc_ref[...]
        cx = cx_ref[...]
        u = cc * cx
        w = w_ref[...]
        u1 = _shift_rows(u, 1)
        u2 = _shift_rows(u, 2)
        cv = w[0:1, :] * u2 + w[1:2, :] * u1 + w[2:3, :] * u
        dy = dy_ref[...]
        dcb_ref[...] = (dy * cv).astype(BF16)
        dcv = dy * cb_ref[...]
        rows = lax.broadcasted_iota(jnp.int32, dcv.shape, 0)
        up1 = jnp.where(rows < SEQ - 1, pltpu.roll(dcv, SEQ - 1, axis=0), 0.0)
        up2 = jnp.where(rows < SEQ - 2, pltpu.roll(dcv, SEQ - 2, axis=0), 0.0)
        du = w[2:3, :] * dcv + w[1:2, :] * up1 + w[0:1, :] * up2
        dcc_ref[...] = (du * cx).astype(BF16)
        dcx_ref[...] = (du * cc).astype(BF16)
        dw_ref[...] = jnp.concatenate(
            [jnp.sum(dcv * u2, axis=0, keepdims=True),
             jnp.sum(dcv * u1, axis=0, keepdims=True),
             jnp.sum(dcv * u, axis=0, keepdims=True)], axis=0)

    def col(part):
        return pl.BlockSpec((SEQ, CONV_COLS), lambda j: (0, part * nblk + j))

    own = pl.BlockSpec((SEQ, CONV_COLS), lambda j: (0, j))
    wsp = pl.BlockSpec((3, CONV_COLS), lambda j: (0, j))
    sd = _sds((SEQ, D_MODEL), BF16)
    return _pcall(
        body, "conv_bwd", (nblk,), [own, col(0), col(1), col(2), wsp], [own, own, own, wsp],
        [sd, sd, sd, _sds((3, D_MODEL), F32)], [dconv_y, proj, proj, proj, conv_w], (), ("parallel",), comm)


def _attn_bwd(proj, dattn, sinks, tables, comm=None):
    def body(sink_ref, q_ref, kp_ref, kc_ref, vp_ref, vc_ref, c_ref, su_ref, sd_ref, cp_ref, sup_ref, sdp_ref,
             do_ref, dq_ref, dkp_ref, dkc_ref, dvp_ref, dvc_ref, ds_ref):
        n = pl.program_id(0)
        mask = _band_mask(n)
        tabs_cur = (c_ref, su_ref, sd_ref)
        k, v = _band_kv(kp_ref, kc_ref, vp_ref, vc_ref, tabs_cur, (cp_ref, sup_ref, sdp_ref))
        lane = lax.broadcasted_iota(jnp.int32, (1, LANES), 1)
        dsink = jnp.zeros((1, LANES), F32)
        c, su, sd = c_ref[:, :LANES], su_ref[:, :LANES], sd_ref[:, :LANES]
        dk_tiles = [None] * (N_KV_HEADS // HEADS_PER_TILE)
        dv_tiles = [None] * (N_KV_HEADS // HEADS_PER_TILE)
        for h in range(N_KV_HEADS):
            k_halves = _head_tiles(k, h)
            v_halves = _head_tiles(v, h)
            dk_par = [None] * HEADS_PER_TILE
            dv_par = [None] * HEADS_PER_TILE
            for t in range(GROUP // HEADS_PER_TILE):
                tile = h * (GROUP // HEADS_PER_TILE) + t
                q_tile = _query_tile(q_ref, tile, tabs_cur)
                do_tile = do_ref[:, tile * LANES:(tile + 1) * LANES]
                dq_tile = None
                for par in range(HEADS_PER_TILE):
                    head = tile * HEADS_PER_TILE + par
                    p, p_sink = _head_softmax(q_tile, k_halves[par], sink_ref[0, head], mask)
                    dp = lax.dot_general(do_tile, v_halves[par], (NT, ((), ())), preferred_element_type=F32)
                    delta = jnp.sum(p * dp, axis=-1, keepdims=True)
                    ds = (p * (dp - delta) * ATTN_SCALE).astype(BF16)
                    dq = jnp.dot(ds, k_halves[par], preferred_element_type=F32)
                    dq_tile = dq if dq_tile is None else dq_tile + dq
                    dk = lax.dot_general(ds, q_tile, (TN, ((), ())), preferred_element_type=F32)
                    dv = lax.dot_general(p.astype(BF16), do_tile, (TN, ((), ())), preferred_element_type=F32)
                    dk_par[par] = dk if dk_par[par] is None else dk_par[par] + dk
                    dv_par[par] = dv if dv_par[par] is None else dv_par[par] + dv
                    val = -jnp.sum(p_sink * delta, axis=0, keepdims=True)
                    dsink = dsink + jnp.where(lane == head, val, 0.0)
                dq_ref[:, tile * LANES:(tile + 1) * LANES] = _rotate(dq_tile, c, -su, -sd).astype(BF16)
            own = h % HEADS_PER_TILE
            for par_grads, tiles in ((dk_par, dk_tiles), (dv_par, dv_tiles)):
                shifted = pltpu.roll(par_grads[1 - own], HEAD_DIM, axis=1)
                total = jnp.where(_lane_half(shifted.shape, own), par_grads[own] + shifted, 0.0)
                i = h // HEADS_PER_TILE
                tiles[i] = total if tiles[i] is None else tiles[i] + total
        for i in range(N_KV_HEADS // HEADS_PER_TILE):
            cols = slice(i * LANES, (i + 1) * LANES)
            dkp_ref[:, cols] = dk_tiles[i][:BLOCK, :]
            dkc_ref[:, cols] = dk_tiles[i][BLOCK:, :]
            dvp_ref[:, cols] = dv_tiles[i][:BLOCK, :]
            dvc_ref[:, cols] = dv_tiles[i][BLOCK:, :]

        @pl.when(n == 0)
        def _():
            ds_ref[...] = dsink

        @pl.when(n > 0)
        def _():
            ds_ref[...] += dsink

    blk = pl.BlockSpec((BLOCK, D_KV), lambda n: (n, 0))
    kv = _sds((SEQ, D_KV), F32)
    return _pcall(
        body, "attn_bwd", (N_QBLK,),
        [pl.BlockSpec(memory_space=pltpu.SMEM)] + _attn_specs() + [pl.BlockSpec((BLOCK, D_ATTN), lambda n: (n, 0))],
        [pl.BlockSpec((BLOCK, D_ATTN), lambda n: (n, 0)), blk, blk, blk, blk, _full_spec((1, LANES))],
        [_sds((SEQ, D_ATTN), BF16), kv, kv, kv, kv, _sds((1, LANES), F32)],
        [sinks] + [proj] * 5 + list(tables) * 2 + [dattn], (), ("arbitrary",), comm)


def _kv_grad_combine(dk_prev, dk_cur, dv_prev, dv_cur, tables):
    def body(kp_ref, kc_ref, vp_ref, vc_ref, c_ref, su_ref, sd_ref, o_ref):
        m = pl.program_id(0)
        has_next = m < N_QBLK - 1
        dk = kc_ref[...] + jnp.where(has_next, kp_ref[...], 0.0)
        dv = vc_ref[...] + jnp.where(has_next, vp_ref[...], 0.0)
        o_ref[:, :D_KV] = _rotate(dk, c_ref[...], -su_ref[...], -sd_ref[...]).astype(BF16)
        o_ref[:, D_KV:] = dv.astype(BF16)

    cur = pl.BlockSpec((BLOCK, D_KV), lambda m: (m, 0))
    nxt = pl.BlockSpec((BLOCK, D_KV), lambda m: (jnp.minimum(m + 1, N_QBLK - 1), 0))
    return pl.pallas_call(
        body, name="kv_grad_combine", grid=(N_QBLK,),
        in_specs=[nxt, cur, nxt, cur, cur, cur, cur],
        out_specs=pl.BlockSpec((BLOCK, 2 * D_KV), lambda m: (m, 0)),
        out_shape=_sds((SEQ, 2 * D_KV), BF16),
        compiler_params=_params(("parallel",)),
    )(dk_prev, dk_cur, dv_prev, dv_cur, *tables)


MATRICES = {
    "w_in": (D_MODEL, N_IN // N_CHIPS, "col"),
    "w_conv_out": (D_MODEL // N_CHIPS, D_MODEL, "row"),
    "w_attn_out": (D_MODEL // N_CHIPS, D_MODEL, "row"),
    "w_o": (D_MODEL // N_CHIPS, D_MODEL, "row"),
    "w_gate_up": (D_MODEL, 2 * D_FF // N_CHIPS, "col"),
    "w_down": (D_FF // N_CHIPS, D_MODEL, "row"),
}
BF16_ROW_TILE = 16
CONV_W_COLS = D_MODEL // N_CHIPS
SMALL_ROWS = 8


def _whole_shape(spec):
    rows, cols, kind = spec
    return (rows, cols * N_CHIPS) if kind == "col" else (rows * N_CHIPS, cols)


def _half_shape(spec):
    return (spec[0] // 2, spec[1])


def _aligned(start, multiple):
    return start if isinstance(start, int) else pl.multiple_of(start, multiple)


def _region(ref, spec, shard, half, part=0, parts=1):
    rows, cols, kind = spec
    hr = rows // 2
    n = hr // parts
    if kind == "col":
        return ref.at[pl.ds(_aligned(half * hr + part * n, BF16_ROW_TILE), n),
                      pl.ds(_aligned(shard * cols, LANES), cols)]
    return ref.at[pl.ds(_aligned(shard * rows + half * hr + part * n, BF16_ROW_TILE), n), :]


def _position():
    x, y, c = lax.axis_index("x"), lax.axis_index("y"), lax.axis_index("c")
    chips = [(1 - x, y), (x, 1 - y), (1 - x, 1 - y)]
    return x, y, c, chips


def _shard_of(chip):
    return 2 * chip[0] + chip[1]


def _remote(src, dst, send_sem, recv_sem, to):
    return pltpu.make_async_remote_copy(src_ref=src, dst_ref=dst, send_sem=send_sem, recv_sem=recv_sem,
                                        device_id=to, device_id_type=MESH)


def _to_bf16_in_whole(name, w, spec, shard, rows):
    steps = spec[0] // rows

    def body(s_ref, w_ref, o_ref):
        del s_ref
        o_ref[...] = w_ref[...].astype(BF16)

    if spec[2] == "col":
        out_spec = pl.BlockSpec((rows, spec[1]), lambda i, s_ref: (i, s_ref[0]))
    else:
        out_spec = pl.BlockSpec((rows, spec[1]), lambda i, s_ref: (s_ref[0] * steps + i, 0))
    grid_spec = pltpu.PrefetchScalarGridSpec(
        num_scalar_prefetch=1, grid=(steps,),
        in_specs=[pl.BlockSpec((rows, spec[1]), lambda i, s_ref: (i, 0))], out_specs=out_spec)
    return pl.pallas_call(
        body, name=name, grid_spec=grid_spec, out_shape=_sds(_whole_shape(spec), BF16),
        compiler_params=_params(("parallel",)),
    )(shard, w)


class _Gather:
    def __init__(self, wholes, pieces, conv_w=None):
        self.pieces = pieces
        self.n = len(wholes)
        self.with_conv_w = conv_w is not None
        self.operands = list(wholes) + ([conv_w] if self.with_conv_w else [])
        self.out_shape = [_sds(w.shape, w.dtype) for w in wholes]
        if self.with_conv_w:
            self.out_shape.append(_sds((3, D_MODEL), F32))
        self.aliases = {i: i for i in range(self.n)}
        n_ici = 3 * len(pieces)
        self.sems = [pltpu.SemaphoreType.DMA((n_ici,))] * 4
        if self.with_conv_w:
            self.sems += [pltpu.SemaphoreType.DMA((1,)), pltpu.SemaphoreType.DMA((3,)), pltpu.SemaphoreType.DMA((3,))]

    def _conv_w(self, cins, couts, sems, with_recvs):
        cw_in, cw_out = cins[self.n], couts[self.n]
        x, y, c, chips = _position()

        def cols(shard):
            return cw_out.at[:, pl.ds(_aligned(shard * CONV_W_COLS, LANES), CONV_W_COLS)]

        me = _shard_of((x, y))
        local = pltpu.make_async_copy(cw_in, cols(me), sems[4].at[0])
        sends = [_remote(cw_in, cols(me), sems[5].at[j], sems[6].at[j], (*chip, c)) for j, chip in enumerate(chips)]
        if not with_recvs:
            return local, sends, []
        recvs = [_remote(cols(_shard_of(chip)), cols(_shard_of(chip)), sems[5].at[j], sems[6].at[j], (*chip, c))
                 for j, chip in enumerate(chips)]
        return local, sends, recvs

    def start(self, cins, couts, sems):
        x, y, c, chips = _position()
        me = _shard_of((x, y))
        if self.with_conv_w:
            local, sends, _ = self._conv_w(cins, couts, sems, False)
            local.start()
            for cp in sends:
                cp.start()
        for p, (i, spec, part, parts) in enumerate(self.pieces):
            mine = _region(couts[i], spec, me, c, part, parts)
            for j, chip in enumerate(chips):
                _remote(mine, mine, sems[0].at[3 * p + j], sems[1].at[3 * p + j], (*chip, c)).start()

    def finish(self, cins, couts, sems):
        x, y, c, chips = _position()
        me = _shard_of((x, y))
        sibling = (x, y, 1 - c)
        send_a, recv_a, send_b, recv_b = sems[:4]
        passed = []
        for p, (i, spec, part, parts) in enumerate(self.pieces):
            for j, chip in enumerate(chips):
                k = 3 * p + j
                landed = _region(couts[i], spec, _shard_of(chip), c, part, parts)
                _remote(landed, landed, send_a.at[k], recv_a.at[k], (*chip, c)).wait_recv()
                cp = _remote(landed, landed, send_b.at[k], recv_b.at[k], sibling)
                cp.start()
                passed.append(cp)
        for p, (i, spec, part, parts) in enumerate(self.pieces):
            mine = _region(couts[i], spec, me, c, part, parts)
            for j, chip in enumerate(chips):
                k = 3 * p + j
                other = _region(couts[i], spec, _shard_of(chip), 1 - c, part, parts)
                _remote(other, other, send_b.at[k], recv_b.at[k], sibling).wait_recv()
                _remote(mine, mine, send_a.at[k], recv_a.at[k], (*chip, c)).wait_send()
        for cp in passed:
            cp.wait_send()
        if self.with_conv_w:
            local, sends, recvs = self._conv_w(cins, couts, sems, True)
            for cp in recvs:
                cp.wait_recv()
            for cp in sends:
                cp.wait_send()
            local.wait()


def _mm_in_gather(h1, w_whole, comm):
    spec = MATRICES["w_in"]
    cols = spec[1]
    bm = SEQ // 2

    def body(h_ref, w_in_ref, proj_ref, w_ref, wbuf, obuf, send_a, recv_a, send_b, recv_b, load_sem, store_sems):
        del w_in_ref
        s, mi = pl.program_id(0), pl.program_id(1)
        x, y, c, chips = _position()
        me = _shard_of((x, y))
        sibling = (x, y, 1 - c)
        mine = _region(w_ref, spec, me, c)

        @pl.when((s == 0) & (mi == 0))
        def _():
            for j, chip in enumerate(chips):
                _remote(mine, mine, send_a.at[j], recv_a.at[j], (*chip, c)).start()

        shard = me
        for j, chip in enumerate(chips):
            shard = jnp.where(s == j + 1, _shard_of(chip), shard)

            @pl.when((s == j + 1) & (mi == 0))
            def _():
                landed = _region(w_ref, spec, _shard_of(chip), c)
                _remote(landed, landed, send_a.at[j], recv_a.at[j], (*chip, c)).wait_recv()
                _remote(landed, landed, send_b.at[j], recv_b.at[j], sibling).start()
                other = _region(w_ref, spec, _shard_of(chip), 1 - c)
                _remote(other, other, send_b.at[j], recv_b.at[j], sibling).wait_recv()

        col0 = pl.multiple_of(shard * cols, LANES)

        @pl.when(mi == 0)
        def _():
            load = pltpu.make_async_copy(w_ref.at[:, pl.ds(col0, cols)], wbuf, load_sem.at[0])
            load.start()
            load.wait()

        def store():
            rows = pl.ds(pl.multiple_of(mi * bm, bm), bm)
            return pltpu.make_async_copy(obuf.at[mi], proj_ref.at[rows, pl.ds(col0, cols)], store_sems.at[mi])

        @pl.when(s > 0)
        def _():
            store().wait()

        obuf[mi] = jnp.dot(h_ref[...], wbuf[...], preferred_element_type=F32)
        store().start()

        @pl.when(s == N_CHIPS - 1)
        def _():
            store().wait()

        @pl.when((s == N_CHIPS - 1) & (mi == 1))
        def _():
            for j, chip in enumerate(chips):
                landed = _region(w_ref, spec, _shard_of(chip), c)
                _remote(mine, mine, send_a.at[j], recv_a.at[j], (*chip, c)).wait_send()
                _remote(landed, landed, send_b.at[j], recv_b.at[j], sibling).wait_send()

    sem3 = pltpu.SemaphoreType.DMA((3,))
    (proj, whole), extra = _pcall(
        body, "mm_in", (N_CHIPS, SEQ // bm),
        [pl.BlockSpec((bm, D_MODEL), lambda s, m: (m, 0)), HBM_SPEC], [HBM_SPEC, HBM_SPEC],
        [_sds((SEQ, N_IN), F32), _sds(w_whole.shape, w_whole.dtype)], [h1, w_whole],
        [pltpu.VMEM((D_MODEL, cols), BF16), pltpu.VMEM((SEQ // bm, bm, cols), F32), sem3, sem3, sem3, sem3,
         pltpu.SemaphoreType.DMA((1,)), pltpu.SemaphoreType.DMA((SEQ // bm,))],
        None, comm, aliases={1: 1})
    return proj, whole, extra


def _pack_small(dg_mix, dg_ffn, dg_final, dconv_w, dsinks, loss_row):
    def body(a_ref, b_ref, c_ref, w_ref, s_ref, l_ref, o_ref):
        pad = jnp.zeros((1, D_MODEL - LANES), F32)
        o_ref[0:1, :] = a_ref[...]
        o_ref[1:2, :] = b_ref[...]
        o_ref[2:3, :] = c_ref[...]
        o_ref[3:6, :] = w_ref[...]
        o_ref[6:7, :] = jnp.concatenate([s_ref[...], pad], axis=1)
        o_ref[7:8, :] = jnp.concatenate([l_ref[...], pad], axis=1)

    return pl.pallas_call(
        body, name="pack_small", out_shape=_sds((SMALL_ROWS, D_MODEL), F32),
        compiler_params=_params(),
    )(dg_mix, dg_ffn, dg_final, dconv_w, dsinks, loss_row)


class _Pair:
    def __init__(self, dws, specs):
        self.specs = specs
        self.operands = list(dws)
        self.out_shape = [_sds((N_CHIPS, *_half_shape(s)), BF16) for s in specs]
        self.aliases = {}
        n = N_CHIPS * len(specs)
        self.sems = [pltpu.SemaphoreType.DMA((n,)), pltpu.SemaphoreType.DMA((n,))]

    def _copies(self, cins, couts, sems):
        x, y, c, _ = _position()
        sibling = (x, y, 1 - c)
        for i, spec in enumerate(self.specs):
            for t in range(N_CHIPS):
                k = N_CHIPS * i + t
                yield _remote(_region(cins[i], spec, t, 1 - c), couts[i].at[t], sems[0].at[k], sems[1].at[k], sibling)

    def start(self, cins, couts, sems):
        for cp in self._copies(cins, couts, sems):
            cp.start()

    def finish(self, cins, couts, sems):
        for cp in self._copies(cins, couts, sems):
            cp.wait()


class _SmallAllToAll:
    def __init__(self, small):
        self.operands = [small]
        self.out_shape = [_sds((N_DEV, SMALL_ROWS, D_MODEL), F32)]
        self.aliases = {}
        self.sems = [pltpu.SemaphoreType.DMA((N_DEV - 1,)), pltpu.SemaphoreType.DMA((N_DEV - 1,)),
                     pltpu.SemaphoreType.DMA((1,))]

    def _copies(self, cins, couts, sems):
        x, y, c, _ = _position()
        me = 4 * x + 2 * y + c
        out = []
        for r in range(1, N_DEV):
            flip = ((r >> 2) & 1, (r >> 1) & 1, r & 1)
            peer = tuple(1 - p if f else p for p, f in zip((x, y, c), flip))
            theirs = couts[0].at[4 * peer[0] + 2 * peer[1] + peer[2]]
            out.append((_remote(cins[0], couts[0].at[me], sems[0].at[r - 1], sems[1].at[r - 1], peer),
                        _remote(theirs, theirs, sems[0].at[r - 1], sems[1].at[r - 1], peer)))
        return pltpu.make_async_copy(cins[0], couts[0].at[me], sems[2].at[0]), out

    def start(self, cins, couts, sems):
        own, copies = self._copies(cins, couts, sems)
        own.start()
        for send, _ in copies:
            send.start()

    def finish(self, cins, couts, sems):
        own, copies = self._copies(cins, couts, sems)
        for send, recv in copies:
            recv.wait_recv()
            send.wait_send()
        own.wait()


class _Both:
    def __init__(self, a, b):
        self.a, self.b = a, b
        self.operands = list(a.operands) + list(b.operands)
        self.out_shape = list(a.out_shape) + list(b.out_shape)
        self.aliases = dict(a.aliases)
        self.aliases.update({len(a.operands) + k: len(a.out_shape) + v for k, v in b.aliases.items()})
        self.sems = list(a.sems) + list(b.sems)

    def _split(self, cins, couts, sems):
        na, ma, sa = len(self.a.operands), len(self.a.out_shape), len(self.a.sems)
        return (cins[:na], couts[:ma], sems[:sa]), (cins[na:], couts[ma:], sems[sa:])

    def start(self, cins, couts, sems):
        for plan, args in zip((self.a, self.b), self._split(cins, couts, sems)):
            plan.start(*args)

    def finish(self, cins, couts, sems):
        for plan, args in zip((self.a, self.b), self._split(cins, couts, sems)):
            plan.finish(*args)


def _pair_sum(name, specs, dws, got, place):
    n_mat = len(specs)

    def body(p_ref, *refs):
        t = pl.program_id(0)
        mine, theirs = refs[:n_mat], refs[n_mat:2 * n_mat]
        outs, owns = refs[2 * n_mat:3 * n_mat], refs[3 * n_mat:]
        for a, b, o, own in zip(mine, theirs, outs, owns):
            s = (a[...].astype(F32) + b[...].astype(F32)).astype(BF16)
            o[...] = s

            @pl.when(t == p_ref[1])
            def _():
                own[...] = s

    def mine_spec(spec):
        hr, cols = _half_shape(spec)
        if spec[2] == "col":
            return pl.BlockSpec((hr, cols), lambda t, p_ref: (p_ref[0], t))
        return pl.BlockSpec((hr, cols), lambda t, p_ref: (2 * t + p_ref[0], 0))

    def slot_spec(spec):
        return pl.BlockSpec((None, *_half_shape(spec)), lambda t, p_ref: (t, 0, 0))

    def own_spec(spec):
        return pl.BlockSpec((None, *_half_shape(spec)), lambda t, p_ref: (p_ref[1], 0, 0))

    slots = [_sds((N_CHIPS, *_half_shape(s)), BF16) for s in specs]
    grid_spec = pltpu.PrefetchScalarGridSpec(
        num_scalar_prefetch=1, grid=(N_CHIPS,),
        in_specs=[mine_spec(s) for s in specs] + [slot_spec(s) for s in specs],
        out_specs=[slot_spec(s) for s in specs] + [own_spec(s) for s in specs])
    res = pl.pallas_call(
        body, name=name, grid_spec=grid_spec, out_shape=slots + slots,
        compiler_params=_params(("arbitrary",)),
    )(place, *dws, *got)
    return list(res[:n_mat]), list(res[n_mat:])


class _ChipExchange:
    def __init__(self, sums, slots, part=0, parts=1):
        self.n = len(sums)
        self.part, self.parts = part, parts
        self.operands = list(sums) + list(slots)
        self.out_shape = [_sds(s.shape, s.dtype) for s in slots]
        self.aliases = {self.n + i: i for i in range(self.n)}
        self.sems = [pltpu.SemaphoreType.DMA((3 * self.n,)), pltpu.SemaphoreType.DMA((3 * self.n,))]

    def _rows(self, ref, slot):
        n = ref.shape[1] // self.parts
        return ref.at[slot, pl.ds(self.part * n, n), :]

    def _copies(self, cins, couts, sems):
        x, y, c, chips = _position()
        me = _shard_of((x, y))
        for i in range(self.n):
            for j, chip in enumerate(chips):
                k = 3 * i + j
                theirs = self._rows(couts[i], _shard_of(chip))
                yield (_remote(self._rows(cins[i], _shard_of(chip)), self._rows(couts[i], me),
                               sems[0].at[k], sems[1].at[k], (*chip, c)),
                       _remote(theirs, theirs, sems[0].at[k], sems[1].at[k], (*chip, c)))

    def start(self, cins, couts, sems):
        for send, _ in self._copies(cins, couts, sems):
            send.start()

    def finish(self, cins, couts, sems):
        for send, recv in self._copies(cins, couts, sems):
            recv.wait_recv()
            send.wait_send()


def _chip_sum(name, specs, slots, core):
    steps = 2
    n_mat = len(specs)

    def body(c_ref, *refs):
        del c_ref
        ins, outs = refs[:n_mat], refs[n_mat:]
        for a, o in zip(ins, outs):
            acc = a[0].astype(F32)
            for t in range(1, N_CHIPS):
                acc = acc + a[t].astype(F32)
            o[...] = acc

    def in_spec(spec):
        hr, cols = _half_shape(spec)
        return pl.BlockSpec((N_CHIPS, hr // steps, cols), lambda i, c_ref: (0, i, 0))

    def out_spec(spec):
        hr, cols = _half_shape(spec)
        return pl.BlockSpec((hr // steps, cols), lambda i, c_ref: (c_ref[0] * steps + i, 0))

    grid_spec = pltpu.PrefetchScalarGridSpec(
        num_scalar_prefetch=1, grid=(steps,),
        in_specs=[in_spec(s) for s in specs], out_specs=[out_spec(s) for s in specs])
    return list(pl.pallas_call(
        body, name=name, grid_spec=grid_spec,
        out_shape=[_sds((s[0], s[1]), F32) for s in specs],
        compiler_params=_params(("parallel",)),
    )(core, *slots))


class _HalfExchange:
    def __init__(self, grads, specs):
        self.specs = specs
        self.operands = list(grads)
        self.out_shape = [_sds(g.shape, g.dtype) for g in grads]
        self.aliases = {i: i for i in range(len(grads))}
        self.sems = [pltpu.SemaphoreType.DMA((len(grads),)), pltpu.SemaphoreType.DMA((len(grads),))]

    def _copies(self, couts, sems):
        x, y, c, _ = _position()
        sibling = (x, y, 1 - c)
        for i, spec in enumerate(self.specs):
            hr = spec[0] // 2
            mine = couts[i].at[pl.ds(_aligned(c * hr, 8), hr), :]
            theirs = couts[i].at[pl.ds(_aligned((1 - c) * hr, 8), hr), :]
            yield (_remote(mine, mine, sems[0].at[i], sems[1].at[i], sibling),
                   _remote(theirs, theirs, sems[0].at[i], sems[1].at[i], sibling))

    def start(self, cins, couts, sems):
        for send, _ in self._copies(couts, sems):
            send.start()

    def finish(self, cins, couts, sems):
        for send, recv in self._copies(couts, sems):
            recv.wait_recv()
            send.wait_send()


def _small_sum(blocks):
    def body(b_ref, o_ref):
        acc = b_ref[0]
        for d in range(1, N_DEV):
            acc = acc + b_ref[d]
        o_ref[...] = acc

    return pl.pallas_call(
        body, name="small_sum", out_shape=_sds((SMALL_ROWS, D_MODEL), F32), compiler_params=_params(),
    )(blocks)


def _adamw(name, params, steps, comm=None):
    n = len(params)

    def body(*refs):
        for p in range(n):
            w_ref, g_ref, m_ref, v_ref = refs[4 * p:4 * p + 4]
            d_ref, nm_ref, nv_ref = refs[4 * n + 3 * p:4 * n + 3 * p + 3]
            g = g_ref[...]
            m = ADAM_B1 * m_ref[...] + (1.0 - ADAM_B1) * g
            v = ADAM_B2 * v_ref[...] + (1.0 - ADAM_B2) * jnp.square(g)
            m_hat = m / (1.0 - ADAM_B1 ** ADAM_STEP)
            v_hat = v / (1.0 - ADAM_B2 ** ADAM_STEP)
            d_ref[...] = -ADAM_LR * (m_hat / (jnp.sqrt(v_hat) + ADAM_EPS) + ADAM_WD * w_ref[...])
            nm_ref[...] = m
            nv_ref[...] = v

    in_specs, out_specs, out_shape, operands = [], [], [], []
    for w, g, m, v in params:
        spec = pl.BlockSpec((w.shape[0] // steps, w.shape[1]), lambda i: (i, 0))
        in_specs += [spec] * 4
        out_specs += [spec] * 3
        out_shape += [_sds(w.shape, F32)] * 3
        operands += [w, g, m, v]
    res = _pcall(body, name, (steps,), in_specs, out_specs, out_shape, operands, (), ("parallel",), comm)
    outs, extra = res if comm is not None else (res, None)
    triples = [tuple(outs[3 * p:3 * p + 3]) for p in range(n)]
    return triples if comm is None else (triples, extra)


MATRIX_NAMES = tuple(MATRICES)
WEIGHT_ORDER = ("g_mix", "w_in", "conv_w", "attn_sinks", "w_conv_out", "w_attn_out", "w_o", "g_ffn",
                "w_gate_up", "w_down", "g_final")


def kernel(x, g_mix, w_in, conv_w, attn_sinks, w_conv_out, w_attn_out, w_o, g_ffn, w_gate_up, w_down, g_final, loss_target, m_g_mix, m_w_in, m_conv_w, m_attn_sinks, m_w_conv_out, m_w_attn_out, m_w_o, m_g_ffn, m_w_gate_up, m_w_down, m_g_final, v_g_mix, v_w_in, v_conv_w, v_attn_sinks, v_w_conv_out, v_w_attn_out, v_w_o, v_g_ffn, v_w_gate_up, v_w_down, v_g_final):
    w = dict(g_mix=g_mix, w_in=w_in[0], conv_w=conv_w[0], attn_sinks=attn_sinks, w_conv_out=w_conv_out[0],
             w_attn_out=w_attn_out[0], w_o=w_o[0], g_ffn=g_ffn, w_gate_up=w_gate_up[0], w_down=w_down[0],
             g_final=g_final[None, :])
    m = dict(g_mix=m_g_mix, w_in=m_w_in[0], conv_w=m_conv_w[0], attn_sinks=m_attn_sinks,
             w_conv_out=m_w_conv_out[0], w_attn_out=m_w_attn_out[0], w_o=m_w_o[0], g_ffn=m_g_ffn,
             w_gate_up=m_w_gate_up[0], w_down=m_w_down[0], g_final=m_g_final[None, :])
    v = dict(g_mix=v_g_mix, w_in=v_w_in[0], conv_w=v_conv_w[0], attn_sinks=v_attn_sinks,
             w_conv_out=v_w_conv_out[0], w_attn_out=v_w_attn_out[0], w_o=v_w_o[0], g_ffn=v_g_ffn,
             w_gate_up=v_w_gate_up[0], w_down=v_w_down[0], g_final=v_g_final[None, :])
    shard = (2 * lax.axis_index("x") + lax.axis_index("y")).astype(jnp.int32)
    core = lax.axis_index("c").astype(jnp.int32)
    shard1, core1, place = shard.reshape((1,)), core.reshape((1,)), jnp.stack([core, shard])
    spec = MATRICES
    xs, target, sinks = x[0], loss_target[0], w["attn_sinks"]
    tables = _rope_tables()

    def gather(names, part=0, parts=1):
        return _Gather([whole[n] for n in names], [(i, spec[n], part, parts) for i, n in enumerate(names)])

    def pair(names):
        return _Pair([dw[n] for n in names], [spec[n] for n in names])

    def pair_sum(tag, names, got):
        return _pair_sum("pair_sum_" + tag, [spec[n] for n in names], [dw[n] for n in names], got, place)

    cast_rows = {"w_down": D_FF // N_CHIPS // 2}
    whole = {n: _to_bf16_in_whole("cast_" + n, w[n], spec[n], shard1, cast_rows.get(n, 256)) for n in MATRIX_NAMES}

    mixers = ("w_conv_out", "w_attn_out", "w_o")
    h1 = _rms_norm("norm_mix", xs, w["g_mix"])
    proj, whole["w_in"], (*got, conv_w_whole) = _mm_in_gather(
        h1, whole["w_in"], _Gather([whole[n] for n in mixers], [(i, spec[n], 0, 1) for i, n in enumerate(mixers)],
                                   conv_w=w["conv_w"]))
    whole.update(zip(mixers, got))
    conv_y = _conv_fwd(proj, conv_w_whole)
    attn, (whole["w_gate_up"],) = _attn_fwd(proj, tables, sinks, comm=gather(("w_gate_up",)))
    conv_out, attn_out, merged = _branch_merge(conv_y, attn, whole["w_conv_out"], whole["w_attn_out"], proj)
    x2 = _mm_nn("mm_o", merged, whole["w_o"], 1024, 1024, F32, res=xs)
    h2 = _rms_norm("norm_ffn", x2, w["g_ffn"])
    (gate, up, act), (whole["w_down"],) = _gate_up_fwd(h2, whole["w_gate_up"], comm=gather(("w_down",)))
    x3 = _mm_nn("mm_down", act, whole["w_down"], 1024, 512, F32, res=x2)
    dx3, dx3b, dg_final, loss_row = _loss_head(x3, w["g_final"], target)

    dw = {}
    dw["w_down"] = _mm_tn("mm_dw_down", act, dx3b, 1408, 1024, BF16)
    (dgate, dup), got = _dact_swiglu(dx3b, whole["w_down"], gate, up, comm=pair(("w_down",)))
    sums_a, own_a = pair_sum("down", ("w_down",), got)
    dh2, slots_a = _mm_dh2(dgate, dup, whole["w_gate_up"], comm=_ChipExchange(sums_a, own_a))
    dw["w_gate_up"] = _mm_dw_gate_up(h2, dgate, dup)
    (dx2, dx2b, dg_ffn), got = _rms_norm_bwd("norm_ffn_bwd", dh2, x2, w["g_ffn"], dx3, True, comm=pair(("w_gate_up",)))
    sums_b, own_b = pair_sum("gate_up", ("w_gate_up",), got)
    dw["w_o"] = _mm_tn("mm_dw_o", merged, dx2b, 1024, 1024, BF16)
    dco, dao, dgc, dga = _merge_bwd(dx2b, whole["w_o"], conv_out, attn_out, proj)
    dconv_y = _mm_nt("mm_dconv_y", dco, whole["w_conv_out"], 1024, 1024, D_MODEL, F32)
    dw["w_conv_out"] = _mm_tn("mm_dw_conv_out", conv_y, dco, 1024, 1024, BF16)
    dattn = _mm_nt("mm_dattn", dao, whole["w_attn_out"], 1024, 1024, D_MODEL, BF16)
    dw["w_attn_out"] = _mm_tn("mm_dw_attn_out", attn, dao, 1024, 1024, BF16)
    (dcb, dcc, dcx, dconv_w), got = _conv_bwd(dconv_y, proj, conv_w_whole, comm=pair(mixers))
    sums_c, own_c = pair_sum("mixers", mixers, got)
    (dq, dk_prev, dk_cur, dv_prev, dv_cur, dsinks), slots_b = _attn_bwd(
        proj, dattn, sinks, tables, comm=_ChipExchange(sums_b, own_b))
    dkv = _kv_grad_combine(dk_prev, dk_cur, dv_prev, dv_cur, tables)
    dproj = jnp.concatenate([dcb, dcc, dcx, dq, dkv, dgc, dga], axis=1)
    dw["w_in"], slots_c = _mm_tn("mm_dw_in", h1, dproj, 1024, 1664, BF16, comm=_ChipExchange(sums_c, own_c))
    sums_d, own_d = pair_sum("in", ("w_in",), _comm_call("pair_exchange_in", pair(("w_in",))))
    early = ("w_down", "w_gate_up") + mixers
    halves = _chip_sum("chip_sum_early", [spec[n] for n in early], slots_a + slots_b + slots_c, core1)
    dh1, (own_d, *reduced) = _mm_nt(
        "mm_dh1", dproj, whole["w_in"], 1024, 1024, 1664, F32,
        comm=_Both(_ChipExchange(sums_d, own_d, 0, 2), _HalfExchange(halves, [spec[n] for n in early])))
    g = dict(zip(early, reduced))
    (grad_x, dg_mix), slots_d = _rms_norm_bwd("norm_mix_bwd", dh1, xs, w["g_mix"], dx2, False,
                                              comm=_ChipExchange(sums_d, [own_d], 1, 2))
    small = _pack_small(dg_mix, dg_ffn, dg_final, dconv_w, dsinks, loss_row)
    half_in = _chip_sum("chip_sum_in", [spec["w_in"]], slots_d, core1)
    g["w_in"], small_blocks = _comm_call(
        "half_exchange_in", _Both(_HalfExchange(half_in, [spec["w_in"]]), _SmallAllToAll(small)))
    delta, new_m, new_v = {}, {}, {}

    def keep(names, triples):
        for n, (d, nm, nv) in zip(names, triples):
            delta[n], new_m[n], new_v[n] = d, nm, nv

    keep(early, _adamw("adamw_early", [(w[n], g[n], m[n], v[n]) for n in early], 8))
    small_sum = _small_sum(small_blocks)
    g["g_mix"] = small_sum[0:1, :]
    g["g_ffn"] = small_sum[1:2, :]
    g["g_final"] = small_sum[2:3, :]
    g["conv_w"] = lax.dynamic_slice(small_sum, (3, shard * CONV_W_COLS), (3, CONV_W_COLS))
    g["attn_sinks"] = small_sum[6:7, :N_HEADS]
    loss = small_sum[7, 0]
    keep(("w_in",), _adamw("adamw_w_in", [(w["w_in"], g["w_in"], m["w_in"], v["w_in"])], 4))
    rest = ("g_mix", "g_ffn", "g_final", "conv_w", "attn_sinks")
    keep(rest, _adamw("adamw_small", [(w[n], g[n], m[n], v[n]) for n in rest], 1))

    def shaped(vals):
        return [vals[n].reshape((D_MODEL,)) if n == "g_final" else
                (vals[n][None] if n in MATRIX_NAMES or n == "conv_w" else vals[n]) for n in WEIGHT_ORDER]

    return (loss, grad_x[None], *shaped(g), *shaped(delta), *shaped(new_m), *shaped(new_v))
```

```python
import functools
import math

import jax
import jax.numpy as jnp
from jax import lax
from jax.experimental import pallas as pl
from jax.experimental.pallas import tpu as pltpu

F32 = jnp.float32
BF16 = jnp.bfloat16

D_MODEL = 1024
SEQ = 2048
HEAD_DIM = 64
N_HEADS = 16
N_KV_HEADS = 4
GROUP = N_HEADS // N_KV_HEADS
D_ATTN = N_HEADS * HEAD_DIM
D_KV = N_KV_HEADS * HEAD_DIM
BLOCK = 128
ROT_DIM = HEAD_DIM // 4
ROPE_THETA = 500000.0
ATTN_SCALE = 1.0 / math.sqrt(HEAD_DIM)
NEG_INF = -1e30
D_FF = 2816
EPS = 1e-5
N_IN = 3 * D_MODEL + D_ATTN + 2 * D_KV + 2 * D_MODEL
COL_Q = 3 * D_MODEL
COL_K = COL_Q + D_ATTN
COL_V = COL_K + D_KV
COL_GC = COL_V + D_KV
COL_GA = COL_GC + D_MODEL

ADAM_LR = 0.001
ADAM_B1 = 0.9
ADAM_B2 = 0.999
ADAM_EPS = 1e-08
ADAM_WD = 0.01
ADAM_STEP = 10

N_CHIPS = 4
N_DEV = 8

V7X_VMEM_BYTES = 64 * 1024 * 1024
VMEM_LIMIT = (V7X_VMEM_BYTES * 3) // 4
LANES = 128
MESH = pl.DeviceIdType.MESH


def _params(semantics=None):
    return pltpu.CompilerParams(dimension_semantics=semantics, vmem_limit_bytes=VMEM_LIMIT)


def _sds(shape, dtype):
    return jax.ShapeDtypeStruct(shape, dtype)


HBM_SPEC = pl.BlockSpec(memory_space=pl.ANY)


def _pcall(body, name, grid, in_specs, out_specs, out_shape, operands, scratch=(), semantics=None, comm=None,
           aliases=None):
    aliases = dict(aliases or {})
    if comm is None:
        return pl.pallas_call(
            body, name=name, grid=grid, in_specs=in_specs, out_specs=out_specs, out_shape=out_shape,
            scratch_shapes=list(scratch), input_output_aliases=aliases,
            compiler_params=_params(semantics))(*operands)
    multi = isinstance(out_shape, (list, tuple))
    o_specs = list(out_specs) if multi else [out_specs]
    o_shape = list(out_shape) if multi else [out_shape]
    n_in, n_out, n_scr = len(operands), len(o_shape), len(scratch)
    n_cin, n_cout = len(comm.operands), len(comm.out_shape)

    def hosted(*refs):
        ins, cins = refs[:n_in], refs[n_in:n_in + n_cin]
        o0 = n_in + n_cin
        outs, couts = refs[o0:o0 + n_out], refs[o0 + n_out:o0 + n_out + n_cout]
        s0 = o0 + n_out + n_cout
        scr, sems = refs[s0:s0 + n_scr], refs[s0 + n_scr:]
        first = last = None
        for axis, size in enumerate(grid):
            i = pl.program_id(axis)
            first = (i == 0) if first is None else first & (i == 0)
            last = (i == size - 1) if last is None else last & (i == size - 1)

        body(*ins, *outs, *scr)

        @pl.when(first)
        def _():
            comm.start(cins, couts, sems)

        @pl.when(last)
        def _():
            comm.finish(cins, couts, sems)

    res = pl.pallas_call(
        hosted, name=name, grid=grid,
        in_specs=list(in_specs) + [HBM_SPEC] * n_cin, out_specs=o_specs + [HBM_SPEC] * n_cout,
        out_shape=o_shape + list(comm.out_shape), scratch_shapes=list(scratch) + list(comm.sems),
        input_output_aliases={**aliases, **{n_in + a: n_out + b for a, b in comm.aliases.items()}},
        compiler_params=_params(("arbitrary",) * len(grid)))(*operands, *comm.operands)
    outs = list(res[:n_out])
    return (outs if multi else outs[0]), list(res[n_out:])


def _comm_call(name, comm):
    def body(*refs):
        n_cin, n_cout = len(comm.operands), len(comm.out_shape)
        cins, couts, sems = refs[:n_cin], refs[n_cin:n_cin + n_cout], refs[n_cin + n_cout:]
        comm.start(cins, couts, sems)
        comm.finish(cins, couts, sems)

    return list(pl.pallas_call(
        body, name=name, in_specs=[HBM_SPEC] * len(comm.operands), out_specs=[HBM_SPEC] * len(comm.out_shape),
        out_shape=list(comm.out_shape), scratch_shapes=list(comm.sems),
        input_output_aliases=dict(comm.aliases))(*comm.operands))


NN = ((1,), (0,))
NT = ((1,), (1,))
TN = ((0,), (0,))


def _matmul(name, a, b, dims, grid, a_spec, b_spec, o_spec, o_shape, o_dtype, res=None, res_spec=None, comm=None):
    nk = grid[2]

    def body(*refs):
        if res is None:
            a_ref, b_ref, o_ref = refs[:3]
            r_ref = None
            scratch = refs[3:]
        else:
            a_ref, b_ref, r_ref, o_ref = refs[:4]
            scratch = refs[4:]
        p = lax.dot_general(a_ref[...], b_ref[...], (dims, ((), ())), preferred_element_type=F32)

        def finish(acc):
            if r_ref is not None:
                acc = r_ref[...] + acc
            o_ref[...] = acc.astype(o_dtype)

        if nk == 1:
            finish(p)
        else:
            acc_ref = scratch[0]
            k = pl.program_id(2)

            @pl.when(k == 0)
            def _():
                acc_ref[...] = p

            @pl.when(k > 0)
            def _():
                acc_ref[...] += p

            @pl.when(k == nk - 1)
            def _():
                finish(acc_ref[...])

    operands = [a, b] if res is None else [a, b, res]
    in_specs = [a_spec, b_spec] if res is None else [a_spec, b_spec, res_spec]
    scratch = [pltpu.VMEM(o_spec.block_shape, F32)] if nk > 1 else []
    return _pcall(body, name, grid, in_specs, o_spec, _sds(o_shape, o_dtype), operands, scratch,
                  ("parallel", "parallel", "arbitrary"), comm)


def _mm_nn(name, a, b, bm, bn, o_dtype, res=None, comm=None):
    m, k = a.shape
    n = b.shape[1]
    return _matmul(
        name, a, b, NN, (m // bm, n // bn, 1),
        pl.BlockSpec((bm, k), lambda i, j, kk: (i, 0)),
        pl.BlockSpec((k, bn), lambda i, j, kk: (0, j)),
        pl.BlockSpec((bm, bn), lambda i, j, kk: (i, j)),
        (m, n), o_dtype, res,
        None if res is None else pl.BlockSpec((bm, bn), lambda i, j, kk: (i, j)), comm,
    )


def _mm_nt(name, a, b, bm, bn, bk, o_dtype, comm=None):
    m, k = a.shape
    n = b.shape[0]
    return _matmul(
        name, a, b, NT, (m // bm, n // bn, k // bk),
        pl.BlockSpec((bm, bk), lambda i, j, kk: (i, kk)),
        pl.BlockSpec((bn, bk), lambda i, j, kk: (j, kk)),
        pl.BlockSpec((bm, bn), lambda i, j, kk: (i, j)),
        (m, n), o_dtype, comm=comm,
    )


def _mm_tn(name, a, b, bm, bn, o_dtype, comm=None):
    k, m = a.shape
    n = b.shape[1]
    return _matmul(
        name, a, b, TN, (m // bm, n // bn, 1),
        pl.BlockSpec((k, bm), lambda i, j, kk: (0, i)),
        pl.BlockSpec((k, bn), lambda i, j, kk: (0, j)),
        pl.BlockSpec((bm, bn), lambda i, j, kk: (i, j)),
        (m, n), o_dtype, comm=comm,
    )


ROWS = 256


def _row_spec(width, col=0):
    return pl.BlockSpec((ROWS, width), lambda i: (i, col))


def _full_spec(shape):
    return pl.BlockSpec(shape, lambda *_: (0,) * len(shape))


def _rms_norm(name, x, g):
    def body(x_ref, g_ref, h_ref):
        xf = x_ref[...]
        r = lax.rsqrt(jnp.mean(xf * xf, axis=-1, keepdims=True) + EPS)
        h_ref[...] = ((xf * r) * g_ref[...]).astype(BF16)

    return pl.pallas_call(
        body, name=name, grid=(SEQ // ROWS,),
        in_specs=[_row_spec(D_MODEL), _full_spec((1, D_MODEL))],
        out_specs=_row_spec(D_MODEL),
        out_shape=_sds((SEQ, D_MODEL), BF16),
        compiler_params=_params(("parallel",)),
    )(x, g)


CONV_COLS = 256


def _shift_rows(u, k):
    rows = lax.broadcasted_iota(jnp.int32, u.shape, 0)
    return jnp.where(rows >= k, pltpu.roll(u, k, axis=0), 0.0)


def _conv_fwd(proj, conv_w):
    nblk = D_MODEL // CONV_COLS

    def body(cb_ref, cc_ref, cx_ref, w_ref, y_ref):
        u = cc_ref[...] * cx_ref[...]
        w = w_ref[...]
        cv = w[0:1, :] * _shift_rows(u, 2) + w[1:2, :] * _shift_rows(u, 1) + w[2:3, :] * u
        y_ref[...] = (cb_ref[...] * cv).astype(BF16)

    def col(part):
        return pl.BlockSpec((SEQ, CONV_COLS), lambda j: (0, part * nblk + j))

    return pl.pallas_call(
        body, name="conv_fwd", grid=(nblk,),
        in_specs=[col(0), col(1), col(2), pl.BlockSpec((3, CONV_COLS), lambda j: (0, j))],
        out_specs=pl.BlockSpec((SEQ, CONV_COLS), lambda j: (0, j)),
        out_shape=_sds((SEQ, D_MODEL), BF16),
        compiler_params=_params(("parallel",)),
    )(proj, proj, proj, conv_w)


ROPE_COLS = 256


def _rope_tables():
    inv_freq = ROPE_THETA ** (-jnp.arange(0, ROT_DIM, 2, dtype=F32) / ROT_DIM)
    ang = jnp.arange(SEQ, dtype=F32)[:, None] * inv_freq[None, :]
    cos, sin = jnp.cos(ang), jnp.sin(ang)
    half = ROT_DIM // 2
    ones = jnp.ones((SEQ, HEAD_DIM - ROT_DIM), F32)
    zeros = jnp.zeros((SEQ, HEAD_DIM - ROT_DIM), F32)
    zh = jnp.zeros((SEQ, half), F32)
    c = jnp.concatenate([cos, cos, ones], axis=1)
    s_up = jnp.concatenate([-sin, zh, zeros], axis=1)
    s_dn = jnp.concatenate([zh, sin, zeros], axis=1)
    reps = ROPE_COLS // HEAD_DIM
    return tuple(jnp.tile(t, (1, reps)) for t in (c, s_up, s_dn))


def _rotate(t, c, s_up, s_dn):
    width = t.shape[1]
    half = ROT_DIM // 2
    return t * c + pltpu.roll(t, width - half, axis=1) * s_up + pltpu.roll(t, half, axis=1) * s_dn


N_QBLK = SEQ // BLOCK


def _attn_specs():
    prev = lambda n: jnp.maximum(n - 1, 0)
    q = pl.BlockSpec((BLOCK, D_ATTN), lambda n: (n, COL_Q // D_ATTN))
    k_prev = pl.BlockSpec((BLOCK, D_KV), lambda n: (prev(n), COL_K // D_KV))
    k_cur = pl.BlockSpec((BLOCK, D_KV), lambda n: (n, COL_K // D_KV))
    v_prev = pl.BlockSpec((BLOCK, D_KV), lambda n: (prev(n), COL_V // D_KV))
    v_cur = pl.BlockSpec((BLOCK, D_KV), lambda n: (n, COL_V // D_KV))
    tab_cur = pl.BlockSpec((BLOCK, ROPE_COLS), lambda n: (n, 0))
    tab_prev = pl.BlockSpec((BLOCK, ROPE_COLS), lambda n: (prev(n), 0))
    return [q, k_prev, k_cur, v_prev, v_cur] + [tab_cur] * 3 + [tab_prev] * 3


def _band_kv(kp_ref, kc_ref, vp_ref, vc_ref, tabs_cur, tabs_prev):
    k = jnp.concatenate([_rotate(kp_ref[...], *(t[...] for t in tabs_prev)),
                         _rotate(kc_ref[...], *(t[...] for t in tabs_cur))], axis=0)
    v = jnp.concatenate([vp_ref[...], vc_ref[...]], axis=0)
    return k, v


def _query_tile(q_ref, tile, tabs_cur):
    c, su, sd = (t[:, :LANES] for t in tabs_cur)
    return _rotate(q_ref[:, tile * LANES:(tile + 1) * LANES], c, su, sd).astype(BF16)


def _band_mask(n):
    kj = lax.broadcasted_iota(jnp.int32, (2 * BLOCK, BLOCK), 0)
    qi = lax.broadcasted_iota(jnp.int32, (2 * BLOCK, BLOCK), 1)
    rel = qi + BLOCK - kj
    return (rel >= 0) & (rel < BLOCK) & ((kj >= BLOCK) | (n > 0))


HEADS_PER_TILE = LANES // HEAD_DIM


def _lane_half(shape, par):
    lane = lax.broadcasted_iota(jnp.int32, shape, 1)
    return (lane < HEAD_DIM) if par == 0 else (lane >= HEAD_DIM)


def _head_tiles(kv, h):
    tile = kv[:, (h // HEADS_PER_TILE) * LANES:(h // HEADS_PER_TILE + 1) * LANES].astype(F32)
    own = jnp.where(_lane_half(tile.shape, h % HEADS_PER_TILE), tile, 0.0)
    other = pltpu.roll(own, HEAD_DIM, axis=1)
    lo, hi = (own, other) if h % HEADS_PER_TILE == 0 else (other, own)
    return lo.astype(BF16), hi.astype(BF16)


def _head_softmax(q_tile, k_half, sink, mask):
    s = lax.dot_general(k_half, q_tile, (NT, ((), ())), preferred_element_type=F32) * ATTN_SCALE
    s = jnp.where(mask, s, NEG_INF)
    m = jnp.maximum(jnp.max(s, axis=0, keepdims=True), sink)
    e = jnp.exp(s - m)
    es = jnp.exp(sink - m)
    inv = 1.0 / (jnp.sum(e, axis=0, keepdims=True) + es)
    return e * inv, es * inv


def _attn_fwd(proj, tables, sinks, comm=None):
    def body(sink_ref, q_ref, kp_ref, kc_ref, vp_ref, vc_ref, c_ref, su_ref, sd_ref, cp_ref, sup_ref, sdp_ref, o_ref):
        n = pl.program_id(0)
        mask = _band_mask(n)
        tabs_cur = (c_ref, su_ref, sd_ref)
        k, v = _band_kv(kp_ref, kc_ref, vp_ref, vc_ref, tabs_cur, (cp_ref, sup_ref, sdp_ref))
        for h in range(N_KV_HEADS):
            k_halves = _head_tiles(k, h)
            v_halves = _head_tiles(v, h)
            for t in range(GROUP // HEADS_PER_TILE):
                tile = h * (GROUP // HEADS_PER_TILE) + t
                q_tile = _query_tile(q_ref, tile, tabs_cur)
                acc = None
                for par in range(HEADS_PER_TILE):
                    sink = sink_ref[0, tile * HEADS_PER_TILE + par]
                    p, _ = _head_softmax(q_tile, k_halves[par], sink, mask)
                    o = lax.dot_general(p.astype(BF16), v_halves[par], (TN, ((), ())), preferred_element_type=F32)
                    acc = o if acc is None else acc + o
                o_ref[:, tile * LANES:(tile + 1) * LANES] = acc.astype(BF16)

    return _pcall(
        body, "attn_fwd", (N_QBLK,),
        [pl.BlockSpec(memory_space=pltpu.SMEM)] + _attn_specs(),
        pl.BlockSpec((BLOCK, D_ATTN), lambda n: (n, 0)),
        _sds((SEQ, D_ATTN), BF16), [sinks] + [proj] * 5 + list(tables) * 2, (), ("parallel",), comm)


def _branch_merge(conv_y, attn, w_co, w_ao, proj, comm=None):
    bm, bn = 1024, 512

    def body(cy_ref, at_ref, wc_ref, wa_ref, gc_ref, ga_ref, co_ref, ao_ref, mg_ref):
        co = jnp.dot(cy_ref[...], wc_ref[...], preferred_element_type=F32)
        ao = jnp.dot(at_ref[...], wa_ref[...], preferred_element_type=F32)
        co_ref[...] = co
        ao_ref[...] = ao
        mg_ref[...] = (jax.nn.sigmoid(gc_ref[...]) * co + jax.nn.sigmoid(ga_ref[...]) * ao).astype(BF16)

    act = pl.BlockSpec((bm, D_MODEL), lambda i, j: (i, 0))
    wgt = pl.BlockSpec((D_MODEL, bn), lambda i, j: (0, j))
    out = pl.BlockSpec((bm, bn), lambda i, j: (i, j))
    return _pcall(
        body, "branch_merge", (SEQ // bm, D_MODEL // bn),
        [act, act, wgt, wgt,
         pl.BlockSpec((bm, bn), lambda i, j: (i, COL_GC // bn + j)),
         pl.BlockSpec((bm, bn), lambda i, j: (i, COL_GA // bn + j))],
        [out, out, out],
        [_sds((SEQ, D_MODEL), F32), _sds((SEQ, D_MODEL), F32), _sds((SEQ, D_MODEL), BF16)],
        [conv_y, attn, w_co, w_ao, proj, proj], (), ("parallel", "parallel"), comm)


FF_BM, FF_BN = 512, 1408
FF_NB = D_FF // FF_BN


def _gate_up_fwd(h2, w_gu, comm=None):
    def body(h_ref, wg_ref, wu_ref, g_ref, u_ref, a_ref):
        h = h_ref[...]
        g = jnp.dot(h, wg_ref[...], preferred_element_type=F32)
        u = jnp.dot(h, wu_ref[...], preferred_element_type=F32)
        g_ref[...] = g
        u_ref[...] = u
        a_ref[...] = (jax.nn.silu(g) * u).astype(BF16)

    out = pl.BlockSpec((FF_BM, FF_BN), lambda i, j: (i, j))
    f32, b16 = _sds((SEQ, D_FF), F32), _sds((SEQ, D_FF), BF16)
    return _pcall(
        body, "mm_gate_up", (SEQ // FF_BM, FF_NB),
        [pl.BlockSpec((FF_BM, D_MODEL), lambda i, j: (i, 0)),
         pl.BlockSpec((D_MODEL, FF_BN), lambda i, j: (0, j)),
         pl.BlockSpec((D_MODEL, FF_BN), lambda i, j: (0, FF_NB + j))],
        [out, out, out], [f32, f32, b16], [h2, w_gu, w_gu], (), ("parallel", "parallel"), comm)


def _dact_swiglu(dx3b, w_down, g, u, comm=None):
    def body(dx_ref, w_ref, g_ref, u_ref, dg_ref, du_ref):
        da = lax.dot_general(dx_ref[...], w_ref[...], (NT, ((), ())), preferred_element_type=F32)
        g = g_ref[...]
        sg = jax.nn.sigmoid(g)
        dg_ref[...] = (da * u_ref[...] * (sg * (1.0 + g * (1.0 - sg)))).astype(BF16)
        du_ref[...] = (da * (g * sg)).astype(BF16)

    blk = pl.BlockSpec((FF_BM, FF_BN), lambda i, j: (i, j))
    b16 = _sds((SEQ, D_FF), BF16)
    return _pcall(
        body, "mm_dact", (SEQ // FF_BM, FF_NB),
        [pl.BlockSpec((FF_BM, D_MODEL), lambda i, j: (i, 0)), pl.BlockSpec((FF_BN, D_MODEL), lambda i, j: (j, 0)),
         blk, blk],
        [blk, blk], [b16, b16], [dx3b, w_down, g, u], (), ("parallel", "parallel"), comm)


def _mm_dh2(dg, du, w_gu, comm=None):
    bm = 1024
    nk = 2 * FF_NB

    def body(dg_ref, du_ref, w_ref, o_ref, acc_ref):
        k = pl.program_id(1)

        def part(a_ref):
            return lax.dot_general(a_ref[...], w_ref[...], (NT, ((), ())), preferred_element_type=F32)

        @pl.when(k == 0)
        def _():
            acc_ref[...] = part(dg_ref)

        @pl.when((k > 0) & (k < FF_NB))
        def _():
            acc_ref[...] += part(dg_ref)

        @pl.when(k >= FF_NB)
        def _():
            acc_ref[...] += part(du_ref)

        @pl.when(k == nk - 1)
        def _():
            o_ref[...] = acc_ref[...]

    return _pcall(
        body, "mm_dh2", (SEQ // bm, nk),
        [pl.BlockSpec((bm, FF_BN), lambda i, k: (i, jnp.minimum(k, FF_NB - 1))),
         pl.BlockSpec((bm, FF_BN), lambda i, k: (i, jnp.maximum(k - FF_NB, 0))),
         pl.BlockSpec((D_MODEL, FF_BN), lambda i, k: (0, k))],
        pl.BlockSpec((bm, D_MODEL), lambda i, k: (i, 0)), _sds((SEQ, D_MODEL), F32),
        [dg, du, w_gu], [pltpu.VMEM((bm, D_MODEL), F32)], ("parallel", "arbitrary"), comm)


def _mm_dw_gate_up(h2, dg, du):
    def body(h_ref, dg_ref, du_ref, o_ref):
        j = pl.program_id(0)

        def part(b_ref):
            return lax.dot_general(h_ref[...], b_ref[...], (TN, ((), ())), preferred_element_type=F32).astype(BF16)

        @pl.when(j < FF_NB)
        def _():
            o_ref[...] = part(dg_ref)

        @pl.when(j >= FF_NB)
        def _():
            o_ref[...] = part(du_ref)

    return pl.pallas_call(
        body, name="mm_dw_gate_up", grid=(2 * FF_NB,),
        in_specs=[_full_spec((SEQ, D_MODEL)),
                  pl.BlockSpec((SEQ, FF_BN), lambda j: (0, jnp.minimum(j, FF_NB - 1))),
                  pl.BlockSpec((SEQ, FF_BN), lambda j: (0, jnp.maximum(j - FF_NB, 0)))],
        out_specs=pl.BlockSpec((D_MODEL, FF_BN), lambda j: (0, j)),
        out_shape=_sds((D_MODEL, 2 * D_FF), BF16),
        compiler_params=_params(("arbitrary",)),
    )(h2, dg, du)


def _loss_head(x3, g, target):
    def body(x_ref, g_ref, t_ref, dx_ref, dxb_ref, dg_ref, loss_ref):
        i = pl.program_id(0)
        xf = x_ref[...]
        r = lax.rsqrt(jnp.mean(xf * xf, axis=-1, keepdims=True) + EPS)
        xn = xf * r
        gg = g_ref[...]
        err = xn * gg - t_ref[...]
        part = 0.5 * jnp.sum(jnp.mean(err * err, axis=-1, keepdims=True), axis=0, keepdims=True)
        dy = err * (1.0 / D_MODEL)
        dxn = dy * gg
        dx = r * (dxn - xn * jnp.mean(dxn * xn, axis=-1, keepdims=True))
        dx_ref[...] = dx
        dxb_ref[...] = dx.astype(BF16)
        dg = jnp.sum(dy * xn, axis=0, keepdims=True)
        lane0 = lax.broadcasted_iota(jnp.int32, (1, LANES), 1) == 0
        lpart = jnp.where(lane0, part, 0.0)

        @pl.when(i == 0)
        def _():
            dg_ref[...] = dg
            loss_ref[...] = lpart

        @pl.when(i > 0)
        def _():
            dg_ref[...] += dg
            loss_ref[...] += lpart

    return pl.pallas_call(
        body, name="loss_head", grid=(SEQ // ROWS,),
        in_specs=[_row_spec(D_MODEL), _full_spec((1, D_MODEL)), _row_spec(D_MODEL)],
        out_specs=[_row_spec(D_MODEL), _row_spec(D_MODEL), _full_spec((1, D_MODEL)), _full_spec((1, LANES))],
        out_shape=[_sds((SEQ, D_MODEL), F32), _sds((SEQ, D_MODEL), BF16),
                   _sds((1, D_MODEL), F32), _sds((1, LANES), F32)],
        compiler_params=_params(("arbitrary",)),
    )(x3, g, target)


def _rms_norm_bwd(name, dh, x, g, dres, with_bf16, comm=None):
    def body(dh_ref, x_ref, g_ref, dr_ref, *outs):
        i = pl.program_id(0)
        dx_ref = outs[0]
        dg_ref = outs[-1]
        xf = x_ref[...]
        r = lax.rsqrt(jnp.mean(xf * xf, axis=-1, keepdims=True) + EPS)
        xn = xf * r
        dh = dh_ref[...]
        dxn = dh * g_ref[...]
        dx = dr_ref[...] + r * (dxn - xn * jnp.mean(dxn * xn, axis=-1, keepdims=True))
        dx_ref[...] = dx
        if with_bf16:
            outs[1][...] = dx.astype(BF16)
        dg = jnp.sum(dh * xn, axis=0, keepdims=True)

        @pl.when(i == 0)
        def _():
            dg_ref[...] = dg

        @pl.when(i > 0)
        def _():
            dg_ref[...] += dg

    row = _row_spec(D_MODEL)
    out_specs = [row] + ([row] if with_bf16 else []) + [_full_spec((1, D_MODEL))]
    out_shape = ([_sds((SEQ, D_MODEL), F32)] + ([_sds((SEQ, D_MODEL), BF16)] if with_bf16 else [])
                 + [_sds((1, D_MODEL), F32)])
    return _pcall(body, name, (SEQ // ROWS,), [row, row, _full_spec((1, D_MODEL)), row], out_specs, out_shape,
                  [dh, x, g, dres], (), ("arbitrary",), comm)


def _merge_bwd(dx2b, w_o, conv_out, attn_out, proj):
    bm, bn = 1024, D_MODEL // 2

    def body(dx_ref, w_ref, co_ref, ao_ref, gc_ref, ga_ref, dco_ref, dao_ref, dgc_ref, dga_ref):
        dm = lax.dot_general(dx_ref[...], w_ref[...], (NT, ((), ())), preferred_element_type=F32)
        sc = jax.nn.sigmoid(gc_ref[...])
        sa = jax.nn.sigmoid(ga_ref[...])
        dco_ref[...] = (dm * sc).astype(BF16)
        dao_ref[...] = (dm * sa).astype(BF16)
        dgc_ref[...] = (dm * co_ref[...] * (sc * (1.0 - sc))).astype(BF16)
        dga_ref[...] = (dm * ao_ref[...] * (sa * (1.0 - sa))).astype(BF16)

    own = pl.BlockSpec((bm, bn), lambda i, j: (i, j))
    sd = _sds((SEQ, D_MODEL), BF16)
    return pl.pallas_call(
        body, name="mm_dmerged", grid=(SEQ // bm, D_MODEL // bn),
        in_specs=[pl.BlockSpec((bm, D_MODEL), lambda i, j: (i, 0)), pl.BlockSpec((bn, D_MODEL), lambda i, j: (j, 0)),
                  own, own,
                  pl.BlockSpec((bm, bn), lambda i, j: (i, COL_GC // bn + j)),
                  pl.BlockSpec((bm, bn), lambda i, j: (i, COL_GA // bn + j))],
        out_specs=[own, own, own, own], out_shape=[sd, sd, sd, sd],
        compiler_params=_params(("parallel", "parallel")),
    )(dx2b, w_o, conv_out, attn_out, proj, proj)


def _conv_bwd(dconv_y, proj, conv_w, comm=None):
    nblk = D_MODEL // CONV_COLS

    def body(dy_ref, cb_ref, cc_ref, cx_ref, w_ref, dcb_ref, dcc_ref, dcx_ref, dw_ref):
        cc = cc_ref[...]
        cx = cx_ref[...]
        u = cc * cx
        w = w_ref[...]
        u1 = _shift_rows(u, 1)
        u2 = _shift_rows(u, 2)
        cv = w[0:1, :] * u2 + w[1:2, :] * u1 + w[2:3, :] * u
        dy = dy_ref[...]
        dcb_ref[...] = (dy * cv).astype(BF16)
        dcv = dy * cb_ref[...]
        rows = lax.broadcasted_iota(jnp.int32, dcv.shape, 0)
        up1 = jnp.where(rows < SEQ - 1, pltpu.roll(dcv, SEQ - 1, axis=0), 0.0)
        up2 = jnp.where(rows < SEQ - 2, pltpu.roll(dcv, SEQ - 2, axis=0), 0.0)
        du = w[2:3, :] * dcv + w[1:2, :] * up1 + w[0:1, :] * up2
        dcc_ref[...] = (du * cx).astype(BF16)
        dcx_ref[...] = (du * cc).astype(BF16)
        dw_ref[...] = jnp.concatenate(
            [jnp.sum(dcv * u2, axis=0, keepdims=True),
             jnp.sum(dcv * u1, axis=0, keepdims=True),
             jnp.sum(dcv * u, axis=0, keepdims=True)], axis=0)

    def col(part):
        return pl.BlockSpec((SEQ, CONV_COLS), lambda j: (0, part * nblk + j))

    own = pl.BlockSpec((SEQ, CONV_COLS), lambda j: (0, j))
    wsp = pl.BlockSpec((3, CONV_COLS), lambda j: (0, j))
    sd = _sds((SEQ, D_MODEL), BF16)
    return _pcall(
        body, "conv_bwd", (nblk,), [own, col(0), col(1), col(2), wsp], [own, own, own, wsp],
        [sd, sd, sd, _sds((3, D_MODEL), F32)], [dconv_y, proj, proj, proj, conv_w], (), ("parallel",), comm)


def _attn_bwd(proj, dattn, sinks, tables, comm=None):
    def body(sink_ref, q_ref, kp_ref, kc_ref, vp_ref, vc_ref, c_ref, su_ref, sd_ref, cp_ref, sup_ref, sdp_ref,
             do_ref, dq_ref, dkp_ref, dkc_ref, dvp_ref, dvc_ref, ds_ref):
        n = pl.program_id(0)
        mask = _band_mask(n)
        tabs_cur = (c_ref, su_ref, sd_ref)
        k, v = _band_kv(kp_ref, kc_ref, vp_ref, vc_ref, tabs_cur, (cp_ref, sup_ref, sdp_ref))
        lane = lax.broadcasted_iota(jnp.int32, (1, LANES), 1)
        dsink = jnp.zeros((1, LANES), F32)
        c, su, sd = c_ref[:, :LANES], su_ref[:, :LANES], sd_ref[:, :LANES]
        dk_tiles = [None] * (N_KV_HEADS // HEADS_PER_TILE)
        dv_tiles = [None] * (N_KV_HEADS // HEADS_PER_TILE)
        for h in range(N_KV_HEADS):
            k_halves = _head_tiles(k, h)
            v_halves = _head_tiles(v, h)
            dk_par = [None] * HEADS_PER_TILE
            dv_par = [None] * HEADS_PER_TILE
            for t in range(GROUP // HEADS_PER_TILE):
                tile = h * (GROUP // HEADS_PER_TILE) + t
                q_tile = _query_tile(q_ref, tile, tabs_cur)
                do_tile = do_ref[:, tile * LANES:(tile + 1) * LANES]
                dq_tile = None
                for par in range(HEADS_PER_TILE):
                    head = tile * HEADS_PER_TILE + par
                    p, p_sink = _head_softmax(q_tile, k_halves[par], sink_ref[0, head], mask)
                    dp = lax.dot_general(v_halves[par], do_tile, (NT, ((), ())), preferred_element_type=F32)
                    delta = jnp.sum(p * dp, axis=0, keepdims=True)
                    ds = (p * (dp - delta) * ATTN_SCALE).astype(BF16)
                    dq = lax.dot_general(ds, k_halves[par], (TN, ((), ())), preferred_element_type=F32)
                    dq_tile = dq if dq_tile is None else dq_tile + dq
                    dk = jnp.dot(ds, q_tile, preferred_element_type=F32)
                    dv = jnp.dot(p.astype(BF16), do_tile, preferred_element_type=F32)
                    dk_par[par] = dk if dk_par[par] is None else dk_par[par] + dk
                    dv_par[par] = dv if dv_par[par] is None else dv_par[par] + dv
                    val = -jnp.sum(p_sink * delta, axis=1, keepdims=True)
                    dsink = dsink + jnp.where(lane == head, val, 0.0)
                dq_ref[:, tile * LANES:(tile + 1) * LANES] = _rotate(dq_tile, c, -su, -sd).astype(BF16)
            own = h % HEADS_PER_TILE
            for par_grads, tiles in ((dk_par, dk_tiles), (dv_par, dv_tiles)):
                shifted = pltpu.roll(par_grads[1 - own], HEAD_DIM, axis=1)
                total = jnp.where(_lane_half(shifted.shape, own), par_grads[own] + shifted, 0.0)
                i = h // HEADS_PER_TILE
                tiles[i] = total if tiles[i] is None else tiles[i] + total
        for i in range(N_KV_HEADS // HEADS_PER_TILE):
            cols = slice(i * LANES, (i + 1) * LANES)
            dkp_ref[:, cols] = dk_tiles[i][:BLOCK, :]
            dkc_ref[:, cols] = dk_tiles[i][BLOCK:, :]
            dvp_ref[:, cols] = dv_tiles[i][:BLOCK, :]
            dvc_ref[:, cols] = dv_tiles[i][BLOCK:, :]

        @pl.when(n == 0)
        def _():
            ds_ref[...] = dsink

        @pl.when(n > 0)
        def _():
            ds_ref[...] += dsink

    blk = pl.BlockSpec((BLOCK, D_KV), lambda n: (n, 0))
    kv = _sds((SEQ, D_KV), F32)
    return _pcall(
        body, "attn_bwd", (N_QBLK,),
        [pl.BlockSpec(memory_space=pltpu.SMEM)] + _attn_specs() + [pl.BlockSpec((BLOCK, D_ATTN), lambda n: (n, 0))],
        [pl.BlockSpec((BLOCK, D_ATTN), lambda n: (n, 0)), blk, blk, blk, blk, _full_spec((1, LANES))],
        [_sds((SEQ, D_ATTN), BF16), kv, kv, kv, kv, _sds((1, LANES), F32)],
        [sinks] + [proj] * 5 + list(tables) * 2 + [dattn], (), ("arbitrary",), comm)


def _kv_grad_combine(dk_prev, dk_cur, dv_prev, dv_cur, tables):
    def body(kp_ref, kc_ref, vp_ref, vc_ref, c_ref, su_ref, sd_ref, o_ref):
        m = pl.program_id(0)
        has_next = m < N_QBLK - 1
        dk = kc_ref[...] + jnp.where(has_next, kp_ref[...], 0.0)
        dv = vc_ref[...] + jnp.where(has_next, vp_ref[...], 0.0)
        o_ref[:, :D_KV] = _rotate(dk, c_ref[...], -su_ref[...], -sd_ref[...]).astype(BF16)
        o_ref[:, D_KV:] = dv.astype(BF16)

    cur = pl.BlockSpec((BLOCK, D_KV), lambda m: (m, 0))
    nxt = pl.BlockSpec((BLOCK, D_KV), lambda m: (jnp.minimum(m + 1, N_QBLK - 1), 0))
    return pl.pallas_call(
        body, name="kv_grad_combine", grid=(N_QBLK,),
        in_specs=[nxt, cur, nxt, cur, cur, cur, cur],
        out_specs=pl.BlockSpec((BLOCK, 2 * D_KV), lambda m: (m, 0)),
        out_shape=_sds((SEQ, 2 * D_KV), BF16),
        compiler_params=_params(("parallel",)),
    )(dk_prev, dk_cur, dv_prev, dv_cur, *tables)


MATRICES = {
    "w_in": (D_MODEL, N_IN // N_CHIPS, "col"),
    "w_conv_out": (D_MODEL // N_CHIPS, D_MODEL, "row"),
    "w_attn_out": (D_MODEL // N_CHIPS, D_MODEL, "row"),
    "w_o": (D_MODEL // N_CHIPS, D_MODEL, "row"),
    "w_gate_up": (D_MODEL, 2 * D_FF // N_CHIPS, "col"),
    "w_down": (D_FF // N_CHIPS, D_MODEL, "row"),
}
BF16_ROW_TILE = 16
CONV_W_COLS = D_MODEL // N_CHIPS
SMALL_ROWS = 8


def _whole_shape(spec):
    rows, cols, kind = spec
    return (rows, cols * N_CHIPS) if kind == "col" else (rows * N_CHIPS, cols)


def _half_shape(spec):
    return (spec[0] // 2, spec[1])


def _aligned(start, multiple):
    return start if isinstance(start, int) else pl.multiple_of(start, multiple)


def _region(ref, spec, shard, half, part=0, parts=1):
    rows, cols, kind = spec
    hr = rows // 2
    n = hr // parts
    if kind == "col":
        return ref.at[pl.ds(_aligned(half * hr + part * n, BF16_ROW_TILE), n),
                      pl.ds(_aligned(shard * cols, LANES), cols)]
    return ref.at[pl.ds(_aligned(shard * rows + half * hr + part * n, BF16_ROW_TILE), n), :]


def _position():
    x, y, c = lax.axis_index("x"), lax.axis_index("y"), lax.axis_index("c")
    chips = [(1 - x, y), (x, 1 - y), (1 - x, 1 - y)]
    return x, y, c, chips


def _shard_of(chip):
    return 2 * chip[0] + chip[1]


def _remote(src, dst, send_sem, recv_sem, to):
    return pltpu.make_async_remote_copy(src_ref=src, dst_ref=dst, send_sem=send_sem, recv_sem=recv_sem,
                                        device_id=to, device_id_type=MESH)


def _to_bf16_in_whole(name, w, spec, shard, rows):
    steps = spec[0] // rows

    def body(s_ref, w_ref, o_ref):
        del s_ref
        o_ref[...] = w_ref[...].astype(BF16)

    if spec[2] == "col":
        out_spec = pl.BlockSpec((rows, spec[1]), lambda i, s_ref: (i, s_ref[0]))
    else:
        out_spec = pl.BlockSpec((rows, spec[1]), lambda i, s_ref: (s_ref[0] * steps + i, 0))
    grid_spec = pltpu.PrefetchScalarGridSpec(
        num_scalar_prefetch=1, grid=(steps,),
        in_specs=[pl.BlockSpec((rows, spec[1]), lambda i, s_ref: (i, 0))], out_specs=out_spec)
    return pl.pallas_call(
        body, name=name, grid_spec=grid_spec, out_shape=_sds(_whole_shape(spec), BF16),
        compiler_params=_params(("parallel",)),
    )(shard, w)


class _Gather:
    def __init__(self, wholes, pieces, conv_w=None):
        self.pieces = pieces
        self.n = len(wholes)
        self.with_conv_w = conv_w is not None
        self.operands = list(wholes) + ([conv_w] if self.with_conv_w else [])
        self.out_shape = [_sds(w.shape, w.dtype) for w in wholes]
        if self.with_conv_w:
            self.out_shape.append(_sds((3, D_MODEL), F32))
        self.aliases = {i: i for i in range(self.n)}
        n_ici = 3 * len(pieces)
        self.sems = [pltpu.SemaphoreType.DMA((n_ici,))] * 4
        if self.with_conv_w:
            self.sems += [pltpu.SemaphoreType.DMA((1,)), pltpu.SemaphoreType.DMA((3,)), pltpu.SemaphoreType.DMA((3,))]

    def _conv_w(self, cins, couts, sems, with_recvs):
        cw_in, cw_out = cins[self.n], couts[self.n]
        x, y, c, chips = _position()

        def cols(shard):
            return cw_out.at[:, pl.ds(_aligned(shard * CONV_W_COLS, LANES), CONV_W_COLS)]

        me = _shard_of((x, y))
        local = pltpu.make_async_copy(cw_in, cols(me), sems[4].at[0])
        sends = [_remote(cw_in, cols(me), sems[5].at[j], sems[6].at[j], (*chip, c)) for j, chip in enumerate(chips)]
        if not with_recvs:
            return local, sends, []
        recvs = [_remote(cols(_shard_of(chip)), cols(_shard_of(chip)), sems[5].at[j], sems[6].at[j], (*chip, c))
                 for j, chip in enumerate(chips)]
        return local, sends, recvs

    def start(self, cins, couts, sems):
        x, y, c, chips = _position()
        me = _shard_of((x, y))
        if self.with_conv_w:
            local, sends, _ = self._conv_w(cins, couts, sems, False)
            local.start()
            for cp in sends:
                cp.start()
        for p, (i, spec, part, parts) in enumerate(self.pieces):
            mine = _region(couts[i], spec, me, c, part, parts)
            for j, chip in enumerate(chips):
                _remote(mine, mine, sems[0].at[3 * p + j], sems[1].at[3 * p + j], (*chip, c)).start()

    def finish(self, cins, couts, sems):
        x, y, c, chips = _position()
        me = _shard_of((x, y))
        sibling = (x, y, 1 - c)
        send_a, recv_a, send_b, recv_b = sems[:4]
        passed = []
        for p, (i, spec, part, parts) in enumerate(self.pieces):
            for j, chip in enumerate(chips):
                k = 3 * p + j
                landed = _region(couts[i], spec, _shard_of(chip), c, part, parts)
                _remote(landed, landed, send_a.at[k], recv_a.at[k], (*chip, c)).wait_recv()
                cp = _remote(landed, landed, send_b.at[k], recv_b.at[k], sibling)
                cp.start()
                passed.append(cp)
        for p, (i, spec, part, parts) in enumerate(self.pieces):
            mine = _region(couts[i], spec, me, c, part, parts)
            for j, chip in enumerate(chips):
                k = 3 * p + j
                other = _region(couts[i], spec, _shard_of(chip), 1 - c, part, parts)
                _remote(other, other, send_b.at[k], recv_b.at[k], sibling).wait_recv()
                _remote(mine, mine, send_a.at[k], recv_a.at[k], (*chip, c)).wait_send()
        for cp in passed:
            cp.wait_send()
        if self.with_conv_w:
            local, sends, recvs = self._conv_w(cins, couts, sems, True)
            for cp in recvs:
                cp.wait_recv()
            for cp in sends:
                cp.wait_send()
            local.wait()


def _mm_in_gather(h1, w_whole, comm):
    spec = MATRICES["w_in"]
    cols = spec[1]
    bm = SEQ // 2

    def body(h_ref, w_in_ref, proj_ref, w_ref, wbuf, obuf, send_a, recv_a, send_b, recv_b, load_sem, store_sems):
        del w_in_ref
        s, mi = pl.program_id(0), pl.program_id(1)
        x, y, c, chips = _position()
        me = _shard_of((x, y))
        sibling = (x, y, 1 - c)
        mine = _region(w_ref, spec, me, c)

        @pl.when((s == 0) & (mi == 0))
        def _():
            for j, chip in enumerate(chips):
                _remote(mine, mine, send_a.at[j], recv_a.at[j], (*chip, c)).start()

        shard = me
        for j, chip in enumerate(chips):
            shard = jnp.where(s == j + 1, _shard_of(chip), shard)

            @pl.when((s == j + 1) & (mi == 0))
            def _():
                landed = _region(w_ref, spec, _shard_of(chip), c)
                _remote(landed, landed, send_a.at[j], recv_a.at[j], (*chip, c)).wait_recv()
                _remote(landed, landed, send_b.at[j], recv_b.at[j], sibling).start()
                other = _region(w_ref, spec, _shard_of(chip), 1 - c)
                _remote(other, other, send_b.at[j], recv_b.at[j], sibling).wait_recv()

        col0 = pl.multiple_of(shard * cols, LANES)

        @pl.when(mi == 0)
        def _():
            load = pltpu.make_async_copy(w_ref.at[:, pl.ds(col0, cols)], wbuf, load_sem.at[0])
            load.start()
            load.wait()

        def store():
            rows = pl.ds(pl.multiple_of(mi * bm, bm), bm)
            return pltpu.make_async_copy(obuf.at[mi], proj_ref.at[rows, pl.ds(col0, cols)], store_sems.at[mi])

        @pl.when(s > 0)
        def _():
            store().wait()

        obuf[mi] = jnp.dot(h_ref[...], wbuf[...], preferred_element_type=F32)
        store().start()

        @pl.when(s == N_CHIPS - 1)
        def _():
            store().wait()

        @pl.when((s == N_CHIPS - 1) & (mi == 1))
        def _():
            for j, chip in enumerate(chips):
                landed = _region(w_ref, spec, _shard_of(chip), c)
                _remote(mine, mine, send_a.at[j], recv_a.at[j], (*chip, c)).wait_send()
                _remote(landed, landed, send_b.at[j], recv_b.at[j], sibling).wait_send()

    sem3 = pltpu.SemaphoreType.DMA((3,))
    (proj, whole), extra = _pcall(
        body, "mm_in", (N_CHIPS, SEQ // bm),
        [pl.BlockSpec((bm, D_MODEL), lambda s, m: (m, 0)), HBM_SPEC], [HBM_SPEC, HBM_SPEC],
        [_sds((SEQ, N_IN), F32), _sds(w_whole.shape, w_whole.dtype)], [h1, w_whole],
        [pltpu.VMEM((D_MODEL, cols), BF16), pltpu.VMEM((SEQ // bm, bm, cols), F32), sem3, sem3, sem3, sem3,
         pltpu.SemaphoreType.DMA((1,)), pltpu.SemaphoreType.DMA((SEQ // bm,))],
        None, comm, aliases={1: 1})
    return proj, whole, extra


def _pack_small(dg_mix, dg_ffn, dg_final, dconv_w, dsinks, loss_row):
    def body(a_ref, b_ref, c_ref, w_ref, s_ref, l_ref, o_ref):
        pad = jnp.zeros((1, D_MODEL - LANES), F32)
        o_ref[0:1, :] = a_ref[...]
        o_ref[1:2, :] = b_ref[...]
        o_ref[2:3, :] = c_ref[...]
        o_ref[3:6, :] = w_ref[...]
        o_ref[6:7, :] = jnp.concatenate([s_ref[...], pad], axis=1)
        o_ref[7:8, :] = jnp.concatenate([l_ref[...], pad], axis=1)

    return pl.pallas_call(
        body, name="pack_small", out_shape=_sds((SMALL_ROWS, D_MODEL), F32),
        compiler_params=_params(),
    )(dg_mix, dg_ffn, dg_final, dconv_w, dsinks, loss_row)


class _Pair:
    def __init__(self, dws, specs):
        self.specs = specs
        self.operands = list(dws)
        self.out_shape = [_sds((N_CHIPS, *_half_shape(s)), BF16) for s in specs]
        self.aliases = {}
        n = N_CHIPS * len(specs)
        self.sems = [pltpu.SemaphoreType.DMA((n,)), pltpu.SemaphoreType.DMA((n,))]

    def _copies(self, cins, couts, sems):
        x, y, c, _ = _position()
        sibling = (x, y, 1 - c)
        for i, spec in enumerate(self.specs):
            for t in range(N_CHIPS):
                k = N_CHIPS * i + t
                yield _remote(_region(cins[i], spec, t, 1 - c), couts[i].at[t], sems[0].at[k], sems[1].at[k], sibling)

    def start(self, cins, couts, sems):
        for cp in self._copies(cins, couts, sems):
            cp.start()

    def finish(self, cins, couts, sems):
        for cp in self._copies(cins, couts, sems):
            cp.wait()


class _SmallAllToAll:
    def __init__(self, small):
        self.operands = [small]
        self.out_shape = [_sds((N_DEV, SMALL_ROWS, D_MODEL), F32)]
        self.aliases = {}
        self.sems = [pltpu.SemaphoreType.DMA((N_DEV - 1,)), pltpu.SemaphoreType.DMA((N_DEV - 1,)),
                     pltpu.SemaphoreType.DMA((1,))]

    def _copies(self, cins, couts, sems):
        x, y, c, _ = _position()
        me = 4 * x + 2 * y + c
        out = []
        for r in range(1, N_DEV):
            flip = ((r >> 2) & 1, (r >> 1) & 1, r & 1)
            peer = tuple(1 - p if f else p for p, f in zip((x, y, c), flip))
            theirs = couts[0].at[4 * peer[0] + 2 * peer[1] + peer[2]]
            out.append((_remote(cins[0], couts[0].at[me], sems[0].at[r - 1], sems[1].at[r - 1], peer),
                        _remote(theirs, theirs, sems[0].at[r - 1], sems[1].at[r - 1], peer)))
        return pltpu.make_async_copy(cins[0], couts[0].at[me], sems[2].at[0]), out

    def start(self, cins, couts, sems):
        own, copies = self._copies(cins, couts, sems)
        own.start()
        for send, _ in copies:
            send.start()

    def finish(self, cins, couts, sems):
        own, copies = self._copies(cins, couts, sems)
        for send, recv in copies:
            recv.wait_recv()
            send.wait_send()
        own.wait()


class _Both:
    def __init__(self, a, b):
        self.a, self.b = a, b
        self.operands = list(a.operands) + list(b.operands)
        self.out_shape = list(a.out_shape) + list(b.out_shape)
        self.aliases = dict(a.aliases)
        self.aliases.update({len(a.operands) + k: len(a.out_shape) + v for k, v in b.aliases.items()})
        self.sems = list(a.sems) + list(b.sems)

    def _split(self, cins, couts, sems):
        na, ma, sa = len(self.a.operands), len(self.a.out_shape), len(self.a.sems)
        return (cins[:na], couts[:ma], sems[:sa]), (cins[na:], couts[ma:], sems[sa:])

    def start(self, cins, couts, sems):
        for plan, args in zip((self.a, self.b), self._split(cins, couts, sems)):
            plan.start(*args)

    def finish(self, cins, couts, sems):
        for plan, args in zip((self.a, self.b), self._split(cins, couts, sems)):
            plan.finish(*args)


def _pair_sum(name, specs, dws, got, place):
    n_mat = len(specs)

    def body(p_ref, *refs):
        t = pl.program_id(0)
        mine, theirs = refs[:n_mat], refs[n_mat:2 * n_mat]
        outs, owns = refs[2 * n_mat:3 * n_mat], refs[3 * n_mat:]
        for a, b, o, own in zip(mine, theirs, outs, owns):
            s = (a[...].astype(F32) + b[...].astype(F32)).astype(BF16)
            o[...] = s

            @pl.when(t == p_ref[1])
            def _():
                own[...] = s

    def mine_spec(spec):
        hr, cols = _half_shape(spec)
        if spec[2] == "col":
            return pl.BlockSpec((hr, cols), lambda t, p_ref: (p_ref[0], t))
        return pl.BlockSpec((hr, cols), lambda t, p_ref: (2 * t + p_ref[0], 0))

    def slot_spec(spec):
        return pl.BlockSpec((None, *_half_shape(spec)), lambda t, p_ref: (t, 0, 0))

    def own_spec(spec):
        return pl.BlockSpec((None, *_half_shape(spec)), lambda t, p_ref: (p_ref[1], 0, 0))

    slots = [_sds((N_CHIPS, *_half_shape(s)), BF16) for s in specs]
    grid_spec = pltpu.PrefetchScalarGridSpec(
        num_scalar_prefetch=1, grid=(N_CHIPS,),
        in_specs=[mine_spec(s) for s in specs] + [slot_spec(s) for s in specs],
        out_specs=[slot_spec(s) for s in specs] + [own_spec(s) for s in specs])
    res = pl.pallas_call(
        body, name=name, grid_spec=grid_spec, out_shape=slots + slots,
        compiler_params=_params(("arbitrary",)),
    )(place, *dws, *got)
    return list(res[:n_mat]), list(res[n_mat:])


class _ChipExchange:
    def __init__(self, sums, slots, part=0, parts=1):
        self.n = len(sums)
        self.part, self.parts = part, parts
        self.operands = list(sums) + list(slots)
        self.out_shape = [_sds(s.shape, s.dtype) for s in slots]
        self.aliases = {self.n + i: i for i in range(self.n)}
        self.sems = [pltpu.SemaphoreType.DMA((3 * self.n,)), pltpu.SemaphoreType.DMA((3 * self.n,))]

    def _rows(self, ref, slot):
        n = ref.shape[1] // self.parts
        return ref.at[slot, pl.ds(self.part * n, n), :]

    def _copies(self, cins, couts, sems):
        x, y, c, chips = _position()
        me = _shard_of((x, y))
        for i in range(self.n):
            for j, chip in enumerate(chips):
                k = 3 * i + j
                theirs = self._rows(couts[i], _shard_of(chip))
                yield (_remote(self._rows(cins[i], _shard_of(chip)), self._rows(couts[i], me),
                               sems[0].at[k], sems[1].at[k], (*chip, c)),
                       _remote(theirs, theirs, sems[0].at[k], sems[1].at[k], (*chip, c)))

    def start(self, cins, couts, sems):
        for send, _ in self._copies(cins, couts, sems):
            send.start()

    def finish(self, cins, couts, sems):
        for send, recv in self._copies(cins, couts, sems):
            recv.wait_recv()
            send.wait_send()


def _chip_sum(name, specs, slots, core):
    steps = 2
    n_mat = len(specs)

    def body(c_ref, *refs):
        del c_ref
        ins, outs = refs[:n_mat], refs[n_mat:]
        for a, o in zip(ins, outs):
            acc = a[0].astype(F32)
            for t in range(1, N_CHIPS):
                acc = acc + a[t].astype(F32)
            o[...] = acc

    def in_spec(spec):
        hr, cols = _half_shape(spec)
        return pl.BlockSpec((N_CHIPS, hr // steps, cols), lambda i, c_ref: (0, i, 0))

    def out_spec(spec):
        hr, cols = _half_shape(spec)
        return pl.BlockSpec((hr // steps, cols), lambda i, c_ref: (c_ref[0] * steps + i, 0))

    grid_spec = pltpu.PrefetchScalarGridSpec(
        num_scalar_prefetch=1, grid=(steps,),
        in_specs=[in_spec(s) for s in specs], out_specs=[out_spec(s) for s in specs])
    return list(pl.pallas_call(
        body, name=name, grid_spec=grid_spec,
        out_shape=[_sds((s[0], s[1]), F32) for s in specs],
        compiler_params=_params(("parallel",)),
    )(core, *slots))


class _HalfExchange:
    def __init__(self, grads, specs):
        self.specs = specs
        self.operands = list(grads)
        self.out_shape = [_sds(g.shape, g.dtype) for g in grads]
        self.aliases = {i: i for i in range(len(grads))}
        self.sems = [pltpu.SemaphoreType.DMA((len(grads),)), pltpu.SemaphoreType.DMA((len(grads),))]

    def _copies(self, couts, sems):
        x, y, c, _ = _position()
        sibling = (x, y, 1 - c)
        for i, spec in enumerate(self.specs):
            hr = spec[0] // 2
            mine = couts[i].at[pl.ds(_aligned(c * hr, 8), hr), :]
            theirs = couts[i].at[pl.ds(_aligned((1 - c) * hr, 8), hr), :]
            yield (_remote(mine, mine, sems[0].at[i], sems[1].at[i], sibling),
                   _remote(theirs, theirs, sems[0].at[i], sems[1].at[i], sibling))

    def start(self, cins, couts, sems):
        for send, _ in self._copies(couts, sems):
            send.start()

    def finish(self, cins, couts, sems):
        for send, recv in self._copies(couts, sems):
            recv.wait_recv()
            send.wait_send()


def _small_sum(blocks):
    def body(b_ref, o_ref):
        acc = b_ref[0]
        for d in range(1, N_DEV):
            acc = acc + b_ref[d]
        o_ref[...] = acc

    return pl.pallas_call(
        body, name="small_sum", out_shape=_sds((SMALL_ROWS, D_MODEL), F32), compiler_params=_params(),
    )(blocks)


def _adamw(name, params, steps, comm=None):
    n = len(params)

    def body(*refs):
        for p in range(n):
            w_ref, g_ref, m_ref, v_ref = refs[4 * p:4 * p + 4]
            d_ref, nm_ref, nv_ref = refs[4 * n + 3 * p:4 * n + 3 * p + 3]
            g = g_ref[...]
            m = ADAM_B1 * m_ref[...] + (1.0 - ADAM_B1) * g
            v = ADAM_B2 * v_ref[...] + (1.0 - ADAM_B2) * jnp.square(g)
            m_hat = m / (1.0 - ADAM_B1 ** ADAM_STEP)
            v_hat = v / (1.0 - ADAM_B2 ** ADAM_STEP)
            d_ref[...] = -ADAM_LR * (m_hat / (jnp.sqrt(v_hat) + ADAM_EPS) + ADAM_WD * w_ref[...])
            nm_ref[...] = m
            nv_ref[...] = v

    in_specs, out_specs, out_shape, operands = [], [], [], []
    for w, g, m, v in params:
        spec = pl.BlockSpec((w.shape[0] // steps, w.shape[1]), lambda i: (i, 0))
        in_specs += [spec] * 4
        out_specs += [spec] * 3
        out_shape += [_sds(w.shape, F32)] * 3
        operands += [w, g, m, v]
    res = _pcall(body, name, (steps,), in_specs, out_specs, out_shape, operands, (), ("parallel",), comm)
    outs, extra = res if comm is not None else (res, None)
    triples = [tuple(outs[3 * p:3 * p + 3]) for p in range(n)]
    return triples if comm is None else (triples, extra)


MATRIX_NAMES = tuple(MATRICES)
WEIGHT_ORDER = ("g_mix", "w_in", "conv_w", "attn_sinks", "w_conv_out", "w_attn_out", "w_o", "g_ffn",
                "w_gate_up", "w_down", "g_final")


def kernel(x, g_mix, w_in, conv_w, attn_sinks, w_conv_out, w_attn_out, w_o, g_ffn, w_gate_up, w_down, g_final, loss_target, m_g_mix, m_w_in, m_conv_w, m_attn_sinks, m_w_conv_out, m_w_attn_out, m_w_o, m_g_ffn, m_w_gate_up, m_w_down, m_g_final, v_g_mix, v_w_in, v_conv_w, v_attn_sinks, v_w_conv_out, v_w_attn_out, v_w_o, v_g_ffn, v_w_gate_up, v_w_down, v_g_final):
    w = dict(g_mix=g_mix, w_in=w_in[0], conv_w=conv_w[0], attn_sinks=attn_sinks, w_conv_out=w_conv_out[0],
             w_attn_out=w_attn_out[0], w_o=w_o[0], g_ffn=g_ffn, w_gate_up=w_gate_up[0], w_down=w_down[0],
             g_final=g_final[None, :])
    m = dict(g_mix=m_g_mix, w_in=m_w_in[0], conv_w=m_conv_w[0], attn_sinks=m_attn_sinks,
             w_conv_out=m_w_conv_out[0], w_attn_out=m_w_attn_out[0], w_o=m_w_o[0], g_ffn=m_g_ffn,
             w_gate_up=m_w_gate_up[0], w_down=m_w_down[0], g_final=m_g_final[None, :])
    v = dict(g_mix=v_g_mix, w_in=v_w_in[0], conv_w=v_conv_w[0], attn_sinks=v_attn_sinks,
             w_conv_out=v_w_conv_out[0], w_attn_out=v_w_attn_out[0], w_o=v_w_o[0], g_ffn=v_g_ffn,
             w_gate_up=v_w_gate_up[0], w_down=v_w_down[0], g_final=v_g_final[None, :])
    shard = (2 * lax.axis_index("x") + lax.axis_index("y")).astype(jnp.int32)
    core = lax.axis_index("c").astype(jnp.int32)
    shard1, core1, place = shard.reshape((1,)), core.reshape((1,)), jnp.stack([core, shard])
    spec = MATRICES
    xs, target, sinks = x[0], loss_target[0], w["attn_sinks"]
    tables = _rope_tables()

    def gather(names, part=0, parts=1):
        return _Gather([whole[n] for n in names], [(i, spec[n], part, parts) for i, n in enumerate(names)])

    def pair(names):
        return _Pair([dw[n] for n in names], [spec[n] for n in names])

    def pair_sum(tag, names, got):
        return _pair_sum("pair_sum_" + tag, [spec[n] for n in names], [dw[n] for n in names], got, place)

    cast_rows = {"w_down": D_FF // N_CHIPS // 2}
    whole = {n: _to_bf16_in_whole("cast_" + n, w[n], spec[n], shard1, cast_rows.get(n, 256)) for n in MATRIX_NAMES}

    mixers = ("w_conv_out", "w_attn_out", "w_o")
    h1 = _rms_norm("norm_mix", xs, w["g_mix"])
    proj, whole["w_in"], (whole["w_conv_out"], conv_w_whole) = _mm_in_gather(
        h1, whole["w_in"], _Gather([whole["w_conv_out"]], [(0, spec["w_conv_out"], 0, 1)], conv_w=w["conv_w"]))
    conv_y = _conv_fwd(proj, conv_w_whole)
    behind_attn = ("w_attn_out", "w_o", "w_gate_up")
    attn, got = _attn_fwd(proj, tables, sinks, comm=gather(behind_attn))
    whole.update(zip(behind_attn, got))
    (conv_out, attn_out, merged), (whole["w_down"],) = _branch_merge(
        conv_y, attn, whole["w_conv_out"], whole["w_attn_out"], proj, comm=gather(("w_down",)))
    x2 = _mm_nn("mm_o", merged, whole["w_o"], 1024, 1024, F32, res=xs)
    h2 = _rms_norm("norm_ffn", x2, w["g_ffn"])
    gate, up, act = _gate_up_fwd(h2, whole["w_gate_up"])
    x3 = _mm_nn("mm_down", act, whole["w_down"], 1024, 512, F32, res=x2)
    dx3, dx3b, dg_final, loss_row = _loss_head(x3, w["g_final"], target)

    dw = {}
    dw["w_down"] = _mm_tn("mm_dw_down", act, dx3b, 1408, 1024, BF16)
    (dgate, dup), got = _dact_swiglu(dx3b, whole["w_down"], gate, up, comm=pair(("w_down",)))
    sums_a, own_a = pair_sum("down", ("w_down",), got)
    dh2, slots_a = _mm_dh2(dgate, dup, whole["w_gate_up"], comm=_ChipExchange(sums_a, own_a))
    dw["w_gate_up"] = _mm_dw_gate_up(h2, dgate, dup)
    (dx2, dx2b, dg_ffn), got = _rms_norm_bwd("norm_ffn_bwd", dh2, x2, w["g_ffn"], dx3, True, comm=pair(("w_gate_up",)))
    sums_b, own_b = pair_sum("gate_up", ("w_gate_up",), got)
    dw["w_o"] = _mm_tn("mm_dw_o", merged, dx2b, 1024, 1024, BF16)
    dco, dao, dgc, dga = _merge_bwd(dx2b, whole["w_o"], conv_out, attn_out, proj)
    dconv_y = _mm_nt("mm_dconv_y", dco, whole["w_conv_out"], 1024, 1024, D_MODEL, F32)
    dw["w_conv_out"] = _mm_tn("mm_dw_conv_out", conv_y, dco, 1024, 1024, BF16)
    dattn = _mm_nt("mm_dattn", dao, whole["w_attn_out"], 1024, 1024, D_MODEL, BF16)
    dw["w_attn_out"] = _mm_tn("mm_dw_attn_out", attn, dao, 1024, 1024, BF16)
    (dcb, dcc, dcx, dconv_w), got = _conv_bwd(dconv_y, proj, conv_w_whole, comm=pair(mixers))
    sums_c, own_c = pair_sum("mixers", mixers, got)
    (dq, dk_prev, dk_cur, dv_prev, dv_cur, dsinks), slots_b = _attn_bwd(
        proj, dattn, sinks, tables, comm=_ChipExchange(sums_b, own_b))
    dkv = _kv_grad_combine(dk_prev, dk_cur, dv_prev, dv_cur, tables)
    dproj = jnp.concatenate([dcb, dcc, dcx, dq, dkv, dgc, dga], axis=1)
    dw["w_in"], slots_c = _mm_tn("mm_dw_in", h1, dproj, 1024, 1664, BF16, comm=_ChipExchange(sums_c, own_c))
    sums_d, own_d = pair_sum("in", ("w_in",), _comm_call("pair_exchange_in", pair(("w_in",))))
    early = ("w_down", "w_gate_up") + mixers
    halves = _chip_sum("chip_sum_early", [spec[n] for n in early], slots_a + slots_b + slots_c, core1)
    dh1, (own_d, *reduced) = _mm_nt(
        "mm_dh1", dproj, whole["w_in"], 1024, 1024, 1664, F32,
        comm=_Both(_ChipExchange(sums_d, own_d, 0, 2), _HalfExchange(halves, [spec[n] for n in early])))
    g = dict(zip(early, reduced))
    (grad_x, dg_mix), slots_d = _rms_norm_bwd("norm_mix_bwd", dh1, xs, w["g_mix"], dx2, False,
                                              comm=_ChipExchange(sums_d, [own_d], 1, 2))
    small = _pack_small(dg_mix, dg_ffn, dg_final, dconv_w, dsinks, loss_row)
    half_in = _chip_sum("chip_sum_in", [spec["w_in"]], slots_d, core1)
    g["w_in"], small_blocks = _comm_call(
        "half_exchange_in", _Both(_HalfExchange(half_in, [spec["w_in"]]), _SmallAllToAll(small)))
    delta, new_m, new_v = {}, {}, {}

    def keep(names, triples):
        for n, (d, nm, nv) in zip(names, triples):
            delta[n], new_m[n], new_v[n] = d, nm, nv

    keep(early, _adamw("adamw_early", [(w[n], g[n], m[n], v[n]) for n in early], 8))
    small_sum = _small_sum(small_blocks)
    g["g_mix"] = small_sum[0:1, :]
    g["g_ffn"] = small_sum[1:2, :]
    g["g_final"] = small_sum[2:3, :]
    g["conv_w"] = lax.dynamic_slice(small_sum, (3, shard * CONV_W_COLS), (3, CONV_W_COLS))
    g["attn_sinks"] = small_sum[6:7, :N_HEADS]
    loss = small_sum[7, 0]
    keep(("w_in",), _adamw("adamw_w_in", [(w["w_in"], g["w_in"], m["w_in"], v["w_in"])], 4))
    rest = ("g_mix", "g_ffn", "g_final", "conv_w", "attn_sinks")
    keep(rest, _adamw("adamw_small", [(w[n], g[n], m[n], v[n]) for n in rest], 1))

    def shaped(vals):
        return [vals[n].reshape((D_MODEL,)) if n == "g_final" else
                (vals[n][None] if n in MATRIX_NAMES or n == "conv_w" else vals[n]) for n in WEIGHT_ORDER]

    return (loss, grad_x[None], *shaped(g), *shaped(delta), *shaped(new_m), *shaped(new_v))
```

```python
import functools
import math

import jax
import jax.numpy as jnp
from jax import lax
from jax.experimental import pallas as pl
from jax.experimental.pallas import tpu as pltpu

F32 = jnp.float32
BF16 = jnp.bfloat16

D_MODEL = 1024
SEQ = 2048
HEAD_DIM = 64
N_HEADS = 16
N_KV_HEADS = 4
GROUP = N_HEADS // N_KV_HEADS
D_ATTN = N_HEADS * HEAD_DIM
D_KV = N_KV_HEADS * HEAD_DIM
BLOCK = 128
ROT_DIM = HEAD_DIM // 4
ROPE_THETA = 500000.0
ATTN_SCALE = 1.0 / math.sqrt(HEAD_DIM)
NEG_INF = -1e30
D_FF = 2816
EPS = 1e-5
N_IN = 3 * D_MODEL + D_ATTN + 2 * D_KV + 2 * D_MODEL
COL_Q = 3 * D_MODEL
COL_K = COL_Q + D_ATTN
COL_V = COL_K + D_KV
COL_GC = COL_V + D_KV
COL_GA = COL_GC + D_MODEL

ADAM_LR = 0.001
ADAM_B1 = 0.9
ADAM_B2 = 0.999
ADAM_EPS = 1e-08
ADAM_WD = 0.01
ADAM_STEP = 10

N_CHIPS = 4
N_DEV = 8

V7X_VMEM_BYTES = 64 * 1024 * 1024
VMEM_LIMIT = (V7X_VMEM_BYTES * 3) // 4
LANES = 128
MESH = pl.DeviceIdType.MESH


def _params(semantics=None):
    return pltpu.CompilerParams(dimension_semantics=semantics, vmem_limit_bytes=VMEM_LIMIT)


def _sds(shape, dtype):
    return jax.ShapeDtypeStruct(shape, dtype)


HBM_SPEC = pl.BlockSpec(memory_space=pl.ANY)


def _pcall(body, name, grid, in_specs, out_specs, out_shape, operands, scratch=(), semantics=None, comm=None,
           aliases=None):
    aliases = dict(aliases or {})
    if comm is None:
        return pl.pallas_call(
            body, name=name, grid=grid, in_specs=in_specs, out_specs=out_specs, out_shape=out_shape,
            scratch_shapes=list(scratch), input_output_aliases=aliases,
            compiler_params=_params(semantics))(*operands)
    multi = isinstance(out_shape, (list, tuple))
    o_specs = list(out_specs) if multi else [out_specs]
    o_shape = list(out_shape) if multi else [out_shape]
    n_in, n_out, n_scr = len(operands), len(o_shape), len(scratch)
    n_cin, n_cout = len(comm.operands), len(comm.out_shape)

    def hosted(*refs):
        ins, cins = refs[:n_in], refs[n_in:n_in + n_cin]
        o0 = n_in + n_cin
        outs, couts = refs[o0:o0 + n_out], refs[o0 + n_out:o0 + n_out + n_cout]
        s0 = o0 + n_out + n_cout
        scr, sems = refs[s0:s0 + n_scr], refs[s0 + n_scr:]
        first = last = None
        for axis, size in enumerate(grid):
            i = pl.program_id(axis)
            first = (i == 0) if first is None else first & (i == 0)
            last = (i == size - 1) if last is None else last & (i == size - 1)

        body(*ins, *outs, *scr)

        @pl.when(first)
        def _():
            comm.start(cins, couts, sems)

        @pl.when(last)
        def _():
            comm.finish(cins, couts, sems)

    res = pl.pallas_call(
        hosted, name=name, grid=grid,
        in_specs=list(in_specs) + [HBM_SPEC] * n_cin, out_specs=o_specs + [HBM_SPEC] * n_cout,
        out_shape=o_shape + list(comm.out_shape), scratch_shapes=list(scratch) + list(comm.sems),
        input_output_aliases={**aliases, **{n_in + a: n_out + b for a, b in comm.aliases.items()}},
        compiler_params=_params(("arbitrary",) * len(grid)))(*operands, *comm.operands)
    outs = list(res[:n_out])
    return (outs if multi else outs[0]), list(res[n_out:])


def _comm_call(name, comm):
    def body(*refs):
        n_cin, n_cout = len(comm.operands), len(comm.out_shape)
        cins, couts, sems = refs[:n_cin], refs[n_cin:n_cin + n_cout], refs[n_cin + n_cout:]
        comm.start(cins, couts, sems)
        comm.finish(cins, couts, sems)

    return list(pl.pallas_call(
        body, name=name, in_specs=[HBM_SPEC] * len(comm.operands), out_specs=[HBM_SPEC] * len(comm.out_shape),
        out_shape=list(comm.out_shape), scratch_shapes=list(comm.sems),
        input_output_aliases=dict(comm.aliases))(*comm.operands))


NN = ((1,), (0,))
NT = ((1,), (1,))
TN = ((0,), (0,))


def _matmul(name, a, b, dims, grid, a_spec, b_spec, o_spec, o_shape, o_dtype, res=None, res_spec=None, comm=None):
    nk = grid[2]

    def body(*refs):
        if res is None:
            a_ref, b_ref, o_ref = refs[:3]
            r_ref = None
            scratch = refs[3:]
        else:
            a_ref, b_ref, r_ref, o_ref = refs[:4]
            scratch = refs[4:]
        p = lax.dot_general(a_ref[...], b_ref[...], (dims, ((), ())), preferred_element_type=F32)

        def finish(acc):
            if r_ref is not None:
                acc = r_ref[...] + acc
            o_ref[...] = acc.astype(o_dtype)

        if nk == 1:
            finish(p)
        else:
            acc_ref = scratch[0]
            k = pl.program_id(2)

            @pl.when(k == 0)
            def _():
                acc_ref[...] = p

            @pl.when(k > 0)
            def _():
                acc_ref[...] += p

            @pl.when(k == nk - 1)
            def _():
                finish(acc_ref[...])

    operands = [a, b] if res is None else [a, b, res]
    in_specs = [a_spec, b_spec] if res is None else [a_spec, b_spec, res_spec]
    scratch = [pltpu.VMEM(o_spec.block_shape, F32)] if nk > 1 else []
    return _pcall(body, name, grid, in_specs, o_spec, _sds(o_shape, o_dtype), operands, scratch,
                  ("parallel", "parallel", "arbitrary"), comm)


def _mm_nn(name, a, b, bm, bn, o_dtype, res=None, comm=None):
    m, k = a.shape
    n = b.shape[1]
    return _matmul(
        name, a, b, NN, (m // bm, n // bn, 1),
        pl.BlockSpec((bm, k), lambda i, j, kk: (i, 0)),
        pl.BlockSpec((k, bn), lambda i, j, kk: (0, j)),
        pl.BlockSpec((bm, bn), lambda i, j, kk: (i, j)),
        (m, n), o_dtype, res,
        None if res is None else pl.BlockSpec((bm, bn), lambda i, j, kk: (i, j)), comm,
    )


def _mm_nt(name, a, b, bm, bn, bk, o_dtype, comm=None):
    m, k = a.shape
    n = b.shape[0]
    return _matmul(
        name, a, b, NT, (m // bm, n // bn, k // bk),
        pl.BlockSpec((bm, bk), lambda i, j, kk: (i, kk)),
        pl.BlockSpec((bn, bk), lambda i, j, kk: (j, kk)),
        pl.BlockSpec((bm, bn), lambda i, j, kk: (i, j)),
        (m, n), o_dtype, comm=comm,
    )


def _mm_tn(name, a, b, bm, bn, o_dtype, comm=None):
    k, m = a.shape
    n = b.shape[1]
    return _matmul(
        name, a, b, TN, (m // bm, n // bn, 1),
        pl.BlockSpec((k, bm), lambda i, j, kk: (0, i)),
        pl.BlockSpec((k, bn), lambda i, j, kk: (0, j)),
        pl.BlockSpec((bm, bn), lambda i, j, kk: (i, j)),
        (m, n), o_dtype, comm=comm,
    )


ROWS = 256


def _row_spec(width, col=0):
    return pl.BlockSpec((ROWS, width), lambda i: (i, col))


def _full_spec(shape):
    return pl.BlockSpec(shape, lambda *_: (0,) * len(shape))


def _rms_norm(name, x, g):
    def body(x_ref, g_ref, h_ref):
        xf = x_ref[...]
        r = lax.rsqrt(jnp.mean(xf * xf, axis=-1, keepdims=True) + EPS)
        h_ref[...] = ((xf * r) * g_ref[...]).astype(BF16)

    return pl.pallas_call(
        body, name=name, grid=(SEQ // ROWS,),
        in_specs=[_row_spec(D_MODEL), _full_spec((1, D_MODEL))],
        out_specs=_row_spec(D_MODEL),
        out_shape=_sds((SEQ, D_MODEL), BF16),
        compiler_params=_params(("parallel",)),
    )(x, g)


CONV_COLS = 256


def _shift_rows(u, k):
    rows = lax.broadcasted_iota(jnp.int32, u.shape, 0)
    return jnp.where(rows >= k, pltpu.roll(u, k, axis=0), 0.0)


def _conv_fwd(proj, conv_w):
    nblk = D_MODEL // CONV_COLS

    def body(cb_ref, cc_ref, cx_ref, w_ref, y_ref):
        u = cc_ref[...] * cx_ref[...]
        w = w_ref[...]
        cv = w[0:1, :] * _shift_rows(u, 2) + w[1:2, :] * _shift_rows(u, 1) + w[2:3, :] * u
        y_ref[...] = (cb_ref[...] * cv).astype(BF16)

    def col(part):
        return pl.BlockSpec((SEQ, CONV_COLS), lambda j: (0, part * nblk + j))

    return pl.pallas_call(
        body, name="conv_fwd", grid=(nblk,),
        in_specs=[col(0), col(1), col(2), pl.BlockSpec((3, CONV_COLS), lambda j: (0, j))],
        out_specs=pl.BlockSpec((SEQ, CONV_COLS), lambda j: (0, j)),
        out_shape=_sds((SEQ, D_MODEL), BF16),
        compiler_params=_params(("parallel",)),
    )(proj, proj, proj, conv_w)


ROPE_COLS = 256


def _rope_tables():
    inv_freq = ROPE_THETA ** (-jnp.arange(0, ROT_DIM, 2, dtype=F32) / ROT_DIM)
    ang = jnp.arange(SEQ, dtype=F32)[:, None] * inv_freq[None, :]
    cos, sin = jnp.cos(ang), jnp.sin(ang)
    half = ROT_DIM // 2
    ones = jnp.ones((SEQ, HEAD_DIM - ROT_DIM), F32)
    zeros = jnp.zeros((SEQ, HEAD_DIM - ROT_DIM), F32)
    zh = jnp.zeros((SEQ, half), F32)
    c = jnp.concatenate([cos, cos, ones], axis=1)
    s_up = jnp.concatenate([-sin, zh, zeros], axis=1)
    s_dn = jnp.concatenate([zh, sin, zeros], axis=1)
    reps = ROPE_COLS // HEAD_DIM
    return tuple(jnp.tile(t, (1, reps)) for t in (c, s_up, s_dn))


def _rotate(t, c, s_up, s_dn):
    width = t.shape[1]
    half = ROT_DIM // 2
    return t * c + pltpu.roll(t, width - half, axis=1) * s_up + pltpu.roll(t, half, axis=1) * s_dn


N_QBLK = SEQ // BLOCK


def _attn_specs():
    prev = lambda n: jnp.maximum(n - 1, 0)
    q = pl.BlockSpec((BLOCK, D_ATTN), lambda n: (n, COL_Q // D_ATTN))
    k_prev = pl.BlockSpec((BLOCK, D_KV), lambda n: (prev(n), COL_K // D_KV))
    k_cur = pl.BlockSpec((BLOCK, D_KV), lambda n: (n, COL_K // D_KV))
    v_prev = pl.BlockSpec((BLOCK, D_KV), lambda n: (prev(n), COL_V // D_KV))
    v_cur = pl.BlockSpec((BLOCK, D_KV), lambda n: (n, COL_V // D_KV))
    tab_cur = pl.BlockSpec((BLOCK, ROPE_COLS), lambda n: (n, 0))
    tab_prev = pl.BlockSpec((BLOCK, ROPE_COLS), lambda n: (prev(n), 0))
    return [q, k_prev, k_cur, v_prev, v_cur] + [tab_cur] * 3 + [tab_prev] * 3


def _band_kv(kp_ref, kc_ref, vp_ref, vc_ref, tabs_cur, tabs_prev):
    k = jnp.concatenate([_rotate(kp_ref[...], *(t[...] for t in tabs_prev)),
                         _rotate(kc_ref[...], *(t[...] for t in tabs_cur))], axis=0)
    v = jnp.concatenate([vp_ref[...], vc_ref[...]], axis=0)
    return k, v


def _query_tile(q_ref, tile, tabs_cur):
    c, su, sd = (t[:, :LANES] for t in tabs_cur)
    return _rotate(q_ref[:, tile * LANES:(tile + 1) * LANES], c, su, sd).astype(BF16)


def _band_mask(n):
    kj = lax.broadcasted_iota(jnp.int32, (2 * BLOCK, BLOCK), 0)
    qi = lax.broadcasted_iota(jnp.int32, (2 * BLOCK, BLOCK), 1)
    rel = qi + BLOCK - kj
    return (rel >= 0) & (rel < BLOCK) & ((kj >= BLOCK) | (n > 0))


HEADS_PER_TILE = LANES // HEAD_DIM


def _lane_half(shape, par):
    lane = lax.broadcasted_iota(jnp.int32, shape, 1)
    return (lane < HEAD_DIM) if par == 0 else (lane >= HEAD_DIM)


def _head_tiles(kv, h):
    tile = kv[:, (h // HEADS_PER_TILE) * LANES:(h // HEADS_PER_TILE + 1) * LANES].astype(F32)
    own = jnp.where(_lane_half(tile.shape, h % HEADS_PER_TILE), tile, 0.0)
    other = pltpu.roll(own, HEAD_DIM, axis=1)
    lo, hi = (own, other) if h % HEADS_PER_TILE == 0 else (other, own)
    return lo.astype(BF16), hi.astype(BF16)


def _head_softmax(q_tile, k_half, sink, mask):
    s = lax.dot_general(k_half, q_tile, (NT, ((), ())), preferred_element_type=F32) * ATTN_SCALE
    s = jnp.where(mask, s, NEG_INF)
    m = jnp.maximum(jnp.max(s, axis=0, keepdims=True), sink)
    e = jnp.exp(s - m)
    es = jnp.exp(sink - m)
    inv = 1.0 / (jnp.sum(e, axis=0, keepdims=True) + es)
    return e * inv, es * inv


def _attn_fwd(proj, tables, sinks, comm=None):
    def body(sink_ref, q_ref, kp_ref, kc_ref, vp_ref, vc_ref, c_ref, su_ref, sd_ref, cp_ref, sup_ref, sdp_ref, o_ref):
        n = pl.program_id(0)
        mask = _band_mask(n)
        tabs_cur = (c_ref, su_ref, sd_ref)
        k, v = _band_kv(kp_ref, kc_ref, vp_ref, vc_ref, tabs_cur, (cp_ref, sup_ref, sdp_ref))
        for h in range(N_KV_HEADS):
            k_halves = _head_tiles(k, h)
            v_halves = _head_tiles(v, h)
            for t in range(GROUP // HEADS_PER_TILE):
                tile = h * (GROUP // HEADS_PER_TILE) + t
                q_tile = _query_tile(q_ref, tile, tabs_cur)
                acc = None
                for par in range(HEADS_PER_TILE):
                    sink = sink_ref[0, tile * HEADS_PER_TILE + par]
                    p, _ = _head_softmax(q_tile, k_halves[par], sink, mask)
                    o = lax.dot_general(p.astype(BF16), v_halves[par], (TN, ((), ())), preferred_element_type=F32)
                    acc = o if acc is None else acc + o
                o_ref[:, tile * LANES:(tile + 1) * LANES] = acc.astype(BF16)

    return _pcall(
        body, "attn_fwd", (N_QBLK,),
        [pl.BlockSpec(memory_space=pltpu.SMEM)] + _attn_specs(),
        pl.BlockSpec((BLOCK, D_ATTN), lambda n: (n, 0)),
        _sds((SEQ, D_ATTN), BF16), [sinks] + [proj] * 5 + list(tables) * 2, (), ("parallel",), comm)


def _branch_merge(conv_y, attn, w_co, w_ao, proj, comm=None):
    bm, bn = 1024, 512

    def body(cy_ref, at_ref, wc_ref, wa_ref, gc_ref, ga_ref, co_ref, ao_ref, mg_ref):
        co = jnp.dot(cy_ref[...], wc_ref[...], preferred_element_type=F32)
        ao = jnp.dot(at_ref[...], wa_ref[...], preferred_element_type=F32)
        co_ref[...] = co
        ao_ref[...] = ao
        mg_ref[...] = (jax.nn.sigmoid(gc_ref[...]) * co + jax.nn.sigmoid(ga_ref[...]) * ao).astype(BF16)

    act = pl.BlockSpec((bm, D_MODEL), lambda i, j: (i, 0))
    wgt = pl.BlockSpec((D_MODEL, bn), lambda i, j: (0, j))
    out = pl.BlockSpec((bm, bn), lambda i, j: (i, j))
    return _pcall(
        body, "branch_merge", (SEQ // bm, D_MODEL // bn),
        [act, act, wgt, wgt,
         pl.BlockSpec((bm, bn), lambda i, j: (i, COL_GC // bn + j)),
         pl.BlockSpec((bm, bn), lambda i, j: (i, COL_GA // bn + j))],
        [out, out, out],
        [_sds((SEQ, D_MODEL), F32), _sds((SEQ, D_MODEL), F32), _sds((SEQ, D_MODEL), BF16)],
        [conv_y, attn, w_co, w_ao, proj, proj], (), ("parallel", "parallel"), comm)


FF_BM, FF_BN = 512, 1408
FF_NB = D_FF // FF_BN


def _gate_up_fwd(h2, w_gu, comm=None):
    def body(h_ref, wg_ref, wu_ref, g_ref, u_ref, a_ref):
        h = h_ref[...]
        g = jnp.dot(h, wg_ref[...], preferred_element_type=F32)
        u = jnp.dot(h, wu_ref[...], preferred_element_type=F32)
        g_ref[...] = g
        u_ref[...] = u
        a_ref[...] = (jax.nn.silu(g) * u).astype(BF16)

    out = pl.BlockSpec((FF_BM, FF_BN), lambda j, i: (i, j))
    f32, b16 = _sds((SEQ, D_FF), F32), _sds((SEQ, D_FF), BF16)
    return _pcall(
        body, "mm_gate_up", (FF_NB, SEQ // FF_BM),
        [pl.BlockSpec((FF_BM, D_MODEL), lambda j, i: (i, 0)),
         pl.BlockSpec((D_MODEL, FF_BN), lambda j, i: (0, j)),
         pl.BlockSpec((D_MODEL, FF_BN), lambda j, i: (0, FF_NB + j))],
        [out, out, out], [f32, f32, b16], [h2, w_gu, w_gu], (), ("parallel", "parallel"), comm)


def _dact_swiglu(dx3b, w_down, g, u, comm=None):
    def body(dx_ref, w_ref, g_ref, u_ref, dg_ref, du_ref):
        da = lax.dot_general(dx_ref[...], w_ref[...], (NT, ((), ())), preferred_element_type=F32)
        g = g_ref[...]
        sg = jax.nn.sigmoid(g)
        dg_ref[...] = (da * u_ref[...] * (sg * (1.0 + g * (1.0 - sg)))).astype(BF16)
        du_ref[...] = (da * (g * sg)).astype(BF16)

    blk = pl.BlockSpec((FF_BM, FF_BN), lambda j, i: (i, j))
    b16 = _sds((SEQ, D_FF), BF16)
    return _pcall(
        body, "mm_dact", (FF_NB, SEQ // FF_BM),
        [pl.BlockSpec((FF_BM, D_MODEL), lambda j, i: (i, 0)), pl.BlockSpec((FF_BN, D_MODEL), lambda j, i: (j, 0)),
         blk, blk],
        [blk, blk], [b16, b16], [dx3b, w_down, g, u], (), ("parallel", "parallel"), comm)


def _mm_dh2(dg, du, w_gu, comm=None):
    bm = 1024
    nk = 2 * FF_NB

    def body(dg_ref, du_ref, w_ref, o_ref, acc_ref):
        k = pl.program_id(1)

        def part(a_ref):
            return lax.dot_general(a_ref[...], w_ref[...], (NT, ((), ())), preferred_element_type=F32)

        @pl.when(k == 0)
        def _():
            acc_ref[...] = part(dg_ref)

        @pl.when((k > 0) & (k < FF_NB))
        def _():
            acc_ref[...] += part(dg_ref)

        @pl.when(k >= FF_NB)
        def _():
            acc_ref[...] += part(du_ref)

        @pl.when(k == nk - 1)
        def _():
            o_ref[...] = acc_ref[...]

    return _pcall(
        body, "mm_dh2", (SEQ // bm, nk),
        [pl.BlockSpec((bm, FF_BN), lambda i, k: (i, jnp.minimum(k, FF_NB - 1))),
         pl.BlockSpec((bm, FF_BN), lambda i, k: (i, jnp.maximum(k - FF_NB, 0))),
         pl.BlockSpec((D_MODEL, FF_BN), lambda i, k: (0, k))],
        pl.BlockSpec((bm, D_MODEL), lambda i, k: (i, 0)), _sds((SEQ, D_MODEL), F32),
        [dg, du, w_gu], [pltpu.VMEM((bm, D_MODEL), F32)], ("parallel", "arbitrary"), comm)


def _mm_dw_gate_up(h2, dg, du):
    def body(h_ref, dg_ref, du_ref, o_ref):
        j = pl.program_id(0)

        def part(b_ref):
            return lax.dot_general(h_ref[...], b_ref[...], (TN, ((), ())), preferred_element_type=F32).astype(BF16)

        @pl.when(j < FF_NB)
        def _():
            o_ref[...] = part(dg_ref)

        @pl.when(j >= FF_NB)
        def _():
            o_ref[...] = part(du_ref)

    return pl.pallas_call(
        body, name="mm_dw_gate_up", grid=(2 * FF_NB,),
        in_specs=[_full_spec((SEQ, D_MODEL)),
                  pl.BlockSpec((SEQ, FF_BN), lambda j: (0, jnp.minimum(j, FF_NB - 1))),
                  pl.BlockSpec((SEQ, FF_BN), lambda j: (0, jnp.maximum(j - FF_NB, 0)))],
        out_specs=pl.BlockSpec((D_MODEL, FF_BN), lambda j: (0, j)),
        out_shape=_sds((D_MODEL, 2 * D_FF), BF16),
        compiler_params=_params(("arbitrary",)),
    )(h2, dg, du)


def _loss_head(x3, g, target):
    def body(x_ref, g_ref, t_ref, dx_ref, dxb_ref, dg_ref, loss_ref):
        i = pl.program_id(0)
        xf = x_ref[...]
        r = lax.rsqrt(jnp.mean(xf * xf, axis=-1, keepdims=True) + EPS)
        xn = xf * r
        gg = g_ref[...]
        err = xn * gg - t_ref[...]
        part = 0.5 * jnp.sum(jnp.mean(err * err, axis=-1, keepdims=True), axis=0, keepdims=True)
        dy = err * (1.0 / D_MODEL)
        dxn = dy * gg
        dx = r * (dxn - xn * jnp.mean(dxn * xn, axis=-1, keepdims=True))
        dx_ref[...] = dx
        dxb_ref[...] = dx.astype(BF16)
        dg = jnp.sum(dy * xn, axis=0, keepdims=True)
        lane0 = lax.broadcasted_iota(jnp.int32, (1, LANES), 1) == 0
        lpart = jnp.where(lane0, part, 0.0)

        @pl.when(i == 0)
        def _():
            dg_ref[...] = dg
            loss_ref[...] = lpart

        @pl.when(i > 0)
        def _():
            dg_ref[...] += dg
            loss_ref[...] += lpart

    return pl.pallas_call(
        body, name="loss_head", grid=(SEQ // ROWS,),
        in_specs=[_row_spec(D_MODEL), _full_spec((1, D_MODEL)), _row_spec(D_MODEL)],
        out_specs=[_row_spec(D_MODEL), _row_spec(D_MODEL), _full_spec((1, D_MODEL)), _full_spec((1, LANES))],
        out_shape=[_sds((SEQ, D_MODEL), F32), _sds((SEQ, D_MODEL), BF16),
                   _sds((1, D_MODEL), F32), _sds((1, LANES), F32)],
        compiler_params=_params(("arbitrary",)),
    )(x3, g, target)


def _rms_norm_bwd(name, dh, x, g, dres, with_bf16, comm=None):
    def body(dh_ref, x_ref, g_ref, dr_ref, *outs):
        i = pl.program_id(0)
        dx_ref = outs[0]
        dg_ref = outs[-1]
        xf = x_ref[...]
        r = lax.rsqrt(jnp.mean(xf * xf, axis=-1, keepdims=True) + EPS)
        xn = xf * r
        dh = dh_ref[...]
        dxn = dh * g_ref[...]
        dx = dr_ref[...] + r * (dxn - xn * jnp.mean(dxn * xn, axis=-1, keepdims=True))
        dx_ref[...] = dx
        if with_bf16:
            outs[1][...] = dx.astype(BF16)
        dg = jnp.sum(dh * xn, axis=0, keepdims=True)

        @pl.when(i == 0)
        def _():
            dg_ref[...] = dg

        @pl.when(i > 0)
        def _():
            dg_ref[...] += dg

    row = _row_spec(D_MODEL)
    out_specs = [row] + ([row] if with_bf16 else []) + [_full_spec((1, D_MODEL))]
    out_shape = ([_sds((SEQ, D_MODEL), F32)] + ([_sds((SEQ, D_MODEL), BF16)] if with_bf16 else [])
                 + [_sds((1, D_MODEL), F32)])
    return _pcall(body, name, (SEQ // ROWS,), [row, row, _full_spec((1, D_MODEL)), row], out_specs, out_shape,
                  [dh, x, g, dres], (), ("arbitrary",), comm)


def _merge_bwd(dx2b, w_o, conv_out, attn_out, proj):
    bm, bn = 1024, D_MODEL // 2

    def body(dx_ref, w_ref, co_ref, ao_ref, gc_ref, ga_ref, dco_ref, dao_ref, dgc_ref, dga_ref):
        dm = lax.dot_general(dx_ref[...], w_ref[...], (NT, ((), ())), preferred_element_type=F32)
        sc = jax.nn.sigmoid(gc_ref[...])
        sa = jax.nn.sigmoid(ga_ref[...])
        dco_ref[...] = (dm * sc).astype(BF16)
        dao_ref[...] = (dm * sa).astype(BF16)
        dgc_ref[...] = (dm * co_ref[...] * (sc * (1.0 - sc))).astype(BF16)
        dga_ref[...] = (dm * ao_ref[...] * (sa * (1.0 - sa))).astype(BF16)

    own = pl.BlockSpec((bm, bn), lambda i, j: (i, j))
    sd = _sds((SEQ, D_MODEL), BF16)
    return pl.pallas_call(
        body, name="mm_dmerged", grid=(SEQ // bm, D_MODEL // bn),
        in_specs=[pl.BlockSpec((bm, D_MODEL), lambda i, j: (i, 0)), pl.BlockSpec((bn, D_MODEL), lambda i, j: (j, 0)),
                  own, own,
                  pl.BlockSpec((bm, bn), lambda i, j: (i, COL_GC // bn + j)),
                  pl.BlockSpec((bm, bn), lambda i, j: (i, COL_GA // bn + j))],
        out_specs=[own, own, own, own], out_shape=[sd, sd, sd, sd],
        compiler_params=_params(("parallel", "parallel")),
    )(dx2b, w_o, conv_out, attn_out, proj, proj)


def _conv_bwd(dconv_y, proj, conv_w, comm=None):
    nblk = D_MODEL // CONV_COLS

    def body(dy_ref, cb_ref, cc_ref, cx_ref, w_ref, dcb_ref, dcc_ref, dcx_ref, dw_ref):
        cc = cc_ref[...]
        cx = cx_ref[...]
        u = cc * cx
        w = w_ref[...]
        u1 = _shift_rows(u, 1)
        u2 = _shift_rows(u, 2)
        cv = w[0:1, :] * u2 + w[1:2, :] * u1 + w[2:3, :] * u
        dy = dy_ref[...]
        dcb_ref[...] = (dy * cv).astype(BF16)
        dcv = dy * cb_ref[...]
        rows = lax.broadcasted_iota(jnp.int32, dcv.shape, 0)
        up1 = jnp.where(rows < SEQ - 1, pltpu.roll(dcv, SEQ - 1, axis=0), 0.0)
        up2 = jnp.where(rows < SEQ - 2, pltpu.roll(dcv, SEQ - 2, axis=0), 0.0)
        du = w[2:3, :] * dcv + w[1:2, :] * up1 + w[0:1, :] * up2
        dcc_ref[...] = (du * cx).astype(BF16)
        dcx_ref[...] = (du * cc).astype(BF16)
        dw_ref[...] = jnp.concatenate(
            [jnp.sum(dcv * u2, axis=0, keepdims=True),
             jnp.sum(dcv * u1, axis=0, keepdims=True),
             jnp.sum(dcv * u, axis=0, keepdims=True)], axis=0)

    def col(part):
        return pl.BlockSpec((SEQ, CONV_COLS), lambda j: (0, part * nblk + j))

    own = pl.BlockSpec((SEQ, CONV_COLS), lambda j: (0, j))
    wsp = pl.BlockSpec((3, CONV_COLS), lambda j: (0, j))
    sd = _sds((SEQ, D_MODEL), BF16)
    return _pcall(
        body, "conv_bwd", (nblk,), [own, col(0), col(1), col(2), wsp], [own, own, own, wsp],
        [sd, sd, sd, _sds((3, D_MODEL), F32)], [dconv_y, proj, proj, proj, conv_w], (), ("parallel",), comm)


def _attn_bwd(proj, dattn, sinks, tables, comm=None):
    def body(sink_ref, q_ref, kp_ref, kc_ref, vp_ref, vc_ref, c_ref, su_ref, sd_ref, cp_ref, sup_ref, sdp_ref,
             do_ref, dq_ref, dkp_ref, dkc_ref, dvp_ref, dvc_ref, ds_ref):
        n = pl.program_id(0)
        mask = _band_mask(n)
        tabs_cur = (c_ref, su_ref, sd_ref)
        k, v = _band_kv(kp_ref, kc_ref, vp_ref, vc_ref, tabs_cur, (cp_ref, sup_ref, sdp_ref))
        lane = lax.broadcasted_iota(jnp.int32, (1, LANES), 1)
        dsink = jnp.zeros((1, LANES), F32)
        c, su, sd = c_ref[:, :LANES], su_ref[:, :LANES], sd_ref[:, :LANES]
        dk_tiles = [None] * (N_KV_HEADS // HEADS_PER_TILE)
        dv_tiles = [None] * (N_KV_HEADS // HEADS_PER_TILE)
        for h in range(N_KV_HEADS):
            k_halves = _head_tiles(k, h)
            v_halves = _head_tiles(v, h)
            dk_par = [None] * HEADS_PER_TILE
            dv_par = [None] * HEADS_PER_TILE
            for t in range(GROUP // HEADS_PER_TILE):
                tile = h * (GROUP // HEADS_PER_TILE) + t
                q_tile = _query_tile(q_ref, tile, tabs_cur)
                do_tile = do_ref[:, tile * LANES:(tile + 1) * LANES]
                dq_tile = None
                for par in range(HEADS_PER_TILE):
                    head = tile * HEADS_PER_TILE + par
                    p, p_sink = _head_softmax(q_tile, k_halves[par], sink_ref[0, head], mask)
                    dp = lax.dot_general(v_halves[par], do_tile, (NT, ((), ())), preferred_element_type=F32)
                    delta = jnp.sum(p * dp, axis=0, keepdims=True)
                    ds = (p * (dp - delta) * ATTN_SCALE).astype(BF16)
                    dq = lax.dot_general(ds, k_halves[par], (TN, ((), ())), preferred_element_type=F32)
                    dq_tile = dq if dq_tile is None else dq_tile + dq
                    dk = jnp.dot(ds, q_tile, preferred_element_type=F32)
                    dv = jnp.dot(p.astype(BF16), do_tile, preferred_element_type=F32)
                    dk_par[par] = dk if dk_par[par] is None else dk_par[par] + dk
                    dv_par[par] = dv if dv_par[par] is None else dv_par[par] + dv
                    val = -jnp.sum(p_sink * delta, axis=1, keepdims=True)
                    dsink = dsink + jnp.where(lane == head, val, 0.0)
                dq_ref[:, tile * LANES:(tile + 1) * LANES] = _rotate(dq_tile, c, -su, -sd).astype(BF16)
            own = h % HEADS_PER_TILE
            for par_grads, tiles in ((dk_par, dk_tiles), (dv_par, dv_tiles)):
                shifted = pltpu.roll(par_grads[1 - own], HEAD_DIM, axis=1)
                total = jnp.where(_lane_half(shifted.shape, own), par_grads[own] + shifted, 0.0)
                i = h // HEADS_PER_TILE
                tiles[i] = total if tiles[i] is None else tiles[i] + total
        for i in range(N_KV_HEADS // HEADS_PER_TILE):
            cols = slice(i * LANES, (i + 1) * LANES)
            dkp_ref[:, cols] = dk_tiles[i][:BLOCK, :]
            dkc_ref[:, cols] = dk_tiles[i][BLOCK:, :]
            dvp_ref[:, cols] = dv_tiles[i][:BLOCK, :]
            dvc_ref[:, cols] = dv_tiles[i][BLOCK:, :]

        @pl.when(n == 0)
        def _():
            ds_ref[...] = dsink

        @pl.when(n > 0)
        def _():
            ds_ref[...] += dsink

    blk = pl.BlockSpec((BLOCK, D_KV), lambda n: (n, 0))
    kv = _sds((SEQ, D_KV), F32)
    return _pcall(
        body, "attn_bwd", (N_QBLK,),
        [pl.BlockSpec(memory_space=pltpu.SMEM)] + _attn_specs() + [pl.BlockSpec((BLOCK, D_ATTN), lambda n: (n, 0))],
        [pl.BlockSpec((BLOCK, D_ATTN), lambda n: (n, 0)), blk, blk, blk, blk, _full_spec((1, LANES))],
        [_sds((SEQ, D_ATTN), BF16), kv, kv, kv, kv, _sds((1, LANES), F32)],
        [sinks] + [proj] * 5 + list(tables) * 2 + [dattn], (), ("arbitrary",), comm)


def _kv_grad_combine(dk_prev, dk_cur, dv_prev, dv_cur, tables):
    def body(kp_ref, kc_ref, vp_ref, vc_ref, c_ref, su_ref, sd_ref, o_ref):
        m = pl.program_id(0)
        has_next = m < N_QBLK - 1
        dk = kc_ref[...] + jnp.where(has_next, kp_ref[...], 0.0)
        dv = vc_ref[...] + jnp.where(has_next, vp_ref[...], 0.0)
        o_ref[:, :D_KV] = _rotate(dk, c_ref[...], -su_ref[...], -sd_ref[...]).astype(BF16)
        o_ref[:, D_KV:] = dv.astype(BF16)

    cur = pl.BlockSpec((BLOCK, D_KV), lambda m: (m, 0))
    nxt = pl.BlockSpec((BLOCK, D_KV), lambda m: (jnp.minimum(m + 1, N_QBLK - 1), 0))
    return pl.pallas_call(
        body, name="kv_grad_combine", grid=(N_QBLK,),
        in_specs=[nxt, cur, nxt, cur, cur, cur, cur],
        out_specs=pl.BlockSpec((BLOCK, 2 * D_KV), lambda m: (m, 0)),
        out_shape=_sds((SEQ, 2 * D_KV), BF16),
        compiler_params=_params(("parallel",)),
    )(dk_prev, dk_cur, dv_prev, dv_cur, *tables)


MATRICES = {
    "w_in": (D_MODEL, N_IN // N_CHIPS, "col"),
    "w_conv_out": (D_MODEL // N_CHIPS, D_MODEL, "row"),
    "w_attn_out": (D_MODEL // N_CHIPS, D_MODEL, "row"),
    "w_o": (D_MODEL // N_CHIPS, D_MODEL, "row"),
    "w_gate_up": (D_MODEL, 2 * D_FF // N_CHIPS, "col"),
    "w_down": (D_FF // N_CHIPS, D_MODEL, "row"),
}
BF16_ROW_TILE = 16
CONV_W_COLS = D_MODEL // N_CHIPS
SMALL_ROWS = 8


def _whole_shape(spec):
    rows, cols, kind = spec
    return (rows, cols * N_CHIPS) if kind == "col" else (rows * N_CHIPS, cols)


def _half_shape(spec):
    return (spec[0] // 2, spec[1])


def _aligned(start, multiple):
    return start if isinstance(start, int) else pl.multiple_of(start, multiple)


def _region(ref, spec, shard, half, part=0, parts=1):
    rows, cols, kind = spec
    hr = rows // 2
    n = hr // parts
    if kind == "col":
        return ref.at[pl.ds(_aligned(half * hr + part * n, BF16_ROW_TILE), n),
                      pl.ds(_aligned(shard * cols, LANES), cols)]
    return ref.at[pl.ds(_aligned(shard * rows + half * hr + part * n, BF16_ROW_TILE), n), :]


def _position():
    x, y, c = lax.axis_index("x"), lax.axis_index("y"), lax.axis_index("c")
    chips = [(1 - x, y), (x, 1 - y), (1 - x, 1 - y)]
    return x, y, c, chips


def _shard_of(chip):
    return 2 * chip[0] + chip[1]


def _remote(src, dst, send_sem, recv_sem, to):
    return pltpu.make_async_remote_copy(src_ref=src, dst_ref=dst, send_sem=send_sem, recv_sem=recv_sem,
                                        device_id=to, device_id_type=MESH)


def _to_bf16_in_whole(name, w, spec, shard, rows):
    steps = spec[0] // rows

    def body(s_ref, w_ref, o_ref):
        del s_ref
        o_ref[...] = w_ref[...].astype(BF16)

    if spec[2] == "col":
        out_spec = pl.BlockSpec((rows, spec[1]), lambda i, s_ref: (i, s_ref[0]))
    else:
        out_spec = pl.BlockSpec((rows, spec[1]), lambda i, s_ref: (s_ref[0] * steps + i, 0))
    grid_spec = pltpu.PrefetchScalarGridSpec(
        num_scalar_prefetch=1, grid=(steps,),
        in_specs=[pl.BlockSpec((rows, spec[1]), lambda i, s_ref: (i, 0))], out_specs=out_spec)
    return pl.pallas_call(
        body, name=name, grid_spec=grid_spec, out_shape=_sds(_whole_shape(spec), BF16),
        compiler_params=_params(("parallel",)),
    )(shard, w)


class _Gather:
    def __init__(self, wholes, pieces, conv_w=None):
        self.pieces = pieces
        self.n = len(wholes)
        self.with_conv_w = conv_w is not None
        self.operands = list(wholes) + ([conv_w] if self.with_conv_w else [])
        self.out_shape = [_sds(w.shape, w.dtype) for w in wholes]
        if self.with_conv_w:
            self.out_shape.append(_sds((3, D_MODEL), F32))
        self.aliases = {i: i for i in range(self.n)}
        n_ici = 3 * len(pieces)
        self.sems = [pltpu.SemaphoreType.DMA((n_ici,))] * 4
        if self.with_conv_w:
            self.sems += [pltpu.SemaphoreType.DMA((1,)), pltpu.SemaphoreType.DMA((3,)), pltpu.SemaphoreType.DMA((3,))]

    def _conv_w(self, cins, couts, sems, with_recvs):
        cw_in, cw_out = cins[self.n], couts[self.n]
        x, y, c, chips = _position()

        def cols(shard):
            return cw_out.at[:, pl.ds(_aligned(shard * CONV_W_COLS, LANES), CONV_W_COLS)]

        me = _shard_of((x, y))
        local = pltpu.make_async_copy(cw_in, cols(me), sems[4].at[0])
        sends = [_remote(cw_in, cols(me), sems[5].at[j], sems[6].at[j], (*chip, c)) for j, chip in enumerate(chips)]
        if not with_recvs:
            return local, sends, []
        recvs = [_remote(cols(_shard_of(chip)), cols(_shard_of(chip)), sems[5].at[j], sems[6].at[j], (*chip, c))
                 for j, chip in enumerate(chips)]
        return local, sends, recvs

    def start(self, cins, couts, sems):
        x, y, c, chips = _position()
        me = _shard_of((x, y))
        if self.with_conv_w:
            local, sends, _ = self._conv_w(cins, couts, sems, False)
            local.start()
            for cp in sends:
                cp.start()
        for p, (i, spec, part, parts) in enumerate(self.pieces):
            mine = _region(couts[i], spec, me, c, part, parts)
            for j, chip in enumerate(chips):
                _remote(mine, mine, sems[0].at[3 * p + j], sems[1].at[3 * p + j], (*chip, c)).start()

    def finish(self, cins, couts, sems):
        x, y, c, chips = _position()
        me = _shard_of((x, y))
        sibling = (x, y, 1 - c)
        send_a, recv_a, send_b, recv_b = sems[:4]
        passed = []
        for p, (i, spec, part, parts) in enumerate(self.pieces):
            for j, chip in enumerate(chips):
                k = 3 * p + j
                landed = _region(couts[i], spec, _shard_of(chip), c, part, parts)
                _remote(landed, landed, send_a.at[k], recv_a.at[k], (*chip, c)).wait_recv()
                cp = _remote(landed, landed, send_b.at[k], recv_b.at[k], sibling)
                cp.start()
                passed.append(cp)
        for p, (i, spec, part, parts) in enumerate(self.pieces):
            mine = _region(couts[i], spec, me, c, part, parts)
            for j, chip in enumerate(chips):
                k = 3 * p + j
                other = _region(couts[i], spec, _shard_of(chip), 1 - c, part, parts)
                _remote(other, other, send_b.at[k], recv_b.at[k], sibling).wait_recv()
                _remote(mine, mine, send_a.at[k], recv_a.at[k], (*chip, c)).wait_send()
        for cp in passed:
            cp.wait_send()
        if self.with_conv_w:
            local, sends, recvs = self._conv_w(cins, couts, sems, True)
            for cp in recvs:
                cp.wait_recv()
            for cp in sends:
                cp.wait_send()
            local.wait()


def _mm_in_gather(h1, w_whole, comm):
    spec = MATRICES["w_in"]
    cols = spec[1]
    bm = SEQ // 2

    def body(h_ref, w_in_ref, proj_ref, w_ref, wbuf, obuf, send_a, recv_a, send_b, recv_b, load_sem, store_sems):
        del w_in_ref
        s, mi = pl.program_id(0), pl.program_id(1)
        x, y, c, chips = _position()
        me = _shard_of((x, y))
        sibling = (x, y, 1 - c)
        mine = _region(w_ref, spec, me, c)

        @pl.when((s == 0) & (mi == 0))
        def _():
            for j, chip in enumerate(chips):
                _remote(mine, mine, send_a.at[j], recv_a.at[j], (*chip, c)).start()

        shard = me
        for j, chip in enumerate(chips):
            shard = jnp.where(s == j + 1, _shard_of(chip), shard)

            @pl.when((s == j + 1) & (mi == 0))
            def _():
                landed = _region(w_ref, spec, _shard_of(chip), c)
                _remote(landed, landed, send_a.at[j], recv_a.at[j], (*chip, c)).wait_recv()
                _remote(landed, landed, send_b.at[j], recv_b.at[j], sibling).start()
                other = _region(w_ref, spec, _shard_of(chip), 1 - c)
                _remote(other, other, send_b.at[j], recv_b.at[j], sibling).wait_recv()

        col0 = pl.multiple_of(shard * cols, LANES)

        @pl.when(mi == 0)
        def _():
            load = pltpu.make_async_copy(w_ref.at[:, pl.ds(col0, cols)], wbuf, load_sem.at[0])
            load.start()
            load.wait()

        def store():
            rows = pl.ds(pl.multiple_of(mi * bm, bm), bm)
            return pltpu.make_async_copy(obuf.at[mi], proj_ref.at[rows, pl.ds(col0, cols)], store_sems.at[mi])

        @pl.when(s > 0)
        def _():
            store().wait()

        obuf[mi] = jnp.dot(h_ref[...], wbuf[...], preferred_element_type=F32)
        store().start()

        @pl.when(s == N_CHIPS - 1)
        def _():
            store().wait()

        @pl.when((s == N_CHIPS - 1) & (mi == 1))
        def _():
            for j, chip in enumerate(chips):
                landed = _region(w_ref, spec, _shard_of(chip), c)
                _remote(mine, mine, send_a.at[j], recv_a.at[j], (*chip, c)).wait_send()
                _remote(landed, landed, send_b.at[j], recv_b.at[j], sibling).wait_send()

    sem3 = pltpu.SemaphoreType.DMA((3,))
    (proj, whole), extra = _pcall(
        body, "mm_in", (N_CHIPS, SEQ // bm),
        [pl.BlockSpec((bm, D_MODEL), lambda s, m: (m, 0)), HBM_SPEC], [HBM_SPEC, HBM_SPEC],
        [_sds((SEQ, N_IN), F32), _sds(w_whole.shape, w_whole.dtype)], [h1, w_whole],
        [pltpu.VMEM((D_MODEL, cols), BF16), pltpu.VMEM((SEQ // bm, bm, cols), F32), sem3, sem3, sem3, sem3,
         pltpu.SemaphoreType.DMA((1,)), pltpu.SemaphoreType.DMA((SEQ // bm,))],
        None, comm, aliases={1: 1})
    return proj, whole, extra


def _pack_small(dg_mix, dg_ffn, dg_final, dconv_w, dsinks, loss_row):
    def body(a_ref, b_ref, c_ref, w_ref, s_ref, l_ref, o_ref):
        pad = jnp.zeros((1, D_MODEL - LANES), F32)
        o_ref[0:1, :] = a_ref[...]
        o_ref[1:2, :] = b_ref[...]
        o_ref[2:3, :] = c_ref[...]
        o_ref[3:6, :] = w_ref[...]
        o_ref[6:7, :] = jnp.concatenate([s_ref[...], pad], axis=1)
        o_ref[7:8, :] = jnp.concatenate([l_ref[...], pad], axis=1)

    return pl.pallas_call(
        body, name="pack_small", out_shape=_sds((SMALL_ROWS, D_MODEL), F32),
        compiler_params=_params(),
    )(dg_mix, dg_ffn, dg_final, dconv_w, dsinks, loss_row)


class _Pair:
    def __init__(self, dws, specs):
        self.specs = specs
        self.operands = list(dws)
        self.out_shape = [_sds((N_CHIPS, *_half_shape(s)), BF16) for s in specs]
        self.aliases = {}
        n = N_CHIPS * len(specs)
        self.sems = [pltpu.SemaphoreType.DMA((n,)), pltpu.SemaphoreType.DMA((n,))]

    def _copies(self, cins, couts, sems):
        x, y, c, _ = _position()
        sibling = (x, y, 1 - c)
        for i, spec in enumerate(self.specs):
            for t in range(N_CHIPS):
                k = N_CHIPS * i + t
                yield _remote(_region(cins[i], spec, t, 1 - c), couts[i].at[t], sems[0].at[k], sems[1].at[k], sibling)

    def start(self, cins, couts, sems):
        for cp in self._copies(cins, couts, sems):
            cp.start()

    def finish(self, cins, couts, sems):
        for cp in self._copies(cins, couts, sems):
            cp.wait()


class _SmallAllToAll:
    def __init__(self, small):
        self.operands = [small]
        self.out_shape = [_sds((N_DEV, SMALL_ROWS, D_MODEL), F32)]
        self.aliases = {}
        self.sems = [pltpu.SemaphoreType.DMA((N_DEV - 1,)), pltpu.SemaphoreType.DMA((N_DEV - 1,)),
                     pltpu.SemaphoreType.DMA((1,))]

    def _copies(self, cins, couts, sems):
        x, y, c, _ = _position()
        me = 4 * x + 2 * y + c
        out = []
        for r in range(1, N_DEV):
            flip = ((r >> 2) & 1, (r >> 1) & 1, r & 1)
            peer = tuple(1 - p if f else p for p, f in zip((x, y, c), flip))
            theirs = couts[0].at[4 * peer[0] + 2 * peer[1] + peer[2]]
            out.append((_remote(cins[0], couts[0].at[me], sems[0].at[r - 1], sems[1].at[r - 1], peer),
                        _remote(theirs, theirs, sems[0].at[r - 1], sems[1].at[r - 1], peer)))
        return pltpu.make_async_copy(cins[0], couts[0].at[me], sems[2].at[0]), out

    def start(self, cins, couts, sems):
        own, copies = self._copies(cins, couts, sems)
        own.start()
        for send, _ in copies:
            send.start()

    def finish(self, cins, couts, sems):
        own, copies = self._copies(cins, couts, sems)
        for send, recv in copies:
            recv.wait_recv()
            send.wait_send()
        own.wait()


class _Both:
    def __init__(self, a, b):
        self.a, self.b = a, b
        self.operands = list(a.operands) + list(b.operands)
        self.out_shape = list(a.out_shape) + list(b.out_shape)
        self.aliases = dict(a.aliases)
        self.aliases.update({len(a.operands) + k: len(a.out_shape) + v for k, v in b.aliases.items()})
        self.sems = list(a.sems) + list(b.sems)

    def _split(self, cins, couts, sems):
        na, ma, sa = len(self.a.operands), len(self.a.out_shape), len(self.a.sems)
        return (cins[:na], couts[:ma], sems[:sa]), (cins[na:], couts[ma:], sems[sa:])

    def start(self, cins, couts, sems):
        for plan, args in zip((self.a, self.b), self._split(cins, couts, sems)):
            plan.start(*args)

    def finish(self, cins, couts, sems):
        for plan, args in zip((self.a, self.b), self._split(cins, couts, sems)):
            plan.finish(*args)


def _pair_sum(name, specs, dws, got, place):
    n_mat = len(specs)

    def body(p_ref, *refs):
        t = pl.program_id(0)
        mine, theirs = refs[:n_mat], refs[n_mat:2 * n_mat]
        outs, owns = refs[2 * n_mat:3 * n_mat], refs[3 * n_mat:]
        for a, b, o, own in zip(mine, theirs, outs, owns):
            s = (a[...].astype(F32) + b[...].astype(F32)).astype(BF16)
            o[...] = s

            @pl.when(t == p_ref[1])
            def _():
                own[...] = s

    def mine_spec(spec):
        hr, cols = _half_shape(spec)
        if spec[2] == "col":
            return pl.BlockSpec((hr, cols), lambda t, p_ref: (p_ref[0], t))
        return pl.BlockSpec((hr, cols), lambda t, p_ref: (2 * t + p_ref[0], 0))

    def slot_spec(spec):
        return pl.BlockSpec((None, *_half_shape(spec)), lambda t, p_ref: (t, 0, 0))

    def own_spec(spec):
        return pl.BlockSpec((None, *_half_shape(spec)), lambda t, p_ref: (p_ref[1], 0, 0))

    slots = [_sds((N_CHIPS, *_half_shape(s)), BF16) for s in specs]
    grid_spec = pltpu.PrefetchScalarGridSpec(
        num_scalar_prefetch=1, grid=(N_CHIPS,),
        in_specs=[mine_spec(s) for s in specs] + [slot_spec(s) for s in specs],
        out_specs=[slot_spec(s) for s in specs] + [own_spec(s) for s in specs])
    res = pl.pallas_call(
        body, name=name, grid_spec=grid_spec, out_shape=slots + slots,
        compiler_params=_params(("arbitrary",)),
    )(place, *dws, *got)
    return list(res[:n_mat]), list(res[n_mat:])


class _ChipExchange:
    def __init__(self, sums, slots, part=0, parts=1):
        self.n = len(sums)
        self.part, self.parts = part, parts
        self.operands = list(sums) + list(slots)
        self.out_shape = [_sds(s.shape, s.dtype) for s in slots]
        self.aliases = {self.n + i: i for i in range(self.n)}
        self.sems = [pltpu.SemaphoreType.DMA((3 * self.n,)), pltpu.SemaphoreType.DMA((3 * self.n,))]

    def _rows(self, ref, slot):
        n = ref.shape[1] // self.parts
        return ref.at[slot, pl.ds(self.part * n, n), :]

    def _copies(self, cins, couts, sems):
        x, y, c, chips = _position()
        me = _shard_of((x, y))
        for i in range(self.n):
            for j, chip in enumerate(chips):
                k = 3 * i + j
                theirs = self._rows(couts[i], _shard_of(chip))
                yield (_remote(self._rows(cins[i], _shard_of(chip)), self._rows(couts[i], me),
                               sems[0].at[k], sems[1].at[k], (*chip, c)),
                       _remote(theirs, theirs, sems[0].at[k], sems[1].at[k], (*chip, c)))

    def start(self, cins, couts, sems):
        for send, _ in self._copies(cins, couts, sems):
            send.start()

    def finish(self, cins, couts, sems):
        for send, recv in self._copies(cins, couts, sems):
            recv.wait_recv()
            send.wait_send()


def _chip_sum(name, specs, slots, core):
    steps = 2
    n_mat = len(specs)

    def body(c_ref, *refs):
        del c_ref
        ins, outs = refs[:n_mat], refs[n_mat:]
        for a, o in zip(ins, outs):
            acc = a[0].astype(F32)
            for t in range(1, N_CHIPS):
                acc = acc + a[t].astype(F32)
            o[...] = acc

    def in_spec(spec):
        hr, cols = _half_shape(spec)
        return pl.BlockSpec((N_CHIPS, hr // steps, cols), lambda i, c_ref: (0, i, 0))

    def out_spec(spec):
        hr, cols = _half_shape(spec)
        return pl.BlockSpec((hr // steps, cols), lambda i, c_ref: (c_ref[0] * steps + i, 0))

    grid_spec = pltpu.PrefetchScalarGridSpec(
        num_scalar_prefetch=1, grid=(steps,),
        in_specs=[in_spec(s) for s in specs], out_specs=[out_spec(s) for s in specs])
    return list(pl.pallas_call(
        body, name=name, grid_spec=grid_spec,
        out_shape=[_sds((s[0], s[1]), F32) for s in specs],
        compiler_params=_params(("parallel",)),
    )(core, *slots))


class _HalfExchange:
    def __init__(self, grads, specs):
        self.specs = specs
        self.operands = list(grads)
        self.out_shape = [_sds(g.shape, g.dtype) for g in grads]
        self.aliases = {i: i for i in range(len(grads))}
        self.sems = [pltpu.SemaphoreType.DMA((len(grads),)), pltpu.SemaphoreType.DMA((len(grads),))]

    def _copies(self, couts, sems):
        x, y, c, _ = _position()
        sibling = (x, y, 1 - c)
        for i, spec in enumerate(self.specs):
            hr = spec[0] // 2
            mine = couts[i].at[pl.ds(_aligned(c * hr, 8), hr), :]
            theirs = couts[i].at[pl.ds(_aligned((1 - c) * hr, 8), hr), :]
            yield (_remote(mine, mine, sems[0].at[i], sems[1].at[i], sibling),
                   _remote(theirs, theirs, sems[0].at[i], sems[1].at[i], sibling))

    def start(self, cins, couts, sems):
        for send, _ in self._copies(couts, sems):
            send.start()

    def finish(self, cins, couts, sems):
        for send, recv in self._copies(couts, sems):
            recv.wait_recv()
            send.wait_send()


def _small_sum(blocks):
    def body(b_ref, o_ref):
        acc = b_ref[0]
        for d in range(1, N_DEV):
            acc = acc + b_ref[d]
        o_ref[...] = acc

    return pl.pallas_call(
        body, name="small_sum", out_shape=_sds((SMALL_ROWS, D_MODEL), F32), compiler_params=_params(),
    )(blocks)


def _adamw(name, params, steps, comm=None):
    n = len(params)

    def body(*refs):
        for p in range(n):
            w_ref, g_ref, m_ref, v_ref = refs[4 * p:4 * p + 4]
            d_ref, nm_ref, nv_ref = refs[4 * n + 3 * p:4 * n + 3 * p + 3]
            g = g_ref[...]
            m = ADAM_B1 * m_ref[...] + (1.0 - ADAM_B1) * g
            v = ADAM_B2 * v_ref[...] + (1.0 - ADAM_B2) * jnp.square(g)
            m_hat = m / (1.0 - ADAM_B1 ** ADAM_STEP)
            v_hat = v / (1.0 - ADAM_B2 ** ADAM_STEP)
            d_ref[...] = -ADAM_LR * (m_hat / (jnp.sqrt(v_hat) + ADAM_EPS) + ADAM_WD * w_ref[...])
            nm_ref[...] = m
            nv_ref[...] = v

    in_specs, out_specs, out_shape, operands = [], [], [], []
    for w, g, m, v in params:
        spec = pl.BlockSpec((w.shape[0] // steps, w.shape[1]), lambda i: (i, 0))
        in_specs += [spec] * 4
        out_specs += [spec] * 3
        out_shape += [_sds(w.shape, F32)] * 3
        operands += [w, g, m, v]
    res = _pcall(body, name, (steps,), in_specs, out_specs, out_shape, operands, (), ("parallel",), comm)
    outs, extra = res if comm is not None else (res, None)
    triples = [tuple(outs[3 * p:3 * p + 3]) for p in range(n)]
    return triples if comm is None else (triples, extra)


MATRIX_NAMES = tuple(MATRICES)
WEIGHT_ORDER = ("g_mix", "w_in", "conv_w", "attn_sinks", "w_conv_out", "w_attn_out", "w_o", "g_ffn",
                "w_gate_up", "w_down", "g_final")


def kernel(x, g_mix, w_in, conv_w, attn_sinks, w_conv_out, w_attn_out, w_o, g_ffn, w_gate_up, w_down, g_final, loss_target, m_g_mix, m_w_in, m_conv_w, m_attn_sinks, m_w_conv_out, m_w_attn_out, m_w_o, m_g_ffn, m_w_gate_up, m_w_down, m_g_final, v_g_mix, v_w_in, v_conv_w, v_attn_sinks, v_w_conv_out, v_w_attn_out, v_w_o, v_g_ffn, v_w_gate_up, v_w_down, v_g_final):
    w = dict(g_mix=g_mix, w_in=w_in[0], conv_w=conv_w[0], attn_sinks=attn_sinks, w_conv_out=w_conv_out[0],
             w_attn_out=w_attn_out[0], w_o=w_o[0], g_ffn=g_ffn, w_gate_up=w_gate_up[0], w_down=w_down[0],
             g_final=g_final[None, :])
    m = dict(g_mix=m_g_mix, w_in=m_w_in[0], conv_w=m_conv_w[0], attn_sinks=m_attn_sinks,
             w_conv_out=m_w_conv_out[0], w_attn_out=m_w_attn_out[0], w_o=m_w_o[0], g_ffn=m_g_ffn,
             w_gate_up=m_w_gate_up[0], w_down=m_w_down[0], g_final=m_g_final[None, :])
    v = dict(g_mix=v_g_mix, w_in=v_w_in[0], conv_w=v_conv_w[0], attn_sinks=v_attn_sinks,
             w_conv_out=v_w_conv_out[0], w_attn_out=v_w_attn_out[0], w_o=v_w_o[0], g_ffn=v_g_ffn,
             w_gate_up=v_w_gate_up[0], w_down=v_w_down[0], g_final=v_g_final[None, :])
    shard = (2 * lax.axis_index("x") + lax.axis_index("y")).astype(jnp.int32)
    core = lax.axis_index("c").astype(jnp.int32)
    shard1, core1, place = shard.reshape((1,)), core.reshape((1,)), jnp.stack([core, shard])
    spec = MATRICES
    xs, target, sinks = x[0], loss_target[0], w["attn_sinks"]
    tables = _rope_tables()

    def gather(names, part=0, parts=1):
        return _Gather([whole[n] for n in names], [(i, spec[n], part, parts) for i, n in enumerate(names)])

    def pair(names):
        return _Pair([dw[n] for n in names], [spec[n] for n in names])

    def pair_sum(tag, names, got):
        return _pair_sum("pair_sum_" + tag, [spec[n] for n in names], [dw[n] for n in names], got, place)

    cast_rows = {"w_down": D_FF // N_CHIPS // 2}
    whole = {n: _to_bf16_in_whole("cast_" + n, w[n], spec[n], shard1, cast_rows.get(n, 256)) for n in MATRIX_NAMES}

    mixers = ("w_conv_out", "w_attn_out", "w_o")
    h1 = _rms_norm("norm_mix", xs, w["g_mix"])
    proj, whole["w_in"], (*got, conv_w_whole) = _mm_in_gather(
        h1, whole["w_in"], _Gather([whole[n] for n in mixers], [(i, spec[n], 0, 1) for i, n in enumerate(mixers)],
                                   conv_w=w["conv_w"]))
    whole.update(zip(mixers, got))
    conv_y = _conv_fwd(proj, conv_w_whole)
    attn, (whole["w_gate_up"],) = _attn_fwd(proj, tables, sinks, comm=gather(("w_gate_up",)))
    conv_out, attn_out, merged = _branch_merge(conv_y, attn, whole["w_conv_out"], whole["w_attn_out"], proj)
    x2 = _mm_nn("mm_o", merged, whole["w_o"], 1024, 1024, F32, res=xs)
    h2 = _rms_norm("norm_ffn", x2, w["g_ffn"])
    (gate, up, act), (whole["w_down"],) = _gate_up_fwd(h2, whole["w_gate_up"], comm=gather(("w_down",)))
    x3 = _mm_nn("mm_down", act, whole["w_down"], 1024, 512, F32, res=x2)
    dx3, dx3b, dg_final, loss_row = _loss_head(x3, w["g_final"], target)

    dw = {}
    dw["w_down"] = _mm_tn("mm_dw_down", act, dx3b, 1408, 1024, BF16)
    (dgate, dup), got = _dact_swiglu(dx3b, whole["w_down"], gate, up, comm=pair(("w_down",)))
    sums_a, own_a = pair_sum("down", ("w_down",), got)
    dh2, slots_a = _mm_dh2(dgate, dup, whole["w_gate_up"], comm=_ChipExchange(sums_a, own_a))
    dw["w_gate_up"] = _mm_dw_gate_up(h2, dgate, dup)
    (dx2, dx2b, dg_ffn), got = _rms_norm_bwd("norm_ffn_bwd", dh2, x2, w["g_ffn"], dx3, True, comm=pair(("w_gate_up",)))
    sums_b, own_b = pair_sum("gate_up", ("w_gate_up",), got)
    dw["w_o"] = _mm_tn("mm_dw_o", merged, dx2b, 1024, 1024, BF16)
    dco, dao, dgc, dga = _merge_bwd(dx2b, whole["w_o"], conv_out, attn_out, proj)
    dconv_y = _mm_nt("mm_dconv_y", dco, whole["w_conv_out"], 1024, 1024, D_MODEL, F32)
    dw["w_conv_out"] = _mm_tn("mm_dw_conv_out", conv_y, dco, 1024, 1024, BF16)
    dattn = _mm_nt("mm_dattn", dao, whole["w_attn_out"], 1024, 1024, D_MODEL, BF16)
    dw["w_attn_out"] = _mm_tn("mm_dw_attn_out", attn, dao, 1024, 1024, BF16)
    (dcb, dcc, dcx, dconv_w), got = _conv_bwd(dconv_y, proj, conv_w_whole, comm=pair(mixers))
    sums_c, own_c = pair_sum("mixers", mixers, got)
    (dq, dk_prev, dk_cur, dv_prev, dv_cur, dsinks), slots_b = _attn_bwd(
        proj, dattn, sinks, tables, comm=_ChipExchange(sums_b, own_b))
    dkv = _kv_grad_combine(dk_prev, dk_cur, dv_prev, dv_cur, tables)
    dproj = jnp.concatenate([dcb, dcc, dcx, dq, dkv, dgc, dga], axis=1)
    dw["w_in"], slots_c = _mm_tn("mm_dw_in", h1, dproj, 1024, 1664, BF16, comm=_ChipExchange(sums_c, own_c))
    sums_d, own_d = pair_sum("in", ("w_in",), _comm_call("pair_exchange_in", pair(("w_in",))))
    early = ("w_down", "w_gate_up") + mixers
    halves = _chip_sum("chip_sum_early", [spec[n] for n in early], slots_a + slots_b + slots_c, core1)
    dh1, (own_d, *reduced) = _mm_nt(
        "mm_dh1", dproj, whole["w_in"], 1024, 1024, 1664, F32,
        comm=_Both(_ChipExchange(sums_d, own_d, 0, 2), _HalfExchange(halves, [spec[n] for n in early])))
    g = dict(zip(early, reduced))
    (grad_x, dg_mix), slots_d = _rms_norm_bwd("norm_mix_bwd", dh1, xs, w["g_mix"], dx2, False,
                                              comm=_ChipExchange(sums_d, [own_d], 1, 2))
    small = _pack_small(dg_mix, dg_ffn, dg_final, dconv_w, dsinks, loss_row)
    half_in = _chip_sum("chip_sum_in", [spec["w_in"]], slots_d, core1)
    g["w_in"], small_blocks = _comm_call(
        "half_exchange_in", _Both(_HalfExchange(half_in, [spec["w_in"]]), _SmallAllToAll(small)))
    delta, new_m, new_v = {}, {}, {}

    def keep(names, triples):
        for n, (d, nm, nv) in zip(names, triples):
            delta[n], new_m[n], new_v[n] = d, nm, nv

    keep(early, _adamw("adamw_early", [(w[n], g[n], m[n], v[n]) for n in early], 8))
    small_sum = _small_sum(small_blocks)
    g["g_mix"] = small_sum[0:1, :]
    g["g_ffn"] = small_sum[1:2, :]
    g["g_final"] = small_sum[2:3, :]
    g["conv_w"] = lax.dynamic_slice(small_sum, (3, shard * CONV_W_COLS), (3, CONV_W_COLS))
    g["attn_sinks"] = small_sum[6:7, :N_HEADS]
    loss = small_sum[7, 0]
    keep(("w_in",), _adamw("adamw_w_in", [(w["w_in"], g["w_in"], m["w_in"], v["w_in"])], 4))
    rest = ("g_mix", "g_ffn", "g_final", "conv_w", "attn_sinks")
    keep(rest, _adamw("adamw_small", [(w[n], g[n], m[n], v[n]) for n in rest], 1))

    def shaped(vals):
        return [vals[n].reshape((D_MODEL,)) if n == "g_final" else
                (vals[n][None] if n in MATRIX_NAMES or n == "conv_w" else vals[n]) for n in WEIGHT_ORDER]

    return (loss, grad_x[None], *shaped(g), *shaped(delta), *shaped(new_m), *shaped(new_v))
```

```python
import functools
import math

import jax
import jax.numpy as jnp
from jax import lax
from jax.experimental import pallas as pl
from jax.experimental.pallas import tpu as pltpu

F32 = jnp.float32
BF16 = jnp.bfloat16

D_MODEL = 1024
SEQ = 2048
HEAD_DIM = 64
N_HEADS = 16
N_KV_HEADS = 4
GROUP = N_HEADS // N_KV_HEADS
D_ATTN = N_HEADS * HEAD_DIM
D_KV = N_KV_HEADS * HEAD_DIM
BLOCK = 128
ROT_DIM = HEAD_DIM // 4
ROPE_THETA = 500000.0
ATTN_SCALE = 1.0 / math.sqrt(HEAD_DIM)
NEG_INF = -1e30
D_FF = 2816
EPS = 1e-5
N_IN = 3 * D_MODEL + D_ATTN + 2 * D_KV + 2 * D_MODEL
COL_Q = 3 * D_MODEL
COL_K = COL_Q + D_ATTN
COL_V = COL_K + D_KV
COL_GC = COL_V + D_KV
COL_GA = COL_GC + D_MODEL

ADAM_LR = 0.001
ADAM_B1 = 0.9
ADAM_B2 = 0.999
ADAM_EPS = 1e-08
ADAM_WD = 0.01
ADAM_STEP = 10

N_CHIPS = 4
N_DEV = 8

V7X_VMEM_BYTES = 64 * 1024 * 1024
VMEM_LIMIT = (V7X_VMEM_BYTES * 3) // 4
LANES = 128
MESH = pl.DeviceIdType.MESH


def _params(semantics=None):
    return pltpu.CompilerParams(dimension_semantics=semantics, vmem_limit_bytes=VMEM_LIMIT)


def _sds(shape, dtype):
    return jax.ShapeDtypeStruct(shape, dtype)


HBM_SPEC = pl.BlockSpec(memory_space=pl.ANY)


def _pcall(body, name, grid, in_specs, out_specs, out_shape, operands, scratch=(), semantics=None, comm=None,
           aliases=None, start_after_body=False):
    aliases = dict(aliases or {})
    if comm is None:
        return pl.pallas_call(
            body, name=name, grid=grid, in_specs=in_specs, out_specs=out_specs, out_shape=out_shape,
            scratch_shapes=list(scratch), input_output_aliases=aliases,
            compiler_params=_params(semantics))(*operands)
    multi = isinstance(out_shape, (list, tuple))
    o_specs = list(out_specs) if multi else [out_specs]
    o_shape = list(out_shape) if multi else [out_shape]
    n_in, n_out, n_scr = len(operands), len(o_shape), len(scratch)
    n_cin, n_cout = len(comm.operands), len(comm.out_shape)

    def hosted(*refs):
        ins, cins = refs[:n_in], refs[n_in:n_in + n_cin]
        o0 = n_in + n_cin
        outs, couts = refs[o0:o0 + n_out], refs[o0 + n_out:o0 + n_out + n_cout]
        s0 = o0 + n_out + n_cout
        scr, sems = refs[s0:s0 + n_scr], refs[s0 + n_scr:]
        first = last = None
        for axis, size in enumerate(grid):
            i = pl.program_id(axis)
            first = (i == 0) if first is None else first & (i == 0)
            last = (i == size - 1) if last is None else last & (i == size - 1)

        if not start_after_body:
            @pl.when(first)
            def _():
                comm.start(cins, couts, sems)

        body(*ins, *outs, *scr)

        if start_after_body:
            @pl.when(first)
            def _():
                comm.start(cins, couts, sems)

        @pl.when(last)
        def _():
            comm.finish(cins, couts, sems)

    res = pl.pallas_call(
        hosted, name=name, grid=grid,
        in_specs=list(in_specs) + [HBM_SPEC] * n_cin, out_specs=o_specs + [HBM_SPEC] * n_cout,
        out_shape=o_shape + list(comm.out_shape), scratch_shapes=list(scratch) + list(comm.sems),
        input_output_aliases={**aliases, **{n_in + a: n_out + b for a, b in comm.aliases.items()}},
        compiler_params=_params(("arbitrary",) * len(grid)))(*operands, *comm.operands)
    outs = list(res[:n_out])
    return (outs if multi else outs[0]), list(res[n_out:])


def _comm_call(name, comm):
    def body(*refs):
        n_cin, n_cout = len(comm.operands), len(comm.out_shape)
        cins, couts, sems = refs[:n_cin], refs[n_cin:n_cin + n_cout], refs[n_cin + n_cout:]
        comm.start(cins, couts, sems)
        comm.finish(cins, couts, sems)

    return list(pl.pallas_call(
        body, name=name, in_specs=[HBM_SPEC] * len(comm.operands), out_specs=[HBM_SPEC] * len(comm.out_shape),
        out_shape=list(comm.out_shape), scratch_shapes=list(comm.sems),
        input_output_aliases=dict(comm.aliases))(*comm.operands))


NN = ((1,), (0,))
NT = ((1,), (1,))
TN = ((0,), (0,))


def _matmul(name, a, b, dims, grid, a_spec, b_spec, o_spec, o_shape, o_dtype, res=None, res_spec=None, comm=None):
    nk = grid[2]

    def body(*refs):
        if res is None:
            a_ref, b_ref, o_ref = refs[:3]
            r_ref = None
            scratch = refs[3:]
        else:
            a_ref, b_ref, r_ref, o_ref = refs[:4]
            scratch = refs[4:]
        p = lax.dot_general(a_ref[...], b_ref[...], (dims, ((), ())), preferred_element_type=F32)

        def finish(acc):
            if r_ref is not None:
                acc = r_ref[...] + acc
            o_ref[...] = acc.astype(o_dtype)

        if nk == 1:
            finish(p)
        else:
            acc_ref = scratch[0]
            k = pl.program_id(2)

            @pl.when(k == 0)
            def _():
                acc_ref[...] = p

            @pl.when(k > 0)
            def _():
                acc_ref[...] += p

            @pl.when(k == nk - 1)
            def _():
                finish(acc_ref[...])

    operands = [a, b] if res is None else [a, b, res]
    in_specs = [a_spec, b_spec] if res is None else [a_spec, b_spec, res_spec]
    scratch = [pltpu.VMEM(o_spec.block_shape, F32)] if nk > 1 else []
    return _pcall(body, name, grid, in_specs, o_spec, _sds(o_shape, o_dtype), operands, scratch,
                  ("parallel", "parallel", "arbitrary"), comm)


def _mm_nn(name, a, b, bm, bn, o_dtype, res=None, comm=None):
    m, k = a.shape
    n = b.shape[1]
    return _matmul(
        name, a, b, NN, (m // bm, n // bn, 1),
        pl.BlockSpec((bm, k), lambda i, j, kk: (i, 0)),
        pl.BlockSpec((k, bn), lambda i, j, kk: (0, j)),
        pl.BlockSpec((bm, bn), lambda i, j, kk: (i, j)),
        (m, n), o_dtype, res,
        None if res is None else pl.BlockSpec((bm, bn), lambda i, j, kk: (i, j)), comm,
    )


def _mm_nt(name, a, b, bm, bn, bk, o_dtype, comm=None):
    m, k = a.shape
    n = b.shape[0]
    return _matmul(
        name, a, b, NT, (m // bm, n // bn, k // bk),
        pl.BlockSpec((bm, bk), lambda i, j, kk: (i, kk)),
        pl.BlockSpec((bn, bk), lambda i, j, kk: (j, kk)),
        pl.BlockSpec((bm, bn), lambda i, j, kk: (i, j)),
        (m, n), o_dtype, comm=comm,
    )


def _mm_tn(name, a, b, bm, bn, o_dtype, comm=None):
    k, m = a.shape
    n = b.shape[1]
    return _matmul(
        name, a, b, TN, (m // bm, n // bn, 1),
        pl.BlockSpec((k, bm), lambda i, j, kk: (0, i)),
        pl.BlockSpec((k, bn), lambda i, j, kk: (0, j)),
        pl.BlockSpec((bm, bn), lambda i, j, kk: (i, j)),
        (m, n), o_dtype, comm=comm,
    )


ROWS = 256


def _row_spec(width, col=0):
    return pl.BlockSpec((ROWS, width), lambda i: (i, col))


def _full_spec(shape):
    return pl.BlockSpec(shape, lambda *_: (0,) * len(shape))


def _rms_norm(name, x, g):
    def body(x_ref, g_ref, h_ref):
        xf = x_ref[...]
        r = lax.rsqrt(jnp.mean(xf * xf, axis=-1, keepdims=True) + EPS)
        h_ref[...] = ((xf * r) * g_ref[...]).astype(BF16)

    return pl.pallas_call(
        body, name=name, grid=(SEQ // ROWS,),
        in_specs=[_row_spec(D_MODEL), _full_spec((1, D_MODEL))],
        out_specs=_row_spec(D_MODEL),
        out_shape=_sds((SEQ, D_MODEL), BF16),
        compiler_params=_params(("parallel",)),
    )(x, g)


CONV_COLS = 256


def _shift_rows(u, k):
    rows = lax.broadcasted_iota(jnp.int32, u.shape, 0)
    return jnp.where(rows >= k, pltpu.roll(u, k, axis=0), 0.0)


def _conv_fwd(proj, conv_w):
    nblk = D_MODEL // CONV_COLS

    def body(cb_ref, cc_ref, cx_ref, w_ref, y_ref):
        u = cc_ref[...] * cx_ref[...]
        w = w_ref[...]
        cv = w[0:1, :] * _shift_rows(u, 2) + w[1:2, :] * _shift_rows(u, 1) + w[2:3, :] * u
        y_ref[...] = (cb_ref[...] * cv).astype(BF16)

    def col(part):
        return pl.BlockSpec((SEQ, CONV_COLS), lambda j: (0, part * nblk + j))

    return pl.pallas_call(
        body, name="conv_fwd", grid=(nblk,),
        in_specs=[col(0), col(1), col(2), pl.BlockSpec((3, CONV_COLS), lambda j: (0, j))],
        out_specs=pl.BlockSpec((SEQ, CONV_COLS), lambda j: (0, j)),
        out_shape=_sds((SEQ, D_MODEL), BF16),
        compiler_params=_params(("parallel",)),
    )(proj, proj, proj, conv_w)


ROPE_COLS = 256


def _rope_tables():
    inv_freq = ROPE_THETA ** (-jnp.arange(0, ROT_DIM, 2, dtype=F32) / ROT_DIM)
    ang = jnp.arange(SEQ, dtype=F32)[:, None] * inv_freq[None, :]
    cos, sin = jnp.cos(ang), jnp.sin(ang)
    half = ROT_DIM // 2
    ones = jnp.ones((SEQ, HEAD_DIM - ROT_DIM), F32)
    zeros = jnp.zeros((SEQ, HEAD_DIM - ROT_DIM), F32)
    zh = jnp.zeros((SEQ, half), F32)
    c = jnp.concatenate([cos, cos, ones], axis=1)
    s_up = jnp.concatenate([-sin, zh, zeros], axis=1)
    s_dn = jnp.concatenate([zh, sin, zeros], axis=1)
    reps = ROPE_COLS // HEAD_DIM
    return tuple(jnp.tile(t, (1, reps)) for t in (c, s_up, s_dn))


def _rotate(t, c, s_up, s_dn):
    width = t.shape[1]
    half = ROT_DIM // 2
    return t * c + pltpu.roll(t, width - half, axis=1) * s_up + pltpu.roll(t, half, axis=1) * s_dn


N_QBLK = SEQ // BLOCK


def _attn_specs():
    prev = lambda n: jnp.maximum(n - 1, 0)
    q = pl.BlockSpec((BLOCK, D_ATTN), lambda n: (n, COL_Q // D_ATTN))
    k_prev = pl.BlockSpec((BLOCK, D_KV), lambda n: (prev(n), COL_K // D_KV))
    k_cur = pl.BlockSpec((BLOCK, D_KV), lambda n: (n, COL_K // D_KV))
    v_prev = pl.BlockSpec((BLOCK, D_KV), lambda n: (prev(n), COL_V // D_KV))
    v_cur = pl.BlockSpec((BLOCK, D_KV), lambda n: (n, COL_V // D_KV))
    tab_cur = pl.BlockSpec((BLOCK, ROPE_COLS), lambda n: (n, 0))
    tab_prev = pl.BlockSpec((BLOCK, ROPE_COLS), lambda n: (prev(n), 0))
    return [q, k_prev, k_cur, v_prev, v_cur] + [tab_cur] * 3 + [tab_prev] * 3


def _band_kv(kp_ref, kc_ref, vp_ref, vc_ref, tabs_cur, tabs_prev):
    k = jnp.concatenate([_rotate(kp_ref[...], *(t[...] for t in tabs_prev)),
                         _rotate(kc_ref[...], *(t[...] for t in tabs_cur))], axis=0)
    v = jnp.concatenate([vp_ref[...], vc_ref[...]], axis=0)
    return k, v


def _query_tile(q_ref, tile, tabs_cur):
    c, su, sd = (t[:, :LANES] for t in tabs_cur)
    return _rotate(q_ref[:, tile * LANES:(tile + 1) * LANES], c, su, sd).astype(BF16)


def _band_mask(n):
    kj = lax.broadcasted_iota(jnp.int32, (2 * BLOCK, BLOCK), 0)
    qi = lax.broadcasted_iota(jnp.int32, (2 * BLOCK, BLOCK), 1)
    rel = qi + BLOCK - kj
    return (rel >= 0) & (rel < BLOCK) & ((kj >= BLOCK) | (n > 0))


HEADS_PER_TILE = LANES // HEAD_DIM


def _lane_half(shape, par):
    lane = lax.broadcasted_iota(jnp.int32, shape, 1)
    return (lane < HEAD_DIM) if par == 0 else (lane >= HEAD_DIM)


def _head_tiles(kv, h):
    tile = kv[:, (h // HEADS_PER_TILE) * LANES:(h // HEADS_PER_TILE + 1) * LANES].astype(F32)
    own = jnp.where(_lane_half(tile.shape, h % HEADS_PER_TILE), tile, 0.0)
    other = pltpu.roll(own, HEAD_DIM, axis=1)
    lo, hi = (own, other) if h % HEADS_PER_TILE == 0 else (other, own)
    return lo.astype(BF16), hi.astype(BF16)


def _head_softmax(q_tile, k_half, sink, mask):
    s = lax.dot_general(k_half, q_tile, (NT, ((), ())), preferred_element_type=F32) * ATTN_SCALE
    s = jnp.where(mask, s, NEG_INF)
    m = jnp.maximum(jnp.max(s, axis=0, keepdims=True), sink)
    e = jnp.exp(s - m)
    es = jnp.exp(sink - m)
    inv = 1.0 / (jnp.sum(e, axis=0, keepdims=True) + es)
    return e * inv, es * inv


def _attn_fwd(proj, tables, sinks, comm=None):
    def body(sink_ref, q_ref, kp_ref, kc_ref, vp_ref, vc_ref, c_ref, su_ref, sd_ref, cp_ref, sup_ref, sdp_ref, o_ref):
        n = pl.program_id(0)
        mask = _band_mask(n)
        tabs_cur = (c_ref, su_ref, sd_ref)
        k, v = _band_kv(kp_ref, kc_ref, vp_ref, vc_ref, tabs_cur, (cp_ref, sup_ref, sdp_ref))
        for h in range(N_KV_HEADS):
            k_halves = _head_tiles(k, h)
            v_halves = _head_tiles(v, h)
            for t in range(GROUP // HEADS_PER_TILE):
                tile = h * (GROUP // HEADS_PER_TILE) + t
                q_tile = _query_tile(q_ref, tile, tabs_cur)
                acc = None
                for par in range(HEADS_PER_TILE):
                    sink = sink_ref[0, tile * HEADS_PER_TILE + par]
                    p, _ = _head_softmax(q_tile, k_halves[par], sink, mask)
                    o = lax.dot_general(p.astype(BF16), v_halves[par], (TN, ((), ())), preferred_element_type=F32)
                    acc = o if acc is None else acc + o
                o_ref[:, tile * LANES:(tile + 1) * LANES] = acc.astype(BF16)

    return _pcall(
        body, "attn_fwd", (N_QBLK,),
        [pl.BlockSpec(memory_space=pltpu.SMEM)] + _attn_specs(),
        pl.BlockSpec((BLOCK, D_ATTN), lambda n: (n, 0)),
        _sds((SEQ, D_ATTN), BF16), [sinks] + [proj] * 5 + list(tables) * 2, (), ("parallel",), comm)


def _branch_merge(conv_y, attn, w_co, w_ao, proj, comm=None):
    bm, bn = 1024, 512

    def body(cy_ref, at_ref, wc_ref, wa_ref, gc_ref, ga_ref, co_ref, ao_ref, mg_ref):
        co = jnp.dot(cy_ref[...], wc_ref[...], preferred_element_type=F32)
        ao = jnp.dot(at_ref[...], wa_ref[...], preferred_element_type=F32)
        co_ref[...] = co
        ao_ref[...] = ao
        mg_ref[...] = (jax.nn.sigmoid(gc_ref[...]) * co + jax.nn.sigmoid(ga_ref[...]) * ao).astype(BF16)

    act = pl.BlockSpec((bm, D_MODEL), lambda i, j: (i, 0))
    wgt = pl.BlockSpec((D_MODEL, bn), lambda i, j: (0, j))
    out = pl.BlockSpec((bm, bn), lambda i, j: (i, j))
    return _pcall(
        body, "branch_merge", (SEQ // bm, D_MODEL // bn),
        [act, act, wgt, wgt,
         pl.BlockSpec((bm, bn), lambda i, j: (i, COL_GC // bn + j)),
         pl.BlockSpec((bm, bn), lambda i, j: (i, COL_GA // bn + j))],
        [out, out, out],
        [_sds((SEQ, D_MODEL), F32), _sds((SEQ, D_MODEL), F32), _sds((SEQ, D_MODEL), BF16)],
        [conv_y, attn, w_co, w_ao, proj, proj], (), ("parallel", "parallel"), comm)


FF_BM, FF_BN = 512, 1408
FF_NB = D_FF // FF_BN


def _gate_up_fwd(h2, w_gu, comm=None):
    def body(h_ref, wg_ref, wu_ref, g_ref, u_ref, a_ref):
        h = h_ref[...]
        g = jnp.dot(h, wg_ref[...], preferred_element_type=F32)
        u = jnp.dot(h, wu_ref[...], preferred_element_type=F32)
        g_ref[...] = g
        u_ref[...] = u
        a_ref[...] = (jax.nn.silu(g) * u).astype(BF16)

    out = pl.BlockSpec((FF_BM, FF_BN), lambda j, i: (i, j))
    f32, b16 = _sds((SEQ, D_FF), F32), _sds((SEQ, D_FF), BF16)
    return _pcall(
        body, "mm_gate_up", (FF_NB, SEQ // FF_BM),
        [pl.BlockSpec((FF_BM, D_MODEL), lambda j, i: (i, 0)),
         pl.BlockSpec((D_MODEL, FF_BN), lambda j, i: (0, j)),
         pl.BlockSpec((D_MODEL, FF_BN), lambda j, i: (0, FF_NB + j))],
        [out, out, out], [f32, f32, b16], [h2, w_gu, w_gu], (), ("parallel", "parallel"), comm)


def _dact_swiglu(dx3b, w_down, g, u, comm=None):
    def body(dx_ref, w_ref, g_ref, u_ref, dg_ref, du_ref):
        da = lax.dot_general(dx_ref[...], w_ref[...], (NT, ((), ())), preferred_element_type=F32)
        g = g_ref[...]
        sg = jax.nn.sigmoid(g)
        dg_ref[...] = (da * u_ref[...] * (sg * (1.0 + g * (1.0 - sg)))).astype(BF16)
        du_ref[...] = (da * (g * sg)).astype(BF16)

    blk = pl.BlockSpec((FF_BM, FF_BN), lambda i, j: (i, j))
    b16 = _sds((SEQ, D_FF), BF16)
    return _pcall(
        body, "mm_dact", (SEQ // FF_BM, FF_NB),
        [pl.BlockSpec((FF_BM, D_MODEL), lambda i, j: (i, 0)), pl.BlockSpec((FF_BN, D_MODEL), lambda i, j: (j, 0)),
         blk, blk],
        [blk, blk], [b16, b16], [dx3b, w_down, g, u], (), ("parallel", "parallel"), comm)


def _mm_dh2(dg, du, w_gu, comm=None):
    bm = 1024
    nk = 2 * FF_NB

    def body(dg_ref, du_ref, w_ref, o_ref, acc_ref):
        k = pl.program_id(1)

        def part(a_ref):
            return lax.dot_general(a_ref[...], w_ref[...], (NT, ((), ())), preferred_element_type=F32)

        @pl.when(k == 0)
        def _():
            acc_ref[...] = part(dg_ref)

        @pl.when((k > 0) & (k < FF_NB))
        def _():
            acc_ref[...] += part(dg_ref)

        @pl.when(k >= FF_NB)
        def _():
            acc_ref[...] += part(du_ref)

        @pl.when(k == nk - 1)
        def _():
            o_ref[...] = acc_ref[...]

    return _pcall(
        body, "mm_dh2", (SEQ // bm, nk),
        [pl.BlockSpec((bm, FF_BN), lambda i, k: (i, jnp.minimum(k, FF_NB - 1))),
         pl.BlockSpec((bm, FF_BN), lambda i, k: (i, jnp.maximum(k - FF_NB, 0))),
         pl.BlockSpec((D_MODEL, FF_BN), lambda i, k: (0, k))],
        pl.BlockSpec((bm, D_MODEL), lambda i, k: (i, 0)), _sds((SEQ, D_MODEL), F32),
        [dg, du, w_gu], [pltpu.VMEM((bm, D_MODEL), F32)], ("parallel", "arbitrary"), comm)


def _mm_dw_gate_up(h2, dg, du):
    def body(h_ref, dg_ref, du_ref, o_ref):
        j = pl.program_id(0)

        def part(b_ref):
            return lax.dot_general(h_ref[...], b_ref[...], (TN, ((), ())), preferred_element_type=F32).astype(BF16)

        @pl.when(j < FF_NB)
        def _():
            o_ref[...] = part(dg_ref)

        @pl.when(j >= FF_NB)
        def _():
            o_ref[...] = part(du_ref)

    return pl.pallas_call(
        body, name="mm_dw_gate_up", grid=(2 * FF_NB,),
        in_specs=[_full_spec((SEQ, D_MODEL)),
                  pl.BlockSpec((SEQ, FF_BN), lambda j: (0, jnp.minimum(j, FF_NB - 1))),
                  pl.BlockSpec((SEQ, FF_BN), lambda j: (0, jnp.maximum(j - FF_NB, 0)))],
        out_specs=pl.BlockSpec((D_MODEL, FF_BN), lambda j: (0, j)),
        out_shape=_sds((D_MODEL, 2 * D_FF), BF16),
        compiler_params=_params(("arbitrary",)),
    )(h2, dg, du)


def _loss_head(x3, g, target):
    def body(x_ref, g_ref, t_ref, dx_ref, dxb_ref, dg_ref, loss_ref):
        i = pl.program_id(0)
        xf = x_ref[...]
        r = lax.rsqrt(jnp.mean(xf * xf, axis=-1, keepdims=True) + EPS)
        xn = xf * r
        gg = g_ref[...]
        err = xn * gg - t_ref[...]
        part = 0.5 * jnp.sum(jnp.mean(err * err, axis=-1, keepdims=True), axis=0, keepdims=True)
        dy = err * (1.0 / D_MODEL)
        dxn = dy * gg
        dx = r * (dxn - xn * jnp.mean(dxn * xn, axis=-1, keepdims=True))
        dx_ref[...] = dx
        dxb_ref[...] = dx.astype(BF16)
        dg = jnp.sum(dy * xn, axis=0, keepdims=True)
        lane0 = lax.broadcasted_iota(jnp.int32, (1, LANES), 1) == 0
        lpart = jnp.where(lane0, part, 0.0)

        @pl.when(i == 0)
        def _():
            dg_ref[...] = dg
            loss_ref[...] = lpart

        @pl.when(i > 0)
        def _():
            dg_ref[...] += dg
            loss_ref[...] += lpart

    return pl.pallas_call(
        body, name="loss_head", grid=(SEQ // ROWS,),
        in_specs=[_row_spec(D_MODEL), _full_spec((1, D_MODEL)), _row_spec(D_MODEL)],
        out_specs=[_row_spec(D_MODEL), _row_spec(D_MODEL), _full_spec((1, D_MODEL)), _full_spec((1, LANES))],
        out_shape=[_sds((SEQ, D_MODEL), F32), _sds((SEQ, D_MODEL), BF16),
                   _sds((1, D_MODEL), F32), _sds((1, LANES), F32)],
        compiler_params=_params(("arbitrary",)),
    )(x3, g, target)


def _rms_norm_bwd(name, dh, x, g, dres, with_bf16, comm=None):
    def body(dh_ref, x_ref, g_ref, dr_ref, *outs):
        i = pl.program_id(0)
        dx_ref = outs[0]
        dg_ref = outs[-1]
        xf = x_ref[...]
        r = lax.rsqrt(jnp.mean(xf * xf, axis=-1, keepdims=True) + EPS)
        xn = xf * r
        dh = dh_ref[...]
        dxn = dh * g_ref[...]
        dx = dr_ref[...] + r * (dxn - xn * jnp.mean(dxn * xn, axis=-1, keepdims=True))
        dx_ref[...] = dx
        if with_bf16:
            outs[1][...] = dx.astype(BF16)
        dg = jnp.sum(dh * xn, axis=0, keepdims=True)

        @pl.when(i == 0)
        def _():
            dg_ref[...] = dg

        @pl.when(i > 0)
        def _():
            dg_ref[...] += dg

    row = _row_spec(D_MODEL)
    out_specs = [row] + ([row] if with_bf16 else []) + [_full_spec((1, D_MODEL))]
    out_shape = ([_sds((SEQ, D_MODEL), F32)] + ([_sds((SEQ, D_MODEL), BF16)] if with_bf16 else [])
                 + [_sds((1, D_MODEL), F32)])
    return _pcall(body, name, (SEQ // ROWS,), [row, row, _full_spec((1, D_MODEL)), row], out_specs, out_shape,
                  [dh, x, g, dres], (), ("arbitrary",), comm)


def _merge_bwd(dx2b, w_o, conv_out, attn_out, proj):
    bm, bn = 1024, D_MODEL // 2

    def body(dx_ref, w_ref, co_ref, ao_ref, gc_ref, ga_ref, dco_ref, dao_ref, dgc_ref, dga_ref):
        dm = lax.dot_general(dx_ref[...], w_ref[...], (NT, ((), ())), preferred_element_type=F32)
        sc = jax.nn.sigmoid(gc_ref[...])
        sa = jax.nn.sigmoid(ga_ref[...])
        dco_ref[...] = (dm * sc).astype(BF16)
        dao_ref[...] = (dm * sa).astype(BF16)
        dgc_ref[...] = (dm * co_ref[...] * (sc * (1.0 - sc))).astype(BF16)
        dga_ref[...] = (dm * ao_ref[...] * (sa * (1.0 - sa))).astype(BF16)

    own = pl.BlockSpec((bm, bn), lambda i, j: (i, j))
    sd = _sds((SEQ, D_MODEL), BF16)
    return pl.pallas_call(
        body, name="mm_dmerged", grid=(SEQ // bm, D_MODEL // bn),
        in_specs=[pl.BlockSpec((bm, D_MODEL), lambda i, j: (i, 0)), pl.BlockSpec((bn, D_MODEL), lambda i, j: (j, 0)),
                  own, own,
                  pl.BlockSpec((bm, bn), lambda i, j: (i, COL_GC // bn + j)),
                  pl.BlockSpec((bm, bn), lambda i, j: (i, COL_GA // bn + j))],
        out_specs=[own, own, own, own], out_shape=[sd, sd, sd, sd],
        compiler_params=_params(("parallel", "parallel")),
    )(dx2b, w_o, conv_out, attn_out, proj, proj)


def _conv_bwd(dconv_y, proj, conv_w, comm=None):
    nblk = D_MODEL // CONV_COLS

    def body(dy_ref, cb_ref, cc_ref, cx_ref, w_ref, dcb_ref, dcc_ref, dcx_ref, dw_ref):
        cc = cc_ref[...]
        cx = cx_ref[...]
        u = cc * cx
        w = w_ref[...]
        u1 = _shift_rows(u, 1)
        u2 = _shift_rows(u, 2)
        cv = w[0:1, :] * u2 + w[1:2, :] * u1 + w[2:3, :] * u
        dy = dy_ref[...]
        dcb_ref[...] = (dy * cv).astype(BF16)
        dcv = dy * cb_ref[...]
        rows = lax.broadcasted_iota(jnp.int32, dcv.shape, 0)
        up1 = jnp.where(rows < SEQ - 1, pltpu.roll(dcv, SEQ - 1, axis=0), 0.0)
        up2 = jnp.where(rows < SEQ - 2, pltpu.roll(dcv, SEQ - 2, axis=0), 0.0)
        du = w[2:3, :] * dcv + w[1:2, :] * up1 + w[0:1, :] * up2
        dcc_ref[...] = (du * cx).astype(BF16)
        dcx_ref[...] = (du * cc).astype(BF16)
        dw_ref[...] = jnp.concatenate(
            [jnp.sum(dcv * u2, axis=0, keepdims=True),
             jnp.sum(dcv * u1, axis=0, keepdims=True),
             jnp.sum(dcv * u, axis=0, keepdims=True)], axis=0)

    def col(part):
        return pl.BlockSpec((SEQ, CONV_COLS), lambda j: (0, part * nblk + j))

    own = pl.BlockSpec((SEQ, CONV_COLS), lambda j: (0, j))
    wsp = pl.BlockSpec((3, CONV_COLS), lambda j: (0, j))
    sd = _sds((SEQ, D_MODEL), BF16)
    return _pcall(
        body, "conv_bwd", (nblk,), [own, col(0), col(1), col(2), wsp], [own, own, own, wsp],
        [sd, sd, sd, _sds((3, D_MODEL), F32)], [dconv_y, proj, proj, proj, conv_w], (), ("parallel",), comm)


def _attn_bwd(proj, dattn, sinks, tables, comm=None):
    def body(sink_ref, q_ref, kp_ref, kc_ref, vp_ref, vc_ref, c_ref, su_ref, sd_ref, cp_ref, sup_ref, sdp_ref,
             do_ref, dq_ref, dkp_ref, dkc_ref, dvp_ref, dvc_ref, ds_ref):
        n = pl.program_id(0)
        mask = _band_mask(n)
        tabs_cur = (c_ref, su_ref, sd_ref)
        k, v = _band_kv(kp_ref, kc_ref, vp_ref, vc_ref, tabs_cur, (cp_ref, sup_ref, sdp_ref))
        lane = lax.broadcasted_iota(jnp.int32, (1, LANES), 1)
        dsink = jnp.zeros((1, LANES), F32)
        c, su, sd = c_ref[:, :LANES], su_ref[:, :LANES], sd_ref[:, :LANES]
        dk_tiles = [None] * (N_KV_HEADS // HEADS_PER_TILE)
        dv_tiles = [None] * (N_KV_HEADS // HEADS_PER_TILE)
        for h in range(N_KV_HEADS):
            k_halves = _head_tiles(k, h)
            v_halves = _head_tiles(v, h)
            dk_par = [None] * HEADS_PER_TILE
            dv_par = [None] * HEADS_PER_TILE
            for t in range(GROUP // HEADS_PER_TILE):
                tile = h * (GROUP // HEADS_PER_TILE) + t
                q_tile = _query_tile(q_ref, tile, tabs_cur)
                do_tile = do_ref[:, tile * LANES:(tile + 1) * LANES]
                dq_tile = None
                for par in range(HEADS_PER_TILE):
                    head = tile * HEADS_PER_TILE + par
                    p, p_sink = _head_softmax(q_tile, k_halves[par], sink_ref[0, head], mask)
                    dp = lax.dot_general(v_halves[par], do_tile, (NT, ((), ())), preferred_element_type=F32)
                    delta = jnp.sum(p * dp, axis=0, keepdims=True)
                    ds = (p * (dp - delta) * ATTN_SCALE).astype(BF16)
                    dq = lax.dot_general(ds, k_halves[par], (TN, ((), ())), preferred_element_type=F32)
                    dq_tile = dq if dq_tile is None else dq_tile + dq
                    dk = jnp.dot(ds, q_tile, preferred_element_type=F32)
                    dv = jnp.dot(p.astype(BF16), do_tile, preferred_element_type=F32)
                    dk_par[par] = dk if dk_par[par] is None else dk_par[par] + dk
                    dv_par[par] = dv if dv_par[par] is None else dv_par[par] + dv
                    val = -jnp.sum(p_sink * delta, axis=1, keepdims=True)
                    dsink = dsink + jnp.where(lane == head, val, 0.0)
                dq_ref[:, tile * LANES:(tile + 1) * LANES] = _rotate(dq_tile, c, -su, -sd).astype(BF16)
            own = h % HEADS_PER_TILE
            for par_grads, tiles in ((dk_par, dk_tiles), (dv_par, dv_tiles)):
                shifted = pltpu.roll(par_grads[1 - own], HEAD_DIM, axis=1)
                total = jnp.where(_lane_half(shifted.shape, own), par_grads[own] + shifted, 0.0)
                i = h // HEADS_PER_TILE
                tiles[i] = total if tiles[i] is None else tiles[i] + total
        for i in range(N_KV_HEADS // HEADS_PER_TILE):
            cols = slice(i * LANES, (i + 1) * LANES)
            dkp_ref[:, cols] = dk_tiles[i][:BLOCK, :]
            dkc_ref[:, cols] = dk_tiles[i][BLOCK:, :]
            dvp_ref[:, cols] = dv_tiles[i][:BLOCK, :]
            dvc_ref[:, cols] = dv_tiles[i][BLOCK:, :]

        @pl.when(n == 0)
        def _():
            ds_ref[...] = dsink

        @pl.when(n > 0)
        def _():
            ds_ref[...] += dsink

    blk = pl.BlockSpec((BLOCK, D_KV), lambda n: (n, 0))
    kv = _sds((SEQ, D_KV), F32)
    return _pcall(
        body, "attn_bwd", (N_QBLK,),
        [pl.BlockSpec(memory_space=pltpu.SMEM)] + _attn_specs() + [pl.BlockSpec((BLOCK, D_ATTN), lambda n: (n, 0))],
        [pl.BlockSpec((BLOCK, D_ATTN), lambda n: (n, 0)), blk, blk, blk, blk, _full_spec((1, LANES))],
        [_sds((SEQ, D_ATTN), BF16), kv, kv, kv, kv, _sds((1, LANES), F32)],
        [sinks] + [proj] * 5 + list(tables) * 2 + [dattn], (), ("arbitrary",), comm)


def _kv_grad_combine(dk_prev, dk_cur, dv_prev, dv_cur, tables):
    def body(kp_ref, kc_ref, vp_ref, vc_ref, c_ref, su_ref, sd_ref, o_ref):
        m = pl.program_id(0)
        has_next = m < N_QBLK - 1
        dk = kc_ref[...] + jnp.where(has_next, kp_ref[...], 0.0)
        dv = vc_ref[...] + jnp.where(has_next, vp_ref[...], 0.0)
        o_ref[:, :D_KV] = _rotate(dk, c_ref[...], -su_ref[...], -sd_ref[...]).astype(BF16)
        o_ref[:, D_KV:] = dv.astype(BF16)

    cur = pl.BlockSpec((BLOCK, D_KV), lambda m: (m, 0))
    nxt = pl.BlockSpec((BLOCK, D_KV), lambda m: (jnp.minimum(m + 1, N_QBLK - 1), 0))
    return pl.pallas_call(
        body, name="kv_grad_combine", grid=(N_QBLK,),
        in_specs=[nxt, cur, nxt, cur, cur, cur, cur],
        out_specs=pl.BlockSpec((BLOCK, 2 * D_KV), lambda m: (m, 0)),
        out_shape=_sds((SEQ, 2 * D_KV), BF16),
        compiler_params=_params(("parallel",)),
    )(dk_prev, dk_cur, dv_prev, dv_cur, *tables)


MATRICES = {
    "w_in": (D_MODEL, N_IN // N_CHIPS, "col"),
    "w_conv_out": (D_MODEL // N_CHIPS, D_MODEL, "row"),
    "w_attn_out": (D_MODEL // N_CHIPS, D_MODEL, "row"),
    "w_o": (D_MODEL // N_CHIPS, D_MODEL, "row"),
    "w_gate_up": (D_MODEL, 2 * D_FF // N_CHIPS, "col"),
    "w_down": (D_FF // N_CHIPS, D_MODEL, "row"),
}
BF16_ROW_TILE = 16
CONV_W_COLS = D_MODEL // N_CHIPS
SMALL_ROWS = 8


def _whole_shape(spec):
    rows, cols, kind = spec
    return (rows, cols * N_CHIPS) if kind == "col" else (rows * N_CHIPS, cols)


def _half_shape(spec):
    return (spec[0] // 2, spec[1])


def _aligned(start, multiple):
    return start if isinstance(start, int) else pl.multiple_of(start, multiple)


def _region(ref, spec, shard, half, part=0, parts=1):
    rows, cols, kind = spec
    hr = rows // 2
    n = hr // parts
    if kind == "col":
        return ref.at[pl.ds(_aligned(half * hr + part * n, BF16_ROW_TILE), n),
                      pl.ds(_aligned(shard * cols, LANES), cols)]
    return ref.at[pl.ds(_aligned(shard * rows + half * hr + part * n, BF16_ROW_TILE), n), :]


def _position():
    x, y, c = lax.axis_index("x"), lax.axis_index("y"), lax.axis_index("c")
    chips = [(1 - x, y), (x, 1 - y), (1 - x, 1 - y)]
    return x, y, c, chips


def _shard_of(chip):
    return 2 * chip[0] + chip[1]


def _remote(src, dst, send_sem, recv_sem, to):
    return pltpu.make_async_remote_copy(src_ref=src, dst_ref=dst, send_sem=send_sem, recv_sem=recv_sem,
                                        device_id=to, device_id_type=MESH)


CAST_STEPS = 4


def _to_bf16_in_whole(ws, specs, shard):
    n = len(ws)

    def body(s_ref, *refs):
        del s_ref
        for w_ref, o_ref in zip(refs[:n], refs[n:]):
            o_ref[...] = w_ref[...].astype(BF16)

    def out_spec(spec):
        rows = spec[0] // CAST_STEPS
        if spec[2] == "col":
            return pl.BlockSpec((rows, spec[1]), lambda i, s_ref: (i, s_ref[0]))
        return pl.BlockSpec((rows, spec[1]), lambda i, s_ref: (s_ref[0] * CAST_STEPS + i, 0))

    grid_spec = pltpu.PrefetchScalarGridSpec(
        num_scalar_prefetch=1, grid=(CAST_STEPS,),
        in_specs=[pl.BlockSpec((s[0] // CAST_STEPS, s[1]), lambda i, s_ref: (i, 0)) for s in specs],
        out_specs=[out_spec(s) for s in specs])
    return list(pl.pallas_call(
        body, name="cast_shards", grid_spec=grid_spec, out_shape=[_sds(_whole_shape(s), BF16) for s in specs],
        compiler_params=_params(("parallel",)),
    )(shard, *ws))


class _Gather:
    def __init__(self, wholes, pieces, conv_w=None):
        self.pieces = pieces
        self.n = len(wholes)
        self.with_conv_w = conv_w is not None
        self.operands = list(wholes) + ([conv_w] if self.with_conv_w else [])
        self.out_shape = [_sds(w.shape, w.dtype) for w in wholes]
        if self.with_conv_w:
            self.out_shape.append(_sds((3, D_MODEL), F32))
        self.aliases = {i: i for i in range(self.n)}
        n_ici = 3 * len(pieces)
        self.sems = [pltpu.SemaphoreType.DMA((n_ici,))] * 4
        if self.with_conv_w:
            self.sems += [pltpu.SemaphoreType.DMA((1,)), pltpu.SemaphoreType.DMA((3,)), pltpu.SemaphoreType.DMA((3,))]

    def _conv_w(self, cins, couts, sems, with_recvs):
        cw_in, cw_out = cins[self.n], couts[self.n]
        x, y, c, chips = _position()

        def cols(shard):
            return cw_out.at[:, pl.ds(_aligned(shard * CONV_W_COLS, LANES), CONV_W_COLS)]

        me = _shard_of((x, y))
        local = pltpu.make_async_copy(cw_in, cols(me), sems[4].at[0])
        sends = [_remote(cw_in, cols(me), sems[5].at[j], sems[6].at[j], (*chip, c)) for j, chip in enumerate(chips)]
        if not with_recvs:
            return local, sends, []
        recvs = [_remote(cols(_shard_of(chip)), cols(_shard_of(chip)), sems[5].at[j], sems[6].at[j], (*chip, c))
                 for j, chip in enumerate(chips)]
        return local, sends, recvs

    def start(self, cins, couts, sems):
        x, y, c, chips = _position()
        me = _shard_of((x, y))
        if self.with_conv_w:
            local, sends, _ = self._conv_w(cins, couts, sems, False)
            local.start()
            for cp in sends:
                cp.start()
        for p, (i, spec, part, parts) in enumerate(self.pieces):
            mine = _region(couts[i], spec, me, c, part, parts)
            for j, chip in enumerate(chips):
                _remote(mine, mine, sems[0].at[3 * p + j], sems[1].at[3 * p + j], (*chip, c)).start()

    def finish(self, cins, couts, sems):
        x, y, c, chips = _position()
        me = _shard_of((x, y))
        sibling = (x, y, 1 - c)
        send_a, recv_a, send_b, recv_b = sems[:4]
        passed = []
        for p, (i, spec, part, parts) in enumerate(self.pieces):
            for j, chip in enumerate(chips):
                k = 3 * p + j
                landed = _region(couts[i], spec, _shard_of(chip), c, part, parts)
                _remote(landed, landed, send_a.at[k], recv_a.at[k], (*chip, c)).wait_recv()
                cp = _remote(landed, landed, send_b.at[k], recv_b.at[k], sibling)
                cp.start()
                passed.append(cp)
        for p, (i, spec, part, parts) in enumerate(self.pieces):
            mine = _region(couts[i], spec, me, c, part, parts)
            for j, chip in enumerate(chips):
                k = 3 * p + j
                other = _region(couts[i], spec, _shard_of(chip), 1 - c, part, parts)
                _remote(other, other, send_b.at[k], recv_b.at[k], sibling).wait_recv()
                _remote(mine, mine, send_a.at[k], recv_a.at[k], (*chip, c)).wait_send()
        for cp in passed:
            cp.wait_send()
        if self.with_conv_w:
            local, sends, recvs = self._conv_w(cins, couts, sems, True)
            for cp in recvs:
                cp.wait_recv()
            for cp in sends:
                cp.wait_send()
            local.wait()


def _mm_in_gather(h1, w_whole, comm):
    spec = MATRICES["w_in"]
    cols = spec[1]
    bm = SEQ // 2

    def body(h_ref, w_in_ref, proj_ref, w_ref, wbuf, obuf, send_a, recv_a, send_b, recv_b, load_sem, store_sems):
        del w_in_ref
        s, mi = pl.program_id(0), pl.program_id(1)
        x, y, c, chips = _position()
        me = _shard_of((x, y))
        sibling = (x, y, 1 - c)
        mine = _region(w_ref, spec, me, c)

        @pl.when((s == 0) & (mi == 0))
        def _():
            for j, chip in enumerate(chips):
                _remote(mine, mine, send_a.at[j], recv_a.at[j], (*chip, c)).start()

        shard = me
        for j, chip in enumerate(chips):
            shard = jnp.where(s == j + 1, _shard_of(chip), shard)

            @pl.when((s == j + 1) & (mi == 0))
            def _():
                landed = _region(w_ref, spec, _shard_of(chip), c)
                _remote(landed, landed, send_a.at[j], recv_a.at[j], (*chip, c)).wait_recv()
                _remote(landed, landed, send_b.at[j], recv_b.at[j], sibling).start()
                other = _region(w_ref, spec, _shard_of(chip), 1 - c)
                _remote(other, other, send_b.at[j], recv_b.at[j], sibling).wait_recv()

        col0 = pl.multiple_of(shard * cols, LANES)

        @pl.when(mi == 0)
        def _():
            load = pltpu.make_async_copy(w_ref.at[:, pl.ds(col0, cols)], wbuf, load_sem.at[0])
            load.start()
            load.wait()

        def store():
            rows = pl.ds(pl.multiple_of(mi * bm, bm), bm)
            return pltpu.make_async_copy(obuf.at[mi], proj_ref.at[rows, pl.ds(col0, cols)], store_sems.at[mi])

        @pl.when(s > 0)
        def _():
            store().wait()

        obuf[mi] = jnp.dot(h_ref[...], wbuf[...], preferred_element_type=F32)
        store().start()

        @pl.when(s == N_CHIPS - 1)
        def _():
            store().wait()

        @pl.when((s == N_CHIPS - 1) & (mi == 1))
        def _():
            for j, chip in enumerate(chips):
                landed = _region(w_ref, spec, _shard_of(chip), c)
                _remote(mine, mine, send_a.at[j], recv_a.at[j], (*chip, c)).wait_send()
                _remote(landed, landed, send_b.at[j], recv_b.at[j], sibling).wait_send()

    sem3 = pltpu.SemaphoreType.DMA((3,))
    (proj, whole), extra = _pcall(
        body, "mm_in", (N_CHIPS, SEQ // bm),
        [pl.BlockSpec((bm, D_MODEL), lambda s, m: (m, 0)), HBM_SPEC], [HBM_SPEC, HBM_SPEC],
        [_sds((SEQ, N_IN), F32), _sds(w_whole.shape, w_whole.dtype)], [h1, w_whole],
        [pltpu.VMEM((D_MODEL, cols), BF16), pltpu.VMEM((SEQ // bm, bm, cols), F32), sem3, sem3, sem3, sem3,
         pltpu.SemaphoreType.DMA((1,)), pltpu.SemaphoreType.DMA((SEQ // bm,))],
        None, comm, aliases={1: 1}, start_after_body=True)
    return proj, whole, extra


def _pack_small(dg_mix, dg_ffn, dg_final, dconv_w, dsinks, loss_row):
    def body(a_ref, b_ref, c_ref, w_ref, s_ref, l_ref, o_ref):
        pad = jnp.zeros((1, D_MODEL - LANES), F32)
        o_ref[0:1, :] = a_ref[...]
        o_ref[1:2, :] = b_ref[...]
        o_ref[2:3, :] = c_ref[...]
        o_ref[3:6, :] = w_ref[...]
        o_ref[6:7, :] = jnp.concatenate([s_ref[...], pad], axis=1)
        o_ref[7:8, :] = jnp.concatenate([l_ref[...], pad], axis=1)

    return pl.pallas_call(
        body, name="pack_small", out_shape=_sds((SMALL_ROWS, D_MODEL), F32),
        compiler_params=_params(),
    )(dg_mix, dg_ffn, dg_final, dconv_w, dsinks, loss_row)


class _Pair:
    def __init__(self, dws, specs):
        self.specs = specs
        self.operands = list(dws)
        self.out_shape = [_sds((N_CHIPS, *_half_shape(s)), BF16) for s in specs]
        self.aliases = {}
        n = N_CHIPS * len(specs)
        self.sems = [pltpu.SemaphoreType.DMA((n,)), pltpu.SemaphoreType.DMA((n,))]

    def _copies(self, cins, couts, sems):
        x, y, c, _ = _position()
        sibling = (x, y, 1 - c)
        for i, spec in enumerate(self.specs):
            for t in range(N_CHIPS):
                k = N_CHIPS * i + t
                yield _remote(_region(cins[i], spec, t, 1 - c), couts[i].at[t], sems[0].at[k], sems[1].at[k], sibling)

    def start(self, cins, couts, sems):
        for cp in self._copies(cins, couts, sems):
            cp.start()

    def finish(self, cins, couts, sems):
        for cp in self._copies(cins, couts, sems):
            cp.wait()


class _SmallAllToAll:
    def __init__(self, small):
        self.operands = [small]
        self.out_shape = [_sds((N_DEV, SMALL_ROWS, D_MODEL), F32)]
        self.aliases = {}
        self.sems = [pltpu.SemaphoreType.DMA((N_DEV - 1,)), pltpu.SemaphoreType.DMA((N_DEV - 1,)),
                     pltpu.SemaphoreType.DMA((1,))]

    def _copies(self, cins, couts, sems):
        x, y, c, _ = _position()
        me = 4 * x + 2 * y + c
        out = []
        for r in range(1, N_DEV):
            flip = ((r >> 2) & 1, (r >> 1) & 1, r & 1)
            peer = tuple(1 - p if f else p for p, f in zip((x, y, c), flip))
            theirs = couts[0].at[4 * peer[0] + 2 * peer[1] + peer[2]]
            out.append((_remote(cins[0], couts[0].at[me], sems[0].at[r - 1], sems[1].at[r - 1], peer),
                        _remote(theirs, theirs, sems[0].at[r - 1], sems[1].at[r - 1], peer)))
        return pltpu.make_async_copy(cins[0], couts[0].at[me], sems[2].at[0]), out

    def start(self, cins, couts, sems):
        own, copies = self._copies(cins, couts, sems)
        own.start()
        for send, _ in copies:
            send.start()

    def finish(self, cins, couts, sems):
        own, copies = self._copies(cins, couts, sems)
        for send, recv in copies:
            recv.wait_recv()
            send.wait_send()
        own.wait()


class _Both:
    def __init__(self, a, b):
        self.a, self.b = a, b
        self.operands = list(a.operands) + list(b.operands)
        self.out_shape = list(a.out_shape) + list(b.out_shape)
        self.aliases = dict(a.aliases)
        self.aliases.update({len(a.operands) + k: len(a.out_shape) + v for k, v in b.aliases.items()})
        self.sems = list(a.sems) + list(b.sems)

    def _split(self, cins, couts, sems):
        na, ma, sa = len(self.a.operands), len(self.a.out_shape), len(self.a.sems)
        return (cins[:na], couts[:ma], sems[:sa]), (cins[na:], couts[ma:], sems[sa:])

    def start(self, cins, couts, sems):
        for plan, args in zip((self.a, self.b), self._split(cins, couts, sems)):
            plan.start(*args)

    def finish(self, cins, couts, sems):
        for plan, args in zip((self.a, self.b), self._split(cins, couts, sems)):
            plan.finish(*args)


def _pair_sum(name, specs, dws, got, place):
    n_mat = len(specs)

    def body(p_ref, *refs):
        t = pl.program_id(0)
        mine, theirs = refs[:n_mat], refs[n_mat:2 * n_mat]
        outs, owns = refs[2 * n_mat:3 * n_mat], refs[3 * n_mat:]
        for a, b, o, own in zip(mine, theirs, outs, owns):
            s = (a[...].astype(F32) + b[...].astype(F32)).astype(BF16)
            o[...] = s

            @pl.when(t == p_ref[1])
            def _():
                own[...] = s

    def mine_spec(spec):
        hr, cols = _half_shape(spec)
        if spec[2] == "col":
            return pl.BlockSpec((hr, cols), lambda t, p_ref: (p_ref[0], t))
        return pl.BlockSpec((hr, cols), lambda t, p_ref: (2 * t + p_ref[0], 0))

    def slot_spec(spec):
        return pl.BlockSpec((None, *_half_shape(spec)), lambda t, p_ref: (t, 0, 0))

    def own_spec(spec):
        return pl.BlockSpec((None, *_half_shape(spec)), lambda t, p_ref: (p_ref[1], 0, 0))

    slots = [_sds((N_CHIPS, *_half_shape(s)), BF16) for s in specs]
    grid_spec = pltpu.PrefetchScalarGridSpec(
        num_scalar_prefetch=1, grid=(N_CHIPS,),
        in_specs=[mine_spec(s) for s in specs] + [slot_spec(s) for s in specs],
        out_specs=[slot_spec(s) for s in specs] + [own_spec(s) for s in specs])
    res = pl.pallas_call(
        body, name=name, grid_spec=grid_spec, out_shape=slots + slots,
        compiler_params=_params(("arbitrary",)),
    )(place, *dws, *got)
    return list(res[:n_mat]), list(res[n_mat:])


class _ChipExchange:
    def __init__(self, sums, slots, part=0, parts=1):
        self.n = len(sums)
        self.part, self.parts = part, parts
        self.operands = list(sums) + list(slots)
        self.out_shape = [_sds(s.shape, s.dtype) for s in slots]
        self.aliases = {self.n + i: i for i in range(self.n)}
        self.sems = [pltpu.SemaphoreType.DMA((3 * self.n,)), pltpu.SemaphoreType.DMA((3 * self.n,))]

    def _rows(self, ref, slot):
        n = ref.shape[1] // self.parts
        return ref.at[slot, pl.ds(self.part * n, n), :]

    def _copies(self, cins, couts, sems):
        x, y, c, chips = _position()
        me = _shard_of((x, y))
        for i in range(self.n):
            for j, chip in enumerate(chips):
                k = 3 * i + j
                theirs = self._rows(couts[i], _shard_of(chip))
                yield (_remote(self._rows(cins[i], _shard_of(chip)), self._rows(couts[i], me),
                               sems[0].at[k], sems[1].at[k], (*chip, c)),
                       _remote(theirs, theirs, sems[0].at[k], sems[1].at[k], (*chip, c)))

    def start(self, cins, couts, sems):
        for send, _ in self._copies(cins, couts, sems):
            send.start()

    def finish(self, cins, couts, sems):
        for send, recv in self._copies(cins, couts, sems):
            recv.wait_recv()
            send.wait_send()


def _chip_sum(name, specs, slots, core):
    steps = 2
    n_mat = len(specs)

    def body(c_ref, *refs):
        del c_ref
        ins, outs = refs[:n_mat], refs[n_mat:]
        for a, o in zip(ins, outs):
            acc = a[0].astype(F32)
            for t in range(1, N_CHIPS):
                acc = acc + a[t].astype(F32)
            o[...] = acc

    def in_spec(spec):
        hr, cols = _half_shape(spec)
        return pl.BlockSpec((N_CHIPS, hr // steps, cols), lambda i, c_ref: (0, i, 0))

    def out_spec(spec):
        hr, cols = _half_shape(spec)
        return pl.BlockSpec((hr // steps, cols), lambda i, c_ref: (c_ref[0] * steps + i, 0))

    grid_spec = pltpu.PrefetchScalarGridSpec(
        num_scalar_prefetch=1, grid=(steps,),
        in_specs=[in_spec(s) for s in specs], out_specs=[out_spec(s) for s in specs])
    return list(pl.pallas_call(
        body, name=name, grid_spec=grid_spec,
        out_shape=[_sds((s[0], s[1]), F32) for s in specs],
        compiler_params=_params(("parallel",)),
    )(core, *slots))


class _HalfExchange:
    def __init__(self, grads, specs):
        self.specs = specs
        self.operands = list(grads)
        self.out_shape = [_sds(g.shape, g.dtype) for g in grads]
        self.aliases = {i: i for i in range(len(grads))}
        self.sems = [pltpu.SemaphoreType.DMA((len(grads),)), pltpu.SemaphoreType.DMA((len(grads),))]

    def _copies(self, couts, sems):
        x, y, c, _ = _position()
        sibling = (x, y, 1 - c)
        for i, spec in enumerate(self.specs):
            hr = spec[0] // 2
            mine = couts[i].at[pl.ds(_aligned(c * hr, 8), hr), :]
            theirs = couts[i].at[pl.ds(_aligned((1 - c) * hr, 8), hr), :]
            yield (_remote(mine, mine, sems[0].at[i], sems[1].at[i], sibling),
                   _remote(theirs, theirs, sems[0].at[i], sems[1].at[i], sibling))

    def start(self, cins, couts, sems):
        for send, _ in self._copies(couts, sems):
            send.start()

    def finish(self, cins, couts, sems):
        for send, recv in self._copies(couts, sems):
            recv.wait_recv()
            send.wait_send()


def _small_sum(blocks):
    def body(b_ref, o_ref):
        acc = b_ref[0]
        for d in range(1, N_DEV):
            acc = acc + b_ref[d]
        o_ref[...] = acc

    return pl.pallas_call(
        body, name="small_sum", out_shape=_sds((SMALL_ROWS, D_MODEL), F32), compiler_params=_params(),
    )(blocks)


def _adamw(name, params, steps):
    n = len(params)

    def body(*refs):
        for p in range(n):
            w_ref, g_ref, m_ref, v_ref = refs[4 * p:4 * p + 4]
            d_ref, nm_ref, nv_ref, go_ref = refs[4 * n + 4 * p:4 * n + 4 * p + 4]
            g = g_ref[...]
            go_ref[...] = g
            m = ADAM_B1 * m_ref[...] + (1.0 - ADAM_B1) * g
            v = ADAM_B2 * v_ref[...] + (1.0 - ADAM_B2) * jnp.square(g)
            m_hat = m / (1.0 - ADAM_B1 ** ADAM_STEP)
            v_hat = v / (1.0 - ADAM_B2 ** ADAM_STEP)
            d_ref[...] = -ADAM_LR * (m_hat / (jnp.sqrt(v_hat) + ADAM_EPS) + ADAM_WD * w_ref[...])
            nm_ref[...] = m
            nv_ref[...] = v

    in_specs, out_specs, out_shape, operands = [], [], [], []
    for w, g, m, v in params:
        spec = pl.BlockSpec((w.shape[0] // steps, w.shape[1]), lambda i: (i, 0))
        in_specs += [spec] * 4
        out_specs += [spec] * 4
        out_shape += [_sds(w.shape, F32)] * 4
        operands += [w, g, m, v]
    outs = _pcall(body, name, (steps,), in_specs, out_specs, out_shape, operands, (), ("parallel",))
    return [tuple(outs[4 * p:4 * p + 4]) for p in range(n)]


MATRIX_NAMES = tuple(MATRICES)
WEIGHT_ORDER = ("g_mix", "w_in", "conv_w", "attn_sinks", "w_conv_out", "w_attn_out", "w_o", "g_ffn",
                "w_gate_up", "w_down", "g_final")


def kernel(x, g_mix, w_in, conv_w, attn_sinks, w_conv_out, w_attn_out, w_o, g_ffn, w_gate_up, w_down, g_final, loss_target, m_g_mix, m_w_in, m_conv_w, m_attn_sinks, m_w_conv_out, m_w_attn_out, m_w_o, m_g_ffn, m_w_gate_up, m_w_down, m_g_final, v_g_mix, v_w_in, v_conv_w, v_attn_sinks, v_w_conv_out, v_w_attn_out, v_w_o, v_g_ffn, v_w_gate_up, v_w_down, v_g_final):
    w = dict(g_mix=g_mix, w_in=w_in[0], conv_w=conv_w[0], attn_sinks=attn_sinks, w_conv_out=w_conv_out[0],
             w_attn_out=w_attn_out[0], w_o=w_o[0], g_ffn=g_ffn, w_gate_up=w_gate_up[0], w_down=w_down[0],
             g_final=g_final[None, :])
    m = dict(g_mix=m_g_mix, w_in=m_w_in[0], conv_w=m_conv_w[0], attn_sinks=m_attn_sinks,
             w_conv_out=m_w_conv_out[0], w_attn_out=m_w_attn_out[0], w_o=m_w_o[0], g_ffn=m_g_ffn,
             w_gate_up=m_w_gate_up[0], w_down=m_w_down[0], g_final=m_g_final[None, :])
    v = dict(g_mix=v_g_mix, w_in=v_w_in[0], conv_w=v_conv_w[0], attn_sinks=v_attn_sinks,
             w_conv_out=v_w_conv_out[0], w_attn_out=v_w_attn_out[0], w_o=v_w_o[0], g_ffn=v_g_ffn,
             w_gate_up=v_w_gate_up[0], w_down=v_w_down[0], g_final=v_g_final[None, :])
    shard = (2 * lax.axis_index("x") + lax.axis_index("y")).astype(jnp.int32)
    core = lax.axis_index("c").astype(jnp.int32)
    shard1, core1, place = shard.reshape((1,)), core.reshape((1,)), jnp.stack([core, shard])
    spec = MATRICES
    xs, target, sinks = x[0], loss_target[0], w["attn_sinks"]
    tables = _rope_tables()

    def gather(names, part=0, parts=1):
        return _Gather([whole[n] for n in names], [(i, spec[n], part, parts) for i, n in enumerate(names)])

    def pair(names):
        return _Pair([dw[n] for n in names], [spec[n] for n in names])

    def pair_sum(tag, names, got):
        return _pair_sum("pair_sum_" + tag, [spec[n] for n in names], [dw[n] for n in names], got, place)

    whole = dict(zip(MATRIX_NAMES, _to_bf16_in_whole(
        [w[n] for n in MATRIX_NAMES], [spec[n] for n in MATRIX_NAMES], shard1)))

    mixers = ("w_conv_out", "w_attn_out", "w_o")
    h1 = _rms_norm("norm_mix", xs, w["g_mix"])
    proj, whole["w_in"], (*got, conv_w_whole) = _mm_in_gather(
        h1, whole["w_in"], _Gather([whole[n] for n in mixers], [(i, spec[n], 0, 1) for i, n in enumerate(mixers)],
                                   conv_w=w["conv_w"]))
    whole.update(zip(mixers, got))
    conv_y = _conv_fwd(proj, conv_w_whole)
    attn, (whole["w_gate_up"],) = _attn_fwd(proj, tables, sinks, comm=gather(("w_gate_up",)))
    conv_out, attn_out, merged = _branch_merge(conv_y, attn, whole["w_conv_out"], whole["w_attn_out"], proj)
    x2 = _mm_nn("mm_o", merged, whole["w_o"], 1024, 1024, F32, res=xs)
    h2 = _rms_norm("norm_ffn", x2, w["g_ffn"])
    (gate, up, act), (whole["w_down"],) = _gate_up_fwd(h2, whole["w_gate_up"], comm=gather(("w_down",)))
    x3 = _mm_nn("mm_down", act, whole["w_down"], 1024, 512, F32, res=x2)
    dx3, dx3b, dg_final, loss_row = _loss_head(x3, w["g_final"], target)

    dw = {}
    dw["w_down"] = _mm_tn("mm_dw_down", act, dx3b, 1408, 1024, BF16)
    (dgate, dup), got = _dact_swiglu(dx3b, whole["w_down"], gate, up, comm=pair(("w_down",)))
    sums_a, own_a = pair_sum("down", ("w_down",), got)
    dh2, slots_a = _mm_dh2(dgate, dup, whole["w_gate_up"], comm=_ChipExchange(sums_a, own_a))
    dw["w_gate_up"] = _mm_dw_gate_up(h2, dgate, dup)
    (dx2, dx2b, dg_ffn), got = _rms_norm_bwd("norm_ffn_bwd", dh2, x2, w["g_ffn"], dx3, True, comm=pair(("w_gate_up",)))
    sums_b, own_b = pair_sum("gate_up", ("w_gate_up",), got)
    dw["w_o"] = _mm_tn("mm_dw_o", merged, dx2b, 1024, 1024, BF16)
    dco, dao, dgc, dga = _merge_bwd(dx2b, whole["w_o"], conv_out, attn_out, proj)
    dconv_y = _mm_nt("mm_dconv_y", dco, whole["w_conv_out"], 1024, 1024, D_MODEL, F32)
    dw["w_conv_out"] = _mm_tn("mm_dw_conv_out", conv_y, dco, 1024, 1024, BF16)
    dattn = _mm_nt("mm_dattn", dao, whole["w_attn_out"], 1024, 1024, D_MODEL, BF16)
    dw["w_attn_out"] = _mm_tn("mm_dw_attn_out", attn, dao, 1024, 1024, BF16)
    (dcb, dcc, dcx, dconv_w), got = _conv_bwd(dconv_y, proj, conv_w_whole, comm=pair(mixers))
    sums_c, own_c = pair_sum("mixers", mixers, got)
    (dq, dk_prev, dk_cur, dv_prev, dv_cur, dsinks), slots_b = _attn_bwd(
        proj, dattn, sinks, tables, comm=_ChipExchange(sums_b, own_b))
    dkv = _kv_grad_combine(dk_prev, dk_cur, dv_prev, dv_cur, tables)
    dproj = jnp.concatenate([dcb, dcc, dcx, dq, dkv, dgc, dga], axis=1)
    dw["w_in"], slots_c = _mm_tn("mm_dw_in", h1, dproj, 1024, 1664, BF16, comm=_ChipExchange(sums_c, own_c))
    sums_d, own_d = pair_sum("in", ("w_in",), _comm_call("pair_exchange_in", pair(("w_in",))))
    early = ("w_down", "w_gate_up") + mixers
    halves = _chip_sum("chip_sum_early", [spec[n] for n in early], slots_a + slots_b + slots_c, core1)
    dh1, (own_d, *reduced) = _mm_nt(
        "mm_dh1", dproj, whole["w_in"], 1024, 1024, 1664, F32,
        comm=_Both(_ChipExchange(sums_d, own_d, 0, 2), _HalfExchange(halves, [spec[n] for n in early])))
    g = dict(zip(early, reduced))
    (grad_x, dg_mix), slots_d = _rms_norm_bwd("norm_mix_bwd", dh1, xs, w["g_mix"], dx2, False,
                                              comm=_ChipExchange(sums_d, [own_d], 1, 2))
    small = _pack_small(dg_mix, dg_ffn, dg_final, dconv_w, dsinks, loss_row)
    half_in = _chip_sum("chip_sum_in", [spec["w_in"]], slots_d, core1)
    g["w_in"], small_blocks = _comm_call(
        "half_exchange_in", _Both(_HalfExchange(half_in, [spec["w_in"]]), _SmallAllToAll(small)))
    delta, new_m, new_v = {}, {}, {}

    def keep(names, results):
        for n, (d, nm, nv, grad) in zip(names, results):
            delta[n], new_m[n], new_v[n], g[n] = d, nm, nv, grad

    keep(early, _adamw("adamw_early", [(w[n], g[n], m[n], v[n]) for n in early], 8))
    small_sum = _small_sum(small_blocks)
    g["g_mix"] = small_sum[0:1, :]
    g["g_ffn"] = small_sum[1:2, :]
    g["g_final"] = small_sum[2:3, :]
    g["conv_w"] = lax.dynamic_slice(small_sum, (3, shard * CONV_W_COLS), (3, CONV_W_COLS))
    g["attn_sinks"] = small_sum[6:7, :N_HEADS]
    loss = small_sum[7, 0]
    keep(("w_in",), _adamw("adamw_w_in", [(w["w_in"], g["w_in"], m["w_in"], v["w_in"])], 4))
    rest = ("g_mix", "g_ffn", "g_final", "conv_w", "attn_sinks")
    keep(rest, _adamw("adamw_small", [(w[n], g[n], m[n], v[n]) for n in rest], 1))

    def shaped(vals):
        return [vals[n].reshape((D_MODEL,)) if n == "g_final" else
                (vals[n][None] if n in MATRIX_NAMES or n == "conv_w" else vals[n]) for n in WEIGHT_ORDER]

    return (loss, grad_x[None], *shaped(g), *shaped(delta), *shaped(new_m), *shaped(new_v))
```

```python
import functools
import math

import jax
import jax.numpy as jnp
from jax import lax
from jax.experimental import pallas as pl
from jax.experimental.pallas import tpu as pltpu

F32 = jnp.float32
BF16 = jnp.bfloat16

D_MODEL = 1024
SEQ = 2048
HEAD_DIM = 64
N_HEADS = 16
N_KV_HEADS = 4
GROUP = N_HEADS // N_KV_HEADS
D_ATTN = N_HEADS * HEAD_DIM
D_KV = N_KV_HEADS * HEAD_DIM
BLOCK = 128
ROT_DIM = HEAD_DIM // 4
ROPE_THETA = 500000.0
ATTN_SCALE = 1.0 / math.sqrt(HEAD_DIM)
NEG_INF = -1e30
D_FF = 2816
EPS = 1e-5
N_IN = 3 * D_MODEL + D_ATTN + 2 * D_KV + 2 * D_MODEL
COL_Q = 3 * D_MODEL
COL_K = COL_Q + D_ATTN
COL_V = COL_K + D_KV
COL_GC = COL_V + D_KV
COL_GA = COL_GC + D_MODEL

ADAM_LR = 0.001
ADAM_B1 = 0.9
ADAM_B2 = 0.999
ADAM_EPS = 1e-08
ADAM_WD = 0.01
ADAM_STEP = 10

N_CHIPS = 4
N_DEV = 8

V7X_VMEM_BYTES = 64 * 1024 * 1024
VMEM_LIMIT = (V7X_VMEM_BYTES * 3) // 4
LANES = 128
MESH = pl.DeviceIdType.MESH


def _params(semantics=None):
    return pltpu.CompilerParams(dimension_semantics=semantics, vmem_limit_bytes=VMEM_LIMIT)


def _sds(shape, dtype):
    return jax.ShapeDtypeStruct(shape, dtype)


HBM_SPEC = pl.BlockSpec(memory_space=pl.ANY)


def _pcall(body, name, grid, in_specs, out_specs, out_shape, operands, scratch=(), semantics=None, comm=None,
           aliases=None, start_after_body=False):
    aliases = dict(aliases or {})
    if comm is None:
        return pl.pallas_call(
            body, name=name, grid=grid, in_specs=in_specs, out_specs=out_specs, out_shape=out_shape,
            scratch_shapes=list(scratch), input_output_aliases=aliases,
            compiler_params=_params(semantics))(*operands)
    multi = isinstance(out_shape, (list, tuple))
    o_specs = list(out_specs) if multi else [out_specs]
    o_shape = list(out_shape) if multi else [out_shape]
    n_in, n_out, n_scr = len(operands), len(o_shape), len(scratch)
    n_cin, n_cout = len(comm.operands), len(comm.out_shape)

    def hosted(*refs):
        ins, cins = refs[:n_in], refs[n_in:n_in + n_cin]
        o0 = n_in + n_cin
        outs, couts = refs[o0:o0 + n_out], refs[o0 + n_out:o0 + n_out + n_cout]
        s0 = o0 + n_out + n_cout
        scr, sems = refs[s0:s0 + n_scr], refs[s0 + n_scr:]
        first = last = None
        for axis, size in enumerate(grid):
            i = pl.program_id(axis)
            first = (i == 0) if first is None else first & (i == 0)
            last = (i == size - 1) if last is None else last & (i == size - 1)

        if not start_after_body:
            @pl.when(first)
            def _():
                comm.start(cins, couts, sems)

        body(*ins, *outs, *scr)

        if start_after_body:
            @pl.when(first)
            def _():
                comm.start(cins, couts, sems)

        @pl.when(last)
        def _():
            comm.finish(cins, couts, sems)

    res = pl.pallas_call(
        hosted, name=name, grid=grid,
        in_specs=list(in_specs) + [HBM_SPEC] * n_cin, out_specs=o_specs + [HBM_SPEC] * n_cout,
        out_shape=o_shape + list(comm.out_shape), scratch_shapes=list(scratch) + list(comm.sems),
        input_output_aliases={**aliases, **{n_in + a: n_out + b for a, b in comm.aliases.items()}},
        compiler_params=_params(("arbitrary",) * len(grid)))(*operands, *comm.operands)
    outs = list(res[:n_out])
    return (outs if multi else outs[0]), list(res[n_out:])


def _comm_call(name, comm):
    def body(*refs):
        n_cin, n_cout = len(comm.operands), len(comm.out_shape)
        cins, couts, sems = refs[:n_cin], refs[n_cin:n_cin + n_cout], refs[n_cin + n_cout:]
        comm.start(cins, couts, sems)
        comm.finish(cins, couts, sems)

    return list(pl.pallas_call(
        body, name=name, in_specs=[HBM_SPEC] * len(comm.operands), out_specs=[HBM_SPEC] * len(comm.out_shape),
        out_shape=list(comm.out_shape), scratch_shapes=list(comm.sems),
        input_output_aliases=dict(comm.aliases))(*comm.operands))


NN = ((1,), (0,))
NT = ((1,), (1,))
TN = ((0,), (0,))


def _matmul(name, a, b, dims, grid, a_spec, b_spec, o_spec, o_shape, o_dtype, res=None, res_spec=None, comm=None):
    nk = grid[2]

    def body(*refs):
        if res is None:
            a_ref, b_ref, o_ref = refs[:3]
            r_ref = None
            scratch = refs[3:]
        else:
            a_ref, b_ref, r_ref, o_ref = refs[:4]
            scratch = refs[4:]
        p = lax.dot_general(a_ref[...], b_ref[...], (dims, ((), ())), preferred_element_type=F32)

        def finish(acc):
            if r_ref is not None:
                acc = r_ref[...] + acc
            o_ref[...] = acc.astype(o_dtype)

        if nk == 1:
            finish(p)
        else:
            acc_ref = scratch[0]
            k = pl.program_id(2)

            @pl.when(k == 0)
            def _():
                acc_ref[...] = p

            @pl.when(k > 0)
            def _():
                acc_ref[...] += p

            @pl.when(k == nk - 1)
            def _():
                finish(acc_ref[...])

    operands = [a, b] if res is None else [a, b, res]
    in_specs = [a_spec, b_spec] if res is None else [a_spec, b_spec, res_spec]
    scratch = [pltpu.VMEM(o_spec.block_shape, F32)] if nk > 1 else []
    return _pcall(body, name, grid, in_specs, o_spec, _sds(o_shape, o_dtype), operands, scratch,
                  ("parallel", "parallel", "arbitrary"), comm)


def _mm_nn(name, a, b, bm, bn, o_dtype, res=None, comm=None):
    m, k = a.shape
    n = b.shape[1]
    return _matmul(
        name, a, b, NN, (m // bm, n // bn, 1),
        pl.BlockSpec((bm, k), lambda i, j, kk: (i, 0)),
        pl.BlockSpec((k, bn), lambda i, j, kk: (0, j)),
        pl.BlockSpec((bm, bn), lambda i, j, kk: (i, j)),
        (m, n), o_dtype, res,
        None if res is None else pl.BlockSpec((bm, bn), lambda i, j, kk: (i, j)), comm,
    )


def _mm_nt(name, a, b, bm, bn, bk, o_dtype, comm=None):
    m, k = a.shape
    n = b.shape[0]
    return _matmul(
        name, a, b, NT, (m // bm, n // bn, k // bk),
        pl.BlockSpec((bm, bk), lambda i, j, kk: (i, kk)),
        pl.BlockSpec((bn, bk), lambda i, j, kk: (j, kk)),
        pl.BlockSpec((bm, bn), lambda i, j, kk: (i, j)),
        (m, n), o_dtype, comm=comm,
    )


def _mm_tn(name, a, b, bm, bn, o_dtype, comm=None):
    k, m = a.shape
    n = b.shape[1]
    return _matmul(
        name, a, b, TN, (m // bm, n // bn, 1),
        pl.BlockSpec((k, bm), lambda i, j, kk: (0, i)),
        pl.BlockSpec((k, bn), lambda i, j, kk: (0, j)),
        pl.BlockSpec((bm, bn), lambda i, j, kk: (i, j)),
        (m, n), o_dtype, comm=comm,
    )


ROWS = 256


def _row_spec(width, col=0):
    return pl.BlockSpec((ROWS, width), lambda i: (i, col))


def _full_spec(shape):
    return pl.BlockSpec(shape, lambda *_: (0,) * len(shape))


def _rms_norm(name, x, g):
    def body(x_ref, g_ref, h_ref):
        xf = x_ref[...]
        r = lax.rsqrt(jnp.mean(xf * xf, axis=-1, keepdims=True) + EPS)
        h_ref[...] = ((xf * r) * g_ref[...]).astype(BF16)

    return pl.pallas_call(
        body, name=name, grid=(SEQ // ROWS,),
        in_specs=[_row_spec(D_MODEL), _full_spec((1, D_MODEL))],
        out_specs=_row_spec(D_MODEL),
        out_shape=_sds((SEQ, D_MODEL), BF16),
        compiler_params=_params(("parallel",)),
    )(x, g)


CONV_COLS = 256


def _shift_rows(u, k):
    rows = lax.broadcasted_iota(jnp.int32, u.shape, 0)
    return jnp.where(rows >= k, pltpu.roll(u, k, axis=0), 0.0)


def _conv_fwd(proj, conv_w):
    nblk = D_MODEL // CONV_COLS

    def body(cb_ref, cc_ref, cx_ref, w_ref, y_ref):
        u = cc_ref[...] * cx_ref[...]
        w = w_ref[...]
        cv = w[0:1, :] * _shift_rows(u, 2) + w[1:2, :] * _shift_rows(u, 1) + w[2:3, :] * u
        y_ref[...] = (cb_ref[...] * cv).astype(BF16)

    def col(part):
        return pl.BlockSpec((SEQ, CONV_COLS), lambda j: (0, part * nblk + j))

    return pl.pallas_call(
        body, name="conv_fwd", grid=(nblk,),
        in_specs=[col(0), col(1), col(2), pl.BlockSpec((3, CONV_COLS), lambda j: (0, j))],
        out_specs=pl.BlockSpec((SEQ, CONV_COLS), lambda j: (0, j)),
        out_shape=_sds((SEQ, D_MODEL), BF16),
        compiler_params=_params(("parallel",)),
    )(proj, proj, proj, conv_w)


ROPE_COLS = 256


def _rope_tables():
    inv_freq = ROPE_THETA ** (-jnp.arange(0, ROT_DIM, 2, dtype=F32) / ROT_DIM)
    ang = jnp.arange(SEQ, dtype=F32)[:, None] * inv_freq[None, :]
    cos, sin = jnp.cos(ang), jnp.sin(ang)
    half = ROT_DIM // 2
    ones = jnp.ones((SEQ, HEAD_DIM - ROT_DIM), F32)
    zeros = jnp.zeros((SEQ, HEAD_DIM - ROT_DIM), F32)
    zh = jnp.zeros((SEQ, half), F32)
    c = jnp.concatenate([cos, cos, ones], axis=1)
    s_up = jnp.concatenate([-sin, zh, zeros], axis=1)
    s_dn = jnp.concatenate([zh, sin, zeros], axis=1)
    reps = ROPE_COLS // HEAD_DIM
    return tuple(jnp.tile(t, (1, reps)) for t in (c, s_up, s_dn))


def _rotate(t, c, s_up, s_dn):
    width = t.shape[1]
    half = ROT_DIM // 2
    return t * c + pltpu.roll(t, width - half, axis=1) * s_up + pltpu.roll(t, half, axis=1) * s_dn


N_QBLK = SEQ // BLOCK


def _attn_specs():
    prev = lambda n: jnp.maximum(n - 1, 0)
    q = pl.BlockSpec((BLOCK, D_ATTN), lambda n: (n, COL_Q // D_ATTN))
    k_prev = pl.BlockSpec((BLOCK, D_KV), lambda n: (prev(n), COL_K // D_KV))
    k_cur = pl.BlockSpec((BLOCK, D_KV), lambda n: (n, COL_K // D_KV))
    v_prev = pl.BlockSpec((BLOCK, D_KV), lambda n: (prev(n), COL_V // D_KV))
    v_cur = pl.BlockSpec((BLOCK, D_KV), lambda n: (n, COL_V // D_KV))
    tab_cur = pl.BlockSpec((BLOCK, ROPE_COLS), lambda n: (n, 0))
    tab_prev = pl.BlockSpec((BLOCK, ROPE_COLS), lambda n: (prev(n), 0))
    return [q, k_prev, k_cur, v_prev, v_cur] + [tab_cur] * 3 + [tab_prev] * 3


def _band_kv(kp_ref, kc_ref, vp_ref, vc_ref, tabs_cur, tabs_prev):
    k = jnp.concatenate([_rotate(kp_ref[...], *(t[...] for t in tabs_prev)),
                         _rotate(kc_ref[...], *(t[...] for t in tabs_cur))], axis=0)
    v = jnp.concatenate([vp_ref[...], vc_ref[...]], axis=0)
    return k, v


def _query_tiles(q_ref, tiles, tabs_cur):
    c, su, sd = (t[:, :LANES] for t in tabs_cur)
    return jnp.concatenate(
        [_rotate(q_ref[:, t * LANES:(t + 1) * LANES], c, su, sd).astype(BF16) for t in tiles], axis=0)


def _sink_row(sink_ref, tiles, par):
    return jnp.concatenate([jnp.full((1, BLOCK), sink_ref[0, t * HEADS_PER_TILE + par], F32) for t in tiles], axis=1)


def _band_mask(n):
    kj = lax.broadcasted_iota(jnp.int32, (2 * BLOCK, BLOCK), 0)
    qi = lax.broadcasted_iota(jnp.int32, (2 * BLOCK, BLOCK), 1)
    rel = qi + BLOCK - kj
    return (rel >= 0) & (rel < BLOCK) & ((kj >= BLOCK) | (n > 0))


HEADS_PER_TILE = LANES // HEAD_DIM
TILES_PER_GROUP = GROUP // HEADS_PER_TILE


def _group_mask(n):
    return jnp.concatenate([_band_mask(n)] * TILES_PER_GROUP, axis=1)


def _lane_half(shape, par):
    lane = lax.broadcasted_iota(jnp.int32, shape, 1)
    return (lane < HEAD_DIM) if par == 0 else (lane >= HEAD_DIM)


def _head_tiles(kv, h):
    tile = kv[:, (h // HEADS_PER_TILE) * LANES:(h // HEADS_PER_TILE + 1) * LANES].astype(F32)
    own = jnp.where(_lane_half(tile.shape, h % HEADS_PER_TILE), tile, 0.0)
    other = pltpu.roll(own, HEAD_DIM, axis=1)
    lo, hi = (own, other) if h % HEADS_PER_TILE == 0 else (other, own)
    return lo.astype(BF16), hi.astype(BF16)


def _head_softmax(q_tile, k_half, sink, mask):
    s = lax.dot_general(k_half, q_tile, (NT, ((), ())), preferred_element_type=F32) * ATTN_SCALE
    s = jnp.where(mask, s, NEG_INF)
    m = jnp.maximum(jnp.max(s, axis=0, keepdims=True), sink)
    e = jnp.exp(s - m)
    es = jnp.exp(sink - m)
    inv = 1.0 / (jnp.sum(e, axis=0, keepdims=True) + es)
    return e * inv, es * inv


def _attn_fwd(proj, tables, sinks, comm=None):
    def body(sink_ref, q_ref, kp_ref, kc_ref, vp_ref, vc_ref, c_ref, su_ref, sd_ref, cp_ref, sup_ref, sdp_ref, o_ref):
        n = pl.program_id(0)
        mask = _group_mask(n)
        tabs_cur = (c_ref, su_ref, sd_ref)
        k, v = _band_kv(kp_ref, kc_ref, vp_ref, vc_ref, tabs_cur, (cp_ref, sup_ref, sdp_ref))
        for h in range(N_KV_HEADS):
            k_halves = _head_tiles(k, h)
            v_halves = _head_tiles(v, h)
            tiles = [h * TILES_PER_GROUP + t for t in range(TILES_PER_GROUP)]
            q_rows = _query_tiles(q_ref, tiles, tabs_cur)
            acc = None
            for par in range(HEADS_PER_TILE):
                p, _ = _head_softmax(q_rows, k_halves[par], _sink_row(sink_ref, tiles, par), mask)
                o = lax.dot_general(p.astype(BF16), v_halves[par], (TN, ((), ())), preferred_element_type=F32)
                acc = o if acc is None else acc + o
            for i, tile in enumerate(tiles):
                o_ref[:, tile * LANES:(tile + 1) * LANES] = acc[i * BLOCK:(i + 1) * BLOCK, :].astype(BF16)

    return _pcall(
        body, "attn_fwd", (N_QBLK,),
        [pl.BlockSpec(memory_space=pltpu.SMEM)] + _attn_specs(),
        pl.BlockSpec((BLOCK, D_ATTN), lambda n: (n, 0)),
        _sds((SEQ, D_ATTN), BF16), [sinks] + [proj] * 5 + list(tables) * 2, (), ("parallel",), comm)


def _branch_merge(conv_y, attn, w_co, w_ao, proj, comm=None):
    bm, bn = 1024, 512

    def body(cy_ref, at_ref, wc_ref, wa_ref, gc_ref, ga_ref, co_ref, ao_ref, mg_ref):
        co = jnp.dot(cy_ref[...], wc_ref[...], preferred_element_type=F32)
        ao = jnp.dot(at_ref[...], wa_ref[...], preferred_element_type=F32)
        co_ref[...] = co
        ao_ref[...] = ao
        mg_ref[...] = (jax.nn.sigmoid(gc_ref[...]) * co + jax.nn.sigmoid(ga_ref[...]) * ao).astype(BF16)

    act = pl.BlockSpec((bm, D_MODEL), lambda i, j: (i, 0))
    wgt = pl.BlockSpec((D_MODEL, bn), lambda i, j: (0, j))
    out = pl.BlockSpec((bm, bn), lambda i, j: (i, j))
    return _pcall(
        body, "branch_merge", (SEQ // bm, D_MODEL // bn),
        [act, act, wgt, wgt,
         pl.BlockSpec((bm, bn), lambda i, j: (i, COL_GC // bn + j)),
         pl.BlockSpec((bm, bn), lambda i, j: (i, COL_GA // bn + j))],
        [out, out, out],
        [_sds((SEQ, D_MODEL), F32), _sds((SEQ, D_MODEL), F32), _sds((SEQ, D_MODEL), BF16)],
        [conv_y, attn, w_co, w_ao, proj, proj], (), ("parallel", "parallel"), comm)


FF_BM, FF_BN = 512, 1408
FF_NB = D_FF // FF_BN


def _gate_up_fwd(h2, w_gu, comm=None):
    def body(h_ref, wg_ref, wu_ref, g_ref, u_ref, a_ref):
        h = h_ref[...]
        g = jnp.dot(h, wg_ref[...], preferred_element_type=F32)
        u = jnp.dot(h, wu_ref[...], preferred_element_type=F32)
        g_ref[...] = g
        u_ref[...] = u
        a_ref[...] = (jax.nn.silu(g) * u).astype(BF16)

    out = pl.BlockSpec((FF_BM, FF_BN), lambda j, i: (i, j))
    f32, b16 = _sds((SEQ, D_FF), F32), _sds((SEQ, D_FF), BF16)
    return _pcall(
        body, "mm_gate_up", (FF_NB, SEQ // FF_BM),
        [pl.BlockSpec((FF_BM, D_MODEL), lambda j, i: (i, 0)),
         pl.BlockSpec((D_MODEL, FF_BN), lambda j, i: (0, j)),
         pl.BlockSpec((D_MODEL, FF_BN), lambda j, i: (0, FF_NB + j))],
        [out, out, out], [f32, f32, b16], [h2, w_gu, w_gu], (), ("parallel", "parallel"), comm)


def _dact_swiglu(dx3b, w_down, g, u, comm=None):
    def body(dx_ref, w_ref, g_ref, u_ref, dg_ref, du_ref):
        da = lax.dot_general(dx_ref[...], w_ref[...], (NT, ((), ())), preferred_element_type=F32)
        g = g_ref[...]
        sg = jax.nn.sigmoid(g)
        dg_ref[...] = (da * u_ref[...] * (sg * (1.0 + g * (1.0 - sg)))).astype(BF16)
        du_ref[...] = (da * (g * sg)).astype(BF16)

    blk = pl.BlockSpec((FF_BM, FF_BN), lambda i, j: (i, j))
    b16 = _sds((SEQ, D_FF), BF16)
    return _pcall(
        body, "mm_dact", (SEQ // FF_BM, FF_NB),
        [pl.BlockSpec((FF_BM, D_MODEL), lambda i, j: (i, 0)), pl.BlockSpec((FF_BN, D_MODEL), lambda i, j: (j, 0)),
         blk, blk],
        [blk, blk], [b16, b16], [dx3b, w_down, g, u], (), ("parallel", "parallel"), comm, start_after_body=True)


def _mm_dh2(dg, du, w_gu, comm=None):
    bm = 1024
    nk = 2 * FF_NB

    def body(dg_ref, du_ref, w_ref, o_ref, acc_ref):
        k = pl.program_id(1)

        def part(a_ref):
            return lax.dot_general(a_ref[...], w_ref[...], (NT, ((), ())), preferred_element_type=F32)

        @pl.when(k == 0)
        def _():
            acc_ref[...] = part(dg_ref)

        @pl.when((k > 0) & (k < FF_NB))
        def _():
            acc_ref[...] += part(dg_ref)

        @pl.when(k >= FF_NB)
        def _():
            acc_ref[...] += part(du_ref)

        @pl.when(k == nk - 1)
        def _():
            o_ref[...] = acc_ref[...]

    return _pcall(
        body, "mm_dh2", (SEQ // bm, nk),
        [pl.BlockSpec((bm, FF_BN), lambda i, k: (i, jnp.minimum(k, FF_NB - 1))),
         pl.BlockSpec((bm, FF_BN), lambda i, k: (i, jnp.maximum(k - FF_NB, 0))),
         pl.BlockSpec((D_MODEL, FF_BN), lambda i, k: (0, k))],
        pl.BlockSpec((bm, D_MODEL), lambda i, k: (i, 0)), _sds((SEQ, D_MODEL), F32),
        [dg, du, w_gu], [pltpu.VMEM((bm, D_MODEL), F32)], ("parallel", "arbitrary"), comm)


def _mm_dw_gate_up(h2, dg, du):
    def body(h_ref, dg_ref, du_ref, o_ref):
        j = pl.program_id(0)

        def part(b_ref):
            return lax.dot_general(h_ref[...], b_ref[...], (TN, ((), ())), preferred_element_type=F32).astype(BF16)

        @pl.when(j < FF_NB)
        def _():
            o_ref[...] = part(dg_ref)

        @pl.when(j >= FF_NB)
        def _():
            o_ref[...] = part(du_ref)

    return pl.pallas_call(
        body, name="mm_dw_gate_up", grid=(2 * FF_NB,),
        in_specs=[_full_spec((SEQ, D_MODEL)),
                  pl.BlockSpec((SEQ, FF_BN), lambda j: (0, jnp.minimum(j, FF_NB - 1))),
                  pl.BlockSpec((SEQ, FF_BN), lambda j: (0, jnp.maximum(j - FF_NB, 0)))],
        out_specs=pl.BlockSpec((D_MODEL, FF_BN), lambda j: (0, j)),
        out_shape=_sds((D_MODEL, 2 * D_FF), BF16),
        compiler_params=_params(("arbitrary",)),
    )(h2, dg, du)


def _loss_head(x3, g, target):
    def body(x_ref, g_ref, t_ref, dx_ref, dxb_ref, dg_ref, loss_ref):
        i = pl.program_id(0)
        xf = x_ref[...]
        r = lax.rsqrt(jnp.mean(xf * xf, axis=-1, keepdims=True) + EPS)
        xn = xf * r
        gg = g_ref[...]
        err = xn * gg - t_ref[...]
        part = 0.5 * jnp.sum(jnp.mean(err * err, axis=-1, keepdims=True), axis=0, keepdims=True)
        dy = err * (1.0 / D_MODEL)
        dxn = dy * gg
        dx = r * (dxn - xn * jnp.mean(dxn * xn, axis=-1, keepdims=True))
        dx_ref[...] = dx
        dxb_ref[...] = dx.astype(BF16)
        dg = jnp.sum(dy * xn, axis=0, keepdims=True)
        lane0 = lax.broadcasted_iota(jnp.int32, (1, LANES), 1) == 0
        lpart = jnp.where(lane0, part, 0.0)

        @pl.when(i == 0)
        def _():
            dg_ref[...] = dg
            loss_ref[...] = lpart

        @pl.when(i > 0)
        def _():
            dg_ref[...] += dg
            loss_ref[...] += lpart

    return pl.pallas_call(
        body, name="loss_head", grid=(SEQ // ROWS,),
        in_specs=[_row_spec(D_MODEL), _full_spec((1, D_MODEL)), _row_spec(D_MODEL)],
        out_specs=[_row_spec(D_MODEL), _row_spec(D_MODEL), _full_spec((1, D_MODEL)), _full_spec((1, LANES))],
        out_shape=[_sds((SEQ, D_MODEL), F32), _sds((SEQ, D_MODEL), BF16),
                   _sds((1, D_MODEL), F32), _sds((1, LANES), F32)],
        compiler_params=_params(("arbitrary",)),
    )(x3, g, target)


def _rms_norm_bwd(name, dh, x, g, dres, with_bf16, comm=None):
    def body(dh_ref, x_ref, g_ref, dr_ref, *outs):
        i = pl.program_id(0)
        dx_ref = outs[0]
        dg_ref = outs[-1]
        xf = x_ref[...]
        r = lax.rsqrt(jnp.mean(xf * xf, axis=-1, keepdims=True) + EPS)
        xn = xf * r
        dh = dh_ref[...]
        dxn = dh * g_ref[...]
        dx = dr_ref[...] + r * (dxn - xn * jnp.mean(dxn * xn, axis=-1, keepdims=True))
        dx_ref[...] = dx
        if with_bf16:
            outs[1][...] = dx.astype(BF16)
        dg = jnp.sum(dh * xn, axis=0, keepdims=True)

        @pl.when(i == 0)
        def _():
            dg_ref[...] = dg

        @pl.when(i > 0)
        def _():
            dg_ref[...] += dg

    row = _row_spec(D_MODEL)
    out_specs = [row] + ([row] if with_bf16 else []) + [_full_spec((1, D_MODEL))]
    out_shape = ([_sds((SEQ, D_MODEL), F32)] + ([_sds((SEQ, D_MODEL), BF16)] if with_bf16 else [])
                 + [_sds((1, D_MODEL), F32)])
    return _pcall(body, name, (SEQ // ROWS,), [row, row, _full_spec((1, D_MODEL)), row], out_specs, out_shape,
                  [dh, x, g, dres], (), ("arbitrary",), comm)


def _merge_bwd(dx2b, w_o, conv_out, attn_out, proj):
    bm, bn = 1024, D_MODEL // 2

    def body(dx_ref, w_ref, co_ref, ao_ref, gc_ref, ga_ref, dco_ref, dao_ref, dgc_ref, dga_ref):
        dm = lax.dot_general(dx_ref[...], w_ref[...], (NT, ((), ())), preferred_element_type=F32)
        sc = jax.nn.sigmoid(gc_ref[...])
        sa = jax.nn.sigmoid(ga_ref[...])
        dco_ref[...] = (dm * sc).astype(BF16)
        dao_ref[...] = (dm * sa).astype(BF16)
        dgc_ref[...] = (dm * co_ref[...] * (sc * (1.0 - sc))).astype(BF16)
        dga_ref[...] = (dm * ao_ref[...] * (sa * (1.0 - sa))).astype(BF16)

    own = pl.BlockSpec((bm, bn), lambda i, j: (i, j))
    sd = _sds((SEQ, D_MODEL), BF16)
    return pl.pallas_call(
        body, name="mm_dmerged", grid=(SEQ // bm, D_MODEL // bn),
        in_specs=[pl.BlockSpec((bm, D_MODEL), lambda i, j: (i, 0)), pl.BlockSpec((bn, D_MODEL), lambda i, j: (j, 0)),
                  own, own,
                  pl.BlockSpec((bm, bn), lambda i, j: (i, COL_GC // bn + j)),
                  pl.BlockSpec((bm, bn), lambda i, j: (i, COL_GA // bn + j))],
        out_specs=[own, own, own, own], out_shape=[sd, sd, sd, sd],
        compiler_params=_params(("parallel", "parallel")),
    )(dx2b, w_o, conv_out, attn_out, proj, proj)


def _conv_bwd(dconv_y, proj, conv_w, comm=None):
    nblk = D_MODEL // CONV_COLS

    def body(dy_ref, cb_ref, cc_ref, cx_ref, w_ref, dcb_ref, dcc_ref, dcx_ref, dw_ref):
        cc = cc_ref[...]
        cx = cx_ref[...]
        u = cc * cx
        w = w_ref[...]
        u1 = _shift_rows(u, 1)
        u2 = _shift_rows(u, 2)
        cv = w[0:1, :] * u2 + w[1:2, :] * u1 + w[2:3, :] * u
        dy = dy_ref[...]
        dcb_ref[...] = (dy * cv).astype(BF16)
        dcv = dy * cb_ref[...]
        rows = lax.broadcasted_iota(jnp.int32, dcv.shape, 0)
        up1 = jnp.where(rows < SEQ - 1, pltpu.roll(dcv, SEQ - 1, axis=0), 0.0)
        up2 = jnp.where(rows < SEQ - 2, pltpu.roll(dcv, SEQ - 2, axis=0), 0.0)
        du = w[2:3, :] * dcv + w[1:2, :] * up1 + w[0:1, :] * up2
        dcc_ref[...] = (du * cx).astype(BF16)
        dcx_ref[...] = (du * cc).astype(BF16)
        dw_ref[...] = jnp.concatenate(
            [jnp.sum(dcv * u2, axis=0, keepdims=True),
             jnp.sum(dcv * u1, axis=0, keepdims=True),
             jnp.sum(dcv * u, axis=0, keepdims=True)], axis=0)

    def col(part):
        return pl.BlockSpec((SEQ, CONV_COLS), lambda j: (0, part * nblk + j))

    own = pl.BlockSpec((SEQ, CONV_COLS), lambda j: (0, j))
    wsp = pl.BlockSpec((3, CONV_COLS), lambda j: (0, j))
    sd = _sds((SEQ, D_MODEL), BF16)
    return _pcall(
        body, "conv_bwd", (nblk,), [own, col(0), col(1), col(2), wsp], [own, own, own, wsp],
        [sd, sd, sd, _sds((3, D_MODEL), F32)], [dconv_y, proj, proj, proj, conv_w], (), ("parallel",), comm)


def _attn_bwd(proj, dattn, sinks, tables, comm=None):
    def body(sink_ref, q_ref, kp_ref, kc_ref, vp_ref, vc_ref, c_ref, su_ref, sd_ref, cp_ref, sup_ref, sdp_ref,
             do_ref, dq_ref, dkp_ref, dkc_ref, dvp_ref, dvc_ref, ds_ref):
        n = pl.program_id(0)
        mask = _group_mask(n)
        tabs_cur = (c_ref, su_ref, sd_ref)
        k, v = _band_kv(kp_ref, kc_ref, vp_ref, vc_ref, tabs_cur, (cp_ref, sup_ref, sdp_ref))
        lane = lax.broadcasted_iota(jnp.int32, (1, LANES), 1)
        dsink = jnp.zeros((1, LANES), F32)
        c, su, sd = c_ref[:, :LANES], su_ref[:, :LANES], sd_ref[:, :LANES]
        dk_tiles = [None] * (N_KV_HEADS // HEADS_PER_TILE)
        dv_tiles = [None] * (N_KV_HEADS // HEADS_PER_TILE)
        for h in range(N_KV_HEADS):
            k_halves = _head_tiles(k, h)
            v_halves = _head_tiles(v, h)
            tiles = [h * TILES_PER_GROUP + t for t in range(TILES_PER_GROUP)]
            q_rows = _query_tiles(q_ref, tiles, tabs_cur)
            do_rows = jnp.concatenate([do_ref[:, t * LANES:(t + 1) * LANES] for t in tiles], axis=0)
            dk_par, dv_par = [], []
            dq_rows = None
            for par in range(HEADS_PER_TILE):
                p, p_sink = _head_softmax(q_rows, k_halves[par], _sink_row(sink_ref, tiles, par), mask)
                dp = lax.dot_general(v_halves[par], do_rows, (NT, ((), ())), preferred_element_type=F32)
                delta = jnp.sum(p * dp, axis=0, keepdims=True)
                ds = (p * (dp - delta) * ATTN_SCALE).astype(BF16)
                dq = lax.dot_general(ds, k_halves[par], (TN, ((), ())), preferred_element_type=F32)
                dq_rows = dq if dq_rows is None else dq_rows + dq
                dk_par.append(jnp.dot(ds, q_rows, preferred_element_type=F32))
                dv_par.append(jnp.dot(p.astype(BF16), do_rows, preferred_element_type=F32))
                sink_grad = p_sink * delta
                for i, tile in enumerate(tiles):
                    val = -jnp.sum(sink_grad[:, i * BLOCK:(i + 1) * BLOCK], axis=1, keepdims=True)
                    dsink = dsink + jnp.where(lane == tile * HEADS_PER_TILE + par, val, 0.0)
            for i, tile in enumerate(tiles):
                dq_tile = dq_rows[i * BLOCK:(i + 1) * BLOCK, :]
                dq_ref[:, tile * LANES:(tile + 1) * LANES] = _rotate(dq_tile, c, -su, -sd).astype(BF16)
            own = h % HEADS_PER_TILE
            for par_grads, tiles in ((dk_par, dk_tiles), (dv_par, dv_tiles)):
                shifted = pltpu.roll(par_grads[1 - own], HEAD_DIM, axis=1)
                total = jnp.where(_lane_half(shifted.shape, own), par_grads[own] + shifted, 0.0)
                i = h // HEADS_PER_TILE
                tiles[i] = total if tiles[i] is None else tiles[i] + total
        for i in range(N_KV_HEADS // HEADS_PER_TILE):
            cols = slice(i * LANES, (i + 1) * LANES)
            dkp_ref[:, cols] = dk_tiles[i][:BLOCK, :]
            dkc_ref[:, cols] = dk_tiles[i][BLOCK:, :]
            dvp_ref[:, cols] = dv_tiles[i][:BLOCK, :]
            dvc_ref[:, cols] = dv_tiles[i][BLOCK:, :]

        @pl.when(n == 0)
        def _():
            ds_ref[...] = dsink

        @pl.when(n > 0)
        def _():
            ds_ref[...] += dsink

    blk = pl.BlockSpec((BLOCK, D_KV), lambda n: (n, 0))
    kv = _sds((SEQ, D_KV), F32)
    return _pcall(
        body, "attn_bwd", (N_QBLK,),
        [pl.BlockSpec(memory_space=pltpu.SMEM)] + _attn_specs() + [pl.BlockSpec((BLOCK, D_ATTN), lambda n: (n, 0))],
        [pl.BlockSpec((BLOCK, D_ATTN), lambda n: (n, 0)), blk, blk, blk, blk, _full_spec((1, LANES))],
        [_sds((SEQ, D_ATTN), BF16), kv, kv, kv, kv, _sds((1, LANES), F32)],
        [sinks] + [proj] * 5 + list(tables) * 2 + [dattn], (), ("arbitrary",), comm)


def _kv_grad_combine(dk_prev, dk_cur, dv_prev, dv_cur, tables):
    def body(kp_ref, kc_ref, vp_ref, vc_ref, c_ref, su_ref, sd_ref, o_ref):
        m = pl.program_id(0)
        has_next = m < N_QBLK - 1
        dk = kc_ref[...] + jnp.where(has_next, kp_ref[...], 0.0)
        dv = vc_ref[...] + jnp.where(has_next, vp_ref[...], 0.0)
        o_ref[:, :D_KV] = _rotate(dk, c_ref[...], -su_ref[...], -sd_ref[...]).astype(BF16)
        o_ref[:, D_KV:] = dv.astype(BF16)

    cur = pl.BlockSpec((BLOCK, D_KV), lambda m: (m, 0))
    nxt = pl.BlockSpec((BLOCK, D_KV), lambda m: (jnp.minimum(m + 1, N_QBLK - 1), 0))
    return pl.pallas_call(
        body, name="kv_grad_combine", grid=(N_QBLK,),
        in_specs=[nxt, cur, nxt, cur, cur, cur, cur],
        out_specs=pl.BlockSpec((BLOCK, 2 * D_KV), lambda m: (m, 0)),
        out_shape=_sds((SEQ, 2 * D_KV), BF16),
        compiler_params=_params(("parallel",)),
    )(dk_prev, dk_cur, dv_prev, dv_cur, *tables)


MATRICES = {
    "w_in": (D_MODEL, N_IN // N_CHIPS, "col"),
    "w_conv_out": (D_MODEL // N_CHIPS, D_MODEL, "row"),
    "w_attn_out": (D_MODEL // N_CHIPS, D_MODEL, "row"),
    "w_o": (D_MODEL // N_CHIPS, D_MODEL, "row"),
    "w_gate_up": (D_MODEL, 2 * D_FF // N_CHIPS, "col"),
    "w_down": (D_FF // N_CHIPS, D_MODEL, "row"),
}
BF16_ROW_TILE = 16
CONV_W_COLS = D_MODEL // N_CHIPS
SMALL_ROWS = 8


def _whole_shape(spec):
    rows, cols, kind = spec
    return (rows, cols * N_CHIPS) if kind == "col" else (rows * N_CHIPS, cols)


def _half_shape(spec):
    return (spec[0] // 2, spec[1])


def _aligned(start, multiple):
    return start if isinstance(start, int) else pl.multiple_of(start, multiple)


def _region(ref, spec, shard, half, part=0, parts=1):
    rows, cols, kind = spec
    hr = rows // 2
    n = hr // parts
    if kind == "col":
        return ref.at[pl.ds(_aligned(half * hr + part * n, BF16_ROW_TILE), n),
                      pl.ds(_aligned(shard * cols, LANES), cols)]
    return ref.at[pl.ds(_aligned(shard * rows + half * hr + part * n, BF16_ROW_TILE), n), :]


def _position():
    x, y, c = lax.axis_index("x"), lax.axis_index("y"), lax.axis_index("c")
    chips = [(1 - x, y), (x, 1 - y), (1 - x, 1 - y)]
    return x, y, c, chips


def _shard_of(chip):
    return 2 * chip[0] + chip[1]


def _remote(src, dst, send_sem, recv_sem, to):
    return pltpu.make_async_remote_copy(src_ref=src, dst_ref=dst, send_sem=send_sem, recv_sem=recv_sem,
                                        device_id=to, device_id_type=MESH)


CAST_STEPS = 4


def _to_bf16_in_whole(ws, specs, shard):
    n = len(ws)

    def body(s_ref, *refs):
        del s_ref
        for w_ref, o_ref in zip(refs[:n], refs[n:]):
            o_ref[...] = w_ref[...].astype(BF16)

    def out_spec(spec):
        rows = spec[0] // CAST_STEPS
        if spec[2] == "col":
            return pl.BlockSpec((rows, spec[1]), lambda i, s_ref: (i, s_ref[0]))
        return pl.BlockSpec((rows, spec[1]), lambda i, s_ref: (s_ref[0] * CAST_STEPS + i, 0))

    grid_spec = pltpu.PrefetchScalarGridSpec(
        num_scalar_prefetch=1, grid=(CAST_STEPS,),
        in_specs=[pl.BlockSpec((s[0] // CAST_STEPS, s[1]), lambda i, s_ref: (i, 0)) for s in specs],
        out_specs=[out_spec(s) for s in specs])
    return list(pl.pallas_call(
        body, name="cast_shards", grid_spec=grid_spec, out_shape=[_sds(_whole_shape(s), BF16) for s in specs],
        compiler_params=_params(("parallel",)),
    )(shard, *ws))


class _Gather:
    def __init__(self, wholes, pieces, conv_w=None):
        self.pieces = pieces
        self.n = len(wholes)
        self.with_conv_w = conv_w is not None
        self.operands = list(wholes) + ([conv_w] if self.with_conv_w else [])
        self.out_shape = [_sds(w.shape, w.dtype) for w in wholes]
        if self.with_conv_w:
            self.out_shape.append(_sds((3, D_MODEL), F32))
        self.aliases = {i: i for i in range(self.n)}
        n_ici = 3 * len(pieces)
        self.sems = [pltpu.SemaphoreType.DMA((n_ici,))] * 4
        if self.with_conv_w:
            self.sems += [pltpu.SemaphoreType.DMA((1,)), pltpu.SemaphoreType.DMA((3,)), pltpu.SemaphoreType.DMA((3,))]

    def _conv_w(self, cins, couts, sems, with_recvs):
        cw_in, cw_out = cins[self.n], couts[self.n]
        x, y, c, chips = _position()

        def cols(shard):
            return cw_out.at[:, pl.ds(_aligned(shard * CONV_W_COLS, LANES), CONV_W_COLS)]

        me = _shard_of((x, y))
        local = pltpu.make_async_copy(cw_in, cols(me), sems[4].at[0])
        sends = [_remote(cw_in, cols(me), sems[5].at[j], sems[6].at[j], (*chip, c)) for j, chip in enumerate(chips)]
        if not with_recvs:
            return local, sends, []
        recvs = [_remote(cols(_shard_of(chip)), cols(_shard_of(chip)), sems[5].at[j], sems[6].at[j], (*chip, c))
                 for j, chip in enumerate(chips)]
        return local, sends, recvs

    def start(self, cins, couts, sems):
        x, y, c, chips = _position()
        me = _shard_of((x, y))
        if self.with_conv_w:
            local, sends, _ = self._conv_w(cins, couts, sems, False)
            local.start()
            for cp in sends:
                cp.start()
        for p, (i, spec, part, parts) in enumerate(self.pieces):
            mine = _region(couts[i], spec, me, c, part, parts)
            for j, chip in enumerate(chips):
                _remote(mine, mine, sems[0].at[3 * p + j], sems[1].at[3 * p + j], (*chip, c)).start()

    def finish(self, cins, couts, sems):
        x, y, c, chips = _position()
        me = _shard_of((x, y))
        sibling = (x, y, 1 - c)
        send_a, recv_a, send_b, recv_b = sems[:4]
        passed = []
        for p, (i, spec, part, parts) in enumerate(self.pieces):
            for j, chip in enumerate(chips):
                k = 3 * p + j
                landed = _region(couts[i], spec, _shard_of(chip), c, part, parts)
                _remote(landed, landed, send_a.at[k], recv_a.at[k], (*chip, c)).wait_recv()
                cp = _remote(landed, landed, send_b.at[k], recv_b.at[k], sibling)
                cp.start()
                passed.append(cp)
        for p, (i, spec, part, parts) in enumerate(self.pieces):
            mine = _region(couts[i], spec, me, c, part, parts)
            for j, chip in enumerate(chips):
                k = 3 * p + j
                other = _region(couts[i], spec, _shard_of(chip), 1 - c, part, parts)
                _remote(other, other, send_b.at[k], recv_b.at[k], sibling).wait_recv()
                _remote(mine, mine, send_a.at[k], recv_a.at[k], (*chip, c)).wait_send()
        for cp in passed:
            cp.wait_send()
        if self.with_conv_w:
            local, sends, recvs = self._conv_w(cins, couts, sems, True)
            for cp in recvs:
                cp.wait_recv()
            for cp in sends:
                cp.wait_send()
            local.wait()


def _mm_in_gather(h1, w_whole, comm):
    spec = MATRICES["w_in"]
    cols = spec[1]
    bm = SEQ // 2

    def body(h_ref, w_in_ref, proj_ref, w_ref, wbuf, obuf, send_a, recv_a, send_b, recv_b, load_sem, store_sems):
        del w_in_ref
        s, mi = pl.program_id(0), pl.program_id(1)
        x, y, c, chips = _position()
        me = _shard_of((x, y))
        sibling = (x, y, 1 - c)
        mine = _region(w_ref, spec, me, c)

        @pl.when((s == 0) & (mi == 0))
        def _():
            for j, chip in enumerate(chips):
                _remote(mine, mine, send_a.at[j], recv_a.at[j], (*chip, c)).start()

        shard = me
        for j, chip in enumerate(chips):
            shard = jnp.where(s == j + 1, _shard_of(chip), shard)

            @pl.when((s == j + 1) & (mi == 0))
            def _():
                landed = _region(w_ref, spec, _shard_of(chip), c)
                _remote(landed, landed, send_a.at[j], recv_a.at[j], (*chip, c)).wait_recv()
                _remote(landed, landed, send_b.at[j], recv_b.at[j], sibling).start()
                other = _region(w_ref, spec, _shard_of(chip), 1 - c)
                _remote(other, other, send_b.at[j], recv_b.at[j], sibling).wait_recv()

        col0 = pl.multiple_of(shard * cols, LANES)

        @pl.when(mi == 0)
        def _():
            load = pltpu.make_async_copy(w_ref.at[:, pl.ds(col0, cols)], wbuf, load_sem.at[0])
            load.start()
            load.wait()

        def store():
            rows = pl.ds(pl.multiple_of(mi * bm, bm), bm)
            return pltpu.make_async_copy(obuf.at[mi], proj_ref.at[rows, pl.ds(col0, cols)], store_sems.at[mi])

        @pl.when(s > 0)
        def _():
            store().wait()

        obuf[mi] = jnp.dot(h_ref[...], wbuf[...], preferred_element_type=F32)
        store().start()

        @pl.when(s == N_CHIPS - 1)
        def _():
            store().wait()

        @pl.when((s == N_CHIPS - 1) & (mi == 1))
        def _():
            for j, chip in enumerate(chips):
                landed = _region(w_ref, spec, _shard_of(chip), c)
                _remote(mine, mine, send_a.at[j], recv_a.at[j], (*chip, c)).wait_send()
                _remote(landed, landed, send_b.at[j], recv_b.at[j], sibling).wait_send()

    sem3 = pltpu.SemaphoreType.DMA((3,))
    (proj, whole), extra = _pcall(
        body, "mm_in", (N_CHIPS, SEQ // bm),
        [pl.BlockSpec((bm, D_MODEL), lambda s, m: (m, 0)), HBM_SPEC], [HBM_SPEC, HBM_SPEC],
        [_sds((SEQ, N_IN), F32), _sds(w_whole.shape, w_whole.dtype)], [h1, w_whole],
        [pltpu.VMEM((D_MODEL, cols), BF16), pltpu.VMEM((SEQ // bm, bm, cols), F32), sem3, sem3, sem3, sem3,
         pltpu.SemaphoreType.DMA((1,)), pltpu.SemaphoreType.DMA((SEQ // bm,))],
        None, comm, aliases={1: 1}, start_after_body=True)
    return proj, whole, extra


def _pack_small(dg_mix, dg_ffn, dg_final, dconv_w, dsinks, loss_row):
    def body(a_ref, b_ref, c_ref, w_ref, s_ref, l_ref, o_ref):
        pad = jnp.zeros((1, D_MODEL - LANES), F32)
        o_ref[0:1, :] = a_ref[...]
        o_ref[1:2, :] = b_ref[...]
        o_ref[2:3, :] = c_ref[...]
        o_ref[3:6, :] = w_ref[...]
        o_ref[6:7, :] = jnp.concatenate([s_ref[...], pad], axis=1)
        o_ref[7:8, :] = jnp.concatenate([l_ref[...], pad], axis=1)

    return pl.pallas_call(
        body, name="pack_small", out_shape=_sds((SMALL_ROWS, D_MODEL), F32),
        compiler_params=_params(),
    )(dg_mix, dg_ffn, dg_final, dconv_w, dsinks, loss_row)


class _Pair:
    def __init__(self, dws, specs):
        self.specs = specs
        self.operands = list(dws)
        self.out_shape = [_sds((N_CHIPS, *_half_shape(s)), BF16) for s in specs]
        self.aliases = {}
        n = N_CHIPS * len(specs)
        self.sems = [pltpu.SemaphoreType.DMA((n,)), pltpu.SemaphoreType.DMA((n,))]

    def _copies(self, cins, couts, sems):
        x, y, c, _ = _position()
        sibling = (x, y, 1 - c)
        for i, spec in enumerate(self.specs):
            for t in range(N_CHIPS):
                k = N_CHIPS * i + t
                yield _remote(_region(cins[i], spec, t, 1 - c), couts[i].at[t], sems[0].at[k], sems[1].at[k], sibling)

    def start(self, cins, couts, sems):
        for cp in self._copies(cins, couts, sems):
            cp.start()

    def finish(self, cins, couts, sems):
        for cp in self._copies(cins, couts, sems):
            cp.wait()


class _SmallAllToAll:
    def __init__(self, small):
        self.operands = [small]
        self.out_shape = [_sds((N_DEV, SMALL_ROWS, D_MODEL), F32)]
        self.aliases = {}
        self.sems = [pltpu.SemaphoreType.DMA((N_DEV - 1,)), pltpu.SemaphoreType.DMA((N_DEV - 1,)),
                     pltpu.SemaphoreType.DMA((1,))]

    def _copies(self, cins, couts, sems):
        x, y, c, _ = _position()
        me = 4 * x + 2 * y + c
        out = []
        for r in range(1, N_DEV):
            flip = ((r >> 2) & 1, (r >> 1) & 1, r & 1)
            peer = tuple(1 - p if f else p for p, f in zip((x, y, c), flip))
            theirs = couts[0].at[4 * peer[0] + 2 * peer[1] + peer[2]]
            out.append((_remote(cins[0], couts[0].at[me], sems[0].at[r - 1], sems[1].at[r - 1], peer),
                        functools.partial(_remote, theirs, theirs, sems[0].at[r - 1], sems[1].at[r - 1], peer)))
        return pltpu.make_async_copy(cins[0], couts[0].at[me], sems[2].at[0]), out

    def start(self, cins, couts, sems):
        own, copies = self._copies(cins, couts, sems)
        own.start()
        for send, _ in copies:
            send.start()

    def finish(self, cins, couts, sems):
        own, copies = self._copies(cins, couts, sems)
        for send, recv in copies:
            recv().wait_recv()
            send.wait_send()
        own.wait()


class _Both:
    def __init__(self, a, b):
        self.a, self.b = a, b
        self.operands = list(a.operands) + list(b.operands)
        self.out_shape = list(a.out_shape) + list(b.out_shape)
        self.aliases = dict(a.aliases)
        self.aliases.update({len(a.operands) + k: len(a.out_shape) + v for k, v in b.aliases.items()})
        self.sems = list(a.sems) + list(b.sems)

    def _split(self, cins, couts, sems):
        na, ma, sa = len(self.a.operands), len(self.a.out_shape), len(self.a.sems)
        return (cins[:na], couts[:ma], sems[:sa]), (cins[na:], couts[ma:], sems[sa:])

    def start(self, cins, couts, sems):
        for plan, args in zip((self.a, self.b), self._split(cins, couts, sems)):
            plan.start(*args)

    def finish(self, cins, couts, sems):
        for plan, args in zip((self.a, self.b), self._split(cins, couts, sems)):
            plan.finish(*args)


def _pair_sum(name, specs, dws, got, place):
    n_mat = len(specs)

    def body(p_ref, *refs):
        t = pl.program_id(0)
        mine, theirs = refs[:n_mat], refs[n_mat:2 * n_mat]
        outs, owns = refs[2 * n_mat:3 * n_mat], refs[3 * n_mat:]
        for a, b, o, own in zip(mine, theirs, outs, owns):
            s = (a[...].astype(F32) + b[...].astype(F32)).astype(BF16)
            o[...] = s

            @pl.when(t == p_ref[1])
            def _():
                own[...] = s

    def mine_spec(spec):
        hr, cols = _half_shape(spec)
        if spec[2] == "col":
            return pl.BlockSpec((hr, cols), lambda t, p_ref: (p_ref[0], t))
        return pl.BlockSpec((hr, cols), lambda t, p_ref: (2 * t + p_ref[0], 0))

    def slot_spec(spec):
        return pl.BlockSpec((None, *_half_shape(spec)), lambda t, p_ref: (t, 0, 0))

    def own_spec(spec):
        return pl.BlockSpec((None, *_half_shape(spec)), lambda t, p_ref: (p_ref[1], 0, 0))

    slots = [_sds((N_CHIPS, *_half_shape(s)), BF16) for s in specs]
    grid_spec = pltpu.PrefetchScalarGridSpec(
        num_scalar_prefetch=1, grid=(N_CHIPS,),
        in_specs=[mine_spec(s) for s in specs] + [slot_spec(s) for s in specs],
        out_specs=[slot_spec(s) for s in specs] + [own_spec(s) for s in specs])
    res = pl.pallas_call(
        body, name=name, grid_spec=grid_spec, out_shape=slots + slots,
        compiler_params=_params(("arbitrary",)),
    )(place, *dws, *got)
    return list(res[:n_mat]), list(res[n_mat:])


class _ChipExchange:
    def __init__(self, sums, slots, part=0, parts=1):
        self.n = len(sums)
        self.part, self.parts = part, parts
        self.operands = list(sums) + list(slots)
        self.out_shape = [_sds(s.shape, s.dtype) for s in slots]
        self.aliases = {self.n + i: i for i in range(self.n)}
        self.sems = [pltpu.SemaphoreType.DMA((3 * self.n,)), pltpu.SemaphoreType.DMA((3 * self.n,))]

    def _rows(self, ref, slot):
        n = ref.shape[1] // self.parts
        return ref.at[slot, pl.ds(self.part * n, n), :]

    def _copies(self, cins, couts, sems):
        x, y, c, chips = _position()
        me = _shard_of((x, y))
        for i in range(self.n):
            for j, chip in enumerate(chips):
                k = 3 * i + j
                theirs = self._rows(couts[i], _shard_of(chip))
                yield (_remote(self._rows(cins[i], _shard_of(chip)), self._rows(couts[i], me),
                               sems[0].at[k], sems[1].at[k], (*chip, c)),
                       functools.partial(_remote, theirs, theirs, sems[0].at[k], sems[1].at[k], (*chip, c)))

    def start(self, cins, couts, sems):
        for send, _ in self._copies(cins, couts, sems):
            send.start()

    def finish(self, cins, couts, sems):
        for send, recv in self._copies(cins, couts, sems):
            recv().wait_recv()
            send.wait_send()


def _chip_sum(name, specs, slots, core):
    steps = 2
    n_mat = len(specs)

    def body(c_ref, *refs):
        del c_ref
        ins, outs = refs[:n_mat], refs[n_mat:]
        for a, o in zip(ins, outs):
            acc = a[0].astype(F32)
            for t in range(1, N_CHIPS):
                acc = acc + a[t].astype(F32)
            o[...] = acc

    def in_spec(spec):
        hr, cols = _half_shape(spec)
        return pl.BlockSpec((N_CHIPS, hr // steps, cols), lambda i, c_ref: (0, i, 0))

    def out_spec(spec):
        hr, cols = _half_shape(spec)
        return pl.BlockSpec((hr // steps, cols), lambda i, c_ref: (c_ref[0] * steps + i, 0))

    grid_spec = pltpu.PrefetchScalarGridSpec(
        num_scalar_prefetch=1, grid=(steps,),
        in_specs=[in_spec(s) for s in specs], out_specs=[out_spec(s) for s in specs])
    return list(pl.pallas_call(
        body, name=name, grid_spec=grid_spec,
        out_shape=[_sds((s[0], s[1]), F32) for s in specs],
        compiler_params=_params(("parallel",)),
    )(core, *slots))


class _HalfExchange:
    def __init__(self, grads, specs):
        self.specs = specs
        self.operands = list(grads)
        self.out_shape = [_sds(g.shape, g.dtype) for g in grads]
        self.aliases = {i: i for i in range(len(grads))}
        self.sems = [pltpu.SemaphoreType.DMA((len(grads),)), pltpu.SemaphoreType.DMA((len(grads),))]

    def _copies(self, couts, sems):
        x, y, c, _ = _position()
        sibling = (x, y, 1 - c)
        for i, spec in enumerate(self.specs):
            hr = spec[0] // 2
            mine = couts[i].at[pl.ds(_aligned(c * hr, 8), hr), :]
            theirs = couts[i].at[pl.ds(_aligned((1 - c) * hr, 8), hr), :]
            yield (_remote(mine, mine, sems[0].at[i], sems[1].at[i], sibling),
                   functools.partial(_remote, theirs, theirs, sems[0].at[i], sems[1].at[i], sibling))

    def start(self, cins, couts, sems):
        for send, _ in self._copies(couts, sems):
            send.start()

    def finish(self, cins, couts, sems):
        for send, recv in self._copies(couts, sems):
            recv().wait_recv()
            send.wait_send()


def _small_sum(blocks):
    def body(b_ref, o_ref):
        acc = b_ref[0]
        for d in range(1, N_DEV):
            acc = acc + b_ref[d]
        o_ref[...] = acc

    return pl.pallas_call(
        body, name="small_sum", out_shape=_sds((SMALL_ROWS, D_MODEL), F32), compiler_params=_params(),
    )(blocks)


def _adamw(name, params, steps):
    n = len(params)

    def body(*refs):
        for p in range(n):
            w_ref, g_ref, m_ref, v_ref = refs[4 * p:4 * p + 4]
            d_ref, nm_ref, nv_ref, go_ref = refs[4 * n + 4 * p:4 * n + 4 * p + 4]
            g = g_ref[...]
            go_ref[...] = g
            m = ADAM_B1 * m_ref[...] + (1.0 - ADAM_B1) * g
            v = ADAM_B2 * v_ref[...] + (1.0 - ADAM_B2) * jnp.square(g)
            m_hat = m / (1.0 - ADAM_B1 ** ADAM_STEP)
            v_hat = v / (1.0 - ADAM_B2 ** ADAM_STEP)
            d_ref[...] = -ADAM_LR * (m_hat / (jnp.sqrt(v_hat) + ADAM_EPS) + ADAM_WD * w_ref[...])
            nm_ref[...] = m
            nv_ref[...] = v

    in_specs, out_specs, out_shape, operands = [], [], [], []
    for w, g, m, v in params:
        spec = pl.BlockSpec((w.shape[0] // steps, w.shape[1]), lambda i: (i, 0))
        in_specs += [spec] * 4
        out_specs += [spec] * 4
        out_shape += [_sds(w.shape, F32)] * 4
        operands += [w, g, m, v]
    outs = _pcall(body, name, (steps,), in_specs, out_specs, out_shape, operands, (), ("parallel",))
    return [tuple(outs[4 * p:4 * p + 4]) for p in range(n)]


MATRIX_NAMES = tuple(MATRICES)
WEIGHT_ORDER = ("g_mix", "w_in", "conv_w", "attn_sinks", "w_conv_out", "w_attn_out", "w_o", "g_ffn",
                "w_gate_up", "w_down", "g_final")


def kernel(x, g_mix, w_in, conv_w, attn_sinks, w_conv_out, w_attn_out, w_o, g_ffn, w_gate_up, w_down, g_final, loss_target, m_g_mix, m_w_in, m_conv_w, m_attn_sinks, m_w_conv_out, m_w_attn_out, m_w_o, m_g_ffn, m_w_gate_up, m_w_down, m_g_final, v_g_mix, v_w_in, v_conv_w, v_attn_sinks, v_w_conv_out, v_w_attn_out, v_w_o, v_g_ffn, v_w_gate_up, v_w_down, v_g_final):
    w = dict(g_mix=g_mix, w_in=w_in[0], conv_w=conv_w[0], attn_sinks=attn_sinks, w_conv_out=w_conv_out[0],
             w_attn_out=w_attn_out[0], w_o=w_o[0], g_ffn=g_ffn, w_gate_up=w_gate_up[0], w_down=w_down[0],
             g_final=g_final[None, :])
    m = dict(g_mix=m_g_mix, w_in=m_w_in[0], conv_w=m_conv_w[0], attn_sinks=m_attn_sinks,
             w_conv_out=m_w_conv_out[0], w_attn_out=m_w_attn_out[0], w_o=m_w_o[0], g_ffn=m_g_ffn,
             w_gate_up=m_w_gate_up[0], w_down=m_w_down[0], g_final=m_g_final[None, :])
    v = dict(g_mix=v_g_mix, w_in=v_w_in[0], conv_w=v_conv_w[0], attn_sinks=v_attn_sinks,
             w_conv_out=v_w_conv_out[0], w_attn_out=v_w_attn_out[0], w_o=v_w_o[0], g_ffn=v_g_ffn,
             w_gate_up=v_w_gate_up[0], w_down=v_w_down[0], g_final=v_g_final[None, :])
    shard = (2 * lax.axis_index("x") + lax.axis_index("y")).astype(jnp.int32)
    core = lax.axis_index("c").astype(jnp.int32)
    shard1, core1, place = shard.reshape((1,)), core.reshape((1,)), jnp.stack([core, shard])
    spec = MATRICES
    xs, target, sinks = x[0], loss_target[0], w["attn_sinks"]
    tables = _rope_tables()

    def gather(names, part=0, parts=1):
        return _Gather([whole[n] for n in names], [(i, spec[n], part, parts) for i, n in enumerate(names)])

    def pair(names):
        return _Pair([dw[n] for n in names], [spec[n] for n in names])

    def pair_sum(tag, names, got):
        return _pair_sum("pair_sum_" + tag, [spec[n] for n in names], [dw[n] for n in names], got, place)

    whole = dict(zip(MATRIX_NAMES, _to_bf16_in_whole(
        [w[n] for n in MATRIX_NAMES], [spec[n] for n in MATRIX_NAMES], shard1)))

    mixers = ("w_conv_out", "w_attn_out", "w_o")
    h1 = _rms_norm("norm_mix", xs, w["g_mix"])
    proj, whole["w_in"], (*got, conv_w_whole) = _mm_in_gather(
        h1, whole["w_in"], _Gather([whole[n] for n in mixers], [(i, spec[n], 0, 1) for i, n in enumerate(mixers)],
                                   conv_w=w["conv_w"]))
    whole.update(zip(mixers, got))
    conv_y = _conv_fwd(proj, conv_w_whole)
    attn, (whole["w_gate_up"],) = _attn_fwd(proj, tables, sinks, comm=gather(("w_gate_up",)))
    conv_out, attn_out, merged = _branch_merge(conv_y, attn, whole["w_conv_out"], whole["w_attn_out"], proj)
    x2 = _mm_nn("mm_o", merged, whole["w_o"], 1024, 1024, F32, res=xs)
    h2 = _rms_norm("norm_ffn", x2, w["g_ffn"])
    (gate, up, act), (whole["w_down"],) = _gate_up_fwd(h2, whole["w_gate_up"], comm=gather(("w_down",)))
    x3 = _mm_nn("mm_down", act, whole["w_down"], 1024, 512, F32, res=x2)
    dx3, dx3b, dg_final, loss_row = _loss_head(x3, w["g_final"], target)

    dw = {}
    dw["w_down"] = _mm_tn("mm_dw_down", act, dx3b, 1408, 1024, BF16)
    (dgate, dup), got = _dact_swiglu(dx3b, whole["w_down"], gate, up, comm=pair(("w_down",)))
    sums_a, own_a = pair_sum("down", ("w_down",), got)
    dh2, slots_a = _mm_dh2(dgate, dup, whole["w_gate_up"], comm=_ChipExchange(sums_a, own_a))
    dw["w_gate_up"] = _mm_dw_gate_up(h2, dgate, dup)
    (dx2, dx2b, dg_ffn), got = _rms_norm_bwd("norm_ffn_bwd", dh2, x2, w["g_ffn"], dx3, True, comm=pair(("w_gate_up",)))
    sums_b, own_b = pair_sum("gate_up", ("w_gate_up",), got)
    dw["w_o"] = _mm_tn("mm_dw_o", merged, dx2b, 1024, 1024, BF16)
    dco, dao, dgc, dga = _merge_bwd(dx2b, whole["w_o"], conv_out, attn_out, proj)
    dconv_y = _mm_nt("mm_dconv_y", dco, whole["w_conv_out"], 1024, 1024, D_MODEL, F32)
    dw["w_conv_out"] = _mm_tn("mm_dw_conv_out", conv_y, dco, 1024, 1024, BF16)
    dattn = _mm_nt("mm_dattn", dao, whole["w_attn_out"], 1024, 1024, D_MODEL, BF16)
    dw["w_attn_out"] = _mm_tn("mm_dw_attn_out", attn, dao, 1024, 1024, BF16)
    (dcb, dcc, dcx, dconv_w), got = _conv_bwd(dconv_y, proj, conv_w_whole, comm=pair(mixers))
    sums_c, own_c = pair_sum("mixers", mixers, got)
    (dq, dk_prev, dk_cur, dv_prev, dv_cur, dsinks), slots_b = _attn_bwd(
        proj, dattn, sinks, tables, comm=_ChipExchange(sums_b, own_b))
    dkv = _kv_grad_combine(dk_prev, dk_cur, dv_prev, dv_cur, tables)
    dproj = jnp.concatenate([dcb, dcc, dcx, dq, dkv, dgc, dga], axis=1)
    dw["w_in"], slots_c = _mm_tn("mm_dw_in", h1, dproj, 1024, 1664, BF16, comm=_ChipExchange(sums_c, own_c))
    sums_d, own_d = pair_sum("in", ("w_in",), _comm_call("pair_exchange_in", pair(("w_in",))))
    early = ("w_down", "w_gate_up") + mixers
    halves = _chip_sum("chip_sum_early", [spec[n] for n in early], slots_a + slots_b + slots_c, core1)
    dh1, (own_d, *reduced) = _mm_nt(
        "mm_dh1", dproj, whole["w_in"], 1024, 1024, 1664, F32,
        comm=_Both(_ChipExchange(sums_d, own_d, 0, 2), _HalfExchange(halves, [spec[n] for n in early])))
    g = dict(zip(early, reduced))
    (grad_x, dg_mix), slots_d = _rms_norm_bwd("norm_mix_bwd", dh1, xs, w["g_mix"], dx2, False,
                                              comm=_ChipExchange(sums_d, [own_d], 1, 2))
    small = _pack_small(dg_mix, dg_ffn, dg_final, dconv_w, dsinks, loss_row)
    half_in = _chip_sum("chip_sum_in", [spec["w_in"]], slots_d, core1)
    g["w_in"], small_blocks = _comm_call(
        "half_exchange_in", _Both(_HalfExchange(half_in, [spec["w_in"]]), _SmallAllToAll(small)))
    delta, new_m, new_v = {}, {}, {}

    def keep(names, results):
        for n, (d, nm, nv, grad) in zip(names, results):
            delta[n], new_m[n], new_v[n], g[n] = d, nm, nv, grad

    keep(early, _adamw("adamw_early", [(w[n], g[n], m[n], v[n]) for n in early], 8))
    small_sum = _small_sum(small_blocks)
    g["g_mix"] = small_sum[0:1, :]
    g["g_ffn"] = small_sum[1:2, :]
    g["g_final"] = small_sum[2:3, :]
    g["conv_w"] = lax.dynamic_slice(small_sum, (3, shard * CONV_W_COLS), (3, CONV_W_COLS))
    g["attn_sinks"] = small_sum[6:7, :N_HEADS]
    loss = small_sum[7, 0]
    keep(("w_in",), _adamw("adamw_w_in", [(w["w_in"], g["w_in"], m["w_in"], v["w_in"])], 4))
    rest = ("g_mix", "g_ffn", "g_final", "conv_w", "attn_sinks")
    keep(rest, _adamw("adamw_small", [(w[n], g[n], m[n], v[n]) for n in rest], 1))

    def shaped(vals):
        return [vals[n].reshape((D_MODEL,)) if n == "g_final" else
                (vals[n][None] if n in MATRIX_NAMES or n == "conv_w" else vals[n]) for n in WEIGHT_ORDER]

    return (loss, grad_x[None], *shaped(g), *shaped(delta), *shaped(new_m), *shaped(new_v))
```

```python
import functools
import math

import jax
import jax.numpy as jnp
from jax import lax
from jax.experimental import pallas as pl
from jax.experimental.pallas import tpu as pltpu

F32 = jnp.float32
BF16 = jnp.bfloat16

D_MODEL = 1024
SEQ = 2048
HEAD_DIM = 64
N_HEADS = 16
N_KV_HEADS = 4
GROUP = N_HEADS // N_KV_HEADS
D_ATTN = N_HEADS * HEAD_DIM
D_KV = N_KV_HEADS * HEAD_DIM
BLOCK = 128
ROT_DIM = HEAD_DIM // 4
ROPE_THETA = 500000.0
ATTN_SCALE = 1.0 / math.sqrt(HEAD_DIM)
NEG_INF = -1e30
D_FF = 2816
EPS = 1e-5
N_IN = 3 * D_MODEL + D_ATTN + 2 * D_KV + 2 * D_MODEL
COL_Q = 3 * D_MODEL
COL_K = COL_Q + D_ATTN
COL_V = COL_K + D_KV
COL_GC = COL_V + D_KV
COL_GA = COL_GC + D_MODEL

ADAM_LR = 0.001
ADAM_B1 = 0.9
ADAM_B2 = 0.999
ADAM_EPS = 1e-08
ADAM_WD = 0.01
ADAM_STEP = 10

N_CHIPS = 4
N_DEV = 8

V7X_VMEM_BYTES = 64 * 1024 * 1024
VMEM_LIMIT = (V7X_VMEM_BYTES * 3) // 4
LANES = 128
MESH = pl.DeviceIdType.MESH


def _params(semantics=None):
    return pltpu.CompilerParams(dimension_semantics=semantics, vmem_limit_bytes=VMEM_LIMIT)


def _sds(shape, dtype):
    return jax.ShapeDtypeStruct(shape, dtype)


HBM_SPEC = pl.BlockSpec(memory_space=pl.ANY)


def _pcall(body, name, grid, in_specs, out_specs, out_shape, operands, scratch=(), semantics=None, comm=None,
           aliases=None, start_after_body=False):
    aliases = dict(aliases or {})
    if comm is None:
        return pl.pallas_call(
            body, name=name, grid=grid, in_specs=in_specs, out_specs=out_specs, out_shape=out_shape,
            scratch_shapes=list(scratch), input_output_aliases=aliases,
            compiler_params=_params(semantics))(*operands)
    multi = isinstance(out_shape, (list, tuple))
    o_specs = list(out_specs) if multi else [out_specs]
    o_shape = list(out_shape) if multi else [out_shape]
    n_in, n_out, n_scr = len(operands), len(o_shape), len(scratch)
    n_cin, n_cout = len(comm.operands), len(comm.out_shape)

    def hosted(*refs):
        ins, cins = refs[:n_in], refs[n_in:n_in + n_cin]
        o0 = n_in + n_cin
        outs, couts = refs[o0:o0 + n_out], refs[o0 + n_out:o0 + n_out + n_cout]
        s0 = o0 + n_out + n_cout
        scr, sems = refs[s0:s0 + n_scr], refs[s0 + n_scr:]
        first = last = None
        for axis, size in enumerate(grid):
            i = pl.program_id(axis)
            first = (i == 0) if first is None else first & (i == 0)
            last = (i == size - 1) if last is None else last & (i == size - 1)

        if not start_after_body:
            @pl.when(first)
            def _():
                comm.start(cins, couts, sems)

        body(*ins, *outs, *scr)

        if start_after_body:
            @pl.when(first)
            def _():
                comm.start(cins, couts, sems)

        @pl.when(last)
        def _():
            comm.finish(cins, couts, sems)

    res = pl.pallas_call(
        hosted, name=name, grid=grid,
        in_specs=list(in_specs) + [HBM_SPEC] * n_cin, out_specs=o_specs + [HBM_SPEC] * n_cout,
        out_shape=o_shape + list(comm.out_shape), scratch_shapes=list(scratch) + list(comm.sems),
        input_output_aliases={**aliases, **{n_in + a: n_out + b for a, b in comm.aliases.items()}},
        compiler_params=_params(("arbitrary",) * len(grid)))(*operands, *comm.operands)
    outs = list(res[:n_out])
    return (outs if multi else outs[0]), list(res[n_out:])


def _comm_call(name, comm):
    def body(*refs):
        n_cin, n_cout = len(comm.operands), len(comm.out_shape)
        cins, couts, sems = refs[:n_cin], refs[n_cin:n_cin + n_cout], refs[n_cin + n_cout:]
        comm.start(cins, couts, sems)
        comm.finish(cins, couts, sems)

    return list(pl.pallas_call(
        body, name=name, in_specs=[HBM_SPEC] * len(comm.operands), out_specs=[HBM_SPEC] * len(comm.out_shape),
        out_shape=list(comm.out_shape), scratch_shapes=list(comm.sems),
        input_output_aliases=dict(comm.aliases))(*comm.operands))


NN = ((1,), (0,))
NT = ((1,), (1,))
TN = ((0,), (0,))


def _matmul(name, a, b, dims, grid, a_spec, b_spec, o_spec, o_shape, o_dtype, res=None, res_spec=None, comm=None):
    nk = grid[2]

    def body(*refs):
        if res is None:
            a_ref, b_ref, o_ref = refs[:3]
            r_ref = None
            scratch = refs[3:]
        else:
            a_ref, b_ref, r_ref, o_ref = refs[:4]
            scratch = refs[4:]
        p = lax.dot_general(a_ref[...], b_ref[...], (dims, ((), ())), preferred_element_type=F32)

        def finish(acc):
            if r_ref is not None:
                acc = r_ref[...] + acc
            o_ref[...] = acc.astype(o_dtype)

        if nk == 1:
            finish(p)
        else:
            acc_ref = scratch[0]
            k = pl.program_id(2)

            @pl.when(k == 0)
            def _():
                acc_ref[...] = p

            @pl.when(k > 0)
            def _():
                acc_ref[...] += p

            @pl.when(k == nk - 1)
            def _():
                finish(acc_ref[...])

    operands = [a, b] if res is None else [a, b, res]
    in_specs = [a_spec, b_spec] if res is None else [a_spec, b_spec, res_spec]
    scratch = [pltpu.VMEM(o_spec.block_shape, F32)] if nk > 1 else []
    return _pcall(body, name, grid, in_specs, o_spec, _sds(o_shape, o_dtype), operands, scratch,
                  ("parallel", "parallel", "arbitrary"), comm)


def _mm_nn(name, a, b, bm, bn, o_dtype, res=None, comm=None):
    m, k = a.shape
    n = b.shape[1]
    return _matmul(
        name, a, b, NN, (m // bm, n // bn, 1),
        pl.BlockSpec((bm, k), lambda i, j, kk: (i, 0)),
        pl.BlockSpec((k, bn), lambda i, j, kk: (0, j)),
        pl.BlockSpec((bm, bn), lambda i, j, kk: (i, j)),
        (m, n), o_dtype, res,
        None if res is None else pl.BlockSpec((bm, bn), lambda i, j, kk: (i, j)), comm,
    )


def _mm_nt(name, a, b, bm, bn, bk, o_dtype, comm=None):
    m, k = a.shape
    n = b.shape[0]
    return _matmul(
        name, a, b, NT, (m // bm, n // bn, k // bk),
        pl.BlockSpec((bm, bk), lambda i, j, kk: (i, kk)),
        pl.BlockSpec((bn, bk), lambda i, j, kk: (j, kk)),
        pl.BlockSpec((bm, bn), lambda i, j, kk: (i, j)),
        (m, n), o_dtype, comm=comm,
    )


def _mm_tn(name, a, b, bm, bn, o_dtype, comm=None):
    k, m = a.shape
    n = b.shape[1]
    return _matmul(
        name, a, b, TN, (m // bm, n // bn, 1),
        pl.BlockSpec((k, bm), lambda i, j, kk: (0, i)),
        pl.BlockSpec((k, bn), lambda i, j, kk: (0, j)),
        pl.BlockSpec((bm, bn), lambda i, j, kk: (i, j)),
        (m, n), o_dtype, comm=comm,
    )


ROWS = 256


def _row_spec(width, col=0):
    return pl.BlockSpec((ROWS, width), lambda i: (i, col))


def _full_spec(shape):
    return pl.BlockSpec(shape, lambda *_: (0,) * len(shape))


def _rms_norm(name, x, g):
    def body(x_ref, g_ref, h_ref):
        xf = x_ref[...]
        r = lax.rsqrt(jnp.mean(xf * xf, axis=-1, keepdims=True) + EPS)
        h_ref[...] = ((xf * r) * g_ref[...]).astype(BF16)

    return pl.pallas_call(
        body, name=name, grid=(SEQ // ROWS,),
        in_specs=[_row_spec(D_MODEL), _full_spec((1, D_MODEL))],
        out_specs=_row_spec(D_MODEL),
        out_shape=_sds((SEQ, D_MODEL), BF16),
        compiler_params=_params(("parallel",)),
    )(x, g)


CONV_COLS = 256


def _shift_rows(u, k):
    rows = lax.broadcasted_iota(jnp.int32, u.shape, 0)
    return jnp.where(rows >= k, pltpu.roll(u, k, axis=0), 0.0)


def _conv_fwd(proj, conv_w):
    nblk = D_MODEL // CONV_COLS

    def body(cb_ref, cc_ref, cx_ref, w_ref, y_ref):
        u = cc_ref[...] * cx_ref[...]
        w = w_ref[...]
        cv = w[0:1, :] * _shift_rows(u, 2) + w[1:2, :] * _shift_rows(u, 1) + w[2:3, :] * u
        y_ref[...] = (cb_ref[...] * cv).astype(BF16)

    def col(part):
        return pl.BlockSpec((SEQ, CONV_COLS), lambda j: (0, part * nblk + j))

    return pl.pallas_call(
        body, name="conv_fwd", grid=(nblk,),
        in_specs=[col(0), col(1), col(2), pl.BlockSpec((3, CONV_COLS), lambda j: (0, j))],
        out_specs=pl.BlockSpec((SEQ, CONV_COLS), lambda j: (0, j)),
        out_shape=_sds((SEQ, D_MODEL), BF16),
        compiler_params=_params(("parallel",)),
    )(proj, proj, proj, conv_w)


ROPE_COLS = 256


def _rope_tables():
    inv_freq = ROPE_THETA ** (-jnp.arange(0, ROT_DIM, 2, dtype=F32) / ROT_DIM)
    ang = jnp.arange(SEQ, dtype=F32)[:, None] * inv_freq[None, :]
    cos, sin = jnp.cos(ang), jnp.sin(ang)
    half = ROT_DIM // 2
    ones = jnp.ones((SEQ, HEAD_DIM - ROT_DIM), F32)
    zeros = jnp.zeros((SEQ, HEAD_DIM - ROT_DIM), F32)
    zh = jnp.zeros((SEQ, half), F32)
    c = jnp.concatenate([cos, cos, ones], axis=1)
    s_up = jnp.concatenate([-sin, zh, zeros], axis=1)
    s_dn = jnp.concatenate([zh, sin, zeros], axis=1)
    reps = ROPE_COLS // HEAD_DIM
    return tuple(jnp.tile(t, (1, reps)) for t in (c, s_up, s_dn))


def _rotate(t, c, s_up, s_dn):
    width = t.shape[1]
    half = ROT_DIM // 2
    return t * c + pltpu.roll(t, width - half, axis=1) * s_up + pltpu.roll(t, half, axis=1) * s_dn


N_QBLK = SEQ // BLOCK


def _attn_specs():
    prev = lambda n: jnp.maximum(n - 1, 0)
    q = pl.BlockSpec((BLOCK, D_ATTN), lambda n: (n, COL_Q // D_ATTN))
    k_prev = pl.BlockSpec((BLOCK, D_KV), lambda n: (prev(n), COL_K // D_KV))
    k_cur = pl.BlockSpec((BLOCK, D_KV), lambda n: (n, COL_K // D_KV))
    v_prev = pl.BlockSpec((BLOCK, D_KV), lambda n: (prev(n), COL_V // D_KV))
    v_cur = pl.BlockSpec((BLOCK, D_KV), lambda n: (n, COL_V // D_KV))
    tab_cur = pl.BlockSpec((BLOCK, ROPE_COLS), lambda n: (n, 0))
    tab_prev = pl.BlockSpec((BLOCK, ROPE_COLS), lambda n: (prev(n), 0))
    return [q, k_prev, k_cur, v_prev, v_cur] + [tab_cur] * 3 + [tab_prev] * 3


def _band_kv(kp_ref, kc_ref, vp_ref, vc_ref, tabs_cur, tabs_prev):
    k = jnp.concatenate([_rotate(kp_ref[...], *(t[...] for t in tabs_prev)),
                         _rotate(kc_ref[...], *(t[...] for t in tabs_cur))], axis=0)
    v = jnp.concatenate([vp_ref[...], vc_ref[...]], axis=0)
    return k, v


def _query_tiles(q_ref, tiles, tabs_cur):
    c, su, sd = (t[:, :LANES] for t in tabs_cur)
    return jnp.concatenate(
        [_rotate(q_ref[:, t * LANES:(t + 1) * LANES], c, su, sd).astype(BF16) for t in tiles], axis=0)


def _sink_row(sink_ref, tiles, par):
    return jnp.concatenate([jnp.full((1, BLOCK), sink_ref[0, t * HEADS_PER_TILE + par], F32) for t in tiles], axis=1)


def _band_mask(n):
    kj = lax.broadcasted_iota(jnp.int32, (2 * BLOCK, BLOCK), 0)
    qi = lax.broadcasted_iota(jnp.int32, (2 * BLOCK, BLOCK), 1)
    rel = qi + BLOCK - kj
    return (rel >= 0) & (rel < BLOCK) & ((kj >= BLOCK) | (n > 0))


HEADS_PER_TILE = LANES // HEAD_DIM
TILES_PER_GROUP = GROUP // HEADS_PER_TILE


def _group_mask(n):
    return jnp.concatenate([_band_mask(n)] * TILES_PER_GROUP, axis=1)


def _lane_half(shape, par):
    lane = lax.broadcasted_iota(jnp.int32, shape, 1)
    return (lane < HEAD_DIM) if par == 0 else (lane >= HEAD_DIM)


def _head_tiles(kv, h):
    tile = kv[:, (h // HEADS_PER_TILE) * LANES:(h // HEADS_PER_TILE + 1) * LANES].astype(F32)
    own = jnp.where(_lane_half(tile.shape, h % HEADS_PER_TILE), tile, 0.0)
    other = pltpu.roll(own, HEAD_DIM, axis=1)
    lo, hi = (own, other) if h % HEADS_PER_TILE == 0 else (other, own)
    return lo.astype(BF16), hi.astype(BF16)


def _head_softmax(q_tile, k_half, sink, mask):
    s = lax.dot_general(k_half, q_tile, (NT, ((), ())), preferred_element_type=F32) * ATTN_SCALE
    s = jnp.where(mask, s, NEG_INF)
    m = jnp.maximum(jnp.max(s, axis=0, keepdims=True), sink)
    e = jnp.exp(s - m)
    es = jnp.exp(sink - m)
    inv = 1.0 / (jnp.sum(e, axis=0, keepdims=True) + es)
    return e * inv, es * inv


def _attn_fwd(proj, tables, sinks, comm=None):
    def body(sink_ref, q_ref, kp_ref, kc_ref, vp_ref, vc_ref, c_ref, su_ref, sd_ref, cp_ref, sup_ref, sdp_ref, o_ref):
        n = pl.program_id(0)
        mask = _group_mask(n)
        tabs_cur = (c_ref, su_ref, sd_ref)
        k, v = _band_kv(kp_ref, kc_ref, vp_ref, vc_ref, tabs_cur, (cp_ref, sup_ref, sdp_ref))
        for h in range(N_KV_HEADS):
            k_halves = _head_tiles(k, h)
            v_halves = _head_tiles(v, h)
            tiles = [h * TILES_PER_GROUP + t for t in range(TILES_PER_GROUP)]
            q_rows = _query_tiles(q_ref, tiles, tabs_cur)
            acc = None
            for par in range(HEADS_PER_TILE):
                p, _ = _head_softmax(q_rows, k_halves[par], _sink_row(sink_ref, tiles, par), mask)
                o = lax.dot_general(p.astype(BF16), v_halves[par], (TN, ((), ())), preferred_element_type=F32)
                acc = o if acc is None else acc + o
            for i, tile in enumerate(tiles):
                o_ref[:, tile * LANES:(tile + 1) * LANES] = acc[i * BLOCK:(i + 1) * BLOCK, :].astype(BF16)

    return _pcall(
        body, "attn_fwd", (N_QBLK,),
        [pl.BlockSpec(memory_space=pltpu.SMEM)] + _attn_specs(),
        pl.BlockSpec((BLOCK, D_ATTN), lambda n: (n, 0)),
        _sds((SEQ, D_ATTN), BF16), [sinks] + [proj] * 5 + list(tables) * 2, (), ("parallel",), comm)


def _branch_merge(conv_y, attn, w_co, w_ao, proj, comm=None):
    bm, bn = 1024, 512

    def body(cy_ref, at_ref, wc_ref, wa_ref, gc_ref, ga_ref, co_ref, ao_ref, mg_ref):
        co = jnp.dot(cy_ref[...], wc_ref[...], preferred_element_type=F32)
        ao = jnp.dot(at_ref[...], wa_ref[...], preferred_element_type=F32)
        co_ref[...] = co
        ao_ref[...] = ao
        mg_ref[...] = (jax.nn.sigmoid(gc_ref[...]) * co + jax.nn.sigmoid(ga_ref[...]) * ao).astype(BF16)

    act = pl.BlockSpec((bm, D_MODEL), lambda i, j: (i, 0))
    wgt = pl.BlockSpec((D_MODEL, bn), lambda i, j: (0, j))
    out = pl.BlockSpec((bm, bn), lambda i, j: (i, j))
    return _pcall(
        body, "branch_merge", (SEQ // bm, D_MODEL // bn),
        [act, act, wgt, wgt,
         pl.BlockSpec((bm, bn), lambda i, j: (i, COL_GC // bn + j)),
         pl.BlockSpec((bm, bn), lambda i, j: (i, COL_GA // bn + j))],
        [out, out, out],
        [_sds((SEQ, D_MODEL), F32), _sds((SEQ, D_MODEL), F32), _sds((SEQ, D_MODEL), BF16)],
        [conv_y, attn, w_co, w_ao, proj, proj], (), ("parallel", "parallel"), comm)


FF_BM, FF_BN = 512, 1408
FF_NB = D_FF // FF_BN


def _gate_up_fwd(h2, w_gu, comm=None):
    def body(h_ref, wg_ref, wu_ref, g_ref, u_ref, a_ref):
        h = h_ref[...]
        g = jnp.dot(h, wg_ref[...], preferred_element_type=F32)
        u = jnp.dot(h, wu_ref[...], preferred_element_type=F32)
        g_ref[...] = g
        u_ref[...] = u
        a_ref[...] = (jax.nn.silu(g) * u).astype(BF16)

    out = pl.BlockSpec((FF_BM, FF_BN), lambda i, j: (i, j))
    f32, b16 = _sds((SEQ, D_FF), F32), _sds((SEQ, D_FF), BF16)
    return _pcall(
        body, "mm_gate_up", (SEQ // FF_BM, FF_NB),
        [pl.BlockSpec((FF_BM, D_MODEL), lambda i, j: (i, 0)),
         pl.BlockSpec((D_MODEL, FF_BN), lambda i, j: (0, j)),
         pl.BlockSpec((D_MODEL, FF_BN), lambda i, j: (0, FF_NB + j))],
        [out, out, out], [f32, f32, b16], [h2, w_gu, w_gu], (), ("parallel", "parallel"), comm)


def _dact_swiglu(dx3b, w_down, g, u, comm=None):
    def body(dx_ref, w_ref, g_ref, u_ref, dg_ref, du_ref):
        da = lax.dot_general(dx_ref[...], w_ref[...], (NT, ((), ())), preferred_element_type=F32)
        g = g_ref[...]
        sg = jax.nn.sigmoid(g)
        dg_ref[...] = (da * u_ref[...] * (sg * (1.0 + g * (1.0 - sg)))).astype(BF16)
        du_ref[...] = (da * (g * sg)).astype(BF16)

    blk = pl.BlockSpec((FF_BM, FF_BN), lambda i, j: (i, j))
    b16 = _sds((SEQ, D_FF), BF16)
    return _pcall(
        body, "mm_dact", (SEQ // FF_BM, FF_NB),
        [pl.BlockSpec((FF_BM, D_MODEL), lambda i, j: (i, 0)), pl.BlockSpec((FF_BN, D_MODEL), lambda i, j: (j, 0)),
         blk, blk],
        [blk, blk], [b16, b16], [dx3b, w_down, g, u], (), ("parallel", "parallel"), comm, start_after_body=True)


def _mm_dh2(dg, du, w_gu, comm=None):
    bm = 1024
    nk = 2 * FF_NB

    def body(dg_ref, du_ref, w_ref, o_ref, acc_ref):
        k = pl.program_id(1)

        def part(a_ref):
            return lax.dot_general(a_ref[...], w_ref[...], (NT, ((), ())), preferred_element_type=F32)

        @pl.when(k == 0)
        def _():
            acc_ref[...] = part(dg_ref)

        @pl.when((k > 0) & (k < FF_NB))
        def _():
            acc_ref[...] += part(dg_ref)

        @pl.when(k >= FF_NB)
        def _():
            acc_ref[...] += part(du_ref)

        @pl.when(k == nk - 1)
        def _():
            o_ref[...] = acc_ref[...]

    return _pcall(
        body, "mm_dh2", (SEQ // bm, nk),
        [pl.BlockSpec((bm, FF_BN), lambda i, k: (i, jnp.minimum(k, FF_NB - 1))),
         pl.BlockSpec((bm, FF_BN), lambda i, k: (i, jnp.maximum(k - FF_NB, 0))),
         pl.BlockSpec((D_MODEL, FF_BN), lambda i, k: (0, k))],
        pl.BlockSpec((bm, D_MODEL), lambda i, k: (i, 0)), _sds((SEQ, D_MODEL), F32),
        [dg, du, w_gu], [pltpu.VMEM((bm, D_MODEL), F32)], ("parallel", "arbitrary"), comm)


def _mm_dw_gate_up(h2, dg, du):
    def body(h_ref, dg_ref, du_ref, o_ref):
        j = pl.program_id(0)

        def part(b_ref):
            return lax.dot_general(h_ref[...], b_ref[...], (TN, ((), ())), preferred_element_type=F32).astype(BF16)

        @pl.when(j < FF_NB)
        def _():
            o_ref[...] = part(dg_ref)

        @pl.when(j >= FF_NB)
        def _():
            o_ref[...] = part(du_ref)

    return pl.pallas_call(
        body, name="mm_dw_gate_up", grid=(2 * FF_NB,),
        in_specs=[_full_spec((SEQ, D_MODEL)),
                  pl.BlockSpec((SEQ, FF_BN), lambda j: (0, jnp.minimum(j, FF_NB - 1))),
                  pl.BlockSpec((SEQ, FF_BN), lambda j: (0, jnp.maximum(j - FF_NB, 0)))],
        out_specs=pl.BlockSpec((D_MODEL, FF_BN), lambda j: (0, j)),
        out_shape=_sds((D_MODEL, 2 * D_FF), BF16),
        compiler_params=_params(("arbitrary",)),
    )(h2, dg, du)


def _loss_head(x3, g, target):
    def body(x_ref, g_ref, t_ref, dx_ref, dxb_ref, dg_ref, loss_ref):
        i = pl.program_id(0)
        xf = x_ref[...]
        r = lax.rsqrt(jnp.mean(xf * xf, axis=-1, keepdims=True) + EPS)
        xn = xf * r
        gg = g_ref[...]
        err = xn * gg - t_ref[...]
        part = 0.5 * jnp.sum(jnp.mean(err * err, axis=-1, keepdims=True), axis=0, keepdims=True)
        dy = err * (1.0 / D_MODEL)
        dxn = dy * gg
        dx = r * (dxn - xn * jnp.mean(dxn * xn, axis=-1, keepdims=True))
        dx_ref[...] = dx
        dxb_ref[...] = dx.astype(BF16)
        dg = jnp.sum(dy * xn, axis=0, keepdims=True)
        lane0 = lax.broadcasted_iota(jnp.int32, (1, LANES), 1) == 0
        lpart = jnp.where(lane0, part, 0.0)

        @pl.when(i == 0)
        def _():
            dg_ref[...] = dg
            loss_ref[...] = lpart

        @pl.when(i > 0)
        def _():
            dg_ref[...] += dg
            loss_ref[...] += lpart

    return pl.pallas_call(
        body, name="loss_head", grid=(SEQ // ROWS,),
        in_specs=[_row_spec(D_MODEL), _full_spec((1, D_MODEL)), _row_spec(D_MODEL)],
        out_specs=[_row_spec(D_MODEL), _row_spec(D_MODEL), _full_spec((1, D_MODEL)), _full_spec((1, LANES))],
        out_shape=[_sds((SEQ, D_MODEL), F32), _sds((SEQ, D_MODEL), BF16),
                   _sds((1, D_MODEL), F32), _sds((1, LANES), F32)],
        compiler_params=_params(("arbitrary",)),
    )(x3, g, target)


def _rms_norm_bwd(name, dh, x, g, dres, with_bf16, comm=None):
    def body(dh_ref, x_ref, g_ref, dr_ref, *outs):
        i = pl.program_id(0)
        dx_ref = outs[0]
        dg_ref = outs[-1]
        xf = x_ref[...]
        r = lax.rsqrt(jnp.mean(xf * xf, axis=-1, keepdims=True) + EPS)
        xn = xf * r
        dh = dh_ref[...]
        dxn = dh * g_ref[...]
        dx = dr_ref[...] + r * (dxn - xn * jnp.mean(dxn * xn, axis=-1, keepdims=True))
        dx_ref[...] = dx
        if with_bf16:
            outs[1][...] = dx.astype(BF16)
        dg = jnp.sum(dh * xn, axis=0, keepdims=True)

        @pl.when(i == 0)
        def _():
            dg_ref[...] = dg

        @pl.when(i > 0)
        def _():
            dg_ref[...] += dg

    row = _row_spec(D_MODEL)
    out_specs = [row] + ([row] if with_bf16 else []) + [_full_spec((1, D_MODEL))]
    out_shape = ([_sds((SEQ, D_MODEL), F32)] + ([_sds((SEQ, D_MODEL), BF16)] if with_bf16 else [])
                 + [_sds((1, D_MODEL), F32)])
    return _pcall(body, name, (SEQ // ROWS,), [row, row, _full_spec((1, D_MODEL)), row], out_specs, out_shape,
                  [dh, x, g, dres], (), ("arbitrary",), comm)


def _merge_bwd(dx2b, w_o, conv_out, attn_out, proj):
    bm, bn = 1024, D_MODEL // 2

    def body(dx_ref, w_ref, co_ref, ao_ref, gc_ref, ga_ref, dco_ref, dao_ref, dgc_ref, dga_ref):
        dm = lax.dot_general(dx_ref[...], w_ref[...], (NT, ((), ())), preferred_element_type=F32)
        sc = jax.nn.sigmoid(gc_ref[...])
        sa = jax.nn.sigmoid(ga_ref[...])
        dco_ref[...] = (dm * sc).astype(BF16)
        dao_ref[...] = (dm * sa).astype(BF16)
        dgc_ref[...] = (dm * co_ref[...] * (sc * (1.0 - sc))).astype(BF16)
        dga_ref[...] = (dm * ao_ref[...] * (sa * (1.0 - sa))).astype(BF16)

    own = pl.BlockSpec((bm, bn), lambda i, j: (i, j))
    sd = _sds((SEQ, D_MODEL), BF16)
    return pl.pallas_call(
        body, name="mm_dmerged", grid=(SEQ // bm, D_MODEL // bn),
        in_specs=[pl.BlockSpec((bm, D_MODEL), lambda i, j: (i, 0)), pl.BlockSpec((bn, D_MODEL), lambda i, j: (j, 0)),
                  own, own,
                  pl.BlockSpec((bm, bn), lambda i, j: (i, COL_GC // bn + j)),
                  pl.BlockSpec((bm, bn), lambda i, j: (i, COL_GA // bn + j))],
        out_specs=[own, own, own, own], out_shape=[sd, sd, sd, sd],
        compiler_params=_params(("parallel", "parallel")),
    )(dx2b, w_o, conv_out, attn_out, proj, proj)


def _conv_bwd(dconv_y, proj, conv_w, comm=None):
    nblk = D_MODEL // CONV_COLS

    def body(dy_ref, cb_ref, cc_ref, cx_ref, w_ref, dcb_ref, dcc_ref, dcx_ref, dw_ref):
        cc = cc_ref[...]
        cx = cx_ref[...]
        u = cc * cx
        w = w_ref[...]
        u1 = _shift_rows(u, 1)
        u2 = _shift_rows(u, 2)
        cv = w[0:1, :] * u2 + w[1:2, :] * u1 + w[2:3, :] * u
        dy = dy_ref[...]
        dcb_ref[...] = (dy * cv).astype(BF16)
        dcv = dy * cb_ref[...]
        rows = lax.broadcasted_iota(jnp.int32, dcv.shape, 0)
        up1 = jnp.where(rows < SEQ - 1, pltpu.roll(dcv, SEQ - 1, axis=0), 0.0)
        up2 = jnp.where(rows < SEQ - 2, pltpu.roll(dcv, SEQ - 2, axis=0), 0.0)
        du = w[2:3, :] * dcv + w[1:2, :] * up1 + w[0:1, :] * up2
        dcc_ref[...] = (du * cx).astype(BF16)
        dcx_ref[...] = (du * cc).astype(BF16)
        dw_ref[...] = jnp.concatenate(
            [jnp.sum(dcv * u2, axis=0, keepdims=True),
             jnp.sum(dcv * u1, axis=0, keepdims=True),
             jnp.sum(dcv * u, axis=0, keepdims=True)], axis=0)

    def col(part):
        return pl.BlockSpec((SEQ, CONV_COLS), lambda j: (0, part * nblk + j))

    own = pl.BlockSpec((SEQ, CONV_COLS), lambda j: (0, j))
    wsp = pl.BlockSpec((3, CONV_COLS), lambda j: (0, j))
    sd = _sds((SEQ, D_MODEL), BF16)
    return _pcall(
        body, "conv_bwd", (nblk,), [own, col(0), col(1), col(2), wsp], [own, own, own, wsp],
        [sd, sd, sd, _sds((3, D_MODEL), F32)], [dconv_y, proj, proj, proj, conv_w], (), ("parallel",), comm)


def _attn_bwd(proj, dattn, sinks, tables, comm=None):
    def body(sink_ref, q_ref, kp_ref, kc_ref, vp_ref, vc_ref, c_ref, su_ref, sd_ref, cp_ref, sup_ref, sdp_ref,
             do_ref, dq_ref, dkp_ref, dkc_ref, dvp_ref, dvc_ref, ds_ref):
        n = pl.program_id(0)
        mask = _group_mask(n)
        tabs_cur = (c_ref, su_ref, sd_ref)
        k, v = _band_kv(kp_ref, kc_ref, vp_ref, vc_ref, tabs_cur, (cp_ref, sup_ref, sdp_ref))
        lane = lax.broadcasted_iota(jnp.int32, (1, LANES), 1)
        dsink = jnp.zeros((1, LANES), F32)
        c, su, sd = c_ref[:, :LANES], su_ref[:, :LANES], sd_ref[:, :LANES]
        dk_tiles = [None] * (N_KV_HEADS // HEADS_PER_TILE)
        dv_tiles = [None] * (N_KV_HEADS // HEADS_PER_TILE)
        for h in range(N_KV_HEADS):
            k_halves = _head_tiles(k, h)
            v_halves = _head_tiles(v, h)
            tiles = [h * TILES_PER_GROUP + t for t in range(TILES_PER_GROUP)]
            q_rows = _query_tiles(q_ref, tiles, tabs_cur)
            do_rows = jnp.concatenate([do_ref[:, t * LANES:(t + 1) * LANES] for t in tiles], axis=0)
            dk_par, dv_par = [], []
            dq_rows = None
            for par in range(HEADS_PER_TILE):
                p, p_sink = _head_softmax(q_rows, k_halves[par], _sink_row(sink_ref, tiles, par), mask)
                dp = lax.dot_general(v_halves[par], do_rows, (NT, ((), ())), preferred_element_type=F32)
                delta = jnp.sum(p * dp, axis=0, keepdims=True)
                ds = (p * (dp - delta) * ATTN_SCALE).astype(BF16)
                dq = lax.dot_general(ds, k_halves[par], (TN, ((), ())), preferred_element_type=F32)
                dq_rows = dq if dq_rows is None else dq_rows + dq
                dk_par.append(jnp.dot(ds, q_rows, preferred_element_type=F32))
                dv_par.append(jnp.dot(p.astype(BF16), do_rows, preferred_element_type=F32))
                sink_grad = p_sink * delta
                for i, tile in enumerate(tiles):
                    val = -jnp.sum(sink_grad[:, i * BLOCK:(i + 1) * BLOCK], axis=1, keepdims=True)
                    dsink = dsink + jnp.where(lane == tile * HEADS_PER_TILE + par, val, 0.0)
            for i, tile in enumerate(tiles):
                dq_tile = dq_rows[i * BLOCK:(i + 1) * BLOCK, :]
                dq_ref[:, tile * LANES:(tile + 1) * LANES] = _rotate(dq_tile, c, -su, -sd).astype(BF16)
            own = h % HEADS_PER_TILE
            for par_grads, tiles in ((dk_par, dk_tiles), (dv_par, dv_tiles)):
                shifted = pltpu.roll(par_grads[1 - own], HEAD_DIM, axis=1)
                total = jnp.where(_lane_half(shifted.shape, own), par_grads[own] + shifted, 0.0)
                i = h // HEADS_PER_TILE
                tiles[i] = total if tiles[i] is None else tiles[i] + total
        for i in range(N_KV_HEADS // HEADS_PER_TILE):
            cols = slice(i * LANES, (i + 1) * LANES)
            dkp_ref[:, cols] = dk_tiles[i][:BLOCK, :]
            dkc_ref[:, cols] = dk_tiles[i][BLOCK:, :]
            dvp_ref[:, cols] = dv_tiles[i][:BLOCK, :]
            dvc_ref[:, cols] = dv_tiles[i][BLOCK:, :]

        @pl.when(n == 0)
        def _():
            ds_ref[...] = dsink

        @pl.when(n > 0)
        def _():
            ds_ref[...] += dsink

    blk = pl.BlockSpec((BLOCK, D_KV), lambda n: (n, 0))
    prev_blk = pl.BlockSpec((BLOCK, D_KV), lambda n: ((n + N_QBLK - 1) % N_QBLK, 0))
    kv = _sds((SEQ, D_KV), F32)
    return _pcall(
        body, "attn_bwd", (N_QBLK,),
        [pl.BlockSpec(memory_space=pltpu.SMEM)] + _attn_specs() + [pl.BlockSpec((BLOCK, D_ATTN), lambda n: (n, 0))],
        [pl.BlockSpec((BLOCK, D_ATTN), lambda n: (n, 0)), prev_blk, blk, prev_blk, blk, _full_spec((1, LANES))],
        [_sds((SEQ, D_ATTN), BF16), kv, kv, kv, kv, _sds((1, LANES), F32)],
        [sinks] + [proj] * 5 + list(tables) * 2 + [dattn], (), ("arbitrary",), comm)


def _kv_grad_combine(dk_prev, dk_cur, dv_prev, dv_cur, tables):
    rows = 4 * BLOCK

    def body(kp_ref, kc_ref, vp_ref, vc_ref, c_ref, su_ref, sd_ref, o_ref):
        dk = kc_ref[...] + kp_ref[...]
        dv = vc_ref[...] + vp_ref[...]
        o_ref[:, :D_KV] = _rotate(dk, c_ref[...], -su_ref[...], -sd_ref[...]).astype(BF16)
        o_ref[:, D_KV:] = dv.astype(BF16)

    blk = pl.BlockSpec((rows, D_KV), lambda m: (m, 0))
    return pl.pallas_call(
        body, name="kv_grad_combine", grid=(SEQ // rows,),
        in_specs=[blk] * 7,
        out_specs=pl.BlockSpec((rows, 2 * D_KV), lambda m: (m, 0)),
        out_shape=_sds((SEQ, 2 * D_KV), BF16),
        compiler_params=_params(("parallel",)),
    )(dk_prev, dk_cur, dv_prev, dv_cur, *tables)


MATRICES = {
    "w_in": (D_MODEL, N_IN // N_CHIPS, "col"),
    "w_conv_out": (D_MODEL // N_CHIPS, D_MODEL, "row"),
    "w_attn_out": (D_MODEL // N_CHIPS, D_MODEL, "row"),
    "w_o": (D_MODEL // N_CHIPS, D_MODEL, "row"),
    "w_gate_up": (D_MODEL, 2 * D_FF // N_CHIPS, "col"),
    "w_down": (D_FF // N_CHIPS, D_MODEL, "row"),
}
BF16_ROW_TILE = 16
CONV_W_COLS = D_MODEL // N_CHIPS
SMALL_ROWS = 8


def _whole_shape(spec):
    rows, cols, kind = spec
    return (rows, cols * N_CHIPS) if kind == "col" else (rows * N_CHIPS, cols)


def _half_shape(spec):
    return (spec[0] // 2, spec[1])


def _aligned(start, multiple):
    return start if isinstance(start, int) else pl.multiple_of(start, multiple)


def _region(ref, spec, shard, half, part=0, parts=1):
    rows, cols, kind = spec
    hr = rows // 2
    n = hr // parts
    if kind == "col":
        return ref.at[pl.ds(_aligned(half * hr + part * n, BF16_ROW_TILE), n),
                      pl.ds(_aligned(shard * cols, LANES), cols)]
    return ref.at[pl.ds(_aligned(shard * rows + half * hr + part * n, BF16_ROW_TILE), n), :]


def _position():
    x, y, c = lax.axis_index("x"), lax.axis_index("y"), lax.axis_index("c")
    chips = [(1 - x, y), (x, 1 - y), (1 - x, 1 - y)]
    return x, y, c, chips


def _shard_of(chip):
    return 2 * chip[0] + chip[1]


def _remote(src, dst, send_sem, recv_sem, to):
    return pltpu.make_async_remote_copy(src_ref=src, dst_ref=dst, send_sem=send_sem, recv_sem=recv_sem,
                                        device_id=to, device_id_type=MESH)


CAST_STEPS = 4


def _to_bf16_in_whole(ws, specs, shard):
    n = len(ws)

    def body(s_ref, *refs):
        del s_ref
        for w_ref, o_ref in zip(refs[:n], refs[n:]):
            o_ref[...] = w_ref[...].astype(BF16)

    def out_spec(spec):
        rows = spec[0] // CAST_STEPS
        if spec[2] == "col":
            return pl.BlockSpec((rows, spec[1]), lambda i, s_ref: (i, s_ref[0]))
        return pl.BlockSpec((rows, spec[1]), lambda i, s_ref: (s_ref[0] * CAST_STEPS + i, 0))

    grid_spec = pltpu.PrefetchScalarGridSpec(
        num_scalar_prefetch=1, grid=(CAST_STEPS,),
        in_specs=[pl.BlockSpec((s[0] // CAST_STEPS, s[1]), lambda i, s_ref: (i, 0)) for s in specs],
        out_specs=[out_spec(s) for s in specs])
    return list(pl.pallas_call(
        body, name="cast_shards", grid_spec=grid_spec, out_shape=[_sds(_whole_shape(s), BF16) for s in specs],
        compiler_params=_params(("parallel",)),
    )(shard, *ws))


class _Gather:
    def __init__(self, wholes, pieces, conv_w=None):
        self.pieces = pieces
        self.n = len(wholes)
        self.with_conv_w = conv_w is not None
        self.operands = list(wholes) + ([conv_w] if self.with_conv_w else [])
        self.out_shape = [_sds(w.shape, w.dtype) for w in wholes]
        if self.with_conv_w:
            self.out_shape.append(_sds((3, D_MODEL), F32))
        self.aliases = {i: i for i in range(self.n)}
        n_ici = 3 * len(pieces)
        self.sems = [pltpu.SemaphoreType.DMA((n_ici,))] * 4
        if self.with_conv_w:
            self.sems += [pltpu.SemaphoreType.DMA((1,)), pltpu.SemaphoreType.DMA((3,)), pltpu.SemaphoreType.DMA((3,))]

    def _conv_w(self, cins, couts, sems, with_recvs):
        cw_in, cw_out = cins[self.n], couts[self.n]
        x, y, c, chips = _position()

        def cols(shard):
            return cw_out.at[:, pl.ds(_aligned(shard * CONV_W_COLS, LANES), CONV_W_COLS)]

        me = _shard_of((x, y))
        local = pltpu.make_async_copy(cw_in, cols(me), sems[4].at[0])
        sends = [_remote(cw_in, cols(me), sems[5].at[j], sems[6].at[j], (*chip, c)) for j, chip in enumerate(chips)]
        if not with_recvs:
            return local, sends, []
        recvs = [_remote(cols(_shard_of(chip)), cols(_shard_of(chip)), sems[5].at[j], sems[6].at[j], (*chip, c))
                 for j, chip in enumerate(chips)]
        return local, sends, recvs

    def start(self, cins, couts, sems):
        x, y, c, chips = _position()
        me = _shard_of((x, y))
        if self.with_conv_w:
            local, sends, _ = self._conv_w(cins, couts, sems, False)
            local.start()
            for cp in sends:
                cp.start()
        for p, (i, spec, part, parts) in enumerate(self.pieces):
            mine = _region(couts[i], spec, me, c, part, parts)
            for j, chip in enumerate(chips):
                _remote(mine, mine, sems[0].at[3 * p + j], sems[1].at[3 * p + j], (*chip, c)).start()

    def finish(self, cins, couts, sems):
        x, y, c, chips = _position()
        me = _shard_of((x, y))
        sibling = (x, y, 1 - c)
        send_a, recv_a, send_b, recv_b = sems[:4]
        passed = []
        for p, (i, spec, part, parts) in enumerate(self.pieces):
            for j, chip in enumerate(chips):
                k = 3 * p + j
                landed = _region(couts[i], spec, _shard_of(chip), c, part, parts)
                _remote(landed, landed, send_a.at[k], recv_a.at[k], (*chip, c)).wait_recv()
                cp = _remote(landed, landed, send_b.at[k], recv_b.at[k], sibling)
                cp.start()
                passed.append(cp)
        for p, (i, spec, part, parts) in enumerate(self.pieces):
            mine = _region(couts[i], spec, me, c, part, parts)
            for j, chip in enumerate(chips):
                k = 3 * p + j
                other = _region(couts[i], spec, _shard_of(chip), 1 - c, part, parts)
                _remote(other, other, send_b.at[k], recv_b.at[k], sibling).wait_recv()
                _remote(mine, mine, send_a.at[k], recv_a.at[k], (*chip, c)).wait_send()
        for cp in passed:
            cp.wait_send()
        if self.with_conv_w:
            local, sends, recvs = self._conv_w(cins, couts, sems, True)
            for cp in recvs:
                cp.wait_recv()
            for cp in sends:
                cp.wait_send()
            local.wait()


def _mm_in_gather(h1, w_whole, comm):
    spec = MATRICES["w_in"]
    cols = spec[1]
    bm = SEQ // 2

    def body(h_ref, w_in_ref, proj_ref, w_ref, wbuf, obuf, send_a, recv_a, send_b, recv_b, load_sem, store_sems):
        del w_in_ref
        s, mi = pl.program_id(0), pl.program_id(1)
        x, y, c, chips = _position()
        me = _shard_of((x, y))
        sibling = (x, y, 1 - c)
        mine = _region(w_ref, spec, me, c)

        @pl.when((s == 0) & (mi == 0))
        def _():
            for j, chip in enumerate(chips):
                _remote(mine, mine, send_a.at[j], recv_a.at[j], (*chip, c)).start()

        shard = me
        for j, chip in enumerate(chips):
            shard = jnp.where(s == j + 1, _shard_of(chip), shard)

            @pl.when((s == j + 1) & (mi == 0))
            def _():
                landed = _region(w_ref, spec, _shard_of(chip), c)
                _remote(landed, landed, send_a.at[j], recv_a.at[j], (*chip, c)).wait_recv()
                _remote(landed, landed, send_b.at[j], recv_b.at[j], sibling).start()
                other = _region(w_ref, spec, _shard_of(chip), 1 - c)
                _remote(other, other, send_b.at[j], recv_b.at[j], sibling).wait_recv()

        col0 = pl.multiple_of(shard * cols, LANES)

        @pl.when(mi == 0)
        def _():
            load = pltpu.make_async_copy(w_ref.at[:, pl.ds(col0, cols)], wbuf, load_sem.at[0])
            load.start()
            load.wait()

        def store():
            rows = pl.ds(pl.multiple_of(mi * bm, bm), bm)
            return pltpu.make_async_copy(obuf.at[mi], proj_ref.at[rows, pl.ds(col0, cols)], store_sems.at[mi])

        @pl.when(s > 0)
        def _():
            store().wait()

        obuf[mi] = jnp.dot(h_ref[...], wbuf[...], preferred_element_type=F32)
        store().start()

        @pl.when(s == N_CHIPS - 1)
        def _():
            store().wait()

        @pl.when((s == N_CHIPS - 1) & (mi == 1))
        def _():
            for j, chip in enumerate(chips):
                landed = _region(w_ref, spec, _shard_of(chip), c)
                _remote(mine, mine, send_a.at[j], recv_a.at[j], (*chip, c)).wait_send()
                _remote(landed, landed, send_b.at[j], recv_b.at[j], sibling).wait_send()

    sem3 = pltpu.SemaphoreType.DMA((3,))
    (proj, whole), extra = _pcall(
        body, "mm_in", (N_CHIPS, SEQ // bm),
        [pl.BlockSpec((bm, D_MODEL), lambda s, m: (m, 0)), HBM_SPEC], [HBM_SPEC, HBM_SPEC],
        [_sds((SEQ, N_IN), F32), _sds(w_whole.shape, w_whole.dtype)], [h1, w_whole],
        [pltpu.VMEM((D_MODEL, cols), BF16), pltpu.VMEM((SEQ // bm, bm, cols), F32), sem3, sem3, sem3, sem3,
         pltpu.SemaphoreType.DMA((1,)), pltpu.SemaphoreType.DMA((SEQ // bm,))],
        None, comm, aliases={1: 1}, start_after_body=True)
    return proj, whole, extra


def _pack_small(dg_mix, dg_ffn, dg_final, dconv_w, dsinks, loss_row):
    def body(a_ref, b_ref, c_ref, w_ref, s_ref, l_ref, o_ref):
        pad = jnp.zeros((1, D_MODEL - LANES), F32)
        o_ref[0:1, :] = a_ref[...]
        o_ref[1:2, :] = b_ref[...]
        o_ref[2:3, :] = c_ref[...]
        o_ref[3:6, :] = w_ref[...]
        o_ref[6:7, :] = jnp.concatenate([s_ref[...], pad], axis=1)
        o_ref[7:8, :] = jnp.concatenate([l_ref[...], pad], axis=1)

    return pl.pallas_call(
        body, name="pack_small", out_shape=_sds((SMALL_ROWS, D_MODEL), F32),
        compiler_params=_params(),
    )(dg_mix, dg_ffn, dg_final, dconv_w, dsinks, loss_row)


class _Pair:
    def __init__(self, dws, specs):
        self.specs = specs
        self.operands = list(dws)
        self.out_shape = [_sds((N_CHIPS, *_half_shape(s)), BF16) for s in specs]
        self.aliases = {}
        n = N_CHIPS * len(specs)
        self.sems = [pltpu.SemaphoreType.DMA((n,)), pltpu.SemaphoreType.DMA((n,))]

    def _copies(self, cins, couts, sems):
        x, y, c, _ = _position()
        sibling = (x, y, 1 - c)
        for i, spec in enumerate(self.specs):
            for t in range(N_CHIPS):
                k = N_CHIPS * i + t
                yield _remote(_region(cins[i], spec, t, 1 - c), couts[i].at[t], sems[0].at[k], sems[1].at[k], sibling)

    def start(self, cins, couts, sems):
        for cp in self._copies(cins, couts, sems):
            cp.start()

    def finish(self, cins, couts, sems):
        for cp in self._copies(cins, couts, sems):
            cp.wait()


class _SmallAllToAll:
    def __init__(self, small):
        self.operands = [small]
        self.out_shape = [_sds((N_DEV, SMALL_ROWS, D_MODEL), F32)]
        self.aliases = {}
        self.sems = [pltpu.SemaphoreType.DMA((N_DEV - 1,)), pltpu.SemaphoreType.DMA((N_DEV - 1,)),
                     pltpu.SemaphoreType.DMA((1,))]

    def _copies(self, cins, couts, sems):
        x, y, c, _ = _position()
        me = 4 * x + 2 * y + c
        out = []
        for r in range(1, N_DEV):
            flip = ((r >> 2) & 1, (r >> 1) & 1, r & 1)
            peer = tuple(1 - p if f else p for p, f in zip((x, y, c), flip))
            theirs = couts[0].at[4 * peer[0] + 2 * peer[1] + peer[2]]
            out.append((_remote(cins[0], couts[0].at[me], sems[0].at[r - 1], sems[1].at[r - 1], peer),
                        functools.partial(_remote, theirs, theirs, sems[0].at[r - 1], sems[1].at[r - 1], peer)))
        return pltpu.make_async_copy(cins[0], couts[0].at[me], sems[2].at[0]), out

    def start(self, cins, couts, sems):
        own, copies = self._copies(cins, couts, sems)
        own.start()
        for send, _ in copies:
            send.start()

    def finish(self, cins, couts, sems):
        own, copies = self._copies(cins, couts, sems)
        for send, recv in copies:
            recv().wait_recv()
            send.wait_send()
        own.wait()


class _Both:
    def __init__(self, a, b):
        self.a, self.b = a, b
        self.operands = list(a.operands) + list(b.operands)
        self.out_shape = list(a.out_shape) + list(b.out_shape)
        self.aliases = dict(a.aliases)
        self.aliases.update({len(a.operands) + k: len(a.out_shape) + v for k, v in b.aliases.items()})
        self.sems = list(a.sems) + list(b.sems)

    def _split(self, cins, couts, sems):
        na, ma, sa = len(self.a.operands), len(self.a.out_shape), len(self.a.sems)
        return (cins[:na], couts[:ma], sems[:sa]), (cins[na:], couts[ma:], sems[sa:])

    def start(self, cins, couts, sems):
        for plan, args in zip((self.a, self.b), self._split(cins, couts, sems)):
            plan.start(*args)

    def finish(self, cins, couts, sems):
        for plan, args in zip((self.a, self.b), self._split(cins, couts, sems)):
            plan.finish(*args)


def _pair_sum(name, specs, dws, got, place):
    n_mat = len(specs)

    def body(p_ref, *refs):
        t = pl.program_id(0)
        mine, theirs = refs[:n_mat], refs[n_mat:2 * n_mat]
        outs, owns = refs[2 * n_mat:3 * n_mat], refs[3 * n_mat:]
        for a, b, o, own in zip(mine, theirs, outs, owns):
            s = (a[...].astype(F32) + b[...].astype(F32)).astype(BF16)
            o[...] = s

            @pl.when(t == p_ref[1])
            def _():
                own[...] = s

    def mine_spec(spec):
        hr, cols = _half_shape(spec)
        if spec[2] == "col":
            return pl.BlockSpec((hr, cols), lambda t, p_ref: (p_ref[0], t))
        return pl.BlockSpec((hr, cols), lambda t, p_ref: (2 * t + p_ref[0], 0))

    def slot_spec(spec):
        return pl.BlockSpec((None, *_half_shape(spec)), lambda t, p_ref: (t, 0, 0))

    def own_spec(spec):
        return pl.BlockSpec((None, *_half_shape(spec)), lambda t, p_ref: (p_ref[1], 0, 0))

    slots = [_sds((N_CHIPS, *_half_shape(s)), BF16) for s in specs]
    grid_spec = pltpu.PrefetchScalarGridSpec(
        num_scalar_prefetch=1, grid=(N_CHIPS,),
        in_specs=[mine_spec(s) for s in specs] + [slot_spec(s) for s in specs],
        out_specs=[slot_spec(s) for s in specs] + [own_spec(s) for s in specs])
    res = pl.pallas_call(
        body, name=name, grid_spec=grid_spec, out_shape=slots + slots,
        compiler_params=_params(("arbitrary",)),
    )(place, *dws, *got)
    return list(res[:n_mat]), list(res[n_mat:])


class _ChipExchange:
    def __init__(self, sums, slots, part=0, parts=1):
        self.n = len(sums)
        self.part, self.parts = part, parts
        self.operands = list(sums) + list(slots)
        self.out_shape = [_sds(s.shape, s.dtype) for s in slots]
        self.aliases = {self.n + i: i for i in range(self.n)}
        self.sems = [pltpu.SemaphoreType.DMA((3 * self.n,)), pltpu.SemaphoreType.DMA((3 * self.n,))]

    def _rows(self, ref, slot):
        n = ref.shape[1] // self.parts
        return ref.at[slot, pl.ds(self.part * n, n), :]

    def _copies(self, cins, couts, sems):
        x, y, c, chips = _position()
        me = _shard_of((x, y))
        for i in range(self.n):
            for j, chip in enumerate(chips):
                k = 3 * i + j
                theirs = self._rows(couts[i], _shard_of(chip))
                yield (_remote(self._rows(cins[i], _shard_of(chip)), self._rows(couts[i], me),
                               sems[0].at[k], sems[1].at[k], (*chip, c)),
                       functools.partial(_remote, theirs, theirs, sems[0].at[k], sems[1].at[k], (*chip, c)))

    def start(self, cins, couts, sems):
        for send, _ in self._copies(cins, couts, sems):
            send.start()

    def finish(self, cins, couts, sems):
        for send, recv in self._copies(cins, couts, sems):
            recv().wait_recv()
            send.wait_send()


def _chip_sum(name, specs, slots, core):
    steps = 2
    n_mat = len(specs)

    def body(c_ref, *refs):
        del c_ref
        ins, outs = refs[:n_mat], refs[n_mat:]
        for a, o in zip(ins, outs):
            acc = a[0].astype(F32)
            for t in range(1, N_CHIPS):
                acc = acc + a[t].astype(F32)
            o[...] = acc

    def in_spec(spec):
        hr, cols = _half_shape(spec)
        return pl.BlockSpec((N_CHIPS, hr // steps, cols), lambda i, c_ref: (0, i, 0))

    def out_spec(spec):
        hr, cols = _half_shape(spec)
        return pl.BlockSpec((hr // steps, cols), lambda i, c_ref: (c_ref[0] * steps + i, 0))

    grid_spec = pltpu.PrefetchScalarGridSpec(
        num_scalar_prefetch=1, grid=(steps,),
        in_specs=[in_spec(s) for s in specs], out_specs=[out_spec(s) for s in specs])
    return list(pl.pallas_call(
        body, name=name, grid_spec=grid_spec,
        out_shape=[_sds((s[0], s[1]), F32) for s in specs],
        compiler_params=_params(("parallel",)),
    )(core, *slots))


class _HalfExchange:
    def __init__(self, grads, specs):
        self.specs = specs
        self.operands = list(grads)
        self.out_shape = [_sds(g.shape, g.dtype) for g in grads]
        self.aliases = {i: i for i in range(len(grads))}
        self.sems = [pltpu.SemaphoreType.DMA((len(grads),)), pltpu.SemaphoreType.DMA((len(grads),))]

    def _copies(self, couts, sems):
        x, y, c, _ = _position()
        sibling = (x, y, 1 - c)
        for i, spec in enumerate(self.specs):
            hr = spec[0] // 2
            mine = couts[i].at[pl.ds(_aligned(c * hr, 8), hr), :]
            theirs = couts[i].at[pl.ds(_aligned((1 - c) * hr, 8), hr), :]
            yield (_remote(mine, mine, sems[0].at[i], sems[1].at[i], sibling),
                   functools.partial(_remote, theirs, theirs, sems[0].at[i], sems[1].at[i], sibling))

    def start(self, cins, couts, sems):
        for send, _ in self._copies(couts, sems):
            send.start()

    def finish(self, cins, couts, sems):
        for send, recv in self._copies(couts, sems):
            recv().wait_recv()
            send.wait_send()


def _small_sum(blocks):
    def body(b_ref, o_ref):
        acc = b_ref[0]
        for d in range(1, N_DEV):
            acc = acc + b_ref[d]
        o_ref[...] = acc

    return pl.pallas_call(
        body, name="small_sum", out_shape=_sds((SMALL_ROWS, D_MODEL), F32), compiler_params=_params(),
    )(blocks)


def _adamw(name, params, steps):
    n = len(params)

    def body(*refs):
        for p in range(n):
            w_ref, g_ref, m_ref, v_ref = refs[4 * p:4 * p + 4]
            d_ref, nm_ref, nv_ref, go_ref = refs[4 * n + 4 * p:4 * n + 4 * p + 4]
            g = g_ref[...]
            go_ref[...] = g
            m = ADAM_B1 * m_ref[...] + (1.0 - ADAM_B1) * g
            v = ADAM_B2 * v_ref[...] + (1.0 - ADAM_B2) * jnp.square(g)
            m_hat = m / (1.0 - ADAM_B1 ** ADAM_STEP)
            v_hat = v / (1.0 - ADAM_B2 ** ADAM_STEP)
            d_ref[...] = -ADAM_LR * (m_hat / (jnp.sqrt(v_hat) + ADAM_EPS) + ADAM_WD * w_ref[...])
            nm_ref[...] = m
            nv_ref[...] = v

    in_specs, out_specs, out_shape, operands = [], [], [], []
    for w, g, m, v in params:
        spec = pl.BlockSpec((w.shape[0] // steps, w.shape[1]), lambda i: (i, 0))
        in_specs += [spec] * 4
        out_specs += [spec] * 4
        out_shape += [_sds(w.shape, F32)] * 4
        operands += [w, g, m, v]
    outs = _pcall(body, name, (steps,), in_specs, out_specs, out_shape, operands, (), ("parallel",))
    return [tuple(outs[4 * p:4 * p + 4]) for p in range(n)]


MATRIX_NAMES = tuple(MATRICES)
WEIGHT_ORDER = ("g_mix", "w_in", "conv_w", "attn_sinks", "w_conv_out", "w_attn_out", "w_o", "g_ffn",
                "w_gate_up", "w_down", "g_final")


def kernel(x, g_mix, w_in, conv_w, attn_sinks, w_conv_out, w_attn_out, w_o, g_ffn, w_gate_up, w_down, g_final, loss_target, m_g_mix, m_w_in, m_conv_w, m_attn_sinks, m_w_conv_out, m_w_attn_out, m_w_o, m_g_ffn, m_w_gate_up, m_w_down, m_g_final, v_g_mix, v_w_in, v_conv_w, v_attn_sinks, v_w_conv_out, v_w_attn_out, v_w_o, v_g_ffn, v_w_gate_up, v_w_down, v_g_final):
    w = dict(g_mix=g_mix, w_in=w_in[0], conv_w=conv_w[0], attn_sinks=attn_sinks, w_conv_out=w_conv_out[0],
             w_attn_out=w_attn_out[0], w_o=w_o[0], g_ffn=g_ffn, w_gate_up=w_gate_up[0], w_down=w_down[0],
             g_final=g_final[None, :])
    m = dict(g_mix=m_g_mix, w_in=m_w_in[0], conv_w=m_conv_w[0], attn_sinks=m_attn_sinks,
             w_conv_out=m_w_conv_out[0], w_attn_out=m_w_attn_out[0], w_o=m_w_o[0], g_ffn=m_g_ffn,
             w_gate_up=m_w_gate_up[0], w_down=m_w_down[0], g_final=m_g_final[None, :])
    v = dict(g_mix=v_g_mix, w_in=v_w_in[0], conv_w=v_conv_w[0], attn_sinks=v_attn_sinks,
             w_conv_out=v_w_conv_out[0], w_attn_out=v_w_attn_out[0], w_o=v_w_o[0], g_ffn=v_g_ffn,
             w_gate_up=v_w_gate_up[0], w_down=v_w_down[0], g_final=v_g_final[None, :])
    shard = (2 * lax.axis_index("x") + lax.axis_index("y")).astype(jnp.int32)
    core = lax.axis_index("c").astype(jnp.int32)
    shard1, core1, place = shard.reshape((1,)), core.reshape((1,)), jnp.stack([core, shard])
    spec = MATRICES
    xs, target, sinks = x[0], loss_target[0], w["attn_sinks"]
    tables = _rope_tables()

    def gather(names, part=0, parts=1):
        return _Gather([whole[n] for n in names], [(i, spec[n], part, parts) for i, n in enumerate(names)])

    def pair(names):
        return _Pair([dw[n] for n in names], [spec[n] for n in names])

    def pair_sum(tag, names, got):
        return _pair_sum("pair_sum_" + tag, [spec[n] for n in names], [dw[n] for n in names], got, place)

    whole = dict(zip(MATRIX_NAMES, _to_bf16_in_whole(
        [w[n] for n in MATRIX_NAMES], [spec[n] for n in MATRIX_NAMES], shard1)))

    mixers = ("w_conv_out", "w_attn_out", "w_o")
    h1 = _rms_norm("norm_mix", xs, w["g_mix"])
    proj, whole["w_in"], (*got, conv_w_whole) = _mm_in_gather(
        h1, whole["w_in"], _Gather([whole[n] for n in mixers], [(i, spec[n], 0, 1) for i, n in enumerate(mixers)],
                                   conv_w=w["conv_w"]))
    whole.update(zip(mixers, got))
    conv_y = _conv_fwd(proj, conv_w_whole)
    attn, (whole["w_gate_up"],) = _attn_fwd(proj, tables, sinks, comm=gather(("w_gate_up",), 0, 2))
    (conv_out, attn_out, merged), (whole["w_gate_up"],) = _branch_merge(
        conv_y, attn, whole["w_conv_out"], whole["w_attn_out"], proj, comm=gather(("w_gate_up",), 1, 2))
    x2 = _mm_nn("mm_o", merged, whole["w_o"], 1024, 1024, F32, res=xs)
    h2 = _rms_norm("norm_ffn", x2, w["g_ffn"])
    (gate, up, act), (whole["w_down"],) = _gate_up_fwd(h2, whole["w_gate_up"], comm=gather(("w_down",)))
    x3 = _mm_nn("mm_down", act, whole["w_down"], 1024, 512, F32, res=x2)
    dx3, dx3b, dg_final, loss_row = _loss_head(x3, w["g_final"], target)

    dw = {}
    dw["w_down"] = _mm_tn("mm_dw_down", act, dx3b, 1408, 1024, BF16)
    (dgate, dup), got = _dact_swiglu(dx3b, whole["w_down"], gate, up, comm=pair(("w_down",)))
    sums_a, own_a = pair_sum("down", ("w_down",), got)
    dh2, slots_a = _mm_dh2(dgate, dup, whole["w_gate_up"], comm=_ChipExchange(sums_a, own_a))
    dw["w_gate_up"] = _mm_dw_gate_up(h2, dgate, dup)
    (dx2, dx2b, dg_ffn), got = _rms_norm_bwd("norm_ffn_bwd", dh2, x2, w["g_ffn"], dx3, True, comm=pair(("w_gate_up",)))
    sums_b, own_b = pair_sum("gate_up", ("w_gate_up",), got)
    dw["w_o"] = _mm_tn("mm_dw_o", merged, dx2b, 1024, 1024, BF16)
    dco, dao, dgc, dga = _merge_bwd(dx2b, whole["w_o"], conv_out, attn_out, proj)
    dconv_y = _mm_nt("mm_dconv_y", dco, whole["w_conv_out"], 1024, 1024, D_MODEL, F32)
    dw["w_conv_out"] = _mm_tn("mm_dw_conv_out", conv_y, dco, 1024, 1024, BF16)
    dattn = _mm_nt("mm_dattn", dao, whole["w_attn_out"], 1024, 1024, D_MODEL, BF16)
    dw["w_attn_out"] = _mm_tn("mm_dw_attn_out", attn, dao, 1024, 1024, BF16)
    (dcb, dcc, dcx, dconv_w), got = _conv_bwd(dconv_y, proj, conv_w_whole, comm=pair(mixers))
    sums_c, own_c = pair_sum("mixers", mixers, got)
    (dq, dk_prev, dk_cur, dv_prev, dv_cur, dsinks), slots_b = _attn_bwd(
        proj, dattn, sinks, tables, comm=_ChipExchange(sums_b, own_b))
    dkv = _kv_grad_combine(dk_prev, dk_cur, dv_prev, dv_cur, tables)
    dproj = jnp.concatenate([dcb, dcc, dcx, dq, dkv, dgc, dga], axis=1)
    dw["w_in"], slots_c = _mm_tn("mm_dw_in", h1, dproj, 1024, 1664, BF16, comm=_ChipExchange(sums_c, own_c))
    sums_d, own_d = pair_sum("in", ("w_in",), _comm_call("pair_exchange_in", pair(("w_in",))))
    early = ("w_down", "w_gate_up") + mixers
    halves = _chip_sum("chip_sum_early", [spec[n] for n in early], slots_a + slots_b + slots_c, core1)
    dh1, (own_d, *reduced) = _mm_nt(
        "mm_dh1", dproj, whole["w_in"], 1024, 1024, 1664, F32,
        comm=_Both(_ChipExchange(sums_d, own_d, 0, 2), _HalfExchange(halves, [spec[n] for n in early])))
    g = dict(zip(early, reduced))
    (grad_x, dg_mix), slots_d = _rms_norm_bwd("norm_mix_bwd", dh1, xs, w["g_mix"], dx2, False,
                                              comm=_ChipExchange(sums_d, [own_d], 1, 2))
    small = _pack_small(dg_mix, dg_ffn, dg_final, dconv_w, dsinks, loss_row)
    half_in = _chip_sum("chip_sum_in", [spec["w_in"]], slots_d, core1)
    g["w_in"], small_blocks = _comm_call(
        "half_exchange_in", _Both(_HalfExchange(half_in, [spec["w_in"]]), _SmallAllToAll(small)))
    delta, new_m, new_v = {}, {}, {}

    def keep(names, results):
        for n, (d, nm, nv, grad) in zip(names, results):
            delta[n], new_m[n], new_v[n], g[n] = d, nm, nv, grad

    keep(early, _adamw("adamw_early", [(w[n], g[n], m[n], v[n]) for n in early], 8))
    small_sum = _small_sum(small_blocks)
    g["g_mix"] = small_sum[0:1, :]
    g["g_ffn"] = small_sum[1:2, :]
    g["g_final"] = small_sum[2:3, :]
    g["conv_w"] = lax.dynamic_slice(small_sum, (3, shard * CONV_W_COLS), (3, CONV_W_COLS))
    g["attn_sinks"] = small_sum[6:7, :N_HEADS]
    loss = small_sum[7, 0]
    keep(("w_in",), _adamw("adamw_w_in", [(w["w_in"], g["w_in"], m["w_in"], v["w_in"])], 4))
    rest = ("g_mix", "g_ffn", "g_final", "conv_w", "attn_sinks")
    keep(rest, _adamw("adamw_small", [(w[n], g[n], m[n], v[n]) for n in rest], 1))

    def shaped(vals):
        return [vals[n].reshape((D_MODEL,)) if n == "g_final" else
                (vals[n][None] if n in MATRIX_NAMES or n == "conv_w" else vals[n]) for n in WEIGHT_ORDER]

    return (loss, grad_x[None], *shaped(g), *shaped(delta), *shaped(new_m), *shaped(new_v))
```

```python
import functools
import math

import jax
import jax.numpy as jnp
import numpy as np
from jax import lax
from jax.experimental import pallas as pl
from jax.experimental.pallas import tpu as pltpu

F32 = jnp.float32
BF16 = jnp.bfloat16

D_MODEL = 1024
SEQ = 2048
HEAD_DIM = 64
N_HEADS = 16
N_KV_HEADS = 4
GROUP = N_HEADS // N_KV_HEADS
D_ATTN = N_HEADS * HEAD_DIM
D_KV = N_KV_HEADS * HEAD_DIM
BLOCK = 128
ROT_DIM = HEAD_DIM // 4
ROPE_THETA = 500000.0
ATTN_SCALE = 1.0 / math.sqrt(HEAD_DIM)
NEG_INF = -1e30
D_FF = 2816
EPS = 1e-5
N_IN = 3 * D_MODEL + D_ATTN + 2 * D_KV + 2 * D_MODEL
COL_Q = 3 * D_MODEL
COL_K = COL_Q + D_ATTN
COL_V = COL_K + D_KV
COL_GC = COL_V + D_KV
COL_GA = COL_GC + D_MODEL

ADAM_LR = 0.001
ADAM_B1 = 0.9
ADAM_B2 = 0.999
ADAM_EPS = 1e-08
ADAM_WD = 0.01
ADAM_STEP = 10

N_CHIPS = 4
N_DEV = 8

V7X_VMEM_BYTES = 64 * 1024 * 1024
VMEM_LIMIT = (V7X_VMEM_BYTES * 3) // 4
LANES = 128
MESH = pl.DeviceIdType.MESH


def _params(semantics=None):
    return pltpu.CompilerParams(dimension_semantics=semantics, vmem_limit_bytes=VMEM_LIMIT)


def _sds(shape, dtype):
    return jax.ShapeDtypeStruct(shape, dtype)


HBM_SPEC = pl.BlockSpec(memory_space=pl.ANY)


def _pcall(body, name, grid, in_specs, out_specs, out_shape, operands, scratch=(), semantics=None, comm=None,
           aliases=None, start_after_body=False):
    aliases = dict(aliases or {})
    if comm is None:
        return pl.pallas_call(
            body, name=name, grid=grid, in_specs=in_specs, out_specs=out_specs, out_shape=out_shape,
            scratch_shapes=list(scratch), input_output_aliases=aliases,
            compiler_params=_params(semantics))(*operands)
    multi = isinstance(out_shape, (list, tuple))
    o_specs = list(out_specs) if multi else [out_specs]
    o_shape = list(out_shape) if multi else [out_shape]
    n_in, n_out, n_scr = len(operands), len(o_shape), len(scratch)
    n_cin, n_cout = len(comm.operands), len(comm.out_shape)

    def hosted(*refs):
        ins, cins = refs[:n_in], refs[n_in:n_in + n_cin]
        o0 = n_in + n_cin
        outs, couts = refs[o0:o0 + n_out], refs[o0 + n_out:o0 + n_out + n_cout]
        s0 = o0 + n_out + n_cout
        scr, sems = refs[s0:s0 + n_scr], refs[s0 + n_scr:]
        first = last = None
        for axis, size in enumerate(grid):
            i = pl.program_id(axis)
            first = (i == 0) if first is None else first & (i == 0)
            last = (i == size - 1) if last is None else last & (i == size - 1)

        if not start_after_body:
            @pl.when(first)
            def _():
                comm.start(cins, couts, sems)

        body(*ins, *outs, *scr)

        if start_after_body:
            @pl.when(first)
            def _():
                comm.start(cins, couts, sems)

        @pl.when(last)
        def _():
            comm.finish(cins, couts, sems)

    res = pl.pallas_call(
        hosted, name=name, grid=grid,
        in_specs=list(in_specs) + [HBM_SPEC] * n_cin, out_specs=o_specs + [HBM_SPEC] * n_cout,
        out_shape=o_shape + list(comm.out_shape), scratch_shapes=list(scratch) + list(comm.sems),
        input_output_aliases={**aliases, **{n_in + a: n_out + b for a, b in comm.aliases.items()}},
        compiler_params=_params(("arbitrary",) * len(grid)))(*operands, *comm.operands)
    outs = list(res[:n_out])
    return (outs if multi else outs[0]), list(res[n_out:])


def _comm_call(name, comm):
    def body(*refs):
        n_cin, n_cout = len(comm.operands), len(comm.out_shape)
        cins, couts, sems = refs[:n_cin], refs[n_cin:n_cin + n_cout], refs[n_cin + n_cout:]
        comm.start(cins, couts, sems)
        comm.finish(cins, couts, sems)

    return list(pl.pallas_call(
        body, name=name, in_specs=[HBM_SPEC] * len(comm.operands), out_specs=[HBM_SPEC] * len(comm.out_shape),
        out_shape=list(comm.out_shape), scratch_shapes=list(comm.sems),
        input_output_aliases=dict(comm.aliases))(*comm.operands))


NN = ((1,), (0,))
NT = ((1,), (1,))
TN = ((0,), (0,))


def _matmul(name, a, b, dims, grid, a_spec, b_spec, o_spec, o_shape, o_dtype, res=None, res_spec=None, comm=None):
    nk = grid[2]

    def body(*refs):
        if res is None:
            a_ref, b_ref, o_ref = refs[:3]
            r_ref = None
            scratch = refs[3:]
        else:
            a_ref, b_ref, r_ref, o_ref = refs[:4]
            scratch = refs[4:]
        p = lax.dot_general(a_ref[...], b_ref[...], (dims, ((), ())), preferred_element_type=F32)

        def finish(acc):
            if r_ref is not None:
                acc = r_ref[...] + acc
            o_ref[...] = acc.astype(o_dtype)

        if nk == 1:
            finish(p)
        else:
            acc_ref = scratch[0]
            k = pl.program_id(2)

            @pl.when(k == 0)
            def _():
                acc_ref[...] = p

            @pl.when(k > 0)
            def _():
                acc_ref[...] += p

            @pl.when(k == nk - 1)
            def _():
                finish(acc_ref[...])

    operands = [a, b] if res is None else [a, b, res]
    in_specs = [a_spec, b_spec] if res is None else [a_spec, b_spec, res_spec]
    scratch = [pltpu.VMEM(o_spec.block_shape, F32)] if nk > 1 else []
    return _pcall(body, name, grid, in_specs, o_spec, _sds(o_shape, o_dtype), operands, scratch,
                  ("parallel", "parallel", "arbitrary"), comm)


def _mm_nn(name, a, b, bm, bn, o_dtype, res=None, comm=None):
    m, k = a.shape
    n = b.shape[1]
    return _matmul(
        name, a, b, NN, (m // bm, n // bn, 1),
        pl.BlockSpec((bm, k), lambda i, j, kk: (i, 0)),
        pl.BlockSpec((k, bn), lambda i, j, kk: (0, j)),
        pl.BlockSpec((bm, bn), lambda i, j, kk: (i, j)),
        (m, n), o_dtype, res,
        None if res is None else pl.BlockSpec((bm, bn), lambda i, j, kk: (i, j)), comm,
    )


def _mm_nt(name, a, b, bm, bn, bk, o_dtype, comm=None):
    m, k = a.shape
    n = b.shape[0]
    return _matmul(
        name, a, b, NT, (m // bm, n // bn, k // bk),
        pl.BlockSpec((bm, bk), lambda i, j, kk: (i, kk)),
        pl.BlockSpec((bn, bk), lambda i, j, kk: (j, kk)),
        pl.BlockSpec((bm, bn), lambda i, j, kk: (i, j)),
        (m, n), o_dtype, comm=comm,
    )


def _mm_tn(name, a, b, bm, bn, o_dtype, comm=None):
    k, m = a.shape
    n = b.shape[1]
    return _matmul(
        name, a, b, TN, (m // bm, n // bn, 1),
        pl.BlockSpec((k, bm), lambda i, j, kk: (0, i)),
        pl.BlockSpec((k, bn), lambda i, j, kk: (0, j)),
        pl.BlockSpec((bm, bn), lambda i, j, kk: (i, j)),
        (m, n), o_dtype, comm=comm,
    )


ROWS = 256


def _row_spec(width, col=0):
    return pl.BlockSpec((ROWS, width), lambda i: (i, col))


def _full_spec(shape):
    return pl.BlockSpec(shape, lambda *_: (0,) * len(shape))


def _rms_norm(name, x, g):
    def body(x_ref, g_ref, h_ref):
        xf = x_ref[...]
        r = lax.rsqrt(jnp.mean(xf * xf, axis=-1, keepdims=True) + EPS)
        h_ref[...] = ((xf * r) * g_ref[...]).astype(BF16)

    return pl.pallas_call(
        body, name=name, grid=(SEQ // ROWS,),
        in_specs=[_row_spec(D_MODEL), _full_spec((1, D_MODEL))],
        out_specs=_row_spec(D_MODEL),
        out_shape=_sds((SEQ, D_MODEL), BF16),
        compiler_params=_params(("parallel",)),
    )(x, g)


CONV_COLS = 256


def _shift_rows(u, k):
    rows = lax.broadcasted_iota(jnp.int32, u.shape, 0)
    return jnp.where(rows >= k, pltpu.roll(u, k, axis=0), 0.0)


def _conv_fwd(proj, conv_w):
    nblk = D_MODEL // CONV_COLS

    def body(cb_ref, cc_ref, cx_ref, w_ref, y_ref):
        u = cc_ref[...] * cx_ref[...]
        w = w_ref[...]
        cv = w[0:1, :] * _shift_rows(u, 2) + w[1:2, :] * _shift_rows(u, 1) + w[2:3, :] * u
        y_ref[...] = (cb_ref[...] * cv).astype(BF16)

    def col(part):
        return pl.BlockSpec((SEQ, CONV_COLS), lambda j: (0, part * nblk + j))

    return pl.pallas_call(
        body, name="conv_fwd", grid=(nblk,),
        in_specs=[col(0), col(1), col(2), pl.BlockSpec((3, CONV_COLS), lambda j: (0, j))],
        out_specs=pl.BlockSpec((SEQ, CONV_COLS), lambda j: (0, j)),
        out_shape=_sds((SEQ, D_MODEL), BF16),
        compiler_params=_params(("parallel",)),
    )(proj, proj, proj, conv_w)


ROPE_COLS = 256


def _rope_tables():
    f32 = np.float32
    inv_freq = (f32(ROPE_THETA) ** (-np.arange(0, ROT_DIM, 2, dtype=f32) / f32(ROT_DIM))).astype(f32)
    ang = np.arange(SEQ, dtype=f32)[:, None] * inv_freq[None, :]
    cos, sin = np.cos(ang).astype(f32), np.sin(ang).astype(f32)
    half = ROT_DIM // 2
    ones = np.ones((SEQ, HEAD_DIM - ROT_DIM), f32)
    zeros = np.zeros((SEQ, HEAD_DIM - ROT_DIM), f32)
    zh = np.zeros((SEQ, half), f32)
    c = np.concatenate([cos, cos, ones], axis=1)
    s_up = np.concatenate([-sin, zh, zeros], axis=1)
    s_dn = np.concatenate([zh, sin, zeros], axis=1)
    reps = ROPE_COLS // HEAD_DIM
    return tuple(jnp.asarray(np.tile(t, (1, reps))) for t in (c, s_up, s_dn))


def _rotate(t, c, s_up, s_dn):
    width = t.shape[1]
    half = ROT_DIM // 2
    return t * c + pltpu.roll(t, width - half, axis=1) * s_up + pltpu.roll(t, half, axis=1) * s_dn


N_QBLK = SEQ // BLOCK


def _attn_specs():
    prev = lambda n: jnp.maximum(n - 1, 0)
    q = pl.BlockSpec((BLOCK, D_ATTN), lambda n: (n, COL_Q // D_ATTN))
    k_prev = pl.BlockSpec((BLOCK, D_KV), lambda n: (prev(n), COL_K // D_KV))
    k_cur = pl.BlockSpec((BLOCK, D_KV), lambda n: (n, COL_K // D_KV))
    v_prev = pl.BlockSpec((BLOCK, D_KV), lambda n: (prev(n), COL_V // D_KV))
    v_cur = pl.BlockSpec((BLOCK, D_KV), lambda n: (n, COL_V // D_KV))
    tab_cur = pl.BlockSpec((BLOCK, ROPE_COLS), lambda n: (n, 0))
    tab_prev = pl.BlockSpec((BLOCK, ROPE_COLS), lambda n: (prev(n), 0))
    return [q, k_prev, k_cur, v_prev, v_cur] + [tab_cur] * 3 + [tab_prev] * 3


def _band_kv(kp_ref, kc_ref, vp_ref, vc_ref, tabs_cur, tabs_prev):
    k = jnp.concatenate([_rotate(kp_ref[...], *(t[...] for t in tabs_prev)),
                         _rotate(kc_ref[...], *(t[...] for t in tabs_cur))], axis=0)
    v = jnp.concatenate([vp_ref[...], vc_ref[...]], axis=0)
    return k, v


def _query_tiles(q_ref, tiles, tabs_cur):
    c, su, sd = (t[:, :LANES] for t in tabs_cur)
    return jnp.concatenate(
        [_rotate(q_ref[:, t * LANES:(t + 1) * LANES], c, su, sd).astype(BF16) for t in tiles], axis=0)


def _sink_row(sink_ref, tiles, par):
    return jnp.concatenate([jnp.full((1, BLOCK), sink_ref[0, t * HEADS_PER_TILE + par], F32) for t in tiles], axis=1)


def _band_mask(n):
    kj = lax.broadcasted_iota(jnp.int32, (2 * BLOCK, BLOCK), 0)
    qi = lax.broadcasted_iota(jnp.int32, (2 * BLOCK, BLOCK), 1)
    rel = qi + BLOCK - kj
    return (rel >= 0) & (rel < BLOCK) & ((kj >= BLOCK) | (n > 0))


HEADS_PER_TILE = LANES // HEAD_DIM
TILES_PER_GROUP = GROUP // HEADS_PER_TILE


def _group_mask(n):
    return jnp.concatenate([_band_mask(n)] * TILES_PER_GROUP, axis=1)


def _lane_half(shape, par):
    lane = lax.broadcasted_iota(jnp.int32, shape, 1)
    return (lane < HEAD_DIM) if par == 0 else (lane >= HEAD_DIM)


def _head_tiles(kv, h):
    tile = kv[:, (h // HEADS_PER_TILE) * LANES:(h // HEADS_PER_TILE + 1) * LANES].astype(F32)
    own = jnp.where(_lane_half(tile.shape, h % HEADS_PER_TILE), tile, 0.0)
    other = pltpu.roll(own, HEAD_DIM, axis=1)
    lo, hi = (own, other) if h % HEADS_PER_TILE == 0 else (other, own)
    return lo.astype(BF16), hi.astype(BF16)


def _head_softmax(q_tile, k_half, sink, mask):
    s = lax.dot_general(k_half, q_tile, (NT, ((), ())), preferred_element_type=F32) * ATTN_SCALE
    s = jnp.where(mask, s, NEG_INF)
    m = jnp.maximum(jnp.max(s, axis=0, keepdims=True), sink)
    e = jnp.exp(s - m)
    es = jnp.exp(sink - m)
    inv = 1.0 / (jnp.sum(e, axis=0, keepdims=True) + es)
    return e * inv, es * inv


def _attn_fwd(proj, tables, sinks, comm=None):
    def body(sink_ref, q_ref, kp_ref, kc_ref, vp_ref, vc_ref, c_ref, su_ref, sd_ref, cp_ref, sup_ref, sdp_ref, o_ref):
        n = pl.program_id(0)
        mask = _group_mask(n)
        tabs_cur = (c_ref, su_ref, sd_ref)
        k, v = _band_kv(kp_ref, kc_ref, vp_ref, vc_ref, tabs_cur, (cp_ref, sup_ref, sdp_ref))
        for h in range(N_KV_HEADS):
            k_halves = _head_tiles(k, h)
            v_halves = _head_tiles(v, h)
            tiles = [h * TILES_PER_GROUP + t for t in range(TILES_PER_GROUP)]
            q_rows = _query_tiles(q_ref, tiles, tabs_cur)
            acc = None
            for par in range(HEADS_PER_TILE):
                p, _ = _head_softmax(q_rows, k_halves[par], _sink_row(sink_ref, tiles, par), mask)
                o = lax.dot_general(p.astype(BF16), v_halves[par], (TN, ((), ())), preferred_element_type=F32)
                acc = o if acc is None else acc + o
            for i, tile in enumerate(tiles):
                o_ref[:, tile * LANES:(tile + 1) * LANES] = acc[i * BLOCK:(i + 1) * BLOCK, :].astype(BF16)

    return _pcall(
        body, "attn_fwd", (N_QBLK,),
        [pl.BlockSpec(memory_space=pltpu.SMEM)] + _attn_specs(),
        pl.BlockSpec((BLOCK, D_ATTN), lambda n: (n, 0)),
        _sds((SEQ, D_ATTN), BF16), [sinks] + [proj] * 5 + list(tables) * 2, (), ("parallel",), comm)


def _branch_merge(conv_y, attn, w_co, w_ao, proj, comm=None):
    bm, bn = 1024, 512

    def body(cy_ref, at_ref, wc_ref, wa_ref, gc_ref, ga_ref, co_ref, ao_ref, mg_ref):
        co = jnp.dot(cy_ref[...], wc_ref[...], preferred_element_type=F32)
        ao = jnp.dot(at_ref[...], wa_ref[...], preferred_element_type=F32)
        co_ref[...] = co
        ao_ref[...] = ao
        mg_ref[...] = (jax.nn.sigmoid(gc_ref[...]) * co + jax.nn.sigmoid(ga_ref[...]) * ao).astype(BF16)

    act = pl.BlockSpec((bm, D_MODEL), lambda i, j: (i, 0))
    wgt = pl.BlockSpec((D_MODEL, bn), lambda i, j: (0, j))
    out = pl.BlockSpec((bm, bn), lambda i, j: (i, j))
    return _pcall(
        body, "branch_merge", (SEQ // bm, D_MODEL // bn),
        [act, act, wgt, wgt,
         pl.BlockSpec((bm, bn), lambda i, j: (i, COL_GC // bn + j)),
         pl.BlockSpec((bm, bn), lambda i, j: (i, COL_GA // bn + j))],
        [out, out, out],
        [_sds((SEQ, D_MODEL), F32), _sds((SEQ, D_MODEL), F32), _sds((SEQ, D_MODEL), BF16)],
        [conv_y, attn, w_co, w_ao, proj, proj], (), ("parallel", "parallel"), comm)


FF_BM, FF_BN = 512, 1408
FF_NB = D_FF // FF_BN


def _gate_up_fwd(h2, w_gu, comm=None):
    def body(h_ref, wg_ref, wu_ref, g_ref, u_ref, a_ref):
        h = h_ref[...]
        g = jnp.dot(h, wg_ref[...], preferred_element_type=F32)
        u = jnp.dot(h, wu_ref[...], preferred_element_type=F32)
        g_ref[...] = g
        u_ref[...] = u
        a_ref[...] = (jax.nn.silu(g) * u).astype(BF16)

    out = pl.BlockSpec((FF_BM, FF_BN), lambda i, j: (i, j))
    f32, b16 = _sds((SEQ, D_FF), F32), _sds((SEQ, D_FF), BF16)
    return _pcall(
        body, "mm_gate_up", (SEQ // FF_BM, FF_NB),
        [pl.BlockSpec((FF_BM, D_MODEL), lambda i, j: (i, 0)),
         pl.BlockSpec((D_MODEL, FF_BN), lambda i, j: (0, j)),
         pl.BlockSpec((D_MODEL, FF_BN), lambda i, j: (0, FF_NB + j))],
        [out, out, out], [f32, f32, b16], [h2, w_gu, w_gu], (), ("parallel", "parallel"), comm)


def _dact_swiglu(dx3b, w_down, g, u, comm=None):
    def body(dx_ref, w_ref, g_ref, u_ref, dg_ref, du_ref):
        da = lax.dot_general(dx_ref[...], w_ref[...], (NT, ((), ())), preferred_element_type=F32)
        g = g_ref[...]
        sg = jax.nn.sigmoid(g)
        dg_ref[...] = (da * u_ref[...] * (sg * (1.0 + g * (1.0 - sg)))).astype(BF16)
        du_ref[...] = (da * (g * sg)).astype(BF16)

    blk = pl.BlockSpec((FF_BM, FF_BN), lambda i, j: (i, j))
    b16 = _sds((SEQ, D_FF), BF16)
    return _pcall(
        body, "mm_dact", (SEQ // FF_BM, FF_NB),
        [pl.BlockSpec((FF_BM, D_MODEL), lambda i, j: (i, 0)), pl.BlockSpec((FF_BN, D_MODEL), lambda i, j: (j, 0)),
         blk, blk],
        [blk, blk], [b16, b16], [dx3b, w_down, g, u], (), ("parallel", "parallel"), comm, start_after_body=True)


def _mm_dh2(dg, du, w_gu, comm=None):
    bm = 1024
    nk = 2 * FF_NB

    def body(dg_ref, du_ref, w_ref, o_ref, acc_ref):
        k = pl.program_id(1)

        def part(a_ref):
            return lax.dot_general(a_ref[...], w_ref[...], (NT, ((), ())), preferred_element_type=F32)

        @pl.when(k == 0)
        def _():
            acc_ref[...] = part(dg_ref)

        @pl.when((k > 0) & (k < FF_NB))
        def _():
            acc_ref[...] += part(dg_ref)

        @pl.when(k >= FF_NB)
        def _():
            acc_ref[...] += part(du_ref)

        @pl.when(k == nk - 1)
        def _():
            o_ref[...] = acc_ref[...]

    return _pcall(
        body, "mm_dh2", (SEQ // bm, nk),
        [pl.BlockSpec((bm, FF_BN), lambda i, k: (i, jnp.minimum(k, FF_NB - 1))),
         pl.BlockSpec((bm, FF_BN), lambda i, k: (i, jnp.maximum(k - FF_NB, 0))),
         pl.BlockSpec((D_MODEL, FF_BN), lambda i, k: (0, k))],
        pl.BlockSpec((bm, D_MODEL), lambda i, k: (i, 0)), _sds((SEQ, D_MODEL), F32),
        [dg, du, w_gu], [pltpu.VMEM((bm, D_MODEL), F32)], ("parallel", "arbitrary"), comm)


def _mm_dw_gate_up(h2, dg, du, comm=None):
    def body(h_ref, dg_ref, du_ref, o_ref):
        j = pl.program_id(0)

        def part(b_ref):
            return lax.dot_general(h_ref[...], b_ref[...], (TN, ((), ())), preferred_element_type=F32).astype(BF16)

        @pl.when(j < FF_NB)
        def _():
            o_ref[...] = part(dg_ref)

        @pl.when(j >= FF_NB)
        def _():
            o_ref[...] = part(du_ref)

    return _pcall(
        body, "mm_dw_gate_up", (2 * FF_NB,),
        [_full_spec((SEQ, D_MODEL)),
         pl.BlockSpec((SEQ, FF_BN), lambda j: (0, jnp.minimum(j, FF_NB - 1))),
         pl.BlockSpec((SEQ, FF_BN), lambda j: (0, jnp.maximum(j - FF_NB, 0)))],
        pl.BlockSpec((D_MODEL, FF_BN), lambda j: (0, j)),
        _sds((D_MODEL, 2 * D_FF), BF16), [h2, dg, du], (), ("arbitrary",), comm)


def _loss_head(x3, g, target):
    def body(x_ref, g_ref, t_ref, dx_ref, dxb_ref, dg_ref, loss_ref):
        i = pl.program_id(0)
        xf = x_ref[...]
        r = lax.rsqrt(jnp.mean(xf * xf, axis=-1, keepdims=True) + EPS)
        xn = xf * r
        gg = g_ref[...]
        err = xn * gg - t_ref[...]
        part = 0.5 * jnp.sum(jnp.mean(err * err, axis=-1, keepdims=True), axis=0, keepdims=True)
        dy = err * (1.0 / D_MODEL)
        dxn = dy * gg
        dx = r * (dxn - xn * jnp.mean(dxn * xn, axis=-1, keepdims=True))
        dx_ref[...] = dx
        dxb_ref[...] = dx.astype(BF16)
        dg = jnp.sum(dy * xn, axis=0, keepdims=True)
        lane0 = lax.broadcasted_iota(jnp.int32, (1, LANES), 1) == 0
        lpart = jnp.where(lane0, part, 0.0)

        @pl.when(i == 0)
        def _():
            dg_ref[...] = dg
            loss_ref[...] = lpart

        @pl.when(i > 0)
        def _():
            dg_ref[...] += dg
            loss_ref[...] += lpart

    return pl.pallas_call(
        body, name="loss_head", grid=(SEQ // ROWS,),
        in_specs=[_row_spec(D_MODEL), _full_spec((1, D_MODEL)), _row_spec(D_MODEL)],
        out_specs=[_row_spec(D_MODEL), _row_spec(D_MODEL), _full_spec((1, D_MODEL)), _full_spec((1, LANES))],
        out_shape=[_sds((SEQ, D_MODEL), F32), _sds((SEQ, D_MODEL), BF16),
                   _sds((1, D_MODEL), F32), _sds((1, LANES), F32)],
        compiler_params=_params(("arbitrary",)),
    )(x3, g, target)


def _rms_norm_bwd(name, dh, x, g, dres, with_bf16, comm=None):
    def body(dh_ref, x_ref, g_ref, dr_ref, *outs):
        i = pl.program_id(0)
        dx_ref = outs[0]
        dg_ref = outs[-1]
        xf = x_ref[...]
        r = lax.rsqrt(jnp.mean(xf * xf, axis=-1, keepdims=True) + EPS)
        xn = xf * r
        dh = dh_ref[...]
        dxn = dh * g_ref[...]
        dx = dr_ref[...] + r * (dxn - xn * jnp.mean(dxn * xn, axis=-1, keepdims=True))
        dx_ref[...] = dx
        if with_bf16:
            outs[1][...] = dx.astype(BF16)
        dg = jnp.sum(dh * xn, axis=0, keepdims=True)

        @pl.when(i == 0)
        def _():
            dg_ref[...] = dg

        @pl.when(i > 0)
        def _():
            dg_ref[...] += dg

    row = _row_spec(D_MODEL)
    out_specs = [row] + ([row] if with_bf16 else []) + [_full_spec((1, D_MODEL))]
    out_shape = ([_sds((SEQ, D_MODEL), F32)] + ([_sds((SEQ, D_MODEL), BF16)] if with_bf16 else [])
                 + [_sds((1, D_MODEL), F32)])
    return _pcall(body, name, (SEQ // ROWS,), [row, row, _full_spec((1, D_MODEL)), row], out_specs, out_shape,
                  [dh, x, g, dres], (), ("arbitrary",), comm)


def _merge_bwd(dx2b, w_o, conv_out, attn_out, proj):
    bm, bn = 1024, D_MODEL // 2

    def body(dx_ref, w_ref, co_ref, ao_ref, gc_ref, ga_ref, dco_ref, dao_ref, dgc_ref, dga_ref):
        dm = lax.dot_general(dx_ref[...], w_ref[...], (NT, ((), ())), preferred_element_type=F32)
        sc = jax.nn.sigmoid(gc_ref[...])
        sa = jax.nn.sigmoid(ga_ref[...])
        dco_ref[...] = (dm * sc).astype(BF16)
        dao_ref[...] = (dm * sa).astype(BF16)
        dgc_ref[...] = (dm * co_ref[...] * (sc * (1.0 - sc))).astype(BF16)
        dga_ref[...] = (dm * ao_ref[...] * (sa * (1.0 - sa))).astype(BF16)

    own = pl.BlockSpec((bm, bn), lambda i, j: (i, j))
    sd = _sds((SEQ, D_MODEL), BF16)
    return pl.pallas_call(
        body, name="mm_dmerged", grid=(SEQ // bm, D_MODEL // bn),
        in_specs=[pl.BlockSpec((bm, D_MODEL), lambda i, j: (i, 0)), pl.BlockSpec((bn, D_MODEL), lambda i, j: (j, 0)),
                  own, own,
                  pl.BlockSpec((bm, bn), lambda i, j: (i, COL_GC // bn + j)),
                  pl.BlockSpec((bm, bn), lambda i, j: (i, COL_GA // bn + j))],
        out_specs=[own, own, own, own], out_shape=[sd, sd, sd, sd],
        compiler_params=_params(("parallel", "parallel")),
    )(dx2b, w_o, conv_out, attn_out, proj, proj)


def _conv_bwd(dconv_y, proj, conv_w, comm=None):
    nblk = D_MODEL // CONV_COLS

    def body(dy_ref, cb_ref, cc_ref, cx_ref, w_ref, dcb_ref, dcc_ref, dcx_ref, dw_ref):
        cc = cc_ref[...]
        cx = cx_ref[...]
        u = cc * cx
        w = w_ref[...]
        u1 = _shift_rows(u, 1)
        u2 = _shift_rows(u, 2)
        cv = w[0:1, :] * u2 + w[1:2, :] * u1 + w[2:3, :] * u
        dy = dy_ref[...]
        dcb_ref[...] = (dy * cv).astype(BF16)
        dcv = dy * cb_ref[...]
        rows = lax.broadcasted_iota(jnp.int32, dcv.shape, 0)
        up1 = jnp.where(rows < SEQ - 1, pltpu.roll(dcv, SEQ - 1, axis=0), 0.0)
        up2 = jnp.where(rows < SEQ - 2, pltpu.roll(dcv, SEQ - 2, axis=0), 0.0)
        du = w[2:3, :] * dcv + w[1:2, :] * up1 + w[0:1, :] * up2
        dcc_ref[...] = (du * cx).astype(BF16)
        dcx_ref[...] = (du * cc).astype(BF16)
        dw_ref[...] = jnp.concatenate(
            [jnp.sum(dcv * u2, axis=0, keepdims=True),
             jnp.sum(dcv * u1, axis=0, keepdims=True),
             jnp.sum(dcv * u, axis=0, keepdims=True)], axis=0)

    def col(part):
        return pl.BlockSpec((SEQ, CONV_COLS), lambda j: (0, part * nblk + j))

    own = pl.BlockSpec((SEQ, CONV_COLS), lambda j: (0, j))
    wsp = pl.BlockSpec((3, CONV_COLS), lambda j: (0, j))
    sd = _sds((SEQ, D_MODEL), BF16)
    return _pcall(
        body, "conv_bwd", (nblk,), [own, col(0), col(1), col(2), wsp], [own, own, own, wsp],
        [sd, sd, sd, _sds((3, D_MODEL), F32)], [dconv_y, proj, proj, proj, conv_w], (), ("parallel",), comm)


def _attn_bwd(proj, dattn, sinks, tables, comm=None):
    def body(sink_ref, q_ref, kp_ref, kc_ref, vp_ref, vc_ref, c_ref, su_ref, sd_ref, cp_ref, sup_ref, sdp_ref,
             do_ref, dq_ref, dkp_ref, dkc_ref, dvp_ref, dvc_ref, ds_ref):
        n = pl.program_id(0)
        mask = _group_mask(n)
        tabs_cur = (c_ref, su_ref, sd_ref)
        k, v = _band_kv(kp_ref, kc_ref, vp_ref, vc_ref, tabs_cur, (cp_ref, sup_ref, sdp_ref))
        lane = lax.broadcasted_iota(jnp.int32, (1, LANES), 1)
        dsink = jnp.zeros((1, LANES), F32)
        c, su, sd = c_ref[:, :LANES], su_ref[:, :LANES], sd_ref[:, :LANES]
        dk_tiles = [None] * (N_KV_HEADS // HEADS_PER_TILE)
        dv_tiles = [None] * (N_KV_HEADS // HEADS_PER_TILE)
        for h in range(N_KV_HEADS):
            k_halves = _head_tiles(k, h)
            v_halves = _head_tiles(v, h)
            tiles = [h * TILES_PER_GROUP + t for t in range(TILES_PER_GROUP)]
            q_rows = _query_tiles(q_ref, tiles, tabs_cur)
            do_rows = jnp.concatenate([do_ref[:, t * LANES:(t + 1) * LANES] for t in tiles], axis=0)
            dk_par, dv_par = [], []
            dq_rows = None
            for par in range(HEADS_PER_TILE):
                p, p_sink = _head_softmax(q_rows, k_halves[par], _sink_row(sink_ref, tiles, par), mask)
                dp = lax.dot_general(v_halves[par], do_rows, (NT, ((), ())), preferred_element_type=F32)
                delta = jnp.sum(p * dp, axis=0, keepdims=True)
                ds = (p * (dp - delta) * ATTN_SCALE).astype(BF16)
                dq = lax.dot_general(ds, k_halves[par], (TN, ((), ())), preferred_element_type=F32)
                dq_rows = dq if dq_rows is None else dq_rows + dq
                dk_par.append(jnp.dot(ds, q_rows, preferred_element_type=F32))
                dv_par.append(jnp.dot(p.astype(BF16), do_rows, preferred_element_type=F32))
                sink_grad = p_sink * delta
                for i, tile in enumerate(tiles):
                    val = -jnp.sum(sink_grad[:, i * BLOCK:(i + 1) * BLOCK], axis=1, keepdims=True)
                    dsink = dsink + jnp.where(lane == tile * HEADS_PER_TILE + par, val, 0.0)
            for i, tile in enumerate(tiles):
                dq_tile = dq_rows[i * BLOCK:(i + 1) * BLOCK, :]
                dq_ref[:, tile * LANES:(tile + 1) * LANES] = _rotate(dq_tile, c, -su, -sd).astype(BF16)
            own = h % HEADS_PER_TILE
            for par_grads, tiles in ((dk_par, dk_tiles), (dv_par, dv_tiles)):
                shifted = pltpu.roll(par_grads[1 - own], HEAD_DIM, axis=1)
                total = jnp.where(_lane_half(shifted.shape, own), par_grads[own] + shifted, 0.0)
                i = h // HEADS_PER_TILE
                tiles[i] = total if tiles[i] is None else tiles[i] + total
        for i in range(N_KV_HEADS // HEADS_PER_TILE):
            cols = slice(i * LANES, (i + 1) * LANES)
            dkp_ref[:, cols] = dk_tiles[i][:BLOCK, :]
            dkc_ref[:, cols] = dk_tiles[i][BLOCK:, :]
            dvp_ref[:, cols] = dv_tiles[i][:BLOCK, :]
            dvc_ref[:, cols] = dv_tiles[i][BLOCK:, :]

        @pl.when(n == 0)
        def _():
            ds_ref[...] = dsink

        @pl.when(n > 0)
        def _():
            ds_ref[...] += dsink

    blk = pl.BlockSpec((BLOCK, D_KV), lambda n: (n, 0))
    prev_blk = pl.BlockSpec((BLOCK, D_KV), lambda n: ((n + N_QBLK - 1) % N_QBLK, 0))
    kv = _sds((SEQ, D_KV), F32)
    return _pcall(
        body, "attn_bwd", (N_QBLK,),
        [pl.BlockSpec(memory_space=pltpu.SMEM)] + _attn_specs() + [pl.BlockSpec((BLOCK, D_ATTN), lambda n: (n, 0))],
        [pl.BlockSpec((BLOCK, D_ATTN), lambda n: (n, 0)), prev_blk, blk, prev_blk, blk, _full_spec((1, LANES))],
        [_sds((SEQ, D_ATTN), BF16), kv, kv, kv, kv, _sds((1, LANES), F32)],
        [sinks] + [proj] * 5 + list(tables) * 2 + [dattn], (), ("arbitrary",), comm)


def _kv_grad_combine(dk_prev, dk_cur, dv_prev, dv_cur, tables):
    rows = 4 * BLOCK

    def body(kp_ref, kc_ref, vp_ref, vc_ref, c_ref, su_ref, sd_ref, o_ref):
        dk = kc_ref[...] + kp_ref[...]
        dv = vc_ref[...] + vp_ref[...]
        o_ref[:, :D_KV] = _rotate(dk, c_ref[...], -su_ref[...], -sd_ref[...]).astype(BF16)
        o_ref[:, D_KV:] = dv.astype(BF16)

    blk = pl.BlockSpec((rows, D_KV), lambda m: (m, 0))
    return pl.pallas_call(
        body, name="kv_grad_combine", grid=(SEQ // rows,),
        in_specs=[blk] * 7,
        out_specs=pl.BlockSpec((rows, 2 * D_KV), lambda m: (m, 0)),
        out_shape=_sds((SEQ, 2 * D_KV), BF16),
        compiler_params=_params(("parallel",)),
    )(dk_prev, dk_cur, dv_prev, dv_cur, *tables)


MATRICES = {
    "w_in": (D_MODEL, N_IN // N_CHIPS, "col"),
    "w_conv_out": (D_MODEL // N_CHIPS, D_MODEL, "row"),
    "w_attn_out": (D_MODEL // N_CHIPS, D_MODEL, "row"),
    "w_o": (D_MODEL // N_CHIPS, D_MODEL, "row"),
    "w_gate_up": (D_MODEL, 2 * D_FF // N_CHIPS, "col"),
    "w_down": (D_FF // N_CHIPS, D_MODEL, "row"),
}
BF16_ROW_TILE = 16
CONV_W_COLS = D_MODEL // N_CHIPS
SMALL_ROWS = 8


def _whole_shape(spec):
    rows, cols, kind = spec
    return (rows, cols * N_CHIPS) if kind == "col" else (rows * N_CHIPS, cols)


def _half_shape(spec):
    return (spec[0] // 2, spec[1])


def _aligned(start, multiple):
    return start if isinstance(start, int) else pl.multiple_of(start, multiple)


def _region(ref, spec, shard, half, part=0, parts=1):
    rows, cols, kind = spec
    hr = rows // 2
    n = hr // parts
    if kind == "col":
        return ref.at[pl.ds(_aligned(half * hr + part * n, BF16_ROW_TILE), n),
                      pl.ds(_aligned(shard * cols, LANES), cols)]
    return ref.at[pl.ds(_aligned(shard * rows + half * hr + part * n, BF16_ROW_TILE), n), :]


def _position():
    x, y, c = lax.axis_index("x"), lax.axis_index("y"), lax.axis_index("c")
    chips = [(1 - x, y), (x, 1 - y), (1 - x, 1 - y)]
    return x, y, c, chips


def _shard_of(chip):
    return 2 * chip[0] + chip[1]


def _remote(src, dst, send_sem, recv_sem, to):
    return pltpu.make_async_remote_copy(src_ref=src, dst_ref=dst, send_sem=send_sem, recv_sem=recv_sem,
                                        device_id=to, device_id_type=MESH)


CAST_STEPS = 4


def _to_bf16_in_whole(ws, specs, shard):
    n = len(ws)

    def body(s_ref, *refs):
        del s_ref
        for w_ref, o_ref in zip(refs[:n], refs[n:]):
            o_ref[...] = w_ref[...].astype(BF16)

    def out_spec(spec):
        rows = spec[0] // CAST_STEPS
        if spec[2] == "col":
            return pl.BlockSpec((rows, spec[1]), lambda i, s_ref: (i, s_ref[0]))
        return pl.BlockSpec((rows, spec[1]), lambda i, s_ref: (s_ref[0] * CAST_STEPS + i, 0))

    grid_spec = pltpu.PrefetchScalarGridSpec(
        num_scalar_prefetch=1, grid=(CAST_STEPS,),
        in_specs=[pl.BlockSpec((s[0] // CAST_STEPS, s[1]), lambda i, s_ref: (i, 0)) for s in specs],
        out_specs=[out_spec(s) for s in specs])
    return list(pl.pallas_call(
        body, name="cast_shards", grid_spec=grid_spec, out_shape=[_sds(_whole_shape(s), BF16) for s in specs],
        compiler_params=_params(("parallel",)),
    )(shard, *ws))


class _Gather:
    def __init__(self, wholes, pieces, conv_w=None):
        self.pieces = pieces
        self.n = len(wholes)
        self.with_conv_w = conv_w is not None
        self.operands = list(wholes) + ([conv_w] if self.with_conv_w else [])
        self.out_shape = [_sds(w.shape, w.dtype) for w in wholes]
        if self.with_conv_w:
            self.out_shape.append(_sds((3, D_MODEL), F32))
        self.aliases = {i: i for i in range(self.n)}
        n_ici = 3 * len(pieces)
        self.sems = [pltpu.SemaphoreType.DMA((n_ici,))] * 4
        if self.with_conv_w:
            self.sems += [pltpu.SemaphoreType.DMA((1,)), pltpu.SemaphoreType.DMA((3,)), pltpu.SemaphoreType.DMA((3,))]

    def _conv_w(self, cins, couts, sems, with_recvs):
        cw_in, cw_out = cins[self.n], couts[self.n]
        x, y, c, chips = _position()

        def cols(shard):
            return cw_out.at[:, pl.ds(_aligned(shard * CONV_W_COLS, LANES), CONV_W_COLS)]

        me = _shard_of((x, y))
        local = pltpu.make_async_copy(cw_in, cols(me), sems[4].at[0])
        sends = [_remote(cw_in, cols(me), sems[5].at[j], sems[6].at[j], (*chip, c)) for j, chip in enumerate(chips)]
        if not with_recvs:
            return local, sends, []
        recvs = [_remote(cols(_shard_of(chip)), cols(_shard_of(chip)), sems[5].at[j], sems[6].at[j], (*chip, c))
                 for j, chip in enumerate(chips)]
        return local, sends, recvs

    def start(self, cins, couts, sems):
        x, y, c, chips = _position()
        me = _shard_of((x, y))
        if self.with_conv_w:
            local, sends, _ = self._conv_w(cins, couts, sems, False)
            local.start()
            for cp in sends:
                cp.start()
        for p, (i, spec, part, parts) in enumerate(self.pieces):
            mine = _region(couts[i], spec, me, c, part, parts)
            for j, chip in enumerate(chips):
                _remote(mine, mine, sems[0].at[3 * p + j], sems[1].at[3 * p + j], (*chip, c)).start()

    def finish(self, cins, couts, sems):
        x, y, c, chips = _position()
        me = _shard_of((x, y))
        sibling = (x, y, 1 - c)
        send_a, recv_a, send_b, recv_b = sems[:4]
        passed = []
        for p, (i, spec, part, parts) in enumerate(self.pieces):
            for j, chip in enumerate(chips):
                k = 3 * p + j
                landed = _region(couts[i], spec, _shard_of(chip), c, part, parts)
                _remote(landed, landed, send_a.at[k], recv_a.at[k], (*chip, c)).wait_recv()
                cp = _remote(landed, landed, send_b.at[k], recv_b.at[k], sibling)
                cp.start()
                passed.append(cp)
        for p, (i, spec, part, parts) in enumerate(self.pieces):
            mine = _region(couts[i], spec, me, c, part, parts)
            for j, chip in enumerate(chips):
                k = 3 * p + j
                other = _region(couts[i], spec, _shard_of(chip), 1 - c, part, parts)
                _remote(other, other, send_b.at[k], recv_b.at[k], sibling).wait_recv()
                _remote(mine, mine, send_a.at[k], recv_a.at[k], (*chip, c)).wait_send()
        for cp in passed:
            cp.wait_send()
        if self.with_conv_w:
            local, sends, recvs = self._conv_w(cins, couts, sems, True)
            for cp in recvs:
                cp.wait_recv()
            for cp in sends:
                cp.wait_send()
            local.wait()


def _mm_in_gather(h1, w_whole, comm):
    spec = MATRICES["w_in"]
    cols = spec[1]
    bm = SEQ // 2

    def body(h_ref, w_in_ref, proj_ref, w_ref, wbuf, obuf, send_a, recv_a, send_b, recv_b, load_sem, store_sems):
        del w_in_ref
        s, mi = pl.program_id(0), pl.program_id(1)
        x, y, c, chips = _position()
        me = _shard_of((x, y))
        sibling = (x, y, 1 - c)
        mine = _region(w_ref, spec, me, c)

        @pl.when((s == 0) & (mi == 0))
        def _():
            for j, chip in enumerate(chips):
                _remote(mine, mine, send_a.at[j], recv_a.at[j], (*chip, c)).start()

        shard = me
        for j, chip in enumerate(chips):
            shard = jnp.where(s == j + 1, _shard_of(chip), shard)

            @pl.when((s == j + 1) & (mi == 0))
            def _():
                landed = _region(w_ref, spec, _shard_of(chip), c)
                _remote(landed, landed, send_a.at[j], recv_a.at[j], (*chip, c)).wait_recv()
                _remote(landed, landed, send_b.at[j], recv_b.at[j], sibling).start()
                other = _region(w_ref, spec, _shard_of(chip), 1 - c)
                _remote(other, other, send_b.at[j], recv_b.at[j], sibling).wait_recv()

        col0 = pl.multiple_of(shard * cols, LANES)

        @pl.when(mi == 0)
        def _():
            load = pltpu.make_async_copy(w_ref.at[:, pl.ds(col0, cols)], wbuf, load_sem.at[0])
            load.start()
            load.wait()

        def store():
            rows = pl.ds(pl.multiple_of(mi * bm, bm), bm)
            return pltpu.make_async_copy(obuf.at[mi], proj_ref.at[rows, pl.ds(col0, cols)], store_sems.at[mi])

        @pl.when(s > 0)
        def _():
            store().wait()

        obuf[mi] = jnp.dot(h_ref[...], wbuf[...], preferred_element_type=F32)
        store().start()

        @pl.when(s == N_CHIPS - 1)
        def _():
            store().wait()

        @pl.when((s == N_CHIPS - 1) & (mi == 1))
        def _():
            for j, chip in enumerate(chips):
                landed = _region(w_ref, spec, _shard_of(chip), c)
                _remote(mine, mine, send_a.at[j], recv_a.at[j], (*chip, c)).wait_send()
                _remote(landed, landed, send_b.at[j], recv_b.at[j], sibling).wait_send()

    sem3 = pltpu.SemaphoreType.DMA((3,))
    (proj, whole), extra = _pcall(
        body, "mm_in", (N_CHIPS, SEQ // bm),
        [pl.BlockSpec((bm, D_MODEL), lambda s, m: (m, 0)), HBM_SPEC], [HBM_SPEC, HBM_SPEC],
        [_sds((SEQ, N_IN), F32), _sds(w_whole.shape, w_whole.dtype)], [h1, w_whole],
        [pltpu.VMEM((D_MODEL, cols), BF16), pltpu.VMEM((SEQ // bm, bm, cols), F32), sem3, sem3, sem3, sem3,
         pltpu.SemaphoreType.DMA((1,)), pltpu.SemaphoreType.DMA((SEQ // bm,))],
        None, comm, aliases={1: 1}, start_after_body=True)
    return proj, whole, extra


def _pack_small(dg_mix, dg_ffn, dg_final, dconv_w, dsinks, loss_row):
    def body(a_ref, b_ref, c_ref, w_ref, s_ref, l_ref, o_ref):
        pad = jnp.zeros((1, D_MODEL - LANES), F32)
        o_ref[0:1, :] = a_ref[...]
        o_ref[1:2, :] = b_ref[...]
        o_ref[2:3, :] = c_ref[...]
        o_ref[3:6, :] = w_ref[...]
        o_ref[6:7, :] = jnp.concatenate([s_ref[...], pad], axis=1)
        o_ref[7:8, :] = jnp.concatenate([l_ref[...], pad], axis=1)

    return pl.pallas_call(
        body, name="pack_small", out_shape=_sds((SMALL_ROWS, D_MODEL), F32),
        compiler_params=_params(),
    )(dg_mix, dg_ffn, dg_final, dconv_w, dsinks, loss_row)


class _Pair:
    def __init__(self, dws, specs):
        self.specs = specs
        self.operands = list(dws)
        self.out_shape = [_sds((N_CHIPS, *_half_shape(s)), BF16) for s in specs]
        self.aliases = {}
        n = N_CHIPS * len(specs)
        self.sems = [pltpu.SemaphoreType.DMA((n,)), pltpu.SemaphoreType.DMA((n,))]

    def _copies(self, cins, couts, sems):
        x, y, c, _ = _position()
        sibling = (x, y, 1 - c)
        for i, spec in enumerate(self.specs):
            for t in range(N_CHIPS):
                k = N_CHIPS * i + t
                yield _remote(_region(cins[i], spec, t, 1 - c), couts[i].at[t], sems[0].at[k], sems[1].at[k], sibling)

    def start(self, cins, couts, sems):
        for cp in self._copies(cins, couts, sems):
            cp.start()

    def finish(self, cins, couts, sems):
        for cp in self._copies(cins, couts, sems):
            cp.wait()


class _SmallAllToAll:
    def __init__(self, small):
        self.operands = [small]
        self.out_shape = [_sds((N_DEV, SMALL_ROWS, D_MODEL), F32)]
        self.aliases = {}
        self.sems = [pltpu.SemaphoreType.DMA((N_DEV - 1,)), pltpu.SemaphoreType.DMA((N_DEV - 1,)),
                     pltpu.SemaphoreType.DMA((1,))]

    def _copies(self, cins, couts, sems):
        x, y, c, _ = _position()
        me = 4 * x + 2 * y + c
        out = []
        for r in range(1, N_DEV):
            flip = ((r >> 2) & 1, (r >> 1) & 1, r & 1)
            peer = tuple(1 - p if f else p for p, f in zip((x, y, c), flip))
            theirs = couts[0].at[4 * peer[0] + 2 * peer[1] + peer[2]]
            out.append((_remote(cins[0], couts[0].at[me], sems[0].at[r - 1], sems[1].at[r - 1], peer),
                        functools.partial(_remote, theirs, theirs, sems[0].at[r - 1], sems[1].at[r - 1], peer)))
        return pltpu.make_async_copy(cins[0], couts[0].at[me], sems[2].at[0]), out

    def start(self, cins, couts, sems):
        own, copies = self._copies(cins, couts, sems)
        own.start()
        for send, _ in copies:
            send.start()

    def finish(self, cins, couts, sems):
        own, copies = self._copies(cins, couts, sems)
        for send, recv in copies:
            recv().wait_recv()
            send.wait_send()
        own.wait()


class _Both:
    def __init__(self, a, b):
        self.a, self.b = a, b
        self.operands = list(a.operands) + list(b.operands)
        self.out_shape = list(a.out_shape) + list(b.out_shape)
        self.aliases = dict(a.aliases)
        self.aliases.update({len(a.operands) + k: len(a.out_shape) + v for k, v in b.aliases.items()})
        self.sems = list(a.sems) + list(b.sems)

    def _split(self, cins, couts, sems):
        na, ma, sa = len(self.a.operands), len(self.a.out_shape), len(self.a.sems)
        return (cins[:na], couts[:ma], sems[:sa]), (cins[na:], couts[ma:], sems[sa:])

    def start(self, cins, couts, sems):
        for plan, args in zip((self.a, self.b), self._split(cins, couts, sems)):
            plan.start(*args)

    def finish(self, cins, couts, sems):
        for plan, args in zip((self.a, self.b), self._split(cins, couts, sems)):
            plan.finish(*args)


def _pair_sum(name, specs, dws, got, place):
    n_mat = len(specs)

    def body(p_ref, *refs):
        t = pl.program_id(0)
        mine, theirs = refs[:n_mat], refs[n_mat:2 * n_mat]
        outs, owns = refs[2 * n_mat:3 * n_mat], refs[3 * n_mat:]
        for a, b, o, own in zip(mine, theirs, outs, owns):
            s = (a[...].astype(F32) + b[...].astype(F32)).astype(BF16)
            o[...] = s

            @pl.when(t == p_ref[1])
            def _():
                own[...] = s

    def mine_spec(spec):
        hr, cols = _half_shape(spec)
        if spec[2] == "col":
            return pl.BlockSpec((hr, cols), lambda t, p_ref: (p_ref[0], t))
        return pl.BlockSpec((hr, cols), lambda t, p_ref: (2 * t + p_ref[0], 0))

    def slot_spec(spec):
        return pl.BlockSpec((None, *_half_shape(spec)), lambda t, p_ref: (t, 0, 0))

    def own_spec(spec):
        return pl.BlockSpec((None, *_half_shape(spec)), lambda t, p_ref: (p_ref[1], 0, 0))

    slots = [_sds((N_CHIPS, *_half_shape(s)), BF16) for s in specs]
    grid_spec = pltpu.PrefetchScalarGridSpec(
        num_scalar_prefetch=1, grid=(N_CHIPS,),
        in_specs=[mine_spec(s) for s in specs] + [slot_spec(s) for s in specs],
        out_specs=[slot_spec(s) for s in specs] + [own_spec(s) for s in specs])
    res = pl.pallas_call(
        body, name=name, grid_spec=grid_spec, out_shape=slots + slots,
        compiler_params=_params(("arbitrary",)),
    )(place, *dws, *got)
    return list(res[:n_mat]), list(res[n_mat:])


class _ChipExchange:
    def __init__(self, sums, slots, part=0, parts=1):
        self.n = len(sums)
        self.part, self.parts = part, parts
        self.operands = list(sums) + list(slots)
        self.out_shape = [_sds(s.shape, s.dtype) for s in slots]
        self.aliases = {self.n + i: i for i in range(self.n)}
        self.sems = [pltpu.SemaphoreType.DMA((3 * self.n,)), pltpu.SemaphoreType.DMA((3 * self.n,))]

    def _rows(self, ref, slot):
        n = ref.shape[1] // self.parts
        return ref.at[slot, pl.ds(self.part * n, n), :]

    def _copies(self, cins, couts, sems):
        x, y, c, chips = _position()
        me = _shard_of((x, y))
        for i in range(self.n):
            for j, chip in enumerate(chips):
                k = 3 * i + j
                theirs = self._rows(couts[i], _shard_of(chip))
                yield (_remote(self._rows(cins[i], _shard_of(chip)), self._rows(couts[i], me),
                               sems[0].at[k], sems[1].at[k], (*chip, c)),
                       functools.partial(_remote, theirs, theirs, sems[0].at[k], sems[1].at[k], (*chip, c)))

    def start(self, cins, couts, sems):
        for send, _ in self._copies(cins, couts, sems):
            send.start()

    def finish(self, cins, couts, sems):
        for send, recv in self._copies(cins, couts, sems):
            recv().wait_recv()
            send.wait_send()


def _chip_sum(name, specs, slots, core):
    steps = 2
    n_mat = len(specs)

    def body(c_ref, *refs):
        del c_ref
        ins, outs = refs[:n_mat], refs[n_mat:]
        for a, o in zip(ins, outs):
            acc = a[0].astype(F32)
            for t in range(1, N_CHIPS):
                acc = acc + a[t].astype(F32)
            o[...] = acc

    def in_spec(spec):
        hr, cols = _half_shape(spec)
        return pl.BlockSpec((N_CHIPS, hr // steps, cols), lambda i, c_ref: (0, i, 0))

    def out_spec(spec):
        hr, cols = _half_shape(spec)
        return pl.BlockSpec((hr // steps, cols), lambda i, c_ref: (c_ref[0] * steps + i, 0))

    grid_spec = pltpu.PrefetchScalarGridSpec(
        num_scalar_prefetch=1, grid=(steps,),
        in_specs=[in_spec(s) for s in specs], out_specs=[out_spec(s) for s in specs])
    return list(pl.pallas_call(
        body, name=name, grid_spec=grid_spec,
        out_shape=[_sds((s[0], s[1]), F32) for s in specs],
        compiler_params=_params(("parallel",)),
    )(core, *slots))


class _HalfExchange:
    def __init__(self, grads, specs):
        self.specs = specs
        self.operands = list(grads)
        self.out_shape = [_sds(g.shape, g.dtype) for g in grads]
        self.aliases = {i: i for i in range(len(grads))}
        self.sems = [pltpu.SemaphoreType.DMA((len(grads),)), pltpu.SemaphoreType.DMA((len(grads),))]

    def _copies(self, couts, sems):
        x, y, c, _ = _position()
        sibling = (x, y, 1 - c)
        for i, spec in enumerate(self.specs):
            hr = spec[0] // 2
            mine = couts[i].at[pl.ds(_aligned(c * hr, 8), hr), :]
            theirs = couts[i].at[pl.ds(_aligned((1 - c) * hr, 8), hr), :]
            yield (_remote(mine, mine, sems[0].at[i], sems[1].at[i], sibling),
                   functools.partial(_remote, theirs, theirs, sems[0].at[i], sems[1].at[i], sibling))

    def start(self, cins, couts, sems):
        for send, _ in self._copies(couts, sems):
            send.start()

    def finish(self, cins, couts, sems):
        for send, recv in self._copies(couts, sems):
            recv().wait_recv()
            send.wait_send()


def _small_sum(blocks):
    def body(b_ref, o_ref):
        acc = b_ref[0]
        for d in range(1, N_DEV):
            acc = acc + b_ref[d]
        o_ref[...] = acc

    return pl.pallas_call(
        body, name="small_sum", out_shape=_sds((SMALL_ROWS, D_MODEL), F32), compiler_params=_params(),
    )(blocks)


def _adamw(name, params, steps):
    n = len(params)

    def body(*refs):
        for p in range(n):
            w_ref, g_ref, m_ref, v_ref = refs[4 * p:4 * p + 4]
            d_ref, nm_ref, nv_ref, go_ref = refs[4 * n + 4 * p:4 * n + 4 * p + 4]
            g = g_ref[...]
            go_ref[...] = g
            m = ADAM_B1 * m_ref[...] + (1.0 - ADAM_B1) * g
            v = ADAM_B2 * v_ref[...] + (1.0 - ADAM_B2) * jnp.square(g)
            m_hat = m / (1.0 - ADAM_B1 ** ADAM_STEP)
            v_hat = v / (1.0 - ADAM_B2 ** ADAM_STEP)
            d_ref[...] = -ADAM_LR * (m_hat / (jnp.sqrt(v_hat) + ADAM_EPS) + ADAM_WD * w_ref[...])
            nm_ref[...] = m
            nv_ref[...] = v

    in_specs, out_specs, out_shape, operands = [], [], [], []
    for w, g, m, v in params:
        spec = pl.BlockSpec((w.shape[0] // steps, w.shape[1]), lambda i: (i, 0))
        in_specs += [spec] * 4
        out_specs += [spec] * 4
        out_shape += [_sds(w.shape, F32)] * 4
        operands += [w, g, m, v]
    outs = _pcall(body, name, (steps,), in_specs, out_specs, out_shape, operands, (), ("parallel",))
    return [tuple(outs[4 * p:4 * p + 4]) for p in range(n)]


MATRIX_NAMES = tuple(MATRICES)
WEIGHT_ORDER = ("g_mix", "w_in", "conv_w", "attn_sinks", "w_conv_out", "w_attn_out", "w_o", "g_ffn",
                "w_gate_up", "w_down", "g_final")


def kernel(x, g_mix, w_in, conv_w, attn_sinks, w_conv_out, w_attn_out, w_o, g_ffn, w_gate_up, w_down, g_final, loss_target, m_g_mix, m_w_in, m_conv_w, m_attn_sinks, m_w_conv_out, m_w_attn_out, m_w_o, m_g_ffn, m_w_gate_up, m_w_down, m_g_final, v_g_mix, v_w_in, v_conv_w, v_attn_sinks, v_w_conv_out, v_w_attn_out, v_w_o, v_g_ffn, v_w_gate_up, v_w_down, v_g_final):
    w = dict(g_mix=g_mix, w_in=w_in[0], conv_w=conv_w[0], attn_sinks=attn_sinks, w_conv_out=w_conv_out[0],
             w_attn_out=w_attn_out[0], w_o=w_o[0], g_ffn=g_ffn, w_gate_up=w_gate_up[0], w_down=w_down[0],
             g_final=g_final[None, :])
    m = dict(g_mix=m_g_mix, w_in=m_w_in[0], conv_w=m_conv_w[0], attn_sinks=m_attn_sinks,
             w_conv_out=m_w_conv_out[0], w_attn_out=m_w_attn_out[0], w_o=m_w_o[0], g_ffn=m_g_ffn,
             w_gate_up=m_w_gate_up[0], w_down=m_w_down[0], g_final=m_g_final[None, :])
    v = dict(g_mix=v_g_mix, w_in=v_w_in[0], conv_w=v_conv_w[0], attn_sinks=v_attn_sinks,
             w_conv_out=v_w_conv_out[0], w_attn_out=v_w_attn_out[0], w_o=v_w_o[0], g_ffn=v_g_ffn,
             w_gate_up=v_w_gate_up[0], w_down=v_w_down[0], g_final=v_g_final[None, :])
    shard = (2 * lax.axis_index("x") + lax.axis_index("y")).astype(jnp.int32)
    core = lax.axis_index("c").astype(jnp.int32)
    shard1, core1, place = shard.reshape((1,)), core.reshape((1,)), jnp.stack([core, shard])
    spec = MATRICES
    xs, target, sinks = x[0], loss_target[0], w["attn_sinks"]
    tables = _rope_tables()

    def gather(names, part=0, parts=1):
        return _Gather([whole[n] for n in names], [(i, spec[n], part, parts) for i, n in enumerate(names)])

    def pair(names):
        return _Pair([dw[n] for n in names], [spec[n] for n in names])

    def pair_sum(tag, names, got):
        return _pair_sum("pair_sum_" + tag, [spec[n] for n in names], [dw[n] for n in names], got, place)

    whole = dict(zip(MATRIX_NAMES, _to_bf16_in_whole(
        [w[n] for n in MATRIX_NAMES], [spec[n] for n in MATRIX_NAMES], shard1)))

    mixers = ("w_conv_out", "w_attn_out", "w_o")
    h1 = _rms_norm("norm_mix", xs, w["g_mix"])
    proj, whole["w_in"], (*got, conv_w_whole) = _mm_in_gather(
        h1, whole["w_in"], _Gather([whole[n] for n in mixers], [(i, spec[n], 0, 1) for i, n in enumerate(mixers)],
                                   conv_w=w["conv_w"]))
    whole.update(zip(mixers, got))
    conv_y = _conv_fwd(proj, conv_w_whole)
    attn, (whole["w_gate_up"],) = _attn_fwd(proj, tables, sinks, comm=gather(("w_gate_up",), 0, 2))
    (conv_out, attn_out, merged), (whole["w_gate_up"],) = _branch_merge(
        conv_y, attn, whole["w_conv_out"], whole["w_attn_out"], proj, comm=gather(("w_gate_up",), 1, 2))
    x2 = _mm_nn("mm_o", merged, whole["w_o"], 1024, 1024, F32, res=xs)
    h2 = _rms_norm("norm_ffn", x2, w["g_ffn"])
    (gate, up, act), (whole["w_down"],) = _gate_up_fwd(h2, whole["w_gate_up"], comm=gather(("w_down",)))
    x3 = _mm_nn("mm_down", act, whole["w_down"], 1024, 512, F32, res=x2)
    dx3, dx3b, dg_final, loss_row = _loss_head(x3, w["g_final"], target)

    dw = {}
    dw["w_down"] = _mm_tn("mm_dw_down", act, dx3b, 1408, 1024, BF16)
    dgate, dup = _dact_swiglu(dx3b, whole["w_down"], gate, up)
    dw["w_gate_up"], got = _mm_dw_gate_up(h2, dgate, dup, comm=pair(("w_down",)))
    sums_a, own_a = pair_sum("down", ("w_down",), got)
    dh2, slots_a = _mm_dh2(dgate, dup, whole["w_gate_up"], comm=_ChipExchange(sums_a, own_a))
    (dx2, dx2b, dg_ffn), got = _rms_norm_bwd("norm_ffn_bwd", dh2, x2, w["g_ffn"], dx3, True, comm=pair(("w_gate_up",)))
    sums_b, own_b = pair_sum("gate_up", ("w_gate_up",), got)
    dw["w_o"] = _mm_tn("mm_dw_o", merged, dx2b, 1024, 1024, BF16)
    dco, dao, dgc, dga = _merge_bwd(dx2b, whole["w_o"], conv_out, attn_out, proj)
    dconv_y = _mm_nt("mm_dconv_y", dco, whole["w_conv_out"], 1024, 1024, D_MODEL, F32)
    dw["w_conv_out"] = _mm_tn("mm_dw_conv_out", conv_y, dco, 1024, 1024, BF16)
    dattn = _mm_nt("mm_dattn", dao, whole["w_attn_out"], 1024, 1024, D_MODEL, BF16)
    dw["w_attn_out"] = _mm_tn("mm_dw_attn_out", attn, dao, 1024, 1024, BF16)
    (dcb, dcc, dcx, dconv_w), got = _conv_bwd(dconv_y, proj, conv_w_whole, comm=pair(mixers))
    sums_c, own_c = pair_sum("mixers", mixers, got)
    (dq, dk_prev, dk_cur, dv_prev, dv_cur, dsinks), slots_b = _attn_bwd(
        proj, dattn, sinks, tables, comm=_ChipExchange(sums_b, own_b))
    dkv = _kv_grad_combine(dk_prev, dk_cur, dv_prev, dv_cur, tables)
    dproj = jnp.concatenate([dcb, dcc, dcx, dq, dkv, dgc, dga], axis=1)
    dw["w_in"], slots_c = _mm_tn("mm_dw_in", h1, dproj, 1024, 1664, BF16, comm=_ChipExchange(sums_c, own_c))
    sums_d, own_d = pair_sum("in", ("w_in",), _comm_call("pair_exchange_in", pair(("w_in",))))
    early = ("w_down", "w_gate_up") + mixers
    halves = _chip_sum("chip_sum_early", [spec[n] for n in early], slots_a + slots_b + slots_c, core1)
    dh1, (own_d, *reduced) = _mm_nt(
        "mm_dh1", dproj, whole["w_in"], 1024, 1024, 1664, F32,
        comm=_Both(_ChipExchange(sums_d, own_d, 0, 2), _HalfExchange(halves, [spec[n] for n in early])))
    g = dict(zip(early, reduced))
    (grad_x, dg_mix), slots_d = _rms_norm_bwd("norm_mix_bwd", dh1, xs, w["g_mix"], dx2, False,
                                              comm=_ChipExchange(sums_d, [own_d], 1, 2))
    small = _pack_small(dg_mix, dg_ffn, dg_final, dconv_w, dsinks, loss_row)
    half_in = _chip_sum("chip_sum_in", [spec["w_in"]], slots_d, core1)
    g["w_in"], small_blocks = _comm_call(
        "half_exchange_in", _Both(_HalfExchange(half_in, [spec["w_in"]]), _SmallAllToAll(small)))
    delta, new_m, new_v = {}, {}, {}

    def keep(names, results):
        for n, (d, nm, nv, grad) in zip(names, results):
            delta[n], new_m[n], new_v[n], g[n] = d, nm, nv, grad

    keep(early, _adamw("adamw_early", [(w[n], g[n], m[n], v[n]) for n in early], 8))
    small_sum = _small_sum(small_blocks)
    g["g_mix"] = small_sum[0:1, :]
    g["g_ffn"] = small_sum[1:2, :]
    g["g_final"] = small_sum[2:3, :]
    g["conv_w"] = lax.dynamic_slice(small_sum, (3, shard * CONV_W_COLS), (3, CONV_W_COLS))
    g["attn_sinks"] = small_sum[6:7, :N_HEADS]
    loss = small_sum[7, 0]
    keep(("w_in",), _adamw("adamw_w_in", [(w["w_in"], g["w_in"], m["w_in"], v["w_in"])], 4))
    rest = ("g_mix", "g_ffn", "g_final", "conv_w", "attn_sinks")
    keep(rest, _adamw("adamw_small", [(w[n], g[n], m[n], v[n]) for n in rest], 1))

    def shaped(vals):
        return [vals[n].reshape((D_MODEL,)) if n == "g_final" else
                (vals[n][None] if n in MATRIX_NAMES or n == "conv_w" else vals[n]) for n in WEIGHT_ORDER]

    return (loss, grad_x[None], *shaped(g), *shaped(delta), *shaped(new_m), *shaped(new_v))
```

```python
import functools
import math

import jax
import jax.numpy as jnp
import numpy as np
from jax import lax
from jax.experimental import pallas as pl
from jax.experimental.pallas import tpu as pltpu

F32 = jnp.float32
BF16 = jnp.bfloat16

D_MODEL = 1024
SEQ = 2048
HEAD_DIM = 64
N_HEADS = 16
N_KV_HEADS = 4
GROUP = N_HEADS // N_KV_HEADS
D_ATTN = N_HEADS * HEAD_DIM
D_KV = N_KV_HEADS * HEAD_DIM
BLOCK = 128
ROT_DIM = HEAD_DIM // 4
ROPE_THETA = 500000.0
ATTN_SCALE = 1.0 / math.sqrt(HEAD_DIM)
NEG_INF = -1e30
D_FF = 2816
EPS = 1e-5
N_IN = 3 * D_MODEL + D_ATTN + 2 * D_KV + 2 * D_MODEL
COL_Q = 3 * D_MODEL
COL_K = COL_Q + D_ATTN
COL_V = COL_K + D_KV
COL_GC = COL_V + D_KV
COL_GA = COL_GC + D_MODEL

ADAM_LR = 0.001
ADAM_B1 = 0.9
ADAM_B2 = 0.999
ADAM_EPS = 1e-08
ADAM_WD = 0.01
ADAM_STEP = 10

N_CHIPS = 4
N_DEV = 8

V7X_VMEM_BYTES = 64 * 1024 * 1024
VMEM_LIMIT = (V7X_VMEM_BYTES * 3) // 4
LANES = 128
MESH = pl.DeviceIdType.MESH


def _params(semantics=None):
    return pltpu.CompilerParams(dimension_semantics=semantics, vmem_limit_bytes=VMEM_LIMIT)


def _sds(shape, dtype):
    return jax.ShapeDtypeStruct(shape, dtype)


HBM_SPEC = pl.BlockSpec(memory_space=pl.ANY)


def _pcall(body, name, grid, in_specs, out_specs, out_shape, operands, scratch=(), semantics=None, comm=None,
           aliases=None, start_after_body=False):
    aliases = dict(aliases or {})
    if comm is None:
        return pl.pallas_call(
            body, name=name, grid=grid, in_specs=in_specs, out_specs=out_specs, out_shape=out_shape,
            scratch_shapes=list(scratch), input_output_aliases=aliases,
            compiler_params=_params(semantics))(*operands)
    multi = isinstance(out_shape, (list, tuple))
    o_specs = list(out_specs) if multi else [out_specs]
    o_shape = list(out_shape) if multi else [out_shape]
    n_in, n_out, n_scr = len(operands), len(o_shape), len(scratch)
    n_cin, n_cout = len(comm.operands), len(comm.out_shape)

    def hosted(*refs):
        ins, cins = refs[:n_in], refs[n_in:n_in + n_cin]
        o0 = n_in + n_cin
        outs, couts = refs[o0:o0 + n_out], refs[o0 + n_out:o0 + n_out + n_cout]
        s0 = o0 + n_out + n_cout
        scr, sems = refs[s0:s0 + n_scr], refs[s0 + n_scr:]
        first = last = None
        for axis, size in enumerate(grid):
            i = pl.program_id(axis)
            first = (i == 0) if first is None else first & (i == 0)
            last = (i == size - 1) if last is None else last & (i == size - 1)

        if not start_after_body:
            @pl.when(first)
            def _():
                comm.start(cins, couts, sems)

        body(*ins, *outs, *scr)

        if start_after_body:
            @pl.when(first)
            def _():
                comm.start(cins, couts, sems)

        @pl.when(last)
        def _():
            comm.finish(cins, couts, sems)

    res = pl.pallas_call(
        hosted, name=name, grid=grid,
        in_specs=list(in_specs) + [HBM_SPEC] * n_cin, out_specs=o_specs + [HBM_SPEC] * n_cout,
        out_shape=o_shape + list(comm.out_shape), scratch_shapes=list(scratch) + list(comm.sems),
        input_output_aliases={**aliases, **{n_in + a: n_out + b for a, b in comm.aliases.items()}},
        compiler_params=_params(("arbitrary",) * len(grid)))(*operands, *comm.operands)
    outs = list(res[:n_out])
    return (outs if multi else outs[0]), list(res[n_out:])


def _comm_call(name, comm):
    def body(*refs):
        n_cin, n_cout = len(comm.operands), len(comm.out_shape)
        cins, couts, sems = refs[:n_cin], refs[n_cin:n_cin + n_cout], refs[n_cin + n_cout:]
        comm.start(cins, couts, sems)
        comm.finish(cins, couts, sems)

    return list(pl.pallas_call(
        body, name=name, in_specs=[HBM_SPEC] * len(comm.operands), out_specs=[HBM_SPEC] * len(comm.out_shape),
        out_shape=list(comm.out_shape), scratch_shapes=list(comm.sems),
        input_output_aliases=dict(comm.aliases))(*comm.operands))


NN = ((1,), (0,))
NT = ((1,), (1,))
TN = ((0,), (0,))


def _matmul(name, a, b, dims, grid, a_spec, b_spec, o_spec, o_shape, o_dtype, res=None, res_spec=None, comm=None):
    nk = grid[2]

    def body(*refs):
        if res is None:
            a_ref, b_ref, o_ref = refs[:3]
            r_ref = None
            scratch = refs[3:]
        else:
            a_ref, b_ref, r_ref, o_ref = refs[:4]
            scratch = refs[4:]
        p = lax.dot_general(a_ref[...], b_ref[...], (dims, ((), ())), preferred_element_type=F32)

        def finish(acc):
            if r_ref is not None:
                acc = r_ref[...] + acc
            o_ref[...] = acc.astype(o_dtype)

        if nk == 1:
            finish(p)
        else:
            acc_ref = scratch[0]
            k = pl.program_id(2)

            @pl.when(k == 0)
            def _():
                acc_ref[...] = p

            @pl.when(k > 0)
            def _():
                acc_ref[...] += p

            @pl.when(k == nk - 1)
            def _():
                finish(acc_ref[...])

    operands = [a, b] if res is None else [a, b, res]
    in_specs = [a_spec, b_spec] if res is None else [a_spec, b_spec, res_spec]
    scratch = [pltpu.VMEM(o_spec.block_shape, F32)] if nk > 1 else []
    return _pcall(body, name, grid, in_specs, o_spec, _sds(o_shape, o_dtype), operands, scratch,
                  ("parallel", "parallel", "arbitrary"), comm)


def _mm_nn(name, a, b, bm, bn, o_dtype, res=None, comm=None):
    m, k = a.shape
    n = b.shape[1]
    return _matmul(
        name, a, b, NN, (m // bm, n // bn, 1),
        pl.BlockSpec((bm, k), lambda i, j, kk: (i, 0)),
        pl.BlockSpec((k, bn), lambda i, j, kk: (0, j)),
        pl.BlockSpec((bm, bn), lambda i, j, kk: (i, j)),
        (m, n), o_dtype, res,
        None if res is None else pl.BlockSpec((bm, bn), lambda i, j, kk: (i, j)), comm,
    )


def _mm_nt(name, a, b, bm, bn, bk, o_dtype, comm=None):
    m, k = a.shape
    n = b.shape[0]
    return _matmul(
        name, a, b, NT, (m // bm, n // bn, k // bk),
        pl.BlockSpec((bm, bk), lambda i, j, kk: (i, kk)),
        pl.BlockSpec((bn, bk), lambda i, j, kk: (j, kk)),
        pl.BlockSpec((bm, bn), lambda i, j, kk: (i, j)),
        (m, n), o_dtype, comm=comm,
    )


def _mm_tn(name, a, b, bm, bn, o_dtype, comm=None):
    k, m = a.shape
    n = b.shape[1]
    return _matmul(
        name, a, b, TN, (m // bm, n // bn, 1),
        pl.BlockSpec((k, bm), lambda i, j, kk: (0, i)),
        pl.BlockSpec((k, bn), lambda i, j, kk: (0, j)),
        pl.BlockSpec((bm, bn), lambda i, j, kk: (i, j)),
        (m, n), o_dtype, comm=comm,
    )


ROWS = 256


def _row_spec(width, col=0):
    return pl.BlockSpec((ROWS, width), lambda i: (i, col))


def _full_spec(shape):
    return pl.BlockSpec(shape, lambda *_: (0,) * len(shape))


def _rms_norm(name, x, g):
    def body(x_ref, g_ref, h_ref):
        xf = x_ref[...]
        r = lax.rsqrt(jnp.mean(xf * xf, axis=-1, keepdims=True) + EPS)
        h_ref[...] = ((xf * r) * g_ref[...]).astype(BF16)

    return pl.pallas_call(
        body, name=name, grid=(SEQ // ROWS,),
        in_specs=[_row_spec(D_MODEL), _full_spec((1, D_MODEL))],
        out_specs=_row_spec(D_MODEL),
        out_shape=_sds((SEQ, D_MODEL), BF16),
        compiler_params=_params(("parallel",)),
    )(x, g)


CONV_COLS = 256


def _shift_rows(u, k):
    rows = lax.broadcasted_iota(jnp.int32, u.shape, 0)
    return jnp.where(rows >= k, pltpu.roll(u, k, axis=0), 0.0)


def _conv_fwd(proj, conv_w):
    nblk = D_MODEL // CONV_COLS

    def body(cb_ref, cc_ref, cx_ref, w_ref, y_ref):
        u = cc_ref[...] * cx_ref[...]
        w = w_ref[...]
        cv = w[0:1, :] * _shift_rows(u, 2) + w[1:2, :] * _shift_rows(u, 1) + w[2:3, :] * u
        y_ref[...] = (cb_ref[...] * cv).astype(BF16)

    def col(part):
        return pl.BlockSpec((SEQ, CONV_COLS), lambda j: (0, part * nblk + j))

    return pl.pallas_call(
        body, name="conv_fwd", grid=(nblk,),
        in_specs=[col(0), col(1), col(2), pl.BlockSpec((3, CONV_COLS), lambda j: (0, j))],
        out_specs=pl.BlockSpec((SEQ, CONV_COLS), lambda j: (0, j)),
        out_shape=_sds((SEQ, D_MODEL), BF16),
        compiler_params=_params(("parallel",)),
    )(proj, proj, proj, conv_w)


ROPE_COLS = 256


def _rope_tables():
    f32 = np.float32
    inv_freq = (f32(ROPE_THETA) ** (-np.arange(0, ROT_DIM, 2, dtype=f32) / f32(ROT_DIM))).astype(f32)
    ang = np.arange(SEQ, dtype=f32)[:, None] * inv_freq[None, :]
    cos, sin = np.cos(ang).astype(f32), np.sin(ang).astype(f32)
    half = ROT_DIM // 2
    ones = np.ones((SEQ, HEAD_DIM - ROT_DIM), f32)
    zeros = np.zeros((SEQ, HEAD_DIM - ROT_DIM), f32)
    zh = np.zeros((SEQ, half), f32)
    c = np.concatenate([cos, cos, ones], axis=1)
    s_up = np.concatenate([-sin, zh, zeros], axis=1)
    s_dn = np.concatenate([zh, sin, zeros], axis=1)
    reps = ROPE_COLS // HEAD_DIM
    return tuple(jnp.asarray(np.tile(t, (1, reps))) for t in (c, s_up, s_dn))


def _rotate(t, c, s_up, s_dn):
    width = t.shape[1]
    half = ROT_DIM // 2
    return t * c + pltpu.roll(t, width - half, axis=1) * s_up + pltpu.roll(t, half, axis=1) * s_dn


N_QBLK = SEQ // BLOCK


def _attn_specs():
    prev = lambda n: jnp.maximum(n - 1, 0)
    q = pl.BlockSpec((BLOCK, D_ATTN), lambda n: (n, COL_Q // D_ATTN))
    k_prev = pl.BlockSpec((BLOCK, D_KV), lambda n: (prev(n), COL_K // D_KV))
    k_cur = pl.BlockSpec((BLOCK, D_KV), lambda n: (n, COL_K // D_KV))
    v_prev = pl.BlockSpec((BLOCK, D_KV), lambda n: (prev(n), COL_V // D_KV))
    v_cur = pl.BlockSpec((BLOCK, D_KV), lambda n: (n, COL_V // D_KV))
    tab_cur = pl.BlockSpec((BLOCK, ROPE_COLS), lambda n: (n, 0))
    tab_prev = pl.BlockSpec((BLOCK, ROPE_COLS), lambda n: (prev(n), 0))
    return [q, k_prev, k_cur, v_prev, v_cur] + [tab_cur] * 3 + [tab_prev] * 3


def _band_kv(kp_ref, kc_ref, vp_ref, vc_ref, tabs_cur, tabs_prev):
    k = jnp.concatenate([_rotate(kp_ref[...], *(t[...] for t in tabs_prev)),
                         _rotate(kc_ref[...], *(t[...] for t in tabs_cur))], axis=0)
    v = jnp.concatenate([vp_ref[...], vc_ref[...]], axis=0)
    return k, v


def _query_tiles(q_ref, tiles, tabs_cur):
    c, su, sd = (t[:, :LANES] for t in tabs_cur)
    return jnp.concatenate(
        [_rotate(q_ref[:, t * LANES:(t + 1) * LANES], c, su, sd).astype(BF16) for t in tiles], axis=0)


def _sink_row(sink_ref, tiles, par):
    return jnp.concatenate([jnp.full((1, BLOCK), sink_ref[0, t * HEADS_PER_TILE + par], F32) for t in tiles], axis=1)


def _band_mask(n):
    kj = lax.broadcasted_iota(jnp.int32, (2 * BLOCK, BLOCK), 0)
    qi = lax.broadcasted_iota(jnp.int32, (2 * BLOCK, BLOCK), 1)
    rel = qi + BLOCK - kj
    return (rel >= 0) & (rel < BLOCK) & ((kj >= BLOCK) | (n > 0))


HEADS_PER_TILE = LANES // HEAD_DIM
TILES_PER_GROUP = GROUP // HEADS_PER_TILE


def _group_mask(n):
    return jnp.concatenate([_band_mask(n)] * TILES_PER_GROUP, axis=1)


def _lane_half(shape, par):
    lane = lax.broadcasted_iota(jnp.int32, shape, 1)
    return (lane < HEAD_DIM) if par == 0 else (lane >= HEAD_DIM)


def _head_tiles(kv, h):
    tile = kv[:, (h // HEADS_PER_TILE) * LANES:(h // HEADS_PER_TILE + 1) * LANES].astype(F32)
    own = jnp.where(_lane_half(tile.shape, h % HEADS_PER_TILE), tile, 0.0)
    other = pltpu.roll(own, HEAD_DIM, axis=1)
    lo, hi = (own, other) if h % HEADS_PER_TILE == 0 else (other, own)
    return lo.astype(BF16), hi.astype(BF16)


def _head_softmax(q_tile, k_half, sink, mask):
    s = lax.dot_general(k_half, q_tile, (NT, ((), ())), preferred_element_type=F32) * ATTN_SCALE
    s = jnp.where(mask, s, NEG_INF)
    m = jnp.maximum(jnp.max(s, axis=0, keepdims=True), sink)
    e = jnp.exp(s - m)
    es = jnp.exp(sink - m)
    inv = 1.0 / (jnp.sum(e, axis=0, keepdims=True) + es)
    return e * inv, es * inv


def _attn_fwd(proj, tables, sinks, comm=None):
    def body(sink_ref, q_ref, kp_ref, kc_ref, vp_ref, vc_ref, c_ref, su_ref, sd_ref, cp_ref, sup_ref, sdp_ref, o_ref):
        n = pl.program_id(0)
        mask = _group_mask(n)
        tabs_cur = (c_ref, su_ref, sd_ref)
        k, v = _band_kv(kp_ref, kc_ref, vp_ref, vc_ref, tabs_cur, (cp_ref, sup_ref, sdp_ref))
        for h in range(N_KV_HEADS):
            k_halves = _head_tiles(k, h)
            v_halves = _head_tiles(v, h)
            tiles = [h * TILES_PER_GROUP + t for t in range(TILES_PER_GROUP)]
            q_rows = _query_tiles(q_ref, tiles, tabs_cur)
            acc = None
            for par in range(HEADS_PER_TILE):
                p, _ = _head_softmax(q_rows, k_halves[par], _sink_row(sink_ref, tiles, par), mask)
                o = lax.dot_general(p.astype(BF16), v_halves[par], (TN, ((), ())), preferred_element_type=F32)
                acc = o if acc is None else acc + o
            for i, tile in enumerate(tiles):
                o_ref[:, tile * LANES:(tile + 1) * LANES] = acc[i * BLOCK:(i + 1) * BLOCK, :].astype(BF16)

    return _pcall(
        body, "attn_fwd", (N_QBLK,),
        [pl.BlockSpec(memory_space=pltpu.SMEM)] + _attn_specs(),
        pl.BlockSpec((BLOCK, D_ATTN), lambda n: (n, 0)),
        _sds((SEQ, D_ATTN), BF16), [sinks] + [proj] * 5 + list(tables) * 2, (), ("parallel",), comm)


def _branch_merge(conv_y, attn, w_co, w_ao, proj, comm=None):
    bm, bn = 1024, 512

    def body(cy_ref, at_ref, wc_ref, wa_ref, gc_ref, ga_ref, co_ref, ao_ref, mg_ref):
        co = jnp.dot(cy_ref[...], wc_ref[...], preferred_element_type=F32)
        ao = jnp.dot(at_ref[...], wa_ref[...], preferred_element_type=F32)
        co_ref[...] = co
        ao_ref[...] = ao
        mg_ref[...] = (jax.nn.sigmoid(gc_ref[...]) * co + jax.nn.sigmoid(ga_ref[...]) * ao).astype(BF16)

    act = pl.BlockSpec((bm, D_MODEL), lambda i, j: (i, 0))
    wgt = pl.BlockSpec((D_MODEL, bn), lambda i, j: (0, j))
    out = pl.BlockSpec((bm, bn), lambda i, j: (i, j))
    return _pcall(
        body, "branch_merge", (SEQ // bm, D_MODEL // bn),
        [act, act, wgt, wgt,
         pl.BlockSpec((bm, bn), lambda i, j: (i, COL_GC // bn + j)),
         pl.BlockSpec((bm, bn), lambda i, j: (i, COL_GA // bn + j))],
        [out, out, out],
        [_sds((SEQ, D_MODEL), F32), _sds((SEQ, D_MODEL), F32), _sds((SEQ, D_MODEL), BF16)],
        [conv_y, attn, w_co, w_ao, proj, proj], (), ("parallel", "parallel"), comm)


FF_BM, FF_BN = 512, 1408
FF_NB = D_FF // FF_BN


def _gate_up_fwd(h2, w_gu, comm=None):
    def body(h_ref, wg_ref, wu_ref, g_ref, u_ref, a_ref):
        h = h_ref[...]
        g = jnp.dot(h, wg_ref[...], preferred_element_type=F32)
        u = jnp.dot(h, wu_ref[...], preferred_element_type=F32)
        g_ref[...] = g
        u_ref[...] = u
        a_ref[...] = (jax.nn.silu(g) * u).astype(BF16)

    out = pl.BlockSpec((FF_BM, FF_BN), lambda i, j: (i, j))
    f32, b16 = _sds((SEQ, D_FF), F32), _sds((SEQ, D_FF), BF16)
    return _pcall(
        body, "mm_gate_up", (SEQ // FF_BM, FF_NB),
        [pl.BlockSpec((FF_BM, D_MODEL), lambda i, j: (i, 0)),
         pl.BlockSpec((D_MODEL, FF_BN), lambda i, j: (0, j)),
         pl.BlockSpec((D_MODEL, FF_BN), lambda i, j: (0, FF_NB + j))],
        [out, out, out], [f32, f32, b16], [h2, w_gu, w_gu], (), ("parallel", "parallel"), comm)


def _dact_swiglu(dx3b, w_down, g, u, comm=None):
    def body(dx_ref, w_ref, g_ref, u_ref, dg_ref, du_ref):
        da = lax.dot_general(dx_ref[...], w_ref[...], (NT, ((), ())), preferred_element_type=F32)
        g = g_ref[...]
        sg = jax.nn.sigmoid(g)
        dg_ref[...] = (da * u_ref[...] * (sg * (1.0 + g * (1.0 - sg)))).astype(BF16)
        du_ref[...] = (da * (g * sg)).astype(BF16)

    blk = pl.BlockSpec((FF_BM, FF_BN), lambda i, j: (i, j))
    b16 = _sds((SEQ, D_FF), BF16)
    return _pcall(
        body, "mm_dact", (SEQ // FF_BM, FF_NB),
        [pl.BlockSpec((FF_BM, D_MODEL), lambda i, j: (i, 0)), pl.BlockSpec((FF_BN, D_MODEL), lambda i, j: (j, 0)),
         blk, blk],
        [blk, blk], [b16, b16], [dx3b, w_down, g, u], (), ("parallel", "parallel"), comm, start_after_body=True)


def _mm_dh2(dg, du, w_gu, comm=None):
    bm = 1024
    nk = 2 * FF_NB

    def body(dg_ref, du_ref, w_ref, o_ref, acc_ref):
        k = pl.program_id(1)

        def part(a_ref):
            return lax.dot_general(a_ref[...], w_ref[...], (NT, ((), ())), preferred_element_type=F32)

        @pl.when(k == 0)
        def _():
            acc_ref[...] = part(dg_ref)

        @pl.when((k > 0) & (k < FF_NB))
        def _():
            acc_ref[...] += part(dg_ref)

        @pl.when(k >= FF_NB)
        def _():
            acc_ref[...] += part(du_ref)

        @pl.when(k == nk - 1)
        def _():
            o_ref[...] = acc_ref[...]

    return _pcall(
        body, "mm_dh2", (SEQ // bm, nk),
        [pl.BlockSpec((bm, FF_BN), lambda i, k: (i, jnp.minimum(k, FF_NB - 1))),
         pl.BlockSpec((bm, FF_BN), lambda i, k: (i, jnp.maximum(k - FF_NB, 0))),
         pl.BlockSpec((D_MODEL, FF_BN), lambda i, k: (0, k))],
        pl.BlockSpec((bm, D_MODEL), lambda i, k: (i, 0)), _sds((SEQ, D_MODEL), F32),
        [dg, du, w_gu], [pltpu.VMEM((bm, D_MODEL), F32)], ("parallel", "arbitrary"), comm)


def _mm_dw_gate_up(h2, dg, du, comm=None):
    def body(h_ref, dg_ref, du_ref, o_ref):
        j = pl.program_id(0)

        def part(b_ref):
            return lax.dot_general(h_ref[...], b_ref[...], (TN, ((), ())), preferred_element_type=F32).astype(BF16)

        @pl.when(j < FF_NB)
        def _():
            o_ref[...] = part(dg_ref)

        @pl.when(j >= FF_NB)
        def _():
            o_ref[...] = part(du_ref)

    return _pcall(
        body, "mm_dw_gate_up", (2 * FF_NB,),
        [_full_spec((SEQ, D_MODEL)),
         pl.BlockSpec((SEQ, FF_BN), lambda j: (0, jnp.minimum(j, FF_NB - 1))),
         pl.BlockSpec((SEQ, FF_BN), lambda j: (0, jnp.maximum(j - FF_NB, 0)))],
        pl.BlockSpec((D_MODEL, FF_BN), lambda j: (0, j)),
        _sds((D_MODEL, 2 * D_FF), BF16), [h2, dg, du], (), ("arbitrary",), comm)


def _loss_head(x3, g, target):
    def body(x_ref, g_ref, t_ref, dx_ref, dxb_ref, dg_ref, loss_ref):
        i = pl.program_id(0)
        xf = x_ref[...]
        r = lax.rsqrt(jnp.mean(xf * xf, axis=-1, keepdims=True) + EPS)
        xn = xf * r
        gg = g_ref[...]
        err = xn * gg - t_ref[...]
        part = 0.5 * jnp.sum(jnp.mean(err * err, axis=-1, keepdims=True), axis=0, keepdims=True)
        dy = err * (1.0 / D_MODEL)
        dxn = dy * gg
        dx = r * (dxn - xn * jnp.mean(dxn * xn, axis=-1, keepdims=True))
        dx_ref[...] = dx
        dxb_ref[...] = dx.astype(BF16)
        dg = jnp.sum(dy * xn, axis=0, keepdims=True)
        lane0 = lax.broadcasted_iota(jnp.int32, (1, LANES), 1) == 0
        lpart = jnp.where(lane0, part, 0.0)

        @pl.when(i == 0)
        def _():
            dg_ref[...] = dg
            loss_ref[...] = lpart

        @pl.when(i > 0)
        def _():
            dg_ref[...] += dg
            loss_ref[...] += lpart

    return pl.pallas_call(
        body, name="loss_head", grid=(SEQ // ROWS,),
        in_specs=[_row_spec(D_MODEL), _full_spec((1, D_MODEL)), _row_spec(D_MODEL)],
        out_specs=[_row_spec(D_MODEL), _row_spec(D_MODEL), _full_spec((1, D_MODEL)), _full_spec((1, LANES))],
        out_shape=[_sds((SEQ, D_MODEL), F32), _sds((SEQ, D_MODEL), BF16),
                   _sds((1, D_MODEL), F32), _sds((1, LANES), F32)],
        compiler_params=_params(("arbitrary",)),
    )(x3, g, target)


def _rms_norm_bwd(name, dh, x, g, dres, with_bf16, comm=None):
    def body(dh_ref, x_ref, g_ref, dr_ref, *outs):
        i = pl.program_id(0)
        dx_ref = outs[0]
        dg_ref = outs[-1]
        xf = x_ref[...]
        r = lax.rsqrt(jnp.mean(xf * xf, axis=-1, keepdims=True) + EPS)
        xn = xf * r
        dh = dh_ref[...]
        dxn = dh * g_ref[...]
        dx = dr_ref[...] + r * (dxn - xn * jnp.mean(dxn * xn, axis=-1, keepdims=True))
        dx_ref[...] = dx
        if with_bf16:
            outs[1][...] = dx.astype(BF16)
        dg = jnp.sum(dh * xn, axis=0, keepdims=True)

        @pl.when(i == 0)
        def _():
            dg_ref[...] = dg

        @pl.when(i > 0)
        def _():
            dg_ref[...] += dg

    row = _row_spec(D_MODEL)
    out_specs = [row] + ([row] if with_bf16 else []) + [_full_spec((1, D_MODEL))]
    out_shape = ([_sds((SEQ, D_MODEL), F32)] + ([_sds((SEQ, D_MODEL), BF16)] if with_bf16 else [])
                 + [_sds((1, D_MODEL), F32)])
    return _pcall(body, name, (SEQ // ROWS,), [row, row, _full_spec((1, D_MODEL)), row], out_specs, out_shape,
                  [dh, x, g, dres], (), ("arbitrary",), comm)


def _merge_bwd(dx2b, w_o, conv_out, attn_out, proj):
    bm, bn = 1024, D_MODEL // 2

    def body(dx_ref, w_ref, co_ref, ao_ref, gc_ref, ga_ref, dco_ref, dao_ref, dgc_ref, dga_ref):
        dm = lax.dot_general(dx_ref[...], w_ref[...], (NT, ((), ())), preferred_element_type=F32)
        sc = jax.nn.sigmoid(gc_ref[...])
        sa = jax.nn.sigmoid(ga_ref[...])
        dco_ref[...] = (dm * sc).astype(BF16)
        dao_ref[...] = (dm * sa).astype(BF16)
        dgc_ref[...] = (dm * co_ref[...] * (sc * (1.0 - sc))).astype(BF16)
        dga_ref[...] = (dm * ao_ref[...] * (sa * (1.0 - sa))).astype(BF16)

    own = pl.BlockSpec((bm, bn), lambda i, j: (i, j))
    sd = _sds((SEQ, D_MODEL), BF16)
    return pl.pallas_call(
        body, name="mm_dmerged", grid=(SEQ // bm, D_MODEL // bn),
        in_specs=[pl.BlockSpec((bm, D_MODEL), lambda i, j: (i, 0)), pl.BlockSpec((bn, D_MODEL), lambda i, j: (j, 0)),
                  own, own,
                  pl.BlockSpec((bm, bn), lambda i, j: (i, COL_GC // bn + j)),
                  pl.BlockSpec((bm, bn), lambda i, j: (i, COL_GA // bn + j))],
        out_specs=[own, own, own, own], out_shape=[sd, sd, sd, sd],
        compiler_params=_params(("parallel", "parallel")),
    )(dx2b, w_o, conv_out, attn_out, proj, proj)


def _conv_bwd(dconv_y, proj, conv_w, comm=None):
    nblk = D_MODEL // CONV_COLS

    def body(dy_ref, cb_ref, cc_ref, cx_ref, w_ref, dcb_ref, dcc_ref, dcx_ref, dw_ref):
        cc = cc_ref[...]
        cx = cx_ref[...]
        u = cc * cx
        w = w_ref[...]
        u1 = _shift_rows(u, 1)
        u2 = _shift_rows(u, 2)
        cv = w[0:1, :] * u2 + w[1:2, :] * u1 + w[2:3, :] * u
        dy = dy_ref[...]
        dcb_ref[...] = (dy * cv).astype(BF16)
        dcv = dy * cb_ref[...]
        rows = lax.broadcasted_iota(jnp.int32, dcv.shape, 0)
        up1 = jnp.where(rows < SEQ - 1, pltpu.roll(dcv, SEQ - 1, axis=0), 0.0)
        up2 = jnp.where(rows < SEQ - 2, pltpu.roll(dcv, SEQ - 2, axis=0), 0.0)
        du = w[2:3, :] * dcv + w[1:2, :] * up1 + w[0:1, :] * up2
        dcc_ref[...] = (du * cx).astype(BF16)
        dcx_ref[...] = (du * cc).astype(BF16)
        dw_ref[...] = jnp.concatenate(
            [jnp.sum(dcv * u2, axis=0, keepdims=True),
             jnp.sum(dcv * u1, axis=0, keepdims=True),
             jnp.sum(dcv * u, axis=0, keepdims=True)], axis=0)

    def col(part):
        return pl.BlockSpec((SEQ, CONV_COLS), lambda j: (0, part * nblk + j))

    own = pl.BlockSpec((SEQ, CONV_COLS), lambda j: (0, j))
    wsp = pl.BlockSpec((3, CONV_COLS), lambda j: (0, j))
    sd = _sds((SEQ, D_MODEL), BF16)
    return _pcall(
        body, "conv_bwd", (nblk,), [own, col(0), col(1), col(2), wsp], [own, own, own, wsp],
        [sd, sd, sd, _sds((3, D_MODEL), F32)], [dconv_y, proj, proj, proj, conv_w], (), ("parallel",), comm)


def _attn_bwd(proj, dattn, sinks, tables, comm=None):
    def body(sink_ref, q_ref, kp_ref, kc_ref, vp_ref, vc_ref, c_ref, su_ref, sd_ref, cp_ref, sup_ref, sdp_ref,
             do_ref, dq_ref, dkp_ref, dkc_ref, dvp_ref, dvc_ref, ds_ref):
        n = pl.program_id(0)
        mask = _group_mask(n)
        tabs_cur = (c_ref, su_ref, sd_ref)
        k, v = _band_kv(kp_ref, kc_ref, vp_ref, vc_ref, tabs_cur, (cp_ref, sup_ref, sdp_ref))
        lane = lax.broadcasted_iota(jnp.int32, (1, LANES), 1)
        dsink = jnp.zeros((1, LANES), F32)
        c, su, sd = c_ref[:, :LANES], su_ref[:, :LANES], sd_ref[:, :LANES]
        dk_tiles = [None] * (N_KV_HEADS // HEADS_PER_TILE)
        dv_tiles = [None] * (N_KV_HEADS // HEADS_PER_TILE)
        for h in range(N_KV_HEADS):
            k_halves = _head_tiles(k, h)
            v_halves = _head_tiles(v, h)
            tiles = [h * TILES_PER_GROUP + t for t in range(TILES_PER_GROUP)]
            q_rows = _query_tiles(q_ref, tiles, tabs_cur)
            do_rows = jnp.concatenate([do_ref[:, t * LANES:(t + 1) * LANES] for t in tiles], axis=0)
            dk_par, dv_par = [], []
            dq_rows = None
            for par in range(HEADS_PER_TILE):
                p, p_sink = _head_softmax(q_rows, k_halves[par], _sink_row(sink_ref, tiles, par), mask)
                dp = lax.dot_general(v_halves[par], do_rows, (NT, ((), ())), preferred_element_type=F32)
                delta = jnp.sum(p * dp, axis=0, keepdims=True)
                ds = (p * (dp - delta) * ATTN_SCALE).astype(BF16)
                dq = lax.dot_general(ds, k_halves[par], (TN, ((), ())), preferred_element_type=F32)
                dq_rows = dq if dq_rows is None else dq_rows + dq
                dk_par.append(jnp.dot(ds, q_rows, preferred_element_type=F32))
                dv_par.append(jnp.dot(p.astype(BF16), do_rows, preferred_element_type=F32))
                sink_grad = p_sink * delta
                for i, tile in enumerate(tiles):
                    val = -jnp.sum(sink_grad[:, i * BLOCK:(i + 1) * BLOCK], axis=1, keepdims=True)
                    dsink = dsink + jnp.where(lane == tile * HEADS_PER_TILE + par, val, 0.0)
            for i, tile in enumerate(tiles):
                dq_tile = dq_rows[i * BLOCK:(i + 1) * BLOCK, :]
                dq_ref[:, tile * LANES:(tile + 1) * LANES] = _rotate(dq_tile, c, -su, -sd).astype(BF16)
            own = h % HEADS_PER_TILE
            for par_grads, tiles in ((dk_par, dk_tiles), (dv_par, dv_tiles)):
                shifted = pltpu.roll(par_grads[1 - own], HEAD_DIM, axis=1)
                total = jnp.where(_lane_half(shifted.shape, own), par_grads[own] + shifted, 0.0)
                i = h // HEADS_PER_TILE
                tiles[i] = total if tiles[i] is None else tiles[i] + total
        for i in range(N_KV_HEADS // HEADS_PER_TILE):
            cols = slice(i * LANES, (i + 1) * LANES)
            dkp_ref[:, cols] = dk_tiles[i][:BLOCK, :]
            dkc_ref[:, cols] = dk_tiles[i][BLOCK:, :]
            dvp_ref[:, cols] = dv_tiles[i][:BLOCK, :]
            dvc_ref[:, cols] = dv_tiles[i][BLOCK:, :]

        @pl.when(n == 0)
        def _():
            ds_ref[...] = dsink

        @pl.when(n > 0)
        def _():
            ds_ref[...] += dsink

    blk = pl.BlockSpec((BLOCK, D_KV), lambda n: (n, 0))
    prev_blk = pl.BlockSpec((BLOCK, D_KV), lambda n: ((n + N_QBLK - 1) % N_QBLK, 0))
    kv = _sds((SEQ, D_KV), F32)
    return _pcall(
        body, "attn_bwd", (N_QBLK,),
        [pl.BlockSpec(memory_space=pltpu.SMEM)] + _attn_specs() + [pl.BlockSpec((BLOCK, D_ATTN), lambda n: (n, 0))],
        [pl.BlockSpec((BLOCK, D_ATTN), lambda n: (n, 0)), prev_blk, blk, prev_blk, blk, _full_spec((1, LANES))],
        [_sds((SEQ, D_ATTN), BF16), kv, kv, kv, kv, _sds((1, LANES), F32)],
        [sinks] + [proj] * 5 + list(tables) * 2 + [dattn], (), ("arbitrary",), comm)


def _kv_grad_combine(dk_prev, dk_cur, dv_prev, dv_cur, tables):
    rows = 4 * BLOCK

    def body(kp_ref, kc_ref, vp_ref, vc_ref, c_ref, su_ref, sd_ref, o_ref):
        dk = kc_ref[...] + kp_ref[...]
        dv = vc_ref[...] + vp_ref[...]
        o_ref[:, :D_KV] = _rotate(dk, c_ref[...], -su_ref[...], -sd_ref[...]).astype(BF16)
        o_ref[:, D_KV:] = dv.astype(BF16)

    blk = pl.BlockSpec((rows, D_KV), lambda m: (m, 0))
    return pl.pallas_call(
        body, name="kv_grad_combine", grid=(SEQ // rows,),
        in_specs=[blk] * 7,
        out_specs=pl.BlockSpec((rows, 2 * D_KV), lambda m: (m, 0)),
        out_shape=_sds((SEQ, 2 * D_KV), BF16),
        compiler_params=_params(("parallel",)),
    )(dk_prev, dk_cur, dv_prev, dv_cur, *tables)


MATRICES = {
    "w_in": (D_MODEL, N_IN // N_CHIPS, "col"),
    "w_conv_out": (D_MODEL // N_CHIPS, D_MODEL, "row"),
    "w_attn_out": (D_MODEL // N_CHIPS, D_MODEL, "row"),
    "w_o": (D_MODEL // N_CHIPS, D_MODEL, "row"),
    "w_gate_up": (D_MODEL, 2 * D_FF // N_CHIPS, "col"),
    "w_down": (D_FF // N_CHIPS, D_MODEL, "row"),
}
BF16_ROW_TILE = 16
CONV_W_COLS = D_MODEL // N_CHIPS
SMALL_ROWS = 8


def _whole_shape(spec):
    rows, cols, kind = spec
    return (rows, cols * N_CHIPS) if kind == "col" else (rows * N_CHIPS, cols)


def _half_shape(spec):
    return (spec[0] // 2, spec[1])


def _aligned(start, multiple):
    return start if isinstance(start, int) else pl.multiple_of(start, multiple)


def _region(ref, spec, shard, half, part=0, parts=1):
    rows, cols, kind = spec
    hr = rows // 2
    n = hr // parts
    if kind == "col":
        return ref.at[pl.ds(_aligned(half * hr + part * n, BF16_ROW_TILE), n),
                      pl.ds(_aligned(shard * cols, LANES), cols)]
    return ref.at[pl.ds(_aligned(shard * rows + half * hr + part * n, BF16_ROW_TILE), n), :]


def _position():
    x, y, c = lax.axis_index("x"), lax.axis_index("y"), lax.axis_index("c")
    chips = [(1 - x, y), (x, 1 - y), (1 - x, 1 - y)]
    return x, y, c, chips


def _shard_of(chip):
    return 2 * chip[0] + chip[1]


def _remote(src, dst, send_sem, recv_sem, to):
    return pltpu.make_async_remote_copy(src_ref=src, dst_ref=dst, send_sem=send_sem, recv_sem=recv_sem,
                                        device_id=to, device_id_type=MESH)


CAST_STEPS = 4


def _to_bf16_in_whole(ws, specs, shard):
    n = len(ws)

    def body(s_ref, *refs):
        del s_ref
        for w_ref, o_ref in zip(refs[:n], refs[n:]):
            o_ref[...] = w_ref[...].astype(BF16)

    def out_spec(spec):
        rows = spec[0] // CAST_STEPS
        if spec[2] == "col":
            return pl.BlockSpec((rows, spec[1]), lambda i, s_ref: (i, s_ref[0]))
        return pl.BlockSpec((rows, spec[1]), lambda i, s_ref: (s_ref[0] * CAST_STEPS + i, 0))

    grid_spec = pltpu.PrefetchScalarGridSpec(
        num_scalar_prefetch=1, grid=(CAST_STEPS,),
        in_specs=[pl.BlockSpec((s[0] // CAST_STEPS, s[1]), lambda i, s_ref: (i, 0)) for s in specs],
        out_specs=[out_spec(s) for s in specs])
    return list(pl.pallas_call(
        body, name="cast_shards", grid_spec=grid_spec, out_shape=[_sds(_whole_shape(s), BF16) for s in specs],
        compiler_params=_params(("parallel",)),
    )(shard, *ws))


class _Gather:
    def __init__(self, wholes, pieces, conv_w=None):
        self.pieces = pieces
        self.n = len(wholes)
        self.with_conv_w = conv_w is not None
        self.operands = list(wholes) + ([conv_w] if self.with_conv_w else [])
        self.out_shape = [_sds(w.shape, w.dtype) for w in wholes]
        if self.with_conv_w:
            self.out_shape.append(_sds((3, D_MODEL), F32))
        self.aliases = {i: i for i in range(self.n)}
        n_ici = 3 * len(pieces)
        self.sems = [pltpu.SemaphoreType.DMA((n_ici,))] * 4
        if self.with_conv_w:
            self.sems += [pltpu.SemaphoreType.DMA((1,)), pltpu.SemaphoreType.DMA((3,)), pltpu.SemaphoreType.DMA((3,))]

    def _conv_w(self, cins, couts, sems, with_recvs):
        cw_in, cw_out = cins[self.n], couts[self.n]
        x, y, c, chips = _position()

        def cols(shard):
            return cw_out.at[:, pl.ds(_aligned(shard * CONV_W_COLS, LANES), CONV_W_COLS)]

        me = _shard_of((x, y))
        local = pltpu.make_async_copy(cw_in, cols(me), sems[4].at[0])
        sends = [_remote(cw_in, cols(me), sems[5].at[j], sems[6].at[j], (*chip, c)) for j, chip in enumerate(chips)]
        if not with_recvs:
            return local, sends, []
        recvs = [_remote(cols(_shard_of(chip)), cols(_shard_of(chip)), sems[5].at[j], sems[6].at[j], (*chip, c))
                 for j, chip in enumerate(chips)]
        return local, sends, recvs

    def start(self, cins, couts, sems):
        x, y, c, chips = _position()
        me = _shard_of((x, y))
        if self.with_conv_w:
            local, sends, _ = self._conv_w(cins, couts, sems, False)
            local.start()
            for cp in sends:
                cp.start()
        for p, (i, spec, part, parts) in enumerate(self.pieces):
            mine = _region(couts[i], spec, me, c, part, parts)
            for j, chip in enumerate(chips):
                _remote(mine, mine, sems[0].at[3 * p + j], sems[1].at[3 * p + j], (*chip, c)).start()

    def finish(self, cins, couts, sems):
        x, y, c, chips = _position()
        me = _shard_of((x, y))
        sibling = (x, y, 1 - c)
        send_a, recv_a, send_b, recv_b = sems[:4]
        passed = []
        for p, (i, spec, part, parts) in enumerate(self.pieces):
            for j, chip in enumerate(chips):
                k = 3 * p + j
                landed = _region(couts[i], spec, _shard_of(chip), c, part, parts)
                _remote(landed, landed, send_a.at[k], recv_a.at[k], (*chip, c)).wait_recv()
                cp = _remote(landed, landed, send_b.at[k], recv_b.at[k], sibling)
                cp.start()
                passed.append(cp)
        for p, (i, spec, part, parts) in enumerate(self.pieces):
            mine = _region(couts[i], spec, me, c, part, parts)
            for j, chip in enumerate(chips):
                k = 3 * p + j
                other = _region(couts[i], spec, _shard_of(chip), 1 - c, part, parts)
                _remote(other, other, send_b.at[k], recv_b.at[k], sibling).wait_recv()
                _remote(mine, mine, send_a.at[k], recv_a.at[k], (*chip, c)).wait_send()
        for cp in passed:
            cp.wait_send()
        if self.with_conv_w:
            local, sends, recvs = self._conv_w(cins, couts, sems, True)
            for cp in recvs:
                cp.wait_recv()
            for cp in sends:
                cp.wait_send()
            local.wait()


def _mm_in_gather(h1, w_whole, comm):
    spec = MATRICES["w_in"]
    cols = spec[1]
    bm = SEQ // 2

    def body(h_ref, w_in_ref, proj_ref, w_ref, wbuf, obuf, send_a, recv_a, send_b, recv_b, load_sem, store_sems):
        del w_in_ref
        s, mi = pl.program_id(0), pl.program_id(1)
        x, y, c, chips = _position()
        me = _shard_of((x, y))
        sibling = (x, y, 1 - c)
        mine = _region(w_ref, spec, me, c)

        @pl.when((s == 0) & (mi == 0))
        def _():
            for j, chip in enumerate(chips):
                _remote(mine, mine, send_a.at[j], recv_a.at[j], (*chip, c)).start()

        shard = me
        for j, chip in enumerate(chips):
            shard = jnp.where(s == j + 1, _shard_of(chip), shard)

            @pl.when((s == j + 1) & (mi == 0))
            def _():
                landed = _region(w_ref, spec, _shard_of(chip), c)
                _remote(landed, landed, send_a.at[j], recv_a.at[j], (*chip, c)).wait_recv()
                _remote(landed, landed, send_b.at[j], recv_b.at[j], sibling).start()
                other = _region(w_ref, spec, _shard_of(chip), 1 - c)
                _remote(other, other, send_b.at[j], recv_b.at[j], sibling).wait_recv()

        col0 = pl.multiple_of(shard * cols, LANES)

        @pl.when(mi == 0)
        def _():
            load = pltpu.make_async_copy(w_ref.at[:, pl.ds(col0, cols)], wbuf, load_sem.at[0])
            load.start()
            load.wait()

        def store():
            rows = pl.ds(pl.multiple_of(mi * bm, bm), bm)
            return pltpu.make_async_copy(obuf.at[mi], proj_ref.at[rows, pl.ds(col0, cols)], store_sems.at[mi])

        @pl.when(s > 0)
        def _():
            store().wait()

        obuf[mi] = jnp.dot(h_ref[...], wbuf[...], preferred_element_type=F32)
        store().start()

        @pl.when(s == N_CHIPS - 1)
        def _():
            store().wait()

        @pl.when((s == N_CHIPS - 1) & (mi == 1))
        def _():
            for j, chip in enumerate(chips):
                landed = _region(w_ref, spec, _shard_of(chip), c)
                _remote(mine, mine, send_a.at[j], recv_a.at[j], (*chip, c)).wait_send()
                _remote(landed, landed, send_b.at[j], recv_b.at[j], sibling).wait_send()

    sem3 = pltpu.SemaphoreType.DMA((3,))
    (proj, whole), extra = _pcall(
        body, "mm_in", (N_CHIPS, SEQ // bm),
        [pl.BlockSpec((bm, D_MODEL), lambda s, m: (m, 0)), HBM_SPEC], [HBM_SPEC, HBM_SPEC],
        [_sds((SEQ, N_IN), F32), _sds(w_whole.shape, w_whole.dtype)], [h1, w_whole],
        [pltpu.VMEM((D_MODEL, cols), BF16), pltpu.VMEM((SEQ // bm, bm, cols), F32), sem3, sem3, sem3, sem3,
         pltpu.SemaphoreType.DMA((1,)), pltpu.SemaphoreType.DMA((SEQ // bm,))],
        None, comm, aliases={1: 1}, start_after_body=True)
    return proj, whole, extra


def _mm_dw_in_pair(h1, dproj, comm):
    spec = MATRICES["w_in"]
    rows, cols, _ = spec
    hr = rows // 2

    def body(h_ref, dp_ref, dw_ref, got_ref, obuf, store_sems, send_sems, recv_sems):
        t = pl.program_id(0)
        x, y, c, _ = _position()
        sibling = (x, y, 1 - c)

        def store(step):
            return pltpu.make_async_copy(obuf.at[step % 2], dw_ref.at[:, pl.ds(step * cols, cols)],
                                         store_sems.at[step % 2])

        def send(step):
            theirs = obuf.at[step % 2, pl.ds(_aligned((1 - c) * hr, BF16_ROW_TILE), hr), :]
            return _remote(theirs, got_ref.at[step], send_sems.at[step], recv_sems.at[step], sibling)

        for step in range(N_CHIPS):
            @pl.when(t == step)
            def _():
                if step >= 2:
                    store(step - 2).wait()
                    send(step - 2).wait_send()
                obuf[step % 2] = lax.dot_general(
                    h_ref[...], dp_ref[...], (TN, ((), ())), preferred_element_type=F32).astype(BF16)
                store(step).start()
                send(step).start()

        @pl.when(t == N_CHIPS - 1)
        def _():
            for step in (N_CHIPS - 2, N_CHIPS - 1):
                store(step).wait()
                send(step).wait_send()
            for step in range(N_CHIPS):
                send(step).wait_recv()

    sem4 = pltpu.SemaphoreType.DMA((N_CHIPS,))
    (dw_in, got), extra = _pcall(
        body, "mm_dw_in", (N_CHIPS,),
        [_full_spec((SEQ, D_MODEL)), pl.BlockSpec((SEQ, cols), lambda t: (0, t))], [HBM_SPEC, HBM_SPEC],
        [_sds(_whole_shape(spec), BF16), _sds((N_CHIPS, hr, cols), BF16)], [h1, dproj],
        [pltpu.VMEM((2, rows, cols), BF16), pltpu.SemaphoreType.DMA((2,)), sem4, sem4], None, comm)
    return dw_in, got, extra


def _pack_small(dg_mix, dg_ffn, dg_final, dconv_w, dsinks, loss_row):
    def body(a_ref, b_ref, c_ref, w_ref, s_ref, l_ref, o_ref):
        pad = jnp.zeros((1, D_MODEL - LANES), F32)
        o_ref[0:1, :] = a_ref[...]
        o_ref[1:2, :] = b_ref[...]
        o_ref[2:3, :] = c_ref[...]
        o_ref[3:6, :] = w_ref[...]
        o_ref[6:7, :] = jnp.concatenate([s_ref[...], pad], axis=1)
        o_ref[7:8, :] = jnp.concatenate([l_ref[...], pad], axis=1)

    return pl.pallas_call(
        body, name="pack_small", out_shape=_sds((SMALL_ROWS, D_MODEL), F32),
        compiler_params=_params(),
    )(dg_mix, dg_ffn, dg_final, dconv_w, dsinks, loss_row)


class _Pair:
    def __init__(self, dws, specs):
        self.specs = specs
        self.operands = list(dws)
        self.out_shape = [_sds((N_CHIPS, *_half_shape(s)), BF16) for s in specs]
        self.aliases = {}
        n = N_CHIPS * len(specs)
        self.sems = [pltpu.SemaphoreType.DMA((n,)), pltpu.SemaphoreType.DMA((n,))]

    def _copies(self, cins, couts, sems):
        x, y, c, _ = _position()
        sibling = (x, y, 1 - c)
        for i, spec in enumerate(self.specs):
            for t in range(N_CHIPS):
                k = N_CHIPS * i + t
                yield _remote(_region(cins[i], spec, t, 1 - c), couts[i].at[t], sems[0].at[k], sems[1].at[k], sibling)

    def start(self, cins, couts, sems):
        for cp in self._copies(cins, couts, sems):
            cp.start()

    def finish(self, cins, couts, sems):
        for cp in self._copies(cins, couts, sems):
            cp.wait()


class _SmallAllToAll:
    def __init__(self, small):
        self.operands = [small]
        self.out_shape = [_sds((N_DEV, SMALL_ROWS, D_MODEL), F32)]
        self.aliases = {}
        self.sems = [pltpu.SemaphoreType.DMA((N_DEV - 1,)), pltpu.SemaphoreType.DMA((N_DEV - 1,)),
                     pltpu.SemaphoreType.DMA((1,))]

    def _copies(self, cins, couts, sems):
        x, y, c, _ = _position()
        me = 4 * x + 2 * y + c
        out = []
        for r in range(1, N_DEV):
            flip = ((r >> 2) & 1, (r >> 1) & 1, r & 1)
            peer = tuple(1 - p if f else p for p, f in zip((x, y, c), flip))
            theirs = couts[0].at[4 * peer[0] + 2 * peer[1] + peer[2]]
            out.append((_remote(cins[0], couts[0].at[me], sems[0].at[r - 1], sems[1].at[r - 1], peer),
                        functools.partial(_remote, theirs, theirs, sems[0].at[r - 1], sems[1].at[r - 1], peer)))
        return pltpu.make_async_copy(cins[0], couts[0].at[me], sems[2].at[0]), out

    def start(self, cins, couts, sems):
        own, copies = self._copies(cins, couts, sems)
        own.start()
        for send, _ in copies:
            send.start()

    def finish(self, cins, couts, sems):
        own, copies = self._copies(cins, couts, sems)
        for send, recv in copies:
            recv().wait_recv()
            send.wait_send()
        own.wait()


class _Both:
    def __init__(self, a, b):
        self.a, self.b = a, b
        self.operands = list(a.operands) + list(b.operands)
        self.out_shape = list(a.out_shape) + list(b.out_shape)
        self.aliases = dict(a.aliases)
        self.aliases.update({len(a.operands) + k: len(a.out_shape) + v for k, v in b.aliases.items()})
        self.sems = list(a.sems) + list(b.sems)

    def _split(self, cins, couts, sems):
        na, ma, sa = len(self.a.operands), len(self.a.out_shape), len(self.a.sems)
        return (cins[:na], couts[:ma], sems[:sa]), (cins[na:], couts[ma:], sems[sa:])

    def start(self, cins, couts, sems):
        for plan, args in zip((self.a, self.b), self._split(cins, couts, sems)):
            plan.start(*args)

    def finish(self, cins, couts, sems):
        for plan, args in zip((self.a, self.b), self._split(cins, couts, sems)):
            plan.finish(*args)


def _pair_sum(name, specs, dws, got, place):
    n_mat = len(specs)

    def body(p_ref, *refs):
        t = pl.program_id(0)
        mine, theirs = refs[:n_mat], refs[n_mat:2 * n_mat]
        outs, owns = refs[2 * n_mat:3 * n_mat], refs[3 * n_mat:]
        for a, b, o, own in zip(mine, theirs, outs, owns):
            s = (a[...].astype(F32) + b[...].astype(F32)).astype(BF16)
            o[...] = s

            @pl.when(t == p_ref[1])
            def _():
                own[...] = s

    def mine_spec(spec):
        hr, cols = _half_shape(spec)
        if spec[2] == "col":
            return pl.BlockSpec((hr, cols), lambda t, p_ref: (p_ref[0], t))
        return pl.BlockSpec((hr, cols), lambda t, p_ref: (2 * t + p_ref[0], 0))

    def slot_spec(spec):
        return pl.BlockSpec((None, *_half_shape(spec)), lambda t, p_ref: (t, 0, 0))

    def own_spec(spec):
        return pl.BlockSpec((None, *_half_shape(spec)), lambda t, p_ref: (p_ref[1], 0, 0))

    slots = [_sds((N_CHIPS, *_half_shape(s)), BF16) for s in specs]
    grid_spec = pltpu.PrefetchScalarGridSpec(
        num_scalar_prefetch=1, grid=(N_CHIPS,),
        in_specs=[mine_spec(s) for s in specs] + [slot_spec(s) for s in specs],
        out_specs=[slot_spec(s) for s in specs] + [own_spec(s) for s in specs])
    res = pl.pallas_call(
        body, name=name, grid_spec=grid_spec, out_shape=slots + slots,
        compiler_params=_params(("arbitrary",)),
    )(place, *dws, *got)
    return list(res[:n_mat]), list(res[n_mat:])


class _ChipExchange:
    def __init__(self, sums, slots, part=0, parts=1):
        self.n = len(sums)
        self.part, self.parts = part, parts
        self.operands = list(sums) + list(slots)
        self.out_shape = [_sds(s.shape, s.dtype) for s in slots]
        self.aliases = {self.n + i: i for i in range(self.n)}
        self.sems = [pltpu.SemaphoreType.DMA((3 * self.n,)), pltpu.SemaphoreType.DMA((3 * self.n,))]

    def _rows(self, ref, slot):
        n = ref.shape[1] // self.parts
        return ref.at[slot, pl.ds(self.part * n, n), :]

    def _copies(self, cins, couts, sems):
        x, y, c, chips = _position()
        me = _shard_of((x, y))
        for i in range(self.n):
            for j, chip in enumerate(chips):
                k = 3 * i + j
                theirs = self._rows(couts[i], _shard_of(chip))
                yield (_remote(self._rows(cins[i], _shard_of(chip)), self._rows(couts[i], me),
                               sems[0].at[k], sems[1].at[k], (*chip, c)),
                       functools.partial(_remote, theirs, theirs, sems[0].at[k], sems[1].at[k], (*chip, c)))

    def start(self, cins, couts, sems):
        for send, _ in self._copies(cins, couts, sems):
            send.start()

    def finish(self, cins, couts, sems):
        for send, recv in self._copies(cins, couts, sems):
            recv().wait_recv()
            send.wait_send()


def _chip_sum(name, specs, slots, core):
    steps = 2
    n_mat = len(specs)

    def body(c_ref, *refs):
        del c_ref
        ins, outs = refs[:n_mat], refs[n_mat:]
        for a, o in zip(ins, outs):
            acc = a[0].astype(F32)
            for t in range(1, N_CHIPS):
                acc = acc + a[t].astype(F32)
            o[...] = acc

    def in_spec(spec):
        hr, cols = _half_shape(spec)
        return pl.BlockSpec((N_CHIPS, hr // steps, cols), lambda i, c_ref: (0, i, 0))

    def out_spec(spec):
        hr, cols = _half_shape(spec)
        return pl.BlockSpec((hr // steps, cols), lambda i, c_ref: (c_ref[0] * steps + i, 0))

    grid_spec = pltpu.PrefetchScalarGridSpec(
        num_scalar_prefetch=1, grid=(steps,),
        in_specs=[in_spec(s) for s in specs], out_specs=[out_spec(s) for s in specs])
    return list(pl.pallas_call(
        body, name=name, grid_spec=grid_spec,
        out_shape=[_sds((s[0], s[1]), F32) for s in specs],
        compiler_params=_params(("parallel",)),
    )(core, *slots))


class _HalfExchange:
    def __init__(self, grads, specs):
        self.specs = specs
        self.operands = list(grads)
        self.out_shape = [_sds(g.shape, g.dtype) for g in grads]
        self.aliases = {i: i for i in range(len(grads))}
        self.sems = [pltpu.SemaphoreType.DMA((len(grads),)), pltpu.SemaphoreType.DMA((len(grads),))]

    def _copies(self, couts, sems):
        x, y, c, _ = _position()
        sibling = (x, y, 1 - c)
        for i, spec in enumerate(self.specs):
            hr = spec[0] // 2
            mine = couts[i].at[pl.ds(_aligned(c * hr, 8), hr), :]
            theirs = couts[i].at[pl.ds(_aligned((1 - c) * hr, 8), hr), :]
            yield (_remote(mine, mine, sems[0].at[i], sems[1].at[i], sibling),
                   functools.partial(_remote, theirs, theirs, sems[0].at[i], sems[1].at[i], sibling))

    def start(self, cins, couts, sems):
        for send, _ in self._copies(couts, sems):
            send.start()

    def finish(self, cins, couts, sems):
        for send, recv in self._copies(couts, sems):
            recv().wait_recv()
            send.wait_send()


def _small_sum(blocks):
    def body(b_ref, o_ref):
        acc = b_ref[0]
        for d in range(1, N_DEV):
            acc = acc + b_ref[d]
        o_ref[...] = acc

    return pl.pallas_call(
        body, name="small_sum", out_shape=_sds((SMALL_ROWS, D_MODEL), F32), compiler_params=_params(),
    )(blocks)


def _adamw(name, params, steps):
    n = len(params)

    def body(*refs):
        for p in range(n):
            w_ref, g_ref, m_ref, v_ref = refs[4 * p:4 * p + 4]
            d_ref, nm_ref, nv_ref, go_ref = refs[4 * n + 4 * p:4 * n + 4 * p + 4]
            g = g_ref[...]
            go_ref[...] = g
            m = ADAM_B1 * m_ref[...] + (1.0 - ADAM_B1) * g
            v = ADAM_B2 * v_ref[...] + (1.0 - ADAM_B2) * jnp.square(g)
            m_hat = m / (1.0 - ADAM_B1 ** ADAM_STEP)
            v_hat = v / (1.0 - ADAM_B2 ** ADAM_STEP)
            d_ref[...] = -ADAM_LR * (m_hat / (jnp.sqrt(v_hat) + ADAM_EPS) + ADAM_WD * w_ref[...])
            nm_ref[...] = m
            nv_ref[...] = v

    in_specs, out_specs, out_shape, operands = [], [], [], []
    for w, g, m, v in params:
        spec = pl.BlockSpec((w.shape[0] // steps, w.shape[1]), lambda i: (i, 0))
        in_specs += [spec] * 4
        out_specs += [spec] * 4
        out_shape += [_sds(w.shape, F32)] * 4
        operands += [w, g, m, v]
    outs = _pcall(body, name, (steps,), in_specs, out_specs, out_shape, operands, (), ("parallel",))
    return [tuple(outs[4 * p:4 * p + 4]) for p in range(n)]


MATRIX_NAMES = tuple(MATRICES)
WEIGHT_ORDER = ("g_mix", "w_in", "conv_w", "attn_sinks", "w_conv_out", "w_attn_out", "w_o", "g_ffn",
                "w_gate_up", "w_down", "g_final")


def kernel(x, g_mix, w_in, conv_w, attn_sinks, w_conv_out, w_attn_out, w_o, g_ffn, w_gate_up, w_down, g_final, loss_target, m_g_mix, m_w_in, m_conv_w, m_attn_sinks, m_w_conv_out, m_w_attn_out, m_w_o, m_g_ffn, m_w_gate_up, m_w_down, m_g_final, v_g_mix, v_w_in, v_conv_w, v_attn_sinks, v_w_conv_out, v_w_attn_out, v_w_o, v_g_ffn, v_w_gate_up, v_w_down, v_g_final):
    w = dict(g_mix=g_mix, w_in=w_in[0], conv_w=conv_w[0], attn_sinks=attn_sinks, w_conv_out=w_conv_out[0],
             w_attn_out=w_attn_out[0], w_o=w_o[0], g_ffn=g_ffn, w_gate_up=w_gate_up[0], w_down=w_down[0],
             g_final=g_final[None, :])
    m = dict(g_mix=m_g_mix, w_in=m_w_in[0], conv_w=m_conv_w[0], attn_sinks=m_attn_sinks,
             w_conv_out=m_w_conv_out[0], w_attn_out=m_w_attn_out[0], w_o=m_w_o[0], g_ffn=m_g_ffn,
             w_gate_up=m_w_gate_up[0], w_down=m_w_down[0], g_final=m_g_final[None, :])
    v = dict(g_mix=v_g_mix, w_in=v_w_in[0], conv_w=v_conv_w[0], attn_sinks=v_attn_sinks,
             w_conv_out=v_w_conv_out[0], w_attn_out=v_w_attn_out[0], w_o=v_w_o[0], g_ffn=v_g_ffn,
             w_gate_up=v_w_gate_up[0], w_down=v_w_down[0], g_final=v_g_final[None, :])
    shard = (2 * lax.axis_index("x") + lax.axis_index("y")).astype(jnp.int32)
    core = lax.axis_index("c").astype(jnp.int32)
    shard1, core1, place = shard.reshape((1,)), core.reshape((1,)), jnp.stack([core, shard])
    spec = MATRICES
    xs, target, sinks = x[0], loss_target[0], w["attn_sinks"]
    tables = _rope_tables()

    def gather(names, part=0, parts=1):
        return _Gather([whole[n] for n in names], [(i, spec[n], part, parts) for i, n in enumerate(names)])

    def pair(names):
        return _Pair([dw[n] for n in names], [spec[n] for n in names])

    def pair_sum(tag, names, got):
        return _pair_sum("pair_sum_" + tag, [spec[n] for n in names], [dw[n] for n in names], got, place)

    whole = dict(zip(MATRIX_NAMES, _to_bf16_in_whole(
        [w[n] for n in MATRIX_NAMES], [spec[n] for n in MATRIX_NAMES], shard1)))

    mixers = ("w_conv_out", "w_attn_out", "w_o")
    h1 = _rms_norm("norm_mix", xs, w["g_mix"])
    proj, whole["w_in"], (*got, conv_w_whole) = _mm_in_gather(
        h1, whole["w_in"], _Gather([whole[n] for n in mixers], [(i, spec[n], 0, 1) for i, n in enumerate(mixers)],
                                   conv_w=w["conv_w"]))
    whole.update(zip(mixers, got))
    conv_y = _conv_fwd(proj, conv_w_whole)
    attn, (whole["w_gate_up"],) = _attn_fwd(proj, tables, sinks, comm=gather(("w_gate_up",), 0, 2))
    (conv_out, attn_out, merged), (whole["w_gate_up"],) = _branch_merge(
        conv_y, attn, whole["w_conv_out"], whole["w_attn_out"], proj, comm=gather(("w_gate_up",), 1, 2))
    x2 = _mm_nn("mm_o", merged, whole["w_o"], 1024, 1024, F32, res=xs)
    h2 = _rms_norm("norm_ffn", x2, w["g_ffn"])
    (gate, up, act), (whole["w_down"],) = _gate_up_fwd(h2, whole["w_gate_up"], comm=gather(("w_down",)))
    x3 = _mm_nn("mm_down", act, whole["w_down"], 1024, 512, F32, res=x2)
    dx3, dx3b, dg_final, loss_row = _loss_head(x3, w["g_final"], target)

    dw = {}
    dw["w_down"] = _mm_tn("mm_dw_down", act, dx3b, 1408, 1024, BF16)
    dgate, dup = _dact_swiglu(dx3b, whole["w_down"], gate, up)
    dw["w_gate_up"], got = _mm_dw_gate_up(h2, dgate, dup, comm=pair(("w_down",)))
    sums_a, own_a = pair_sum("down", ("w_down",), got)
    dh2, slots_a = _mm_dh2(dgate, dup, whole["w_gate_up"], comm=_ChipExchange(sums_a, own_a))
    (dx2, dx2b, dg_ffn), got = _rms_norm_bwd("norm_ffn_bwd", dh2, x2, w["g_ffn"], dx3, True, comm=pair(("w_gate_up",)))
    sums_b, own_b = pair_sum("gate_up", ("w_gate_up",), got)
    dw["w_o"] = _mm_tn("mm_dw_o", merged, dx2b, 1024, 1024, BF16)
    dco, dao, dgc, dga = _merge_bwd(dx2b, whole["w_o"], conv_out, attn_out, proj)
    dconv_y = _mm_nt("mm_dconv_y", dco, whole["w_conv_out"], 1024, 1024, D_MODEL, F32)
    dw["w_conv_out"] = _mm_tn("mm_dw_conv_out", conv_y, dco, 1024, 1024, BF16)
    dattn = _mm_nt("mm_dattn", dao, whole["w_attn_out"], 1024, 1024, D_MODEL, BF16)
    dw["w_attn_out"] = _mm_tn("mm_dw_attn_out", attn, dao, 1024, 1024, BF16)
    (dcb, dcc, dcx, dconv_w), got = _conv_bwd(dconv_y, proj, conv_w_whole, comm=pair(mixers))
    sums_c, own_c = pair_sum("mixers", mixers, got)
    (dq, dk_prev, dk_cur, dv_prev, dv_cur, dsinks), slots_b = _attn_bwd(
        proj, dattn, sinks, tables, comm=_ChipExchange(sums_b, own_b))
    dkv = _kv_grad_combine(dk_prev, dk_cur, dv_prev, dv_cur, tables)
    dproj = jnp.concatenate([dcb, dcc, dcx, dq, dkv, dgc, dga], axis=1)
    dw["w_in"], got, slots_c = _mm_dw_in_pair(h1, dproj, _ChipExchange(sums_c, own_c))
    sums_d, own_d = pair_sum("in", ("w_in",), [got])
    early = ("w_down", "w_gate_up") + mixers
    halves = _chip_sum("chip_sum_early", [spec[n] for n in early], slots_a + slots_b + slots_c, core1)
    dh1, (own_d, *reduced) = _mm_nt(
        "mm_dh1", dproj, whole["w_in"], 1024, 1024, 1664, F32,
        comm=_Both(_ChipExchange(sums_d, own_d, 0, 2), _HalfExchange(halves, [spec[n] for n in early])))
    g = dict(zip(early, reduced))
    (grad_x, dg_mix), slots_d = _rms_norm_bwd("norm_mix_bwd", dh1, xs, w["g_mix"], dx2, False,
                                              comm=_ChipExchange(sums_d, [own_d], 1, 2))
    small = _pack_small(dg_mix, dg_ffn, dg_final, dconv_w, dsinks, loss_row)
    half_in = _chip_sum("chip_sum_in", [spec["w_in"]], slots_d, core1)
    g["w_in"], small_blocks = _comm_call(
        "half_exchange_in", _Both(_HalfExchange(half_in, [spec["w_in"]]), _SmallAllToAll(small)))
    delta, new_m, new_v = {}, {}, {}

    def keep(names, results):
        for n, (d, nm, nv, grad) in zip(names, results):
            delta[n], new_m[n], new_v[n], g[n] = d, nm, nv, grad

    keep(early, _adamw("adamw_early", [(w[n], g[n], m[n], v[n]) for n in early], 8))
    small_sum = _small_sum(small_blocks)
    g["g_mix"] = small_sum[0:1, :]
    g["g_ffn"] = small_sum[1:2, :]
    g["g_final"] = small_sum[2:3, :]
    g["conv_w"] = lax.dynamic_slice(small_sum, (3, shard * CONV_W_COLS), (3, CONV_W_COLS))
    g["attn_sinks"] = small_sum[6:7, :N_HEADS]
    loss = small_sum[7, 0]
    keep(("w_in",), _adamw("adamw_w_in", [(w["w_in"], g["w_in"], m["w_in"], v["w_in"])], 4))
    rest = ("g_mix", "g_ffn", "g_final", "conv_w", "attn_sinks")
    keep(rest, _adamw("adamw_small", [(w[n], g[n], m[n], v[n]) for n in rest], 1))

    def shaped(vals):
        return [vals[n].reshape((D_MODEL,)) if n == "g_final" else
                (vals[n][None] if n in MATRIX_NAMES or n == "conv_w" else vals[n]) for n in WEIGHT_ORDER]

    return (loss, grad_x[None], *shaped(g), *shaped(delta), *shaped(new_m), *shaped(new_v))
```

```python
import functools
import math

import jax
import jax.numpy as jnp
import numpy as np
from jax import lax
from jax.experimental import pallas as pl
from jax.experimental.pallas import tpu as pltpu

F32 = jnp.float32
BF16 = jnp.bfloat16

D_MODEL = 1024
SEQ = 2048
HEAD_DIM = 64
N_HEADS = 16
N_KV_HEADS = 4
GROUP = N_HEADS // N_KV_HEADS
D_ATTN = N_HEADS * HEAD_DIM
D_KV = N_KV_HEADS * HEAD_DIM
BLOCK = 128
ROT_DIM = HEAD_DIM // 4
ROPE_THETA = 500000.0
ATTN_SCALE = 1.0 / math.sqrt(HEAD_DIM)
NEG_INF = -1e30
D_FF = 2816
EPS = 1e-5
N_IN = 3 * D_MODEL + D_ATTN + 2 * D_KV + 2 * D_MODEL
COL_Q = 3 * D_MODEL
COL_K = COL_Q + D_ATTN
COL_V = COL_K + D_KV
COL_GC = COL_V + D_KV
COL_GA = COL_GC + D_MODEL

ADAM_LR = 0.001
ADAM_B1 = 0.9
ADAM_B2 = 0.999
ADAM_EPS = 1e-08
ADAM_WD = 0.01
ADAM_STEP = 10

N_CHIPS = 4
N_DEV = 8

V7X_VMEM_BYTES = 64 * 1024 * 1024
VMEM_LIMIT = (V7X_VMEM_BYTES * 3) // 4
LANES = 128
MESH = pl.DeviceIdType.MESH


def _params(semantics=None):
    return pltpu.CompilerParams(dimension_semantics=semantics, vmem_limit_bytes=VMEM_LIMIT)


def _sds(shape, dtype):
    return jax.ShapeDtypeStruct(shape, dtype)


HBM_SPEC = pl.BlockSpec(memory_space=pl.ANY)


def _pcall(body, name, grid, in_specs, out_specs, out_shape, operands, scratch=(), semantics=None, comm=None,
           aliases=None, start_after_body=False):
    aliases = dict(aliases or {})
    if comm is None:
        return pl.pallas_call(
            body, name=name, grid=grid, in_specs=in_specs, out_specs=out_specs, out_shape=out_shape,
            scratch_shapes=list(scratch), input_output_aliases=aliases,
            compiler_params=_params(semantics))(*operands)
    multi = isinstance(out_shape, (list, tuple))
    o_specs = list(out_specs) if multi else [out_specs]
    o_shape = list(out_shape) if multi else [out_shape]
    n_in, n_out, n_scr = len(operands), len(o_shape), len(scratch)
    n_cin, n_cout = len(comm.operands), len(comm.out_shape)

    def hosted(*refs):
        ins, cins = refs[:n_in], refs[n_in:n_in + n_cin]
        o0 = n_in + n_cin
        outs, couts = refs[o0:o0 + n_out], refs[o0 + n_out:o0 + n_out + n_cout]
        s0 = o0 + n_out + n_cout
        scr, sems = refs[s0:s0 + n_scr], refs[s0 + n_scr:]
        first = last = None
        for axis, size in enumerate(grid):
            i = pl.program_id(axis)
            first = (i == 0) if first is None else first & (i == 0)
            last = (i == size - 1) if last is None else last & (i == size - 1)

        if not start_after_body:
            @pl.when(first)
            def _():
                comm.start(cins, couts, sems)

        body(*ins, *outs, *scr)

        if start_after_body:
            @pl.when(first)
            def _():
                comm.start(cins, couts, sems)

        @pl.when(last)
        def _():
            comm.finish(cins, couts, sems)

    res = pl.pallas_call(
        hosted, name=name, grid=grid,
        in_specs=list(in_specs) + [HBM_SPEC] * n_cin, out_specs=o_specs + [HBM_SPEC] * n_cout,
        out_shape=o_shape + list(comm.out_shape), scratch_shapes=list(scratch) + list(comm.sems),
        input_output_aliases={**aliases, **{n_in + a: n_out + b for a, b in comm.aliases.items()}},
        compiler_params=_params(("arbitrary",) * len(grid)))(*operands, *comm.operands)
    outs = list(res[:n_out])
    return (outs if multi else outs[0]), list(res[n_out:])


def _comm_call(name, comm):
    def body(*refs):
        n_cin, n_cout = len(comm.operands), len(comm.out_shape)
        cins, couts, sems = refs[:n_cin], refs[n_cin:n_cin + n_cout], refs[n_cin + n_cout:]
        comm.start(cins, couts, sems)
        comm.finish(cins, couts, sems)

    return list(pl.pallas_call(
        body, name=name, in_specs=[HBM_SPEC] * len(comm.operands), out_specs=[HBM_SPEC] * len(comm.out_shape),
        out_shape=list(comm.out_shape), scratch_shapes=list(comm.sems),
        input_output_aliases=dict(comm.aliases))(*comm.operands))


NN = ((1,), (0,))
NT = ((1,), (1,))
TN = ((0,), (0,))


def _matmul(name, a, b, dims, grid, a_spec, b_spec, o_spec, o_shape, o_dtype, res=None, res_spec=None, comm=None):
    nk = grid[2]

    def body(*refs):
        if res is None:
            a_ref, b_ref, o_ref = refs[:3]
            r_ref = None
            scratch = refs[3:]
        else:
            a_ref, b_ref, r_ref, o_ref = refs[:4]
            scratch = refs[4:]
        p = lax.dot_general(a_ref[...], b_ref[...], (dims, ((), ())), preferred_element_type=F32)

        def finish(acc):
            if r_ref is not None:
                acc = r_ref[...] + acc
            o_ref[...] = acc.astype(o_dtype)

        if nk == 1:
            finish(p)
        else:
            acc_ref = scratch[0]
            k = pl.program_id(2)

            @pl.when(k == 0)
            def _():
                acc_ref[...] = p

            @pl.when(k > 0)
            def _():
                acc_ref[...] += p

            @pl.when(k == nk - 1)
            def _():
                finish(acc_ref[...])

    operands = [a, b] if res is None else [a, b, res]
    in_specs = [a_spec, b_spec] if res is None else [a_spec, b_spec, res_spec]
    scratch = [pltpu.VMEM(o_spec.block_shape, F32)] if nk > 1 else []
    return _pcall(body, name, grid, in_specs, o_spec, _sds(o_shape, o_dtype), operands, scratch,
                  ("parallel", "parallel", "arbitrary"), comm)


def _mm_nn(name, a, b, bm, bn, o_dtype, res=None, comm=None):
    m, k = a.shape
    n = b.shape[1]
    return _matmul(
        name, a, b, NN, (m // bm, n // bn, 1),
        pl.BlockSpec((bm, k), lambda i, j, kk: (i, 0)),
        pl.BlockSpec((k, bn), lambda i, j, kk: (0, j)),
        pl.BlockSpec((bm, bn), lambda i, j, kk: (i, j)),
        (m, n), o_dtype, res,
        None if res is None else pl.BlockSpec((bm, bn), lambda i, j, kk: (i, j)), comm,
    )


def _mm_nt(name, a, b, bm, bn, bk, o_dtype, comm=None):
    m, k = a.shape
    n = b.shape[0]
    return _matmul(
        name, a, b, NT, (m // bm, n // bn, k // bk),
        pl.BlockSpec((bm, bk), lambda i, j, kk: (i, kk)),
        pl.BlockSpec((bn, bk), lambda i, j, kk: (j, kk)),
        pl.BlockSpec((bm, bn), lambda i, j, kk: (i, j)),
        (m, n), o_dtype, comm=comm,
    )


def _mm_tn(name, a, b, bm, bn, o_dtype, comm=None):
    k, m = a.shape
    n = b.shape[1]
    return _matmul(
        name, a, b, TN, (m // bm, n // bn, 1),
        pl.BlockSpec((k, bm), lambda i, j, kk: (0, i)),
        pl.BlockSpec((k, bn), lambda i, j, kk: (0, j)),
        pl.BlockSpec((bm, bn), lambda i, j, kk: (i, j)),
        (m, n), o_dtype, comm=comm,
    )


ROWS = 256


def _row_spec(width, col=0):
    return pl.BlockSpec((ROWS, width), lambda i: (i, col))


def _full_spec(shape):
    return pl.BlockSpec(shape, lambda *_: (0,) * len(shape))


def _rms_norm(name, x, g):
    def body(x_ref, g_ref, h_ref):
        xf = x_ref[...]
        r = lax.rsqrt(jnp.mean(xf * xf, axis=-1, keepdims=True) + EPS)
        h_ref[...] = ((xf * r) * g_ref[...]).astype(BF16)

    return pl.pallas_call(
        body, name=name, grid=(SEQ // ROWS,),
        in_specs=[_row_spec(D_MODEL), _full_spec((1, D_MODEL))],
        out_specs=_row_spec(D_MODEL),
        out_shape=_sds((SEQ, D_MODEL), BF16),
        compiler_params=_params(("parallel",)),
    )(x, g)


CONV_COLS = 256


def _shift_rows(u, k):
    rows = lax.broadcasted_iota(jnp.int32, u.shape, 0)
    return jnp.where(rows >= k, pltpu.roll(u, k, axis=0), 0.0)


def _conv_fwd(proj, conv_w):
    nblk = D_MODEL // CONV_COLS

    def body(cb_ref, cc_ref, cx_ref, w_ref, y_ref):
        u = cc_ref[...] * cx_ref[...]
        w = w_ref[...]
        cv = w[0:1, :] * _shift_rows(u, 2) + w[1:2, :] * _shift_rows(u, 1) + w[2:3, :] * u
        y_ref[...] = (cb_ref[...] * cv).astype(BF16)

    def col(part):
        return pl.BlockSpec((SEQ, CONV_COLS), lambda j: (0, part * nblk + j))

    return pl.pallas_call(
        body, name="conv_fwd", grid=(nblk,),
        in_specs=[col(0), col(1), col(2), pl.BlockSpec((3, CONV_COLS), lambda j: (0, j))],
        out_specs=pl.BlockSpec((SEQ, CONV_COLS), lambda j: (0, j)),
        out_shape=_sds((SEQ, D_MODEL), BF16),
        compiler_params=_params(("parallel",)),
    )(proj, proj, proj, conv_w)


ROPE_COLS = 256


def _rope_tables():
    f32 = np.float32
    inv_freq = (f32(ROPE_THETA) ** (-np.arange(0, ROT_DIM, 2, dtype=f32) / f32(ROT_DIM))).astype(f32)
    ang = np.arange(SEQ, dtype=f32)[:, None] * inv_freq[None, :]
    cos, sin = np.cos(ang).astype(f32), np.sin(ang).astype(f32)
    half = ROT_DIM // 2
    ones = np.ones((SEQ, HEAD_DIM - ROT_DIM), f32)
    zeros = np.zeros((SEQ, HEAD_DIM - ROT_DIM), f32)
    zh = np.zeros((SEQ, half), f32)
    c = np.concatenate([cos, cos, ones], axis=1)
    s_up = np.concatenate([-sin, zh, zeros], axis=1)
    s_dn = np.concatenate([zh, sin, zeros], axis=1)
    reps = ROPE_COLS // HEAD_DIM
    return tuple(jnp.asarray(np.tile(t, (1, reps))) for t in (c, s_up, s_dn))


def _rotate(t, c, s_up, s_dn):
    width = t.shape[1]
    half = ROT_DIM // 2
    return t * c + pltpu.roll(t, width - half, axis=1) * s_up + pltpu.roll(t, half, axis=1) * s_dn


N_QBLK = SEQ // BLOCK


def _attn_specs():
    prev = lambda n: jnp.maximum(n - 1, 0)
    q = pl.BlockSpec((BLOCK, D_ATTN), lambda n: (n, COL_Q // D_ATTN))
    k_prev = pl.BlockSpec((BLOCK, D_KV), lambda n: (prev(n), COL_K // D_KV))
    k_cur = pl.BlockSpec((BLOCK, D_KV), lambda n: (n, COL_K // D_KV))
    v_prev = pl.BlockSpec((BLOCK, D_KV), lambda n: (prev(n), COL_V // D_KV))
    v_cur = pl.BlockSpec((BLOCK, D_KV), lambda n: (n, COL_V // D_KV))
    tab_cur = pl.BlockSpec((BLOCK, ROPE_COLS), lambda n: (n, 0))
    tab_prev = pl.BlockSpec((BLOCK, ROPE_COLS), lambda n: (prev(n), 0))
    return [q, k_prev, k_cur, v_prev, v_cur] + [tab_cur] * 3 + [tab_prev] * 3


def _band_kv(kp_ref, kc_ref, vp_ref, vc_ref, tabs_cur, tabs_prev):
    k = jnp.concatenate([_rotate(kp_ref[...], *(t[...] for t in tabs_prev)),
                         _rotate(kc_ref[...], *(t[...] for t in tabs_cur))], axis=0)
    v = jnp.concatenate([vp_ref[...], vc_ref[...]], axis=0)
    return k, v


def _query_tiles(q_ref, tiles, tabs_cur):
    c, su, sd = (t[:, :LANES] for t in tabs_cur)
    return jnp.concatenate(
        [_rotate(q_ref[:, t * LANES:(t + 1) * LANES], c, su, sd).astype(BF16) for t in tiles], axis=0)


def _sink_row(sink_ref, tiles, par):
    return jnp.concatenate([jnp.full((1, BLOCK), sink_ref[0, t * HEADS_PER_TILE + par], F32) for t in tiles], axis=1)


def _band_mask(n):
    kj = lax.broadcasted_iota(jnp.int32, (2 * BLOCK, BLOCK), 0)
    qi = lax.broadcasted_iota(jnp.int32, (2 * BLOCK, BLOCK), 1)
    rel = qi + BLOCK - kj
    return (rel >= 0) & (rel < BLOCK) & ((kj >= BLOCK) | (n > 0))


HEADS_PER_TILE = LANES // HEAD_DIM
TILES_PER_GROUP = GROUP // HEADS_PER_TILE


def _group_mask(n):
    return jnp.concatenate([_band_mask(n)] * TILES_PER_GROUP, axis=1)


def _lane_half(shape, par):
    lane = lax.broadcasted_iota(jnp.int32, shape, 1)
    return (lane < HEAD_DIM) if par == 0 else (lane >= HEAD_DIM)


def _head_tiles(kv, h):
    tile = kv[:, (h // HEADS_PER_TILE) * LANES:(h // HEADS_PER_TILE + 1) * LANES].astype(F32)
    own = jnp.where(_lane_half(tile.shape, h % HEADS_PER_TILE), tile, 0.0)
    other = pltpu.roll(own, HEAD_DIM, axis=1)
    lo, hi = (own, other) if h % HEADS_PER_TILE == 0 else (other, own)
    return lo.astype(BF16), hi.astype(BF16)


def _head_softmax(q_tile, k_half, sink, mask):
    s = lax.dot_general(k_half, q_tile, (NT, ((), ())), preferred_element_type=F32) * ATTN_SCALE
    s = jnp.where(mask, s, NEG_INF)
    m = jnp.maximum(jnp.max(s, axis=0, keepdims=True), sink)
    e = jnp.exp(s - m)
    es = jnp.exp(sink - m)
    inv = 1.0 / (jnp.sum(e, axis=0, keepdims=True) + es)
    return e * inv, es * inv


def _attn_fwd(proj, tables, sinks, comm=None):
    def body(sink_ref, q_ref, kp_ref, kc_ref, vp_ref, vc_ref, c_ref, su_ref, sd_ref, cp_ref, sup_ref, sdp_ref, o_ref):
        n = pl.program_id(0)
        mask = _group_mask(n)
        tabs_cur = (c_ref, su_ref, sd_ref)
        k, v = _band_kv(kp_ref, kc_ref, vp_ref, vc_ref, tabs_cur, (cp_ref, sup_ref, sdp_ref))
        for h in range(N_KV_HEADS):
            k_halves = _head_tiles(k, h)
            v_halves = _head_tiles(v, h)
            tiles = [h * TILES_PER_GROUP + t for t in range(TILES_PER_GROUP)]
            q_rows = _query_tiles(q_ref, tiles, tabs_cur)
            acc = None
            for par in range(HEADS_PER_TILE):
                p, _ = _head_softmax(q_rows, k_halves[par], _sink_row(sink_ref, tiles, par), mask)
                o = lax.dot_general(p.astype(BF16), v_halves[par], (TN, ((), ())), preferred_element_type=F32)
                acc = o if acc is None else acc + o
            for i, tile in enumerate(tiles):
                o_ref[:, tile * LANES:(tile + 1) * LANES] = acc[i * BLOCK:(i + 1) * BLOCK, :].astype(BF16)

    return _pcall(
        body, "attn_fwd", (N_QBLK,),
        [pl.BlockSpec(memory_space=pltpu.SMEM)] + _attn_specs(),
        pl.BlockSpec((BLOCK, D_ATTN), lambda n: (n, 0)),
        _sds((SEQ, D_ATTN), BF16), [sinks] + [proj] * 5 + list(tables) * 2, (), ("parallel",), comm)


def _branch_merge(conv_y, attn, w_co, w_ao, proj, comm=None):
    bm, bn = 1024, 512

    def body(cy_ref, at_ref, wc_ref, wa_ref, gc_ref, ga_ref, co_ref, ao_ref, mg_ref):
        co = jnp.dot(cy_ref[...], wc_ref[...], preferred_element_type=F32)
        ao = jnp.dot(at_ref[...], wa_ref[...], preferred_element_type=F32)
        co_ref[...] = co
        ao_ref[...] = ao
        mg_ref[...] = (jax.nn.sigmoid(gc_ref[...]) * co + jax.nn.sigmoid(ga_ref[...]) * ao).astype(BF16)

    act = pl.BlockSpec((bm, D_MODEL), lambda i, j: (i, 0))
    wgt = pl.BlockSpec((D_MODEL, bn), lambda i, j: (0, j))
    out = pl.BlockSpec((bm, bn), lambda i, j: (i, j))
    return _pcall(
        body, "branch_merge", (SEQ // bm, D_MODEL // bn),
        [act, act, wgt, wgt,
         pl.BlockSpec((bm, bn), lambda i, j: (i, COL_GC // bn + j)),
         pl.BlockSpec((bm, bn), lambda i, j: (i, COL_GA // bn + j))],
        [out, out, out],
        [_sds((SEQ, D_MODEL), F32), _sds((SEQ, D_MODEL), F32), _sds((SEQ, D_MODEL), BF16)],
        [conv_y, attn, w_co, w_ao, proj, proj], (), ("parallel", "parallel"), comm)


FF_BM, FF_BN = 512, 1408
FF_NB = D_FF // FF_BN


def _gate_up_fwd(h2, w_gu, comm=None):
    def body(h_ref, wg_ref, wu_ref, g_ref, u_ref, a_ref):
        h = h_ref[...]
        g = jnp.dot(h, wg_ref[...], preferred_element_type=F32)
        u = jnp.dot(h, wu_ref[...], preferred_element_type=F32)
        g_ref[...] = g
        u_ref[...] = u
        a_ref[...] = (jax.nn.silu(g) * u).astype(BF16)

    out = pl.BlockSpec((FF_BM, FF_BN), lambda i, j: (i, j))
    f32, b16 = _sds((SEQ, D_FF), F32), _sds((SEQ, D_FF), BF16)
    return _pcall(
        body, "mm_gate_up", (SEQ // FF_BM, FF_NB),
        [pl.BlockSpec((FF_BM, D_MODEL), lambda i, j: (i, 0)),
         pl.BlockSpec((D_MODEL, FF_BN), lambda i, j: (0, j)),
         pl.BlockSpec((D_MODEL, FF_BN), lambda i, j: (0, FF_NB + j))],
        [out, out, out], [f32, f32, b16], [h2, w_gu, w_gu], (), ("parallel", "parallel"), comm)


def _dact_swiglu(dx3b, w_down, g, u):
    def body(dx_ref, w_ref, g_ref, u_ref, dg_ref, du_ref):
        da = lax.dot_general(dx_ref[...], w_ref[...], (NT, ((), ())), preferred_element_type=F32)
        g = g_ref[...]
        sg = jax.nn.sigmoid(g)
        dg_ref[...] = (da * u_ref[...] * (sg * (1.0 + g * (1.0 - sg)))).astype(BF16)
        du_ref[...] = (da * (g * sg)).astype(BF16)

    blk = pl.BlockSpec((FF_BM, FF_BN), lambda i, j: (i, j))
    b16 = _sds((SEQ, D_FF), BF16)
    return _pcall(
        body, "mm_dact", (SEQ // FF_BM, FF_NB),
        [pl.BlockSpec((FF_BM, D_MODEL), lambda i, j: (i, 0)), pl.BlockSpec((FF_BN, D_MODEL), lambda i, j: (j, 0)),
         blk, blk],
        [blk, blk], [b16, b16], [dx3b, w_down, g, u], (), ("parallel", "parallel"))


def _mm_dh2(dg, du, w_gu, comm=None):
    bm = 1024
    nk = 2 * FF_NB

    def body(dg_ref, du_ref, w_ref, o_ref, acc_ref):
        k = pl.program_id(1)

        def part(a_ref):
            return lax.dot_general(a_ref[...], w_ref[...], (NT, ((), ())), preferred_element_type=F32)

        @pl.when(k == 0)
        def _():
            acc_ref[...] = part(dg_ref)

        @pl.when((k > 0) & (k < FF_NB))
        def _():
            acc_ref[...] += part(dg_ref)

        @pl.when(k >= FF_NB)
        def _():
            acc_ref[...] += part(du_ref)

        @pl.when(k == nk - 1)
        def _():
            o_ref[...] = acc_ref[...]

    return _pcall(
        body, "mm_dh2", (SEQ // bm, nk),
        [pl.BlockSpec((bm, FF_BN), lambda i, k: (i, jnp.minimum(k, FF_NB - 1))),
         pl.BlockSpec((bm, FF_BN), lambda i, k: (i, jnp.maximum(k - FF_NB, 0))),
         pl.BlockSpec((D_MODEL, FF_BN), lambda i, k: (0, k))],
        pl.BlockSpec((bm, D_MODEL), lambda i, k: (i, 0)), _sds((SEQ, D_MODEL), F32),
        [dg, du, w_gu], [pltpu.VMEM((bm, D_MODEL), F32)], ("parallel", "arbitrary"), comm)


def _mm_dw_gate_up(h2, dg, du, comm=None):
    def body(h_ref, dg_ref, du_ref, o_ref):
        j = pl.program_id(0)

        def part(b_ref):
            return lax.dot_general(h_ref[...], b_ref[...], (TN, ((), ())), preferred_element_type=F32).astype(BF16)

        @pl.when(j < FF_NB)
        def _():
            o_ref[...] = part(dg_ref)

        @pl.when(j >= FF_NB)
        def _():
            o_ref[...] = part(du_ref)

    return _pcall(
        body, "mm_dw_gate_up", (2 * FF_NB,),
        [_full_spec((SEQ, D_MODEL)),
         pl.BlockSpec((SEQ, FF_BN), lambda j: (0, jnp.minimum(j, FF_NB - 1))),
         pl.BlockSpec((SEQ, FF_BN), lambda j: (0, jnp.maximum(j - FF_NB, 0)))],
        pl.BlockSpec((D_MODEL, FF_BN), lambda j: (0, j)),
        _sds((D_MODEL, 2 * D_FF), BF16), [h2, dg, du], (), ("arbitrary",), comm)


def _loss_head(x3, g, target):
    def body(x_ref, g_ref, t_ref, dx_ref, dxb_ref, dg_ref, loss_ref):
        i = pl.program_id(0)
        xf = x_ref[...]
        r = lax.rsqrt(jnp.mean(xf * xf, axis=-1, keepdims=True) + EPS)
        xn = xf * r
        gg = g_ref[...]
        err = xn * gg - t_ref[...]
        part = 0.5 * jnp.sum(jnp.mean(err * err, axis=-1, keepdims=True), axis=0, keepdims=True)
        dy = err * (1.0 / D_MODEL)
        dxn = dy * gg
        dx = r * (dxn - xn * jnp.mean(dxn * xn, axis=-1, keepdims=True))
        dx_ref[...] = dx
        dxb_ref[...] = dx.astype(BF16)
        dg = jnp.sum(dy * xn, axis=0, keepdims=True)
        lane0 = lax.broadcasted_iota(jnp.int32, (1, LANES), 1) == 0
        lpart = jnp.where(lane0, part, 0.0)

        @pl.when(i == 0)
        def _():
            dg_ref[...] = dg
            loss_ref[...] = lpart

        @pl.when(i > 0)
        def _():
            dg_ref[...] += dg
            loss_ref[...] += lpart

    return pl.pallas_call(
        body, name="loss_head", grid=(SEQ // ROWS,),
        in_specs=[_row_spec(D_MODEL), _full_spec((1, D_MODEL)), _row_spec(D_MODEL)],
        out_specs=[_row_spec(D_MODEL), _row_spec(D_MODEL), _full_spec((1, D_MODEL)), _full_spec((1, LANES))],
        out_shape=[_sds((SEQ, D_MODEL), F32), _sds((SEQ, D_MODEL), BF16),
                   _sds((1, D_MODEL), F32), _sds((1, LANES), F32)],
        compiler_params=_params(("arbitrary",)),
    )(x3, g, target)


def _rms_norm_bwd(name, dh, x, g, dres, with_bf16, comm=None):
    def body(dh_ref, x_ref, g_ref, dr_ref, *outs):
        i = pl.program_id(0)
        dx_ref = outs[0]
        dg_ref = outs[-1]
        xf = x_ref[...]
        r = lax.rsqrt(jnp.mean(xf * xf, axis=-1, keepdims=True) + EPS)
        xn = xf * r
        dh = dh_ref[...]
        dxn = dh * g_ref[...]
        dx = dr_ref[...] + r * (dxn - xn * jnp.mean(dxn * xn, axis=-1, keepdims=True))
        dx_ref[...] = dx
        if with_bf16:
            outs[1][...] = dx.astype(BF16)
        dg = jnp.sum(dh * xn, axis=0, keepdims=True)

        @pl.when(i == 0)
        def _():
            dg_ref[...] = dg

        @pl.when(i > 0)
        def _():
            dg_ref[...] += dg

    row = _row_spec(D_MODEL)
    out_specs = [row] + ([row] if with_bf16 else []) + [_full_spec((1, D_MODEL))]
    out_shape = ([_sds((SEQ, D_MODEL), F32)] + ([_sds((SEQ, D_MODEL), BF16)] if with_bf16 else [])
                 + [_sds((1, D_MODEL), F32)])
    return _pcall(body, name, (SEQ // ROWS,), [row, row, _full_spec((1, D_MODEL)), row], out_specs, out_shape,
                  [dh, x, g, dres], (), ("arbitrary",), comm)


def _merge_bwd(dx2b, w_o, conv_out, attn_out, proj):
    bm, bn = 1024, D_MODEL // 2

    def body(dx_ref, w_ref, co_ref, ao_ref, gc_ref, ga_ref, dco_ref, dao_ref, dgc_ref, dga_ref):
        dm = lax.dot_general(dx_ref[...], w_ref[...], (NT, ((), ())), preferred_element_type=F32)
        sc = jax.nn.sigmoid(gc_ref[...])
        sa = jax.nn.sigmoid(ga_ref[...])
        dco_ref[...] = (dm * sc).astype(BF16)
        dao_ref[...] = (dm * sa).astype(BF16)
        dgc_ref[...] = (dm * co_ref[...] * (sc * (1.0 - sc))).astype(BF16)
        dga_ref[...] = (dm * ao_ref[...] * (sa * (1.0 - sa))).astype(BF16)

    own = pl.BlockSpec((bm, bn), lambda i, j: (i, j))
    sd = _sds((SEQ, D_MODEL), BF16)
    return pl.pallas_call(
        body, name="mm_dmerged", grid=(SEQ // bm, D_MODEL // bn),
        in_specs=[pl.BlockSpec((bm, D_MODEL), lambda i, j: (i, 0)), pl.BlockSpec((bn, D_MODEL), lambda i, j: (j, 0)),
                  own, own,
                  pl.BlockSpec((bm, bn), lambda i, j: (i, COL_GC // bn + j)),
                  pl.BlockSpec((bm, bn), lambda i, j: (i, COL_GA // bn + j))],
        out_specs=[own, own, own, own], out_shape=[sd, sd, sd, sd],
        compiler_params=_params(("parallel", "parallel")),
    )(dx2b, w_o, conv_out, attn_out, proj, proj)


def _conv_bwd(dconv_y, proj, conv_w, comm=None):
    nblk = D_MODEL // CONV_COLS

    def body(dy_ref, cb_ref, cc_ref, cx_ref, w_ref, dcb_ref, dcc_ref, dcx_ref, dw_ref):
        cc = cc_ref[...]
        cx = cx_ref[...]
        u = cc * cx
        w = w_ref[...]
        u1 = _shift_rows(u, 1)
        u2 = _shift_rows(u, 2)
        cv = w[0:1, :] * u2 + w[1:2, :] * u1 + w[2:3, :] * u
        dy = dy_ref[...]
        dcb_ref[...] = (dy * cv).astype(BF16)
        dcv = dy * cb_ref[...]
        rows = lax.broadcasted_iota(jnp.int32, dcv.shape, 0)
        up1 = jnp.where(rows < SEQ - 1, pltpu.roll(dcv, SEQ - 1, axis=0), 0.0)
        up2 = jnp.where(rows < SEQ - 2, pltpu.roll(dcv, SEQ - 2, axis=0), 0.0)
        du = w[2:3, :] * dcv + w[1:2, :] * up1 + w[0:1, :] * up2
        dcc_ref[...] = (du * cx).astype(BF16)
        dcx_ref[...] = (du * cc).astype(BF16)
        dw_ref[...] = jnp.concatenate(
            [jnp.sum(dcv * u2, axis=0, keepdims=True),
             jnp.sum(dcv * u1, axis=0, keepdims=True),
             jnp.sum(dcv * u, axis=0, keepdims=True)], axis=0)

    def col(part):
        return pl.BlockSpec((SEQ, CONV_COLS), lambda j: (0, part * nblk + j))

    own = pl.BlockSpec((SEQ, CONV_COLS), lambda j: (0, j))
    wsp = pl.BlockSpec((3, CONV_COLS), lambda j: (0, j))
    sd = _sds((SEQ, D_MODEL), BF16)
    return _pcall(
        body, "conv_bwd", (nblk,), [own, col(0), col(1), col(2), wsp], [own, own, own, wsp],
        [sd, sd, sd, _sds((3, D_MODEL), F32)], [dconv_y, proj, proj, proj, conv_w], (), ("parallel",), comm)


def _attn_bwd(proj, dattn, sinks, tables, comm=None):
    def body(sink_ref, q_ref, kp_ref, kc_ref, vp_ref, vc_ref, c_ref, su_ref, sd_ref, cp_ref, sup_ref, sdp_ref,
             do_ref, dq_ref, dkp_ref, dkc_ref, dvp_ref, dvc_ref, ds_ref):
        n = pl.program_id(0)
        mask = _group_mask(n)
        tabs_cur = (c_ref, su_ref, sd_ref)
        k, v = _band_kv(kp_ref, kc_ref, vp_ref, vc_ref, tabs_cur, (cp_ref, sup_ref, sdp_ref))
        lane = lax.broadcasted_iota(jnp.int32, (1, LANES), 1)
        dsink = jnp.zeros((1, LANES), F32)
        c, su, sd = c_ref[:, :LANES], su_ref[:, :LANES], sd_ref[:, :LANES]
        dk_tiles = [None] * (N_KV_HEADS // HEADS_PER_TILE)
        dv_tiles = [None] * (N_KV_HEADS // HEADS_PER_TILE)
        for h in range(N_KV_HEADS):
            k_halves = _head_tiles(k, h)
            v_halves = _head_tiles(v, h)
            tiles = [h * TILES_PER_GROUP + t for t in range(TILES_PER_GROUP)]
            q_rows = _query_tiles(q_ref, tiles, tabs_cur)
            do_rows = jnp.concatenate([do_ref[:, t * LANES:(t + 1) * LANES] for t in tiles], axis=0)
            dk_par, dv_par = [], []
            dq_rows = None
            for par in range(HEADS_PER_TILE):
                p, p_sink = _head_softmax(q_rows, k_halves[par], _sink_row(sink_ref, tiles, par), mask)
                dp = lax.dot_general(v_halves[par], do_rows, (NT, ((), ())), preferred_element_type=F32)
                delta = jnp.sum(p * dp, axis=0, keepdims=True)
                ds = (p * (dp - delta) * ATTN_SCALE).astype(BF16)
                dq = lax.dot_general(ds, k_halves[par], (TN, ((), ())), preferred_element_type=F32)
                dq_rows = dq if dq_rows is None else dq_rows + dq
                dk_par.append(jnp.dot(ds, q_rows, preferred_element_type=F32))
                dv_par.append(jnp.dot(p.astype(BF16), do_rows, preferred_element_type=F32))
                sink_grad = p_sink * delta
                for i, tile in enumerate(tiles):
                    val = -jnp.sum(sink_grad[:, i * BLOCK:(i + 1) * BLOCK], axis=1, keepdims=True)
                    dsink = dsink + jnp.where(lane == tile * HEADS_PER_TILE + par, val, 0.0)
            for i, tile in enumerate(tiles):
                dq_tile = dq_rows[i * BLOCK:(i + 1) * BLOCK, :]
                dq_ref[:, tile * LANES:(tile + 1) * LANES] = _rotate(dq_tile, c, -su, -sd).astype(BF16)
            own = h % HEADS_PER_TILE
            for par_grads, tiles in ((dk_par, dk_tiles), (dv_par, dv_tiles)):
                shifted = pltpu.roll(par_grads[1 - own], HEAD_DIM, axis=1)
                total = jnp.where(_lane_half(shifted.shape, own), par_grads[own] + shifted, 0.0)
                i = h // HEADS_PER_TILE
                tiles[i] = total if tiles[i] is None else tiles[i] + total
        for i in range(N_KV_HEADS // HEADS_PER_TILE):
            cols = slice(i * LANES, (i + 1) * LANES)
            dkp_ref[:, cols] = dk_tiles[i][:BLOCK, :]
            dkc_ref[:, cols] = dk_tiles[i][BLOCK:, :]
            dvp_ref[:, cols] = dv_tiles[i][:BLOCK, :]
            dvc_ref[:, cols] = dv_tiles[i][BLOCK:, :]

        @pl.when(n == 0)
        def _():
            ds_ref[...] = dsink

        @pl.when(n > 0)
        def _():
            ds_ref[...] += dsink

    blk = pl.BlockSpec((BLOCK, D_KV), lambda n: (n, 0))
    prev_blk = pl.BlockSpec((BLOCK, D_KV), lambda n: ((n + N_QBLK - 1) % N_QBLK, 0))
    kv = _sds((SEQ, D_KV), F32)
    return _pcall(
        body, "attn_bwd", (N_QBLK,),
        [pl.BlockSpec(memory_space=pltpu.SMEM)] + _attn_specs() + [pl.BlockSpec((BLOCK, D_ATTN), lambda n: (n, 0))],
        [pl.BlockSpec((BLOCK, D_ATTN), lambda n: (n, 0)), prev_blk, blk, prev_blk, blk, _full_spec((1, LANES))],
        [_sds((SEQ, D_ATTN), BF16), kv, kv, kv, kv, _sds((1, LANES), F32)],
        [sinks] + [proj] * 5 + list(tables) * 2 + [dattn], (), ("arbitrary",), comm)


def _kv_grad_combine(dk_prev, dk_cur, dv_prev, dv_cur, tables):
    rows = 4 * BLOCK

    def body(kp_ref, kc_ref, vp_ref, vc_ref, c_ref, su_ref, sd_ref, o_ref):
        dk = kc_ref[...] + kp_ref[...]
        dv = vc_ref[...] + vp_ref[...]
        o_ref[:, :D_KV] = _rotate(dk, c_ref[...], -su_ref[...], -sd_ref[...]).astype(BF16)
        o_ref[:, D_KV:] = dv.astype(BF16)

    blk = pl.BlockSpec((rows, D_KV), lambda m: (m, 0))
    return pl.pallas_call(
        body, name="kv_grad_combine", grid=(SEQ // rows,),
        in_specs=[blk] * 7,
        out_specs=pl.BlockSpec((rows, 2 * D_KV), lambda m: (m, 0)),
        out_shape=_sds((SEQ, 2 * D_KV), BF16),
        compiler_params=_params(("parallel",)),
    )(dk_prev, dk_cur, dv_prev, dv_cur, *tables)


MATRICES = {
    "w_in": (D_MODEL, N_IN // N_CHIPS, "col"),
    "w_conv_out": (D_MODEL // N_CHIPS, D_MODEL, "row"),
    "w_attn_out": (D_MODEL // N_CHIPS, D_MODEL, "row"),
    "w_o": (D_MODEL // N_CHIPS, D_MODEL, "row"),
    "w_gate_up": (D_MODEL, 2 * D_FF // N_CHIPS, "col"),
    "w_down": (D_FF // N_CHIPS, D_MODEL, "row"),
}
BF16_ROW_TILE = 16
CONV_W_COLS = D_MODEL // N_CHIPS
SMALL_ROWS = 8


def _whole_shape(spec):
    rows, cols, kind = spec
    return (rows, cols * N_CHIPS) if kind == "col" else (rows * N_CHIPS, cols)


def _half_shape(spec):
    return (spec[0] // 2, spec[1])


def _aligned(start, multiple):
    return start if isinstance(start, int) else pl.multiple_of(start, multiple)


def _region(ref, spec, shard, half, part=0, parts=1):
    rows, cols, kind = spec
    hr = rows // 2
    n = hr // parts
    if kind == "col":
        return ref.at[pl.ds(_aligned(half * hr + part * n, BF16_ROW_TILE), n),
                      pl.ds(_aligned(shard * cols, LANES), cols)]
    return ref.at[pl.ds(_aligned(shard * rows + half * hr + part * n, BF16_ROW_TILE), n), :]


def _position():
    x, y, c = lax.axis_index("x"), lax.axis_index("y"), lax.axis_index("c")
    chips = [(1 - x, y), (x, 1 - y), (1 - x, 1 - y)]
    return x, y, c, chips


def _shard_of(chip):
    return 2 * chip[0] + chip[1]


def _remote(src, dst, send_sem, recv_sem, to):
    return pltpu.make_async_remote_copy(src_ref=src, dst_ref=dst, send_sem=send_sem, recv_sem=recv_sem,
                                        device_id=to, device_id_type=MESH)


CAST_STEPS = 4


def _to_bf16_in_whole(ws, specs, shard):
    n = len(ws)

    def body(s_ref, *refs):
        del s_ref
        for w_ref, o_ref in zip(refs[:n], refs[n:]):
            o_ref[...] = w_ref[...].astype(BF16)

    def out_spec(spec):
        rows = spec[0] // CAST_STEPS
        if spec[2] == "col":
            return pl.BlockSpec((rows, spec[1]), lambda i, s_ref: (i, s_ref[0]))
        return pl.BlockSpec((rows, spec[1]), lambda i, s_ref: (s_ref[0] * CAST_STEPS + i, 0))

    grid_spec = pltpu.PrefetchScalarGridSpec(
        num_scalar_prefetch=1, grid=(CAST_STEPS,),
        in_specs=[pl.BlockSpec((s[0] // CAST_STEPS, s[1]), lambda i, s_ref: (i, 0)) for s in specs],
        out_specs=[out_spec(s) for s in specs])
    return list(pl.pallas_call(
        body, name="cast_shards", grid_spec=grid_spec, out_shape=[_sds(_whole_shape(s), BF16) for s in specs],
        compiler_params=_params(("parallel",)),
    )(shard, *ws))


class _Gather:
    def __init__(self, wholes, pieces, conv_w=None):
        self.pieces = pieces
        self.n = len(wholes)
        self.with_conv_w = conv_w is not None
        self.operands = list(wholes) + ([conv_w] if self.with_conv_w else [])
        self.out_shape = [_sds(w.shape, w.dtype) for w in wholes]
        if self.with_conv_w:
            self.out_shape.append(_sds((3, D_MODEL), F32))
        self.aliases = {i: i for i in range(self.n)}
        n_ici = 3 * len(pieces)
        self.sems = [pltpu.SemaphoreType.DMA((n_ici,))] * 4
        if self.with_conv_w:
            self.sems += [pltpu.SemaphoreType.DMA((1,)), pltpu.SemaphoreType.DMA((3,)), pltpu.SemaphoreType.DMA((3,))]

    def _conv_w(self, cins, couts, sems, with_recvs):
        cw_in, cw_out = cins[self.n], couts[self.n]
        x, y, c, chips = _position()

        def cols(shard):
            return cw_out.at[:, pl.ds(_aligned(shard * CONV_W_COLS, LANES), CONV_W_COLS)]

        me = _shard_of((x, y))
        local = pltpu.make_async_copy(cw_in, cols(me), sems[4].at[0])
        sends = [_remote(cw_in, cols(me), sems[5].at[j], sems[6].at[j], (*chip, c)) for j, chip in enumerate(chips)]
        if not with_recvs:
            return local, sends, []
        recvs = [_remote(cols(_shard_of(chip)), cols(_shard_of(chip)), sems[5].at[j], sems[6].at[j], (*chip, c))
                 for j, chip in enumerate(chips)]
        return local, sends, recvs

    def start(self, cins, couts, sems):
        x, y, c, chips = _position()
        me = _shard_of((x, y))
        if self.with_conv_w:
            local, sends, _ = self._conv_w(cins, couts, sems, False)
            local.start()
            for cp in sends:
                cp.start()
        for p, (i, spec, part, parts) in enumerate(self.pieces):
            mine = _region(couts[i], spec, me, c, part, parts)
            for j, chip in enumerate(chips):
                _remote(mine, mine, sems[0].at[3 * p + j], sems[1].at[3 * p + j], (*chip, c)).start()

    def finish(self, cins, couts, sems):
        x, y, c, chips = _position()
        me = _shard_of((x, y))
        sibling = (x, y, 1 - c)
        send_a, recv_a, send_b, recv_b = sems[:4]
        passed = []
        for p, (i, spec, part, parts) in enumerate(self.pieces):
            for j, chip in enumerate(chips):
                k = 3 * p + j
                landed = _region(couts[i], spec, _shard_of(chip), c, part, parts)
                _remote(landed, landed, send_a.at[k], recv_a.at[k], (*chip, c)).wait_recv()
                cp = _remote(landed, landed, send_b.at[k], recv_b.at[k], sibling)
                cp.start()
                passed.append(cp)
        for p, (i, spec, part, parts) in enumerate(self.pieces):
            mine = _region(couts[i], spec, me, c, part, parts)
            for j, chip in enumerate(chips):
                k = 3 * p + j
                other = _region(couts[i], spec, _shard_of(chip), 1 - c, part, parts)
                _remote(other, other, send_b.at[k], recv_b.at[k], sibling).wait_recv()
                _remote(mine, mine, send_a.at[k], recv_a.at[k], (*chip, c)).wait_send()
        for cp in passed:
            cp.wait_send()
        if self.with_conv_w:
            local, sends, recvs = self._conv_w(cins, couts, sems, True)
            for cp in recvs:
                cp.wait_recv()
            for cp in sends:
                cp.wait_send()
            local.wait()


def _mm_in_gather(h1, w_whole, comm):
    spec = MATRICES["w_in"]
    cols = spec[1]
    bm = SEQ // 2

    def body(h_ref, w_in_ref, proj_ref, w_ref, wbuf, obuf, send_a, recv_a, send_b, recv_b, load_sem, store_sems):
        del w_in_ref
        s, mi = pl.program_id(0), pl.program_id(1)
        x, y, c, chips = _position()
        me = _shard_of((x, y))
        sibling = (x, y, 1 - c)
        mine = _region(w_ref, spec, me, c)

        @pl.when((s == 0) & (mi == 0))
        def _():
            for j, chip in enumerate(chips):
                _remote(mine, mine, send_a.at[j], recv_a.at[j], (*chip, c)).start()

        shard = me
        for j, chip in enumerate(chips):
            shard = jnp.where(s == j + 1, _shard_of(chip), shard)

            @pl.when((s == j + 1) & (mi == 0))
            def _():
                landed = _region(w_ref, spec, _shard_of(chip), c)
                _remote(landed, landed, send_a.at[j], recv_a.at[j], (*chip, c)).wait_recv()
                _remote(landed, landed, send_b.at[j], recv_b.at[j], sibling).start()
                other = _region(w_ref, spec, _shard_of(chip), 1 - c)
                _remote(other, other, send_b.at[j], recv_b.at[j], sibling).wait_recv()

        col0 = pl.multiple_of(shard * cols, LANES)

        @pl.when(mi == 0)
        def _():
            load = pltpu.make_async_copy(w_ref.at[:, pl.ds(col0, cols)], wbuf, load_sem.at[0])
            load.start()
            load.wait()

        def store():
            rows = pl.ds(pl.multiple_of(mi * bm, bm), bm)
            return pltpu.make_async_copy(obuf.at[mi], proj_ref.at[rows, pl.ds(col0, cols)], store_sems.at[mi])

        @pl.when(s > 0)
        def _():
            store().wait()

        obuf[mi] = jnp.dot(h_ref[...], wbuf[...], preferred_element_type=F32)
        store().start()

        @pl.when(s == N_CHIPS - 1)
        def _():
            store().wait()

        @pl.when((s == N_CHIPS - 1) & (mi == 1))
        def _():
            for j, chip in enumerate(chips):
                landed = _region(w_ref, spec, _shard_of(chip), c)
                _remote(mine, mine, send_a.at[j], recv_a.at[j], (*chip, c)).wait_send()
                _remote(landed, landed, send_b.at[j], recv_b.at[j], sibling).wait_send()

    sem3 = pltpu.SemaphoreType.DMA((3,))
    (proj, whole), extra = _pcall(
        body, "mm_in", (N_CHIPS, SEQ // bm),
        [pl.BlockSpec((bm, D_MODEL), lambda s, m: (m, 0)), HBM_SPEC], [HBM_SPEC, HBM_SPEC],
        [_sds((SEQ, N_IN), F32), _sds(w_whole.shape, w_whole.dtype)], [h1, w_whole],
        [pltpu.VMEM((D_MODEL, cols), BF16), pltpu.VMEM((SEQ // bm, bm, cols), F32), sem3, sem3, sem3, sem3,
         pltpu.SemaphoreType.DMA((1,)), pltpu.SemaphoreType.DMA((SEQ // bm,))],
        None, comm, aliases={1: 1}, start_after_body=True)
    return proj, whole, extra


def _mm_dw_in_pair(h1, dproj, comm):
    spec = MATRICES["w_in"]
    rows, cols, _ = spec
    hr = rows // 2

    def body(h_ref, dp_ref, dw_ref, got_ref, obuf, store_sems, send_sems, recv_sems):
        t = pl.program_id(0)
        x, y, c, _ = _position()
        sibling = (x, y, 1 - c)

        def store(step):
            return pltpu.make_async_copy(obuf.at[step % 2], dw_ref.at[:, pl.ds(step * cols, cols)],
                                         store_sems.at[step % 2])

        def send(step):
            theirs = obuf.at[step % 2, pl.ds(_aligned((1 - c) * hr, BF16_ROW_TILE), hr), :]
            return _remote(theirs, got_ref.at[step], send_sems.at[step], recv_sems.at[step], sibling)

        for step in range(N_CHIPS):
            @pl.when(t == step)
            def _():
                if step >= 2:
                    store(step - 2).wait()
                    send(step - 2).wait_send()
                obuf[step % 2] = lax.dot_general(
                    h_ref[...], dp_ref[...], (TN, ((), ())), preferred_element_type=F32).astype(BF16)
                store(step).start()
                send(step).start()

        @pl.when(t == N_CHIPS - 1)
        def _():
            for step in (N_CHIPS - 2, N_CHIPS - 1):
                store(step).wait()
                send(step).wait_send()
            for step in range(N_CHIPS):
                send(step).wait_recv()

    sem4 = pltpu.SemaphoreType.DMA((N_CHIPS,))
    (dw_in, got), extra = _pcall(
        body, "mm_dw_in", (N_CHIPS,),
        [_full_spec((SEQ, D_MODEL)), pl.BlockSpec((SEQ, cols), lambda t: (0, t))], [HBM_SPEC, HBM_SPEC],
        [_sds(_whole_shape(spec), BF16), _sds((N_CHIPS, hr, cols), BF16)], [h1, dproj],
        [pltpu.VMEM((2, rows, cols), BF16), pltpu.SemaphoreType.DMA((2,)), sem4, sem4], None, comm)
    return dw_in, got, extra


def _pack_small(dg_mix, dg_ffn, dg_final, dconv_w, dsinks, loss_row):
    def body(a_ref, b_ref, c_ref, w_ref, s_ref, l_ref, o_ref):
        pad = jnp.zeros((1, D_MODEL - LANES), F32)
        o_ref[0:1, :] = a_ref[...]
        o_ref[1:2, :] = b_ref[...]
        o_ref[2:3, :] = c_ref[...]
        o_ref[3:6, :] = w_ref[...]
        o_ref[6:7, :] = jnp.concatenate([s_ref[...], pad], axis=1)
        o_ref[7:8, :] = jnp.concatenate([l_ref[...], pad], axis=1)

    return pl.pallas_call(
        body, name="pack_small", out_shape=_sds((SMALL_ROWS, D_MODEL), F32),
        compiler_params=_params(),
    )(dg_mix, dg_ffn, dg_final, dconv_w, dsinks, loss_row)


class _Pair:
    def __init__(self, dws, specs):
        self.specs = specs
        self.operands = list(dws)
        self.out_shape = [_sds((N_CHIPS, *_half_shape(s)), BF16) for s in specs]
        self.aliases = {}
        n = N_CHIPS * len(specs)
        self.sems = [pltpu.SemaphoreType.DMA((n,)), pltpu.SemaphoreType.DMA((n,))]

    def _copies(self, cins, couts, sems):
        x, y, c, _ = _position()
        sibling = (x, y, 1 - c)
        for i, spec in enumerate(self.specs):
            for t in range(N_CHIPS):
                k = N_CHIPS * i + t
                yield _remote(_region(cins[i], spec, t, 1 - c), couts[i].at[t], sems[0].at[k], sems[1].at[k], sibling)

    def start(self, cins, couts, sems):
        for cp in self._copies(cins, couts, sems):
            cp.start()

    def finish(self, cins, couts, sems):
        for cp in self._copies(cins, couts, sems):
            cp.wait()


class _SmallAllToAll:
    def __init__(self, small):
        self.operands = [small]
        self.out_shape = [_sds((N_DEV, SMALL_ROWS, D_MODEL), F32)]
        self.aliases = {}
        self.sems = [pltpu.SemaphoreType.DMA((N_DEV - 1,)), pltpu.SemaphoreType.DMA((N_DEV - 1,)),
                     pltpu.SemaphoreType.DMA((1,))]

    def _copies(self, cins, couts, sems):
        x, y, c, _ = _position()
        me = 4 * x + 2 * y + c
        out = []
        for r in range(1, N_DEV):
            flip = ((r >> 2) & 1, (r >> 1) & 1, r & 1)
            peer = tuple(1 - p if f else p for p, f in zip((x, y, c), flip))
            theirs = couts[0].at[4 * peer[0] + 2 * peer[1] + peer[2]]
            out.append((_remote(cins[0], couts[0].at[me], sems[0].at[r - 1], sems[1].at[r - 1], peer),
                        functools.partial(_remote, theirs, theirs, sems[0].at[r - 1], sems[1].at[r - 1], peer)))
        return pltpu.make_async_copy(cins[0], couts[0].at[me], sems[2].at[0]), out

    def start(self, cins, couts, sems):
        own, copies = self._copies(cins, couts, sems)
        own.start()
        for send, _ in copies:
            send.start()

    def finish(self, cins, couts, sems):
        own, copies = self._copies(cins, couts, sems)
        for send, recv in copies:
            recv().wait_recv()
            send.wait_send()
        own.wait()


class _Both:
    def __init__(self, a, b):
        self.a, self.b = a, b
        self.operands = list(a.operands) + list(b.operands)
        self.out_shape = list(a.out_shape) + list(b.out_shape)
        self.aliases = dict(a.aliases)
        self.aliases.update({len(a.operands) + k: len(a.out_shape) + v for k, v in b.aliases.items()})
        self.sems = list(a.sems) + list(b.sems)

    def _split(self, cins, couts, sems):
        na, ma, sa = len(self.a.operands), len(self.a.out_shape), len(self.a.sems)
        return (cins[:na], couts[:ma], sems[:sa]), (cins[na:], couts[ma:], sems[sa:])

    def start(self, cins, couts, sems):
        for plan, args in zip((self.a, self.b), self._split(cins, couts, sems)):
            plan.start(*args)

    def finish(self, cins, couts, sems):
        for plan, args in zip((self.a, self.b), self._split(cins, couts, sems)):
            plan.finish(*args)


def _pair_sum(name, specs, dws, got, place):
    n_mat = len(specs)

    def body(p_ref, *refs):
        t = pl.program_id(0)
        mine, theirs = refs[:n_mat], refs[n_mat:2 * n_mat]
        outs, owns = refs[2 * n_mat:3 * n_mat], refs[3 * n_mat:]
        for a, b, o, own in zip(mine, theirs, outs, owns):
            s = (a[...].astype(F32) + b[...].astype(F32)).astype(BF16)
            o[...] = s

            @pl.when(t == p_ref[1])
            def _():
                own[...] = s

    def mine_spec(spec):
        hr, cols = _half_shape(spec)
        if spec[2] == "col":
            return pl.BlockSpec((hr, cols), lambda t, p_ref: (p_ref[0], t))
        return pl.BlockSpec((hr, cols), lambda t, p_ref: (2 * t + p_ref[0], 0))

    def slot_spec(spec):
        return pl.BlockSpec((None, *_half_shape(spec)), lambda t, p_ref: (t, 0, 0))

    def own_spec(spec):
        return pl.BlockSpec((None, *_half_shape(spec)), lambda t, p_ref: (p_ref[1], 0, 0))

    slots = [_sds((N_CHIPS, *_half_shape(s)), BF16) for s in specs]
    grid_spec = pltpu.PrefetchScalarGridSpec(
        num_scalar_prefetch=1, grid=(N_CHIPS,),
        in_specs=[mine_spec(s) for s in specs] + [slot_spec(s) for s in specs],
        out_specs=[slot_spec(s) for s in specs] + [own_spec(s) for s in specs])
    res = pl.pallas_call(
        body, name=name, grid_spec=grid_spec, out_shape=slots + slots,
        compiler_params=_params(("arbitrary",)),
    )(place, *dws, *got)
    return list(res[:n_mat]), list(res[n_mat:])


class _ChipExchange:
    def __init__(self, sums, slots, part=0, parts=1):
        self.n = len(sums)
        self.part, self.parts = part, parts
        self.operands = list(sums) + list(slots)
        self.out_shape = [_sds(s.shape, s.dtype) for s in slots]
        self.aliases = {self.n + i: i for i in range(self.n)}
        self.sems = [pltpu.SemaphoreType.DMA((3 * self.n,)), pltpu.SemaphoreType.DMA((3 * self.n,))]

    def _rows(self, ref, slot):
        n = ref.shape[1] // self.parts
        return ref.at[slot, pl.ds(self.part * n, n), :]

    def _copies(self, cins, couts, sems):
        x, y, c, chips = _position()
        me = _shard_of((x, y))
        for i in range(self.n):
            for j, chip in enumerate(chips):
                k = 3 * i + j
                theirs = self._rows(couts[i], _shard_of(chip))
                yield (_remote(self._rows(cins[i], _shard_of(chip)), self._rows(couts[i], me),
                               sems[0].at[k], sems[1].at[k], (*chip, c)),
                       functools.partial(_remote, theirs, theirs, sems[0].at[k], sems[1].at[k], (*chip, c)))

    def start(self, cins, couts, sems):
        for send, _ in self._copies(cins, couts, sems):
            send.start()

    def finish(self, cins, couts, sems):
        for send, recv in self._copies(cins, couts, sems):
            recv().wait_recv()
            send.wait_send()


def _chip_sum(name, specs, slots, core):
    steps = 2
    n_mat = len(specs)

    def body(c_ref, *refs):
        del c_ref
        ins, outs = refs[:n_mat], refs[n_mat:]
        for a, o in zip(ins, outs):
            acc = a[0].astype(F32)
            for t in range(1, N_CHIPS):
                acc = acc + a[t].astype(F32)
            o[...] = acc

    def in_spec(spec):
        hr, cols = _half_shape(spec)
        return pl.BlockSpec((N_CHIPS, hr // steps, cols), lambda i, c_ref: (0, i, 0))

    def out_spec(spec):
        hr, cols = _half_shape(spec)
        return pl.BlockSpec((hr // steps, cols), lambda i, c_ref: (c_ref[0] * steps + i, 0))

    grid_spec = pltpu.PrefetchScalarGridSpec(
        num_scalar_prefetch=1, grid=(steps,),
        in_specs=[in_spec(s) for s in specs], out_specs=[out_spec(s) for s in specs])
    return list(pl.pallas_call(
        body, name=name, grid_spec=grid_spec,
        out_shape=[_sds((s[0], s[1]), F32) for s in specs],
        compiler_params=_params(("parallel",)),
    )(core, *slots))


class _HalfExchange:
    def __init__(self, grads, specs):
        self.specs = specs
        self.operands = list(grads)
        self.out_shape = [_sds(g.shape, g.dtype) for g in grads]
        self.aliases = {i: i for i in range(len(grads))}
        self.sems = [pltpu.SemaphoreType.DMA((len(grads),)), pltpu.SemaphoreType.DMA((len(grads),))]

    def _copies(self, couts, sems):
        x, y, c, _ = _position()
        sibling = (x, y, 1 - c)
        for i, spec in enumerate(self.specs):
            hr = spec[0] // 2
            mine = couts[i].at[pl.ds(_aligned(c * hr, 8), hr), :]
            theirs = couts[i].at[pl.ds(_aligned((1 - c) * hr, 8), hr), :]
            yield (_remote(mine, mine, sems[0].at[i], sems[1].at[i], sibling),
                   functools.partial(_remote, theirs, theirs, sems[0].at[i], sems[1].at[i], sibling))

    def start(self, cins, couts, sems):
        for send, _ in self._copies(couts, sems):
            send.start()

    def finish(self, cins, couts, sems):
        for send, recv in self._copies(couts, sems):
            recv().wait_recv()
            send.wait_send()


def _small_sum(blocks):
    def body(b_ref, o_ref):
        acc = b_ref[0]
        for d in range(1, N_DEV):
            acc = acc + b_ref[d]
        o_ref[...] = acc

    return pl.pallas_call(
        body, name="small_sum", out_shape=_sds((SMALL_ROWS, D_MODEL), F32), compiler_params=_params(),
    )(blocks)


def _adamw(name, params, steps):
    n = len(params)

    def body(*refs):
        for p in range(n):
            w_ref, g_ref, m_ref, v_ref = refs[4 * p:4 * p + 4]
            d_ref, nm_ref, nv_ref, go_ref = refs[4 * n + 4 * p:4 * n + 4 * p + 4]
            g = g_ref[...]
            go_ref[...] = g
            m = ADAM_B1 * m_ref[...] + (1.0 - ADAM_B1) * g
            v = ADAM_B2 * v_ref[...] + (1.0 - ADAM_B2) * jnp.square(g)
            m_hat = m / (1.0 - ADAM_B1 ** ADAM_STEP)
            v_hat = v / (1.0 - ADAM_B2 ** ADAM_STEP)
            d_ref[...] = -ADAM_LR * (m_hat / (jnp.sqrt(v_hat) + ADAM_EPS) + ADAM_WD * w_ref[...])
            nm_ref[...] = m
            nv_ref[...] = v

    in_specs, out_specs, out_shape, operands = [], [], [], []
    for w, g, m, v in params:
        spec = pl.BlockSpec((w.shape[0] // steps, w.shape[1]), lambda i: (i, 0))
        in_specs += [spec] * 4
        out_specs += [spec] * 4
        out_shape += [_sds(w.shape, F32)] * 4
        operands += [w, g, m, v]
    outs = _pcall(body, name, (steps,), in_specs, out_specs, out_shape, operands, (), ("parallel",))
    return [tuple(outs[4 * p:4 * p + 4]) for p in range(n)]


MATRIX_NAMES = tuple(MATRICES)
WEIGHT_ORDER = ("g_mix", "w_in", "conv_w", "attn_sinks", "w_conv_out", "w_attn_out", "w_o", "g_ffn",
                "w_gate_up", "w_down", "g_final")


def kernel(x, g_mix, w_in, conv_w, attn_sinks, w_conv_out, w_attn_out, w_o, g_ffn, w_gate_up, w_down, g_final, loss_target, m_g_mix, m_w_in, m_conv_w, m_attn_sinks, m_w_conv_out, m_w_attn_out, m_w_o, m_g_ffn, m_w_gate_up, m_w_down, m_g_final, v_g_mix, v_w_in, v_conv_w, v_attn_sinks, v_w_conv_out, v_w_attn_out, v_w_o, v_g_ffn, v_w_gate_up, v_w_down, v_g_final):
    w = dict(g_mix=g_mix, w_in=w_in[0], conv_w=conv_w[0], attn_sinks=attn_sinks, w_conv_out=w_conv_out[0],
             w_attn_out=w_attn_out[0], w_o=w_o[0], g_ffn=g_ffn, w_gate_up=w_gate_up[0], w_down=w_down[0],
             g_final=g_final[None, :])
    m = dict(g_mix=m_g_mix, w_in=m_w_in[0], conv_w=m_conv_w[0], attn_sinks=m_attn_sinks,
             w_conv_out=m_w_conv_out[0], w_attn_out=m_w_attn_out[0], w_o=m_w_o[0], g_ffn=m_g_ffn,
             w_gate_up=m_w_gate_up[0], w_down=m_w_down[0], g_final=m_g_final[None, :])
    v = dict(g_mix=v_g_mix, w_in=v_w_in[0], conv_w=v_conv_w[0], attn_sinks=v_attn_sinks,
             w_conv_out=v_w_conv_out[0], w_attn_out=v_w_attn_out[0], w_o=v_w_o[0], g_ffn=v_g_ffn,
             w_gate_up=v_w_gate_up[0], w_down=v_w_down[0], g_final=v_g_final[None, :])
    shard = (2 * lax.axis_index("x") + lax.axis_index("y")).astype(jnp.int32)
    core = lax.axis_index("c").astype(jnp.int32)
    shard1, core1, place = shard.reshape((1,)), core.reshape((1,)), jnp.stack([core, shard])
    spec = MATRICES
    xs, target, sinks = x[0], loss_target[0], w["attn_sinks"]
    tables = _rope_tables()

    def gather(names, part=0, parts=1):
        return _Gather([whole[n] for n in names], [(i, spec[n], part, parts) for i, n in enumerate(names)])

    def pair(names):
        return _Pair([dw[n] for n in names], [spec[n] for n in names])

    def pair_sum(tag, names, got):
        return _pair_sum("pair_sum_" + tag, [spec[n] for n in names], [dw[n] for n in names], got, place)

    whole = dict(zip(MATRIX_NAMES, _to_bf16_in_whole(
        [w[n] for n in MATRIX_NAMES], [spec[n] for n in MATRIX_NAMES], shard1)))

    mixers = ("w_conv_out", "w_attn_out", "w_o")
    h1 = _rms_norm("norm_mix", xs, w["g_mix"])
    proj, whole["w_in"], (*got, conv_w_whole) = _mm_in_gather(
        h1, whole["w_in"], _Gather([whole[n] for n in mixers], [(i, spec[n], 0, 1) for i, n in enumerate(mixers)],
                                   conv_w=w["conv_w"]))
    whole.update(zip(mixers, got))
    conv_y = _conv_fwd(proj, conv_w_whole)
    attn, (whole["w_gate_up"],) = _attn_fwd(proj, tables, sinks, comm=gather(("w_gate_up",), 0, 2))
    (conv_out, attn_out, merged), (whole["w_gate_up"],) = _branch_merge(
        conv_y, attn, whole["w_conv_out"], whole["w_attn_out"], proj, comm=gather(("w_gate_up",), 1, 2))
    x2 = _mm_nn("mm_o", merged, whole["w_o"], 1024, 1024, F32, res=xs)
    h2 = _rms_norm("norm_ffn", x2, w["g_ffn"])
    (gate, up, act), (whole["w_down"],) = _gate_up_fwd(h2, whole["w_gate_up"], comm=gather(("w_down",)))
    x3 = _mm_nn("mm_down", act, whole["w_down"], 1024, 512, F32, res=x2)
    dx3, dx3b, dg_final, loss_row = _loss_head(x3, w["g_final"], target)

    dw = {}
    dw["w_down"] = _mm_tn("mm_dw_down", act, dx3b, 1408, 1024, BF16)
    dgate, dup = _dact_swiglu(dx3b, whole["w_down"], gate, up)
    dw["w_gate_up"], got = _mm_dw_gate_up(h2, dgate, dup, comm=pair(("w_down",)))
    sums_a, own_a = pair_sum("down", ("w_down",), got)
    dh2, slots_a = _mm_dh2(dgate, dup, whole["w_gate_up"], comm=_ChipExchange(sums_a, own_a))
    (dx2, dx2b, dg_ffn), got_b = _rms_norm_bwd("norm_ffn_bwd", dh2, x2, w["g_ffn"], dx3, True,
                                               comm=pair(("w_gate_up",)))
    dw["w_o"] = _mm_tn("mm_dw_o", merged, dx2b, 1024, 1024, BF16)
    dco, dao, dgc, dga = _merge_bwd(dx2b, whole["w_o"], conv_out, attn_out, proj)
    dconv_y = _mm_nt("mm_dconv_y", dco, whole["w_conv_out"], 1024, 1024, D_MODEL, F32)
    dw["w_conv_out"] = _mm_tn("mm_dw_conv_out", conv_y, dco, 1024, 1024, BF16)
    dattn = _mm_nt("mm_dattn", dao, whole["w_attn_out"], 1024, 1024, D_MODEL, BF16)
    dw["w_attn_out"] = _mm_tn("mm_dw_attn_out", attn, dao, 1024, 1024, BF16)
    (dcb, dcc, dcx, dconv_w), got_c = _conv_bwd(dconv_y, proj, conv_w_whole, comm=pair(mixers))
    sums_bc, own_bc = pair_sum("gate_up_mixers", ("w_gate_up",) + mixers, got_b + got_c)
    sums_b, own_b, sums_c, own_c = sums_bc[:1], own_bc[:1], sums_bc[1:], own_bc[1:]
    (dq, dk_prev, dk_cur, dv_prev, dv_cur, dsinks), slots_b = _attn_bwd(
        proj, dattn, sinks, tables, comm=_ChipExchange(sums_b, own_b))
    dkv = _kv_grad_combine(dk_prev, dk_cur, dv_prev, dv_cur, tables)
    dproj = jnp.concatenate([dcb, dcc, dcx, dq, dkv, dgc, dga], axis=1)
    dw["w_in"], got, slots_c = _mm_dw_in_pair(h1, dproj, _ChipExchange(sums_c, own_c))
    sums_d, own_d = pair_sum("in", ("w_in",), [got])
    early = ("w_down", "w_gate_up") + mixers
    halves = _chip_sum("chip_sum_early", [spec[n] for n in early], slots_a + slots_b + slots_c, core1)
    dh1, (own_d, *reduced) = _mm_nt(
        "mm_dh1", dproj, whole["w_in"], 1024, 1024, 1664, F32,
        comm=_Both(_ChipExchange(sums_d, own_d, 0, 2), _HalfExchange(halves, [spec[n] for n in early])))
    g = dict(zip(early, reduced))
    (grad_x, dg_mix), slots_d = _rms_norm_bwd("norm_mix_bwd", dh1, xs, w["g_mix"], dx2, False,
                                              comm=_ChipExchange(sums_d, [own_d], 1, 2))
    small = _pack_small(dg_mix, dg_ffn, dg_final, dconv_w, dsinks, loss_row)
    half_in = _chip_sum("chip_sum_in", [spec["w_in"]], slots_d, core1)
    g["w_in"], small_blocks = _comm_call(
        "half_exchange_in", _Both(_HalfExchange(half_in, [spec["w_in"]]), _SmallAllToAll(small)))
    delta, new_m, new_v = {}, {}, {}

    def keep(names, results):
        for n, (d, nm, nv, grad) in zip(names, results):
            delta[n], new_m[n], new_v[n], g[n] = d, nm, nv, grad

    keep(early, _adamw("adamw_early", [(w[n], g[n], m[n], v[n]) for n in early], 8))
    small_sum = _small_sum(small_blocks)
    g["g_mix"] = small_sum[0:1, :]
    g["g_ffn"] = small_sum[1:2, :]
    g["g_final"] = small_sum[2:3, :]
    g["conv_w"] = lax.dynamic_slice(small_sum, (3, shard * CONV_W_COLS), (3, CONV_W_COLS))
    g["attn_sinks"] = small_sum[6:7, :N_HEADS]
    loss = small_sum[7, 0]
    keep(("w_in",), _adamw("adamw_w_in", [(w["w_in"], g["w_in"], m["w_in"], v["w_in"])], 4))
    rest = ("g_mix", "g_ffn", "g_final", "conv_w", "attn_sinks")
    keep(rest, _adamw("adamw_small", [(w[n], g[n], m[n], v[n]) for n in rest], 1))

    def shaped(vals):
        return [vals[n].reshape((D_MODEL,)) if n == "g_final" else
                (vals[n][None] if n in MATRIX_NAMES or n == "conv_w" else vals[n]) for n in WEIGHT_ORDER]

    return (loss, grad_x[None], *shaped(g), *shaped(delta), *shaped(new_m), *shaped(new_v))
```

```python
import functools
import math

import jax
import jax.numpy as jnp
import numpy as np
from jax import lax
from jax.experimental import pallas as pl
from jax.experimental.pallas import tpu as pltpu

F32 = jnp.float32
BF16 = jnp.bfloat16

D_MODEL = 1024
SEQ = 2048
HEAD_DIM = 64
N_HEADS = 16
N_KV_HEADS = 4
GROUP = N_HEADS // N_KV_HEADS
D_ATTN = N_HEADS * HEAD_DIM
D_KV = N_KV_HEADS * HEAD_DIM
BLOCK = 128
ROT_DIM = HEAD_DIM // 4
ROPE_THETA = 500000.0
ATTN_SCALE = 1.0 / math.sqrt(HEAD_DIM)
NEG_INF = -1e30
D_FF = 2816
EPS = 1e-5
N_IN = 3 * D_MODEL + D_ATTN + 2 * D_KV + 2 * D_MODEL
COL_Q = 3 * D_MODEL
COL_K = COL_Q + D_ATTN
COL_V = COL_K + D_KV
COL_GC = COL_V + D_KV
COL_GA = COL_GC + D_MODEL

ADAM_LR = 0.001
ADAM_B1 = 0.9
ADAM_B2 = 0.999
ADAM_EPS = 1e-08
ADAM_WD = 0.01
ADAM_STEP = 10

N_CHIPS = 4
N_DEV = 8

V7X_VMEM_BYTES = 64 * 1024 * 1024
VMEM_LIMIT = (V7X_VMEM_BYTES * 3) // 4
LANES = 128
MESH = pl.DeviceIdType.MESH


def _params(semantics=None):
    return pltpu.CompilerParams(dimension_semantics=semantics, vmem_limit_bytes=VMEM_LIMIT)


def _sds(shape, dtype):
    return jax.ShapeDtypeStruct(shape, dtype)


HBM_SPEC = pl.BlockSpec(memory_space=pl.ANY)


def _pcall(body, name, grid, in_specs, out_specs, out_shape, operands, scratch=(), semantics=None, comm=None,
           aliases=None, start_after_body=False):
    aliases = dict(aliases or {})
    if comm is None:
        return pl.pallas_call(
            body, name=name, grid=grid, in_specs=in_specs, out_specs=out_specs, out_shape=out_shape,
            scratch_shapes=list(scratch), input_output_aliases=aliases,
            compiler_params=_params(semantics))(*operands)
    multi = isinstance(out_shape, (list, tuple))
    o_specs = list(out_specs) if multi else [out_specs]
    o_shape = list(out_shape) if multi else [out_shape]
    n_in, n_out, n_scr = len(operands), len(o_shape), len(scratch)
    n_cin, n_cout = len(comm.operands), len(comm.out_shape)

    def hosted(*refs):
        ins, cins = refs[:n_in], refs[n_in:n_in + n_cin]
        o0 = n_in + n_cin
        outs, couts = refs[o0:o0 + n_out], refs[o0 + n_out:o0 + n_out + n_cout]
        s0 = o0 + n_out + n_cout
        scr, sems = refs[s0:s0 + n_scr], refs[s0 + n_scr:]
        first = last = None
        for axis, size in enumerate(grid):
            i = pl.program_id(axis)
            first = (i == 0) if first is None else first & (i == 0)
            last = (i == size - 1) if last is None else last & (i == size - 1)

        if not start_after_body:
            @pl.when(first)
            def _():
                comm.start(cins, couts, sems)

        body(*ins, *outs, *scr)

        if start_after_body:
            @pl.when(first)
            def _():
                comm.start(cins, couts, sems)

        @pl.when(last)
        def _():
            comm.finish(cins, couts, sems)

    res = pl.pallas_call(
        hosted, name=name, grid=grid,
        in_specs=list(in_specs) + [HBM_SPEC] * n_cin, out_specs=o_specs + [HBM_SPEC] * n_cout,
        out_shape=o_shape + list(comm.out_shape), scratch_shapes=list(scratch) + list(comm.sems),
        input_output_aliases={**aliases, **{n_in + a: n_out + b for a, b in comm.aliases.items()}},
        compiler_params=_params(("arbitrary",) * len(grid)))(*operands, *comm.operands)
    outs = list(res[:n_out])
    return (outs if multi else outs[0]), list(res[n_out:])


def _comm_call(name, comm):
    def body(*refs):
        n_cin, n_cout = len(comm.operands), len(comm.out_shape)
        cins, couts, sems = refs[:n_cin], refs[n_cin:n_cin + n_cout], refs[n_cin + n_cout:]
        comm.start(cins, couts, sems)
        comm.finish(cins, couts, sems)

    return list(pl.pallas_call(
        body, name=name, in_specs=[HBM_SPEC] * len(comm.operands), out_specs=[HBM_SPEC] * len(comm.out_shape),
        out_shape=list(comm.out_shape), scratch_shapes=list(comm.sems),
        input_output_aliases=dict(comm.aliases))(*comm.operands))


NN = ((1,), (0,))
NT = ((1,), (1,))
TN = ((0,), (0,))


def _matmul(name, a, b, dims, grid, a_spec, b_spec, o_spec, o_shape, o_dtype, res=None, res_spec=None, comm=None):
    nk = grid[2]

    def body(*refs):
        if res is None:
            a_ref, b_ref, o_ref = refs[:3]
            r_ref = None
            scratch = refs[3:]
        else:
            a_ref, b_ref, r_ref, o_ref = refs[:4]
            scratch = refs[4:]
        p = lax.dot_general(a_ref[...], b_ref[...], (dims, ((), ())), preferred_element_type=F32)

        def finish(acc):
            if r_ref is not None:
                acc = r_ref[...] + acc
            o_ref[...] = acc.astype(o_dtype)

        if nk == 1:
            finish(p)
        else:
            acc_ref = o_ref if in_place else scratch[0]
            k = pl.program_id(2)

            @pl.when(k == 0)
            def _():
                acc_ref[...] = p

            @pl.when(k > 0)
            def _():
                acc_ref[...] += p

            if not in_place:
                @pl.when(k == nk - 1)
                def _():
                    finish(acc_ref[...])

    in_place = nk > 1 and res is None and o_dtype == F32
    operands = [a, b] if res is None else [a, b, res]
    in_specs = [a_spec, b_spec] if res is None else [a_spec, b_spec, res_spec]
    scratch = [pltpu.VMEM(o_spec.block_shape, F32)] if nk > 1 and not in_place else []
    return _pcall(body, name, grid, in_specs, o_spec, _sds(o_shape, o_dtype), operands, scratch,
                  ("parallel", "parallel", "arbitrary"), comm)


def _mm_nn(name, a, b, bm, bn, o_dtype, res=None, comm=None):
    m, k = a.shape
    n = b.shape[1]
    return _matmul(
        name, a, b, NN, (m // bm, n // bn, 1),
        pl.BlockSpec((bm, k), lambda i, j, kk: (i, 0)),
        pl.BlockSpec((k, bn), lambda i, j, kk: (0, j)),
        pl.BlockSpec((bm, bn), lambda i, j, kk: (i, j)),
        (m, n), o_dtype, res,
        None if res is None else pl.BlockSpec((bm, bn), lambda i, j, kk: (i, j)), comm,
    )


def _mm_nt(name, a, b, bm, bn, bk, o_dtype, comm=None):
    m, k = a.shape
    n = b.shape[0]
    return _matmul(
        name, a, b, NT, (m // bm, n // bn, k // bk),
        pl.BlockSpec((bm, bk), lambda i, j, kk: (i, kk)),
        pl.BlockSpec((bn, bk), lambda i, j, kk: (j, kk)),
        pl.BlockSpec((bm, bn), lambda i, j, kk: (i, j)),
        (m, n), o_dtype, comm=comm,
    )


def _mm_tn(name, a, b, bm, bn, o_dtype, comm=None):
    k, m = a.shape
    n = b.shape[1]
    return _matmul(
        name, a, b, TN, (m // bm, n // bn, 1),
        pl.BlockSpec((k, bm), lambda i, j, kk: (0, i)),
        pl.BlockSpec((k, bn), lambda i, j, kk: (0, j)),
        pl.BlockSpec((bm, bn), lambda i, j, kk: (i, j)),
        (m, n), o_dtype, comm=comm,
    )


ROWS = 256


def _row_spec(width, col=0):
    return pl.BlockSpec((ROWS, width), lambda i: (i, col))


def _full_spec(shape):
    return pl.BlockSpec(shape, lambda *_: (0,) * len(shape))


def _rms_norm(name, x, g):
    def body(x_ref, g_ref, h_ref):
        xf = x_ref[...]
        r = lax.rsqrt(jnp.mean(xf * xf, axis=-1, keepdims=True) + EPS)
        h_ref[...] = ((xf * r) * g_ref[...]).astype(BF16)

    return pl.pallas_call(
        body, name=name, grid=(SEQ // ROWS,),
        in_specs=[_row_spec(D_MODEL), _full_spec((1, D_MODEL))],
        out_specs=_row_spec(D_MODEL),
        out_shape=_sds((SEQ, D_MODEL), BF16),
        compiler_params=_params(("parallel",)),
    )(x, g)


CONV_COLS = 256


def _shift_rows(u, k):
    rows = lax.broadcasted_iota(jnp.int32, u.shape, 0)
    return jnp.where(rows >= k, pltpu.roll(u, k, axis=0), 0.0)


def _conv_fwd(proj, conv_w):
    nblk = D_MODEL // CONV_COLS

    def body(cb_ref, cc_ref, cx_ref, w_ref, y_ref):
        u = cc_ref[...] * cx_ref[...]
        w = w_ref[...]
        cv = w[0:1, :] * _shift_rows(u, 2) + w[1:2, :] * _shift_rows(u, 1) + w[2:3, :] * u
        y_ref[...] = (cb_ref[...] * cv).astype(BF16)

    def col(part):
        return pl.BlockSpec((SEQ, CONV_COLS), lambda j: (0, part * nblk + j))

    return pl.pallas_call(
        body, name="conv_fwd", grid=(nblk,),
        in_specs=[col(0), col(1), col(2), pl.BlockSpec((3, CONV_COLS), lambda j: (0, j))],
        out_specs=pl.BlockSpec((SEQ, CONV_COLS), lambda j: (0, j)),
        out_shape=_sds((SEQ, D_MODEL), BF16),
        compiler_params=_params(("parallel",)),
    )(proj, proj, proj, conv_w)


ROPE_COLS = 256


def _rope_tables():
    f32 = np.float32
    inv_freq = (f32(ROPE_THETA) ** (-np.arange(0, ROT_DIM, 2, dtype=f32) / f32(ROT_DIM))).astype(f32)
    ang = np.arange(SEQ, dtype=f32)[:, None] * inv_freq[None, :]
    cos, sin = np.cos(ang).astype(f32), np.sin(ang).astype(f32)
    half = ROT_DIM // 2
    ones = np.ones((SEQ, HEAD_DIM - ROT_DIM), f32)
    zeros = np.zeros((SEQ, HEAD_DIM - ROT_DIM), f32)
    zh = np.zeros((SEQ, half), f32)
    c = np.concatenate([cos, cos, ones], axis=1)
    s_up = np.concatenate([-sin, zh, zeros], axis=1)
    s_dn = np.concatenate([zh, sin, zeros], axis=1)
    reps = ROPE_COLS // HEAD_DIM
    return tuple(jnp.asarray(np.tile(t, (1, reps))) for t in (c, s_up, s_dn))


def _rotate(t, c, s_up, s_dn):
    width = t.shape[1]
    half = ROT_DIM // 2
    return t * c + pltpu.roll(t, width - half, axis=1) * s_up + pltpu.roll(t, half, axis=1) * s_dn


N_QBLK = SEQ // BLOCK


def _attn_specs():
    prev = lambda n: jnp.maximum(n - 1, 0)
    q = pl.BlockSpec((BLOCK, D_ATTN), lambda n: (n, COL_Q // D_ATTN))
    k_prev = pl.BlockSpec((BLOCK, D_KV), lambda n: (prev(n), COL_K // D_KV))
    k_cur = pl.BlockSpec((BLOCK, D_KV), lambda n: (n, COL_K // D_KV))
    v_prev = pl.BlockSpec((BLOCK, D_KV), lambda n: (prev(n), COL_V // D_KV))
    v_cur = pl.BlockSpec((BLOCK, D_KV), lambda n: (n, COL_V // D_KV))
    tab_cur = pl.BlockSpec((BLOCK, ROPE_COLS), lambda n: (n, 0))
    tab_prev = pl.BlockSpec((BLOCK, ROPE_COLS), lambda n: (prev(n), 0))
    return [q, k_prev, k_cur, v_prev, v_cur] + [tab_cur] * 3 + [tab_prev] * 3


def _band_kv(kp_ref, kc_ref, vp_ref, vc_ref, tabs_cur, tabs_prev):
    k = jnp.concatenate([_rotate(kp_ref[...], *(t[...] for t in tabs_prev)),
                         _rotate(kc_ref[...], *(t[...] for t in tabs_cur))], axis=0)
    v = jnp.concatenate([vp_ref[...], vc_ref[...]], axis=0)
    return k, v


def _query_tiles(q_ref, tiles, tabs_cur):
    c, su, sd = (t[:, :LANES] for t in tabs_cur)
    return jnp.concatenate(
        [_rotate(q_ref[:, t * LANES:(t + 1) * LANES], c, su, sd).astype(BF16) for t in tiles], axis=0)


def _sink_row(sink_ref, tiles, par):
    return jnp.concatenate([jnp.full((1, BLOCK), sink_ref[0, t * HEADS_PER_TILE + par], F32) for t in tiles], axis=1)


def _band_mask(n):
    kj = lax.broadcasted_iota(jnp.int32, (2 * BLOCK, BLOCK), 0)
    qi = lax.broadcasted_iota(jnp.int32, (2 * BLOCK, BLOCK), 1)
    rel = qi + BLOCK - kj
    return (rel >= 0) & (rel < BLOCK) & ((kj >= BLOCK) | (n > 0))


HEADS_PER_TILE = LANES // HEAD_DIM
TILES_PER_GROUP = GROUP // HEADS_PER_TILE


def _group_mask(n):
    return jnp.concatenate([_band_mask(n)] * TILES_PER_GROUP, axis=1)


def _lane_half(shape, par):
    lane = lax.broadcasted_iota(jnp.int32, shape, 1)
    return (lane < HEAD_DIM) if par == 0 else (lane >= HEAD_DIM)


def _head_tiles(kv, h):
    tile = kv[:, (h // HEADS_PER_TILE) * LANES:(h // HEADS_PER_TILE + 1) * LANES].astype(F32)
    own = jnp.where(_lane_half(tile.shape, h % HEADS_PER_TILE), tile, 0.0)
    other = pltpu.roll(own, HEAD_DIM, axis=1)
    lo, hi = (own, other) if h % HEADS_PER_TILE == 0 else (other, own)
    return lo.astype(BF16), hi.astype(BF16)


def _head_softmax(q_tile, k_half, sink, mask):
    s = lax.dot_general(k_half, q_tile, (NT, ((), ())), preferred_element_type=F32) * ATTN_SCALE
    s = jnp.where(mask, s, NEG_INF)
    m = jnp.maximum(jnp.max(s, axis=0, keepdims=True), sink)
    e = jnp.exp(s - m)
    es = jnp.exp(sink - m)
    inv = 1.0 / (jnp.sum(e, axis=0, keepdims=True) + es)
    return e * inv, es * inv


def _attn_fwd(proj, tables, sinks, comm=None):
    def body(sink_ref, q_ref, kp_ref, kc_ref, vp_ref, vc_ref, c_ref, su_ref, sd_ref, cp_ref, sup_ref, sdp_ref, o_ref):
        n = pl.program_id(0)
        mask = _group_mask(n)
        tabs_cur = (c_ref, su_ref, sd_ref)
        k, v = _band_kv(kp_ref, kc_ref, vp_ref, vc_ref, tabs_cur, (cp_ref, sup_ref, sdp_ref))
        for h in range(N_KV_HEADS):
            k_halves = _head_tiles(k, h)
            v_halves = _head_tiles(v, h)
            tiles = [h * TILES_PER_GROUP + t for t in range(TILES_PER_GROUP)]
            q_rows = _query_tiles(q_ref, tiles, tabs_cur)
            acc = None
            for par in range(HEADS_PER_TILE):
                p, _ = _head_softmax(q_rows, k_halves[par], _sink_row(sink_ref, tiles, par), mask)
                o = lax.dot_general(p.astype(BF16), v_halves[par], (TN, ((), ())), preferred_element_type=F32)
                acc = o if acc is None else acc + o
            for i, tile in enumerate(tiles):
                o_ref[:, tile * LANES:(tile + 1) * LANES] = acc[i * BLOCK:(i + 1) * BLOCK, :].astype(BF16)

    return _pcall(
        body, "attn_fwd", (N_QBLK,),
        [pl.BlockSpec(memory_space=pltpu.SMEM)] + _attn_specs(),
        pl.BlockSpec((BLOCK, D_ATTN), lambda n: (n, 0)),
        _sds((SEQ, D_ATTN), BF16), [sinks] + [proj] * 5 + list(tables) * 2, (), ("parallel",), comm)


def _branch_merge(conv_y, attn, w_co, w_ao, proj, comm=None):
    bm, bn = 1024, 512

    def body(cy_ref, at_ref, wc_ref, wa_ref, gc_ref, ga_ref, co_ref, ao_ref, mg_ref):
        co = jnp.dot(cy_ref[...], wc_ref[...], preferred_element_type=F32)
        ao = jnp.dot(at_ref[...], wa_ref[...], preferred_element_type=F32)
        co_ref[...] = co
        ao_ref[...] = ao
        mg_ref[...] = (jax.nn.sigmoid(gc_ref[...]) * co + jax.nn.sigmoid(ga_ref[...]) * ao).astype(BF16)

    act = pl.BlockSpec((bm, D_MODEL), lambda i, j: (i, 0))
    wgt = pl.BlockSpec((D_MODEL, bn), lambda i, j: (0, j))
    out = pl.BlockSpec((bm, bn), lambda i, j: (i, j))
    return _pcall(
        body, "branch_merge", (SEQ // bm, D_MODEL // bn),
        [act, act, wgt, wgt,
         pl.BlockSpec((bm, bn), lambda i, j: (i, COL_GC // bn + j)),
         pl.BlockSpec((bm, bn), lambda i, j: (i, COL_GA // bn + j))],
        [out, out, out],
        [_sds((SEQ, D_MODEL), F32), _sds((SEQ, D_MODEL), F32), _sds((SEQ, D_MODEL), BF16)],
        [conv_y, attn, w_co, w_ao, proj, proj], (), ("parallel", "parallel"), comm)


def _mm_o_norm(merged, w_o, x, g):
    bm = 1024

    def body(a_ref, w_ref, x_ref, g_ref, o_ref, h_ref):
        x2 = x_ref[...] + jnp.dot(a_ref[...], w_ref[...], preferred_element_type=F32)
        o_ref[...] = x2
        r = lax.rsqrt(jnp.mean(x2 * x2, axis=-1, keepdims=True) + EPS)
        h_ref[...] = ((x2 * r) * g_ref[...]).astype(BF16)

    row = pl.BlockSpec((bm, D_MODEL), lambda i: (i, 0))
    return pl.pallas_call(
        body, name="mm_o", grid=(SEQ // bm,),
        in_specs=[row, _full_spec((D_MODEL, D_MODEL)), row, _full_spec((1, D_MODEL))],
        out_specs=[row, row], out_shape=[_sds((SEQ, D_MODEL), F32), _sds((SEQ, D_MODEL), BF16)],
        compiler_params=_params(("parallel",)),
    )(merged, w_o, x, g)


FF_BM, FF_BN = 512, 1408
FF_NB = D_FF // FF_BN


def _gate_up_fwd(h2, w_gu, comm=None):
    def body(h_ref, wg_ref, wu_ref, g_ref, u_ref, a_ref):
        h = h_ref[...]
        g = jnp.dot(h, wg_ref[...], preferred_element_type=F32)
        u = jnp.dot(h, wu_ref[...], preferred_element_type=F32)
        g_ref[...] = g
        u_ref[...] = u
        a_ref[...] = (jax.nn.silu(g) * u).astype(BF16)

    out = pl.BlockSpec((FF_BM, FF_BN), lambda i, j: (i, j))
    f32, b16 = _sds((SEQ, D_FF), F32), _sds((SEQ, D_FF), BF16)
    return _pcall(
        body, "mm_gate_up", (SEQ // FF_BM, FF_NB),
        [pl.BlockSpec((FF_BM, D_MODEL), lambda i, j: (i, 0)),
         pl.BlockSpec((D_MODEL, FF_BN), lambda i, j: (0, j)),
         pl.BlockSpec((D_MODEL, FF_BN), lambda i, j: (0, FF_NB + j))],
        [out, out, out], [f32, f32, b16], [h2, w_gu, w_gu], (), ("parallel", "parallel"), comm)


def _dact_swiglu(dx3b, w_down, g, u):
    def body(dx_ref, w_ref, g_ref, u_ref, dg_ref, du_ref):
        da = lax.dot_general(dx_ref[...], w_ref[...], (NT, ((), ())), preferred_element_type=F32)
        g = g_ref[...]
        sg = jax.nn.sigmoid(g)
        dg_ref[...] = (da * u_ref[...] * (sg * (1.0 + g * (1.0 - sg)))).astype(BF16)
        du_ref[...] = (da * (g * sg)).astype(BF16)

    blk = pl.BlockSpec((FF_BM, FF_BN), lambda i, j: (i, j))
    b16 = _sds((SEQ, D_FF), BF16)
    return _pcall(
        body, "mm_dact", (SEQ // FF_BM, FF_NB),
        [pl.BlockSpec((FF_BM, D_MODEL), lambda i, j: (i, 0)), pl.BlockSpec((FF_BN, D_MODEL), lambda i, j: (j, 0)),
         blk, blk],
        [blk, blk], [b16, b16], [dx3b, w_down, g, u], (), ("parallel", "parallel"))


def _mm_dh2(dg, du, w_gu, comm=None):
    bm = 1024
    nk = 2 * FF_NB

    def body(dg_ref, du_ref, w_ref, o_ref):
        k = pl.program_id(1)

        def part(a_ref):
            return lax.dot_general(a_ref[...], w_ref[...], (NT, ((), ())), preferred_element_type=F32)

        @pl.when(k == 0)
        def _():
            o_ref[...] = part(dg_ref)

        @pl.when((k > 0) & (k < FF_NB))
        def _():
            o_ref[...] += part(dg_ref)

        @pl.when(k >= FF_NB)
        def _():
            o_ref[...] += part(du_ref)

    return _pcall(
        body, "mm_dh2", (SEQ // bm, nk),
        [pl.BlockSpec((bm, FF_BN), lambda i, k: (i, jnp.minimum(k, FF_NB - 1))),
         pl.BlockSpec((bm, FF_BN), lambda i, k: (i, jnp.maximum(k - FF_NB, 0))),
         pl.BlockSpec((D_MODEL, FF_BN), lambda i, k: (0, k))],
        pl.BlockSpec((bm, D_MODEL), lambda i, k: (i, 0)), _sds((SEQ, D_MODEL), F32),
        [dg, du, w_gu], (), ("parallel", "arbitrary"), comm)


def _mm_dw_gate_up(h2, dg, du, comm=None):
    def body(h_ref, dg_ref, du_ref, o_ref):
        j = pl.program_id(0)

        def part(b_ref):
            return lax.dot_general(h_ref[...], b_ref[...], (TN, ((), ())), preferred_element_type=F32).astype(BF16)

        @pl.when(j < FF_NB)
        def _():
            o_ref[...] = part(dg_ref)

        @pl.when(j >= FF_NB)
        def _():
            o_ref[...] = part(du_ref)

    return _pcall(
        body, "mm_dw_gate_up", (2 * FF_NB,),
        [_full_spec((SEQ, D_MODEL)),
         pl.BlockSpec((SEQ, FF_BN), lambda j: (0, jnp.minimum(j, FF_NB - 1))),
         pl.BlockSpec((SEQ, FF_BN), lambda j: (0, jnp.maximum(j - FF_NB, 0)))],
        pl.BlockSpec((D_MODEL, FF_BN), lambda j: (0, j)),
        _sds((D_MODEL, 2 * D_FF), BF16), [h2, dg, du], (), ("arbitrary",), comm)


def _loss_head(x3, g, target):
    def body(x_ref, g_ref, t_ref, dx_ref, dxb_ref, dg_ref, loss_ref):
        i = pl.program_id(0)
        xf = x_ref[...]
        r = lax.rsqrt(jnp.mean(xf * xf, axis=-1, keepdims=True) + EPS)
        xn = xf * r
        gg = g_ref[...]
        err = xn * gg - t_ref[...]
        part = 0.5 * jnp.sum(jnp.mean(err * err, axis=-1, keepdims=True), axis=0, keepdims=True)
        dy = err * (1.0 / D_MODEL)
        dxn = dy * gg
        dx = r * (dxn - xn * jnp.mean(dxn * xn, axis=-1, keepdims=True))
        dx_ref[...] = dx
        dxb_ref[...] = dx.astype(BF16)
        dg = jnp.sum(dy * xn, axis=0, keepdims=True)
        lane0 = lax.broadcasted_iota(jnp.int32, (1, LANES), 1) == 0
        lpart = jnp.where(lane0, part, 0.0)

        @pl.when(i == 0)
        def _():
            dg_ref[...] = dg
            loss_ref[...] = lpart

        @pl.when(i > 0)
        def _():
            dg_ref[...] += dg
            loss_ref[...] += lpart

    return pl.pallas_call(
        body, name="loss_head", grid=(SEQ // ROWS,),
        in_specs=[_row_spec(D_MODEL), _full_spec((1, D_MODEL)), _row_spec(D_MODEL)],
        out_specs=[_row_spec(D_MODEL), _row_spec(D_MODEL), _full_spec((1, D_MODEL)), _full_spec((1, LANES))],
        out_shape=[_sds((SEQ, D_MODEL), F32), _sds((SEQ, D_MODEL), BF16),
                   _sds((1, D_MODEL), F32), _sds((1, LANES), F32)],
        compiler_params=_params(("arbitrary",)),
    )(x3, g, target)


def _rms_norm_bwd(name, dh, x, g, dres, with_bf16, comm=None):
    def body(dh_ref, x_ref, g_ref, dr_ref, *outs):
        i = pl.program_id(0)
        dx_ref = outs[0]
        dg_ref = outs[-1]
        xf = x_ref[...]
        r = lax.rsqrt(jnp.mean(xf * xf, axis=-1, keepdims=True) + EPS)
        xn = xf * r
        dh = dh_ref[...]
        dxn = dh * g_ref[...]
        dx = dr_ref[...] + r * (dxn - xn * jnp.mean(dxn * xn, axis=-1, keepdims=True))
        dx_ref[...] = dx
        if with_bf16:
            outs[1][...] = dx.astype(BF16)
        dg = jnp.sum(dh * xn, axis=0, keepdims=True)

        @pl.when(i == 0)
        def _():
            dg_ref[...] = dg

        @pl.when(i > 0)
        def _():
            dg_ref[...] += dg

    row = _row_spec(D_MODEL)
    out_specs = [row] + ([row] if with_bf16 else []) + [_full_spec((1, D_MODEL))]
    out_shape = ([_sds((SEQ, D_MODEL), F32)] + ([_sds((SEQ, D_MODEL), BF16)] if with_bf16 else [])
                 + [_sds((1, D_MODEL), F32)])
    return _pcall(body, name, (SEQ // ROWS,), [row, row, _full_spec((1, D_MODEL)), row], out_specs, out_shape,
                  [dh, x, g, dres], (), ("arbitrary",), comm)


def _merge_bwd(dx2b, w_o, conv_out, attn_out, proj):
    bm, bn = 1024, D_MODEL // 2

    def body(dx_ref, w_ref, co_ref, ao_ref, gc_ref, ga_ref, dco_ref, dao_ref, dgc_ref, dga_ref):
        dm = lax.dot_general(dx_ref[...], w_ref[...], (NT, ((), ())), preferred_element_type=F32)
        sc = jax.nn.sigmoid(gc_ref[...])
        sa = jax.nn.sigmoid(ga_ref[...])
        dco_ref[...] = (dm * sc).astype(BF16)
        dao_ref[...] = (dm * sa).astype(BF16)
        dgc_ref[...] = (dm * co_ref[...] * (sc * (1.0 - sc))).astype(BF16)
        dga_ref[...] = (dm * ao_ref[...] * (sa * (1.0 - sa))).astype(BF16)

    own = pl.BlockSpec((bm, bn), lambda i, j: (i, j))
    sd = _sds((SEQ, D_MODEL), BF16)
    return pl.pallas_call(
        body, name="mm_dmerged", grid=(SEQ // bm, D_MODEL // bn),
        in_specs=[pl.BlockSpec((bm, D_MODEL), lambda i, j: (i, 0)), pl.BlockSpec((bn, D_MODEL), lambda i, j: (j, 0)),
                  own, own,
                  pl.BlockSpec((bm, bn), lambda i, j: (i, COL_GC // bn + j)),
                  pl.BlockSpec((bm, bn), lambda i, j: (i, COL_GA // bn + j))],
        out_specs=[own, own, own, own], out_shape=[sd, sd, sd, sd],
        compiler_params=_params(("parallel", "parallel")),
    )(dx2b, w_o, conv_out, attn_out, proj, proj)


def _conv_bwd(dconv_y, proj, conv_w, comm=None):
    nblk = D_MODEL // CONV_COLS

    def body(dy_ref, cb_ref, cc_ref, cx_ref, w_ref, dcb_ref, dcc_ref, dcx_ref, dw_ref):
        cc = cc_ref[...]
        cx = cx_ref[...]
        u = cc * cx
        w = w_ref[...]
        u1 = _shift_rows(u, 1)
        u2 = _shift_rows(u, 2)
        cv = w[0:1, :] * u2 + w[1:2, :] * u1 + w[2:3, :] * u
        dy = dy_ref[...]
        dcb_ref[...] = (dy * cv).astype(BF16)
        dcv = dy * cb_ref[...]
        rows = lax.broadcasted_iota(jnp.int32, dcv.shape, 0)
        up1 = jnp.where(rows < SEQ - 1, pltpu.roll(dcv, SEQ - 1, axis=0), 0.0)
        up2 = jnp.where(rows < SEQ - 2, pltpu.roll(dcv, SEQ - 2, axis=0), 0.0)
        du = w[2:3, :] * dcv + w[1:2, :] * up1 + w[0:1, :] * up2
        dcc_ref[...] = (du * cx).astype(BF16)
        dcx_ref[...] = (du * cc).astype(BF16)
        dw_ref[...] = jnp.concatenate(
            [jnp.sum(dcv * u2, axis=0, keepdims=True),
             jnp.sum(dcv * u1, axis=0, keepdims=True),
             jnp.sum(dcv * u, axis=0, keepdims=True)], axis=0)

    def col(part):
        return pl.BlockSpec((SEQ, CONV_COLS), lambda j: (0, part * nblk + j))

    own = pl.BlockSpec((SEQ, CONV_COLS), lambda j: (0, j))
    wsp = pl.BlockSpec((3, CONV_COLS), lambda j: (0, j))
    sd = _sds((SEQ, D_MODEL), BF16)
    return _pcall(
        body, "conv_bwd", (nblk,), [own, col(0), col(1), col(2), wsp], [own, own, own, wsp],
        [sd, sd, sd, _sds((3, D_MODEL), F32)], [dconv_y, proj, proj, proj, conv_w], (), ("parallel",), comm)


def _attn_bwd(proj, dattn, sinks, tables, comm=None):
    def body(sink_ref, q_ref, kp_ref, kc_ref, vp_ref, vc_ref, c_ref, su_ref, sd_ref, cp_ref, sup_ref, sdp_ref,
             do_ref, dq_ref, dkp_ref, dkc_ref, dvp_ref, dvc_ref, ds_ref):
        n = pl.program_id(0)
        mask = _group_mask(n)
        tabs_cur = (c_ref, su_ref, sd_ref)
        k, v = _band_kv(kp_ref, kc_ref, vp_ref, vc_ref, tabs_cur, (cp_ref, sup_ref, sdp_ref))
        lane = lax.broadcasted_iota(jnp.int32, (1, LANES), 1)
        dsink = jnp.zeros((1, LANES), F32)
        c, su, sd = c_ref[:, :LANES], su_ref[:, :LANES], sd_ref[:, :LANES]
        dk_tiles = [None] * (N_KV_HEADS // HEADS_PER_TILE)
        dv_tiles = [None] * (N_KV_HEADS // HEADS_PER_TILE)
        for h in range(N_KV_HEADS):
            k_halves = _head_tiles(k, h)
            v_halves = _head_tiles(v, h)
            tiles = [h * TILES_PER_GROUP + t for t in range(TILES_PER_GROUP)]
            q_rows = _query_tiles(q_ref, tiles, tabs_cur)
            do_rows = jnp.concatenate([do_ref[:, t * LANES:(t + 1) * LANES] for t in tiles], axis=0)
            dk_par, dv_par = [], []
            dq_rows = None
            for par in range(HEADS_PER_TILE):
                p, p_sink = _head_softmax(q_rows, k_halves[par], _sink_row(sink_ref, tiles, par), mask)
                dp = lax.dot_general(v_halves[par], do_rows, (NT, ((), ())), preferred_element_type=F32)
                delta = jnp.sum(p * dp, axis=0, keepdims=True)
                ds = (p * (dp - delta) * ATTN_SCALE).astype(BF16)
                dq = lax.dot_general(ds, k_halves[par], (TN, ((), ())), preferred_element_type=F32)
                dq_rows = dq if dq_rows is None else dq_rows + dq
                dk_par.append(jnp.dot(ds, q_rows, preferred_element_type=F32))
                dv_par.append(jnp.dot(p.astype(BF16), do_rows, preferred_element_type=F32))
                sink_grad = p_sink * delta
                for i, tile in enumerate(tiles):
                    val = -jnp.sum(sink_grad[:, i * BLOCK:(i + 1) * BLOCK], axis=1, keepdims=True)
                    dsink = dsink + jnp.where(lane == tile * HEADS_PER_TILE + par, val, 0.0)
            for i, tile in enumerate(tiles):
                dq_tile = dq_rows[i * BLOCK:(i + 1) * BLOCK, :]
                dq_ref[:, tile * LANES:(tile + 1) * LANES] = _rotate(dq_tile, c, -su, -sd).astype(BF16)
            own = h % HEADS_PER_TILE
            for par_grads, tiles in ((dk_par, dk_tiles), (dv_par, dv_tiles)):
                shifted = pltpu.roll(par_grads[1 - own], HEAD_DIM, axis=1)
                total = jnp.where(_lane_half(shifted.shape, own), par_grads[own] + shifted, 0.0)
                i = h // HEADS_PER_TILE
                tiles[i] = total if tiles[i] is None else tiles[i] + total
        for i in range(N_KV_HEADS // HEADS_PER_TILE):
            cols = slice(i * LANES, (i + 1) * LANES)
            dkp_ref[:, cols] = dk_tiles[i][:BLOCK, :]
            dkc_ref[:, cols] = dk_tiles[i][BLOCK:, :]
            dvp_ref[:, cols] = dv_tiles[i][:BLOCK, :]
            dvc_ref[:, cols] = dv_tiles[i][BLOCK:, :]

        @pl.when(n == 0)
        def _():
            ds_ref[...] = dsink

        @pl.when(n > 0)
        def _():
            ds_ref[...] += dsink

    blk = pl.BlockSpec((BLOCK, D_KV), lambda n: (n, 0))
    prev_blk = pl.BlockSpec((BLOCK, D_KV), lambda n: ((n + N_QBLK - 1) % N_QBLK, 0))
    kv = _sds((SEQ, D_KV), F32)
    return _pcall(
        body, "attn_bwd", (N_QBLK,),
        [pl.BlockSpec(memory_space=pltpu.SMEM)] + _attn_specs() + [pl.BlockSpec((BLOCK, D_ATTN), lambda n: (n, 0))],
        [pl.BlockSpec((BLOCK, D_ATTN), lambda n: (n, 0)), prev_blk, blk, prev_blk, blk, _full_spec((1, LANES))],
        [_sds((SEQ, D_ATTN), BF16), kv, kv, kv, kv, _sds((1, LANES), F32)],
        [sinks] + [proj] * 5 + list(tables) * 2 + [dattn], (), ("arbitrary",), comm)


def _kv_grad_combine(dk_prev, dk_cur, dv_prev, dv_cur, tables):
    rows = 4 * BLOCK

    def body(kp_ref, kc_ref, vp_ref, vc_ref, c_ref, su_ref, sd_ref, o_ref):
        dk = kc_ref[...] + kp_ref[...]
        dv = vc_ref[...] + vp_ref[...]
        o_ref[:, :D_KV] = _rotate(dk, c_ref[...], -su_ref[...], -sd_ref[...]).astype(BF16)
        o_ref[:, D_KV:] = dv.astype(BF16)

    blk = pl.BlockSpec((rows, D_KV), lambda m: (m, 0))
    return pl.pallas_call(
        body, name="kv_grad_combine", grid=(SEQ // rows,),
        in_specs=[blk] * 7,
        out_specs=pl.BlockSpec((rows, 2 * D_KV), lambda m: (m, 0)),
        out_shape=_sds((SEQ, 2 * D_KV), BF16),
        compiler_params=_params(("parallel",)),
    )(dk_prev, dk_cur, dv_prev, dv_cur, *tables)


MATRICES = {
    "w_in": (D_MODEL, N_IN // N_CHIPS, "col"),
    "w_conv_out": (D_MODEL // N_CHIPS, D_MODEL, "row"),
    "w_attn_out": (D_MODEL // N_CHIPS, D_MODEL, "row"),
    "w_o": (D_MODEL // N_CHIPS, D_MODEL, "row"),
    "w_gate_up": (D_MODEL, 2 * D_FF // N_CHIPS, "col"),
    "w_down": (D_FF // N_CHIPS, D_MODEL, "row"),
}
BF16_ROW_TILE = 16
CONV_W_COLS = D_MODEL // N_CHIPS
SMALL_ROWS = 8


def _whole_shape(spec):
    rows, cols, kind = spec
    return (rows, cols * N_CHIPS) if kind == "col" else (rows * N_CHIPS, cols)


def _half_shape(spec):
    return (spec[0] // 2, spec[1])


def _aligned(start, multiple):
    return start if isinstance(start, int) else pl.multiple_of(start, multiple)


def _region(ref, spec, shard, half, part=0, parts=1):
    rows, cols, kind = spec
    hr = rows // 2
    n = hr // parts
    if kind == "col":
        return ref.at[pl.ds(_aligned(half * hr + part * n, BF16_ROW_TILE), n),
                      pl.ds(_aligned(shard * cols, LANES), cols)]
    return ref.at[pl.ds(_aligned(shard * rows + half * hr + part * n, BF16_ROW_TILE), n), :]


def _position():
    x, y, c = lax.axis_index("x"), lax.axis_index("y"), lax.axis_index("c")
    chips = [(1 - x, y), (x, 1 - y), (1 - x, 1 - y)]
    return x, y, c, chips


def _shard_of(chip):
    return 2 * chip[0] + chip[1]


def _remote(src, dst, send_sem, recv_sem, to):
    return pltpu.make_async_remote_copy(src_ref=src, dst_ref=dst, send_sem=send_sem, recv_sem=recv_sem,
                                        device_id=to, device_id_type=MESH)


CAST_STEPS = 4


def _to_bf16_in_whole(ws, specs, shard):
    n = len(ws)

    def body(s_ref, *refs):
        del s_ref
        for w_ref, o_ref in zip(refs[:n], refs[n:]):
            o_ref[...] = w_ref[...].astype(BF16)

    def out_spec(spec):
        rows = spec[0] // CAST_STEPS
        if spec[2] == "col":
            return pl.BlockSpec((rows, spec[1]), lambda i, s_ref: (i, s_ref[0]))
        return pl.BlockSpec((rows, spec[1]), lambda i, s_ref: (s_ref[0] * CAST_STEPS + i, 0))

    grid_spec = pltpu.PrefetchScalarGridSpec(
        num_scalar_prefetch=1, grid=(CAST_STEPS,),
        in_specs=[pl.BlockSpec((s[0] // CAST_STEPS, s[1]), lambda i, s_ref: (i, 0)) for s in specs],
        out_specs=[out_spec(s) for s in specs])
    return list(pl.pallas_call(
        body, name="cast_shards", grid_spec=grid_spec, out_shape=[_sds(_whole_shape(s), BF16) for s in specs],
        compiler_params=_params(("parallel",)),
    )(shard, *ws))


class _Gather:
    def __init__(self, wholes, pieces, conv_w=None):
        self.pieces = pieces
        self.n = len(wholes)
        self.with_conv_w = conv_w is not None
        self.operands = list(wholes) + ([conv_w] if self.with_conv_w else [])
        self.out_shape = [_sds(w.shape, w.dtype) for w in wholes]
        if self.with_conv_w:
            self.out_shape.append(_sds((3, D_MODEL), F32))
        self.aliases = {i: i for i in range(self.n)}
        n_ici = 3 * len(pieces)
        self.sems = [pltpu.SemaphoreType.DMA((n_ici,))] * 4
        if self.with_conv_w:
            self.sems += [pltpu.SemaphoreType.DMA((1,)), pltpu.SemaphoreType.DMA((3,)), pltpu.SemaphoreType.DMA((3,))]

    def _conv_w(self, cins, couts, sems, with_recvs):
        cw_in, cw_out = cins[self.n], couts[self.n]
        x, y, c, chips = _position()

        def cols(shard):
            return cw_out.at[:, pl.ds(_aligned(shard * CONV_W_COLS, LANES), CONV_W_COLS)]

        me = _shard_of((x, y))
        local = pltpu.make_async_copy(cw_in, cols(me), sems[4].at[0])
        sends = [_remote(cw_in, cols(me), sems[5].at[j], sems[6].at[j], (*chip, c)) for j, chip in enumerate(chips)]
        if not with_recvs:
            return local, sends, []
        recvs = [_remote(cols(_shard_of(chip)), cols(_shard_of(chip)), sems[5].at[j], sems[6].at[j], (*chip, c))
                 for j, chip in enumerate(chips)]
        return local, sends, recvs

    def start(self, cins, couts, sems):
        x, y, c, chips = _position()
        me = _shard_of((x, y))
        if self.with_conv_w:
            local, sends, _ = self._conv_w(cins, couts, sems, False)
            local.start()
            for cp in sends:
                cp.start()
        for p, (i, spec, part, parts) in enumerate(self.pieces):
            mine = _region(couts[i], spec, me, c, part, parts)
            for j, chip in enumerate(chips):
                _remote(mine, mine, sems[0].at[3 * p + j], sems[1].at[3 * p + j], (*chip, c)).start()

    def finish(self, cins, couts, sems):
        x, y, c, chips = _position()
        me = _shard_of((x, y))
        sibling = (x, y, 1 - c)
        send_a, recv_a, send_b, recv_b = sems[:4]
        passed = []
        for p, (i, spec, part, parts) in enumerate(self.pieces):
            for j, chip in enumerate(chips):
                k = 3 * p + j
                landed = _region(couts[i], spec, _shard_of(chip), c, part, parts)
                _remote(landed, landed, send_a.at[k], recv_a.at[k], (*chip, c)).wait_recv()
                cp = _remote(landed, landed, send_b.at[k], recv_b.at[k], sibling)
                cp.start()
                passed.append(cp)
        for p, (i, spec, part, parts) in enumerate(self.pieces):
            mine = _region(couts[i], spec, me, c, part, parts)
            for j, chip in enumerate(chips):
                k = 3 * p + j
                other = _region(couts[i], spec, _shard_of(chip), 1 - c, part, parts)
                _remote(other, other, send_b.at[k], recv_b.at[k], sibling).wait_recv()
                _remote(mine, mine, send_a.at[k], recv_a.at[k], (*chip, c)).wait_send()
        for cp in passed:
            cp.wait_send()
        if self.with_conv_w:
            local, sends, recvs = self._conv_w(cins, couts, sems, True)
            for cp in recvs:
                cp.wait_recv()
            for cp in sends:
                cp.wait_send()
            local.wait()


def _mm_in_gather(h1, w_whole, comm):
    spec = MATRICES["w_in"]
    cols = spec[1]
    bm = SEQ // 2

    def body(h_ref, w_in_ref, proj_ref, w_ref, wbuf, obuf, send_a, recv_a, send_b, recv_b, load_sem, store_sems):
        del w_in_ref
        s, mi = pl.program_id(0), pl.program_id(1)
        x, y, c, chips = _position()
        me = _shard_of((x, y))
        sibling = (x, y, 1 - c)
        mine = _region(w_ref, spec, me, c)

        @pl.when((s == 0) & (mi == 0))
        def _():
            for j, chip in enumerate(chips):
                _remote(mine, mine, send_a.at[j], recv_a.at[j], (*chip, c)).start()

        shard = me
        for j, chip in enumerate(chips):
            shard = jnp.where(s == j + 1, _shard_of(chip), shard)

            @pl.when((s == j + 1) & (mi == 0))
            def _():
                landed = _region(w_ref, spec, _shard_of(chip), c)
                _remote(landed, landed, send_a.at[j], recv_a.at[j], (*chip, c)).wait_recv()
                _remote(landed, landed, send_b.at[j], recv_b.at[j], sibling).start()
                other = _region(w_ref, spec, _shard_of(chip), 1 - c)
                _remote(other, other, send_b.at[j], recv_b.at[j], sibling).wait_recv()

        col0 = pl.multiple_of(shard * cols, LANES)

        @pl.when(mi == 0)
        def _():
            load = pltpu.make_async_copy(w_ref.at[:, pl.ds(col0, cols)], wbuf, load_sem.at[0])
            load.start()
            load.wait()

        def store():
            rows = pl.ds(pl.multiple_of(mi * bm, bm), bm)
            return pltpu.make_async_copy(obuf.at[mi], proj_ref.at[rows, pl.ds(col0, cols)], store_sems.at[mi])

        @pl.when(s > 0)
        def _():
            store().wait()

        obuf[mi] = jnp.dot(h_ref[...], wbuf[...], preferred_element_type=F32)
        store().start()

        @pl.when(s == N_CHIPS - 1)
        def _():
            store().wait()

        @pl.when((s == N_CHIPS - 1) & (mi == 1))
        def _():
            for j, chip in enumerate(chips):
                landed = _region(w_ref, spec, _shard_of(chip), c)
                _remote(mine, mine, send_a.at[j], recv_a.at[j], (*chip, c)).wait_send()
                _remote(landed, landed, send_b.at[j], recv_b.at[j], sibling).wait_send()

    sem3 = pltpu.SemaphoreType.DMA((3,))
    (proj, whole), extra = _pcall(
        body, "mm_in", (N_CHIPS, SEQ // bm),
        [pl.BlockSpec((bm, D_MODEL), lambda s, m: (m, 0)), HBM_SPEC], [HBM_SPEC, HBM_SPEC],
        [_sds((SEQ, N_IN), F32), _sds(w_whole.shape, w_whole.dtype)], [h1, w_whole],
        [pltpu.VMEM((D_MODEL, cols), BF16), pltpu.VMEM((SEQ // bm, bm, cols), F32), sem3, sem3, sem3, sem3,
         pltpu.SemaphoreType.DMA((1,)), pltpu.SemaphoreType.DMA((SEQ // bm,))],
        None, comm, aliases={1: 1}, start_after_body=True)
    return proj, whole, extra


def _mm_dw_in_pair(h1, dproj, comm):
    spec = MATRICES["w_in"]
    rows, cols, _ = spec
    hr = rows // 2

    def body(h_ref, dp_ref, dw_ref, got_ref, obuf, store_sems, send_sems, recv_sems):
        t = pl.program_id(0)
        x, y, c, _ = _position()
        sibling = (x, y, 1 - c)

        def store(step):
            return pltpu.make_async_copy(obuf.at[step % 2], dw_ref.at[:, pl.ds(step * cols, cols)],
                                         store_sems.at[step % 2])

        def send(step):
            theirs = obuf.at[step % 2, pl.ds(_aligned((1 - c) * hr, BF16_ROW_TILE), hr), :]
            return _remote(theirs, got_ref.at[step], send_sems.at[step], recv_sems.at[step], sibling)

        for step in range(N_CHIPS):
            @pl.when(t == step)
            def _():
                if step >= 2:
                    store(step - 2).wait()
                    send(step - 2).wait_send()
                obuf[step % 2] = lax.dot_general(
                    h_ref[...], dp_ref[...], (TN, ((), ())), preferred_element_type=F32).astype(BF16)
                store(step).start()
                send(step).start()

        @pl.when(t == N_CHIPS - 1)
        def _():
            for step in (N_CHIPS - 2, N_CHIPS - 1):
                store(step).wait()
                send(step).wait_send()
            for step in range(N_CHIPS):
                send(step).wait_recv()

    sem4 = pltpu.SemaphoreType.DMA((N_CHIPS,))
    (dw_in, got), extra = _pcall(
        body, "mm_dw_in", (N_CHIPS,),
        [_full_spec((SEQ, D_MODEL)), pl.BlockSpec((SEQ, cols), lambda t: (0, t))], [HBM_SPEC, HBM_SPEC],
        [_sds(_whole_shape(spec), BF16), _sds((N_CHIPS, hr, cols), BF16)], [h1, dproj],
        [pltpu.VMEM((2, rows, cols), BF16), pltpu.SemaphoreType.DMA((2,)), sem4, sem4], None, comm)
    return dw_in, got, extra


def _pack_small(dg_mix, dg_ffn, dg_final, dconv_w, dsinks, loss_row):
    def body(a_ref, b_ref, c_ref, w_ref, s_ref, l_ref, o_ref):
        pad = jnp.zeros((1, D_MODEL - LANES), F32)
        o_ref[0:1, :] = a_ref[...]
        o_ref[1:2, :] = b_ref[...]
        o_ref[2:3, :] = c_ref[...]
        o_ref[3:6, :] = w_ref[...]
        o_ref[6:7, :] = jnp.concatenate([s_ref[...], pad], axis=1)
        o_ref[7:8, :] = jnp.concatenate([l_ref[...], pad], axis=1)

    return pl.pallas_call(
        body, name="pack_small", out_shape=_sds((SMALL_ROWS, D_MODEL), F32),
        compiler_params=_params(),
    )(dg_mix, dg_ffn, dg_final, dconv_w, dsinks, loss_row)


class _Pair:
    def __init__(self, dws, specs):
        self.specs = specs
        self.operands = list(dws)
        self.out_shape = [_sds((N_CHIPS, *_half_shape(s)), BF16) for s in specs]
        self.aliases = {}
        n = N_CHIPS * len(specs)
        self.sems = [pltpu.SemaphoreType.DMA((n,)), pltpu.SemaphoreType.DMA((n,))]

    def _copies(self, cins, couts, sems):
        x, y, c, _ = _position()
        sibling = (x, y, 1 - c)
        for i, spec in enumerate(self.specs):
            for t in range(N_CHIPS):
                k = N_CHIPS * i + t
                yield _remote(_region(cins[i], spec, t, 1 - c), couts[i].at[t], sems[0].at[k], sems[1].at[k], sibling)

    def start(self, cins, couts, sems):
        for cp in self._copies(cins, couts, sems):
            cp.start()

    def finish(self, cins, couts, sems):
        for cp in self._copies(cins, couts, sems):
            cp.wait()


class _SmallAllToAll:
    def __init__(self, small):
        self.operands = [small]
        self.out_shape = [_sds((N_DEV, SMALL_ROWS, D_MODEL), F32)]
        self.aliases = {}
        self.sems = [pltpu.SemaphoreType.DMA((N_DEV - 1,)), pltpu.SemaphoreType.DMA((N_DEV - 1,)),
                     pltpu.SemaphoreType.DMA((1,))]

    def _copies(self, cins, couts, sems):
        x, y, c, _ = _position()
        me = 4 * x + 2 * y + c
        out = []
        for r in range(1, N_DEV):
            flip = ((r >> 2) & 1, (r >> 1) & 1, r & 1)
            peer = tuple(1 - p if f else p for p, f in zip((x, y, c), flip))
            theirs = couts[0].at[4 * peer[0] + 2 * peer[1] + peer[2]]
            out.append((_remote(cins[0], couts[0].at[me], sems[0].at[r - 1], sems[1].at[r - 1], peer),
                        functools.partial(_remote, theirs, theirs, sems[0].at[r - 1], sems[1].at[r - 1], peer)))
        return pltpu.make_async_copy(cins[0], couts[0].at[me], sems[2].at[0]), out

    def start(self, cins, couts, sems):
        own, copies = self._copies(cins, couts, sems)
        own.start()
        for send, _ in copies:
            send.start()

    def finish(self, cins, couts, sems):
        own, copies = self._copies(cins, couts, sems)
        for send, recv in copies:
            recv().wait_recv()
            send.wait_send()
        own.wait()


class _Both:
    def __init__(self, a, b):
        self.a, self.b = a, b
        self.operands = list(a.operands) + list(b.operands)
        self.out_shape = list(a.out_shape) + list(b.out_shape)
        self.aliases = dict(a.aliases)
        self.aliases.update({len(a.operands) + k: len(a.out_shape) + v for k, v in b.aliases.items()})
        self.sems = list(a.sems) + list(b.sems)

    def _split(self, cins, couts, sems):
        na, ma, sa = len(self.a.operands), len(self.a.out_shape), len(self.a.sems)
        return (cins[:na], couts[:ma], sems[:sa]), (cins[na:], couts[ma:], sems[sa:])

    def start(self, cins, couts, sems):
        for plan, args in zip((self.a, self.b), self._split(cins, couts, sems)):
            plan.start(*args)

    def finish(self, cins, couts, sems):
        for plan, args in zip((self.a, self.b), self._split(cins, couts, sems)):
            plan.finish(*args)


def _pair_sum(name, specs, dws, got, place):
    n_mat = len(specs)

    def body(p_ref, *refs):
        t = pl.program_id(0)
        mine, theirs = refs[:n_mat], refs[n_mat:2 * n_mat]
        outs, owns = refs[2 * n_mat:3 * n_mat], refs[3 * n_mat:]
        for a, b, o, own in zip(mine, theirs, outs, owns):
            s = (a[...].astype(F32) + b[...].astype(F32)).astype(BF16)
            o[...] = s

            @pl.when(t == p_ref[1])
            def _():
                own[...] = s

    def mine_spec(spec):
        hr, cols = _half_shape(spec)
        if spec[2] == "col":
            return pl.BlockSpec((hr, cols), lambda t, p_ref: (p_ref[0], t))
        return pl.BlockSpec((hr, cols), lambda t, p_ref: (2 * t + p_ref[0], 0))

    def slot_spec(spec):
        return pl.BlockSpec((None, *_half_shape(spec)), lambda t, p_ref: (t, 0, 0))

    def own_spec(spec):
        return pl.BlockSpec((None, *_half_shape(spec)), lambda t, p_ref: (p_ref[1], 0, 0))

    slots = [_sds((N_CHIPS, *_half_shape(s)), BF16) for s in specs]
    grid_spec = pltpu.PrefetchScalarGridSpec(
        num_scalar_prefetch=1, grid=(N_CHIPS,),
        in_specs=[mine_spec(s) for s in specs] + [slot_spec(s) for s in specs],
        out_specs=[slot_spec(s) for s in specs] + [own_spec(s) for s in specs])
    res = pl.pallas_call(
        body, name=name, grid_spec=grid_spec, out_shape=slots + slots,
        compiler_params=_params(("arbitrary",)),
    )(place, *dws, *got)
    return list(res[:n_mat]), list(res[n_mat:])


class _ChipExchange:
    def __init__(self, sums, slots, part=0, parts=1):
        self.n = len(sums)
        self.part, self.parts = part, parts
        self.operands = list(sums) + list(slots)
        self.out_shape = [_sds(s.shape, s.dtype) for s in slots]
        self.aliases = {self.n + i: i for i in range(self.n)}
        self.sems = [pltpu.SemaphoreType.DMA((3 * self.n,)), pltpu.SemaphoreType.DMA((3 * self.n,))]

    def _rows(self, ref, slot):
        n = ref.shape[1] // self.parts
        return ref.at[slot, pl.ds(self.part * n, n), :]

    def _copies(self, cins, couts, sems):
        x, y, c, chips = _position()
        me = _shard_of((x, y))
        for i in range(self.n):
            for j, chip in enumerate(chips):
                k = 3 * i + j
                theirs = self._rows(couts[i], _shard_of(chip))
                yield (_remote(self._rows(cins[i], _shard_of(chip)), self._rows(couts[i], me),
                               sems[0].at[k], sems[1].at[k], (*chip, c)),
                       functools.partial(_remote, theirs, theirs, sems[0].at[k], sems[1].at[k], (*chip, c)))

    def start(self, cins, couts, sems):
        for send, _ in self._copies(cins, couts, sems):
            send.start()

    def finish(self, cins, couts, sems):
        for send, recv in self._copies(cins, couts, sems):
            recv().wait_recv()
            send.wait_send()


def _chip_sum(name, specs, slots, core):
    steps = 2
    n_mat = len(specs)

    def body(c_ref, *refs):
        del c_ref
        ins, outs = refs[:n_mat], refs[n_mat:]
        for a, o in zip(ins, outs):
            acc = a[0].astype(F32)
            for t in range(1, N_CHIPS):
                acc = acc + a[t].astype(F32)
            o[...] = acc

    def in_spec(spec):
        hr, cols = _half_shape(spec)
        return pl.BlockSpec((N_CHIPS, hr // steps, cols), lambda i, c_ref: (0, i, 0))

    def out_spec(spec):
        hr, cols = _half_shape(spec)
        return pl.BlockSpec((hr // steps, cols), lambda i, c_ref: (c_ref[0] * steps + i, 0))

    grid_spec = pltpu.PrefetchScalarGridSpec(
        num_scalar_prefetch=1, grid=(steps,),
        in_specs=[in_spec(s) for s in specs], out_specs=[out_spec(s) for s in specs])
    return list(pl.pallas_call(
        body, name=name, grid_spec=grid_spec,
        out_shape=[_sds((s[0], s[1]), F32) for s in specs],
        compiler_params=_params(("parallel",)),
    )(core, *slots))


class _HalfExchange:
    def __init__(self, grads, specs):
        self.specs = specs
        self.operands = list(grads)
        self.out_shape = [_sds(g.shape, g.dtype) for g in grads]
        self.aliases = {i: i for i in range(len(grads))}
        self.sems = [pltpu.SemaphoreType.DMA((len(grads),)), pltpu.SemaphoreType.DMA((len(grads),))]

    def _copies(self, couts, sems):
        x, y, c, _ = _position()
        sibling = (x, y, 1 - c)
        for i, spec in enumerate(self.specs):
            hr = spec[0] // 2
            mine = couts[i].at[pl.ds(_aligned(c * hr, 8), hr), :]
            theirs = couts[i].at[pl.ds(_aligned((1 - c) * hr, 8), hr), :]
            yield (_remote(mine, mine, sems[0].at[i], sems[1].at[i], sibling),
                   functools.partial(_remote, theirs, theirs, sems[0].at[i], sems[1].at[i], sibling))

    def start(self, cins, couts, sems):
        for send, _ in self._copies(couts, sems):
            send.start()

    def finish(self, cins, couts, sems):
        for send, recv in self._copies(couts, sems):
            recv().wait_recv()
            send.wait_send()


def _small_sum(blocks):
    def body(b_ref, o_ref):
        acc = b_ref[0]
        for d in range(1, N_DEV):
            acc = acc + b_ref[d]
        o_ref[...] = acc

    return pl.pallas_call(
        body, name="small_sum", out_shape=_sds((SMALL_ROWS, D_MODEL), F32), compiler_params=_params(),
    )(blocks)


def _adamw(name, params, steps):
    n = len(params)

    def body(*refs):
        for p in range(n):
            w_ref, g_ref, m_ref, v_ref = refs[4 * p:4 * p + 4]
            d_ref, nm_ref, nv_ref, go_ref = refs[4 * n + 4 * p:4 * n + 4 * p + 4]
            g = g_ref[...]
            go_ref[...] = g
            m = ADAM_B1 * m_ref[...] + (1.0 - ADAM_B1) * g
            v = ADAM_B2 * v_ref[...] + (1.0 - ADAM_B2) * jnp.square(g)
            m_hat = m / (1.0 - ADAM_B1 ** ADAM_STEP)
            v_hat = v / (1.0 - ADAM_B2 ** ADAM_STEP)
            d_ref[...] = -ADAM_LR * (m_hat / (jnp.sqrt(v_hat) + ADAM_EPS) + ADAM_WD * w_ref[...])
            nm_ref[...] = m
            nv_ref[...] = v

    in_specs, out_specs, out_shape, operands = [], [], [], []
    for w, g, m, v in params:
        spec = pl.BlockSpec((w.shape[0] // steps, w.shape[1]), lambda i: (i, 0))
        in_specs += [spec] * 4
        out_specs += [spec] * 4
        out_shape += [_sds(w.shape, F32)] * 4
        operands += [w, g, m, v]
    outs = _pcall(body, name, (steps,), in_specs, out_specs, out_shape, operands, (), ("parallel",))
    return [tuple(outs[4 * p:4 * p + 4]) for p in range(n)]


MATRIX_NAMES = tuple(MATRICES)
WEIGHT_ORDER = ("g_mix", "w_in", "conv_w", "attn_sinks", "w_conv_out", "w_attn_out", "w_o", "g_ffn",
                "w_gate_up", "w_down", "g_final")


def kernel(x, g_mix, w_in, conv_w, attn_sinks, w_conv_out, w_attn_out, w_o, g_ffn, w_gate_up, w_down, g_final, loss_target, m_g_mix, m_w_in, m_conv_w, m_attn_sinks, m_w_conv_out, m_w_attn_out, m_w_o, m_g_ffn, m_w_gate_up, m_w_down, m_g_final, v_g_mix, v_w_in, v_conv_w, v_attn_sinks, v_w_conv_out, v_w_attn_out, v_w_o, v_g_ffn, v_w_gate_up, v_w_down, v_g_final):
    w = dict(g_mix=g_mix, w_in=w_in[0], conv_w=conv_w[0], attn_sinks=attn_sinks, w_conv_out=w_conv_out[0],
             w_attn_out=w_attn_out[0], w_o=w_o[0], g_ffn=g_ffn, w_gate_up=w_gate_up[0], w_down=w_down[0],
             g_final=g_final[None, :])
    m = dict(g_mix=m_g_mix, w_in=m_w_in[0], conv_w=m_conv_w[0], attn_sinks=m_attn_sinks,
             w_conv_out=m_w_conv_out[0], w_attn_out=m_w_attn_out[0], w_o=m_w_o[0], g_ffn=m_g_ffn,
             w_gate_up=m_w_gate_up[0], w_down=m_w_down[0], g_final=m_g_final[None, :])
    v = dict(g_mix=v_g_mix, w_in=v_w_in[0], conv_w=v_conv_w[0], attn_sinks=v_attn_sinks,
             w_conv_out=v_w_conv_out[0], w_attn_out=v_w_attn_out[0], w_o=v_w_o[0], g_ffn=v_g_ffn,
             w_gate_up=v_w_gate_up[0], w_down=v_w_down[0], g_final=v_g_final[None, :])
    shard = (2 * lax.axis_index("x") + lax.axis_index("y")).astype(jnp.int32)
    core = lax.axis_index("c").astype(jnp.int32)
    shard1, core1, place = shard.reshape((1,)), core.reshape((1,)), jnp.stack([core, shard])
    spec = MATRICES
    xs, target, sinks = x[0], loss_target[0], w["attn_sinks"]
    tables = _rope_tables()

    def gather(names, part=0, parts=1):
        return _Gather([whole[n] for n in names], [(i, spec[n], part, parts) for i, n in enumerate(names)])

    def pair(names):
        return _Pair([dw[n] for n in names], [spec[n] for n in names])

    def pair_sum(tag, names, got):
        return _pair_sum("pair_sum_" + tag, [spec[n] for n in names], [dw[n] for n in names], got, place)

    whole = dict(zip(MATRIX_NAMES, _to_bf16_in_whole(
        [w[n] for n in MATRIX_NAMES], [spec[n] for n in MATRIX_NAMES], shard1)))

    mixers = ("w_conv_out", "w_attn_out", "w_o")
    h1 = _rms_norm("norm_mix", xs, w["g_mix"])
    proj, whole["w_in"], (*got, conv_w_whole) = _mm_in_gather(
        h1, whole["w_in"], _Gather([whole[n] for n in mixers], [(i, spec[n], 0, 1) for i, n in enumerate(mixers)],
                                   conv_w=w["conv_w"]))
    whole.update(zip(mixers, got))
    conv_y = _conv_fwd(proj, conv_w_whole)
    attn, (whole["w_gate_up"],) = _attn_fwd(proj, tables, sinks, comm=gather(("w_gate_up",), 0, 2))
    (conv_out, attn_out, merged), (whole["w_gate_up"],) = _branch_merge(
        conv_y, attn, whole["w_conv_out"], whole["w_attn_out"], proj, comm=gather(("w_gate_up",), 1, 2))
    x2, h2 = _mm_o_norm(merged, whole["w_o"], xs, w["g_ffn"])
    (gate, up, act), (whole["w_down"],) = _gate_up_fwd(h2, whole["w_gate_up"], comm=gather(("w_down",)))
    x3 = _mm_nn("mm_down", act, whole["w_down"], 1024, 512, F32, res=x2)
    dx3, dx3b, dg_final, loss_row = _loss_head(x3, w["g_final"], target)

    dw = {}
    dw["w_down"] = _mm_tn("mm_dw_down", act, dx3b, 1408, 1024, BF16)
    dgate, dup = _dact_swiglu(dx3b, whole["w_down"], gate, up)
    dw["w_gate_up"], got = _mm_dw_gate_up(h2, dgate, dup, comm=pair(("w_down",)))
    sums_a, own_a = pair_sum("down", ("w_down",), got)
    dh2, slots_a = _mm_dh2(dgate, dup, whole["w_gate_up"], comm=_ChipExchange(sums_a, own_a))
    (dx2, dx2b, dg_ffn), got_b = _rms_norm_bwd("norm_ffn_bwd", dh2, x2, w["g_ffn"], dx3, True,
                                               comm=pair(("w_gate_up",)))
    dw["w_o"] = _mm_tn("mm_dw_o", merged, dx2b, 1024, 1024, BF16)
    dco, dao, dgc, dga = _merge_bwd(dx2b, whole["w_o"], conv_out, attn_out, proj)
    dconv_y = _mm_nt("mm_dconv_y", dco, whole["w_conv_out"], 1024, 1024, D_MODEL, F32)
    dw["w_conv_out"] = _mm_tn("mm_dw_conv_out", conv_y, dco, 1024, 1024, BF16)
    dattn = _mm_nt("mm_dattn", dao, whole["w_attn_out"], 1024, 1024, D_MODEL, BF16)
    dw["w_attn_out"] = _mm_tn("mm_dw_attn_out", attn, dao, 1024, 1024, BF16)
    (dcb, dcc, dcx, dconv_w), got_c = _conv_bwd(dconv_y, proj, conv_w_whole, comm=pair(mixers))
    sums_bc, own_bc = pair_sum("gate_up_mixers", ("w_gate_up",) + mixers, got_b + got_c)
    sums_b, own_b, sums_c, own_c = sums_bc[:1], own_bc[:1], sums_bc[1:], own_bc[1:]
    (dq, dk_prev, dk_cur, dv_prev, dv_cur, dsinks), slots_b = _attn_bwd(
        proj, dattn, sinks, tables, comm=_ChipExchange(sums_b, own_b))
    dkv = _kv_grad_combine(dk_prev, dk_cur, dv_prev, dv_cur, tables)
    dproj = jnp.concatenate([dcb, dcc, dcx, dq, dkv, dgc, dga], axis=1)
    dw["w_in"], got, slots_c = _mm_dw_in_pair(h1, dproj, _ChipExchange(sums_c, own_c))
    sums_d, own_d = pair_sum("in", ("w_in",), [got])
    early = ("w_down", "w_gate_up") + mixers
    halves = _chip_sum("chip_sum_early", [spec[n] for n in early], slots_a + slots_b + slots_c, core1)
    dh1, (own_d, *reduced) = _mm_nt(
        "mm_dh1", dproj, whole["w_in"], 1024, 1024, 1664, F32,
        comm=_Both(_ChipExchange(sums_d, own_d, 0, 2), _HalfExchange(halves, [spec[n] for n in early])))
    g = dict(zip(early, reduced))
    (grad_x, dg_mix), slots_d = _rms_norm_bwd("norm_mix_bwd", dh1, xs, w["g_mix"], dx2, False,
                                              comm=_ChipExchange(sums_d, [own_d], 1, 2))
    small = _pack_small(dg_mix, dg_ffn, dg_final, dconv_w, dsinks, loss_row)
    half_in = _chip_sum("chip_sum_in", [spec["w_in"]], slots_d, core1)
    g["w_in"], small_blocks = _comm_call(
        "half_exchange_in", _Both(_HalfExchange(half_in, [spec["w_in"]]), _SmallAllToAll(small)))
    delta, new_m, new_v = {}, {}, {}

    def keep(names, results):
        for n, (d, nm, nv, grad) in zip(names, results):
            delta[n], new_m[n], new_v[n], g[n] = d, nm, nv, grad

    keep(early, _adamw("adamw_early", [(w[n], g[n], m[n], v[n]) for n in early], 8))
    small_sum = _small_sum(small_blocks)
    g["g_mix"] = small_sum[0:1, :]
    g["g_ffn"] = small_sum[1:2, :]
    g["g_final"] = small_sum[2:3, :]
    g["conv_w"] = lax.dynamic_slice(small_sum, (3, shard * CONV_W_COLS), (3, CONV_W_COLS))
    g["attn_sinks"] = small_sum[6:7, :N_HEADS]
    loss = small_sum[7, 0]
    keep(("w_in",), _adamw("adamw_w_in", [(w["w_in"], g["w_in"], m["w_in"], v["w_in"])], 4))
    rest = ("g_mix", "g_ffn", "g_final", "conv_w", "attn_sinks")
    keep(rest, _adamw("adamw_small", [(w[n], g[n], m[n], v[n]) for n in rest], 1))

    def shaped(vals):
        return [vals[n].reshape((D_MODEL,)) if n == "g_final" else
                (vals[n][None] if n in MATRIX_NAMES or n == "conv_w" else vals[n]) for n in WEIGHT_ORDER]

    return (loss, grad_x[None], *shaped(g), *shaped(delta), *shaped(new_m), *shaped(new_v))
```

```python
import functools
import math

import jax
import jax.numpy as jnp
import numpy as np
from jax import lax
from jax.experimental import pallas as pl
from jax.experimental.pallas import tpu as pltpu

F32 = jnp.float32
BF16 = jnp.bfloat16

D_MODEL = 1024
SEQ = 2048
HEAD_DIM = 64
N_HEADS = 16
N_KV_HEADS = 4
GROUP = N_HEADS // N_KV_HEADS
D_ATTN = N_HEADS * HEAD_DIM
D_KV = N_KV_HEADS * HEAD_DIM
BLOCK = 128
ROT_DIM = HEAD_DIM // 4
ROPE_THETA = 500000.0
ATTN_SCALE = 1.0 / math.sqrt(HEAD_DIM)
NEG_INF = -1e30
D_FF = 2816
EPS = 1e-5
N_IN = 3 * D_MODEL + D_ATTN + 2 * D_KV + 2 * D_MODEL
COL_Q = 3 * D_MODEL
COL_K = COL_Q + D_ATTN
COL_V = COL_K + D_KV
COL_GC = COL_V + D_KV
COL_GA = COL_GC + D_MODEL

ADAM_LR = 0.001
ADAM_B1 = 0.9
ADAM_B2 = 0.999
ADAM_EPS = 1e-08
ADAM_WD = 0.01
ADAM_STEP = 10

N_CHIPS = 4
N_DEV = 8

V7X_VMEM_BYTES = 64 * 1024 * 1024
VMEM_LIMIT = (V7X_VMEM_BYTES * 3) // 4
LANES = 128
MESH = pl.DeviceIdType.MESH


def _params(semantics=None):
    return pltpu.CompilerParams(dimension_semantics=semantics, vmem_limit_bytes=VMEM_LIMIT)


def _sds(shape, dtype):
    return jax.ShapeDtypeStruct(shape, dtype)


HBM_SPEC = pl.BlockSpec(memory_space=pl.ANY)


def _pcall(body, name, grid, in_specs, out_specs, out_shape, operands, scratch=(), semantics=None, comm=None,
           aliases=None, start_after_body=False):
    aliases = dict(aliases or {})
    if comm is None:
        return pl.pallas_call(
            body, name=name, grid=grid, in_specs=in_specs, out_specs=out_specs, out_shape=out_shape,
            scratch_shapes=list(scratch), input_output_aliases=aliases,
            compiler_params=_params(semantics))(*operands)
    multi = isinstance(out_shape, (list, tuple))
    o_specs = list(out_specs) if multi else [out_specs]
    o_shape = list(out_shape) if multi else [out_shape]
    n_in, n_out, n_scr = len(operands), len(o_shape), len(scratch)
    n_cin, n_cout = len(comm.operands), len(comm.out_shape)

    def hosted(*refs):
        ins, cins = refs[:n_in], refs[n_in:n_in + n_cin]
        o0 = n_in + n_cin
        outs, couts = refs[o0:o0 + n_out], refs[o0 + n_out:o0 + n_out + n_cout]
        s0 = o0 + n_out + n_cout
        scr, sems = refs[s0:s0 + n_scr], refs[s0 + n_scr:]
        first = last = None
        for axis, size in enumerate(grid):
            i = pl.program_id(axis)
            first = (i == 0) if first is None else first & (i == 0)
            last = (i == size - 1) if last is None else last & (i == size - 1)

        if not start_after_body:
            @pl.when(first)
            def _():
                comm.start(cins, couts, sems)

        body(*ins, *outs, *scr)

        if start_after_body:
            @pl.when(first)
            def _():
                comm.start(cins, couts, sems)

        @pl.when(last)
        def _():
            comm.finish(cins, couts, sems)

    res = pl.pallas_call(
        hosted, name=name, grid=grid,
        in_specs=list(in_specs) + [HBM_SPEC] * n_cin, out_specs=o_specs + [HBM_SPEC] * n_cout,
        out_shape=o_shape + list(comm.out_shape), scratch_shapes=list(scratch) + list(comm.sems),
        input_output_aliases={**aliases, **{n_in + a: n_out + b for a, b in comm.aliases.items()}},
        compiler_params=_params(("arbitrary",) * len(grid)))(*operands, *comm.operands)
    outs = list(res[:n_out])
    return (outs if multi else outs[0]), list(res[n_out:])


def _comm_call(name, comm):
    def body(*refs):
        n_cin, n_cout = len(comm.operands), len(comm.out_shape)
        cins, couts, sems = refs[:n_cin], refs[n_cin:n_cin + n_cout], refs[n_cin + n_cout:]
        comm.start(cins, couts, sems)
        comm.finish(cins, couts, sems)

    return list(pl.pallas_call(
        body, name=name, in_specs=[HBM_SPEC] * len(comm.operands), out_specs=[HBM_SPEC] * len(comm.out_shape),
        out_shape=list(comm.out_shape), scratch_shapes=list(comm.sems),
        input_output_aliases=dict(comm.aliases))(*comm.operands))


NN = ((1,), (0,))
NT = ((1,), (1,))
TN = ((0,), (0,))


def _matmul(name, a, b, dims, grid, a_spec, b_spec, o_spec, o_shape, o_dtype, res=None, res_spec=None, comm=None):
    nk = grid[2]

    def body(*refs):
        if res is None:
            a_ref, b_ref, o_ref = refs[:3]
            r_ref = None
            scratch = refs[3:]
        else:
            a_ref, b_ref, r_ref, o_ref = refs[:4]
            scratch = refs[4:]
        p = lax.dot_general(a_ref[...], b_ref[...], (dims, ((), ())), preferred_element_type=F32)

        def finish(acc):
            if r_ref is not None:
                acc = r_ref[...] + acc
            o_ref[...] = acc.astype(o_dtype)

        if nk == 1:
            finish(p)
        else:
            acc_ref = o_ref if in_place else scratch[0]
            k = pl.program_id(2)

            @pl.when(k == 0)
            def _():
                acc_ref[...] = p

            @pl.when(k > 0)
            def _():
                acc_ref[...] += p

            if not in_place:
                @pl.when(k == nk - 1)
                def _():
                    finish(acc_ref[...])

    in_place = nk > 1 and res is None and o_dtype == F32
    operands = [a, b] if res is None else [a, b, res]
    in_specs = [a_spec, b_spec] if res is None else [a_spec, b_spec, res_spec]
    scratch = [pltpu.VMEM(o_spec.block_shape, F32)] if nk > 1 and not in_place else []
    return _pcall(body, name, grid, in_specs, o_spec, _sds(o_shape, o_dtype), operands, scratch,
                  ("parallel", "parallel", "arbitrary"), comm)


def _mm_nt(name, a, b, bm, bn, bk, o_dtype, comm=None):
    m, k = a.shape
    n = b.shape[0]
    return _matmul(
        name, a, b, NT, (m // bm, n // bn, k // bk),
        pl.BlockSpec((bm, bk), lambda i, j, kk: (i, kk)),
        pl.BlockSpec((bn, bk), lambda i, j, kk: (j, kk)),
        pl.BlockSpec((bm, bn), lambda i, j, kk: (i, j)),
        (m, n), o_dtype, comm=comm,
    )


def _mm_tn(name, a, b, bm, bn, o_dtype, comm=None):
    k, m = a.shape
    n = b.shape[1]
    return _matmul(
        name, a, b, TN, (m // bm, n // bn, 1),
        pl.BlockSpec((k, bm), lambda i, j, kk: (0, i)),
        pl.BlockSpec((k, bn), lambda i, j, kk: (0, j)),
        pl.BlockSpec((bm, bn), lambda i, j, kk: (i, j)),
        (m, n), o_dtype, comm=comm,
    )


ROWS = 256


def _row_spec(width, col=0):
    return pl.BlockSpec((ROWS, width), lambda i: (i, col))


def _full_spec(shape):
    return pl.BlockSpec(shape, lambda *_: (0,) * len(shape))


def _rms_norm(name, x, g):
    def body(x_ref, g_ref, h_ref):
        xf = x_ref[...]
        r = lax.rsqrt(jnp.mean(xf * xf, axis=-1, keepdims=True) + EPS)
        h_ref[...] = ((xf * r) * g_ref[...]).astype(BF16)

    return pl.pallas_call(
        body, name=name, grid=(SEQ // ROWS,),
        in_specs=[_row_spec(D_MODEL), _full_spec((1, D_MODEL))],
        out_specs=_row_spec(D_MODEL),
        out_shape=_sds((SEQ, D_MODEL), BF16),
        compiler_params=_params(("parallel",)),
    )(x, g)


CONV_COLS = 256


def _shift_rows(u, k):
    rows = lax.broadcasted_iota(jnp.int32, u.shape, 0)
    return jnp.where(rows >= k, pltpu.roll(u, k, axis=0), 0.0)


def _conv_fwd(proj, conv_w):
    nblk = D_MODEL // CONV_COLS

    def body(cb_ref, cc_ref, cx_ref, w_ref, y_ref):
        u = cc_ref[...] * cx_ref[...]
        w = w_ref[...]
        cv = w[0:1, :] * _shift_rows(u, 2) + w[1:2, :] * _shift_rows(u, 1) + w[2:3, :] * u
        y_ref[...] = (cb_ref[...] * cv).astype(BF16)

    def col(part):
        return pl.BlockSpec((SEQ, CONV_COLS), lambda j: (0, part * nblk + j))

    return pl.pallas_call(
        body, name="conv_fwd", grid=(nblk,),
        in_specs=[col(0), col(1), col(2), pl.BlockSpec((3, CONV_COLS), lambda j: (0, j))],
        out_specs=pl.BlockSpec((SEQ, CONV_COLS), lambda j: (0, j)),
        out_shape=_sds((SEQ, D_MODEL), BF16),
        compiler_params=_params(("parallel",)),
    )(proj, proj, proj, conv_w)


ROPE_COLS = 256


def _rope_tables():
    f32 = np.float32
    inv_freq = (f32(ROPE_THETA) ** (-np.arange(0, ROT_DIM, 2, dtype=f32) / f32(ROT_DIM))).astype(f32)
    ang = np.arange(SEQ, dtype=f32)[:, None] * inv_freq[None, :]
    cos, sin = np.cos(ang).astype(f32), np.sin(ang).astype(f32)
    half = ROT_DIM // 2
    ones = np.ones((SEQ, HEAD_DIM - ROT_DIM), f32)
    zeros = np.zeros((SEQ, HEAD_DIM - ROT_DIM), f32)
    zh = np.zeros((SEQ, half), f32)
    c = np.concatenate([cos, cos, ones], axis=1)
    s_up = np.concatenate([-sin, zh, zeros], axis=1)
    s_dn = np.concatenate([zh, sin, zeros], axis=1)
    reps = ROPE_COLS // HEAD_DIM
    return tuple(jnp.asarray(np.tile(t, (1, reps))) for t in (c, s_up, s_dn))


def _rotate(t, c, s_up, s_dn):
    width = t.shape[1]
    half = ROT_DIM // 2
    return t * c + pltpu.roll(t, width - half, axis=1) * s_up + pltpu.roll(t, half, axis=1) * s_dn


N_QBLK = SEQ // BLOCK


def _attn_specs():
    prev = lambda n: jnp.maximum(n - 1, 0)
    q = pl.BlockSpec((BLOCK, D_ATTN), lambda n: (n, COL_Q // D_ATTN))
    k_prev = pl.BlockSpec((BLOCK, D_KV), lambda n: (prev(n), COL_K // D_KV))
    k_cur = pl.BlockSpec((BLOCK, D_KV), lambda n: (n, COL_K // D_KV))
    v_prev = pl.BlockSpec((BLOCK, D_KV), lambda n: (prev(n), COL_V // D_KV))
    v_cur = pl.BlockSpec((BLOCK, D_KV), lambda n: (n, COL_V // D_KV))
    tab_cur = pl.BlockSpec((BLOCK, ROPE_COLS), lambda n: (n, 0))
    tab_prev = pl.BlockSpec((BLOCK, ROPE_COLS), lambda n: (prev(n), 0))
    return [q, k_prev, k_cur, v_prev, v_cur] + [tab_cur] * 3 + [tab_prev] * 3


def _band_kv(kp_ref, kc_ref, vp_ref, vc_ref, tabs_cur, tabs_prev):
    k = jnp.concatenate([_rotate(kp_ref[...], *(t[...] for t in tabs_prev)),
                         _rotate(kc_ref[...], *(t[...] for t in tabs_cur))], axis=0)
    v = jnp.concatenate([vp_ref[...], vc_ref[...]], axis=0)
    return k, v


def _query_tiles(q_ref, tiles, tabs_cur):
    c, su, sd = (t[:, :LANES] for t in tabs_cur)
    return jnp.concatenate(
        [_rotate(q_ref[:, t * LANES:(t + 1) * LANES], c, su, sd).astype(BF16) for t in tiles], axis=0)


def _sink_row(sink_ref, tiles, par):
    return jnp.concatenate([jnp.full((1, BLOCK), sink_ref[0, t * HEADS_PER_TILE + par], F32) for t in tiles], axis=1)


def _band_mask(n):
    kj = lax.broadcasted_iota(jnp.int32, (2 * BLOCK, BLOCK), 0)
    qi = lax.broadcasted_iota(jnp.int32, (2 * BLOCK, BLOCK), 1)
    rel = qi + BLOCK - kj
    return (rel >= 0) & (rel < BLOCK) & ((kj >= BLOCK) | (n > 0))


HEADS_PER_TILE = LANES // HEAD_DIM
TILES_PER_GROUP = GROUP // HEADS_PER_TILE


def _group_mask(n):
    return jnp.concatenate([_band_mask(n)] * TILES_PER_GROUP, axis=1)


def _lane_half(shape, par):
    lane = lax.broadcasted_iota(jnp.int32, shape, 1)
    return (lane < HEAD_DIM) if par == 0 else (lane >= HEAD_DIM)


def _head_tiles(kv, h):
    tile = kv[:, (h // HEADS_PER_TILE) * LANES:(h // HEADS_PER_TILE + 1) * LANES].astype(F32)
    own = jnp.where(_lane_half(tile.shape, h % HEADS_PER_TILE), tile, 0.0)
    other = pltpu.roll(own, HEAD_DIM, axis=1)
    lo, hi = (own, other) if h % HEADS_PER_TILE == 0 else (other, own)
    return lo.astype(BF16), hi.astype(BF16)


def _head_softmax(q_tile, k_half, sink, mask):
    s = lax.dot_general(k_half, q_tile, (NT, ((), ())), preferred_element_type=F32) * ATTN_SCALE
    s = jnp.where(mask, s, NEG_INF)
    m = jnp.maximum(jnp.max(s, axis=0, keepdims=True), sink)
    e = jnp.exp(s - m)
    es = jnp.exp(sink - m)
    inv = 1.0 / (jnp.sum(e, axis=0, keepdims=True) + es)
    return e * inv, es * inv


def _attn_fwd(proj, tables, sinks, comm=None):
    def body(sink_ref, q_ref, kp_ref, kc_ref, vp_ref, vc_ref, c_ref, su_ref, sd_ref, cp_ref, sup_ref, sdp_ref, o_ref):
        n = pl.program_id(0)
        mask = _group_mask(n)
        tabs_cur = (c_ref, su_ref, sd_ref)
        k, v = _band_kv(kp_ref, kc_ref, vp_ref, vc_ref, tabs_cur, (cp_ref, sup_ref, sdp_ref))
        for h in range(N_KV_HEADS):
            k_halves = _head_tiles(k, h)
            v_halves = _head_tiles(v, h)
            tiles = [h * TILES_PER_GROUP + t for t in range(TILES_PER_GROUP)]
            q_rows = _query_tiles(q_ref, tiles, tabs_cur)
            acc = None
            for par in range(HEADS_PER_TILE):
                p, _ = _head_softmax(q_rows, k_halves[par], _sink_row(sink_ref, tiles, par), mask)
                o = lax.dot_general(p.astype(BF16), v_halves[par], (TN, ((), ())), preferred_element_type=F32)
                acc = o if acc is None else acc + o
            for i, tile in enumerate(tiles):
                o_ref[:, tile * LANES:(tile + 1) * LANES] = acc[i * BLOCK:(i + 1) * BLOCK, :].astype(BF16)

    return _pcall(
        body, "attn_fwd", (N_QBLK,),
        [pl.BlockSpec(memory_space=pltpu.SMEM)] + _attn_specs(),
        pl.BlockSpec((BLOCK, D_ATTN), lambda n: (n, 0)),
        _sds((SEQ, D_ATTN), BF16), [sinks] + [proj] * 5 + list(tables) * 2, (), ("parallel",), comm)


def _branch_merge(conv_y, attn, w_co, w_ao, proj, comm=None):
    bm, bn = 1024, 512

    def body(cy_ref, at_ref, wc_ref, wa_ref, gc_ref, ga_ref, co_ref, ao_ref, mg_ref):
        co = jnp.dot(cy_ref[...], wc_ref[...], preferred_element_type=F32)
        ao = jnp.dot(at_ref[...], wa_ref[...], preferred_element_type=F32)
        co_ref[...] = co
        ao_ref[...] = ao
        mg_ref[...] = (jax.nn.sigmoid(gc_ref[...]) * co + jax.nn.sigmoid(ga_ref[...]) * ao).astype(BF16)

    act = pl.BlockSpec((bm, D_MODEL), lambda i, j: (i, 0))
    wgt = pl.BlockSpec((D_MODEL, bn), lambda i, j: (0, j))
    out = pl.BlockSpec((bm, bn), lambda i, j: (i, j))
    return _pcall(
        body, "branch_merge", (SEQ // bm, D_MODEL // bn),
        [act, act, wgt, wgt,
         pl.BlockSpec((bm, bn), lambda i, j: (i, COL_GC // bn + j)),
         pl.BlockSpec((bm, bn), lambda i, j: (i, COL_GA // bn + j))],
        [out, out, out],
        [_sds((SEQ, D_MODEL), F32), _sds((SEQ, D_MODEL), F32), _sds((SEQ, D_MODEL), BF16)],
        [conv_y, attn, w_co, w_ao, proj, proj], (), ("parallel", "parallel"), comm)


def _mm_o_norm(merged, w_o, x, g):
    bm = 1024

    def body(a_ref, w_ref, x_ref, g_ref, o_ref, h_ref):
        x2 = x_ref[...] + jnp.dot(a_ref[...], w_ref[...], preferred_element_type=F32)
        o_ref[...] = x2
        r = lax.rsqrt(jnp.mean(x2 * x2, axis=-1, keepdims=True) + EPS)
        h_ref[...] = ((x2 * r) * g_ref[...]).astype(BF16)

    row = pl.BlockSpec((bm, D_MODEL), lambda i: (i, 0))
    return pl.pallas_call(
        body, name="mm_o", grid=(SEQ // bm,),
        in_specs=[row, _full_spec((D_MODEL, D_MODEL)), row, _full_spec((1, D_MODEL))],
        out_specs=[row, row], out_shape=[_sds((SEQ, D_MODEL), F32), _sds((SEQ, D_MODEL), BF16)],
        compiler_params=_params(("parallel",)),
    )(merged, w_o, x, g)


FF_BM, FF_BN = 512, 1408
FF_NB = D_FF // FF_BN


def _gate_up_fwd(h2, w_gu, comm=None):
    def body(h_ref, wg_ref, wu_ref, g_ref, u_ref, a_ref):
        h = h_ref[...]
        g = jnp.dot(h, wg_ref[...], preferred_element_type=F32)
        u = jnp.dot(h, wu_ref[...], preferred_element_type=F32)
        g_ref[...] = g
        u_ref[...] = u
        a_ref[...] = (jax.nn.silu(g) * u).astype(BF16)

    out = pl.BlockSpec((FF_BM, FF_BN), lambda i, j: (i, j))
    f32, b16 = _sds((SEQ, D_FF), F32), _sds((SEQ, D_FF), BF16)
    return _pcall(
        body, "mm_gate_up", (SEQ // FF_BM, FF_NB),
        [pl.BlockSpec((FF_BM, D_MODEL), lambda i, j: (i, 0)),
         pl.BlockSpec((D_MODEL, FF_BN), lambda i, j: (0, j)),
         pl.BlockSpec((D_MODEL, FF_BN), lambda i, j: (0, FF_NB + j))],
        [out, out, out], [f32, f32, b16], [h2, w_gu, w_gu], (), ("parallel", "parallel"), comm)


def _dact_swiglu(dx3b, w_down, g, u):
    def body(dx_ref, w_ref, g_ref, u_ref, dg_ref, du_ref):
        da = lax.dot_general(dx_ref[...], w_ref[...], (NT, ((), ())), preferred_element_type=F32)
        g = g_ref[...]
        sg = jax.nn.sigmoid(g)
        dg_ref[...] = (da * u_ref[...] * (sg * (1.0 + g * (1.0 - sg)))).astype(BF16)
        du_ref[...] = (da * (g * sg)).astype(BF16)

    blk = pl.BlockSpec((FF_BM, FF_BN), lambda i, j: (i, j))
    b16 = _sds((SEQ, D_FF), BF16)
    return _pcall(
        body, "mm_dact", (SEQ // FF_BM, FF_NB),
        [pl.BlockSpec((FF_BM, D_MODEL), lambda i, j: (i, 0)), pl.BlockSpec((FF_BN, D_MODEL), lambda i, j: (j, 0)),
         blk, blk],
        [blk, blk], [b16, b16], [dx3b, w_down, g, u], (), ("parallel", "parallel"))


def _mm_dh2(dg, du, w_gu, comm=None):
    bm = 1024
    nk = 2 * FF_NB

    def body(dg_ref, du_ref, w_ref, o_ref):
        k = pl.program_id(1)

        def part(a_ref):
            return lax.dot_general(a_ref[...], w_ref[...], (NT, ((), ())), preferred_element_type=F32)

        @pl.when(k == 0)
        def _():
            o_ref[...] = part(dg_ref)

        @pl.when((k > 0) & (k < FF_NB))
        def _():
            o_ref[...] += part(dg_ref)

        @pl.when(k >= FF_NB)
        def _():
            o_ref[...] += part(du_ref)

    return _pcall(
        body, "mm_dh2", (SEQ // bm, nk),
        [pl.BlockSpec((bm, FF_BN), lambda i, k: (i, jnp.minimum(k, FF_NB - 1))),
         pl.BlockSpec((bm, FF_BN), lambda i, k: (i, jnp.maximum(k - FF_NB, 0))),
         pl.BlockSpec((D_MODEL, FF_BN), lambda i, k: (0, k))],
        pl.BlockSpec((bm, D_MODEL), lambda i, k: (i, 0)), _sds((SEQ, D_MODEL), F32),
        [dg, du, w_gu], (), ("parallel", "arbitrary"), comm)


def _mm_dw_gate_up(h2, dg, du, comm=None):
    def body(h_ref, dg_ref, du_ref, o_ref):
        j = pl.program_id(0)

        def part(b_ref):
            return lax.dot_general(h_ref[...], b_ref[...], (TN, ((), ())), preferred_element_type=F32).astype(BF16)

        @pl.when(j < FF_NB)
        def _():
            o_ref[...] = part(dg_ref)

        @pl.when(j >= FF_NB)
        def _():
            o_ref[...] = part(du_ref)

    return _pcall(
        body, "mm_dw_gate_up", (2 * FF_NB,),
        [_full_spec((SEQ, D_MODEL)),
         pl.BlockSpec((SEQ, FF_BN), lambda j: (0, jnp.minimum(j, FF_NB - 1))),
         pl.BlockSpec((SEQ, FF_BN), lambda j: (0, jnp.maximum(j - FF_NB, 0)))],
        pl.BlockSpec((D_MODEL, FF_BN), lambda j: (0, j)),
        _sds((D_MODEL, 2 * D_FF), BF16), [h2, dg, du], (), ("arbitrary",), comm)


def _mm_down_loss(act, w_down, x2, g, target):
    bm = 512

    def body(a_ref, w_ref, x_ref, g_ref, t_ref, dx_ref, dxb_ref, dg_ref, loss_ref):
        i = pl.program_id(0)
        xf = x_ref[...] + jnp.dot(a_ref[...], w_ref[...], preferred_element_type=F32)
        r = lax.rsqrt(jnp.mean(xf * xf, axis=-1, keepdims=True) + EPS)
        xn = xf * r
        gg = g_ref[...]
        err = xn * gg - t_ref[...]
        part = 0.5 * jnp.sum(jnp.mean(err * err, axis=-1, keepdims=True), axis=0, keepdims=True)
        dy = err * (1.0 / D_MODEL)
        dxn = dy * gg
        dx = r * (dxn - xn * jnp.mean(dxn * xn, axis=-1, keepdims=True))
        dx_ref[...] = dx
        dxb_ref[...] = dx.astype(BF16)
        dg = jnp.sum(dy * xn, axis=0, keepdims=True)
        lane0 = lax.broadcasted_iota(jnp.int32, (1, LANES), 1) == 0
        lpart = jnp.where(lane0, part, 0.0)

        @pl.when(i == 0)
        def _():
            dg_ref[...] = dg
            loss_ref[...] = lpart

        @pl.when(i > 0)
        def _():
            dg_ref[...] += dg
            loss_ref[...] += lpart

    row = pl.BlockSpec((bm, D_MODEL), lambda i: (i, 0))
    return pl.pallas_call(
        body, name="mm_down", grid=(SEQ // bm,),
        in_specs=[pl.BlockSpec((bm, D_FF), lambda i: (i, 0)), _full_spec((D_FF, D_MODEL)), row,
                  _full_spec((1, D_MODEL)), row],
        out_specs=[row, row, _full_spec((1, D_MODEL)), _full_spec((1, LANES))],
        out_shape=[_sds((SEQ, D_MODEL), F32), _sds((SEQ, D_MODEL), BF16),
                   _sds((1, D_MODEL), F32), _sds((1, LANES), F32)],
        compiler_params=_params(("arbitrary",)),
    )(act, w_down, x2, g, target)


def _rms_norm_bwd(name, dh, x, g, dres, with_bf16, comm=None):
    def body(dh_ref, x_ref, g_ref, dr_ref, *outs):
        i = pl.program_id(0)
        dx_ref = outs[0]
        dg_ref = outs[-1]
        xf = x_ref[...]
        r = lax.rsqrt(jnp.mean(xf * xf, axis=-1, keepdims=True) + EPS)
        xn = xf * r
        dh = dh_ref[...]
        dxn = dh * g_ref[...]
        dx = dr_ref[...] + r * (dxn - xn * jnp.mean(dxn * xn, axis=-1, keepdims=True))
        dx_ref[...] = dx
        if with_bf16:
            outs[1][...] = dx.astype(BF16)
        dg = jnp.sum(dh * xn, axis=0, keepdims=True)

        @pl.when(i == 0)
        def _():
            dg_ref[...] = dg

        @pl.when(i > 0)
        def _():
            dg_ref[...] += dg

    row = _row_spec(D_MODEL)
    out_specs = [row] + ([row] if with_bf16 else []) + [_full_spec((1, D_MODEL))]
    out_shape = ([_sds((SEQ, D_MODEL), F32)] + ([_sds((SEQ, D_MODEL), BF16)] if with_bf16 else [])
                 + [_sds((1, D_MODEL), F32)])
    return _pcall(body, name, (SEQ // ROWS,), [row, row, _full_spec((1, D_MODEL)), row], out_specs, out_shape,
                  [dh, x, g, dres], (), ("arbitrary",), comm)


def _merge_bwd(dx2b, w_o, conv_out, attn_out, proj):
    bm, bn = 1024, D_MODEL // 2

    def body(dx_ref, w_ref, co_ref, ao_ref, gc_ref, ga_ref, dco_ref, dao_ref, dgc_ref, dga_ref):
        dm = lax.dot_general(dx_ref[...], w_ref[...], (NT, ((), ())), preferred_element_type=F32)
        sc = jax.nn.sigmoid(gc_ref[...])
        sa = jax.nn.sigmoid(ga_ref[...])
        dco_ref[...] = (dm * sc).astype(BF16)
        dao_ref[...] = (dm * sa).astype(BF16)
        dgc_ref[...] = (dm * co_ref[...] * (sc * (1.0 - sc))).astype(BF16)
        dga_ref[...] = (dm * ao_ref[...] * (sa * (1.0 - sa))).astype(BF16)

    own = pl.BlockSpec((bm, bn), lambda i, j: (i, j))
    sd = _sds((SEQ, D_MODEL), BF16)
    return pl.pallas_call(
        body, name="mm_dmerged", grid=(SEQ // bm, D_MODEL // bn),
        in_specs=[pl.BlockSpec((bm, D_MODEL), lambda i, j: (i, 0)), pl.BlockSpec((bn, D_MODEL), lambda i, j: (j, 0)),
                  own, own,
                  pl.BlockSpec((bm, bn), lambda i, j: (i, COL_GC // bn + j)),
                  pl.BlockSpec((bm, bn), lambda i, j: (i, COL_GA // bn + j))],
        out_specs=[own, own, own, own], out_shape=[sd, sd, sd, sd],
        compiler_params=_params(("parallel", "parallel")),
    )(dx2b, w_o, conv_out, attn_out, proj, proj)


def _conv_bwd(dco, w_co, proj, conv_w, comm=None):
    nblk = D_MODEL // CONV_COLS

    def body(dco_ref, wco_ref, cb_ref, cc_ref, cx_ref, w_ref, dcb_ref, dcc_ref, dcx_ref, dw_ref):
        cc = cc_ref[...]
        cx = cx_ref[...]
        u = cc * cx
        w = w_ref[...]
        u1 = _shift_rows(u, 1)
        u2 = _shift_rows(u, 2)
        cv = w[0:1, :] * u2 + w[1:2, :] * u1 + w[2:3, :] * u
        dy = lax.dot_general(dco_ref[...], wco_ref[...], (NT, ((), ())), preferred_element_type=F32)
        dcb_ref[...] = (dy * cv).astype(BF16)
        dcv = dy * cb_ref[...]
        rows = lax.broadcasted_iota(jnp.int32, dcv.shape, 0)
        up1 = jnp.where(rows < SEQ - 1, pltpu.roll(dcv, SEQ - 1, axis=0), 0.0)
        up2 = jnp.where(rows < SEQ - 2, pltpu.roll(dcv, SEQ - 2, axis=0), 0.0)
        du = w[2:3, :] * dcv + w[1:2, :] * up1 + w[0:1, :] * up2
        dcc_ref[...] = (du * cx).astype(BF16)
        dcx_ref[...] = (du * cc).astype(BF16)
        dw_ref[...] = jnp.concatenate(
            [jnp.sum(dcv * u2, axis=0, keepdims=True),
             jnp.sum(dcv * u1, axis=0, keepdims=True),
             jnp.sum(dcv * u, axis=0, keepdims=True)], axis=0)

    def col(part):
        return pl.BlockSpec((SEQ, CONV_COLS), lambda j: (0, part * nblk + j))

    own = pl.BlockSpec((SEQ, CONV_COLS), lambda j: (0, j))
    wsp = pl.BlockSpec((3, CONV_COLS), lambda j: (0, j))
    sd = _sds((SEQ, D_MODEL), BF16)
    return _pcall(
        body, "conv_bwd", (nblk,),
        [_full_spec((SEQ, D_MODEL)), pl.BlockSpec((CONV_COLS, D_MODEL), lambda j: (j, 0)), col(0), col(1), col(2), wsp],
        [own, own, own, wsp], [sd, sd, sd, _sds((3, D_MODEL), F32)],
        [dco, w_co, proj, proj, proj, conv_w], (), ("parallel",), comm)


def _attn_bwd(proj, dao, w_ao, sinks, tables, comm=None):
    def body(sink_ref, q_ref, kp_ref, kc_ref, vp_ref, vc_ref, c_ref, su_ref, sd_ref, cp_ref, sup_ref, sdp_ref,
             dao_ref, wao_ref, dq_ref, dkp_ref, dkc_ref, dvp_ref, dvc_ref, ds_ref):
        n = pl.program_id(0)
        mask = _group_mask(n)
        tabs_cur = (c_ref, su_ref, sd_ref)
        k, v = _band_kv(kp_ref, kc_ref, vp_ref, vc_ref, tabs_cur, (cp_ref, sup_ref, sdp_ref))
        do = lax.dot_general(dao_ref[...], wao_ref[...], (NT, ((), ())), preferred_element_type=F32).astype(BF16)
        lane = lax.broadcasted_iota(jnp.int32, (1, LANES), 1)
        dsink = jnp.zeros((1, LANES), F32)
        c, su, sd = c_ref[:, :LANES], su_ref[:, :LANES], sd_ref[:, :LANES]
        dk_tiles = [None] * (N_KV_HEADS // HEADS_PER_TILE)
        dv_tiles = [None] * (N_KV_HEADS // HEADS_PER_TILE)
        for h in range(N_KV_HEADS):
            k_halves = _head_tiles(k, h)
            v_halves = _head_tiles(v, h)
            tiles = [h * TILES_PER_GROUP + t for t in range(TILES_PER_GROUP)]
            q_rows = _query_tiles(q_ref, tiles, tabs_cur)
            do_rows = jnp.concatenate([do[:, t * LANES:(t + 1) * LANES] for t in tiles], axis=0)
            dk_par, dv_par = [], []
            dq_rows = None
            for par in range(HEADS_PER_TILE):
                p, p_sink = _head_softmax(q_rows, k_halves[par], _sink_row(sink_ref, tiles, par), mask)
                dp = lax.dot_general(v_halves[par], do_rows, (NT, ((), ())), preferred_element_type=F32)
                delta = jnp.sum(p * dp, axis=0, keepdims=True)
                ds = (p * (dp - delta) * ATTN_SCALE).astype(BF16)
                dq = lax.dot_general(ds, k_halves[par], (TN, ((), ())), preferred_element_type=F32)
                dq_rows = dq if dq_rows is None else dq_rows + dq
                dk_par.append(jnp.dot(ds, q_rows, preferred_element_type=F32))
                dv_par.append(jnp.dot(p.astype(BF16), do_rows, preferred_element_type=F32))
                sink_grad = p_sink * delta
                for i, tile in enumerate(tiles):
                    val = -jnp.sum(sink_grad[:, i * BLOCK:(i + 1) * BLOCK], axis=1, keepdims=True)
                    dsink = dsink + jnp.where(lane == tile * HEADS_PER_TILE + par, val, 0.0)
            for i, tile in enumerate(tiles):
                dq_tile = dq_rows[i * BLOCK:(i + 1) * BLOCK, :]
                dq_ref[:, tile * LANES:(tile + 1) * LANES] = _rotate(dq_tile, c, -su, -sd).astype(BF16)
            own = h % HEADS_PER_TILE
            for par_grads, tiles in ((dk_par, dk_tiles), (dv_par, dv_tiles)):
                shifted = pltpu.roll(par_grads[1 - own], HEAD_DIM, axis=1)
                total = jnp.where(_lane_half(shifted.shape, own), par_grads[own] + shifted, 0.0)
                i = h // HEADS_PER_TILE
                tiles[i] = total if tiles[i] is None else tiles[i] + total
        for i in range(N_KV_HEADS // HEADS_PER_TILE):
            cols = slice(i * LANES, (i + 1) * LANES)
            dkp_ref[:, cols] = dk_tiles[i][:BLOCK, :]
            dkc_ref[:, cols] = dk_tiles[i][BLOCK:, :]
            dvp_ref[:, cols] = dv_tiles[i][:BLOCK, :]
            dvc_ref[:, cols] = dv_tiles[i][BLOCK:, :]

        @pl.when(n == 0)
        def _():
            ds_ref[...] = dsink

        @pl.when(n > 0)
        def _():
            ds_ref[...] += dsink

    blk = pl.BlockSpec((BLOCK, D_KV), lambda n: (n, 0))
    prev_blk = pl.BlockSpec((BLOCK, D_KV), lambda n: ((n + N_QBLK - 1) % N_QBLK, 0))
    kv = _sds((SEQ, D_KV), F32)
    return _pcall(
        body, "attn_bwd", (N_QBLK,),
        [pl.BlockSpec(memory_space=pltpu.SMEM)] + _attn_specs()
        + [pl.BlockSpec((BLOCK, D_MODEL), lambda n: (n, 0)), _full_spec((D_ATTN, D_MODEL))],
        [pl.BlockSpec((BLOCK, D_ATTN), lambda n: (n, 0)), prev_blk, blk, prev_blk, blk, _full_spec((1, LANES))],
        [_sds((SEQ, D_ATTN), BF16), kv, kv, kv, kv, _sds((1, LANES), F32)],
        [sinks] + [proj] * 5 + list(tables) * 2 + [dao, w_ao], (), ("arbitrary",), comm)


def _kv_grad_combine(dk_prev, dk_cur, dv_prev, dv_cur, tables):
    rows = 4 * BLOCK

    def body(kp_ref, kc_ref, vp_ref, vc_ref, c_ref, su_ref, sd_ref, o_ref):
        dk = kc_ref[...] + kp_ref[...]
        dv = vc_ref[...] + vp_ref[...]
        o_ref[:, :D_KV] = _rotate(dk, c_ref[...], -su_ref[...], -sd_ref[...]).astype(BF16)
        o_ref[:, D_KV:] = dv.astype(BF16)

    blk = pl.BlockSpec((rows, D_KV), lambda m: (m, 0))
    return pl.pallas_call(
        body, name="kv_grad_combine", grid=(SEQ // rows,),
        in_specs=[blk] * 7,
        out_specs=pl.BlockSpec((rows, 2 * D_KV), lambda m: (m, 0)),
        out_shape=_sds((SEQ, 2 * D_KV), BF16),
        compiler_params=_params(("parallel",)),
    )(dk_prev, dk_cur, dv_prev, dv_cur, *tables)


MATRICES = {
    "w_in": (D_MODEL, N_IN // N_CHIPS, "col"),
    "w_conv_out": (D_MODEL // N_CHIPS, D_MODEL, "row"),
    "w_attn_out": (D_MODEL // N_CHIPS, D_MODEL, "row"),
    "w_o": (D_MODEL // N_CHIPS, D_MODEL, "row"),
    "w_gate_up": (D_MODEL, 2 * D_FF // N_CHIPS, "col"),
    "w_down": (D_FF // N_CHIPS, D_MODEL, "row"),
}
BF16_ROW_TILE = 16
CONV_W_COLS = D_MODEL // N_CHIPS
SMALL_ROWS = 8


def _whole_shape(spec):
    rows, cols, kind = spec
    return (rows, cols * N_CHIPS) if kind == "col" else (rows * N_CHIPS, cols)


def _half_shape(spec):
    return (spec[0] // 2, spec[1])


def _aligned(start, multiple):
    return start if isinstance(start, int) else pl.multiple_of(start, multiple)


def _region(ref, spec, shard, half, part=0, parts=1):
    rows, cols, kind = spec
    hr = rows // 2
    n = hr // parts
    if kind == "col":
        return ref.at[pl.ds(_aligned(half * hr + part * n, BF16_ROW_TILE), n),
                      pl.ds(_aligned(shard * cols, LANES), cols)]
    return ref.at[pl.ds(_aligned(shard * rows + half * hr + part * n, BF16_ROW_TILE), n), :]


def _position():
    x, y, c = lax.axis_index("x"), lax.axis_index("y"), lax.axis_index("c")
    chips = [(1 - x, y), (x, 1 - y), (1 - x, 1 - y)]
    return x, y, c, chips


def _shard_of(chip):
    return 2 * chip[0] + chip[1]


def _remote(src, dst, send_sem, recv_sem, to):
    return pltpu.make_async_remote_copy(src_ref=src, dst_ref=dst, send_sem=send_sem, recv_sem=recv_sem,
                                        device_id=to, device_id_type=MESH)


CAST_STEPS = 4


def _to_bf16_in_whole(ws, specs, shard):
    n = len(ws)

    def body(s_ref, *refs):
        del s_ref
        for w_ref, o_ref in zip(refs[:n], refs[n:]):
            o_ref[...] = w_ref[...].astype(BF16)

    def out_spec(spec):
        rows = spec[0] // CAST_STEPS
        if spec[2] == "col":
            return pl.BlockSpec((rows, spec[1]), lambda i, s_ref: (i, s_ref[0]))
        return pl.BlockSpec((rows, spec[1]), lambda i, s_ref: (s_ref[0] * CAST_STEPS + i, 0))

    grid_spec = pltpu.PrefetchScalarGridSpec(
        num_scalar_prefetch=1, grid=(CAST_STEPS,),
        in_specs=[pl.BlockSpec((s[0] // CAST_STEPS, s[1]), lambda i, s_ref: (i, 0)) for s in specs],
        out_specs=[out_spec(s) for s in specs])
    return list(pl.pallas_call(
        body, name="cast_shards", grid_spec=grid_spec, out_shape=[_sds(_whole_shape(s), BF16) for s in specs],
        compiler_params=_params(("parallel",)),
    )(shard, *ws))


class _Gather:
    def __init__(self, wholes, pieces, conv_w=None):
        self.pieces = pieces
        self.n = len(wholes)
        self.with_conv_w = conv_w is not None
        self.operands = list(wholes) + ([conv_w] if self.with_conv_w else [])
        self.out_shape = [_sds(w.shape, w.dtype) for w in wholes]
        if self.with_conv_w:
            self.out_shape.append(_sds((3, D_MODEL), F32))
        self.aliases = {i: i for i in range(self.n)}
        n_ici = 3 * len(pieces)
        self.sems = [pltpu.SemaphoreType.DMA((n_ici,))] * 4
        if self.with_conv_w:
            self.sems += [pltpu.SemaphoreType.DMA((1,)), pltpu.SemaphoreType.DMA((3,)), pltpu.SemaphoreType.DMA((3,))]

    def _conv_w(self, cins, couts, sems, with_recvs):
        cw_in, cw_out = cins[self.n], couts[self.n]
        x, y, c, chips = _position()

        def cols(shard):
            return cw_out.at[:, pl.ds(_aligned(shard * CONV_W_COLS, LANES), CONV_W_COLS)]

        me = _shard_of((x, y))
        local = pltpu.make_async_copy(cw_in, cols(me), sems[4].at[0])
        sends = [_remote(cw_in, cols(me), sems[5].at[j], sems[6].at[j], (*chip, c)) for j, chip in enumerate(chips)]
        if not with_recvs:
            return local, sends, []
        recvs = [_remote(cols(_shard_of(chip)), cols(_shard_of(chip)), sems[5].at[j], sems[6].at[j], (*chip, c))
                 for j, chip in enumerate(chips)]
        return local, sends, recvs

    def start(self, cins, couts, sems):
        x, y, c, chips = _position()
        me = _shard_of((x, y))
        if self.with_conv_w:
            local, sends, _ = self._conv_w(cins, couts, sems, False)
            local.start()
            for cp in sends:
                cp.start()
        for p, (i, spec, part, parts) in enumerate(self.pieces):
            mine = _region(couts[i], spec, me, c, part, parts)
            for j, chip in enumerate(chips):
                _remote(mine, mine, sems[0].at[3 * p + j], sems[1].at[3 * p + j], (*chip, c)).start()

    def finish(self, cins, couts, sems):
        x, y, c, chips = _position()
        me = _shard_of((x, y))
        sibling = (x, y, 1 - c)
        send_a, recv_a, send_b, recv_b = sems[:4]
        passed = []
        for p, (i, spec, part, parts) in enumerate(self.pieces):
            for j, chip in enumerate(chips):
                k = 3 * p + j
                landed = _region(couts[i], spec, _shard_of(chip), c, part, parts)
                _remote(landed, landed, send_a.at[k], recv_a.at[k], (*chip, c)).wait_recv()
                cp = _remote(landed, landed, send_b.at[k], recv_b.at[k], sibling)
                cp.start()
                passed.append(cp)
        for p, (i, spec, part, parts) in enumerate(self.pieces):
            mine = _region(couts[i], spec, me, c, part, parts)
            for j, chip in enumerate(chips):
                k = 3 * p + j
                other = _region(couts[i], spec, _shard_of(chip), 1 - c, part, parts)
                _remote(other, other, send_b.at[k], recv_b.at[k], sibling).wait_recv()
                _remote(mine, mine, send_a.at[k], recv_a.at[k], (*chip, c)).wait_send()
        for cp in passed:
            cp.wait_send()
        if self.with_conv_w:
            local, sends, recvs = self._conv_w(cins, couts, sems, True)
            for cp in recvs:
                cp.wait_recv()
            for cp in sends:
                cp.wait_send()
            local.wait()


def _mm_in_gather(h1, w_whole, comm):
    spec = MATRICES["w_in"]
    cols = spec[1]
    bm = SEQ // 2

    def body(h_ref, w_in_ref, proj_ref, w_ref, wbuf, obuf, send_a, recv_a, send_b, recv_b, load_sem, store_sems):
        del w_in_ref
        s, mi = pl.program_id(0), pl.program_id(1)
        x, y, c, chips = _position()
        me = _shard_of((x, y))
        sibling = (x, y, 1 - c)
        mine = _region(w_ref, spec, me, c)

        @pl.when((s == 0) & (mi == 0))
        def _():
            for j, chip in enumerate(chips):
                _remote(mine, mine, send_a.at[j], recv_a.at[j], (*chip, c)).start()

        shard = me
        for j, chip in enumerate(chips):
            shard = jnp.where(s == j + 1, _shard_of(chip), shard)

            @pl.when((s == j + 1) & (mi == 0))
            def _():
                landed = _region(w_ref, spec, _shard_of(chip), c)
                _remote(landed, landed, send_a.at[j], recv_a.at[j], (*chip, c)).wait_recv()
                _remote(landed, landed, send_b.at[j], recv_b.at[j], sibling).start()
                other = _region(w_ref, spec, _shard_of(chip), 1 - c)
                _remote(other, other, send_b.at[j], recv_b.at[j], sibling).wait_recv()

        col0 = pl.multiple_of(shard * cols, LANES)

        @pl.when(mi == 0)
        def _():
            load = pltpu.make_async_copy(w_ref.at[:, pl.ds(col0, cols)], wbuf, load_sem.at[0])
            load.start()
            load.wait()

        def store():
            rows = pl.ds(pl.multiple_of(mi * bm, bm), bm)
            return pltpu.make_async_copy(obuf.at[mi], proj_ref.at[rows, pl.ds(col0, cols)], store_sems.at[mi])

        @pl.when(s > 0)
        def _():
            store().wait()

        obuf[mi] = jnp.dot(h_ref[...], wbuf[...], preferred_element_type=F32)
        store().start()

        @pl.when(s == N_CHIPS - 1)
        def _():
            store().wait()

        @pl.when((s == N_CHIPS - 1) & (mi == 1))
        def _():
            for j, chip in enumerate(chips):
                landed = _region(w_ref, spec, _shard_of(chip), c)
                _remote(mine, mine, send_a.at[j], recv_a.at[j], (*chip, c)).wait_send()
                _remote(landed, landed, send_b.at[j], recv_b.at[j], sibling).wait_send()

    sem3 = pltpu.SemaphoreType.DMA((3,))
    (proj, whole), extra = _pcall(
        body, "mm_in", (N_CHIPS, SEQ // bm),
        [pl.BlockSpec((bm, D_MODEL), lambda s, m: (m, 0)), HBM_SPEC], [HBM_SPEC, HBM_SPEC],
        [_sds((SEQ, N_IN), F32), _sds(w_whole.shape, w_whole.dtype)], [h1, w_whole],
        [pltpu.VMEM((D_MODEL, cols), BF16), pltpu.VMEM((SEQ // bm, bm, cols), F32), sem3, sem3, sem3, sem3,
         pltpu.SemaphoreType.DMA((1,)), pltpu.SemaphoreType.DMA((SEQ // bm,))],
        None, comm, aliases={1: 1}, start_after_body=True)
    return proj, whole, extra


def _mm_dw_in_pair(h1, dproj, comm):
    spec = MATRICES["w_in"]
    rows, cols, _ = spec
    hr = rows // 2

    def body(h_ref, dp_ref, dw_ref, got_ref, obuf, store_sems, send_sems, recv_sems):
        t = pl.program_id(0)
        x, y, c, _ = _position()
        sibling = (x, y, 1 - c)

        def store(step):
            return pltpu.make_async_copy(obuf.at[step % 2], dw_ref.at[:, pl.ds(step * cols, cols)],
                                         store_sems.at[step % 2])

        def send(step):
            theirs = obuf.at[step % 2, pl.ds(_aligned((1 - c) * hr, BF16_ROW_TILE), hr), :]
            return _remote(theirs, got_ref.at[step], send_sems.at[step], recv_sems.at[step], sibling)

        for step in range(N_CHIPS):
            @pl.when(t == step)
            def _():
                if step >= 2:
                    store(step - 2).wait()
                    send(step - 2).wait_send()
                obuf[step % 2] = lax.dot_general(
                    h_ref[...], dp_ref[...], (TN, ((), ())), preferred_element_type=F32).astype(BF16)
                store(step).start()
                send(step).start()

        @pl.when(t == N_CHIPS - 1)
        def _():
            for step in (N_CHIPS - 2, N_CHIPS - 1):
                store(step).wait()
                send(step).wait_send()
            for step in range(N_CHIPS):
                send(step).wait_recv()

    sem4 = pltpu.SemaphoreType.DMA((N_CHIPS,))
    (dw_in, got), extra = _pcall(
        body, "mm_dw_in", (N_CHIPS,),
        [_full_spec((SEQ, D_MODEL)), pl.BlockSpec((SEQ, cols), lambda t: (0, t))], [HBM_SPEC, HBM_SPEC],
        [_sds(_whole_shape(spec), BF16), _sds((N_CHIPS, hr, cols), BF16)], [h1, dproj],
        [pltpu.VMEM((2, rows, cols), BF16), pltpu.SemaphoreType.DMA((2,)), sem4, sem4], None, comm)
    return dw_in, got, extra


def _pack_small(dg_mix, dg_ffn, dg_final, dconv_w, dsinks, loss_row):
    def body(a_ref, b_ref, c_ref, w_ref, s_ref, l_ref, o_ref):
        pad = jnp.zeros((1, D_MODEL - LANES), F32)
        o_ref[0:1, :] = a_ref[...]
        o_ref[1:2, :] = b_ref[...]
        o_ref[2:3, :] = c_ref[...]
        o_ref[3:6, :] = w_ref[...]
        o_ref[6:7, :] = jnp.concatenate([s_ref[...], pad], axis=1)
        o_ref[7:8, :] = jnp.concatenate([l_ref[...], pad], axis=1)

    return pl.pallas_call(
        body, name="pack_small", out_shape=_sds((SMALL_ROWS, D_MODEL), F32),
        compiler_params=_params(),
    )(dg_mix, dg_ffn, dg_final, dconv_w, dsinks, loss_row)


class _Pair:
    def __init__(self, dws, specs):
        self.specs = specs
        self.operands = list(dws)
        self.out_shape = [_sds((N_CHIPS, *_half_shape(s)), BF16) for s in specs]
        self.aliases = {}
        n = N_CHIPS * len(specs)
        self.sems = [pltpu.SemaphoreType.DMA((n,)), pltpu.SemaphoreType.DMA((n,))]

    def _copies(self, cins, couts, sems):
        x, y, c, _ = _position()
        sibling = (x, y, 1 - c)
        for i, spec in enumerate(self.specs):
            for t in range(N_CHIPS):
                k = N_CHIPS * i + t
                yield _remote(_region(cins[i], spec, t, 1 - c), couts[i].at[t], sems[0].at[k], sems[1].at[k], sibling)

    def start(self, cins, couts, sems):
        for cp in self._copies(cins, couts, sems):
            cp.start()

    def finish(self, cins, couts, sems):
        for cp in self._copies(cins, couts, sems):
            cp.wait()


class _SmallAllToAll:
    def __init__(self, small):
        self.operands = [small]
        self.out_shape = [_sds((N_DEV, SMALL_ROWS, D_MODEL), F32)]
        self.aliases = {}
        self.sems = [pltpu.SemaphoreType.DMA((N_DEV - 1,)), pltpu.SemaphoreType.DMA((N_DEV - 1,)),
                     pltpu.SemaphoreType.DMA((1,))]

    def _copies(self, cins, couts, sems):
        x, y, c, _ = _position()
        me = 4 * x + 2 * y + c
        out = []
        for r in range(1, N_DEV):
            flip = ((r >> 2) & 1, (r >> 1) & 1, r & 1)
            peer = tuple(1 - p if f else p for p, f in zip((x, y, c), flip))
            theirs = couts[0].at[4 * peer[0] + 2 * peer[1] + peer[2]]
            out.append((_remote(cins[0], couts[0].at[me], sems[0].at[r - 1], sems[1].at[r - 1], peer),
                        functools.partial(_remote, theirs, theirs, sems[0].at[r - 1], sems[1].at[r - 1], peer)))
        return pltpu.make_async_copy(cins[0], couts[0].at[me], sems[2].at[0]), out

    def start(self, cins, couts, sems):
        own, copies = self._copies(cins, couts, sems)
        own.start()
        for send, _ in copies:
            send.start()

    def finish(self, cins, couts, sems):
        own, copies = self._copies(cins, couts, sems)
        for send, recv in copies:
            recv().wait_recv()
            send.wait_send()
        own.wait()


class _Both:
    def __init__(self, a, b):
        self.a, self.b = a, b
        self.operands = list(a.operands) + list(b.operands)
        self.out_shape = list(a.out_shape) + list(b.out_shape)
        self.aliases = dict(a.aliases)
        self.aliases.update({len(a.operands) + k: len(a.out_shape) + v for k, v in b.aliases.items()})
        self.sems = list(a.sems) + list(b.sems)

    def _split(self, cins, couts, sems):
        na, ma, sa = len(self.a.operands), len(self.a.out_shape), len(self.a.sems)
        return (cins[:na], couts[:ma], sems[:sa]), (cins[na:], couts[ma:], sems[sa:])

    def start(self, cins, couts, sems):
        for plan, args in zip((self.a, self.b), self._split(cins, couts, sems)):
            plan.start(*args)

    def finish(self, cins, couts, sems):
        for plan, args in zip((self.a, self.b), self._split(cins, couts, sems)):
            plan.finish(*args)


def _pair_sum(name, specs, dws, got, place):
    n_mat = len(specs)

    def body(p_ref, *refs):
        t = pl.program_id(0)
        mine, theirs = refs[:n_mat], refs[n_mat:2 * n_mat]
        outs, owns = refs[2 * n_mat:3 * n_mat], refs[3 * n_mat:]
        for a, b, o, own in zip(mine, theirs, outs, owns):
            s = (a[...].astype(F32) + b[...].astype(F32)).astype(BF16)
            o[...] = s

            @pl.when(t == p_ref[1])
            def _():
                own[...] = s

    def mine_spec(spec):
        hr, cols = _half_shape(spec)
        if spec[2] == "col":
            return pl.BlockSpec((hr, cols), lambda t, p_ref: (p_ref[0], t))
        return pl.BlockSpec((hr, cols), lambda t, p_ref: (2 * t + p_ref[0], 0))

    def slot_spec(spec):
        return pl.BlockSpec((None, *_half_shape(spec)), lambda t, p_ref: (t, 0, 0))

    def own_spec(spec):
        return pl.BlockSpec((None, *_half_shape(spec)), lambda t, p_ref: (p_ref[1], 0, 0))

    slots = [_sds((N_CHIPS, *_half_shape(s)), BF16) for s in specs]
    grid_spec = pltpu.PrefetchScalarGridSpec(
        num_scalar_prefetch=1, grid=(N_CHIPS,),
        in_specs=[mine_spec(s) for s in specs] + [slot_spec(s) for s in specs],
        out_specs=[slot_spec(s) for s in specs] + [own_spec(s) for s in specs])
    res = pl.pallas_call(
        body, name=name, grid_spec=grid_spec, out_shape=slots + slots,
        compiler_params=_params(("arbitrary",)),
    )(place, *dws, *got)
    return list(res[:n_mat]), list(res[n_mat:])


class _ChipExchange:
    def __init__(self, sums, slots, part=0, parts=1):
        self.n = len(sums)
        self.part, self.parts = part, parts
        self.operands = list(sums) + list(slots)
        self.out_shape = [_sds(s.shape, s.dtype) for s in slots]
        self.aliases = {self.n + i: i for i in range(self.n)}
        self.sems = [pltpu.SemaphoreType.DMA((3 * self.n,)), pltpu.SemaphoreType.DMA((3 * self.n,))]

    def _rows(self, ref, slot):
        n = ref.shape[1] // self.parts
        return ref.at[slot, pl.ds(self.part * n, n), :]

    def _copies(self, cins, couts, sems):
        x, y, c, chips = _position()
        me = _shard_of((x, y))
        for i in range(self.n):
            for j, chip in enumerate(chips):
                k = 3 * i + j
                theirs = self._rows(couts[i], _shard_of(chip))
                yield (_remote(self._rows(cins[i], _shard_of(chip)), self._rows(couts[i], me),
                               sems[0].at[k], sems[1].at[k], (*chip, c)),
                       functools.partial(_remote, theirs, theirs, sems[0].at[k], sems[1].at[k], (*chip, c)))

    def start(self, cins, couts, sems):
        for send, _ in self._copies(cins, couts, sems):
            send.start()

    def finish(self, cins, couts, sems):
        for send, recv in self._copies(cins, couts, sems):
            recv().wait_recv()
            send.wait_send()


def _chip_sum(name, specs, slots, core):
    steps = 2
    n_mat = len(specs)

    def body(c_ref, *refs):
        del c_ref
        ins, outs = refs[:n_mat], refs[n_mat:]
        for a, o in zip(ins, outs):
            acc = a[0].astype(F32)
            for t in range(1, N_CHIPS):
                acc = acc + a[t].astype(F32)
            o[...] = acc

    def in_spec(spec):
        hr, cols = _half_shape(spec)
        return pl.BlockSpec((N_CHIPS, hr // steps, cols), lambda i, c_ref: (0, i, 0))

    def out_spec(spec):
        hr, cols = _half_shape(spec)
        return pl.BlockSpec((hr // steps, cols), lambda i, c_ref: (c_ref[0] * steps + i, 0))

    grid_spec = pltpu.PrefetchScalarGridSpec(
        num_scalar_prefetch=1, grid=(steps,),
        in_specs=[in_spec(s) for s in specs], out_specs=[out_spec(s) for s in specs])
    return list(pl.pallas_call(
        body, name=name, grid_spec=grid_spec,
        out_shape=[_sds((s[0], s[1]), F32) for s in specs],
        compiler_params=_params(("parallel",)),
    )(core, *slots))


class _HalfExchange:
    def __init__(self, grads, specs):
        self.specs = specs
        self.operands = list(grads)
        self.out_shape = [_sds(g.shape, g.dtype) for g in grads]
        self.aliases = {i: i for i in range(len(grads))}
        self.sems = [pltpu.SemaphoreType.DMA((len(grads),)), pltpu.SemaphoreType.DMA((len(grads),))]

    def _copies(self, couts, sems):
        x, y, c, _ = _position()
        sibling = (x, y, 1 - c)
        for i, spec in enumerate(self.specs):
            hr = spec[0] // 2
            mine = couts[i].at[pl.ds(_aligned(c * hr, 8), hr), :]
            theirs = couts[i].at[pl.ds(_aligned((1 - c) * hr, 8), hr), :]
            yield (_remote(mine, mine, sems[0].at[i], sems[1].at[i], sibling),
                   functools.partial(_remote, theirs, theirs, sems[0].at[i], sems[1].at[i], sibling))

    def start(self, cins, couts, sems):
        for send, _ in self._copies(couts, sems):
            send.start()

    def finish(self, cins, couts, sems):
        for send, recv in self._copies(couts, sems):
            recv().wait_recv()
            send.wait_send()


def _small_sum(blocks):
    def body(b_ref, o_ref):
        acc = b_ref[0]
        for d in range(1, N_DEV):
            acc = acc + b_ref[d]
        o_ref[...] = acc

    return pl.pallas_call(
        body, name="small_sum", out_shape=_sds((SMALL_ROWS, D_MODEL), F32), compiler_params=_params(),
    )(blocks)


def _adamw(name, params, steps):
    n = len(params)

    def body(*refs):
        for p in range(n):
            w_ref, g_ref, m_ref, v_ref = refs[4 * p:4 * p + 4]
            d_ref, nm_ref, nv_ref, go_ref = refs[4 * n + 4 * p:4 * n + 4 * p + 4]
            g = g_ref[...]
            go_ref[...] = g
            m = ADAM_B1 * m_ref[...] + (1.0 - ADAM_B1) * g
            v = ADAM_B2 * v_ref[...] + (1.0 - ADAM_B2) * jnp.square(g)
            m_hat = m / (1.0 - ADAM_B1 ** ADAM_STEP)
            v_hat = v / (1.0 - ADAM_B2 ** ADAM_STEP)
            d_ref[...] = -ADAM_LR * (m_hat / (jnp.sqrt(v_hat) + ADAM_EPS) + ADAM_WD * w_ref[...])
            nm_ref[...] = m
            nv_ref[...] = v

    in_specs, out_specs, out_shape, operands = [], [], [], []
    for w, g, m, v in params:
        spec = pl.BlockSpec((w.shape[0] // steps, w.shape[1]), lambda i: (i, 0))
        in_specs += [spec] * 4
        out_specs += [spec] * 4
        out_shape += [_sds(w.shape, F32)] * 4
        operands += [w, g, m, v]
    outs = _pcall(body, name, (steps,), in_specs, out_specs, out_shape, operands, (), ("parallel",))
    return [tuple(outs[4 * p:4 * p + 4]) for p in range(n)]


MATRIX_NAMES = tuple(MATRICES)
WEIGHT_ORDER = ("g_mix", "w_in", "conv_w", "attn_sinks", "w_conv_out", "w_attn_out", "w_o", "g_ffn",
                "w_gate_up", "w_down", "g_final")


def kernel(x, g_mix, w_in, conv_w, attn_sinks, w_conv_out, w_attn_out, w_o, g_ffn, w_gate_up, w_down, g_final, loss_target, m_g_mix, m_w_in, m_conv_w, m_attn_sinks, m_w_conv_out, m_w_attn_out, m_w_o, m_g_ffn, m_w_gate_up, m_w_down, m_g_final, v_g_mix, v_w_in, v_conv_w, v_attn_sinks, v_w_conv_out, v_w_attn_out, v_w_o, v_g_ffn, v_w_gate_up, v_w_down, v_g_final):
    w = dict(g_mix=g_mix, w_in=w_in[0], conv_w=conv_w[0], attn_sinks=attn_sinks, w_conv_out=w_conv_out[0],
             w_attn_out=w_attn_out[0], w_o=w_o[0], g_ffn=g_ffn, w_gate_up=w_gate_up[0], w_down=w_down[0],
             g_final=g_final[None, :])
    m = dict(g_mix=m_g_mix, w_in=m_w_in[0], conv_w=m_conv_w[0], attn_sinks=m_attn_sinks,
             w_conv_out=m_w_conv_out[0], w_attn_out=m_w_attn_out[0], w_o=m_w_o[0], g_ffn=m_g_ffn,
             w_gate_up=m_w_gate_up[0], w_down=m_w_down[0], g_final=m_g_final[None, :])
    v = dict(g_mix=v_g_mix, w_in=v_w_in[0], conv_w=v_conv_w[0], attn_sinks=v_attn_sinks,
             w_conv_out=v_w_conv_out[0], w_attn_out=v_w_attn_out[0], w_o=v_w_o[0], g_ffn=v_g_ffn,
             w_gate_up=v_w_gate_up[0], w_down=v_w_down[0], g_final=v_g_final[None, :])
    shard = (2 * lax.axis_index("x") + lax.axis_index("y")).astype(jnp.int32)
    core = lax.axis_index("c").astype(jnp.int32)
    shard1, core1, place = shard.reshape((1,)), core.reshape((1,)), jnp.stack([core, shard])
    spec = MATRICES
    xs, target, sinks = x[0], loss_target[0], w["attn_sinks"]
    tables = _rope_tables()

    def gather(names, part=0, parts=1):
        return _Gather([whole[n] for n in names], [(i, spec[n], part, parts) for i, n in enumerate(names)])

    def pair(names):
        return _Pair([dw[n] for n in names], [spec[n] for n in names])

    def pair_sum(tag, names, got):
        return _pair_sum("pair_sum_" + tag, [spec[n] for n in names], [dw[n] for n in names], got, place)

    whole = dict(zip(MATRIX_NAMES, _to_bf16_in_whole(
        [w[n] for n in MATRIX_NAMES], [spec[n] for n in MATRIX_NAMES], shard1)))

    mixers = ("w_conv_out", "w_attn_out", "w_o")
    h1 = _rms_norm("norm_mix", xs, w["g_mix"])
    proj, whole["w_in"], (*got, conv_w_whole) = _mm_in_gather(
        h1, whole["w_in"], _Gather([whole[n] for n in mixers], [(i, spec[n], 0, 1) for i, n in enumerate(mixers)],
                                   conv_w=w["conv_w"]))
    whole.update(zip(mixers, got))
    conv_y = _conv_fwd(proj, conv_w_whole)
    attn, (whole["w_gate_up"],) = _attn_fwd(proj, tables, sinks, comm=gather(("w_gate_up",), 0, 2))
    (conv_out, attn_out, merged), (whole["w_gate_up"],) = _branch_merge(
        conv_y, attn, whole["w_conv_out"], whole["w_attn_out"], proj, comm=gather(("w_gate_up",), 1, 2))
    x2, h2 = _mm_o_norm(merged, whole["w_o"], xs, w["g_ffn"])
    (gate, up, act), (whole["w_down"],) = _gate_up_fwd(h2, whole["w_gate_up"], comm=gather(("w_down",)))
    dx3, dx3b, dg_final, loss_row = _mm_down_loss(act, whole["w_down"], x2, w["g_final"], target)

    dw = {}
    dw["w_down"] = _mm_tn("mm_dw_down", act, dx3b, 1408, 1024, BF16)
    dgate, dup = _dact_swiglu(dx3b, whole["w_down"], gate, up)
    dw["w_gate_up"], got = _mm_dw_gate_up(h2, dgate, dup, comm=pair(("w_down",)))
    sums_a, own_a = pair_sum("down", ("w_down",), got)
    dh2, slots_a = _mm_dh2(dgate, dup, whole["w_gate_up"], comm=_ChipExchange(sums_a, own_a))
    (dx2, dx2b, dg_ffn), got_b = _rms_norm_bwd("norm_ffn_bwd", dh2, x2, w["g_ffn"], dx3, True,
                                               comm=pair(("w_gate_up",)))
    dw["w_o"] = _mm_tn("mm_dw_o", merged, dx2b, 1024, 1024, BF16)
    dco, dao, dgc, dga = _merge_bwd(dx2b, whole["w_o"], conv_out, attn_out, proj)
    dw["w_conv_out"] = _mm_tn("mm_dw_conv_out", conv_y, dco, 1024, 1024, BF16)
    dw["w_attn_out"] = _mm_tn("mm_dw_attn_out", attn, dao, 1024, 1024, BF16)
    (dcb, dcc, dcx, dconv_w), got_c = _conv_bwd(dco, whole["w_conv_out"], proj, conv_w_whole, comm=pair(mixers))
    sums_bc, own_bc = pair_sum("gate_up_mixers", ("w_gate_up",) + mixers, got_b + got_c)
    sums_b, own_b, sums_c, own_c = sums_bc[:1], own_bc[:1], sums_bc[1:], own_bc[1:]
    (dq, dk_prev, dk_cur, dv_prev, dv_cur, dsinks), slots_b = _attn_bwd(
        proj, dao, whole["w_attn_out"], sinks, tables, comm=_ChipExchange(sums_b, own_b))
    dkv = _kv_grad_combine(dk_prev, dk_cur, dv_prev, dv_cur, tables)
    dproj = jnp.concatenate([dcb, dcc, dcx, dq, dkv, dgc, dga], axis=1)
    dw["w_in"], got, slots_c = _mm_dw_in_pair(h1, dproj, _ChipExchange(sums_c, own_c))
    sums_d, own_d = pair_sum("in", ("w_in",), [got])
    early = ("w_down", "w_gate_up") + mixers
    halves = _chip_sum("chip_sum_early", [spec[n] for n in early], slots_a + slots_b + slots_c, core1)
    dh1, (own_d, *reduced) = _mm_nt(
        "mm_dh1", dproj, whole["w_in"], 1024, 1024, 1664, F32,
        comm=_Both(_ChipExchange(sums_d, own_d, 0, 2), _HalfExchange(halves, [spec[n] for n in early])))
    g = dict(zip(early, reduced))
    (grad_x, dg_mix), slots_d = _rms_norm_bwd("norm_mix_bwd", dh1, xs, w["g_mix"], dx2, False,
                                              comm=_ChipExchange(sums_d, [own_d], 1, 2))
    small = _pack_small(dg_mix, dg_ffn, dg_final, dconv_w, dsinks, loss_row)
    half_in = _chip_sum("chip_sum_in", [spec["w_in"]], slots_d, core1)
    g["w_in"], small_blocks = _comm_call(
        "half_exchange_in", _Both(_HalfExchange(half_in, [spec["w_in"]]), _SmallAllToAll(small)))
    delta, new_m, new_v = {}, {}, {}

    def keep(names, results):
        for n, (d, nm, nv, grad) in zip(names, results):
            delta[n], new_m[n], new_v[n], g[n] = d, nm, nv, grad

    keep(early, _adamw("adamw_early", [(w[n], g[n], m[n], v[n]) for n in early], 8))
    small_sum = _small_sum(small_blocks)
    g["g_mix"] = small_sum[0:1, :]
    g["g_ffn"] = small_sum[1:2, :]
    g["g_final"] = small_sum[2:3, :]
    g["conv_w"] = lax.dynamic_slice(small_sum, (3, shard * CONV_W_COLS), (3, CONV_W_COLS))
    g["attn_sinks"] = small_sum[6:7, :N_HEADS]
    loss = small_sum[7, 0]
    keep(("w_in",), _adamw("adamw_w_in", [(w["w_in"], g["w_in"], m["w_in"], v["w_in"])], 4))
    rest = ("g_mix", "g_ffn", "g_final", "conv_w", "attn_sinks")
    keep(rest, _adamw("adamw_small", [(w[n], g[n], m[n], v[n]) for n in rest], 1))

    def shaped(vals):
        return [vals[n].reshape((D_MODEL,)) if n == "g_final" else
                (vals[n][None] if n in MATRIX_NAMES or n == "conv_w" else vals[n]) for n in WEIGHT_ORDER]

    return (loss, grad_x[None], *shaped(g), *shaped(delta), *shaped(new_m), *shaped(new_v))
```

```python
import functools
import math

import jax
import jax.numpy as jnp
import numpy as np
from jax import lax
from jax.experimental import pallas as pl
from jax.experimental.pallas import tpu as pltpu

F32 = jnp.float32
BF16 = jnp.bfloat16

D_MODEL = 1024
SEQ = 2048
HEAD_DIM = 64
N_HEADS = 16
N_KV_HEADS = 4
GROUP = N_HEADS // N_KV_HEADS
D_ATTN = N_HEADS * HEAD_DIM
D_KV = N_KV_HEADS * HEAD_DIM
BLOCK = 128
ROT_DIM = HEAD_DIM // 4
ROPE_THETA = 500000.0
ATTN_SCALE = 1.0 / math.sqrt(HEAD_DIM)
NEG_INF = -1e30
D_FF = 2816
EPS = 1e-5
N_IN = 3 * D_MODEL + D_ATTN + 2 * D_KV + 2 * D_MODEL
COL_Q = 3 * D_MODEL
COL_K = COL_Q + D_ATTN
COL_V = COL_K + D_KV
COL_GC = COL_V + D_KV
COL_GA = COL_GC + D_MODEL

ADAM_LR = 0.001
ADAM_B1 = 0.9
ADAM_B2 = 0.999
ADAM_EPS = 1e-08
ADAM_WD = 0.01
ADAM_STEP = 10

N_CHIPS = 4
N_DEV = 8

V7X_VMEM_BYTES = 64 * 1024 * 1024
VMEM_LIMIT = (V7X_VMEM_BYTES * 3) // 4
LANES = 128
MESH = pl.DeviceIdType.MESH


def _params(semantics=None):
    return pltpu.CompilerParams(dimension_semantics=semantics, vmem_limit_bytes=VMEM_LIMIT)


def _sds(shape, dtype):
    return jax.ShapeDtypeStruct(shape, dtype)


HBM_SPEC = pl.BlockSpec(memory_space=pl.ANY)


def _pcall(body, name, grid, in_specs, out_specs, out_shape, operands, scratch=(), semantics=None, comm=None,
           aliases=None, start_after_body=False):
    aliases = dict(aliases or {})
    if comm is None:
        return pl.pallas_call(
            body, name=name, grid=grid, in_specs=in_specs, out_specs=out_specs, out_shape=out_shape,
            scratch_shapes=list(scratch), input_output_aliases=aliases,
            compiler_params=_params(semantics))(*operands)
    multi = isinstance(out_shape, (list, tuple))
    o_specs = list(out_specs) if multi else [out_specs]
    o_shape = list(out_shape) if multi else [out_shape]
    n_in, n_out, n_scr = len(operands), len(o_shape), len(scratch)
    n_cin, n_cout = len(comm.operands), len(comm.out_shape)

    def hosted(*refs):
        ins, cins = refs[:n_in], refs[n_in:n_in + n_cin]
        o0 = n_in + n_cin
        outs, couts = refs[o0:o0 + n_out], refs[o0 + n_out:o0 + n_out + n_cout]
        s0 = o0 + n_out + n_cout
        scr, sems = refs[s0:s0 + n_scr], refs[s0 + n_scr:]
        first = last = None
        for axis, size in enumerate(grid):
            i = pl.program_id(axis)
            first = (i == 0) if first is None else first & (i == 0)
            last = (i == size - 1) if last is None else last & (i == size - 1)

        if not start_after_body:
            @pl.when(first)
            def _():
                comm.start(cins, couts, sems)

        body(*ins, *outs, *scr)

        if start_after_body:
            @pl.when(first)
            def _():
                comm.start(cins, couts, sems)

        @pl.when(last)
        def _():
            comm.finish(cins, couts, sems)

    res = pl.pallas_call(
        hosted, name=name, grid=grid,
        in_specs=list(in_specs) + [HBM_SPEC] * n_cin, out_specs=o_specs + [HBM_SPEC] * n_cout,
        out_shape=o_shape + list(comm.out_shape), scratch_shapes=list(scratch) + list(comm.sems),
        input_output_aliases={**aliases, **{n_in + a: n_out + b for a, b in comm.aliases.items()}},
        compiler_params=_params(("arbitrary",) * len(grid)))(*operands, *comm.operands)
    outs = list(res[:n_out])
    return (outs if multi else outs[0]), list(res[n_out:])


def _comm_call(name, comm):
    def body(*refs):
        n_cin, n_cout = len(comm.operands), len(comm.out_shape)
        cins, couts, sems = refs[:n_cin], refs[n_cin:n_cin + n_cout], refs[n_cin + n_cout:]
        comm.start(cins, couts, sems)
        comm.finish(cins, couts, sems)

    return list(pl.pallas_call(
        body, name=name, in_specs=[HBM_SPEC] * len(comm.operands), out_specs=[HBM_SPEC] * len(comm.out_shape),
        out_shape=list(comm.out_shape), scratch_shapes=list(comm.sems),
        input_output_aliases=dict(comm.aliases))(*comm.operands))


NN = ((1,), (0,))
NT = ((1,), (1,))
TN = ((0,), (0,))


def _matmul(name, a, b, dims, grid, a_spec, b_spec, o_spec, o_shape, o_dtype, res=None, res_spec=None, comm=None):
    nk = grid[2]

    def body(*refs):
        if res is None:
            a_ref, b_ref, o_ref = refs[:3]
            r_ref = None
            scratch = refs[3:]
        else:
            a_ref, b_ref, r_ref, o_ref = refs[:4]
            scratch = refs[4:]
        p = lax.dot_general(a_ref[...], b_ref[...], (dims, ((), ())), preferred_element_type=F32)

        def finish(acc):
            if r_ref is not None:
                acc = r_ref[...] + acc
            o_ref[...] = acc.astype(o_dtype)

        if nk == 1:
            finish(p)
        else:
            acc_ref = o_ref if in_place else scratch[0]
            k = pl.program_id(2)

            @pl.when(k == 0)
            def _():
                acc_ref[...] = p

            @pl.when(k > 0)
            def _():
                acc_ref[...] += p

            if not in_place:
                @pl.when(k == nk - 1)
                def _():
                    finish(acc_ref[...])

    in_place = nk > 1 and res is None and o_dtype == F32
    operands = [a, b] if res is None else [a, b, res]
    in_specs = [a_spec, b_spec] if res is None else [a_spec, b_spec, res_spec]
    scratch = [pltpu.VMEM(o_spec.block_shape, F32)] if nk > 1 and not in_place else []
    return _pcall(body, name, grid, in_specs, o_spec, _sds(o_shape, o_dtype), operands, scratch,
                  ("parallel", "parallel", "arbitrary"), comm)


def _mm_nt(name, a, b, bm, bn, bk, o_dtype, comm=None):
    m, k = a.shape
    n = b.shape[0]
    return _matmul(
        name, a, b, NT, (m // bm, n // bn, k // bk),
        pl.BlockSpec((bm, bk), lambda i, j, kk: (i, kk)),
        pl.BlockSpec((bn, bk), lambda i, j, kk: (j, kk)),
        pl.BlockSpec((bm, bn), lambda i, j, kk: (i, j)),
        (m, n), o_dtype, comm=comm,
    )


def _mm_tn(name, a, b, bm, bn, o_dtype, comm=None):
    k, m = a.shape
    n = b.shape[1]
    return _matmul(
        name, a, b, TN, (m // bm, n // bn, 1),
        pl.BlockSpec((k, bm), lambda i, j, kk: (0, i)),
        pl.BlockSpec((k, bn), lambda i, j, kk: (0, j)),
        pl.BlockSpec((bm, bn), lambda i, j, kk: (i, j)),
        (m, n), o_dtype, comm=comm,
    )


ROWS = 256


def _row_spec(width, col=0):
    return pl.BlockSpec((ROWS, width), lambda i: (i, col))


def _full_spec(shape):
    return pl.BlockSpec(shape, lambda *_: (0,) * len(shape))


def _rms_norm(name, x, g):
    def body(x_ref, g_ref, h_ref):
        xf = x_ref[...]
        r = lax.rsqrt(jnp.mean(xf * xf, axis=-1, keepdims=True) + EPS)
        h_ref[...] = ((xf * r) * g_ref[...]).astype(BF16)

    return pl.pallas_call(
        body, name=name, grid=(SEQ // ROWS,),
        in_specs=[_row_spec(D_MODEL), _full_spec((1, D_MODEL))],
        out_specs=_row_spec(D_MODEL),
        out_shape=_sds((SEQ, D_MODEL), BF16),
        compiler_params=_params(("parallel",)),
    )(x, g)


CONV_COLS = 256


def _shift_rows(u, k):
    rows = lax.broadcasted_iota(jnp.int32, u.shape, 0)
    return jnp.where(rows >= k, pltpu.roll(u, k, axis=0), 0.0)


def _conv_fwd(proj, conv_w):
    nblk = D_MODEL // CONV_COLS

    def body(cb_ref, cc_ref, cx_ref, w_ref, y_ref):
        u = cc_ref[...] * cx_ref[...]
        w = w_ref[...]
        cv = w[0:1, :] * _shift_rows(u, 2) + w[1:2, :] * _shift_rows(u, 1) + w[2:3, :] * u
        y_ref[...] = (cb_ref[...] * cv).astype(BF16)

    def col(part):
        return pl.BlockSpec((SEQ, CONV_COLS), lambda j: (0, part * nblk + j))

    return pl.pallas_call(
        body, name="conv_fwd", grid=(nblk,),
        in_specs=[col(0), col(1), col(2), pl.BlockSpec((3, CONV_COLS), lambda j: (0, j))],
        out_specs=pl.BlockSpec((SEQ, CONV_COLS), lambda j: (0, j)),
        out_shape=_sds((SEQ, D_MODEL), BF16),
        compiler_params=_params(("parallel",)),
    )(proj, proj, proj, conv_w)


ROPE_COLS = 256


def _rope_tables():
    f32 = np.float32
    inv_freq = (f32(ROPE_THETA) ** (-np.arange(0, ROT_DIM, 2, dtype=f32) / f32(ROT_DIM))).astype(f32)
    ang = np.arange(SEQ, dtype=f32)[:, None] * inv_freq[None, :]
    cos, sin = np.cos(ang).astype(f32), np.sin(ang).astype(f32)
    half = ROT_DIM // 2
    ones = np.ones((SEQ, HEAD_DIM - ROT_DIM), f32)
    zeros = np.zeros((SEQ, HEAD_DIM - ROT_DIM), f32)
    zh = np.zeros((SEQ, half), f32)
    c = np.concatenate([cos, cos, ones], axis=1)
    s_up = np.concatenate([-sin, zh, zeros], axis=1)
    s_dn = np.concatenate([zh, sin, zeros], axis=1)
    reps = ROPE_COLS // HEAD_DIM
    return tuple(jnp.asarray(np.tile(t, (1, reps))) for t in (c, s_up, s_dn))


def _rotate(t, c, s_up, s_dn):
    width = t.shape[1]
    half = ROT_DIM // 2
    return t * c + pltpu.roll(t, width - half, axis=1) * s_up + pltpu.roll(t, half, axis=1) * s_dn


N_QBLK = SEQ // BLOCK


def _attn_specs():
    prev = lambda n: jnp.maximum(n - 1, 0)
    q = pl.BlockSpec((BLOCK, D_ATTN), lambda n: (n, COL_Q // D_ATTN))
    k_prev = pl.BlockSpec((BLOCK, D_KV), lambda n: (prev(n), COL_K // D_KV))
    k_cur = pl.BlockSpec((BLOCK, D_KV), lambda n: (n, COL_K // D_KV))
    v_prev = pl.BlockSpec((BLOCK, D_KV), lambda n: (prev(n), COL_V // D_KV))
    v_cur = pl.BlockSpec((BLOCK, D_KV), lambda n: (n, COL_V // D_KV))
    tab_cur = pl.BlockSpec((BLOCK, ROPE_COLS), lambda n: (n, 0))
    tab_prev = pl.BlockSpec((BLOCK, ROPE_COLS), lambda n: (prev(n), 0))
    return [q, k_prev, k_cur, v_prev, v_cur] + [tab_cur] * 3 + [tab_prev] * 3


def _band_kv(kp_ref, kc_ref, vp_ref, vc_ref, tabs_cur, tabs_prev):
    k = jnp.concatenate([_rotate(kp_ref[...], *(t[...] for t in tabs_prev)),
                         _rotate(kc_ref[...], *(t[...] for t in tabs_cur))], axis=0)
    v = jnp.concatenate([vp_ref[...], vc_ref[...]], axis=0)
    return k, v


def _query_tiles(q_ref, tiles, tabs_cur):
    c, su, sd = (t[:, :LANES] for t in tabs_cur)
    return jnp.concatenate(
        [_rotate(q_ref[:, t * LANES:(t + 1) * LANES], c, su, sd).astype(BF16) for t in tiles], axis=0)


def _sink_row(sink_ref, tiles, par):
    return jnp.concatenate([jnp.full((1, BLOCK), sink_ref[0, t * HEADS_PER_TILE + par], F32) for t in tiles], axis=1)


def _band_mask(n):
    kj = lax.broadcasted_iota(jnp.int32, (2 * BLOCK, BLOCK), 0)
    qi = lax.broadcasted_iota(jnp.int32, (2 * BLOCK, BLOCK), 1)
    rel = qi + BLOCK - kj
    return (rel >= 0) & (rel < BLOCK) & ((kj >= BLOCK) | (n > 0))


HEADS_PER_TILE = LANES // HEAD_DIM
TILES_PER_GROUP = GROUP // HEADS_PER_TILE


def _group_mask(n):
    return jnp.concatenate([_band_mask(n)] * TILES_PER_GROUP, axis=1)


def _lane_half(shape, par):
    lane = lax.broadcasted_iota(jnp.int32, shape, 1)
    return (lane < HEAD_DIM) if par == 0 else (lane >= HEAD_DIM)


def _head_tiles(kv, h):
    tile = kv[:, (h // HEADS_PER_TILE) * LANES:(h // HEADS_PER_TILE + 1) * LANES].astype(F32)
    own = jnp.where(_lane_half(tile.shape, h % HEADS_PER_TILE), tile, 0.0)
    other = pltpu.roll(own, HEAD_DIM, axis=1)
    lo, hi = (own, other) if h % HEADS_PER_TILE == 0 else (other, own)
    return lo.astype(BF16), hi.astype(BF16)


def _head_softmax(q_tile, k_half, sink, mask):
    s = lax.dot_general(k_half, q_tile, (NT, ((), ())), preferred_element_type=F32) * ATTN_SCALE
    s = jnp.where(mask, s, NEG_INF)
    m = jnp.maximum(jnp.max(s, axis=0, keepdims=True), sink)
    e = jnp.exp(s - m)
    es = jnp.exp(sink - m)
    inv = 1.0 / (jnp.sum(e, axis=0, keepdims=True) + es)
    return e * inv, es * inv


def _attn_fwd(proj, tables, sinks, comm=None):
    def body(sink_ref, q_ref, kp_ref, kc_ref, vp_ref, vc_ref, c_ref, su_ref, sd_ref, cp_ref, sup_ref, sdp_ref, o_ref):
        n = pl.program_id(0)
        mask = _group_mask(n)
        tabs_cur = (c_ref, su_ref, sd_ref)
        k, v = _band_kv(kp_ref, kc_ref, vp_ref, vc_ref, tabs_cur, (cp_ref, sup_ref, sdp_ref))
        for h in range(N_KV_HEADS):
            k_halves = _head_tiles(k, h)
            v_halves = _head_tiles(v, h)
            tiles = [h * TILES_PER_GROUP + t for t in range(TILES_PER_GROUP)]
            q_rows = _query_tiles(q_ref, tiles, tabs_cur)
            acc = None
            for par in range(HEADS_PER_TILE):
                p, _ = _head_softmax(q_rows, k_halves[par], _sink_row(sink_ref, tiles, par), mask)
                o = lax.dot_general(p.astype(BF16), v_halves[par], (TN, ((), ())), preferred_element_type=F32)
                acc = o if acc is None else acc + o
            for i, tile in enumerate(tiles):
                o_ref[:, tile * LANES:(tile + 1) * LANES] = acc[i * BLOCK:(i + 1) * BLOCK, :].astype(BF16)

    return _pcall(
        body, "attn_fwd", (N_QBLK,),
        [pl.BlockSpec(memory_space=pltpu.SMEM)] + _attn_specs(),
        pl.BlockSpec((BLOCK, D_ATTN), lambda n: (n, 0)),
        _sds((SEQ, D_ATTN), BF16), [sinks] + [proj] * 5 + list(tables) * 2, (), ("parallel",), comm)


def _branch_merge(conv_y, attn, w_co, w_ao, proj, comm=None):
    bm, bn = 1024, 512

    def body(cy_ref, at_ref, wc_ref, wa_ref, gc_ref, ga_ref, co_ref, ao_ref, mg_ref):
        co = jnp.dot(cy_ref[...], wc_ref[...], preferred_element_type=F32)
        ao = jnp.dot(at_ref[...], wa_ref[...], preferred_element_type=F32)
        co_ref[...] = co
        ao_ref[...] = ao
        mg_ref[...] = (jax.nn.sigmoid(gc_ref[...]) * co + jax.nn.sigmoid(ga_ref[...]) * ao).astype(BF16)

    act = pl.BlockSpec((bm, D_MODEL), lambda i, j: (i, 0))
    wgt = pl.BlockSpec((D_MODEL, bn), lambda i, j: (0, j))
    out = pl.BlockSpec((bm, bn), lambda i, j: (i, j))
    return _pcall(
        body, "branch_merge", (SEQ // bm, D_MODEL // bn),
        [act, act, wgt, wgt,
         pl.BlockSpec((bm, bn), lambda i, j: (i, COL_GC // bn + j)),
         pl.BlockSpec((bm, bn), lambda i, j: (i, COL_GA // bn + j))],
        [out, out, out],
        [_sds((SEQ, D_MODEL), F32), _sds((SEQ, D_MODEL), F32), _sds((SEQ, D_MODEL), BF16)],
        [conv_y, attn, w_co, w_ao, proj, proj], (), ("parallel", "parallel"), comm)


def _mm_o_norm(merged, w_o, x, g):
    bm = 1024

    def body(a_ref, w_ref, x_ref, g_ref, o_ref, h_ref):
        x2 = x_ref[...] + jnp.dot(a_ref[...], w_ref[...], preferred_element_type=F32)
        o_ref[...] = x2
        r = lax.rsqrt(jnp.mean(x2 * x2, axis=-1, keepdims=True) + EPS)
        h_ref[...] = ((x2 * r) * g_ref[...]).astype(BF16)

    row = pl.BlockSpec((bm, D_MODEL), lambda i: (i, 0))
    return pl.pallas_call(
        body, name="mm_o", grid=(SEQ // bm,),
        in_specs=[row, _full_spec((D_MODEL, D_MODEL)), row, _full_spec((1, D_MODEL))],
        out_specs=[row, row], out_shape=[_sds((SEQ, D_MODEL), F32), _sds((SEQ, D_MODEL), BF16)],
        compiler_params=_params(("parallel",)),
    )(merged, w_o, x, g)


FF_BM, FF_BN = 512, 1408
FF_NB = D_FF // FF_BN


def _gate_up_fwd(h2, w_gu, comm=None):
    def body(h_ref, wg_ref, wu_ref, g_ref, u_ref, a_ref):
        h = h_ref[...]
        g = jnp.dot(h, wg_ref[...], preferred_element_type=F32)
        u = jnp.dot(h, wu_ref[...], preferred_element_type=F32)
        g_ref[...] = g
        u_ref[...] = u
        a_ref[...] = (jax.nn.silu(g) * u).astype(BF16)

    out = pl.BlockSpec((FF_BM, FF_BN), lambda i, j: (i, j))
    f32, b16 = _sds((SEQ, D_FF), F32), _sds((SEQ, D_FF), BF16)
    return _pcall(
        body, "mm_gate_up", (SEQ // FF_BM, FF_NB),
        [pl.BlockSpec((FF_BM, D_MODEL), lambda i, j: (i, 0)),
         pl.BlockSpec((D_MODEL, FF_BN), lambda i, j: (0, j)),
         pl.BlockSpec((D_MODEL, FF_BN), lambda i, j: (0, FF_NB + j))],
        [out, out, out], [f32, f32, b16], [h2, w_gu, w_gu], (), ("parallel", "parallel"), comm)


def _dact_swiglu(dx3b, w_down, g, u):
    def body(dx_ref, w_ref, g_ref, u_ref, dg_ref, du_ref):
        da = lax.dot_general(dx_ref[...], w_ref[...], (NT, ((), ())), preferred_element_type=F32)
        g = g_ref[...]
        sg = jax.nn.sigmoid(g)
        dg_ref[...] = (da * u_ref[...] * (sg * (1.0 + g * (1.0 - sg)))).astype(BF16)
        du_ref[...] = (da * (g * sg)).astype(BF16)

    blk = pl.BlockSpec((FF_BM, FF_BN), lambda i, j: (i, j))
    b16 = _sds((SEQ, D_FF), BF16)
    return _pcall(
        body, "mm_dact", (SEQ // FF_BM, FF_NB),
        [pl.BlockSpec((FF_BM, D_MODEL), lambda i, j: (i, 0)), pl.BlockSpec((FF_BN, D_MODEL), lambda i, j: (j, 0)),
         blk, blk],
        [blk, blk], [b16, b16], [dx3b, w_down, g, u], (), ("parallel", "parallel"))


def _mm_dh2(dg, du, w_gu, comm=None):
    bm = 1024
    nk = 2 * FF_NB

    def body(dg_ref, du_ref, w_ref, o_ref):
        k = pl.program_id(1)

        def part(a_ref):
            return lax.dot_general(a_ref[...], w_ref[...], (NT, ((), ())), preferred_element_type=F32)

        @pl.when(k == 0)
        def _():
            o_ref[...] = part(dg_ref)

        @pl.when((k > 0) & (k < FF_NB))
        def _():
            o_ref[...] += part(dg_ref)

        @pl.when(k >= FF_NB)
        def _():
            o_ref[...] += part(du_ref)

    return _pcall(
        body, "mm_dh2", (SEQ // bm, nk),
        [pl.BlockSpec((bm, FF_BN), lambda i, k: (i, jnp.minimum(k, FF_NB - 1))),
         pl.BlockSpec((bm, FF_BN), lambda i, k: (i, jnp.maximum(k - FF_NB, 0))),
         pl.BlockSpec((D_MODEL, FF_BN), lambda i, k: (0, k))],
        pl.BlockSpec((bm, D_MODEL), lambda i, k: (i, 0)), _sds((SEQ, D_MODEL), F32),
        [dg, du, w_gu], (), ("parallel", "arbitrary"), comm)


def _mm_dw_gate_up(h2, dg, du, comm=None):
    def body(h_ref, dg_ref, du_ref, o_ref):
        j = pl.program_id(0)

        def part(b_ref):
            return lax.dot_general(h_ref[...], b_ref[...], (TN, ((), ())), preferred_element_type=F32).astype(BF16)

        @pl.when(j < FF_NB)
        def _():
            o_ref[...] = part(dg_ref)

        @pl.when(j >= FF_NB)
        def _():
            o_ref[...] = part(du_ref)

    return _pcall(
        body, "mm_dw_gate_up", (2 * FF_NB,),
        [_full_spec((SEQ, D_MODEL)),
         pl.BlockSpec((SEQ, FF_BN), lambda j: (0, jnp.minimum(j, FF_NB - 1))),
         pl.BlockSpec((SEQ, FF_BN), lambda j: (0, jnp.maximum(j - FF_NB, 0)))],
        pl.BlockSpec((D_MODEL, FF_BN), lambda j: (0, j)),
        _sds((D_MODEL, 2 * D_FF), BF16), [h2, dg, du], (), ("arbitrary",), comm)


def _mm_down_loss(act, w_down, x2, g, target):
    bm = 512

    def body(a_ref, w_ref, x_ref, g_ref, t_ref, dx_ref, dxb_ref, dg_ref, loss_ref):
        i = pl.program_id(0)
        xf = x_ref[...] + jnp.dot(a_ref[...], w_ref[...], preferred_element_type=F32)
        r = lax.rsqrt(jnp.mean(xf * xf, axis=-1, keepdims=True) + EPS)
        xn = xf * r
        gg = g_ref[...]
        err = xn * gg - t_ref[...]
        part = 0.5 * jnp.sum(jnp.mean(err * err, axis=-1, keepdims=True), axis=0, keepdims=True)
        dy = err * (1.0 / D_MODEL)
        dxn = dy * gg
        dx = r * (dxn - xn * jnp.mean(dxn * xn, axis=-1, keepdims=True))
        dx_ref[...] = dx
        dxb_ref[...] = dx.astype(BF16)
        dg = jnp.sum(dy * xn, axis=0, keepdims=True)
        lane0 = lax.broadcasted_iota(jnp.int32, (1, LANES), 1) == 0
        lpart = jnp.where(lane0, part, 0.0)

        @pl.when(i == 0)
        def _():
            dg_ref[...] = dg
            loss_ref[...] = lpart

        @pl.when(i > 0)
        def _():
            dg_ref[...] += dg
            loss_ref[...] += lpart

    row = pl.BlockSpec((bm, D_MODEL), lambda i: (i, 0))
    return pl.pallas_call(
        body, name="mm_down", grid=(SEQ // bm,),
        in_specs=[pl.BlockSpec((bm, D_FF), lambda i: (i, 0)), _full_spec((D_FF, D_MODEL)), row,
                  _full_spec((1, D_MODEL)), row],
        out_specs=[row, row, _full_spec((1, D_MODEL)), _full_spec((1, LANES))],
        out_shape=[_sds((SEQ, D_MODEL), F32), _sds((SEQ, D_MODEL), BF16),
                   _sds((1, D_MODEL), F32), _sds((1, LANES), F32)],
        compiler_params=_params(("arbitrary",)),
    )(act, w_down, x2, g, target)


def _rms_norm_bwd(name, dh, x, g, dres, with_bf16, comm=None):
    def body(dh_ref, x_ref, g_ref, dr_ref, *outs):
        i = pl.program_id(0)
        dx_ref = outs[0]
        dg_ref = outs[-1]
        xf = x_ref[...]
        r = lax.rsqrt(jnp.mean(xf * xf, axis=-1, keepdims=True) + EPS)
        xn = xf * r
        dh = dh_ref[...]
        dxn = dh * g_ref[...]
        dx = dr_ref[...] + r * (dxn - xn * jnp.mean(dxn * xn, axis=-1, keepdims=True))
        dx_ref[...] = dx
        if with_bf16:
            outs[1][...] = dx.astype(BF16)
        dg = jnp.sum(dh * xn, axis=0, keepdims=True)

        @pl.when(i == 0)
        def _():
            dg_ref[...] = dg

        @pl.when(i > 0)
        def _():
            dg_ref[...] += dg

    row = _row_spec(D_MODEL)
    out_specs = [row] + ([row] if with_bf16 else []) + [_full_spec((1, D_MODEL))]
    out_shape = ([_sds((SEQ, D_MODEL), F32)] + ([_sds((SEQ, D_MODEL), BF16)] if with_bf16 else [])
                 + [_sds((1, D_MODEL), F32)])
    return _pcall(body, name, (SEQ // ROWS,), [row, row, _full_spec((1, D_MODEL)), row], out_specs, out_shape,
                  [dh, x, g, dres], (), ("arbitrary",), comm)


def _merge_bwd(dx2b, w_o, conv_out, attn_out, proj):
    bm, bn = 1024, D_MODEL // 2

    def body(dx_ref, w_ref, co_ref, ao_ref, gc_ref, ga_ref, dco_ref, dao_ref, dgc_ref, dga_ref):
        dm = lax.dot_general(dx_ref[...], w_ref[...], (NT, ((), ())), preferred_element_type=F32)
        sc = jax.nn.sigmoid(gc_ref[...])
        sa = jax.nn.sigmoid(ga_ref[...])
        dco_ref[...] = (dm * sc).astype(BF16)
        dao_ref[...] = (dm * sa).astype(BF16)
        dgc_ref[...] = (dm * co_ref[...] * (sc * (1.0 - sc))).astype(BF16)
        dga_ref[...] = (dm * ao_ref[...] * (sa * (1.0 - sa))).astype(BF16)

    own = pl.BlockSpec((bm, bn), lambda i, j: (i, j))
    sd = _sds((SEQ, D_MODEL), BF16)
    return pl.pallas_call(
        body, name="mm_dmerged", grid=(SEQ // bm, D_MODEL // bn),
        in_specs=[pl.BlockSpec((bm, D_MODEL), lambda i, j: (i, 0)), pl.BlockSpec((bn, D_MODEL), lambda i, j: (j, 0)),
                  own, own,
                  pl.BlockSpec((bm, bn), lambda i, j: (i, COL_GC // bn + j)),
                  pl.BlockSpec((bm, bn), lambda i, j: (i, COL_GA // bn + j))],
        out_specs=[own, own, own, own], out_shape=[sd, sd, sd, sd],
        compiler_params=_params(("parallel", "parallel")),
    )(dx2b, w_o, conv_out, attn_out, proj, proj)


def _conv_bwd(dco, w_co, proj, conv_w, comm=None):
    nblk = D_MODEL // CONV_COLS

    def body(dco_ref, wco_ref, cb_ref, cc_ref, cx_ref, w_ref, dcb_ref, dcc_ref, dcx_ref, dw_ref):
        cc = cc_ref[...]
        cx = cx_ref[...]
        u = cc * cx
        w = w_ref[...]
        u1 = _shift_rows(u, 1)
        u2 = _shift_rows(u, 2)
        cv = w[0:1, :] * u2 + w[1:2, :] * u1 + w[2:3, :] * u
        dy = lax.dot_general(dco_ref[...], wco_ref[...], (NT, ((), ())), preferred_element_type=F32)
        dcb_ref[...] = (dy * cv).astype(BF16)
        dcv = dy * cb_ref[...]
        rows = lax.broadcasted_iota(jnp.int32, dcv.shape, 0)
        up1 = jnp.where(rows < SEQ - 1, pltpu.roll(dcv, SEQ - 1, axis=0), 0.0)
        up2 = jnp.where(rows < SEQ - 2, pltpu.roll(dcv, SEQ - 2, axis=0), 0.0)
        du = w[2:3, :] * dcv + w[1:2, :] * up1 + w[0:1, :] * up2
        dcc_ref[...] = (du * cx).astype(BF16)
        dcx_ref[...] = (du * cc).astype(BF16)
        dw_ref[...] = jnp.concatenate(
            [jnp.sum(dcv * u2, axis=0, keepdims=True),
             jnp.sum(dcv * u1, axis=0, keepdims=True),
             jnp.sum(dcv * u, axis=0, keepdims=True)], axis=0)

    def col(part):
        return pl.BlockSpec((SEQ, CONV_COLS), lambda j: (0, part * nblk + j))

    own = pl.BlockSpec((SEQ, CONV_COLS), lambda j: (0, j))
    wsp = pl.BlockSpec((3, CONV_COLS), lambda j: (0, j))
    sd = _sds((SEQ, D_MODEL), BF16)
    return _pcall(
        body, "conv_bwd", (nblk,),
        [_full_spec((SEQ, D_MODEL)), pl.BlockSpec((CONV_COLS, D_MODEL), lambda j: (j, 0)), col(0), col(1), col(2), wsp],
        [own, own, own, wsp], [sd, sd, sd, _sds((3, D_MODEL), F32)],
        [dco, w_co, proj, proj, proj, conv_w], (), ("parallel",), comm)


def _attn_bwd(proj, dao, w_ao, sinks, tables, comm=None):
    def body(sink_ref, q_ref, kp_ref, kc_ref, vp_ref, vc_ref, c_ref, su_ref, sd_ref, cp_ref, sup_ref, sdp_ref,
             dao_ref, wao_ref, dq_ref, dkp_ref, dkc_ref, dvp_ref, dvc_ref, ds_ref):
        n = pl.program_id(0)
        mask = _group_mask(n)
        tabs_cur = (c_ref, su_ref, sd_ref)
        k, v = _band_kv(kp_ref, kc_ref, vp_ref, vc_ref, tabs_cur, (cp_ref, sup_ref, sdp_ref))
        do = lax.dot_general(dao_ref[...], wao_ref[...], (NT, ((), ())), preferred_element_type=F32).astype(BF16)
        lane = lax.broadcasted_iota(jnp.int32, (1, LANES), 1)
        dsink = jnp.zeros((1, LANES), F32)
        c, su, sd = c_ref[:, :LANES], su_ref[:, :LANES], sd_ref[:, :LANES]
        dk_tiles = [None] * (N_KV_HEADS // HEADS_PER_TILE)
        dv_tiles = [None] * (N_KV_HEADS // HEADS_PER_TILE)
        for h in range(N_KV_HEADS):
            k_halves = _head_tiles(k, h)
            v_halves = _head_tiles(v, h)
            tiles = [h * TILES_PER_GROUP + t for t in range(TILES_PER_GROUP)]
            q_rows = _query_tiles(q_ref, tiles, tabs_cur)
            do_rows = jnp.concatenate([do[:, t * LANES:(t + 1) * LANES] for t in tiles], axis=0)
            dk_par, dv_par = [], []
            dq_rows = None
            for par in range(HEADS_PER_TILE):
                p, p_sink = _head_softmax(q_rows, k_halves[par], _sink_row(sink_ref, tiles, par), mask)
                dp = lax.dot_general(v_halves[par], do_rows, (NT, ((), ())), preferred_element_type=F32)
                delta = jnp.sum(p * dp, axis=0, keepdims=True)
                ds = (p * (dp - delta) * ATTN_SCALE).astype(BF16)
                dq = lax.dot_general(ds, k_halves[par], (TN, ((), ())), preferred_element_type=F32)
                dq_rows = dq if dq_rows is None else dq_rows + dq
                dk_par.append(jnp.dot(ds, q_rows, preferred_element_type=F32))
                dv_par.append(jnp.dot(p.astype(BF16), do_rows, preferred_element_type=F32))
                sink_grad = p_sink * delta
                for i, tile in enumerate(tiles):
                    val = -jnp.sum(sink_grad[:, i * BLOCK:(i + 1) * BLOCK], axis=1, keepdims=True)
                    dsink = dsink + jnp.where(lane == tile * HEADS_PER_TILE + par, val, 0.0)
            for i, tile in enumerate(tiles):
                dq_tile = dq_rows[i * BLOCK:(i + 1) * BLOCK, :]
                dq_ref[:, tile * LANES:(tile + 1) * LANES] = _rotate(dq_tile, c, -su, -sd).astype(BF16)
            own = h % HEADS_PER_TILE
            for par_grads, tiles in ((dk_par, dk_tiles), (dv_par, dv_tiles)):
                shifted = pltpu.roll(par_grads[1 - own], HEAD_DIM, axis=1)
                total = jnp.where(_lane_half(shifted.shape, own), par_grads[own] + shifted, 0.0)
                i = h // HEADS_PER_TILE
                tiles[i] = total if tiles[i] is None else tiles[i] + total
        for i in range(N_KV_HEADS // HEADS_PER_TILE):
            cols = slice(i * LANES, (i + 1) * LANES)
            dkp_ref[:, cols] = dk_tiles[i][:BLOCK, :]
            dkc_ref[:, cols] = dk_tiles[i][BLOCK:, :]
            dvp_ref[:, cols] = dv_tiles[i][:BLOCK, :]
            dvc_ref[:, cols] = dv_tiles[i][BLOCK:, :]

        @pl.when(n == 0)
        def _():
            ds_ref[...] = dsink

        @pl.when(n > 0)
        def _():
            ds_ref[...] += dsink

    blk = pl.BlockSpec((BLOCK, D_KV), lambda n: (n, 0))
    prev_blk = pl.BlockSpec((BLOCK, D_KV), lambda n: ((n + N_QBLK - 1) % N_QBLK, 0))
    kv = _sds((SEQ, D_KV), F32)
    return _pcall(
        body, "attn_bwd", (N_QBLK,),
        [pl.BlockSpec(memory_space=pltpu.SMEM)] + _attn_specs()
        + [pl.BlockSpec((BLOCK, D_MODEL), lambda n: (n, 0)), _full_spec((D_ATTN, D_MODEL))],
        [pl.BlockSpec((BLOCK, D_ATTN), lambda n: (n, 0)), prev_blk, blk, prev_blk, blk, _full_spec((1, LANES))],
        [_sds((SEQ, D_ATTN), BF16), kv, kv, kv, kv, _sds((1, LANES), F32)],
        [sinks] + [proj] * 5 + list(tables) * 2 + [dao, w_ao], (), ("arbitrary",), comm)


def _kv_grad_combine(dk_prev, dk_cur, dv_prev, dv_cur, tables):
    rows = 4 * BLOCK

    def body(kp_ref, kc_ref, vp_ref, vc_ref, c_ref, su_ref, sd_ref, o_ref):
        dk = kc_ref[...] + kp_ref[...]
        dv = vc_ref[...] + vp_ref[...]
        o_ref[:, :D_KV] = _rotate(dk, c_ref[...], -su_ref[...], -sd_ref[...]).astype(BF16)
        o_ref[:, D_KV:] = dv.astype(BF16)

    blk = pl.BlockSpec((rows, D_KV), lambda m: (m, 0))
    return pl.pallas_call(
        body, name="kv_grad_combine", grid=(SEQ // rows,),
        in_specs=[blk] * 7,
        out_specs=pl.BlockSpec((rows, 2 * D_KV), lambda m: (m, 0)),
        out_shape=_sds((SEQ, 2 * D_KV), BF16),
        compiler_params=_params(("parallel",)),
    )(dk_prev, dk_cur, dv_prev, dv_cur, *tables)


MATRICES = {
    "w_in": (D_MODEL, N_IN // N_CHIPS, "col"),
    "w_conv_out": (D_MODEL // N_CHIPS, D_MODEL, "row"),
    "w_attn_out": (D_MODEL // N_CHIPS, D_MODEL, "row"),
    "w_o": (D_MODEL // N_CHIPS, D_MODEL, "row"),
    "w_gate_up": (D_MODEL, 2 * D_FF // N_CHIPS, "col"),
    "w_down": (D_FF // N_CHIPS, D_MODEL, "row"),
}
BF16_ROW_TILE = 16
CONV_W_COLS = D_MODEL // N_CHIPS
SMALL_ROWS = 8


def _whole_shape(spec):
    rows, cols, kind = spec
    return (rows, cols * N_CHIPS) if kind == "col" else (rows * N_CHIPS, cols)


def _half_shape(spec):
    return (spec[0] // 2, spec[1])


def _aligned(start, multiple):
    return start if isinstance(start, int) else pl.multiple_of(start, multiple)


def _region(ref, spec, shard, half, part=0, parts=1):
    rows, cols, kind = spec
    hr = rows // 2
    n = hr // parts
    if kind == "col":
        return ref.at[pl.ds(_aligned(half * hr + part * n, BF16_ROW_TILE), n),
                      pl.ds(_aligned(shard * cols, LANES), cols)]
    return ref.at[pl.ds(_aligned(shard * rows + half * hr + part * n, BF16_ROW_TILE), n), :]


def _position():
    x, y, c = lax.axis_index("x"), lax.axis_index("y"), lax.axis_index("c")
    chips = [(1 - x, y), (x, 1 - y), (1 - x, 1 - y)]
    return x, y, c, chips


def _shard_of(chip):
    return 2 * chip[0] + chip[1]


def _remote(src, dst, send_sem, recv_sem, to):
    return pltpu.make_async_remote_copy(src_ref=src, dst_ref=dst, send_sem=send_sem, recv_sem=recv_sem,
                                        device_id=to, device_id_type=MESH)


CAST_STEPS = 4


def _to_bf16_in_whole(ws, specs, shard):
    n = len(ws)

    def body(s_ref, *refs):
        del s_ref
        for w_ref, o_ref in zip(refs[:n], refs[n:]):
            o_ref[...] = w_ref[...].astype(BF16)

    def out_spec(spec):
        rows = spec[0] // CAST_STEPS
        if spec[2] == "col":
            return pl.BlockSpec((rows, spec[1]), lambda i, s_ref: (i, s_ref[0]))
        return pl.BlockSpec((rows, spec[1]), lambda i, s_ref: (s_ref[0] * CAST_STEPS + i, 0))

    grid_spec = pltpu.PrefetchScalarGridSpec(
        num_scalar_prefetch=1, grid=(CAST_STEPS,),
        in_specs=[pl.BlockSpec((s[0] // CAST_STEPS, s[1]), lambda i, s_ref: (i, 0)) for s in specs],
        out_specs=[out_spec(s) for s in specs])
    return list(pl.pallas_call(
        body, name="cast_shards", grid_spec=grid_spec, out_shape=[_sds(_whole_shape(s), BF16) for s in specs],
        compiler_params=_params(("parallel",)),
    )(shard, *ws))


class _Gather:
    def __init__(self, wholes, pieces, conv_w=None):
        self.pieces = pieces
        self.n = len(wholes)
        self.with_conv_w = conv_w is not None
        self.operands = list(wholes) + ([conv_w] if self.with_conv_w else [])
        self.out_shape = [_sds(w.shape, w.dtype) for w in wholes]
        if self.with_conv_w:
            self.out_shape.append(_sds((3, D_MODEL), F32))
        self.aliases = {i: i for i in range(self.n)}
        n_ici = 3 * len(pieces)
        self.sems = [pltpu.SemaphoreType.DMA((n_ici,))] * 4
        if self.with_conv_w:
            self.sems += [pltpu.SemaphoreType.DMA((1,)), pltpu.SemaphoreType.DMA((3,)), pltpu.SemaphoreType.DMA((3,))]

    def _conv_w(self, cins, couts, sems, with_recvs):
        cw_in, cw_out = cins[self.n], couts[self.n]
        x, y, c, chips = _position()

        def cols(shard):
            return cw_out.at[:, pl.ds(_aligned(shard * CONV_W_COLS, LANES), CONV_W_COLS)]

        me = _shard_of((x, y))
        local = pltpu.make_async_copy(cw_in, cols(me), sems[4].at[0])
        sends = [_remote(cw_in, cols(me), sems[5].at[j], sems[6].at[j], (*chip, c)) for j, chip in enumerate(chips)]
        if not with_recvs:
            return local, sends, []
        recvs = [_remote(cols(_shard_of(chip)), cols(_shard_of(chip)), sems[5].at[j], sems[6].at[j], (*chip, c))
                 for j, chip in enumerate(chips)]
        return local, sends, recvs

    def start(self, cins, couts, sems):
        x, y, c, chips = _position()
        me = _shard_of((x, y))
        if self.with_conv_w:
            local, sends, _ = self._conv_w(cins, couts, sems, False)
            local.start()
            for cp in sends:
                cp.start()
        for p, (i, spec, part, parts) in enumerate(self.pieces):
            mine = _region(couts[i], spec, me, c, part, parts)
            for j, chip in enumerate(chips):
                _remote(mine, mine, sems[0].at[3 * p + j], sems[1].at[3 * p + j], (*chip, c)).start()

    def finish(self, cins, couts, sems):
        x, y, c, chips = _position()
        me = _shard_of((x, y))
        sibling = (x, y, 1 - c)
        send_a, recv_a, send_b, recv_b = sems[:4]
        passed = []
        for p, (i, spec, part, parts) in enumerate(self.pieces):
            for j, chip in enumerate(chips):
                k = 3 * p + j
                landed = _region(couts[i], spec, _shard_of(chip), c, part, parts)
                _remote(landed, landed, send_a.at[k], recv_a.at[k], (*chip, c)).wait_recv()
                cp = _remote(landed, landed, send_b.at[k], recv_b.at[k], sibling)
                cp.start()
                passed.append(cp)
        for p, (i, spec, part, parts) in enumerate(self.pieces):
            mine = _region(couts[i], spec, me, c, part, parts)
            for j, chip in enumerate(chips):
                k = 3 * p + j
                other = _region(couts[i], spec, _shard_of(chip), 1 - c, part, parts)
                _remote(other, other, send_b.at[k], recv_b.at[k], sibling).wait_recv()
                _remote(mine, mine, send_a.at[k], recv_a.at[k], (*chip, c)).wait_send()
        for cp in passed:
            cp.wait_send()
        if self.with_conv_w:
            local, sends, recvs = self._conv_w(cins, couts, sems, True)
            for cp in recvs:
                cp.wait_recv()
            for cp in sends:
                cp.wait_send()
            local.wait()


def _mm_in_gather(h1, w_whole, comm):
    spec = MATRICES["w_in"]
    cols = spec[1]
    bm = SEQ // 2

    def body(h_ref, w_in_ref, proj_ref, w_ref, wbuf, obuf, send_a, recv_a, send_b, recv_b, load_sem, store_sems):
        del w_in_ref
        s, mi = pl.program_id(0), pl.program_id(1)
        x, y, c, chips = _position()
        me = _shard_of((x, y))
        sibling = (x, y, 1 - c)
        mine = _region(w_ref, spec, me, c)

        @pl.when((s == 0) & (mi == 0))
        def _():
            for j, chip in enumerate(chips):
                _remote(mine, mine, send_a.at[j], recv_a.at[j], (*chip, c)).start()

        shard = me
        for j, chip in enumerate(chips):
            shard = jnp.where(s == j + 1, _shard_of(chip), shard)

            @pl.when((s == j + 1) & (mi == 0))
            def _():
                landed = _region(w_ref, spec, _shard_of(chip), c)
                _remote(landed, landed, send_a.at[j], recv_a.at[j], (*chip, c)).wait_recv()
                _remote(landed, landed, send_b.at[j], recv_b.at[j], sibling).start()
                other = _region(w_ref, spec, _shard_of(chip), 1 - c)
                _remote(other, other, send_b.at[j], recv_b.at[j], sibling).wait_recv()

        col0 = pl.multiple_of(shard * cols, LANES)

        @pl.when(mi == 0)
        def _():
            load = pltpu.make_async_copy(w_ref.at[:, pl.ds(col0, cols)], wbuf, load_sem.at[0])
            load.start()
            load.wait()

        def store():
            rows = pl.ds(pl.multiple_of(mi * bm, bm), bm)
            return pltpu.make_async_copy(obuf.at[mi], proj_ref.at[rows, pl.ds(col0, cols)], store_sems.at[mi])

        @pl.when(s > 0)
        def _():
            store().wait()

        obuf[mi] = jnp.dot(h_ref[...], wbuf[...], preferred_element_type=F32)
        store().start()

        @pl.when(s == N_CHIPS - 1)
        def _():
            store().wait()

        @pl.when((s == N_CHIPS - 1) & (mi == 1))
        def _():
            for j, chip in enumerate(chips):
                landed = _region(w_ref, spec, _shard_of(chip), c)
                _remote(mine, mine, send_a.at[j], recv_a.at[j], (*chip, c)).wait_send()
                _remote(landed, landed, send_b.at[j], recv_b.at[j], sibling).wait_send()

    sem3 = pltpu.SemaphoreType.DMA((3,))
    (proj, whole), extra = _pcall(
        body, "mm_in", (N_CHIPS, SEQ // bm),
        [pl.BlockSpec((bm, D_MODEL), lambda s, m: (m, 0)), HBM_SPEC], [HBM_SPEC, HBM_SPEC],
        [_sds((SEQ, N_IN), F32), _sds(w_whole.shape, w_whole.dtype)], [h1, w_whole],
        [pltpu.VMEM((D_MODEL, cols), BF16), pltpu.VMEM((SEQ // bm, bm, cols), F32), sem3, sem3, sem3, sem3,
         pltpu.SemaphoreType.DMA((1,)), pltpu.SemaphoreType.DMA((SEQ // bm,))],
        None, comm, aliases={1: 1}, start_after_body=True)
    return proj, whole, extra


def _mm_dw_in_pair(h1, dproj, comm):
    spec = MATRICES["w_in"]
    rows, cols, _ = spec
    hr = rows // 2

    def body(h_ref, dp_ref, dw_ref, got_ref, obuf, store_sems, send_sems, recv_sems):
        t = pl.program_id(0)
        x, y, c, _ = _position()
        sibling = (x, y, 1 - c)

        def store(step):
            return pltpu.make_async_copy(obuf.at[step % 2], dw_ref.at[:, pl.ds(step * cols, cols)],
                                         store_sems.at[step % 2])

        def send(step):
            theirs = obuf.at[step % 2, pl.ds(_aligned((1 - c) * hr, BF16_ROW_TILE), hr), :]
            return _remote(theirs, got_ref.at[step], send_sems.at[step], recv_sems.at[step], sibling)

        for step in range(N_CHIPS):
            @pl.when(t == step)
            def _():
                if step >= 2:
                    store(step - 2).wait()
                    send(step - 2).wait_send()
                obuf[step % 2] = lax.dot_general(
                    h_ref[...], dp_ref[...], (TN, ((), ())), preferred_element_type=F32).astype(BF16)
                store(step).start()
                send(step).start()

        @pl.when(t == N_CHIPS - 1)
        def _():
            for step in (N_CHIPS - 2, N_CHIPS - 1):
                store(step).wait()
                send(step).wait_send()
            for step in range(N_CHIPS):
                send(step).wait_recv()

    sem4 = pltpu.SemaphoreType.DMA((N_CHIPS,))
    (dw_in, got), extra = _pcall(
        body, "mm_dw_in", (N_CHIPS,),
        [_full_spec((SEQ, D_MODEL)), pl.BlockSpec((SEQ, cols), lambda t: (0, t))], [HBM_SPEC, HBM_SPEC],
        [_sds(_whole_shape(spec), BF16), _sds((N_CHIPS, hr, cols), BF16)], [h1, dproj],
        [pltpu.VMEM((2, rows, cols), BF16), pltpu.SemaphoreType.DMA((2,)), sem4, sem4], None, comm)
    return dw_in, got, extra


def _pack_small(dg_mix, dg_ffn, dg_final, dconv_w, dsinks, loss_row):
    def body(a_ref, b_ref, c_ref, w_ref, s_ref, l_ref, o_ref):
        pad = jnp.zeros((1, D_MODEL - LANES), F32)
        o_ref[0:1, :] = a_ref[...]
        o_ref[1:2, :] = b_ref[...]
        o_ref[2:3, :] = c_ref[...]
        o_ref[3:6, :] = w_ref[...]
        o_ref[6:7, :] = jnp.concatenate([s_ref[...], pad], axis=1)
        o_ref[7:8, :] = jnp.concatenate([l_ref[...], pad], axis=1)

    return pl.pallas_call(
        body, name="pack_small", out_shape=_sds((SMALL_ROWS, D_MODEL), F32),
        compiler_params=_params(),
    )(dg_mix, dg_ffn, dg_final, dconv_w, dsinks, loss_row)


class _Pair:
    def __init__(self, dws, specs):
        self.specs = specs
        self.operands = list(dws)
        self.out_shape = [_sds((N_CHIPS, *_half_shape(s)), BF16) for s in specs]
        self.aliases = {}
        n = N_CHIPS * len(specs)
        self.sems = [pltpu.SemaphoreType.DMA((n,)), pltpu.SemaphoreType.DMA((n,))]

    def _copies(self, cins, couts, sems):
        x, y, c, _ = _position()
        sibling = (x, y, 1 - c)
        for i, spec in enumerate(self.specs):
            for t in range(N_CHIPS):
                k = N_CHIPS * i + t
                yield _remote(_region(cins[i], spec, t, 1 - c), couts[i].at[t], sems[0].at[k], sems[1].at[k], sibling)

    def start(self, cins, couts, sems):
        for cp in self._copies(cins, couts, sems):
            cp.start()

    def finish(self, cins, couts, sems):
        for cp in self._copies(cins, couts, sems):
            cp.wait()


class _SmallAllToAll:
    def __init__(self, small):
        self.operands = [small]
        self.out_shape = [_sds((N_DEV, SMALL_ROWS, D_MODEL), F32)]
        self.aliases = {}
        self.sems = [pltpu.SemaphoreType.DMA((N_DEV - 1,)), pltpu.SemaphoreType.DMA((N_DEV - 1,)),
                     pltpu.SemaphoreType.DMA((1,))]

    def _copies(self, cins, couts, sems):
        x, y, c, _ = _position()
        me = 4 * x + 2 * y + c
        out = []
        for r in range(1, N_DEV):
            flip = ((r >> 2) & 1, (r >> 1) & 1, r & 1)
            peer = tuple(1 - p if f else p for p, f in zip((x, y, c), flip))
            theirs = couts[0].at[4 * peer[0] + 2 * peer[1] + peer[2]]
            out.append((_remote(cins[0], couts[0].at[me], sems[0].at[r - 1], sems[1].at[r - 1], peer),
                        functools.partial(_remote, theirs, theirs, sems[0].at[r - 1], sems[1].at[r - 1], peer)))
        return pltpu.make_async_copy(cins[0], couts[0].at[me], sems[2].at[0]), out

    def start(self, cins, couts, sems):
        own, copies = self._copies(cins, couts, sems)
        own.start()
        for send, _ in copies:
            send.start()

    def finish(self, cins, couts, sems):
        own, copies = self._copies(cins, couts, sems)
        for send, recv in copies:
            recv().wait_recv()
            send.wait_send()
        own.wait()


class _Both:
    def __init__(self, a, b):
        self.a, self.b = a, b
        self.operands = list(a.operands) + list(b.operands)
        self.out_shape = list(a.out_shape) + list(b.out_shape)
        self.aliases = dict(a.aliases)
        self.aliases.update({len(a.operands) + k: len(a.out_shape) + v for k, v in b.aliases.items()})
        self.sems = list(a.sems) + list(b.sems)

    def _split(self, cins, couts, sems):
        na, ma, sa = len(self.a.operands), len(self.a.out_shape), len(self.a.sems)
        return (cins[:na], couts[:ma], sems[:sa]), (cins[na:], couts[ma:], sems[sa:])

    def start(self, cins, couts, sems):
        for plan, args in zip((self.a, self.b), self._split(cins, couts, sems)):
            plan.start(*args)

    def finish(self, cins, couts, sems):
        for plan, args in zip((self.a, self.b), self._split(cins, couts, sems)):
            plan.finish(*args)


def _pair_sum(name, specs, dws, got, place):
    n_mat = len(specs)

    def body(p_ref, *refs):
        t = pl.program_id(0)
        mine, theirs = refs[:n_mat], refs[n_mat:2 * n_mat]
        outs, owns = refs[2 * n_mat:3 * n_mat], refs[3 * n_mat:]
        for a, b, o, own in zip(mine, theirs, outs, owns):
            s = (a[...].astype(F32) + b[...].astype(F32)).astype(BF16)
            o[...] = s

            @pl.when(t == p_ref[1])
            def _():
                own[...] = s

    def mine_spec(spec):
        hr, cols = _half_shape(spec)
        if spec[2] == "col":
            return pl.BlockSpec((hr, cols), lambda t, p_ref: (p_ref[0], t))
        return pl.BlockSpec((hr, cols), lambda t, p_ref: (2 * t + p_ref[0], 0))

    def slot_spec(spec):
        return pl.BlockSpec((None, *_half_shape(spec)), lambda t, p_ref: (t, 0, 0))

    def own_spec(spec):
        return pl.BlockSpec((None, *_half_shape(spec)), lambda t, p_ref: (p_ref[1], 0, 0))

    slots = [_sds((N_CHIPS, *_half_shape(s)), BF16) for s in specs]
    grid_spec = pltpu.PrefetchScalarGridSpec(
        num_scalar_prefetch=1, grid=(N_CHIPS,),
        in_specs=[mine_spec(s) for s in specs] + [slot_spec(s) for s in specs],
        out_specs=[slot_spec(s) for s in specs] + [own_spec(s) for s in specs])
    res = pl.pallas_call(
        body, name=name, grid_spec=grid_spec, out_shape=slots + slots,
        compiler_params=_params(("arbitrary",)),
    )(place, *dws, *got)
    return list(res[:n_mat]), list(res[n_mat:])


class _ChipExchange:
    def __init__(self, sums, slots, part=0, parts=1):
        self.n = len(sums)
        self.part, self.parts = part, parts
        self.operands = list(sums) + list(slots)
        self.out_shape = [_sds(s.shape, s.dtype) for s in slots]
        self.aliases = {self.n + i: i for i in range(self.n)}
        self.sems = [pltpu.SemaphoreType.DMA((3 * self.n,)), pltpu.SemaphoreType.DMA((3 * self.n,))]

    def _rows(self, ref, slot):
        n = ref.shape[1] // self.parts
        return ref.at[slot, pl.ds(self.part * n, n), :]

    def _copies(self, cins, couts, sems):
        x, y, c, chips = _position()
        me = _shard_of((x, y))
        for i in range(self.n):
            for j, chip in enumerate(chips):
                k = 3 * i + j
                theirs = self._rows(couts[i], _shard_of(chip))
                yield (_remote(self._rows(cins[i], _shard_of(chip)), self._rows(couts[i], me),
                               sems[0].at[k], sems[1].at[k], (*chip, c)),
                       functools.partial(_remote, theirs, theirs, sems[0].at[k], sems[1].at[k], (*chip, c)))

    def start(self, cins, couts, sems):
        for send, _ in self._copies(cins, couts, sems):
            send.start()

    def finish(self, cins, couts, sems):
        for send, recv in self._copies(cins, couts, sems):
            recv().wait_recv()
            send.wait_send()


def _chip_sum(name, specs, slots, core):
    steps = 2
    n_mat = len(specs)

    def body(c_ref, *refs):
        del c_ref
        ins, outs = refs[:n_mat], refs[n_mat:]
        for a, o in zip(ins, outs):
            acc = a[0].astype(F32)
            for t in range(1, N_CHIPS):
                acc = acc + a[t].astype(F32)
            o[...] = acc

    def in_spec(spec):
        hr, cols = _half_shape(spec)
        return pl.BlockSpec((N_CHIPS, hr // steps, cols), lambda i, c_ref: (0, i, 0))

    def out_spec(spec):
        hr, cols = _half_shape(spec)
        return pl.BlockSpec((hr // steps, cols), lambda i, c_ref: (c_ref[0] * steps + i, 0))

    grid_spec = pltpu.PrefetchScalarGridSpec(
        num_scalar_prefetch=1, grid=(steps,),
        in_specs=[in_spec(s) for s in specs], out_specs=[out_spec(s) for s in specs])
    return list(pl.pallas_call(
        body, name=name, grid_spec=grid_spec,
        out_shape=[_sds((s[0], s[1]), F32) for s in specs],
        compiler_params=_params(("parallel",)),
    )(core, *slots))


class _HalfExchange:
    def __init__(self, grads, specs):
        self.specs = specs
        self.operands = list(grads)
        self.out_shape = [_sds(g.shape, g.dtype) for g in grads]
        self.aliases = {i: i for i in range(len(grads))}
        self.sems = [pltpu.SemaphoreType.DMA((len(grads),)), pltpu.SemaphoreType.DMA((len(grads),))]

    def _copies(self, couts, sems):
        x, y, c, _ = _position()
        sibling = (x, y, 1 - c)
        for i, spec in enumerate(self.specs):
            hr = spec[0] // 2
            mine = couts[i].at[pl.ds(_aligned(c * hr, 8), hr), :]
            theirs = couts[i].at[pl.ds(_aligned((1 - c) * hr, 8), hr), :]
            yield (_remote(mine, mine, sems[0].at[i], sems[1].at[i], sibling),
                   functools.partial(_remote, theirs, theirs, sems[0].at[i], sems[1].at[i], sibling))

    def start(self, cins, couts, sems):
        for send, _ in self._copies(couts, sems):
            send.start()

    def finish(self, cins, couts, sems):
        for send, recv in self._copies(couts, sems):
            recv().wait_recv()
            send.wait_send()


def _small_sum(blocks):
    def body(b_ref, o_ref):
        acc = b_ref[0]
        for d in range(1, N_DEV):
            acc = acc + b_ref[d]
        o_ref[...] = acc

    return pl.pallas_call(
        body, name="small_sum", out_shape=_sds((SMALL_ROWS, D_MODEL), F32), compiler_params=_params(),
    )(blocks)


def _adamw(name, params, steps):
    n = len(params)

    def body(*refs):
        for p in range(n):
            w_ref, g_ref, m_ref, v_ref = refs[4 * p:4 * p + 4]
            d_ref, nm_ref, nv_ref, go_ref = refs[4 * n + 4 * p:4 * n + 4 * p + 4]
            g = g_ref[...]
            go_ref[...] = g
            m = ADAM_B1 * m_ref[...] + (1.0 - ADAM_B1) * g
            v = ADAM_B2 * v_ref[...] + (1.0 - ADAM_B2) * jnp.square(g)
            m_hat = m / (1.0 - ADAM_B1 ** ADAM_STEP)
            v_hat = v / (1.0 - ADAM_B2 ** ADAM_STEP)
            d_ref[...] = -ADAM_LR * (m_hat / (jnp.sqrt(v_hat) + ADAM_EPS) + ADAM_WD * w_ref[...])
            nm_ref[...] = m
            nv_ref[...] = v

    in_specs, out_specs, out_shape, operands = [], [], [], []
    for w, g, m, v in params:
        spec = pl.BlockSpec((w.shape[0] // steps, w.shape[1]), lambda i: (i, 0))
        in_specs += [spec] * 4
        out_specs += [spec] * 4
        out_shape += [_sds(w.shape, F32)] * 4
        operands += [w, g, m, v]
    outs = _pcall(body, name, (steps,), in_specs, out_specs, out_shape, operands, (), ("parallel",))
    return [tuple(outs[4 * p:4 * p + 4]) for p in range(n)]


MATRIX_NAMES = tuple(MATRICES)
WEIGHT_ORDER = ("g_mix", "w_in", "conv_w", "attn_sinks", "w_conv_out", "w_attn_out", "w_o", "g_ffn",
                "w_gate_up", "w_down", "g_final")


def kernel(x, g_mix, w_in, conv_w, attn_sinks, w_conv_out, w_attn_out, w_o, g_ffn, w_gate_up, w_down, g_final, loss_target, m_g_mix, m_w_in, m_conv_w, m_attn_sinks, m_w_conv_out, m_w_attn_out, m_w_o, m_g_ffn, m_w_gate_up, m_w_down, m_g_final, v_g_mix, v_w_in, v_conv_w, v_attn_sinks, v_w_conv_out, v_w_attn_out, v_w_o, v_g_ffn, v_w_gate_up, v_w_down, v_g_final):
    w = dict(g_mix=g_mix, w_in=w_in[0], conv_w=conv_w[0], attn_sinks=attn_sinks, w_conv_out=w_conv_out[0],
             w_attn_out=w_attn_out[0], w_o=w_o[0], g_ffn=g_ffn, w_gate_up=w_gate_up[0], w_down=w_down[0],
             g_final=g_final[None, :])
    m = dict(g_mix=m_g_mix, w_in=m_w_in[0], conv_w=m_conv_w[0], attn_sinks=m_attn_sinks,
             w_conv_out=m_w_conv_out[0], w_attn_out=m_w_attn_out[0], w_o=m_w_o[0], g_ffn=m_g_ffn,
             w_gate_up=m_w_gate_up[0], w_down=m_w_down[0], g_final=m_g_final[None, :])
    v = dict(g_mix=v_g_mix, w_in=v_w_in[0], conv_w=v_conv_w[0], attn_sinks=v_attn_sinks,
             w_conv_out=v_w_conv_out[0], w_attn_out=v_w_attn_out[0], w_o=v_w_o[0], g_ffn=v_g_ffn,
             w_gate_up=v_w_gate_up[0], w_down=v_w_down[0], g_final=v_g_final[None, :])
    shard = (2 * lax.axis_index("x") + lax.axis_index("y")).astype(jnp.int32)
    core = lax.axis_index("c").astype(jnp.int32)
    shard1, core1, place = shard.reshape((1,)), core.reshape((1,)), jnp.stack([core, shard])
    spec = MATRICES
    xs, target, sinks = x[0], loss_target[0], w["attn_sinks"]
    tables = _rope_tables()

    def gather(names, which=(0,), parts=1):
        return _Gather([whole[n] for n in names],
                       [(i, spec[n], part, parts) for i, n in enumerate(names) for part in which])

    def pair(names):
        return _Pair([dw[n] for n in names], [spec[n] for n in names])

    def pair_sum(tag, names, got):
        return _pair_sum("pair_sum_" + tag, [spec[n] for n in names], [dw[n] for n in names], got, place)

    whole = dict(zip(MATRIX_NAMES, _to_bf16_in_whole(
        [w[n] for n in MATRIX_NAMES], [spec[n] for n in MATRIX_NAMES], shard1)))

    mixers = ("w_conv_out", "w_attn_out", "w_o")
    h1 = _rms_norm("norm_mix", xs, w["g_mix"])
    proj, whole["w_in"], (*got, conv_w_whole) = _mm_in_gather(
        h1, whole["w_in"], _Gather([whole[n] for n in mixers], [(i, spec[n], 0, 1) for i, n in enumerate(mixers)],
                                   conv_w=w["conv_w"]))
    whole.update(zip(mixers, got))
    conv_y = _conv_fwd(proj, conv_w_whole)
    attn, (whole["w_gate_up"],) = _attn_fwd(proj, tables, sinks, comm=gather(("w_gate_up",), (0, 1, 2), 4))
    (conv_out, attn_out, merged), (whole["w_gate_up"],) = _branch_merge(
        conv_y, attn, whole["w_conv_out"], whole["w_attn_out"], proj, comm=gather(("w_gate_up",), (3,), 4))
    x2, h2 = _mm_o_norm(merged, whole["w_o"], xs, w["g_ffn"])
    (gate, up, act), (whole["w_down"],) = _gate_up_fwd(h2, whole["w_gate_up"], comm=gather(("w_down",)))
    dx3, dx3b, dg_final, loss_row = _mm_down_loss(act, whole["w_down"], x2, w["g_final"], target)

    dw = {}
    dw["w_down"] = _mm_tn("mm_dw_down", act, dx3b, 1408, 1024, BF16)
    dgate, dup = _dact_swiglu(dx3b, whole["w_down"], gate, up)
    dw["w_gate_up"], got = _mm_dw_gate_up(h2, dgate, dup, comm=pair(("w_down",)))
    sums_a, own_a = pair_sum("down", ("w_down",), got)
    dh2, slots_a = _mm_dh2(dgate, dup, whole["w_gate_up"], comm=_ChipExchange(sums_a, own_a))
    (dx2, dx2b, dg_ffn), got_b = _rms_norm_bwd("norm_ffn_bwd", dh2, x2, w["g_ffn"], dx3, True,
                                               comm=pair(("w_gate_up",)))
    dw["w_o"] = _mm_tn("mm_dw_o", merged, dx2b, 1024, 1024, BF16)
    dco, dao, dgc, dga = _merge_bwd(dx2b, whole["w_o"], conv_out, attn_out, proj)
    dw["w_conv_out"] = _mm_tn("mm_dw_conv_out", conv_y, dco, 1024, 1024, BF16)
    dw["w_attn_out"] = _mm_tn("mm_dw_attn_out", attn, dao, 1024, 1024, BF16)
    (dcb, dcc, dcx, dconv_w), got_c = _conv_bwd(dco, whole["w_conv_out"], proj, conv_w_whole, comm=pair(mixers))
    sums_bc, own_bc = pair_sum("gate_up_mixers", ("w_gate_up",) + mixers, got_b + got_c)
    sums_b, own_b, sums_c, own_c = sums_bc[:1], own_bc[:1], sums_bc[1:], own_bc[1:]
    (dq, dk_prev, dk_cur, dv_prev, dv_cur, dsinks), slots_b = _attn_bwd(
        proj, dao, whole["w_attn_out"], sinks, tables, comm=_ChipExchange(sums_b, own_b))
    dkv = _kv_grad_combine(dk_prev, dk_cur, dv_prev, dv_cur, tables)
    dproj = jnp.concatenate([dcb, dcc, dcx, dq, dkv, dgc, dga], axis=1)
    dw["w_in"], got, slots_c = _mm_dw_in_pair(h1, dproj, _ChipExchange(sums_c, own_c))
    sums_d, own_d = pair_sum("in", ("w_in",), [got])
    early = ("w_down", "w_gate_up") + mixers
    halves = _chip_sum("chip_sum_early", [spec[n] for n in early], slots_a + slots_b + slots_c, core1)
    dh1, (own_d, *reduced) = _mm_nt(
        "mm_dh1", dproj, whole["w_in"], 1024, 1024, 1664, F32,
        comm=_Both(_ChipExchange(sums_d, own_d, 0, 2), _HalfExchange(halves, [spec[n] for n in early])))
    g = dict(zip(early, reduced))
    (grad_x, dg_mix), slots_d = _rms_norm_bwd("norm_mix_bwd", dh1, xs, w["g_mix"], dx2, False,
                                              comm=_ChipExchange(sums_d, [own_d], 1, 2))
    small = _pack_small(dg_mix, dg_ffn, dg_final, dconv_w, dsinks, loss_row)
    half_in = _chip_sum("chip_sum_in", [spec["w_in"]], slots_d, core1)
    g["w_in"], small_blocks = _comm_call(
        "half_exchange_in", _Both(_HalfExchange(half_in, [spec["w_in"]]), _SmallAllToAll(small)))
    delta, new_m, new_v = {}, {}, {}

    def keep(names, results):
        for n, (d, nm, nv, grad) in zip(names, results):
            delta[n], new_m[n], new_v[n], g[n] = d, nm, nv, grad

    keep(early, _adamw("adamw_early", [(w[n], g[n], m[n], v[n]) for n in early], 8))
    small_sum = _small_sum(small_blocks)
    g["g_mix"] = small_sum[0:1, :]
    g["g_ffn"] = small_sum[1:2, :]
    g["g_final"] = small_sum[2:3, :]
    g["conv_w"] = lax.dynamic_slice(small_sum, (3, shard * CONV_W_COLS), (3, CONV_W_COLS))
    g["attn_sinks"] = small_sum[6:7, :N_HEADS]
    loss = small_sum[7, 0]
    keep(("w_in",), _adamw("adamw_w_in", [(w["w_in"], g["w_in"], m["w_in"], v["w_in"])], 4))
    rest = ("g_mix", "g_ffn", "g_final", "conv_w", "attn_sinks")
    keep(rest, _adamw("adamw_small", [(w[n], g[n], m[n], v[n]) for n in rest], 1))

    def shaped(vals):
        return [vals[n].reshape((D_MODEL,)) if n == "g_final" else
                (vals[n][None] if n in MATRIX_NAMES or n == "conv_w" else vals[n]) for n in WEIGHT_ORDER]

    return (loss, grad_x[None], *shaped(g), *shaped(delta), *shaped(new_m), *shaped(new_v))
```

```python
import functools
import math

import jax
import jax.numpy as jnp
import numpy as np
from jax import lax
from jax.experimental import pallas as pl
from jax.experimental.pallas import tpu as pltpu

F32 = jnp.float32
BF16 = jnp.bfloat16

D_MODEL = 1024
SEQ = 2048
HEAD_DIM = 64
N_HEADS = 16
N_KV_HEADS = 4
GROUP = N_HEADS // N_KV_HEADS
D_ATTN = N_HEADS * HEAD_DIM
D_KV = N_KV_HEADS * HEAD_DIM
BLOCK = 128
ROT_DIM = HEAD_DIM // 4
ROPE_THETA = 500000.0
ATTN_SCALE = 1.0 / math.sqrt(HEAD_DIM)
NEG_INF = -1e30
D_FF = 2816
EPS = 1e-5
N_IN = 3 * D_MODEL + D_ATTN + 2 * D_KV + 2 * D_MODEL
COL_Q = 3 * D_MODEL
COL_K = COL_Q + D_ATTN
COL_V = COL_K + D_KV
COL_GC = COL_V + D_KV
COL_GA = COL_GC + D_MODEL

ADAM_LR = 0.001
ADAM_B1 = 0.9
ADAM_B2 = 0.999
ADAM_EPS = 1e-08
ADAM_WD = 0.01
ADAM_STEP = 10

N_CHIPS = 4
N_DEV = 8

V7X_VMEM_BYTES = 64 * 1024 * 1024
VMEM_LIMIT = (V7X_VMEM_BYTES * 3) // 4
LANES = 128
MESH = pl.DeviceIdType.MESH


def _params(semantics=None):
    return pltpu.CompilerParams(dimension_semantics=semantics, vmem_limit_bytes=VMEM_LIMIT)


def _sds(shape, dtype):
    return jax.ShapeDtypeStruct(shape, dtype)


HBM_SPEC = pl.BlockSpec(memory_space=pl.ANY)


def _pcall(body, name, grid, in_specs, out_specs, out_shape, operands, scratch=(), semantics=None, comm=None,
           aliases=None, start_after_body=False):
    aliases = dict(aliases or {})
    if comm is None:
        return pl.pallas_call(
            body, name=name, grid=grid, in_specs=in_specs, out_specs=out_specs, out_shape=out_shape,
            scratch_shapes=list(scratch), input_output_aliases=aliases,
            compiler_params=_params(semantics))(*operands)
    multi = isinstance(out_shape, (list, tuple))
    o_specs = list(out_specs) if multi else [out_specs]
    o_shape = list(out_shape) if multi else [out_shape]
    n_in, n_out, n_scr = len(operands), len(o_shape), len(scratch)
    n_cin, n_cout = len(comm.operands), len(comm.out_shape)

    def hosted(*refs):
        ins, cins = refs[:n_in], refs[n_in:n_in + n_cin]
        o0 = n_in + n_cin
        outs, couts = refs[o0:o0 + n_out], refs[o0 + n_out:o0 + n_out + n_cout]
        s0 = o0 + n_out + n_cout
        scr, sems = refs[s0:s0 + n_scr], refs[s0 + n_scr:]
        first = last = None
        for axis, size in enumerate(grid):
            i = pl.program_id(axis)
            first = (i == 0) if first is None else first & (i == 0)
            last = (i == size - 1) if last is None else last & (i == size - 1)

        if not start_after_body:
            @pl.when(first)
            def _():
                comm.start(cins, couts, sems)

        body(*ins, *outs, *scr)

        if start_after_body:
            @pl.when(first)
            def _():
                comm.start(cins, couts, sems)

        @pl.when(last)
        def _():
            comm.finish(cins, couts, sems)

    res = pl.pallas_call(
        hosted, name=name, grid=grid,
        in_specs=list(in_specs) + [HBM_SPEC] * n_cin, out_specs=o_specs + [HBM_SPEC] * n_cout,
        out_shape=o_shape + list(comm.out_shape), scratch_shapes=list(scratch) + list(comm.sems),
        input_output_aliases={**aliases, **{n_in + a: n_out + b for a, b in comm.aliases.items()}},
        compiler_params=_params(("arbitrary",) * len(grid)))(*operands, *comm.operands)
    outs = list(res[:n_out])
    return (outs if multi else outs[0]), list(res[n_out:])


def _comm_call(name, comm):
    def body(*refs):
        n_cin, n_cout = len(comm.operands), len(comm.out_shape)
        cins, couts, sems = refs[:n_cin], refs[n_cin:n_cin + n_cout], refs[n_cin + n_cout:]
        comm.start(cins, couts, sems)
        comm.finish(cins, couts, sems)

    return list(pl.pallas_call(
        body, name=name, in_specs=[HBM_SPEC] * len(comm.operands), out_specs=[HBM_SPEC] * len(comm.out_shape),
        out_shape=list(comm.out_shape), scratch_shapes=list(comm.sems),
        input_output_aliases=dict(comm.aliases))(*comm.operands))


NN = ((1,), (0,))
NT = ((1,), (1,))
TN = ((0,), (0,))


def _matmul(name, a, b, dims, grid, a_spec, b_spec, o_spec, o_shape, o_dtype, res=None, res_spec=None, comm=None):
    nk = grid[2]

    def body(*refs):
        if res is None:
            a_ref, b_ref, o_ref = refs[:3]
            r_ref = None
            scratch = refs[3:]
        else:
            a_ref, b_ref, r_ref, o_ref = refs[:4]
            scratch = refs[4:]
        p = lax.dot_general(a_ref[...], b_ref[...], (dims, ((), ())), preferred_element_type=F32)

        def finish(acc):
            if r_ref is not None:
                acc = r_ref[...] + acc
            o_ref[...] = acc.astype(o_dtype)

        if nk == 1:
            finish(p)
        else:
            acc_ref = o_ref if in_place else scratch[0]
            k = pl.program_id(2)

            @pl.when(k == 0)
            def _():
                acc_ref[...] = p

            @pl.when(k > 0)
            def _():
                acc_ref[...] += p

            if not in_place:
                @pl.when(k == nk - 1)
                def _():
                    finish(acc_ref[...])

    in_place = nk > 1 and res is None and o_dtype == F32
    operands = [a, b] if res is None else [a, b, res]
    in_specs = [a_spec, b_spec] if res is None else [a_spec, b_spec, res_spec]
    scratch = [pltpu.VMEM(o_spec.block_shape, F32)] if nk > 1 and not in_place else []
    return _pcall(body, name, grid, in_specs, o_spec, _sds(o_shape, o_dtype), operands, scratch,
                  ("parallel", "parallel", "arbitrary"), comm)


def _mm_nt(name, a, b, bm, bn, bk, o_dtype, comm=None):
    m, k = a.shape
    n = b.shape[0]
    return _matmul(
        name, a, b, NT, (m // bm, n // bn, k // bk),
        pl.BlockSpec((bm, bk), lambda i, j, kk: (i, kk)),
        pl.BlockSpec((bn, bk), lambda i, j, kk: (j, kk)),
        pl.BlockSpec((bm, bn), lambda i, j, kk: (i, j)),
        (m, n), o_dtype, comm=comm,
    )


def _mm_tn(name, a, b, bm, bn, o_dtype, comm=None):
    k, m = a.shape
    n = b.shape[1]
    return _matmul(
        name, a, b, TN, (m // bm, n // bn, 1),
        pl.BlockSpec((k, bm), lambda i, j, kk: (0, i)),
        pl.BlockSpec((k, bn), lambda i, j, kk: (0, j)),
        pl.BlockSpec((bm, bn), lambda i, j, kk: (i, j)),
        (m, n), o_dtype, comm=comm,
    )


ROWS = 256


def _row_spec(width, col=0):
    return pl.BlockSpec((ROWS, width), lambda i: (i, col))


def _full_spec(shape):
    return pl.BlockSpec(shape, lambda *_: (0,) * len(shape))


def _rms_norm(name, x, g):
    def body(x_ref, g_ref, h_ref):
        xf = x_ref[...]
        r = lax.rsqrt(jnp.mean(xf * xf, axis=-1, keepdims=True) + EPS)
        h_ref[...] = ((xf * r) * g_ref[...]).astype(BF16)

    return pl.pallas_call(
        body, name=name, grid=(SEQ // ROWS,),
        in_specs=[_row_spec(D_MODEL), _full_spec((1, D_MODEL))],
        out_specs=_row_spec(D_MODEL),
        out_shape=_sds((SEQ, D_MODEL), BF16),
        compiler_params=_params(("parallel",)),
    )(x, g)


CONV_COLS = 256


def _shift_rows(u, k):
    rows = lax.broadcasted_iota(jnp.int32, u.shape, 0)
    return jnp.where(rows >= k, pltpu.roll(u, k, axis=0), 0.0)


def _conv_fwd(proj, conv_w):
    nblk = D_MODEL // CONV_COLS

    def body(cb_ref, cc_ref, cx_ref, w_ref, y_ref):
        u = cc_ref[...] * cx_ref[...]
        w = w_ref[...]
        cv = w[0:1, :] * _shift_rows(u, 2) + w[1:2, :] * _shift_rows(u, 1) + w[2:3, :] * u
        y_ref[...] = (cb_ref[...] * cv).astype(BF16)

    def col(part):
        return pl.BlockSpec((SEQ, CONV_COLS), lambda j: (0, part * nblk + j))

    return pl.pallas_call(
        body, name="conv_fwd", grid=(nblk,),
        in_specs=[col(0), col(1), col(2), pl.BlockSpec((3, CONV_COLS), lambda j: (0, j))],
        out_specs=pl.BlockSpec((SEQ, CONV_COLS), lambda j: (0, j)),
        out_shape=_sds((SEQ, D_MODEL), BF16),
        compiler_params=_params(("parallel",)),
    )(proj, proj, proj, conv_w)


ROPE_COLS = 256


def _rope_tables():
    f32 = np.float32
    inv_freq = (f32(ROPE_THETA) ** (-np.arange(0, ROT_DIM, 2, dtype=f32) / f32(ROT_DIM))).astype(f32)
    ang = np.arange(SEQ, dtype=f32)[:, None] * inv_freq[None, :]
    cos, sin = np.cos(ang).astype(f32), np.sin(ang).astype(f32)
    half = ROT_DIM // 2
    ones = np.ones((SEQ, HEAD_DIM - ROT_DIM), f32)
    zeros = np.zeros((SEQ, HEAD_DIM - ROT_DIM), f32)
    zh = np.zeros((SEQ, half), f32)
    c = np.concatenate([cos, cos, ones], axis=1)
    s_up = np.concatenate([-sin, zh, zeros], axis=1)
    s_dn = np.concatenate([zh, sin, zeros], axis=1)
    reps = ROPE_COLS // HEAD_DIM
    return tuple(jnp.asarray(np.tile(t, (1, reps))) for t in (c, s_up, s_dn))


def _rotate(t, c, s_up, s_dn):
    width = t.shape[1]
    half = ROT_DIM // 2
    return t * c + pltpu.roll(t, width - half, axis=1) * s_up + pltpu.roll(t, half, axis=1) * s_dn


N_QBLK = SEQ // BLOCK


def _attn_specs():
    prev = lambda n: jnp.maximum(n - 1, 0)
    q = pl.BlockSpec((BLOCK, D_ATTN), lambda n: (n, COL_Q // D_ATTN))
    k_prev = pl.BlockSpec((BLOCK, D_KV), lambda n: (prev(n), COL_K // D_KV))
    k_cur = pl.BlockSpec((BLOCK, D_KV), lambda n: (n, COL_K // D_KV))
    v_prev = pl.BlockSpec((BLOCK, D_KV), lambda n: (prev(n), COL_V // D_KV))
    v_cur = pl.BlockSpec((BLOCK, D_KV), lambda n: (n, COL_V // D_KV))
    tab_cur = pl.BlockSpec((BLOCK, ROPE_COLS), lambda n: (n, 0))
    tab_prev = pl.BlockSpec((BLOCK, ROPE_COLS), lambda n: (prev(n), 0))
    return [q, k_prev, k_cur, v_prev, v_cur] + [tab_cur] * 3 + [tab_prev] * 3


def _band_kv(kp_ref, kc_ref, vp_ref, vc_ref, tabs_cur, tabs_prev):
    k = jnp.concatenate([_rotate(kp_ref[...], *(t[...] for t in tabs_prev)),
                         _rotate(kc_ref[...], *(t[...] for t in tabs_cur))], axis=0)
    v = jnp.concatenate([vp_ref[...], vc_ref[...]], axis=0)
    return k, v


def _query_tiles(q_ref, tiles, tabs_cur):
    c, su, sd = (t[:, :LANES] for t in tabs_cur)
    return jnp.concatenate(
        [_rotate(q_ref[:, t * LANES:(t + 1) * LANES], c, su, sd).astype(BF16) for t in tiles], axis=0)


def _sink_row(sink_ref, tiles, par):
    return jnp.concatenate([jnp.full((1, BLOCK), sink_ref[0, t * HEADS_PER_TILE + par], F32) for t in tiles], axis=1)


def _band_mask(n):
    kj = lax.broadcasted_iota(jnp.int32, (2 * BLOCK, BLOCK), 0)
    qi = lax.broadcasted_iota(jnp.int32, (2 * BLOCK, BLOCK), 1)
    rel = qi + BLOCK - kj
    return (rel >= 0) & (rel < BLOCK) & ((kj >= BLOCK) | (n > 0))


HEADS_PER_TILE = LANES // HEAD_DIM
TILES_PER_GROUP = GROUP // HEADS_PER_TILE


def _group_mask(n):
    return jnp.concatenate([_band_mask(n)] * TILES_PER_GROUP, axis=1)


def _lane_half(shape, par):
    lane = lax.broadcasted_iota(jnp.int32, shape, 1)
    return (lane < HEAD_DIM) if par == 0 else (lane >= HEAD_DIM)


def _head_tiles(kv, h):
    tile = kv[:, (h // HEADS_PER_TILE) * LANES:(h // HEADS_PER_TILE + 1) * LANES].astype(F32)
    own = jnp.where(_lane_half(tile.shape, h % HEADS_PER_TILE), tile, 0.0)
    other = pltpu.roll(own, HEAD_DIM, axis=1)
    lo, hi = (own, other) if h % HEADS_PER_TILE == 0 else (other, own)
    return lo.astype(BF16), hi.astype(BF16)


def _head_softmax(q_tile, k_half, sink, mask):
    s = lax.dot_general(k_half, q_tile, (NT, ((), ())), preferred_element_type=F32) * ATTN_SCALE
    s = jnp.where(mask, s, NEG_INF)
    m = jnp.maximum(jnp.max(s, axis=0, keepdims=True), sink)
    e = jnp.exp(s - m)
    es = jnp.exp(sink - m)
    inv = 1.0 / (jnp.sum(e, axis=0, keepdims=True) + es)
    return e * inv, es * inv


def _attn_fwd(proj, tables, sinks, comm=None):
    def body(sink_ref, q_ref, kp_ref, kc_ref, vp_ref, vc_ref, c_ref, su_ref, sd_ref, cp_ref, sup_ref, sdp_ref, o_ref):
        n = pl.program_id(0)
        mask = _group_mask(n)
        tabs_cur = (c_ref, su_ref, sd_ref)
        k, v = _band_kv(kp_ref, kc_ref, vp_ref, vc_ref, tabs_cur, (cp_ref, sup_ref, sdp_ref))
        for h in range(N_KV_HEADS):
            k_halves = _head_tiles(k, h)
            v_halves = _head_tiles(v, h)
            tiles = [h * TILES_PER_GROUP + t for t in range(TILES_PER_GROUP)]
            q_rows = _query_tiles(q_ref, tiles, tabs_cur)
            acc = None
            for par in range(HEADS_PER_TILE):
                p, _ = _head_softmax(q_rows, k_halves[par], _sink_row(sink_ref, tiles, par), mask)
                o = lax.dot_general(p.astype(BF16), v_halves[par], (TN, ((), ())), preferred_element_type=F32)
                acc = o if acc is None else acc + o
            for i, tile in enumerate(tiles):
                o_ref[:, tile * LANES:(tile + 1) * LANES] = acc[i * BLOCK:(i + 1) * BLOCK, :].astype(BF16)

    return _pcall(
        body, "attn_fwd", (N_QBLK,),
        [pl.BlockSpec(memory_space=pltpu.SMEM)] + _attn_specs(),
        pl.BlockSpec((BLOCK, D_ATTN), lambda n: (n, 0)),
        _sds((SEQ, D_ATTN), BF16), [sinks] + [proj] * 5 + list(tables) * 2, (), ("parallel",), comm)


def _branch_merge(conv_y, attn, w_co, w_ao, proj, comm=None):
    bm, bn = 1024, 512

    def body(cy_ref, at_ref, wc_ref, wa_ref, gc_ref, ga_ref, co_ref, ao_ref, mg_ref):
        co = jnp.dot(cy_ref[...], wc_ref[...], preferred_element_type=F32)
        ao = jnp.dot(at_ref[...], wa_ref[...], preferred_element_type=F32)
        co_ref[...] = co
        ao_ref[...] = ao
        mg_ref[...] = (jax.nn.sigmoid(gc_ref[...]) * co + jax.nn.sigmoid(ga_ref[...]) * ao).astype(BF16)

    act = pl.BlockSpec((bm, D_MODEL), lambda i, j: (i, 0))
    wgt = pl.BlockSpec((D_MODEL, bn), lambda i, j: (0, j))
    out = pl.BlockSpec((bm, bn), lambda i, j: (i, j))
    return _pcall(
        body, "branch_merge", (SEQ // bm, D_MODEL // bn),
        [act, act, wgt, wgt,
         pl.BlockSpec((bm, bn), lambda i, j: (i, COL_GC // bn + j)),
         pl.BlockSpec((bm, bn), lambda i, j: (i, COL_GA // bn + j))],
        [out, out, out],
        [_sds((SEQ, D_MODEL), F32), _sds((SEQ, D_MODEL), F32), _sds((SEQ, D_MODEL), BF16)],
        [conv_y, attn, w_co, w_ao, proj, proj], (), ("parallel", "parallel"), comm)


def _mm_o_norm(merged, w_o, x, g):
    bm = 1024

    def body(a_ref, w_ref, x_ref, g_ref, o_ref, h_ref):
        x2 = x_ref[...] + jnp.dot(a_ref[...], w_ref[...], preferred_element_type=F32)
        o_ref[...] = x2
        r = lax.rsqrt(jnp.mean(x2 * x2, axis=-1, keepdims=True) + EPS)
        h_ref[...] = ((x2 * r) * g_ref[...]).astype(BF16)

    row = pl.BlockSpec((bm, D_MODEL), lambda i: (i, 0))
    return pl.pallas_call(
        body, name="mm_o", grid=(SEQ // bm,),
        in_specs=[row, _full_spec((D_MODEL, D_MODEL)), row, _full_spec((1, D_MODEL))],
        out_specs=[row, row], out_shape=[_sds((SEQ, D_MODEL), F32), _sds((SEQ, D_MODEL), BF16)],
        compiler_params=_params(("parallel",)),
    )(merged, w_o, x, g)


FF_BM, FF_BN = 512, 1408
FF_NB = D_FF // FF_BN


def _gate_up_fwd(h2, w_gu, comm=None):
    def body(h_ref, wg_ref, wu_ref, g_ref, u_ref, a_ref):
        h = h_ref[...]
        g = jnp.dot(h, wg_ref[...], preferred_element_type=F32)
        u = jnp.dot(h, wu_ref[...], preferred_element_type=F32)
        g_ref[...] = g
        u_ref[...] = u
        a_ref[...] = (jax.nn.silu(g) * u).astype(BF16)

    out = pl.BlockSpec((FF_BM, FF_BN), lambda i, j: (i, j))
    f32, b16 = _sds((SEQ, D_FF), F32), _sds((SEQ, D_FF), BF16)
    return _pcall(
        body, "mm_gate_up", (SEQ // FF_BM, FF_NB),
        [pl.BlockSpec((FF_BM, D_MODEL), lambda i, j: (i, 0)),
         pl.BlockSpec((D_MODEL, FF_BN), lambda i, j: (0, j)),
         pl.BlockSpec((D_MODEL, FF_BN), lambda i, j: (0, FF_NB + j))],
        [out, out, out], [f32, f32, b16], [h2, w_gu, w_gu], (), ("parallel", "parallel"), comm)


def _dact_swiglu(dx3b, w_down, g, u):
    def body(dx_ref, w_ref, g_ref, u_ref, dg_ref, du_ref):
        da = lax.dot_general(dx_ref[...], w_ref[...], (NT, ((), ())), preferred_element_type=F32)
        g = g_ref[...]
        sg = jax.nn.sigmoid(g)
        dg_ref[...] = (da * u_ref[...] * (sg * (1.0 + g * (1.0 - sg)))).astype(BF16)
        du_ref[...] = (da * (g * sg)).astype(BF16)

    blk = pl.BlockSpec((FF_BM, FF_BN), lambda i, j: (i, j))
    b16 = _sds((SEQ, D_FF), BF16)
    return _pcall(
        body, "mm_dact", (SEQ // FF_BM, FF_NB),
        [pl.BlockSpec((FF_BM, D_MODEL), lambda i, j: (i, 0)), pl.BlockSpec((FF_BN, D_MODEL), lambda i, j: (j, 0)),
         blk, blk],
        [blk, blk], [b16, b16], [dx3b, w_down, g, u], (), ("parallel", "parallel"))


def _mm_dh2(dg, du, w_gu, comm=None):
    bm = 1024
    nk = 2 * FF_NB

    def body(dg_ref, du_ref, w_ref, o_ref):
        k = pl.program_id(1)

        def part(a_ref):
            return lax.dot_general(a_ref[...], w_ref[...], (NT, ((), ())), preferred_element_type=F32)

        @pl.when(k == 0)
        def _():
            o_ref[...] = part(dg_ref)

        @pl.when((k > 0) & (k < FF_NB))
        def _():
            o_ref[...] += part(dg_ref)

        @pl.when(k >= FF_NB)
        def _():
            o_ref[...] += part(du_ref)

    return _pcall(
        body, "mm_dh2", (SEQ // bm, nk),
        [pl.BlockSpec((bm, FF_BN), lambda i, k: (i, jnp.minimum(k, FF_NB - 1))),
         pl.BlockSpec((bm, FF_BN), lambda i, k: (i, jnp.maximum(k - FF_NB, 0))),
         pl.BlockSpec((D_MODEL, FF_BN), lambda i, k: (0, k))],
        pl.BlockSpec((bm, D_MODEL), lambda i, k: (i, 0)), _sds((SEQ, D_MODEL), F32),
        [dg, du, w_gu], (), ("parallel", "arbitrary"), comm)


def _mm_dw_gate_up(h2, dg, du, comm=None):
    def body(h_ref, dg_ref, du_ref, o_ref):
        j = pl.program_id(0)

        def part(b_ref):
            return lax.dot_general(h_ref[...], b_ref[...], (TN, ((), ())), preferred_element_type=F32).astype(BF16)

        @pl.when(j < FF_NB)
        def _():
            o_ref[...] = part(dg_ref)

        @pl.when(j >= FF_NB)
        def _():
            o_ref[...] = part(du_ref)

    return _pcall(
        body, "mm_dw_gate_up", (2 * FF_NB,),
        [_full_spec((SEQ, D_MODEL)),
         pl.BlockSpec((SEQ, FF_BN), lambda j: (0, jnp.minimum(j, FF_NB - 1))),
         pl.BlockSpec((SEQ, FF_BN), lambda j: (0, jnp.maximum(j - FF_NB, 0)))],
        pl.BlockSpec((D_MODEL, FF_BN), lambda j: (0, j)),
        _sds((D_MODEL, 2 * D_FF), BF16), [h2, dg, du], (), ("arbitrary",), comm)


def _mm_down_loss(act, w_down, x2, g, target):
    bm = 512

    def body(a_ref, w_ref, x_ref, g_ref, t_ref, dx_ref, dxb_ref, dg_ref, loss_ref):
        i = pl.program_id(0)
        xf = x_ref[...] + jnp.dot(a_ref[...], w_ref[...], preferred_element_type=F32)
        r = lax.rsqrt(jnp.mean(xf * xf, axis=-1, keepdims=True) + EPS)
        xn = xf * r
        gg = g_ref[...]
        err = xn * gg - t_ref[...]
        part = 0.5 * jnp.sum(jnp.mean(err * err, axis=-1, keepdims=True), axis=0, keepdims=True)
        dy = err * (1.0 / D_MODEL)
        dxn = dy * gg
        dx = r * (dxn - xn * jnp.mean(dxn * xn, axis=-1, keepdims=True))
        dx_ref[...] = dx
        dxb_ref[...] = dx.astype(BF16)
        dg = jnp.sum(dy * xn, axis=0, keepdims=True)
        lane0 = lax.broadcasted_iota(jnp.int32, (1, LANES), 1) == 0
        lpart = jnp.where(lane0, part, 0.0)

        @pl.when(i == 0)
        def _():
            dg_ref[...] = dg
            loss_ref[...] = lpart

        @pl.when(i > 0)
        def _():
            dg_ref[...] += dg
            loss_ref[...] += lpart

    row = pl.BlockSpec((bm, D_MODEL), lambda i: (i, 0))
    return pl.pallas_call(
        body, name="mm_down", grid=(SEQ // bm,),
        in_specs=[pl.BlockSpec((bm, D_FF), lambda i: (i, 0)), _full_spec((D_FF, D_MODEL)), row,
                  _full_spec((1, D_MODEL)), row],
        out_specs=[row, row, _full_spec((1, D_MODEL)), _full_spec((1, LANES))],
        out_shape=[_sds((SEQ, D_MODEL), F32), _sds((SEQ, D_MODEL), BF16),
                   _sds((1, D_MODEL), F32), _sds((1, LANES), F32)],
        compiler_params=_params(("arbitrary",)),
    )(act, w_down, x2, g, target)


def _rms_norm_bwd(name, dh, x, g, dres, with_bf16, comm=None):
    def body(dh_ref, x_ref, g_ref, dr_ref, *outs):
        i = pl.program_id(0)
        dx_ref = outs[0]
        dg_ref = outs[-1]
        xf = x_ref[...]
        r = lax.rsqrt(jnp.mean(xf * xf, axis=-1, keepdims=True) + EPS)
        xn = xf * r
        dh = dh_ref[...]
        dxn = dh * g_ref[...]
        dx = dr_ref[...] + r * (dxn - xn * jnp.mean(dxn * xn, axis=-1, keepdims=True))
        dx_ref[...] = dx
        if with_bf16:
            outs[1][...] = dx.astype(BF16)
        dg = jnp.sum(dh * xn, axis=0, keepdims=True)

        @pl.when(i == 0)
        def _():
            dg_ref[...] = dg

        @pl.when(i > 0)
        def _():
            dg_ref[...] += dg

    row = _row_spec(D_MODEL)
    out_specs = [row] + ([row] if with_bf16 else []) + [_full_spec((1, D_MODEL))]
    out_shape = ([_sds((SEQ, D_MODEL), F32)] + ([_sds((SEQ, D_MODEL), BF16)] if with_bf16 else [])
                 + [_sds((1, D_MODEL), F32)])
    return _pcall(body, name, (SEQ // ROWS,), [row, row, _full_spec((1, D_MODEL)), row], out_specs, out_shape,
                  [dh, x, g, dres], (), ("arbitrary",), comm)


def _merge_bwd(dx2b, w_o, conv_out, attn_out, proj):
    bm, bn = 1024, D_MODEL // 2

    def body(dx_ref, w_ref, co_ref, ao_ref, gc_ref, ga_ref, dco_ref, dao_ref, dgc_ref, dga_ref):
        dm = lax.dot_general(dx_ref[...], w_ref[...], (NT, ((), ())), preferred_element_type=F32)
        sc = jax.nn.sigmoid(gc_ref[...])
        sa = jax.nn.sigmoid(ga_ref[...])
        dco_ref[...] = (dm * sc).astype(BF16)
        dao_ref[...] = (dm * sa).astype(BF16)
        dgc_ref[...] = (dm * co_ref[...] * (sc * (1.0 - sc))).astype(BF16)
        dga_ref[...] = (dm * ao_ref[...] * (sa * (1.0 - sa))).astype(BF16)

    own = pl.BlockSpec((bm, bn), lambda i, j: (i, j))
    sd = _sds((SEQ, D_MODEL), BF16)
    return pl.pallas_call(
        body, name="mm_dmerged", grid=(SEQ // bm, D_MODEL // bn),
        in_specs=[pl.BlockSpec((bm, D_MODEL), lambda i, j: (i, 0)), pl.BlockSpec((bn, D_MODEL), lambda i, j: (j, 0)),
                  own, own,
                  pl.BlockSpec((bm, bn), lambda i, j: (i, COL_GC // bn + j)),
                  pl.BlockSpec((bm, bn), lambda i, j: (i, COL_GA // bn + j))],
        out_specs=[own, own, own, own], out_shape=[sd, sd, sd, sd],
        compiler_params=_params(("parallel", "parallel")),
    )(dx2b, w_o, conv_out, attn_out, proj, proj)


def _conv_bwd(dco, w_co, proj, conv_w, comm=None):
    nblk = D_MODEL // CONV_COLS

    def body(dco_ref, wco_ref, cb_ref, cc_ref, cx_ref, w_ref, dcb_ref, dcc_ref, dcx_ref, dw_ref):
        cc = cc_ref[...]
        cx = cx_ref[...]
        u = cc * cx
        w = w_ref[...]
        u1 = _shift_rows(u, 1)
        u2 = _shift_rows(u, 2)
        cv = w[0:1, :] * u2 + w[1:2, :] * u1 + w[2:3, :] * u
        dy = lax.dot_general(dco_ref[...], wco_ref[...], (NT, ((), ())), preferred_element_type=F32)
        dcb_ref[...] = (dy * cv).astype(BF16)
        dcv = dy * cb_ref[...]
        rows = lax.broadcasted_iota(jnp.int32, dcv.shape, 0)
        up1 = jnp.where(rows < SEQ - 1, pltpu.roll(dcv, SEQ - 1, axis=0), 0.0)
        up2 = jnp.where(rows < SEQ - 2, pltpu.roll(dcv, SEQ - 2, axis=0), 0.0)
        du = w[2:3, :] * dcv + w[1:2, :] * up1 + w[0:1, :] * up2
        dcc_ref[...] = (du * cx).astype(BF16)
        dcx_ref[...] = (du * cc).astype(BF16)
        dw_ref[...] = jnp.concatenate(
            [jnp.sum(dcv * u2, axis=0, keepdims=True),
             jnp.sum(dcv * u1, axis=0, keepdims=True),
             jnp.sum(dcv * u, axis=0, keepdims=True)], axis=0)

    def col(part):
        return pl.BlockSpec((SEQ, CONV_COLS), lambda j: (0, part * nblk + j))

    own = pl.BlockSpec((SEQ, CONV_COLS), lambda j: (0, j))
    wsp = pl.BlockSpec((3, CONV_COLS), lambda j: (0, j))
    sd = _sds((SEQ, D_MODEL), BF16)
    return _pcall(
        body, "conv_bwd", (nblk,),
        [_full_spec((SEQ, D_MODEL)), pl.BlockSpec((CONV_COLS, D_MODEL), lambda j: (j, 0)), col(0), col(1), col(2), wsp],
        [own, own, own, wsp], [sd, sd, sd, _sds((3, D_MODEL), F32)],
        [dco, w_co, proj, proj, proj, conv_w], (), ("parallel",), comm)


def _attn_bwd(proj, dao, w_ao, sinks, tables, comm=None):
    def body(sink_ref, q_ref, kp_ref, kc_ref, vp_ref, vc_ref, c_ref, su_ref, sd_ref, cp_ref, sup_ref, sdp_ref,
             dao_ref, wao_ref, dq_ref, dkp_ref, dkc_ref, dvp_ref, dvc_ref, ds_ref):
        n = pl.program_id(0)
        mask = _group_mask(n)
        tabs_cur = (c_ref, su_ref, sd_ref)
        k, v = _band_kv(kp_ref, kc_ref, vp_ref, vc_ref, tabs_cur, (cp_ref, sup_ref, sdp_ref))
        do = lax.dot_general(dao_ref[...], wao_ref[...], (NT, ((), ())), preferred_element_type=F32).astype(BF16)
        lane = lax.broadcasted_iota(jnp.int32, (1, LANES), 1)
        dsink = jnp.zeros((1, LANES), F32)
        c, su, sd = c_ref[:, :LANES], su_ref[:, :LANES], sd_ref[:, :LANES]
        dk_tiles = [None] * (N_KV_HEADS // HEADS_PER_TILE)
        dv_tiles = [None] * (N_KV_HEADS // HEADS_PER_TILE)
        for h in range(N_KV_HEADS):
            k_halves = _head_tiles(k, h)
            v_halves = _head_tiles(v, h)
            tiles = [h * TILES_PER_GROUP + t for t in range(TILES_PER_GROUP)]
            q_rows = _query_tiles(q_ref, tiles, tabs_cur)
            do_rows = jnp.concatenate([do[:, t * LANES:(t + 1) * LANES] for t in tiles], axis=0)
            dk_par, dv_par = [], []
            dq_rows = None
            for par in range(HEADS_PER_TILE):
                p, p_sink = _head_softmax(q_rows, k_halves[par], _sink_row(sink_ref, tiles, par), mask)
                dp = lax.dot_general(v_halves[par], do_rows, (NT, ((), ())), preferred_element_type=F32)
                delta = jnp.sum(p * dp, axis=0, keepdims=True)
                ds = (p * (dp - delta) * ATTN_SCALE).astype(BF16)
                dq = lax.dot_general(ds, k_halves[par], (TN, ((), ())), preferred_element_type=F32)
                dq_rows = dq if dq_rows is None else dq_rows + dq
                dk_par.append(jnp.dot(ds, q_rows, preferred_element_type=F32))
                dv_par.append(jnp.dot(p.astype(BF16), do_rows, preferred_element_type=F32))
                sink_grad = p_sink * delta
                for i, tile in enumerate(tiles):
                    val = -jnp.sum(sink_grad[:, i * BLOCK:(i + 1) * BLOCK], axis=1, keepdims=True)
                    dsink = dsink + jnp.where(lane == tile * HEADS_PER_TILE + par, val, 0.0)
            for i, tile in enumerate(tiles):
                dq_tile = dq_rows[i * BLOCK:(i + 1) * BLOCK, :]
                dq_ref[:, tile * LANES:(tile + 1) * LANES] = _rotate(dq_tile, c, -su, -sd).astype(BF16)
            own = h % HEADS_PER_TILE
            for par_grads, tiles in ((dk_par, dk_tiles), (dv_par, dv_tiles)):
                shifted = pltpu.roll(par_grads[1 - own], HEAD_DIM, axis=1)
                total = jnp.where(_lane_half(shifted.shape, own), par_grads[own] + shifted, 0.0)
                i = h // HEADS_PER_TILE
                tiles[i] = total if tiles[i] is None else tiles[i] + total
        for i in range(N_KV_HEADS // HEADS_PER_TILE):
            cols = slice(i * LANES, (i + 1) * LANES)
            dkp_ref[:, cols] = dk_tiles[i][:BLOCK, :]
            dkc_ref[:, cols] = dk_tiles[i][BLOCK:, :]
            dvp_ref[:, cols] = dv_tiles[i][:BLOCK, :]
            dvc_ref[:, cols] = dv_tiles[i][BLOCK:, :]

        @pl.when(n == 0)
        def _():
            ds_ref[...] = dsink

        @pl.when(n > 0)
        def _():
            ds_ref[...] += dsink

    blk = pl.BlockSpec((BLOCK, D_KV), lambda n: (n, 0))
    prev_blk = pl.BlockSpec((BLOCK, D_KV), lambda n: ((n + N_QBLK - 1) % N_QBLK, 0))
    kv = _sds((SEQ, D_KV), F32)
    return _pcall(
        body, "attn_bwd", (N_QBLK,),
        [pl.BlockSpec(memory_space=pltpu.SMEM)] + _attn_specs()
        + [pl.BlockSpec((BLOCK, D_MODEL), lambda n: (n, 0)), _full_spec((D_ATTN, D_MODEL))],
        [pl.BlockSpec((BLOCK, D_ATTN), lambda n: (n, 0)), prev_blk, blk, prev_blk, blk, _full_spec((1, LANES))],
        [_sds((SEQ, D_ATTN), BF16), kv, kv, kv, kv, _sds((1, LANES), F32)],
        [sinks] + [proj] * 5 + list(tables) * 2 + [dao, w_ao], (), ("arbitrary",), comm)


def _kv_grad_combine(dk_prev, dk_cur, dv_prev, dv_cur, tables):
    rows = 4 * BLOCK

    def body(kp_ref, kc_ref, vp_ref, vc_ref, c_ref, su_ref, sd_ref, o_ref):
        dk = kc_ref[...] + kp_ref[...]
        dv = vc_ref[...] + vp_ref[...]
        o_ref[:, :D_KV] = _rotate(dk, c_ref[...], -su_ref[...], -sd_ref[...]).astype(BF16)
        o_ref[:, D_KV:] = dv.astype(BF16)

    blk = pl.BlockSpec((rows, D_KV), lambda m: (m, 0))
    return pl.pallas_call(
        body, name="kv_grad_combine", grid=(SEQ // rows,),
        in_specs=[blk] * 7,
        out_specs=pl.BlockSpec((rows, 2 * D_KV), lambda m: (m, 0)),
        out_shape=_sds((SEQ, 2 * D_KV), BF16),
        compiler_params=_params(("parallel",)),
    )(dk_prev, dk_cur, dv_prev, dv_cur, *tables)


MATRICES = {
    "w_in": (D_MODEL, N_IN // N_CHIPS, "col"),
    "w_conv_out": (D_MODEL // N_CHIPS, D_MODEL, "row"),
    "w_attn_out": (D_MODEL // N_CHIPS, D_MODEL, "row"),
    "w_o": (D_MODEL // N_CHIPS, D_MODEL, "row"),
    "w_gate_up": (D_MODEL, 2 * D_FF // N_CHIPS, "col"),
    "w_down": (D_FF // N_CHIPS, D_MODEL, "row"),
}
BF16_ROW_TILE = 16
CONV_W_COLS = D_MODEL // N_CHIPS
SMALL_ROWS = 8


def _whole_shape(spec):
    rows, cols, kind = spec
    return (rows, cols * N_CHIPS) if kind == "col" else (rows * N_CHIPS, cols)


def _half_shape(spec):
    return (spec[0] // 2, spec[1])


def _aligned(start, multiple):
    return start if isinstance(start, int) else pl.multiple_of(start, multiple)


def _region(ref, spec, shard, half, part=0, parts=1):
    rows, cols, kind = spec
    hr = rows // 2
    n = hr // parts
    if kind == "col":
        return ref.at[pl.ds(_aligned(half * hr + part * n, BF16_ROW_TILE), n),
                      pl.ds(_aligned(shard * cols, LANES), cols)]
    return ref.at[pl.ds(_aligned(shard * rows + half * hr + part * n, BF16_ROW_TILE), n), :]


def _position():
    x, y, c = lax.axis_index("x"), lax.axis_index("y"), lax.axis_index("c")
    chips = [(1 - x, y), (x, 1 - y), (1 - x, 1 - y)]
    return x, y, c, chips


def _shard_of(chip):
    return 2 * chip[0] + chip[1]


def _remote(src, dst, send_sem, recv_sem, to):
    return pltpu.make_async_remote_copy(src_ref=src, dst_ref=dst, send_sem=send_sem, recv_sem=recv_sem,
                                        device_id=to, device_id_type=MESH)


CAST_STEPS = 4


def _to_bf16_in_whole(ws, specs, shard):
    n = len(ws)

    def body(s_ref, *refs):
        del s_ref
        for w_ref, o_ref in zip(refs[:n], refs[n:]):
            o_ref[...] = w_ref[...].astype(BF16)

    def out_spec(spec):
        rows = spec[0] // CAST_STEPS
        if spec[2] == "col":
            return pl.BlockSpec((rows, spec[1]), lambda i, s_ref: (i, s_ref[0]))
        return pl.BlockSpec((rows, spec[1]), lambda i, s_ref: (s_ref[0] * CAST_STEPS + i, 0))

    grid_spec = pltpu.PrefetchScalarGridSpec(
        num_scalar_prefetch=1, grid=(CAST_STEPS,),
        in_specs=[pl.BlockSpec((s[0] // CAST_STEPS, s[1]), lambda i, s_ref: (i, 0)) for s in specs],
        out_specs=[out_spec(s) for s in specs])
    return list(pl.pallas_call(
        body, name="cast_shards", grid_spec=grid_spec, out_shape=[_sds(_whole_shape(s), BF16) for s in specs],
        compiler_params=_params(("parallel",)),
    )(shard, *ws))


class _Gather:
    def __init__(self, wholes, pieces, conv_w=None):
        self.pieces = pieces
        self.n = len(wholes)
        self.with_conv_w = conv_w is not None
        self.operands = list(wholes) + ([conv_w] if self.with_conv_w else [])
        self.out_shape = [_sds(w.shape, w.dtype) for w in wholes]
        if self.with_conv_w:
            self.out_shape.append(_sds((3, D_MODEL), F32))
        self.aliases = {i: i for i in range(self.n)}
        n_ici = 3 * len(pieces)
        self.sems = [pltpu.SemaphoreType.DMA((n_ici,))] * 4
        if self.with_conv_w:
            self.sems += [pltpu.SemaphoreType.DMA((1,)), pltpu.SemaphoreType.DMA((3,)), pltpu.SemaphoreType.DMA((3,))]

    def _conv_w(self, cins, couts, sems, with_recvs):
        cw_in, cw_out = cins[self.n], couts[self.n]
        x, y, c, chips = _position()

        def cols(shard):
            return cw_out.at[:, pl.ds(_aligned(shard * CONV_W_COLS, LANES), CONV_W_COLS)]

        me = _shard_of((x, y))
        local = pltpu.make_async_copy(cw_in, cols(me), sems[4].at[0])
        sends = [_remote(cw_in, cols(me), sems[5].at[j], sems[6].at[j], (*chip, c)) for j, chip in enumerate(chips)]
        if not with_recvs:
            return local, sends, []
        recvs = [_remote(cols(_shard_of(chip)), cols(_shard_of(chip)), sems[5].at[j], sems[6].at[j], (*chip, c))
                 for j, chip in enumerate(chips)]
        return local, sends, recvs

    def start(self, cins, couts, sems):
        x, y, c, chips = _position()
        me = _shard_of((x, y))
        if self.with_conv_w:
            local, sends, _ = self._conv_w(cins, couts, sems, False)
            local.start()
            for cp in sends:
                cp.start()
        for p, (i, spec, part, parts) in enumerate(self.pieces):
            mine = _region(couts[i], spec, me, c, part, parts)
            for j, chip in enumerate(chips):
                _remote(mine, mine, sems[0].at[3 * p + j], sems[1].at[3 * p + j], (*chip, c)).start()

    def finish(self, cins, couts, sems):
        x, y, c, chips = _position()
        me = _shard_of((x, y))
        sibling = (x, y, 1 - c)
        send_a, recv_a, send_b, recv_b = sems[:4]
        passed = []
        for p, (i, spec, part, parts) in enumerate(self.pieces):
            for j, chip in enumerate(chips):
                k = 3 * p + j
                landed = _region(couts[i], spec, _shard_of(chip), c, part, parts)
                _remote(landed, landed, send_a.at[k], recv_a.at[k], (*chip, c)).wait_recv()
                cp = _remote(landed, landed, send_b.at[k], recv_b.at[k], sibling)
                cp.start()
                passed.append(cp)
        for p, (i, spec, part, parts) in enumerate(self.pieces):
            mine = _region(couts[i], spec, me, c, part, parts)
            for j, chip in enumerate(chips):
                k = 3 * p + j
                other = _region(couts[i], spec, _shard_of(chip), 1 - c, part, parts)
                _remote(other, other, send_b.at[k], recv_b.at[k], sibling).wait_recv()
                _remote(mine, mine, send_a.at[k], recv_a.at[k], (*chip, c)).wait_send()
        for cp in passed:
            cp.wait_send()
        if self.with_conv_w:
            local, sends, recvs = self._conv_w(cins, couts, sems, True)
            for cp in recvs:
                cp.wait_recv()
            for cp in sends:
                cp.wait_send()
            local.wait()


def _mm_in_gather(h1, w_whole, comm):
    spec = MATRICES["w_in"]
    cols = spec[1]
    bm = SEQ // 2

    def body(h_ref, w_in_ref, proj_ref, w_ref, wbuf, obuf, send_a, recv_a, send_b, recv_b, load_sem, store_sems):
        del w_in_ref
        s, mi = pl.program_id(0), pl.program_id(1)
        x, y, c, chips = _position()
        me = _shard_of((x, y))
        sibling = (x, y, 1 - c)
        mine = _region(w_ref, spec, me, c)

        @pl.when((s == 0) & (mi == 0))
        def _():
            for j, chip in enumerate(chips):
                _remote(mine, mine, send_a.at[j], recv_a.at[j], (*chip, c)).start()

        shard = me
        for j, chip in enumerate(chips):
            shard = jnp.where(s == j + 1, _shard_of(chip), shard)

            @pl.when((s == j + 1) & (mi == 0))
            def _():
                landed = _region(w_ref, spec, _shard_of(chip), c)
                _remote(landed, landed, send_a.at[j], recv_a.at[j], (*chip, c)).wait_recv()
                _remote(landed, landed, send_b.at[j], recv_b.at[j], sibling).start()
                other = _region(w_ref, spec, _shard_of(chip), 1 - c)
                _remote(other, other, send_b.at[j], recv_b.at[j], sibling).wait_recv()

        col0 = pl.multiple_of(shard * cols, LANES)

        @pl.when(mi == 0)
        def _():
            load = pltpu.make_async_copy(w_ref.at[:, pl.ds(col0, cols)], wbuf, load_sem.at[0])
            load.start()
            load.wait()

        def store():
            rows = pl.ds(pl.multiple_of(mi * bm, bm), bm)
            return pltpu.make_async_copy(obuf.at[mi], proj_ref.at[rows, pl.ds(col0, cols)], store_sems.at[mi])

        @pl.when(s > 0)
        def _():
            store().wait()

        obuf[mi] = jnp.dot(h_ref[...], wbuf[...], preferred_element_type=F32)
        store().start()

        @pl.when(s == N_CHIPS - 1)
        def _():
            store().wait()

        @pl.when((s == N_CHIPS - 1) & (mi == 1))
        def _():
            for j, chip in enumerate(chips):
                landed = _region(w_ref, spec, _shard_of(chip), c)
                _remote(mine, mine, send_a.at[j], recv_a.at[j], (*chip, c)).wait_send()
                _remote(landed, landed, send_b.at[j], recv_b.at[j], sibling).wait_send()

    sem3 = pltpu.SemaphoreType.DMA((3,))
    (proj, whole), extra = _pcall(
        body, "mm_in", (N_CHIPS, SEQ // bm),
        [pl.BlockSpec((bm, D_MODEL), lambda s, m: (m, 0)), HBM_SPEC], [HBM_SPEC, HBM_SPEC],
        [_sds((SEQ, N_IN), F32), _sds(w_whole.shape, w_whole.dtype)], [h1, w_whole],
        [pltpu.VMEM((D_MODEL, cols), BF16), pltpu.VMEM((SEQ // bm, bm, cols), F32), sem3, sem3, sem3, sem3,
         pltpu.SemaphoreType.DMA((1,)), pltpu.SemaphoreType.DMA((SEQ // bm,))],
        None, comm, aliases={1: 1}, start_after_body=True)
    return proj, whole, extra


def _mm_dw_in_pair(h1, dproj, comm):
    spec = MATRICES["w_in"]
    rows, cols, _ = spec
    hr = rows // 2

    def body(h_ref, dp_ref, dw_ref, got_ref, obuf, store_sems, send_sems, recv_sems):
        t = pl.program_id(0)
        x, y, c, _ = _position()
        sibling = (x, y, 1 - c)

        def store(step):
            return pltpu.make_async_copy(obuf.at[step % 2], dw_ref.at[:, pl.ds(step * cols, cols)],
                                         store_sems.at[step % 2])

        def send(step):
            theirs = obuf.at[step % 2, pl.ds(_aligned((1 - c) * hr, BF16_ROW_TILE), hr), :]
            return _remote(theirs, got_ref.at[step], send_sems.at[step], recv_sems.at[step], sibling)

        for step in range(N_CHIPS):
            @pl.when(t == step)
            def _():
                if step >= 2:
                    store(step - 2).wait()
                    send(step - 2).wait_send()
                obuf[step % 2] = lax.dot_general(
                    h_ref[...], dp_ref[...], (TN, ((), ())), preferred_element_type=F32).astype(BF16)
                store(step).start()
                send(step).start()

        @pl.when(t == N_CHIPS - 1)
        def _():
            for step in (N_CHIPS - 2, N_CHIPS - 1):
                store(step).wait()
                send(step).wait_send()
            for step in range(N_CHIPS):
                send(step).wait_recv()

    sem4 = pltpu.SemaphoreType.DMA((N_CHIPS,))
    (dw_in, got), extra = _pcall(
        body, "mm_dw_in", (N_CHIPS,),
        [_full_spec((SEQ, D_MODEL)), pl.BlockSpec((SEQ, cols), lambda t: (0, t))], [HBM_SPEC, HBM_SPEC],
        [_sds(_whole_shape(spec), BF16), _sds((N_CHIPS, hr, cols), BF16)], [h1, dproj],
        [pltpu.VMEM((2, rows, cols), BF16), pltpu.SemaphoreType.DMA((2,)), sem4, sem4], None, comm)
    return dw_in, got, extra


def _pack_small(dg_mix, dg_ffn, dg_final, dconv_w, dsinks, loss_row):
    def body(a_ref, b_ref, c_ref, w_ref, s_ref, l_ref, o_ref):
        pad = jnp.zeros((1, D_MODEL - LANES), F32)
        o_ref[0:1, :] = a_ref[...]
        o_ref[1:2, :] = b_ref[...]
        o_ref[2:3, :] = c_ref[...]
        o_ref[3:6, :] = w_ref[...]
        o_ref[6:7, :] = jnp.concatenate([s_ref[...], pad], axis=1)
        o_ref[7:8, :] = jnp.concatenate([l_ref[...], pad], axis=1)

    return pl.pallas_call(
        body, name="pack_small", out_shape=_sds((SMALL_ROWS, D_MODEL), F32),
        compiler_params=_params(),
    )(dg_mix, dg_ffn, dg_final, dconv_w, dsinks, loss_row)


class _Pair:
    def __init__(self, dws, specs):
        self.specs = specs
        self.operands = list(dws)
        self.out_shape = [_sds((N_CHIPS, *_half_shape(s)), BF16) for s in specs]
        self.aliases = {}
        n = N_CHIPS * len(specs)
        self.sems = [pltpu.SemaphoreType.DMA((n,)), pltpu.SemaphoreType.DMA((n,))]

    def _copies(self, cins, couts, sems):
        x, y, c, _ = _position()
        sibling = (x, y, 1 - c)
        for i, spec in enumerate(self.specs):
            for t in range(N_CHIPS):
                k = N_CHIPS * i + t
                yield _remote(_region(cins[i], spec, t, 1 - c), couts[i].at[t], sems[0].at[k], sems[1].at[k], sibling)

    def start(self, cins, couts, sems):
        for cp in self._copies(cins, couts, sems):
            cp.start()

    def finish(self, cins, couts, sems):
        for cp in self._copies(cins, couts, sems):
            cp.wait()


class _SmallAllToAll:
    def __init__(self, small):
        self.operands = [small]
        self.out_shape = [_sds((N_DEV, SMALL_ROWS, D_MODEL), F32)]
        self.aliases = {}
        self.sems = [pltpu.SemaphoreType.DMA((N_DEV - 1,)), pltpu.SemaphoreType.DMA((N_DEV - 1,)),
                     pltpu.SemaphoreType.DMA((1,))]

    def _copies(self, cins, couts, sems):
        x, y, c, _ = _position()
        me = 4 * x + 2 * y + c
        out = []
        for r in range(1, N_DEV):
            flip = ((r >> 2) & 1, (r >> 1) & 1, r & 1)
            peer = tuple(1 - p if f else p for p, f in zip((x, y, c), flip))
            theirs = couts[0].at[4 * peer[0] + 2 * peer[1] + peer[2]]
            out.append((_remote(cins[0], couts[0].at[me], sems[0].at[r - 1], sems[1].at[r - 1], peer),
                        functools.partial(_remote, theirs, theirs, sems[0].at[r - 1], sems[1].at[r - 1], peer)))
        return pltpu.make_async_copy(cins[0], couts[0].at[me], sems[2].at[0]), out

    def start(self, cins, couts, sems):
        own, copies = self._copies(cins, couts, sems)
        own.start()
        for send, _ in copies:
            send.start()

    def finish(self, cins, couts, sems):
        own, copies = self._copies(cins, couts, sems)
        for send, recv in copies:
            recv().wait_recv()
            send.wait_send()
        own.wait()


class _Both:
    def __init__(self, a, b):
        self.a, self.b = a, b
        self.operands = list(a.operands) + list(b.operands)
        self.out_shape = list(a.out_shape) + list(b.out_shape)
        self.aliases = dict(a.aliases)
        self.aliases.update({len(a.operands) + k: len(a.out_shape) + v for k, v in b.aliases.items()})
        self.sems = list(a.sems) + list(b.sems)

    def _split(self, cins, couts, sems):
        na, ma, sa = len(self.a.operands), len(self.a.out_shape), len(self.a.sems)
        return (cins[:na], couts[:ma], sems[:sa]), (cins[na:], couts[ma:], sems[sa:])

    def start(self, cins, couts, sems):
        for plan, args in zip((self.a, self.b), self._split(cins, couts, sems)):
            plan.start(*args)

    def finish(self, cins, couts, sems):
        for plan, args in zip((self.a, self.b), self._split(cins, couts, sems)):
            plan.finish(*args)


def _pair_sum(name, specs, dws, got, place):
    n_mat = len(specs)

    def body(p_ref, *refs):
        t = pl.program_id(0)
        mine, theirs = refs[:n_mat], refs[n_mat:2 * n_mat]
        outs, owns = refs[2 * n_mat:3 * n_mat], refs[3 * n_mat:]
        for a, b, o, own in zip(mine, theirs, outs, owns):
            s = (a[...].astype(F32) + b[...].astype(F32)).astype(BF16)
            o[...] = s

            @pl.when(t == p_ref[1])
            def _():
                own[...] = s

    def mine_spec(spec):
        hr, cols = _half_shape(spec)
        if spec[2] == "col":
            return pl.BlockSpec((hr, cols), lambda t, p_ref: (p_ref[0], t))
        return pl.BlockSpec((hr, cols), lambda t, p_ref: (2 * t + p_ref[0], 0))

    def slot_spec(spec):
        return pl.BlockSpec((None, *_half_shape(spec)), lambda t, p_ref: (t, 0, 0))

    def own_spec(spec):
        return pl.BlockSpec((None, *_half_shape(spec)), lambda t, p_ref: (p_ref[1], 0, 0))

    slots = [_sds((N_CHIPS, *_half_shape(s)), BF16) for s in specs]
    grid_spec = pltpu.PrefetchScalarGridSpec(
        num_scalar_prefetch=1, grid=(N_CHIPS,),
        in_specs=[mine_spec(s) for s in specs] + [slot_spec(s) for s in specs],
        out_specs=[slot_spec(s) for s in specs] + [own_spec(s) for s in specs])
    res = pl.pallas_call(
        body, name=name, grid_spec=grid_spec, out_shape=slots + slots,
        compiler_params=_params(("arbitrary",)),
    )(place, *dws, *got)
    return list(res[:n_mat]), list(res[n_mat:])


class _ChipExchange:
    def __init__(self, sums, slots, which=(0,), parts=1):
        self.n = len(sums)
        self.which, self.parts = tuple(which), parts
        self.operands = list(sums) + list(slots)
        self.out_shape = [_sds(s.shape, s.dtype) for s in slots]
        self.aliases = {self.n + i: i for i in range(self.n)}
        n_copies = 3 * self.n * len(self.which)
        self.sems = [pltpu.SemaphoreType.DMA((n_copies,)), pltpu.SemaphoreType.DMA((n_copies,))]

    def _rows(self, ref, slot, part):
        n = ref.shape[1] // self.parts
        return ref.at[slot, pl.ds(part * n, n), :]

    def _copies(self, cins, couts, sems):
        x, y, c, chips = _position()
        me = _shard_of((x, y))
        k = 0
        for i in range(self.n):
            for chip in chips:
                for part in self.which:
                    theirs = self._rows(couts[i], _shard_of(chip), part)
                    yield (_remote(self._rows(cins[i], _shard_of(chip), part), self._rows(couts[i], me, part),
                                   sems[0].at[k], sems[1].at[k], (*chip, c)),
                           functools.partial(_remote, theirs, theirs, sems[0].at[k], sems[1].at[k], (*chip, c)))
                    k += 1

    def start(self, cins, couts, sems):
        for send, _ in self._copies(cins, couts, sems):
            send.start()

    def finish(self, cins, couts, sems):
        for send, recv in self._copies(cins, couts, sems):
            recv().wait_recv()
            send.wait_send()


def _chip_sum(name, specs, slots, core):
    steps = 2
    n_mat = len(specs)

    def body(c_ref, *refs):
        del c_ref
        ins, outs = refs[:n_mat], refs[n_mat:]
        for a, o in zip(ins, outs):
            acc = a[0].astype(F32)
            for t in range(1, N_CHIPS):
                acc = acc + a[t].astype(F32)
            o[...] = acc

    def in_spec(spec):
        hr, cols = _half_shape(spec)
        return pl.BlockSpec((N_CHIPS, hr // steps, cols), lambda i, c_ref: (0, i, 0))

    def out_spec(spec):
        hr, cols = _half_shape(spec)
        return pl.BlockSpec((hr // steps, cols), lambda i, c_ref: (c_ref[0] * steps + i, 0))

    grid_spec = pltpu.PrefetchScalarGridSpec(
        num_scalar_prefetch=1, grid=(steps,),
        in_specs=[in_spec(s) for s in specs], out_specs=[out_spec(s) for s in specs])
    return list(pl.pallas_call(
        body, name=name, grid_spec=grid_spec,
        out_shape=[_sds((s[0], s[1]), F32) for s in specs],
        compiler_params=_params(("parallel",)),
    )(core, *slots))


class _HalfExchange:
    def __init__(self, grads, specs):
        self.specs = specs
        self.operands = list(grads)
        self.out_shape = [_sds(g.shape, g.dtype) for g in grads]
        self.aliases = {i: i for i in range(len(grads))}
        self.sems = [pltpu.SemaphoreType.DMA((len(grads),)), pltpu.SemaphoreType.DMA((len(grads),))]

    def _copies(self, couts, sems):
        x, y, c, _ = _position()
        sibling = (x, y, 1 - c)
        for i, spec in enumerate(self.specs):
            hr = spec[0] // 2
            mine = couts[i].at[pl.ds(_aligned(c * hr, 8), hr), :]
            theirs = couts[i].at[pl.ds(_aligned((1 - c) * hr, 8), hr), :]
            yield (_remote(mine, mine, sems[0].at[i], sems[1].at[i], sibling),
                   functools.partial(_remote, theirs, theirs, sems[0].at[i], sems[1].at[i], sibling))

    def start(self, cins, couts, sems):
        for send, _ in self._copies(couts, sems):
            send.start()

    def finish(self, cins, couts, sems):
        for send, recv in self._copies(couts, sems):
            recv().wait_recv()
            send.wait_send()


def _small_sum(blocks):
    def body(b_ref, o_ref):
        acc = b_ref[0]
        for d in range(1, N_DEV):
            acc = acc + b_ref[d]
        o_ref[...] = acc

    return pl.pallas_call(
        body, name="small_sum", out_shape=_sds((SMALL_ROWS, D_MODEL), F32), compiler_params=_params(),
    )(blocks)


def _adamw(name, params, steps):
    n = len(params)

    def body(*refs):
        for p in range(n):
            w_ref, g_ref, m_ref, v_ref = refs[4 * p:4 * p + 4]
            d_ref, nm_ref, nv_ref, go_ref = refs[4 * n + 4 * p:4 * n + 4 * p + 4]
            g = g_ref[...]
            go_ref[...] = g
            m = ADAM_B1 * m_ref[...] + (1.0 - ADAM_B1) * g
            v = ADAM_B2 * v_ref[...] + (1.0 - ADAM_B2) * jnp.square(g)
            m_hat = m / (1.0 - ADAM_B1 ** ADAM_STEP)
            v_hat = v / (1.0 - ADAM_B2 ** ADAM_STEP)
            d_ref[...] = -ADAM_LR * (m_hat / (jnp.sqrt(v_hat) + ADAM_EPS) + ADAM_WD * w_ref[...])
            nm_ref[...] = m
            nv_ref[...] = v

    in_specs, out_specs, out_shape, operands = [], [], [], []
    for w, g, m, v in params:
        spec = pl.BlockSpec((w.shape[0] // steps, w.shape[1]), lambda i: (i, 0))
        in_specs += [spec] * 4
        out_specs += [spec] * 4
        out_shape += [_sds(w.shape, F32)] * 4
        operands += [w, g, m, v]
    outs = _pcall(body, name, (steps,), in_specs, out_specs, out_shape, operands, (), ("parallel",))
    return [tuple(outs[4 * p:4 * p + 4]) for p in range(n)]


MATRIX_NAMES = tuple(MATRICES)
WEIGHT_ORDER = ("g_mix", "w_in", "conv_w", "attn_sinks", "w_conv_out", "w_attn_out", "w_o", "g_ffn",
                "w_gate_up", "w_down", "g_final")


def kernel(x, g_mix, w_in, conv_w, attn_sinks, w_conv_out, w_attn_out, w_o, g_ffn, w_gate_up, w_down, g_final, loss_target, m_g_mix, m_w_in, m_conv_w, m_attn_sinks, m_w_conv_out, m_w_attn_out, m_w_o, m_g_ffn, m_w_gate_up, m_w_down, m_g_final, v_g_mix, v_w_in, v_conv_w, v_attn_sinks, v_w_conv_out, v_w_attn_out, v_w_o, v_g_ffn, v_w_gate_up, v_w_down, v_g_final):
    w = dict(g_mix=g_mix, w_in=w_in[0], conv_w=conv_w[0], attn_sinks=attn_sinks, w_conv_out=w_conv_out[0],
             w_attn_out=w_attn_out[0], w_o=w_o[0], g_ffn=g_ffn, w_gate_up=w_gate_up[0], w_down=w_down[0],
             g_final=g_final[None, :])
    m = dict(g_mix=m_g_mix, w_in=m_w_in[0], conv_w=m_conv_w[0], attn_sinks=m_attn_sinks,
             w_conv_out=m_w_conv_out[0], w_attn_out=m_w_attn_out[0], w_o=m_w_o[0], g_ffn=m_g_ffn,
             w_gate_up=m_w_gate_up[0], w_down=m_w_down[0], g_final=m_g_final[None, :])
    v = dict(g_mix=v_g_mix, w_in=v_w_in[0], conv_w=v_conv_w[0], attn_sinks=v_attn_sinks,
             w_conv_out=v_w_conv_out[0], w_attn_out=v_w_attn_out[0], w_o=v_w_o[0], g_ffn=v_g_ffn,
             w_gate_up=v_w_gate_up[0], w_down=v_w_down[0], g_final=v_g_final[None, :])
    shard = (2 * lax.axis_index("x") + lax.axis_index("y")).astype(jnp.int32)
    core = lax.axis_index("c").astype(jnp.int32)
    shard1, core1, place = shard.reshape((1,)), core.reshape((1,)), jnp.stack([core, shard])
    spec = MATRICES
    xs, target, sinks = x[0], loss_target[0], w["attn_sinks"]
    tables = _rope_tables()

    def gather(names, which=(0,), parts=1):
        return _Gather([whole[n] for n in names],
                       [(i, spec[n], part, parts) for i, n in enumerate(names) for part in which])

    def pair(names):
        return _Pair([dw[n] for n in names], [spec[n] for n in names])

    def pair_sum(tag, names, got):
        return _pair_sum("pair_sum_" + tag, [spec[n] for n in names], [dw[n] for n in names], got, place)

    whole = dict(zip(MATRIX_NAMES, _to_bf16_in_whole(
        [w[n] for n in MATRIX_NAMES], [spec[n] for n in MATRIX_NAMES], shard1)))

    mixers = ("w_conv_out", "w_attn_out", "w_o")
    h1 = _rms_norm("norm_mix", xs, w["g_mix"])
    proj, whole["w_in"], (*got, conv_w_whole) = _mm_in_gather(
        h1, whole["w_in"], _Gather([whole[n] for n in mixers], [(i, spec[n], 0, 1) for i, n in enumerate(mixers)],
                                   conv_w=w["conv_w"]))
    whole.update(zip(mixers, got))
    conv_y = _conv_fwd(proj, conv_w_whole)
    attn, (whole["w_gate_up"],) = _attn_fwd(proj, tables, sinks, comm=gather(("w_gate_up",), (0, 1, 2), 4))
    (conv_out, attn_out, merged), (whole["w_gate_up"],) = _branch_merge(
        conv_y, attn, whole["w_conv_out"], whole["w_attn_out"], proj, comm=gather(("w_gate_up",), (3,), 4))
    x2, h2 = _mm_o_norm(merged, whole["w_o"], xs, w["g_ffn"])
    (gate, up, act), (whole["w_down"],) = _gate_up_fwd(h2, whole["w_gate_up"], comm=gather(("w_down",)))
    dx3, dx3b, dg_final, loss_row = _mm_down_loss(act, whole["w_down"], x2, w["g_final"], target)

    dw = {}
    dw["w_down"] = _mm_tn("mm_dw_down", act, dx3b, 1408, 1024, BF16)
    dgate, dup = _dact_swiglu(dx3b, whole["w_down"], gate, up)
    dw["w_gate_up"], got = _mm_dw_gate_up(h2, dgate, dup, comm=pair(("w_down",)))
    sums_a, own_a = pair_sum("down", ("w_down",), got)
    dh2, slots_a = _mm_dh2(dgate, dup, whole["w_gate_up"], comm=_ChipExchange(sums_a, own_a))
    (dx2, dx2b, dg_ffn), got_b = _rms_norm_bwd("norm_ffn_bwd", dh2, x2, w["g_ffn"], dx3, True,
                                               comm=pair(("w_gate_up",)))
    sums_b, own_b = pair_sum("gate_up", ("w_gate_up",), got_b)
    dw["w_o"] = _mm_tn("mm_dw_o", merged, dx2b, 1024, 1024, BF16)
    dco, dao, dgc, dga = _merge_bwd(dx2b, whole["w_o"], conv_out, attn_out, proj)
    dw["w_conv_out"] = _mm_tn("mm_dw_conv_out", conv_y, dco, 1024, 1024, BF16)
    dw["w_attn_out"] = _mm_tn("mm_dw_attn_out", attn, dao, 1024, 1024, BF16)
    (dcb, dcc, dcx, dconv_w), (*got_c, own_b) = _conv_bwd(
        dco, whole["w_conv_out"], proj, conv_w_whole,
        comm=_Both(pair(mixers), _ChipExchange(sums_b, own_b, (0,), 4)))
    sums_c, own_c = pair_sum("mixers", mixers, got_c)
    (dq, dk_prev, dk_cur, dv_prev, dv_cur, dsinks), slots_b = _attn_bwd(
        proj, dao, whole["w_attn_out"], sinks, tables, comm=_ChipExchange(sums_b, [own_b], (1, 2, 3), 4))
    dkv = _kv_grad_combine(dk_prev, dk_cur, dv_prev, dv_cur, tables)
    dproj = jnp.concatenate([dcb, dcc, dcx, dq, dkv, dgc, dga], axis=1)
    dw["w_in"], got, slots_c = _mm_dw_in_pair(h1, dproj, _ChipExchange(sums_c, own_c))
    sums_d, own_d = pair_sum("in", ("w_in",), [got])
    early = ("w_down", "w_gate_up") + mixers
    halves = _chip_sum("chip_sum_early", [spec[n] for n in early], slots_a + slots_b + slots_c, core1)
    dh1, (own_d, *reduced) = _mm_nt(
        "mm_dh1", dproj, whole["w_in"], 1024, 1024, 1664, F32,
        comm=_Both(_ChipExchange(sums_d, own_d, (0,), 2), _HalfExchange(halves, [spec[n] for n in early])))
    g = dict(zip(early, reduced))
    (grad_x, dg_mix), slots_d = _rms_norm_bwd("norm_mix_bwd", dh1, xs, w["g_mix"], dx2, False,
                                              comm=_ChipExchange(sums_d, [own_d], (1,), 2))
    small = _pack_small(dg_mix, dg_ffn, dg_final, dconv_w, dsinks, loss_row)
    half_in = _chip_sum("chip_sum_in", [spec["w_in"]], slots_d, core1)
    g["w_in"], small_blocks = _comm_call(
        "half_exchange_in", _Both(_HalfExchange(half_in, [spec["w_in"]]), _SmallAllToAll(small)))
    delta, new_m, new_v = {}, {}, {}

    def keep(names, results):
        for n, (d, nm, nv, grad) in zip(names, results):
            delta[n], new_m[n], new_v[n], g[n] = d, nm, nv, grad

    keep(early, _adamw("adamw_early", [(w[n], g[n], m[n], v[n]) for n in early], 8))
    small_sum = _small_sum(small_blocks)
    g["g_mix"] = small_sum[0:1, :]
    g["g_ffn"] = small_sum[1:2, :]
    g["g_final"] = small_sum[2:3, :]
    g["conv_w"] = lax.dynamic_slice(small_sum, (3, shard * CONV_W_COLS), (3, CONV_W_COLS))
    g["attn_sinks"] = small_sum[6:7, :N_HEADS]
    loss = small_sum[7, 0]
    keep(("w_in",), _adamw("adamw_w_in", [(w["w_in"], g["w_in"], m["w_in"], v["w_in"])], 4))
    rest = ("g_mix", "g_ffn", "g_final", "conv_w", "attn_sinks")
    keep(rest, _adamw("adamw_small", [(w[n], g[n], m[n], v[n]) for n in rest], 1))

    def shaped(vals):
        return [vals[n].reshape((D_MODEL,)) if n == "g_final" else
                (vals[n][None] if n in MATRIX_NAMES or n == "conv_w" else vals[n]) for n in WEIGHT_ORDER]

    return (loss, grad_x[None], *shaped(g), *shaped(delta), *shaped(new_m), *shaped(new_v))
```
